```python
import jax
import jax.numpy as jnp
from jax import lax
import numpy as np

D_MODEL = 1024
BATCH = 8
SEQ = 8192
DEPTH = 2

CHUNK = 64
D_FF = 2816
D_CONV = 512
CONV_WIDTH = 31
GLA_HEADS = 4
GLA_DK = 64
GLA_DV = 128
D_GLA = GLA_HEADS * GLA_DV
D_MIX = D_CONV + D_GLA
GATE_RANK = 16
GATE_TAU = 16.0
N_MOD = 9
EPS = 1e-6
SPLITS = (D_CONV, 2 * D_CONV, 2 * D_CONV + GLA_HEADS * GLA_DK, 2 * D_CONV + 2 * GLA_HEADS * GLA_DK, 2 * D_CONV + 2 * GLA_HEADS * GLA_DK + D_GLA, 2 * D_CONV + 2 * GLA_HEADS * GLA_DK + 2 * D_GLA)
D_IN = SPLITS[-1] + GATE_RANK

kernel_name = 'hybrid_conformer_gla_macaron'


def rmsnorm(x, g):
    xf = x.astype(jnp.float32)
    y = xf * lax.rsqrt(jnp.mean(xf * xf, axis=-1, keepdims=True) + EPS)
    return (y * g.astype(jnp.float32)).astype(x.dtype)


def layernorm(x, g, b):
    xf = x.astype(jnp.float32)
    xc = xf - jnp.mean(xf, axis=-1, keepdims=True)
    y = xc * lax.rsqrt(jnp.mean(xc * xc, axis=-1, keepdims=True) + EPS)
    return (y * g.astype(jnp.float32) + b.astype(jnp.float32)).astype(x.dtype)


def modulate(h, shift, scale):
    return h * (1 + scale) + shift


def swiglu(h, w_in, w_out):
    gate, up = jnp.split(h @ w_in, 2, axis=-1)
    return (jax.nn.silu(gate) * up) @ w_out


def conformer_conv(a, b, w_dw, b_dw, g_ln, b_ln):
    u = a * jax.nn.sigmoid(b)
    u = jnp.pad(u, ((0, 0), (CONV_WIDTH - 1, 0), (0, 0)))
    y = lax.conv_general_dilated(u, w_dw.astype(u.dtype)[:, None, :], window_strides=(1,), padding='VALID', dimension_numbers=('NWC', 'WIO', 'NWC'), feature_group_count=D_CONV)
    return jax.nn.silu(layernorm(y + b_dw, g_ln, b_ln))


def gla(q, k, v, r, glr, w_gate_up, b_gate, g_norm):
    bsz, seq, _ = q.shape
    n = seq // CHUNK
    f32 = jnp.float32

    def heads(t, d):
        return t.astype(f32).reshape(bsz, n, CHUNK, GLA_HEADS, d).transpose(0, 3, 1, 2, 4)

    log_a = jax.nn.log_sigmoid((glr @ w_gate_up + b_gate).astype(f32)) / GATE_TAU
    qh = heads(q, GLA_DK) * (GLA_DK ** -0.5)
    kh = heads(k, GLA_DK)
    vh = heads(v, GLA_DV)
    bc = jnp.cumsum(heads(log_a, GLA_DK), axis=3)
    b_end = bc[:, :, :, -1:, :]
    q_fwd = qh * jnp.exp(bc)
    att_fwd = jnp.einsum('bhnik,bhnjk->bhnij', q_fwd, kh * jnp.exp(-bc))
    att_bwd = jnp.einsum('bhnik,bhnjk->bhnij', qh * jnp.exp(-bc), kh * jnp.exp(bc))
    tri = jnp.tril(jnp.ones((CHUNK, CHUNK), dtype=bool))
    o = jnp.einsum('bhnij,bhnjv->bhniv', jnp.where(tri, att_fwd, att_bwd), vh)
    u = jnp.einsum('bhnjk,bhnjv->bhnkv', kh * jnp.exp(b_end - bc), vh)
    g = jnp.exp(b_end[:, :, :, 0, :])

    def step(state, inp):
        g_c, u_c = inp
        return g_c[..., None] * state + u_c, state

    s0 = jnp.zeros((bsz, GLA_HEADS, GLA_DK, GLA_DV), f32)
    _, s_prev = lax.scan(step, s0, (jnp.moveaxis(g, 2, 0), jnp.moveaxis(u, 2, 0)))
    o = o + jnp.einsum('bhnik,nbhkv->bhniv', q_fwd, s_prev)
    o = o * lax.rsqrt(jnp.mean(o * o, axis=-1, keepdims=True) + EPS) * g_norm.astype(f32)[:, None, None, :]
    o = o.transpose(0, 2, 3, 1, 4).reshape(bsz, seq, D_GLA)
    return (o * jax.nn.silu(r.astype(f32))).astype(q.dtype)


def _fwd_setup_inputs(seed: int = 0) -> dict:
    key = jax.random.key(seed)
    ks = jax.random.split(key, 26)
    L = DEPTH

    def nrm(k, shape, scale):
        return jax.random.normal(k, shape, jnp.float32) * scale

    return {
        'x': nrm(ks[0], (BATCH, SEQ, D_MODEL), 1.0),
        'c': nrm(ks[1], (BATCH, D_MODEL), 1.0),
        'w_ada': nrm(ks[2], (L, D_MODEL, N_MOD * D_MODEL), 0.5 * D_MODEL ** -0.5),
        'b_ada': nrm(ks[3], (L, N_MOD * D_MODEL), 0.02),
        'g_norm_ffn1': 1.0 + nrm(ks[4], (L, D_MODEL), 0.02),
        'w_ffn1_in': nrm(ks[5], (L, D_MODEL, 2 * D_FF), D_MODEL ** -0.5),
        'w_ffn1_out': nrm(ks[6], (L, D_FF, D_MODEL), D_FF ** -0.5),
        'g_norm_mix': 1.0 + nrm(ks[7], (L, D_MODEL), 0.02),
        'w_in': nrm(ks[8], (L, D_MODEL, D_IN), D_MODEL ** -0.5),
        'w_dw': nrm(ks[9], (L, CONV_WIDTH, D_CONV), CONV_WIDTH ** -0.5),
        'b_dw': nrm(ks[10], (L, D_CONV), 0.02),
        'g_conv_ln': 1.0 + nrm(ks[11], (L, D_CONV), 0.02),
        'b_conv_ln': nrm(ks[12], (L, D_CONV), 0.02),
        'w_gate_up': nrm(ks[13], (L, GATE_RANK, GLA_HEADS * GLA_DK), GATE_RANK ** -0.5),
        'b_gate': nrm(ks[14], (L, GLA_HEADS * GLA_DK), 0.02),
        'g_gla_norm': 1.0 + nrm(ks[15], (L, GLA_HEADS, GLA_DV), 0.02),
        'w_out': nrm(ks[16], (L, D_MIX, D_MODEL), D_MIX ** -0.5),
        'g_norm_ffn2': 1.0 + nrm(ks[17], (L, D_MODEL), 0.02),
        'w_ffn2_in': nrm(ks[18], (L, D_MODEL, 2 * D_FF), D_MODEL ** -0.5),
        'w_ffn2_out': nrm(ks[19], (L, D_FF, D_MODEL), D_FF ** -0.5),
        'g_norm_final': 1.0 + nrm(ks[20], (D_MODEL,), 0.02),
        'w_ada_final': nrm(ks[21], (D_MODEL, 2 * D_MODEL), 0.5 * D_MODEL ** -0.5),
        'b_ada_final': nrm(ks[22], (2 * D_MODEL,), 0.02),
    }


def _fwd_reference(x, c, w_ada, b_ada, g_norm_ffn1, w_ffn1_in, w_ffn1_out, g_norm_mix, w_in, w_dw, b_dw, g_conv_ln, b_conv_ln, w_gate_up, b_gate, g_gla_norm, w_out, g_norm_ffn2, w_ffn2_in, w_ffn2_out, g_norm_final, w_ada_final, b_ada_final):
    bsz = x.shape[0]
    c_act = jax.nn.silu(c)
    for l in range(DEPTH):
        mod = (c_act @ w_ada[l] + b_ada[l]).reshape(bsz, N_MOD, 1, D_MODEL)
        h = modulate(rmsnorm(x, g_norm_ffn1[l]), mod[:, 0], mod[:, 1])
        x = x + 0.5 * mod[:, 2] * swiglu(h, w_ffn1_in[l], w_ffn1_out[l])
        h = modulate(rmsnorm(x, g_norm_mix[l]), mod[:, 3], mod[:, 4])
        a, b, q, k, v, r, glr = jnp.split(h @ w_in[l], SPLITS, axis=-1)
        y_conv = conformer_conv(a, b, w_dw[l], b_dw[l], g_conv_ln[l], b_conv_ln[l])
        y_gla = gla(q, k, v, r, glr, w_gate_up[l], b_gate[l], g_gla_norm[l])
        x = x + mod[:, 5] * (jnp.concatenate([y_conv, y_gla], axis=-1) @ w_out[l])
        h = modulate(rmsnorm(x, g_norm_ffn2[l]), mod[:, 6], mod[:, 7])
        x = x + 0.5 * mod[:, 8] * swiglu(h, w_ffn2_in[l], w_ffn2_out[l])
    fmod = (c_act @ w_ada_final + b_ada_final).reshape(bsz, 2, 1, D_MODEL)
    return modulate(rmsnorm(x, g_norm_final), fmod[:, 0], fmod[:, 1])


import jax as _jax
import jax.numpy as _jnp

TWIN_FORMAT = 'train_step'
FWD_PARAMS = ['x', 'c', 'w_ada', 'b_ada', 'g_norm_ffn1', 'w_ffn1_in', 'w_ffn1_out', 'g_norm_mix', 'w_in', 'w_dw', 'b_dw', 'g_conv_ln', 'b_conv_ln', 'w_gate_up', 'b_gate', 'g_gla_norm', 'w_out', 'g_norm_ffn2', 'w_ffn2_in', 'w_ffn2_out', 'g_norm_final', 'w_ada_final', 'b_ada_final']
TWIN_WEIGHTS = ['w_ada', 'b_ada', 'g_norm_ffn1', 'w_ffn1_in', 'w_ffn1_out', 'g_norm_mix', 'w_in', 'w_dw', 'b_dw', 'g_conv_ln', 'b_conv_ln', 'w_gate_up', 'b_gate', 'g_gla_norm', 'w_out', 'g_norm_ffn2', 'w_ffn2_in', 'w_ffn2_out', 'g_norm_final', 'w_ada_final', 'b_ada_final']
TWIN_DIFF_INPUT = 'x'
TWIN_INPUTS = ['x', 'c', 'w_ada', 'b_ada', 'g_norm_ffn1', 'w_ffn1_in', 'w_ffn1_out', 'g_norm_mix', 'w_in', 'w_dw', 'b_dw', 'g_conv_ln', 'b_conv_ln', 'w_gate_up', 'b_gate', 'g_gla_norm', 'w_out', 'g_norm_ffn2', 'w_ffn2_in', 'w_ffn2_out', 'g_norm_final', 'w_ada_final', 'b_ada_final', 'loss_target', 'm_w_ada', 'm_b_ada', 'm_g_norm_ffn1', 'm_w_ffn1_in', 'm_w_ffn1_out', 'm_g_norm_mix', 'm_w_in', 'm_w_dw', 'm_b_dw', 'm_g_conv_ln', 'm_b_conv_ln', 'm_w_gate_up', 'm_b_gate', 'm_g_gla_norm', 'm_w_out', 'm_g_norm_ffn2', 'm_w_ffn2_in', 'm_w_ffn2_out', 'm_g_norm_final', 'm_w_ada_final', 'm_b_ada_final', 'v_w_ada', 'v_b_ada', 'v_g_norm_ffn1', 'v_w_ffn1_in', 'v_w_ffn1_out', 'v_g_norm_mix', 'v_w_in', 'v_w_dw', 'v_b_dw', 'v_g_conv_ln', 'v_b_conv_ln', 'v_w_gate_up', 'v_b_gate', 'v_g_gla_norm', 'v_w_out', 'v_g_norm_ffn2', 'v_w_ffn2_in', 'v_w_ffn2_out', 'v_g_norm_final', 'v_w_ada_final', 'v_b_ada_final']
TWIN_OUTPUTS = ['loss', 'grad_x', 'grad_w_ada', 'grad_b_ada', 'grad_g_norm_ffn1', 'grad_w_ffn1_in', 'grad_w_ffn1_out', 'grad_g_norm_mix', 'grad_w_in', 'grad_w_dw', 'grad_b_dw', 'grad_g_conv_ln', 'grad_b_conv_ln', 'grad_w_gate_up', 'grad_b_gate', 'grad_g_gla_norm', 'grad_w_out', 'grad_g_norm_ffn2', 'grad_w_ffn2_in', 'grad_w_ffn2_out', 'grad_g_norm_final', 'grad_w_ada_final', 'grad_b_ada_final', 'delta_w_ada', 'delta_b_ada', 'delta_g_norm_ffn1', 'delta_w_ffn1_in', 'delta_w_ffn1_out', 'delta_g_norm_mix', 'delta_w_in', 'delta_w_dw', 'delta_b_dw', 'delta_g_conv_ln', 'delta_b_conv_ln', 'delta_w_gate_up', 'delta_b_gate', 'delta_g_gla_norm', 'delta_w_out', 'delta_g_norm_ffn2', 'delta_w_ffn2_in', 'delta_w_ffn2_out', 'delta_g_norm_final', 'delta_w_ada_final', 'delta_b_ada_final', 'new_m_w_ada', 'new_m_b_ada', 'new_m_g_norm_ffn1', 'new_m_w_ffn1_in', 'new_m_w_ffn1_out', 'new_m_g_norm_mix', 'new_m_w_in', 'new_m_w_dw', 'new_m_b_dw', 'new_m_g_conv_ln', 'new_m_b_conv_ln', 'new_m_w_gate_up', 'new_m_b_gate', 'new_m_g_gla_norm', 'new_m_w_out', 'new_m_g_norm_ffn2', 'new_m_w_ffn2_in', 'new_m_w_ffn2_out', 'new_m_g_norm_final', 'new_m_w_ada_final', 'new_m_b_ada_final', 'new_v_w_ada', 'new_v_b_ada', 'new_v_g_norm_ffn1', 'new_v_w_ffn1_in', 'new_v_w_ffn1_out', 'new_v_g_norm_mix', 'new_v_w_in', 'new_v_w_dw', 'new_v_b_dw', 'new_v_g_conv_ln', 'new_v_b_conv_ln', 'new_v_w_gate_up', 'new_v_b_gate', 'new_v_g_gla_norm', 'new_v_w_out', 'new_v_g_norm_ffn2', 'new_v_w_ffn2_in', 'new_v_w_ffn2_out', 'new_v_g_norm_final', 'new_v_w_ada_final', 'new_v_b_ada_final']
TWIN_LEAF_KINDS = {'loss': 'loss', 'grad_x': 'grad_x', 'grad_w_ada': 'grad_w', 'grad_b_ada': 'grad_w', 'grad_g_norm_ffn1': 'grad_w', 'grad_w_ffn1_in': 'grad_w', 'grad_w_ffn1_out': 'grad_w', 'grad_g_norm_mix': 'grad_w', 'grad_w_in': 'grad_w', 'grad_w_dw': 'grad_w', 'grad_b_dw': 'grad_w', 'grad_g_conv_ln': 'grad_w', 'grad_b_conv_ln': 'grad_w', 'grad_w_gate_up': 'grad_w', 'grad_b_gate': 'grad_w', 'grad_g_gla_norm': 'grad_w', 'grad_w_out': 'grad_w', 'grad_g_norm_ffn2': 'grad_w', 'grad_w_ffn2_in': 'grad_w', 'grad_w_ffn2_out': 'grad_w', 'grad_g_norm_final': 'grad_w', 'grad_w_ada_final': 'grad_w', 'grad_b_ada_final': 'grad_w', 'delta_w_ada': 'delta_w', 'delta_b_ada': 'delta_w', 'delta_g_norm_ffn1': 'delta_w', 'delta_w_ffn1_in': 'delta_w', 'delta_w_ffn1_out': 'delta_w', 'delta_g_norm_mix': 'delta_w', 'delta_w_in': 'delta_w', 'delta_w_dw': 'delta_w', 'delta_b_dw': 'delta_w', 'delta_g_conv_ln': 'delta_w', 'delta_b_conv_ln': 'delta_w', 'delta_w_gate_up': 'delta_w', 'delta_b_gate': 'delta_w', 'delta_g_gla_norm': 'delta_w', 'delta_w_out': 'delta_w', 'delta_g_norm_ffn2': 'delta_w', 'delta_w_ffn2_in': 'delta_w', 'delta_w_ffn2_out': 'delta_w', 'delta_g_norm_final': 'delta_w', 'delta_w_ada_final': 'delta_w', 'delta_b_ada_final': 'delta_w', 'new_m_w_ada': 'new_m', 'new_m_b_ada': 'new_m', 'new_m_g_norm_ffn1': 'new_m', 'new_m_w_ffn1_in': 'new_m', 'new_m_w_ffn1_out': 'new_m', 'new_m_g_norm_mix': 'new_m', 'new_m_w_in': 'new_m', 'new_m_w_dw': 'new_m', 'new_m_b_dw': 'new_m', 'new_m_g_conv_ln': 'new_m', 'new_m_b_conv_ln': 'new_m', 'new_m_w_gate_up': 'new_m', 'new_m_b_gate': 'new_m', 'new_m_g_gla_norm': 'new_m', 'new_m_w_out': 'new_m', 'new_m_g_norm_ffn2': 'new_m', 'new_m_w_ffn2_in': 'new_m', 'new_m_w_ffn2_out': 'new_m', 'new_m_g_norm_final': 'new_m', 'new_m_w_ada_final': 'new_m', 'new_m_b_ada_final': 'new_m', 'new_v_w_ada': 'new_v', 'new_v_b_ada': 'new_v', 'new_v_g_norm_ffn1': 'new_v', 'new_v_w_ffn1_in': 'new_v', 'new_v_w_ffn1_out': 'new_v', 'new_v_g_norm_mix': 'new_v', 'new_v_w_in': 'new_v', 'new_v_w_dw': 'new_v', 'new_v_b_dw': 'new_v', 'new_v_g_conv_ln': 'new_v', 'new_v_b_conv_ln': 'new_v', 'new_v_w_gate_up': 'new_v', 'new_v_b_gate': 'new_v', 'new_v_g_gla_norm': 'new_v', 'new_v_w_out': 'new_v', 'new_v_g_norm_ffn2': 'new_v', 'new_v_w_ffn2_in': 'new_v', 'new_v_w_ffn2_out': 'new_v', 'new_v_g_norm_final': 'new_v', 'new_v_w_ada_final': 'new_v', 'new_v_b_ada_final': 'new_v'}


def _forward(args):
    return _fwd_reference(*[args[k] for k in FWD_PARAMS])


def _output_shape():
    def fwd():
        inp = _fwd_setup_inputs(0)
        return _fwd_reference(*[inp[k] for k in FWD_PARAMS])
    out = _jax.eval_shape(fwd)
    return out.shape, out.dtype

N_MICROBATCH = 1
ADAM_LR = 0.001
ADAM_B1 = 0.9
ADAM_B2 = 0.999
ADAM_EPS = 1e-08
ADAM_WD = 0.01
ADAM_STEP = 10
PER_EXAMPLE_BATCH_AXIS = {'x': 0, 'c': 0, 'loss_target': 0}
SHARED_INPUTS = []
_WEIGHT_DTYPES = {'w_ada': _jnp.float32, 'b_ada': _jnp.float32, 'g_norm_ffn1': _jnp.float32, 'w_ffn1_in': _jnp.float32, 'w_ffn1_out': _jnp.float32, 'g_norm_mix': _jnp.float32, 'w_in': _jnp.float32, 'w_dw': _jnp.float32, 'b_dw': _jnp.float32, 'g_conv_ln': _jnp.float32, 'b_conv_ln': _jnp.float32, 'w_gate_up': _jnp.float32, 'b_gate': _jnp.float32, 'g_gla_norm': _jnp.float32, 'w_out': _jnp.float32, 'g_norm_ffn2': _jnp.float32, 'w_ffn2_in': _jnp.float32, 'w_ffn2_out': _jnp.float32, 'g_norm_final': _jnp.float32, 'w_ada_final': _jnp.float32, 'b_ada_final': _jnp.float32}
MOMENT_SCALE = {'w_ada': 3.950073e-01, 'b_ada': 7.345243e-01, 'g_norm_ffn1': 5.395343e-02, 'w_ffn1_in': 3.442009e-02, 'w_ffn1_out': 6.733454e-02, 'g_norm_mix': 1.013861e-01, 'w_in': 1.073000e-01, 'w_dw': 2.419297e-01, 'b_dw': 1.646790e+00, 'g_conv_ln': 7.387715e-01, 'b_conv_ln': 1.053689e+00, 'w_gate_up': 1.927924e-02, 'b_gate': 5.738758e-02, 'g_gla_norm': 9.470297e-02, 'w_out': 3.194727e-01, 'g_norm_ffn2': 5.854367e-02, 'w_ffn2_in': 3.583246e-02, 'w_ffn2_out': 7.001032e-02, 'g_norm_final': 7.225805e+01, 'w_ada_final': 1.376519e+01, 'b_ada_final': 4.628402e+01}


def _to_microbatches(a, axis):
    t = _jnp.moveaxis(a, axis, 0)
    t = t.reshape((N_MICROBATCH, t.shape[0] // N_MICROBATCH) + t.shape[1:])
    return _jnp.moveaxis(t, 1, axis + 1)


def setup_inputs(seed: int = 0) -> dict:
    inp = _fwd_setup_inputs(seed)
    key = _jax.random.fold_in(_jax.random.key(seed), 7919)
    shape, _ = _output_shape()
    out = dict(inp)
    out["loss_target"] = _jax.random.normal(_jax.random.fold_in(key, 0), shape, _jnp.float32)
    for i, name in enumerate(TWIN_WEIGHTS):
        w = inp[name].astype(_jnp.float32)
        if MOMENT_SCALE is None:
            s = _jnp.sqrt(_jnp.mean(_jnp.square(w)) + 1e-30)
        else:
            s = MOMENT_SCALE[name]
        km, kv = _jax.random.split(_jax.random.fold_in(key, i + 1))
        out[name] = w
        out["m_" + name] = s * _jax.random.normal(km, w.shape, _jnp.float32)
        out["v_" + name] = (s * s) * _jax.random.uniform(kv, w.shape, _jnp.float32, 0.5, 1.5)
    if N_MICROBATCH > 1:
        for name, axis in PER_EXAMPLE_BATCH_AXIS.items():
            out[name] = _to_microbatches(out[name], axis)
    return {'x': out['x'], 'c': out['c'], 'w_ada': out['w_ada'], 'b_ada': out['b_ada'], 'g_norm_ffn1': out['g_norm_ffn1'], 'w_ffn1_in': out['w_ffn1_in'], 'w_ffn1_out': out['w_ffn1_out'], 'g_norm_mix': out['g_norm_mix'], 'w_in': out['w_in'], 'w_dw': out['w_dw'], 'b_dw': out['b_dw'], 'g_conv_ln': out['g_conv_ln'], 'b_conv_ln': out['b_conv_ln'], 'w_gate_up': out['w_gate_up'], 'b_gate': out['b_gate'], 'g_gla_norm': out['g_gla_norm'], 'w_out': out['w_out'], 'g_norm_ffn2': out['g_norm_ffn2'], 'w_ffn2_in': out['w_ffn2_in'], 'w_ffn2_out': out['w_ffn2_out'], 'g_norm_final': out['g_norm_final'], 'w_ada_final': out['w_ada_final'], 'b_ada_final': out['b_ada_final'], 'loss_target': out['loss_target'], 'm_w_ada': out['m_w_ada'], 'm_b_ada': out['m_b_ada'], 'm_g_norm_ffn1': out['m_g_norm_ffn1'], 'm_w_ffn1_in': out['m_w_ffn1_in'], 'm_w_ffn1_out': out['m_w_ffn1_out'], 'm_g_norm_mix': out['m_g_norm_mix'], 'm_w_in': out['m_w_in'], 'm_w_dw': out['m_w_dw'], 'm_b_dw': out['m_b_dw'], 'm_g_conv_ln': out['m_g_conv_ln'], 'm_b_conv_ln': out['m_b_conv_ln'], 'm_w_gate_up': out['m_w_gate_up'], 'm_b_gate': out['m_b_gate'], 'm_g_gla_norm': out['m_g_gla_norm'], 'm_w_out': out['m_w_out'], 'm_g_norm_ffn2': out['m_g_norm_ffn2'], 'm_w_ffn2_in': out['m_w_ffn2_in'], 'm_w_ffn2_out': out['m_w_ffn2_out'], 'm_g_norm_final': out['m_g_norm_final'], 'm_w_ada_final': out['m_w_ada_final'], 'm_b_ada_final': out['m_b_ada_final'], 'v_w_ada': out['v_w_ada'], 'v_b_ada': out['v_b_ada'], 'v_g_norm_ffn1': out['v_g_norm_ffn1'], 'v_w_ffn1_in': out['v_w_ffn1_in'], 'v_w_ffn1_out': out['v_w_ffn1_out'], 'v_g_norm_mix': out['v_g_norm_mix'], 'v_w_in': out['v_w_in'], 'v_w_dw': out['v_w_dw'], 'v_b_dw': out['v_b_dw'], 'v_g_conv_ln': out['v_g_conv_ln'], 'v_b_conv_ln': out['v_b_conv_ln'], 'v_w_gate_up': out['v_w_gate_up'], 'v_b_gate': out['v_b_gate'], 'v_g_gla_norm': out['v_g_gla_norm'], 'v_w_out': out['v_w_out'], 'v_g_norm_ffn2': out['v_g_norm_ffn2'], 'v_w_ffn2_in': out['v_w_ffn2_in'], 'v_w_ffn2_out': out['v_w_ffn2_out'], 'v_g_norm_final': out['v_g_norm_final'], 'v_w_ada_final': out['v_w_ada_final'], 'v_b_ada_final': out['v_b_ada_final']}


def _loss(weights, diff, rest, loss_target):
    with _jax.named_scope("forward"):
        args = {**rest, TWIN_DIFF_INPUT: diff, **{k: w.astype(_WEIGHT_DTYPES[k]) for k, w in weights.items()}}
        y = _forward(args)
    with _jax.named_scope("loss_head"):
        err = _jnp.square(y.astype(_jnp.float32) - loss_target)
        return 0.5 * _jnp.sum(_jnp.mean(err, axis=-1)) if err.ndim else 0.5 * err


def _adamw(w, g, m, v):
    m = ADAM_B1 * m + (1.0 - ADAM_B1) * g
    v = ADAM_B2 * v + (1.0 - ADAM_B2) * _jnp.square(g)
    m_hat = m / (1.0 - ADAM_B1 ** ADAM_STEP)
    v_hat = v / (1.0 - ADAM_B2 ** ADAM_STEP)
    delta = -ADAM_LR * (m_hat / (_jnp.sqrt(v_hat) + ADAM_EPS) + ADAM_WD * w)
    return delta, m, v


def reference(x, c, w_ada, b_ada, g_norm_ffn1, w_ffn1_in, w_ffn1_out, g_norm_mix, w_in, w_dw, b_dw, g_conv_ln, b_conv_ln, w_gate_up, b_gate, g_gla_norm, w_out, g_norm_ffn2, w_ffn2_in, w_ffn2_out, g_norm_final, w_ada_final, b_ada_final, loss_target, m_w_ada, m_b_ada, m_g_norm_ffn1, m_w_ffn1_in, m_w_ffn1_out, m_g_norm_mix, m_w_in, m_w_dw, m_b_dw, m_g_conv_ln, m_b_conv_ln, m_w_gate_up, m_b_gate, m_g_gla_norm, m_w_out, m_g_norm_ffn2, m_w_ffn2_in, m_w_ffn2_out, m_g_norm_final, m_w_ada_final, m_b_ada_final, v_w_ada, v_b_ada, v_g_norm_ffn1, v_w_ffn1_in, v_w_ffn1_out, v_g_norm_mix, v_w_in, v_w_dw, v_b_dw, v_g_conv_ln, v_b_conv_ln, v_w_gate_up, v_b_gate, v_g_gla_norm, v_w_out, v_g_norm_ffn2, v_w_ffn2_in, v_w_ffn2_out, v_g_norm_final, v_w_ada_final, v_b_ada_final):
    given = dict(x=x, c=c, w_ada=w_ada, b_ada=b_ada, g_norm_ffn1=g_norm_ffn1, w_ffn1_in=w_ffn1_in, w_ffn1_out=w_ffn1_out, g_norm_mix=g_norm_mix, w_in=w_in, w_dw=w_dw, b_dw=b_dw, g_conv_ln=g_conv_ln, b_conv_ln=b_conv_ln, w_gate_up=w_gate_up, b_gate=b_gate, g_gla_norm=g_gla_norm, w_out=w_out, g_norm_ffn2=g_norm_ffn2, w_ffn2_in=w_ffn2_in, w_ffn2_out=w_ffn2_out, g_norm_final=g_norm_final, w_ada_final=w_ada_final, b_ada_final=b_ada_final, loss_target=loss_target, m_w_ada=m_w_ada, m_b_ada=m_b_ada, m_g_norm_ffn1=m_g_norm_ffn1, m_w_ffn1_in=m_w_ffn1_in, m_w_ffn1_out=m_w_ffn1_out, m_g_norm_mix=m_g_norm_mix, m_w_in=m_w_in, m_w_dw=m_w_dw, m_b_dw=m_b_dw, m_g_conv_ln=m_g_conv_ln, m_b_conv_ln=m_b_conv_ln, m_w_gate_up=m_w_gate_up, m_b_gate=m_b_gate, m_g_gla_norm=m_g_gla_norm, m_w_out=m_w_out, m_g_norm_ffn2=m_g_norm_ffn2, m_w_ffn2_in=m_w_ffn2_in, m_w_ffn2_out=m_w_ffn2_out, m_g_norm_final=m_g_norm_final, m_w_ada_final=m_w_ada_final, m_b_ada_final=m_b_ada_final, v_w_ada=v_w_ada, v_b_ada=v_b_ada, v_g_norm_ffn1=v_g_norm_ffn1, v_w_ffn1_in=v_w_ffn1_in, v_w_ffn1_out=v_w_ffn1_out, v_g_norm_mix=v_g_norm_mix, v_w_in=v_w_in, v_w_dw=v_w_dw, v_b_dw=v_b_dw, v_g_conv_ln=v_g_conv_ln, v_b_conv_ln=v_b_conv_ln, v_w_gate_up=v_w_gate_up, v_b_gate=v_b_gate, v_g_gla_norm=v_g_gla_norm, v_w_out=v_w_out, v_g_norm_ffn2=v_g_norm_ffn2, v_w_ffn2_in=v_w_ffn2_in, v_w_ffn2_out=v_w_ffn2_out, v_g_norm_final=v_g_norm_final, v_w_ada_final=v_w_ada_final, v_b_ada_final=v_b_ada_final)
    weights = {n: given[n] for n in TWIN_WEIGHTS}
    shared = {n: given[n] for n in SHARED_INPUTS}
    per_example = {n: given[n] for n in ['x', 'c']}
    grad_fn = _jax.value_and_grad(_loss, argnums=(0, 1))

    def one_microbatch(ex, loss_target):
        ex = dict(ex)
        diff = ex.pop(TWIN_DIFF_INPUT)
        return grad_fn(weights, diff, {**shared, **ex}, loss_target)

    if N_MICROBATCH == 1:
        loss, (grad_w, grad_x) = one_microbatch(per_example, given["loss_target"])
    else:
        def body(carry, xs):
            loss_sum, grad_sum = carry
            l_k, (gw_k, gx_k) = one_microbatch(xs[0], xs[1])
            with _jax.named_scope("update"):
                return (loss_sum + l_k, _jax.tree.map(_jnp.add, grad_sum, gw_k)), gx_k

        init = (_jnp.zeros((), _jnp.float32), _jax.tree.map(_jnp.zeros_like, weights))
        (loss, grad_w), grad_x = _jax.lax.scan(body, init, (per_example, given["loss_target"]))
    with _jax.named_scope("update"):
        delta_w, new_m, new_v = {}, {}, {}
        for n in TWIN_WEIGHTS:
            delta_w[n], new_m[n], new_v[n] = _adamw(weights[n], grad_w[n], given["m_" + n], given["v_" + n])
    return (loss, grad_x, *[grad_w[n] for n in TWIN_WEIGHTS], *[delta_w[n] for n in TWIN_WEIGHTS],
            *[new_m[n] for n in TWIN_WEIGHTS], *[new_v[n] for n in TWIN_WEIGHTS])
```

```python
import functools

import jax
import jax.numpy as jnp
from jax import lax
from jax.experimental import pallas as pl
from jax.experimental.pallas import tpu as pltpu

F32 = jnp.float32
BF16 = jnp.bfloat16

CHUNK = 64
HEADS = 4
DK = 64
DV = 128
DKP = 128
GATE_RANK = 16
GATE_TAU = 16.0
N_MOD = 9
EPS = 1e-6
ADAM_LR = 0.001
ADAM_B1 = 0.9
ADAM_B2 = 0.999
ADAM_EPS = 1e-08
ADAM_WD = 0.01
ADAM_STEP = 10

LANE = 128
HALO = 32
VMEM_LIMIT = 52 * 1024 * 1024
MESH = pl.DeviceIdType.MESH

D_CONV = 512
D_GLA = HEADS * DV
ZC_A, ZC_B = 0, D_CONV
ZC_Q = 2 * D_CONV
ZC_K = ZC_Q + HEADS * DKP
ZC_V = ZC_K + HEADS * DKP
ZC_R = ZC_V + D_GLA
ZC_G = ZC_R + D_GLA
Z_COLS = ZC_G + LANE


def _div(n, target, mult):
    best = None
    d = mult
    while d <= min(n, target):
        if n % d == 0:
            best = d
        d += mult
    return n if best is None else best


def _cp(sem=None, **kw):
    return pltpu.CompilerParams(dimension_semantics=sem, vmem_limit_bytes=VMEM_LIMIT, **kw)


def _sigmoid(x):
    return 1.0 / (1.0 + jnp.exp(-x))


def _dot(a, b):
    return jnp.dot(a.astype(BF16), b.astype(BF16), preferred_element_type=F32)


def _dot_nt(a, b):
    return lax.dot_general(a.astype(BF16), b.astype(BF16), (((1,), (1,)), ((), ())), preferred_element_type=F32)


def _dot_tn(a, b):
    return lax.dot_general(a.astype(BF16), b.astype(BF16), (((0,), (0,)), ((), ())), preferred_element_type=F32)


def _dot_exact(a, b):
    return jnp.dot(a, b, preferred_element_type=F32, precision=lax.Precision.HIGHEST)


def _normmod(x, g, shift, scale, name):
    S, D = x.shape
    tm = _div(S, 512, 8)

    def body(x_ref, g_ref, sh_ref, sc_ref, o_ref):
        xv = x_ref[...]
        r = lax.rsqrt(jnp.mean(xv * xv, axis=-1, keepdims=True) + EPS)
        o_ref[...] = ((xv * r) * g_ref[...] * (1.0 + sc_ref[...]) + sh_ref[...]).astype(o_ref.dtype)

    row = pl.BlockSpec((tm, D), lambda i: (i, 0))
    vec = pl.BlockSpec((1, D), lambda i: (0, 0))
    return pl.pallas_call(
        body, grid=(S // tm,), in_specs=[row, vec, vec, vec], out_specs=row,
        out_shape=jax.ShapeDtypeStruct((S, D), BF16), compiler_params=_cp(("parallel",)), name=name,
    )(x, g, shift, scale)


def _final_loss(x, g, shift, scale, tgt):
    S, D = x.shape
    tm = _div(S, 512, 8)

    def body(x_ref, g_ref, sh_ref, sc_ref, t_ref, dh_ref, sq_ref):
        @pl.when(pl.program_id(0) == 0)
        def _():
            sq_ref[...] = jnp.zeros_like(sq_ref)

        xv = x_ref[...]
        r = lax.rsqrt(jnp.mean(xv * xv, axis=-1, keepdims=True) + EPS)
        h = (xv * r) * g_ref[...] * (1.0 + sc_ref[...]) + sh_ref[...]
        e = h - t_ref[...]
        dh_ref[...] = e * (1.0 / D)
        sq_ref[...] += jnp.sum(e * e, axis=0, keepdims=True)

    row = pl.BlockSpec((tm, D), lambda i: (i, 0))
    vec = pl.BlockSpec((1, D), lambda i: (0, 0))
    return pl.pallas_call(
        body, grid=(S // tm,), in_specs=[row, vec, vec, vec, row], out_specs=[row, vec],
        out_shape=[jax.ShapeDtypeStruct((S, D), F32), jax.ShapeDtypeStruct((1, D), F32)],
        compiler_params=_cp(("arbitrary",)), name="final_loss",
    )(x, g, shift, scale, tgt)


def _normmod_bwd(x, dh, dres, g, scale, name):
    S, D = x.shape
    tm = _div(S, 512, 8)
    with_res = dres is not None

    def body(*refs):
        if with_res:
            x_ref, dh_ref, dr_ref, g_ref, sc_ref, dx_ref, dsh_ref, dsc_ref, dg_ref = refs
        else:
            x_ref, dh_ref, g_ref, sc_ref, dx_ref, dsh_ref, dsc_ref, dg_ref = refs

        @pl.when(pl.program_id(0) == 0)
        def _():
            dsh_ref[...] = jnp.zeros_like(dsh_ref)
            dsc_ref[...] = jnp.zeros_like(dsc_ref)
            dg_ref[...] = jnp.zeros_like(dg_ref)

        xv = x_ref[...]
        dh = dh_ref[...].astype(F32)
        gv = g_ref[...]
        r = lax.rsqrt(jnp.mean(xv * xv, axis=-1, keepdims=True) + EPS)
        xh = xv * r
        dsh_ref[...] += jnp.sum(dh, axis=0, keepdims=True)
        dsc_ref[...] += jnp.sum(dh * (xh * gv), axis=0, keepdims=True)
        dn = dh * (1.0 + sc_ref[...])
        dg_ref[...] += jnp.sum(dn * xh, axis=0, keepdims=True)
        dxh = dn * gv
        dx = r * (dxh - xh * jnp.mean(dxh * xh, axis=-1, keepdims=True))
        if with_res:
            dx = dx + dr_ref[...]
        dx_ref[...] = dx

    row = pl.BlockSpec((tm, D), lambda i: (i, 0))
    vec = pl.BlockSpec((1, D), lambda i: (0, 0))
    ins = [row, row, row, vec, vec] if with_res else [row, row, vec, vec]
    args = (x, dh, dres, g, scale) if with_res else (x, dh, g, scale)
    vs = jax.ShapeDtypeStruct((1, D), F32)
    return pl.pallas_call(
        body, grid=(S // tm,), in_specs=ins, out_specs=[row, vec, vec, vec],
        out_shape=[jax.ShapeDtypeStruct((S, D), F32), vs, vs, vs],
        compiler_params=_cp(("arbitrary",)), name=name,
    )(*args)


def _mm(pairs, out_dtype, name, tm_target=512):
    M = pairs[0][0].shape[0]
    N = pairs[0][1].shape[1]
    ktot = sum(a.shape[1] for a, _ in pairs)
    tn = N if ktot * N * 2 <= 12 * 1024 * 1024 else _div(N, 1408, LANE)
    tm = _div(M, tm_target if ktot <= 4096 else 256, 8)
    n = len(pairs)

    def body(*refs):
        o_ref = refs[2 * n]
        acc = _dot(refs[0][...], refs[1][...])
        for p in range(1, n):
            acc = acc + _dot(refs[2 * p][...], refs[2 * p + 1][...])
        o_ref[...] = acc.astype(o_ref.dtype)

    ins, args = [], []
    for a, b in pairs:
        ins += [pl.BlockSpec((tm, a.shape[1]), lambda i, j: (i, 0)), pl.BlockSpec((b.shape[0], tn), lambda i, j: (0, j))]
        args += [a, b]
    return pl.pallas_call(
        body, grid=(M // tm, N // tn), in_specs=ins, out_specs=pl.BlockSpec((tm, tn), lambda i, j: (i, j)),
        out_shape=jax.ShapeDtypeStruct((M, N), out_dtype), compiler_params=_cp(("parallel", "parallel")), name=name,
    )(*args)


def _mm_tn(a, g, name):
    S, Ka = a.shape
    N = g.shape[1]
    tk = _div(Ka, 1408, LANE)
    tn = _div(N, 1408, LANE)
    ts = _div(S, 512, 8)

    def body(a_ref, g_ref, o_ref):
        @pl.when(pl.program_id(2) == 0)
        def _():
            o_ref[...] = jnp.zeros_like(o_ref)

        o_ref[...] += _dot_tn(a_ref[...], g_ref[...])

    return pl.pallas_call(
        body, grid=(Ka // tk, N // tn, S // ts),
        in_specs=[pl.BlockSpec((ts, tk), lambda i, j, s: (s, i)), pl.BlockSpec((ts, tn), lambda i, j, s: (s, j))],
        out_specs=pl.BlockSpec((tk, tn), lambda i, j, s: (i, j)),
        out_shape=jax.ShapeDtypeStruct((Ka, N), F32),
        compiler_params=_cp(("parallel", "parallel", "arbitrary")), name=name,
    )(a, g)


def _ffn_up(h, wg, wu, name):
    S, D = h.shape
    Fd = wg.shape[1]
    tm = _div(S, 512, 8)
    tn = _div(Fd, 1408, LANE)

    def body(h_ref, wg_ref, wu_ref, g_ref, u_ref, a_ref):
        hv = h_ref[...]
        gt = _dot(hv, wg_ref[...])
        up = _dot(hv, wu_ref[...])
        g_ref[...] = gt.astype(BF16)
        u_ref[...] = up.astype(BF16)
        a_ref[...] = (gt * _sigmoid(gt) * up).astype(BF16)

    wspec = pl.BlockSpec((D, tn), lambda i, j: (0, j))
    ospec = pl.BlockSpec((tm, tn), lambda i, j: (i, j))
    os_ = jax.ShapeDtypeStruct((S, Fd), BF16)
    return pl.pallas_call(
        body, grid=(S // tm, Fd // tn), in_specs=[pl.BlockSpec((tm, D), lambda i, j: (i, 0)), wspec, wspec],
        out_specs=[ospec, ospec, ospec], out_shape=[os_, os_, os_],
        compiler_params=_cp(("parallel", "parallel")), name=name,
    )(h, wg, wu)


def _mm_resid(a, w, x, gv, name):
    S, K = a.shape
    N = w.shape[1]
    tm = _div(S, 512, 8)

    def body(a_ref, w_ref, x_ref, gv_ref, y_ref, xn_ref):
        y = _dot(a_ref[...], w_ref[...])
        y_ref[...] = y
        xn_ref[...] = x_ref[...] + gv_ref[...] * y

    row = pl.BlockSpec((tm, N), lambda i: (i, 0))
    os_ = jax.ShapeDtypeStruct((S, N), F32)
    return pl.pallas_call(
        body, grid=(S // tm,),
        in_specs=[pl.BlockSpec((tm, K), lambda i: (i, 0)), pl.BlockSpec((K, N), lambda i: (0, 0)), row, pl.BlockSpec((1, N), lambda i: (0, 0))],
        out_specs=[row, row], out_shape=[os_, os_], compiler_params=_cp(("parallel",)), name=name,
    )(a, w, x, gv)


def _resid_bwd(dxn, y, gv, name):
    S, D = dxn.shape
    tm = _div(S, 512, 8)

    def body(d_ref, y_ref, gv_ref, dy_ref, dg_ref):
        @pl.when(pl.program_id(0) == 0)
        def _():
            dg_ref[...] = jnp.zeros_like(dg_ref)

        d = d_ref[...]
        dy_ref[...] = (gv_ref[...] * d).astype(BF16)
        dg_ref[...] += jnp.sum(d * y_ref[...], axis=0, keepdims=True)

    row = pl.BlockSpec((tm, D), lambda i: (i, 0))
    vec = pl.BlockSpec((1, D), lambda i: (0, 0))
    return pl.pallas_call(
        body, grid=(S // tm,), in_specs=[row, row, vec], out_specs=[row, vec],
        out_shape=[jax.ShapeDtypeStruct((S, D), BF16), jax.ShapeDtypeStruct((1, D), F32)],
        compiler_params=_cp(("arbitrary",)), name=name,
    )(dxn, y, gv)


def _ffn_bwd_act(dy, wo_t, gate, up, name):
    S, D = dy.shape
    Fd = wo_t.shape[1]
    tm = _div(S, 512, 8)
    tn = _div(Fd, 1408, LANE)

    def body(dy_ref, w_ref, g_ref, u_ref, dg_ref, du_ref):
        da = _dot(dy_ref[...], w_ref[...])
        gt = g_ref[...].astype(F32)
        sg = _sigmoid(gt)
        du_ref[...] = (da * gt * sg).astype(BF16)
        dg_ref[...] = (da * u_ref[...].astype(F32) * (sg * (1.0 + gt * (1.0 - sg)))).astype(BF16)

    ospec = pl.BlockSpec((tm, tn), lambda i, j: (i, j))
    os_ = jax.ShapeDtypeStruct((S, Fd), BF16)
    return pl.pallas_call(
        body, grid=(S // tm, Fd // tn),
        in_specs=[pl.BlockSpec((tm, D), lambda i, j: (i, 0)), pl.BlockSpec((D, tn), lambda i, j: (0, j)), ospec, ospec],
        out_specs=[ospec, ospec], out_shape=[os_, os_], compiler_params=_cp(("parallel", "parallel")), name=name,
    )(dy, wo_t, gate, up)


def _ln_parts(yc, g, b):
    mu = jnp.mean(yc, axis=-1, keepdims=True)
    xc = yc - mu
    rs = lax.rsqrt(jnp.mean(xc * xc, axis=-1, keepdims=True) + EPS)
    xh = xc * rs
    return xh, rs, xh * g + b


def _conv_fwd(z, w_dw, b_dw, g_ln, b_ln, name):
    S = z.shape[0]
    W, C = w_dw.shape
    ts = _div(S, 512, HALO)
    hb = ts // HALO
    off = HALO - (W - 1)

    def body(a_ref, b_ref, pa_ref, pb_ref, w_ref, bd_ref, g_ref, bl_ref, u_ref, yc_ref, o_ref, ext):
        keep = (pl.program_id(0) > 0).astype(F32)
        u = a_ref[...] * _sigmoid(b_ref[...])
        ext[pl.ds(0, HALO), :] = pa_ref[...] * _sigmoid(pb_ref[...]) * keep
        ext[pl.ds(HALO, ts), :] = u
        u_ref[...] = u
        acc = jnp.zeros((ts, C), F32)
        for j in range(W):
            acc = acc + w_ref[pl.ds(j, 1), :] * ext[pl.ds(off + j, ts), :]
        yc = acc + bd_ref[...]
        yc_ref[...] = yc
        _, _, ln = _ln_parts(yc, g_ref[...], bl_ref[...])
        o_ref[...] = (ln * _sigmoid(ln)).astype(BF16)

    cur = lambda col: pl.BlockSpec((ts, C), lambda i: (i, col))
    prev = lambda col: pl.BlockSpec((HALO, C), lambda i: (jnp.maximum(i * hb - 1, 0), col))
    vec = pl.BlockSpec((1, C), lambda i: (0, 0))
    row = pl.BlockSpec((ts, C), lambda i: (i, 0))
    fs = jax.ShapeDtypeStruct((S, C), F32)
    return pl.pallas_call(
        body, grid=(S // ts,),
        in_specs=[cur(0), cur(1), prev(0), prev(1), pl.BlockSpec((W, C), lambda i: (0, 0)), vec, vec, vec],
        out_specs=[row, row, row], out_shape=[fs, fs, jax.ShapeDtypeStruct((S, C), BF16)],
        scratch_shapes=[pltpu.VMEM((ts + HALO, C), F32)],
        compiler_params=_cp(("parallel",)), name=name,
    )(z, z, z, z, w_dw, b_dw, g_ln, b_ln)


def _conv_bwd(dycat, z, u, yc, w_dw, g_ln, b_ln, name):
    S = z.shape[0]
    W, C = w_dw.shape
    ts = _div(S, 512, HALO)
    hb = ts // HALO
    nblk = S // ts
    off = HALO - (W - 1)

    def ln_silu_bwd(dy, ycv, g, b):
        xh, rs, ln = _ln_parts(ycv, g, b)
        sl = _sigmoid(ln)
        dln = dy * (sl * (1.0 + ln * (1.0 - sl)))
        dxh = dln * g
        dyc = rs * (dxh - jnp.mean(dxh, axis=-1, keepdims=True) - xh * jnp.mean(dxh * xh, axis=-1, keepdims=True))
        return dyc, dln, xh

    def body(dy_ref, ndy_ref, yc_ref, nyc_ref, u_ref, pu_ref, a_ref, b_ref, w_ref, g_ref, bl_ref,
             da_ref, db_ref, dw_ref, dbd_ref, dg_ref, dbl_ref, uext, dext):
        i = pl.program_id(0)

        @pl.when(i == 0)
        def _():
            dw_ref[...] = jnp.zeros_like(dw_ref)
            dbd_ref[...] = jnp.zeros_like(dbd_ref)
            dg_ref[...] = jnp.zeros_like(dg_ref)
            dbl_ref[...] = jnp.zeros_like(dbl_ref)

        g = g_ref[...]
        bl = bl_ref[...]
        dyc, dln, xh = ln_silu_bwd(dy_ref[...], yc_ref[...], g, bl)
        ndyc, _, _ = ln_silu_bwd(ndy_ref[...], nyc_ref[...], g, bl)
        dg_ref[...] += jnp.sum(dln * xh, axis=0, keepdims=True)
        dbl_ref[...] += jnp.sum(dln, axis=0, keepdims=True)
        dbd_ref[...] += jnp.sum(dyc, axis=0, keepdims=True)
        dext[pl.ds(0, ts), :] = dyc
        dext[pl.ds(ts, HALO), :] = ndyc * (i < nblk - 1).astype(F32)
        uext[pl.ds(0, HALO), :] = pu_ref[...] * (i > 0).astype(F32)
        uext[pl.ds(HALO, ts), :] = u_ref[...]
        du = jnp.zeros((ts, C), F32)
        for j in range(W):
            du = du + w_ref[pl.ds(j, 1), :] * dext[pl.ds(W - 1 - j, ts), :]
            dw_ref[pl.ds(j, 1), :] += jnp.sum(dyc * uext[pl.ds(off + j, ts), :], axis=0, keepdims=True)
        sb = _sigmoid(b_ref[...])
        da_ref[...] = (du * sb).astype(BF16)
        db_ref[...] = (du * a_ref[...] * sb * (1.0 - sb)).astype(BF16)

    row = pl.BlockSpec((ts, C), lambda i: (i, 0))
    nxt = pl.BlockSpec((HALO, C), lambda i: (jnp.minimum((i + 1) * hb, S // HALO - 1), 0))
    prv = pl.BlockSpec((HALO, C), lambda i: (jnp.maximum(i * hb - 1, 0), 0))
    vec = pl.BlockSpec((1, C), lambda i: (0, 0))
    wsp = pl.BlockSpec((W, C), lambda i: (0, 0))
    bs = jax.ShapeDtypeStruct((S, C), BF16)
    vs = jax.ShapeDtypeStruct((1, C), F32)
    return pl.pallas_call(
        body, grid=(nblk,),
        in_specs=[row, nxt, row, nxt, row, prv, row, pl.BlockSpec((ts, C), lambda i: (i, 1)), wsp, vec, vec],
        out_specs=[row, row, wsp, vec, vec, vec],
        out_shape=[bs, bs, jax.ShapeDtypeStruct((W, C), F32), vs, vs, vs],
        scratch_shapes=[pltpu.VMEM((ts + HALO, C), F32), pltpu.VMEM((ts + HALO, C), F32)],
        compiler_params=_cp(("arbitrary",)), name=name,
    )(dycat, dycat, yc, yc, u, u, z, z, w_dw, g_ln, b_ln)


def _log_gate(zg):
    return (jnp.minimum(zg, 0.0) - jnp.log(1.0 + jnp.exp(-jnp.abs(zg)))) * (1.0 / GATE_TAU)


def _loggate(z, wgp, bgp, name):
    S = z.shape[0]
    N = wgp.shape[1]
    ts = _div(S, 512, 8)

    def body(g_ref, w_ref, b_ref, o_ref):
        o_ref[...] = _log_gate(_dot(g_ref[...], w_ref[...]) + b_ref[...])

    return pl.pallas_call(
        body, grid=(S // ts,),
        in_specs=[pl.BlockSpec((ts, LANE), lambda i: (i, ZC_G // LANE)), pl.BlockSpec((LANE, N), lambda i: (0, 0)), pl.BlockSpec((1, N), lambda i: (0, 0))],
        out_specs=pl.BlockSpec((ts, N), lambda i: (i, 0)), out_shape=jax.ShapeDtypeStruct((S, N), F32),
        compiler_params=_cp(("parallel",)), name=name,
    )(z, wgp, bgp)


def _loggate_bwd(dla, z, wgp, wgp_t, bgp, name):
    S = z.shape[0]
    N = wgp.shape[1]
    ts = _div(S, 512, 8)

    def body(dla_ref, g_ref, w_ref, wt_ref, b_ref, dg_ref, dw_ref, db_ref):
        @pl.when(pl.program_id(0) == 0)
        def _():
            dw_ref[...] = jnp.zeros_like(dw_ref)
            db_ref[...] = jnp.zeros_like(db_ref)

        glr = g_ref[...]
        zg = _dot(glr, w_ref[...]) + b_ref[...]
        dzg = dla_ref[...] * (1.0 / GATE_TAU) * (1.0 - _sigmoid(zg))
        dg_ref[...] = _dot(dzg, wt_ref[...]).astype(BF16)
        dw_ref[...] += _dot_tn(glr, dzg)
        db_ref[...] += jnp.sum(dzg, axis=0, keepdims=True)

    return pl.pallas_call(
        body, grid=(S // ts,),
        in_specs=[pl.BlockSpec((ts, N), lambda i: (i, 0)), pl.BlockSpec((ts, LANE), lambda i: (i, ZC_G // LANE)),
                  pl.BlockSpec((LANE, N), lambda i: (0, 0)), pl.BlockSpec((N, LANE), lambda i: (0, 0)), pl.BlockSpec((1, N), lambda i: (0, 0))],
        out_specs=[pl.BlockSpec((ts, LANE), lambda i: (i, 0)), pl.BlockSpec((LANE, N), lambda i: (0, 0)), pl.BlockSpec((1, N), lambda i: (0, 0))],
        out_shape=[jax.ShapeDtypeStruct((S, LANE), BF16), jax.ShapeDtypeStruct((LANE, N), F32), jax.ShapeDtypeStruct((1, N), F32)],
        compiler_params=_cp(("arbitrary",)), name=name,
    )(dla, z, wgp, wgp_t, bgp)


def _chunk_fwd_terms(q, k, la, tril):
    bc = _dot_exact(tril, la)
    bend = jnp.sum(la, axis=0, keepdims=True)
    eb = jnp.exp(bc)
    enb = jnp.exp(-bc)
    ee = jnp.exp(bend - bc)
    qs = q * (DK ** -0.5)
    return bc, bend, eb, enb, ee, qs * eb, qs * enb, k * enb, k * eb, k * ee


def _gla_fwd(z, la, gn3, name):
    S = z.shape[0]
    tb = _div(S, 512, CHUNK)
    cpb = tb // CHUNK
    nb = S // tb

    def body(q_ref, k_ref, v_ref, r_ref, la_ref, gn_ref, o_ref, sp_ref, y_ref, st):
        @pl.when(pl.program_id(1) == 0)
        def _():
            st[...] = jnp.zeros_like(st)

        ri = lax.broadcasted_iota(jnp.int32, (CHUNK, CHUNK), 0)
        ci = lax.broadcasted_iota(jnp.int32, (CHUNK, CHUNK), 1)
        tri = ri >= ci
        tril = tri.astype(F32)
        gn = gn_ref[...]

        def chunk(c, carry):
            rows = pl.ds(pl.multiple_of(c * CHUNK, CHUNK), CHUNK)
            q, k, v, lav = q_ref[rows, :], k_ref[rows, :], v_ref[rows, :], la_ref[rows, :]
            _, bend, _, _, _, qf, qb, kb, kf, ke = _chunk_fwd_terms(q, k, lav, tril)
            att = jnp.where(tri, _dot_nt(qf, kb), _dot_nt(qb, kf))
            s_prev = st[...]
            o = _dot(att, v) + _dot_nt(qf, s_prev)
            sp_ref[c] = s_prev
            st[...] = s_prev * jnp.exp(bend) + _dot_tn(v, ke)
            o_ref[rows, :] = o
            rms = lax.rsqrt(jnp.mean(o * o, axis=-1, keepdims=True) + EPS)
            rv = r_ref[rows, :]
            y_ref[rows, :] = (o * rms * gn * (rv * _sigmoid(rv))).astype(BF16)
            return carry

        lax.fori_loop(0, cpb, chunk, 0)

    zb = lambda base: pl.BlockSpec((tb, LANE), lambda h, i: (i, base // LANE + h))
    hb_ = pl.BlockSpec((tb, LANE), lambda h, i: (i, h))
    return pl.pallas_call(
        body, grid=(HEADS, nb),
        in_specs=[zb(ZC_Q), zb(ZC_K), zb(ZC_V), zb(ZC_R), hb_, pl.BlockSpec((None, 1, DV), lambda h, i: (h, 0, 0))],
        out_specs=[hb_, pl.BlockSpec((None, cpb, DV, DKP), lambda h, i: (h, i, 0, 0)), hb_],
        out_shape=[jax.ShapeDtypeStruct((S, D_GLA), F32), jax.ShapeDtypeStruct((HEADS, S // CHUNK, DV, DKP), F32),
                   jax.ShapeDtypeStruct((S, D_GLA), BF16)],
        scratch_shapes=[pltpu.VMEM((DV, DKP), F32)],
        compiler_params=_cp(("parallel", "arbitrary")), name=name,
    )(z, z, z, z, la, gn3)


def _gla_bwd(dycat, z, la, o_raw, sprev, gn3, name):
    S = z.shape[0]
    tb = _div(S, 512, CHUNK)
    cpb = tb // CHUNK
    nb = S // tb

    def body(q_ref, k_ref, v_ref, r_ref, la_ref, o_ref, sp_ref, dy_ref, gn_ref,
             dq_ref, dk_ref, dv_ref, dr_ref, dla_ref, dgn_ref, dst):
        @pl.when(pl.program_id(1) == 0)
        def _():
            dst[...] = jnp.zeros_like(dst)
            dgn_ref[...] = jnp.zeros_like(dgn_ref)

        ri = lax.broadcasted_iota(jnp.int32, (CHUNK, CHUNK), 0)
        ci = lax.broadcasted_iota(jnp.int32, (CHUNK, CHUNK), 1)
        tri = ri >= ci
        tril = tri.astype(F32)
        triu = (ri <= ci).astype(F32)
        gn = gn_ref[...]

        def chunk(cc, carry):
            c = cpb - 1 - cc
            rows = pl.ds(pl.multiple_of(c * CHUNK, CHUNK), CHUNK)
            q, k, v, lav = q_ref[rows, :], k_ref[rows, :], v_ref[rows, :], la_ref[rows, :]
            _, bend, eb, enb, ee, qf, qb, kb, kf, ke = _chunk_fwd_terms(q, k, lav, tril)
            att = jnp.where(tri, _dot_nt(qf, kb), _dot_nt(qb, kf))
            s_prev = sp_ref[c]
            gdec = jnp.exp(bend)
            o = o_ref[rows, :]
            rv = r_ref[rows, :]
            dy = dy_ref[rows, :]
            rms = lax.rsqrt(jnp.mean(o * o, axis=-1, keepdims=True) + EPS)
            oh = o * rms
            sg = _sigmoid(rv)
            sr = rv * sg
            dr_ref[rows, :] = (dy * oh * gn * (sg * (1.0 + rv * (1.0 - sg)))).astype(BF16)
            dgn_ref[...] += jnp.sum(dy * sr * oh, axis=0, keepdims=True)
            w = dy * sr * gn
            do = rms * (w - oh * jnp.mean(w * oh, axis=-1, keepdims=True))
            datt = _dot_nt(do, v)
            daf = jnp.where(tri, datt, 0.0)
            dab = jnp.where(tri, 0.0, datt)
            ds = dst[...]
            dv_ref[rows, :] = (_dot_tn(att, do) + _dot_nt(ke, ds)).astype(BF16)
            dke = _dot(v, ds)
            dqf = _dot(daf, kb) + _dot(do, s_prev)
            dkb = _dot_tn(daf, qf)
            dqb = _dot(dab, kf)
            dkf = _dot_tn(dab, qb)
            dg = jnp.sum(ds * s_prev, axis=0, keepdims=True)
            dst[...] = ds * gdec + _dot_tn(do, qf)
            dq_ref[rows, :] = ((dqf * eb + dqb * enb) * (DK ** -0.5)).astype(BF16)
            dk_ref[rows, :] = (dkb * enb + dkf * eb + dke * ee).astype(BF16)
            dbc = dqf * qf - dkb * kb - dqb * qb + dkf * kf - dke * ke
            dbend = jnp.sum(dke * ke, axis=0, keepdims=True) + dg * gdec
            dla_ref[rows, :] = _dot_exact(triu, dbc) + dbend
            return carry

        lax.fori_loop(0, cpb, chunk, 0)

    zb = lambda base: pl.BlockSpec((tb, LANE), lambda h, i: (nb - 1 - i, base // LANE + h))
    hb_ = pl.BlockSpec((tb, LANE), lambda h, i: (nb - 1 - i, h))
    bs = jax.ShapeDtypeStruct((S, D_GLA), BF16)
    return pl.pallas_call(
        body, grid=(HEADS, nb),
        in_specs=[zb(ZC_Q), zb(ZC_K), zb(ZC_V), zb(ZC_R), hb_, hb_,
                  pl.BlockSpec((None, cpb, DV, DKP), lambda h, i: (h, nb - 1 - i, 0, 0)),
                  pl.BlockSpec((tb, LANE), lambda h, i: (nb - 1 - i, D_CONV // LANE + h)),
                  pl.BlockSpec((None, 1, DV), lambda h, i: (h, 0, 0))],
        out_specs=[hb_, hb_, hb_, hb_, hb_, pl.BlockSpec((None, 1, DV), lambda h, i: (h, 0, 0))],
        out_shape=[bs, bs, bs, bs, jax.ShapeDtypeStruct((S, HEADS * DKP), F32), jax.ShapeDtypeStruct((HEADS, 1, DV), F32)],
        scratch_shapes=[pltpu.VMEM((DV, DKP), F32)],
        compiler_params=_cp(("parallel", "arbitrary")), name=name,
    )(z, z, z, z, la, o_raw, sprev, dycat, gn3)


def _mod_proj(c_all, w3, layer, b, name):
    B, D = c_all.shape
    N = w3.shape[2]
    tn = _div(N, 1024, LANE)

    def body(c_ref, w_ref, b_ref, o_ref):
        cv = c_ref[...]
        o_ref[...] = _dot(cv * _sigmoid(cv), w_ref[...]) + b_ref[...]

    return pl.pallas_call(
        body, grid=(N // tn,),
        in_specs=[pl.BlockSpec((B, D), lambda j: (0, 0)), pl.BlockSpec((None, D, tn), lambda j: (layer, 0, j)), pl.BlockSpec((1, tn), lambda j: (0, j))],
        out_specs=pl.BlockSpec((B, tn), lambda j: (0, j)), out_shape=jax.ShapeDtypeStruct((B, N), F32),
        compiler_params=_cp(("parallel",)), name=name,
    )(c_all, w3, b)


def _mod_wgrad(c_t, dm, name):
    D, B = c_t.shape
    N = dm.shape[1]
    tn = _div(N, 1024, LANE)

    def body(c_ref, d_ref, o_ref):
        cv = c_ref[...]
        ca = cv * _sigmoid(cv)
        acc = ca[:, 0:1] * d_ref[pl.ds(0, 1), :]
        for b in range(1, B):
            acc = acc + ca[:, b:b + 1] * d_ref[pl.ds(b, 1), :]
        o_ref[...] = acc

    return pl.pallas_call(
        body, grid=(N // tn,),
        in_specs=[pl.BlockSpec((D, B), lambda j: (0, 0)), pl.BlockSpec((B, tn), lambda j: (0, j))],
        out_specs=pl.BlockSpec((D, tn), lambda j: (0, j)), out_shape=jax.ShapeDtypeStruct((D, N), F32),
        compiler_params=_cp(("parallel",)), name=name,
    )(c_t, dm)


def _rowsum(xs, name):
    n, N = xs.shape
    tn = _div(N, 8192, LANE)

    def body(x_ref, o_ref):
        acc = x_ref[pl.ds(0, 1), :]
        for r in range(1, n):
            acc = acc + x_ref[pl.ds(r, 1), :]
        o_ref[...] = acc

    return pl.pallas_call(
        body, grid=(N // tn,), in_specs=[pl.BlockSpec((n, tn), lambda j: (0, j))],
        out_specs=pl.BlockSpec((1, tn), lambda j: (0, j)), out_shape=jax.ShapeDtypeStruct((1, N), F32),
        compiler_params=_cp(("parallel",)), name=name,
    )(xs)


def _adamw(w, g, m, v, name):
    R, C = w.shape
    tr = _div(R, max(8, (1 << 18) // C), 8)

    def body(w_ref, g_ref, m_ref, v_ref, d_ref, nm_ref, nv_ref):
        gv = g_ref[...]
        mn = ADAM_B1 * m_ref[...] + (1.0 - ADAM_B1) * gv
        vn = ADAM_B2 * v_ref[...] + (1.0 - ADAM_B2) * (gv * gv)
        m_hat = mn / (1.0 - ADAM_B1 ** ADAM_STEP)
        v_hat = vn / (1.0 - ADAM_B2 ** ADAM_STEP)
        d_ref[...] = -ADAM_LR * (m_hat / (jnp.sqrt(v_hat) + ADAM_EPS) + ADAM_WD * w_ref[...])
        nm_ref[...] = mn
        nv_ref[...] = vn

    blk = pl.BlockSpec((tr, C), lambda i: (i, 0))
    os_ = jax.ShapeDtypeStruct((R, C), F32)
    return pl.pallas_call(
        body, grid=(R // tr,), in_specs=[blk] * 4, out_specs=[blk] * 3, out_shape=[os_] * 3,
        compiler_params=_cp(("parallel",)), name=name,
    )(w, g, m, v)


def _sum_partials(a, b, r, name):
    R, C = a.shape
    tr = _div(R, 1024, 16)

    def body(a_ref, b_ref, r_ref, o_ref):
        acc = a_ref[...] + b_ref[...]
        for j in range(r.shape[0]):
            acc = acc + r_ref[j].astype(F32)
        o_ref[...] = acc

    blk = pl.BlockSpec((tr, C), lambda i: (i, 0))
    return pl.pallas_call(
        body, grid=(R // tr,), in_specs=[blk, blk, pl.BlockSpec((r.shape[0], tr, C), lambda i: (0, i, 0))],
        out_specs=blk, out_shape=jax.ShapeDtypeStruct((R, C), F32), compiler_params=_cp(("parallel",)), name=name,
    )(a, b, r)


def _add_cast(a, b, name):
    n, R, C = a.shape
    tr = _div(R, 1024, 16)

    def body(a_ref, b_ref, o_ref):
        o_ref[...] = (a_ref[...] + b_ref[...]).astype(BF16)

    blk = pl.BlockSpec((None, tr, C), lambda s, i: (s, i, 0))
    return pl.pallas_call(
        body, grid=(n, R // tr), in_specs=[blk, blk], out_specs=blk, out_shape=jax.ShapeDtypeStruct((n, R, C), BF16),
        compiler_params=_cp(("parallel", "parallel")), name=name,
    )(a, b)


def _place():
    return lax.axis_index("x"), lax.axis_index("y"), lax.axis_index("c")


def _other_chips(x, y):
    return [(1 - x, y), (x, 1 - y), (1 - x, 1 - y)]


def _ag_small(v, name):
    r, n = v.shape

    def body(v_ref, o_ref, send_sems, recv_sems):
        x, y, c = _place()
        me = 4 * x + 2 * y + c
        o_ref[pl.ds(me, 1)] = v_ref[...][None]
        peers = [(x ^ (k >> 2), y ^ ((k >> 1) & 1), c ^ (k & 1)) for k in range(1, 8)]
        copies = []
        for k, peer in enumerate(peers):
            cp = pltpu.make_async_remote_copy(
                src_ref=v_ref, dst_ref=o_ref.at[me], send_sem=send_sems.at[k], recv_sem=recv_sems.at[k],
                device_id=peer, device_id_type=MESH)
            cp.start()
            copies.append(cp)
        for cp in copies:
            cp.wait()

    return pl.pallas_call(
        body, out_shape=jax.ShapeDtypeStruct((8, r, n), v.dtype),
        in_specs=[pl.BlockSpec(memory_space=pltpu.VMEM)], out_specs=pl.BlockSpec(memory_space=pltpu.VMEM),
        scratch_shapes=[pltpu.SemaphoreType.DMA((7,)), pltpu.SemaphoreType.DMA((7,))],
        compiler_params=pltpu.CompilerParams(vmem_limit_bytes=VMEM_LIMIT), name=name,
    )(v)


def _ag_weights(pack, name):
    _, R, C = pack.shape

    def body(p_ref, o_ref, send_sems, recv_sems, local_sem):
        x, y, c = _place()
        s_me = 2 * x + y
        chips = _other_chips(x, y)
        sibling = (x, y, 1 - c)
        mine = pltpu.make_async_copy(p_ref, o_ref.at[s_me], local_sem)
        mine.start()

        def copy(k, s, h, to, src=None):
            blk = o_ref.at[s, h]
            return pltpu.make_async_remote_copy(
                src_ref=blk if src is None else src, dst_ref=blk, send_sem=send_sems.at[k], recv_sem=recv_sems.at[k],
                device_id=to, device_id_type=MESH)

        first = [copy(j, s_me, c, (*chip, c), src=p_ref.at[c]) for j, chip in enumerate(chips)]
        for cp in first:
            cp.start()
        passed = []
        for j, (cx, cy) in enumerate(chips):
            s_j = 2 * cx + cy
            copy(j, s_j, c, sibling).wait_recv()
            fw = copy(3 + j, s_j, c, sibling)
            fw.start()
            passed.append(fw)
        for j, (cx, cy) in enumerate(chips):
            copy(3 + j, 2 * cx + cy, 1 - c, sibling).wait_recv()
        for cp in first + passed:
            cp.wait_send()
        mine.wait()

    return pl.pallas_call(
        body, out_shape=jax.ShapeDtypeStruct((4, 2, R, C), pack.dtype),
        in_specs=[pl.BlockSpec(memory_space=pl.ANY)], out_specs=pl.BlockSpec(memory_space=pl.ANY),
        scratch_shapes=[pltpu.SemaphoreType.DMA((6,)), pltpu.SemaphoreType.DMA((6,)), pltpu.SemaphoreType.DMA],
        compiler_params=pltpu.CompilerParams(has_side_effects=True), name=name,
    )(pack)


def _to_sibling(v, name):
    def body(v_ref, o_ref, send_sem, recv_sem):
        x, y, c = _place()
        cp = pltpu.make_async_remote_copy(
            src_ref=v_ref, dst_ref=o_ref, send_sem=send_sem, recv_sem=recv_sem, device_id=(x, y, 1 - c), device_id_type=MESH)
        cp.start()
        cp.wait()

    return pl.pallas_call(
        body, out_shape=jax.ShapeDtypeStruct(v.shape, v.dtype),
        in_specs=[pl.BlockSpec(memory_space=pl.ANY)], out_specs=pl.BlockSpec(memory_space=pl.ANY),
        scratch_shapes=[pltpu.SemaphoreType.DMA, pltpu.SemaphoreType.DMA],
        compiler_params=pltpu.CompilerParams(has_side_effects=True), name=name,
    )(v)


def _to_chips(p, name):
    _, R, C = p.shape

    def body(p_ref, o_ref, send_sems, recv_sems):
        x, y, c = _place()
        copies = []
        for j, (cx, cy) in enumerate(_other_chips(x, y)):
            cp = pltpu.make_async_remote_copy(
                src_ref=p_ref.at[2 * cx + cy], dst_ref=o_ref.at[j], send_sem=send_sems.at[j], recv_sem=recv_sems.at[j],
                device_id=(cx, cy, c), device_id_type=MESH)
            cp.start()
            copies.append(cp)
        for cp in copies:
            cp.wait()

    return pl.pallas_call(
        body, out_shape=jax.ShapeDtypeStruct((3, R, C), p.dtype),
        in_specs=[pl.BlockSpec(memory_space=pl.ANY)], out_specs=pl.BlockSpec(memory_space=pl.ANY),
        scratch_shapes=[pltpu.SemaphoreType.DMA((3,)), pltpu.SemaphoreType.DMA((3,))],
        compiler_params=pltpu.CompilerParams(has_side_effects=True), name=name,
    )(p)


BIG = (("w_ffn1_in", 2), ("w_ffn1_out", 1), ("w_in", 2), ("w_out", 1), ("w_ffn2_in", 2), ("w_ffn2_out", 1))
PACK_UNIT = 2 * 16 * LANE


def _pack(pieces, dtype):
    flat = jnp.concatenate([p.astype(dtype).reshape(-1) for p in pieces])
    pad = (-flat.shape[0]) % PACK_UNIT
    if pad:
        flat = jnp.concatenate([flat, jnp.zeros((pad,), dtype)])
    return flat.reshape(2, -1, LANE)


def _unpack(flat, shapes):
    out, off = [], 0
    for shp in shapes:
        n = 1
        for d in shp:
            n *= d
        out.append(flat[off:off + n].reshape(shp))
        off += n
    return out


def _pad_heads(w, axis_cols=True):
    lead = w.shape[:-1]
    w4 = w.reshape(*lead, HEADS, DK)
    w4 = jnp.pad(w4, [(0, 0)] * len(lead) + [(0, 0), (0, DKP - DK)])
    return w4.reshape(*lead, HEADS * DKP)


def _unpad_heads(w):
    lead = w.shape[:-1]
    return w.reshape(*lead, HEADS, DKP)[..., :DK].reshape(*lead, HEADS * DK)


def _mix_weight(w_in):
    o = 2 * D_CONV
    ab = w_in[:, :o]
    q = _pad_heads(w_in[:, o:o + HEADS * DK])
    k = _pad_heads(w_in[:, o + HEADS * DK:o + 2 * HEADS * DK])
    o2 = o + 2 * HEADS * DK
    vr = w_in[:, o2:o2 + 2 * D_GLA]
    glr = jnp.pad(w_in[:, o2 + 2 * D_GLA:], ((0, 0), (0, LANE - GATE_RANK)))
    return jnp.concatenate([ab, q, k, vr, glr], axis=1)


def _mix_weight_grad(dw):
    return jnp.concatenate([
        dw[:, :ZC_Q], _unpad_heads(dw[:, ZC_Q:ZC_K]), _unpad_heads(dw[:, ZC_K:ZC_V]), dw[:, ZC_V:ZC_G], dw[:, ZC_G:ZC_G + GATE_RANK]], axis=1)


_ARG_NAMES = ['x', 'c', 'w_ada', 'b_ada', 'g_norm_ffn1', 'w_ffn1_in', 'w_ffn1_out', 'g_norm_mix', 'w_in', 'w_dw', 'b_dw', 'g_conv_ln', 'b_conv_ln', 'w_gate_up', 'b_gate', 'g_gla_norm', 'w_out', 'g_norm_ffn2', 'w_ffn2_in', 'w_ffn2_out', 'g_norm_final', 'w_ada_final', 'b_ada_final']
_WEIGHTS = _ARG_NAMES[2:]
_SHARDED_AXIS = {'w_ada': 2, 'w_ffn1_in': 2, 'w_ffn1_out': 1, 'w_in': 2, 'w_dw': 2, 'w_gate_up': 2, 'w_out': 1, 'w_ffn2_in': 2, 'w_ffn2_out': 1, 'w_ada_final': 1}
_RS_GRADS = tuple(n for n, _ in BIG)
_SMALL = ('g_norm_ffn1', 'g_norm_mix', 'w_dw', 'b_dw', 'g_conv_ln', 'b_conv_ln', 'w_gate_up', 'b_gate', 'g_gla_norm', 'g_norm_ffn2', 'g_norm_final')


def _ffn_fwd(x, g, shift, scale, gv, wg, wu, wo, tag):
    h = _normmod(x, g, shift, scale, f"normmod_{tag}")
    gate, up, act = _ffn_up(h, wg, wu, f"ffn_up_{tag}")
    y, xn = _mm_resid(act, wo, x, gv, f"ffn_down_{tag}")
    return xn, (x, h, gate, up, act, y)


def _ffn_bwd(dxn, saved, g, scale, gv, wg_t, wu_t, wo_t, tag):
    x, h, gate, up, act, y = saved
    dy, dgv = _resid_bwd(dxn, y, gv, f"resid_bwd_{tag}")
    dzg, dzu = _ffn_bwd_act(dy, wo_t, gate, up, f"ffn_bwd_act_{tag}")
    dwo = _mm_tn(act, dy, f"dw_out_{tag}")
    dh = _mm([(dzg, wg_t), (dzu, wu_t)], F32, f"ffn_dh_{tag}")
    dwg = _mm_tn(h, dzg, f"dw_gate_{tag}")
    dwu = _mm_tn(h, dzu, f"dw_up_{tag}")
    dx, dsh, dsc, dg = _normmod_bwd(x, dh, dxn, g, scale, f"normmod_bwd_{tag}")
    return dx, dict(dshift=dsh, dscale=dsc, dgv=dgv, dg=dg, dw_in=jnp.concatenate([dwg, dwu], axis=1), dw_out=dwo)


def _mix_fwd(x, g, shift, scale, gv, wmix, w_dw, b_dw, g_ln, b_ln, wgp, bgp, gn3, wout, tag):
    h = _normmod(x, g, shift, scale, f"normmod_{tag}")
    z = _mm([(h, wmix)], F32, f"mix_in_{tag}")
    u, yc, yconv = _conv_fwd(z, w_dw, b_dw, g_ln, b_ln, f"conv_fwd_{tag}")
    la = _loggate(z, wgp, bgp, f"loggate_{tag}")
    o_raw, sprev, ygla = _gla_fwd(z, la, gn3, f"gla_fwd_{tag}")
    ycat = jnp.concatenate([yconv, ygla], axis=1)
    y, xn = _mm_resid(ycat, wout, x, gv, f"mix_out_{tag}")
    return xn, (x, h, z, u, yc, la, o_raw, sprev, ycat, y)


def _mix_bwd(dxn, saved, g, scale, gv, wmix_t, w_dw, g_ln, b_ln, wgp, wgp_t, bgp, gn3, wout_t, tag):
    x, h, z, u, yc, la, o_raw, sprev, ycat, y = saved
    dy, dgv = _resid_bwd(dxn, y, gv, f"resid_bwd_{tag}")
    dycat = _mm([(dy, wout_t)], F32, f"mix_dycat_{tag}")
    dwout = _mm_tn(ycat, dy, f"dw_mixout_{tag}")
    da, db, dwdw, dbdw, dgln, dbln = _conv_bwd(dycat, z, u, yc, w_dw, g_ln, b_ln, f"conv_bwd_{tag}")
    dq, dk, dv, dr, dla, dgn = _gla_bwd(dycat, z, la, o_raw, sprev, gn3, f"gla_bwd_{tag}")
    dglr, dwgp, dbgp = _loggate_bwd(dla, z, wgp, wgp_t, bgp, f"loggate_bwd_{tag}")
    dz = jnp.concatenate([da, db, dq, dk, dv, dr, dglr], axis=1)
    dh = _mm([(dz, wmix_t)], F32, f"mix_dh_{tag}")
    dwmix = _mm_tn(h, dz, f"dw_mixin_{tag}")
    dx, dsh, dsc, dg = _normmod_bwd(x, dh, dxn, g, scale, f"normmod_bwd_{tag}")
    grads = dict(dshift=dsh, dscale=dsc, dgv=dgv, dg=dg, dw_in=_mix_weight_grad(dwmix), dw_out=dwout, dw_dw=dwdw, db_dw=dbdw,
                 dg_ln=dgln, db_ln=dbln, dw_gate=_unpad_heads(dwgp[:GATE_RANK]), db_gate=_unpad_heads(dbgp)[0], dgn=dgn[:, 0, :])
    return dx, grads


def kernel(x, c, w_ada, b_ada, g_norm_ffn1, w_ffn1_in, w_ffn1_out, g_norm_mix, w_in, w_dw, b_dw, g_conv_ln, b_conv_ln, w_gate_up, b_gate, g_gla_norm, w_out, g_norm_ffn2, w_ffn2_in, w_ffn2_out, g_norm_final, w_ada_final, b_ada_final, loss_target, m_w_ada, m_b_ada, m_g_norm_ffn1, m_w_ffn1_in, m_w_ffn1_out, m_g_norm_mix, m_w_in, m_w_dw, m_b_dw, m_g_conv_ln, m_b_conv_ln, m_w_gate_up, m_b_gate, m_g_gla_norm, m_w_out, m_g_norm_ffn2, m_w_ffn2_in, m_w_ffn2_out, m_g_norm_final, m_w_ada_final, m_b_ada_final, v_w_ada, v_b_ada, v_g_norm_ffn1, v_w_ffn1_in, v_w_ffn1_out, v_g_norm_mix, v_w_in, v_w_dw, v_b_dw, v_g_conv_ln, v_b_conv_ln, v_w_gate_up, v_b_gate, v_g_gla_norm, v_w_out, v_g_norm_ffn2, v_w_ffn2_in, v_w_ffn2_out, v_g_norm_final, v_w_ada_final, v_b_ada_final):
    given = dict(locals())
    W = {n: given[n] for n in _WEIGHTS}
    M1 = {n: given["m_" + n] for n in _WEIGHTS}
    M2 = {n: given["v_" + n] for n in _WEIGHTS}
    xs = x[0]
    tgt = loss_target[0]
    S, D = xs.shape
    L = w_ada.shape[0]
    Fd = w_ffn1_out.shape[1] * 4
    xi, yi, ci = _place()
    s_me = 2 * xi + yi
    b_me = 4 * xi + 2 * yi + ci
    nsh = w_ada.shape[2]
    nfin = w_ada_final.shape[1]

    c_all = _ag_small(c.reshape(8, D // 8), "ag_c").reshape(8, D)
    parts = [_mod_proj(c_all, w_ada, l, lax.dynamic_slice(b_ada, (l, s_me * nsh), (1, nsh)), f"mod_proj_{l}") for l in range(L)]
    parts.append(_mod_proj(c_all, w_ada_final[None], 0, lax.dynamic_slice(b_ada_final, (s_me * nfin,), (nfin,))[None], "mod_proj_final"))
    mod_all = _ag_small(jnp.concatenate(parts, axis=1), "ag_mod")
    mine = [lax.dynamic_index_in_dim(lax.dynamic_index_in_dim(mod_all, 2 * s + ci, 0, False), b_me, 0, False) for s in range(4)]
    mods = [jnp.concatenate([mine[s][l * nsh:(l + 1) * nsh] for s in range(4)]).reshape(N_MOD, 1, D) for l in range(L)]
    fmod = jnp.concatenate([mine[s][L * nsh:] for s in range(4)]).reshape(2, 1, D)

    gathered = _ag_weights(_pack([W[n] for n, _ in BIG], BF16), "ag_weights").reshape(4, -1)
    shard_shapes = [W[n].shape for n, _ in BIG]
    per_shard = [_unpack(gathered[s], shard_shapes) for s in range(4)]
    full = {n: jnp.concatenate([per_shard[s][i] for s in range(4)], axis=ax) for i, (n, ax) in enumerate(BIG)}

    gn3 = g_gla_norm.reshape(L, HEADS, 1, DV)

    tiny = jnp.concatenate([w_dw.reshape(-1), w_gate_up.reshape(-1)])
    tiny_pad = (-tiny.shape[0]) % (8 * LANE)
    tiny_all = _ag_small(jnp.pad(tiny, (0, tiny_pad)).reshape(8, -1), "ag_tiny").reshape(8, -1)
    n_dw = w_dw.size
    dw_parts = [lax.dynamic_index_in_dim(tiny_all, 2 * s + ci, 0, False) for s in range(4)]
    w_dw_full = jnp.concatenate([p[:n_dw].reshape(w_dw.shape) for p in dw_parts], axis=2)
    w_gu_full = jnp.concatenate([p[n_dw:n_dw + w_gate_up.size].reshape(w_gate_up.shape) for p in dw_parts], axis=2)

    saved = []
    lw = []
    xcur = xs
    for l in range(L):
        md = mods[l]
        w1 = full["w_ffn1_in"][l]
        w2 = full["w_ffn2_in"][l]
        wmix = _mix_weight(full["w_in"][l])
        wgp_l = jnp.pad(_pad_heads(w_gu_full[l]), ((0, LANE - GATE_RANK), (0, 0))).astype(BF16)
        bgp_l = _pad_heads(b_gate[l])[None]
        d = dict(
            wg1=w1[:, :Fd], wu1=w1[:, Fd:], wo1=full["w_ffn1_out"][l], wg2=w2[:, :Fd], wu2=w2[:, Fd:], wo2=full["w_ffn2_out"][l],
            wmix=wmix, wout=full["w_out"][l], wgp=wgp_l, bgp=bgp_l, gv1=0.5 * md[2], gv2=md[5], gv3=0.5 * md[8])
        lw.append(d)
        xcur, s1 = _ffn_fwd(xcur, g_norm_ffn1[l][None], md[0], md[1], d["gv1"], d["wg1"], d["wu1"], d["wo1"], f"ffn1_l{l}")
        xcur, s2 = _mix_fwd(xcur, g_norm_mix[l][None], md[3], md[4], d["gv2"], wmix, w_dw_full[l], b_dw[l][None], g_conv_ln[l][None],
                            b_conv_ln[l][None], wgp_l, bgp_l, gn3[l], d["wout"], f"mix_l{l}")
        xcur, s3 = _ffn_fwd(xcur, g_norm_ffn2[l][None], md[6], md[7], d["gv3"], d["wg2"], d["wu2"], d["wo2"], f"ffn2_l{l}")
        saved.append((s1, s2, s3))

    dh, sq = _final_loss(xcur, g_norm_final[None], fmod[0], fmod[1], tgt)
    loss_part = 0.5 / D * jnp.sum(sq)
    dx, dfsh, dfsc, dgfin = _normmod_bwd(xcur, dh, None, g_norm_final[None], fmod[1], "normmod_bwd_final")
    G = {n: [None] * L for n in _WEIGHTS}
    dmods = [None] * L
    for l in reversed(range(L)):
        md, d = mods[l], lw[l]
        s1, s2, s3 = saved[l]
        dx, g3 = _ffn_bwd(dx, s3, g_norm_ffn2[l][None], md[7], d["gv3"], d["wg2"].T, d["wu2"].T, d["wo2"].T, f"ffn2_l{l}")
        dx, g2 = _mix_bwd(dx, s2, g_norm_mix[l][None], md[4], d["gv2"], d["wmix"].T, w_dw_full[l], g_conv_ln[l][None], b_conv_ln[l][None],
                          d["wgp"], d["wgp"].T, d["bgp"], gn3[l], d["wout"].T, f"mix_l{l}")
        dx, g1 = _ffn_bwd(dx, s1, g_norm_ffn1[l][None], md[1], d["gv1"], d["wg1"].T, d["wu1"].T, d["wo1"].T, f"ffn1_l{l}")
        dmods[l] = jnp.concatenate([g1["dshift"], g1["dscale"], 0.5 * g1["dgv"], g2["dshift"], g2["dscale"], g2["dgv"],
                                    g3["dshift"], g3["dscale"], 0.5 * g3["dgv"]], axis=1)[0]
        G["g_norm_ffn1"][l], G["w_ffn1_in"][l], G["w_ffn1_out"][l] = g1["dg"][0], g1["dw_in"], g1["dw_out"]
        G["g_norm_ffn2"][l], G["w_ffn2_in"][l], G["w_ffn2_out"][l] = g3["dg"][0], g3["dw_in"], g3["dw_out"]
        G["g_norm_mix"][l], G["w_in"][l], G["w_out"][l] = g2["dg"][0], g2["dw_in"], g2["dw_out"]
        G["w_dw"][l], G["b_dw"][l], G["g_conv_ln"][l], G["b_conv_ln"][l] = g2["dw_dw"], g2["db_dw"][0], g2["dg_ln"][0], g2["db_ln"][0]
        G["w_gate_up"][l], G["b_gate"][l], G["g_gla_norm"][l] = g2["dw_gate"], g2["db_gate"], g2["dgn"]
    grad_x = dx[None]

    small = [jnp.stack(G[n]).reshape(-1) for n in _SMALL if n != 'g_norm_final'] + [dgfin[0]]
    dmod_vec = jnp.concatenate(dmods + [dfsh[0], dfsc[0]])
    n_mod_vec = dmod_vec.shape[0]
    vec = jnp.concatenate([dmod_vec] + small + [loss_part[None]])
    n_vec = vec.shape[0]
    vec = jnp.pad(vec, (0, (-n_vec) % (8 * LANE)))
    vec_all = _ag_small(vec.reshape(8, -1), "ag_small_grads").reshape(8, -1)
    vec_sum = _rowsum(vec_all, "sum_small_grads")[0]
    loss = vec_sum[n_vec - 1]
    gsm, off = {}, n_mod_vec
    for n in _SMALL:
        if n == 'g_norm_final':
            shp = W[n].shape
        elif n == 'w_dw':
            shp = w_dw_full.shape
        elif n == 'w_gate_up':
            shp = w_gu_full.shape
        else:
            shp = W[n].shape
        cnt = 1
        for dd in shp:
            cnt *= dd
        gsm[n] = vec_sum[off:off + cnt].reshape(shp)
        off += cnt
    gsm['w_dw'] = lax.dynamic_slice_in_dim(gsm['w_dw'], s_me * w_dw.shape[2], w_dw.shape[2], 2)
    gsm['w_gate_up'] = lax.dynamic_slice_in_dim(gsm['w_gate_up'], s_me * w_gate_up.shape[2], w_gate_up.shape[2], 2)
    dmod_sum = vec_sum[:n_mod_vec]
    gsm['b_ada'] = dmod_sum[:L * N_MOD * D].reshape(L, N_MOD * D)
    gsm['b_ada_final'] = dmod_sum[L * N_MOD * D:]
    c_t = c_all.T
    dmod_rows = vec_all[:, :n_mod_vec]
    gsm['w_ada'] = jnp.stack([
        _mod_wgrad(c_t, lax.dynamic_slice_in_dim(dmod_rows, l * N_MOD * D + s_me * nsh, nsh, 1), f"dw_ada_{l}") for l in range(L)])
    gsm['w_ada_final'] = _mod_wgrad(c_t, lax.dynamic_slice_in_dim(dmod_rows, L * N_MOD * D + s_me * nfin, nfin, 1), "dw_ada_final")

    def shard_piece(gfull, ax, s):
        n = gfull.shape[ax] // 4
        return lax.slice_in_dim(gfull, s * n, (s + 1) * n, axis=ax)

    gfulls = [jnp.stack(G[n]) for n, _ in BIG]
    gp = jnp.stack([_pack([shard_piece(gf, ax, s) for gf, (_, ax) in zip(gfulls, BIG)], F32) for s in range(4)])
    g_mine = lax.dynamic_index_in_dim(gp, ci, 1, False)
    g_sib = lax.dynamic_index_in_dim(gp, 1 - ci, 1, False)
    from_sib = _to_sibling(g_sib, "rs_sibling")
    own_a = lax.dynamic_index_in_dim(g_mine, s_me, 0, False)
    own_b = lax.dynamic_index_in_dim(from_sib, s_me, 0, False)
    recv = _to_chips(_add_cast(g_mine, from_sib, "rs_presum"), "rs_chips")
    half = _sum_partials(own_a, own_b, recv, "rs_sum")
    other = _to_sibling(half, "rs_share")
    both = jnp.where(ci == 0, jnp.stack([half, other]), jnp.stack([other, half])).reshape(-1)
    for (n, _), gsh in zip(BIG, _unpack(both, shard_shapes)):
        gsm[n] = gsh

    outs = {}
    small_names = [n for n in _WEIGHTS if W[n].size < 65536]
    for n in _WEIGHTS:
        if n in small_names:
            continue
        shp = W[n].shape
        v2 = lambda a: a.reshape(-1, shp[-1])
        d_, m_, v_ = _adamw(v2(W[n]), v2(gsm[n]), v2(M1[n]), v2(M2[n]), f"adamw_{n}")
        outs[n] = (d_.reshape(shp), m_.reshape(shp), v_.reshape(shp))
    flat = lambda dct: jnp.concatenate([dct[n].reshape(-1) for n in small_names])
    n_small = sum(W[n].size for n in small_names)
    padn = (-n_small) % (8 * LANE)
    v2 = lambda a: jnp.pad(a, (0, padn)).reshape(-1, LANE)
    d_, m_, v_ = _adamw(v2(flat(W)), v2(flat(gsm)), v2(flat(M1)), v2(flat(M2)), "adamw_small")
    shapes_small = [W[n].shape for n in small_names]
    for n, dd, mm, vv in zip(small_names, _unpack(d_.reshape(-1), shapes_small), _unpack(m_.reshape(-1), shapes_small), _unpack(v_.reshape(-1), shapes_small)):
        outs[n] = (dd, mm, vv)

    return (loss, grad_x, *[gsm[n] for n in _WEIGHTS], *[outs[n][0] for n in _WEIGHTS], *[outs[n][1] for n in _WEIGHTS], *[outs[n][2] for n in _WEIGHTS])
```

```python
import jax
import jax.numpy as jnp
from jax import lax
from jax.experimental import pallas as pl
from jax.experimental.pallas import tpu as pltpu

F32 = jnp.float32
BF16 = jnp.bfloat16

CHUNK = 64
HEADS = 4
DK = 64
DV = 128
DKP = 128
GATE_RANK = 16
GATE_TAU = 16.0
N_MOD = 9
EPS = 1e-6
ADAM_LR = 0.001
ADAM_B1 = 0.9
ADAM_B2 = 0.999
ADAM_EPS = 1e-08
ADAM_WD = 0.01
ADAM_STEP = 10

LANE = 128
HALO = 32
VMEM_LIMIT = 52 * 1024 * 1024
MESH = pl.DeviceIdType.MESH
N_CHIPS = 4

D_CONV = 512
D_GLA = HEADS * DV
ZC_Q = 0
ZC_K = ZC_Q + HEADS * DKP
ZC_V = ZC_K + HEADS * DKP
ZC_R = ZC_V + D_GLA
ZC_A = ZC_R + D_GLA
ZC_B = ZC_A + D_CONV
ZC_G = ZC_B + D_CONV
Z_COLS = ZC_G + LANE
Z_GLA = ZC_A


def _div(n, target, mult):
    best = None
    d = mult
    while d <= min(n, target):
        if n % d == 0:
            best = d
        d += mult
    return n if best is None else best


def _cp(sem=None, **kw):
    return pltpu.CompilerParams(dimension_semantics=sem, vmem_limit_bytes=VMEM_LIMIT, **kw)


def _resident(shape, index_map):
    return pl.BlockSpec(shape, index_map, pipeline_mode=pl.Buffered(1))


def _sigmoid(x):
    return 0.5 * jnp.tanh(0.5 * x) + 0.5


def _dot(a, b):
    return jnp.dot(a.astype(BF16), b.astype(BF16), preferred_element_type=F32)


def _dot_nt(a, b):
    return lax.dot_general(a.astype(BF16), b.astype(BF16), (((1,), (1,)), ((), ())), preferred_element_type=F32)


def _dot_tn(a, b):
    return lax.dot_general(a.astype(BF16), b.astype(BF16), (((0,), (0,)), ((), ())), preferred_element_type=F32)


def _dot_exact(a, b):
    return jnp.dot(a, b, preferred_element_type=F32, precision=lax.Precision.HIGHEST)


def _normmod(x, g, shift, scale, name):
    S, D = x.shape
    tm = _div(S, 512, 8)

    def body(x_ref, g_ref, sh_ref, sc_ref, o_ref):
        xv = x_ref[...]
        r = lax.rsqrt(jnp.mean(xv * xv, axis=-1, keepdims=True) + EPS)
        o_ref[...] = ((xv * r) * g_ref[...] * (1.0 + sc_ref[...]) + sh_ref[...]).astype(o_ref.dtype)

    row = pl.BlockSpec((tm, D), lambda i: (i, 0))
    vec = pl.BlockSpec((1, D), lambda i: (0, 0))
    return pl.pallas_call(
        body, grid=(S // tm,), in_specs=[row, vec, vec, vec], out_specs=row,
        out_shape=jax.ShapeDtypeStruct((S, D), BF16), compiler_params=_cp(("parallel",)), name=name,
    )(x, g, shift, scale)


def _final_loss(x, g, shift, scale, tgt):
    S, D = x.shape
    tm = _div(S, 512, 8)

    def body(x_ref, g_ref, sh_ref, sc_ref, t_ref, dh_ref, sq_ref):
        @pl.when(pl.program_id(0) == 0)
        def _():
            sq_ref[...] = jnp.zeros_like(sq_ref)

        xv = x_ref[...]
        r = lax.rsqrt(jnp.mean(xv * xv, axis=-1, keepdims=True) + EPS)
        h = (xv * r) * g_ref[...] * (1.0 + sc_ref[...]) + sh_ref[...]
        e = h - t_ref[...]
        dh_ref[...] = e * (1.0 / D)
        sq_ref[...] += jnp.sum(e * e, axis=0, keepdims=True)

    row = pl.BlockSpec((tm, D), lambda i: (i, 0))
    vec = pl.BlockSpec((1, D), lambda i: (0, 0))
    return pl.pallas_call(
        body, grid=(S // tm,), in_specs=[row, vec, vec, vec, row], out_specs=[row, vec],
        out_shape=[jax.ShapeDtypeStruct((S, D), F32), jax.ShapeDtypeStruct((1, D), F32)],
        compiler_params=_cp(("arbitrary",)), name="final_loss",
    )(x, g, shift, scale, tgt)


def _normmod_bwd(x, dh, dres, g, scale, name):
    S, D = x.shape
    tm = _div(S, 512, 8)
    with_res = dres is not None

    def body(*refs):
        if with_res:
            x_ref, dh_ref, dr_ref, g_ref, sc_ref, dx_ref, dsh_ref, dsc_ref, dg_ref = refs
        else:
            x_ref, dh_ref, g_ref, sc_ref, dx_ref, dsh_ref, dsc_ref, dg_ref = refs

        @pl.when(pl.program_id(0) == 0)
        def _():
            dsh_ref[...] = jnp.zeros_like(dsh_ref)
            dsc_ref[...] = jnp.zeros_like(dsc_ref)
            dg_ref[...] = jnp.zeros_like(dg_ref)

        xv = x_ref[...]
        dh = dh_ref[...].astype(F32)
        gv = g_ref[...]
        r = lax.rsqrt(jnp.mean(xv * xv, axis=-1, keepdims=True) + EPS)
        xh = xv * r
        dsh_ref[...] += jnp.sum(dh, axis=0, keepdims=True)
        dsc_ref[...] += jnp.sum(dh * (xh * gv), axis=0, keepdims=True)
        dn = dh * (1.0 + sc_ref[...])
        dg_ref[...] += jnp.sum(dn * xh, axis=0, keepdims=True)
        dxh = dn * gv
        dx = r * (dxh - xh * jnp.mean(dxh * xh, axis=-1, keepdims=True))
        if with_res:
            dx = dx + dr_ref[...]
        dx_ref[...] = dx

    row = pl.BlockSpec((tm, D), lambda i: (i, 0))
    vec = pl.BlockSpec((1, D), lambda i: (0, 0))
    ins = [row, row, row, vec, vec] if with_res else [row, row, vec, vec]
    args = (x, dh, dres, g, scale) if with_res else (x, dh, g, scale)
    vs = jax.ShapeDtypeStruct((1, D), F32)
    return pl.pallas_call(
        body, grid=(S // tm,), in_specs=ins, out_specs=[row, vec, vec, vec],
        out_shape=[jax.ShapeDtypeStruct((S, D), F32), vs, vs, vs],
        compiler_params=_cp(("arbitrary",)), name=name,
    )(*args)


def _mm(pairs, out_dtype, name, nt=False):
    M = pairs[0][0].shape[0]
    N = pairs[0][1].shape[0] if nt else pairs[0][1].shape[1]
    ktot = sum(a.shape[1] for a, _, _ in pairs)
    tm = _div(M, 512 if ktot <= 4096 else 256, 8)
    n = len(pairs)

    def body(*refs):
        o_ref = refs[2 * n]
        dot = _dot_nt if nt else _dot
        acc = dot(refs[0][...], refs[1][...])
        for p in range(1, n):
            acc = acc + dot(refs[2 * p][...], refs[2 * p + 1][...])
        o_ref[...] = acc.astype(o_ref.dtype)

    ins, args = [], []
    for a, b, blk in pairs:
        k = a.shape[1]
        ins.append(pl.BlockSpec((tm, k), lambda i: (i, 0)))
        ins.append(_resident((N, k), lambda i, blk=blk: (0, blk)) if nt else _resident((k, N), lambda i: (0, 0)))
        args += [a, b]
    return pl.pallas_call(
        body, grid=(M // tm,), in_specs=ins, out_specs=pl.BlockSpec((tm, N), lambda i: (i, 0)),
        out_shape=jax.ShapeDtypeStruct((M, N), out_dtype), compiler_params=_cp(("parallel",)), name=name,
    )(*args)


def _mm_tn(a, g, name):
    S, Ka = a.shape
    N = g.shape[1]
    tk = _div(Ka, 1408, LANE)
    tn = _div(N, 1408, LANE)
    ts = _div(S, 512, 8)

    def body(a_ref, g_ref, o_ref):
        @pl.when(pl.program_id(2) == 0)
        def _():
            o_ref[...] = jnp.zeros_like(o_ref)

        o_ref[...] += _dot_tn(a_ref[...], g_ref[...])

    return pl.pallas_call(
        body, grid=(Ka // tk, N // tn, S // ts),
        in_specs=[pl.BlockSpec((ts, tk), lambda i, j, s: (s, i)), pl.BlockSpec((ts, tn), lambda i, j, s: (s, j))],
        out_specs=pl.BlockSpec((tk, tn), lambda i, j, s: (i, j)),
        out_shape=jax.ShapeDtypeStruct((Ka, N), F32),
        compiler_params=_cp(("parallel", "parallel", "arbitrary")), name=name,
    )(a, g)


def _mm_tn_two(a0, a1, g, name):
    S, K = a0.shape
    N = g.shape[1]
    ts = _div(S, 512, 8)

    def body(a0_ref, a1_ref, g_ref, o_ref):
        i = pl.program_id(0)

        @pl.when(pl.program_id(1) == 0)
        def _():
            o_ref[...] = jnp.zeros_like(o_ref)

        @pl.when(i == 0)
        def _():
            o_ref[...] += _dot_tn(a0_ref[...], g_ref[...])

        @pl.when(i == 1)
        def _():
            o_ref[...] += _dot_tn(a1_ref[...], g_ref[...])

    return pl.pallas_call(
        body, grid=(2, S // ts),
        in_specs=[pl.BlockSpec((ts, K), lambda i, s: (jnp.where(i == 0, s, 0), 0)),
                  pl.BlockSpec((ts, K), lambda i, s: (jnp.where(i == 1, s, 0), 0)),
                  pl.BlockSpec((ts, N), lambda i, s: (s, 0))],
        out_specs=pl.BlockSpec((K, N), lambda i, s: (i, 0)),
        out_shape=jax.ShapeDtypeStruct((2 * K, N), F32),
        compiler_params=_cp(("parallel", "arbitrary")), name=name,
    )(a0, a1, g)


def _ffn_up(h, w4, name):
    S, D = h.shape
    ns, _, C = w4.shape
    tm = _div(S, 256, 8)

    def body(h_ref, w_ref, z_ref):
        hv = h_ref[...]
        for s in range(ns):
            z_ref[:, s * C:(s + 1) * C] = _dot(hv, w_ref[s]).astype(BF16)

    return pl.pallas_call(
        body, grid=(S // tm,), in_specs=[pl.BlockSpec((tm, D), lambda i: (i, 0)), _resident((ns, D, C), lambda i: (0, 0, 0))],
        out_specs=pl.BlockSpec((tm, ns * C), lambda i: (i, 0)), out_shape=jax.ShapeDtypeStruct((S, ns * C), BF16),
        compiler_params=_cp(("parallel",)), name=name,
    )(h, w4)


def _swiglu(gt, up):
    return gt * _sigmoid(gt) * up


def _ffn_down(z, wo, x, gv, name):
    S = z.shape[0]
    Fd, D = wo.shape
    tm = _div(S, 256, 8)

    def body(g_ref, u_ref, w_ref, x_ref, gv_ref, y_ref, xn_ref):
        act = _swiglu(g_ref[...].astype(F32), u_ref[...].astype(F32))
        y = _dot(act, w_ref[...])
        y_ref[...] = y
        xn_ref[...] = x_ref[...] + gv_ref[...] * y

    row = pl.BlockSpec((tm, D), lambda i: (i, 0))
    os_ = jax.ShapeDtypeStruct((S, D), F32)
    return pl.pallas_call(
        body, grid=(S // tm,),
        in_specs=[pl.BlockSpec((tm, Fd), lambda i: (i, 0)), pl.BlockSpec((tm, Fd), lambda i: (i, 1)), _resident((Fd, D), lambda i: (0, 0)), row,
                  pl.BlockSpec((1, D), lambda i: (0, 0))],
        out_specs=[row, row], out_shape=[os_, os_], compiler_params=_cp(("parallel",)), name=name,
    )(z, z, wo, x, gv)


def _resid_bwd(dxn, y, gv, name):
    S, D = dxn.shape
    tm = _div(S, 512, 8)

    def body(d_ref, y_ref, gv_ref, dy_ref, dg_ref):
        @pl.when(pl.program_id(0) == 0)
        def _():
            dg_ref[...] = jnp.zeros_like(dg_ref)

        d = d_ref[...]
        dy_ref[...] = (gv_ref[...] * d).astype(BF16)
        dg_ref[...] += jnp.sum(d * y_ref[...], axis=0, keepdims=True)

    row = pl.BlockSpec((tm, D), lambda i: (i, 0))
    vec = pl.BlockSpec((1, D), lambda i: (0, 0))
    return pl.pallas_call(
        body, grid=(S // tm,), in_specs=[row, row, vec], out_specs=[row, vec],
        out_shape=[jax.ShapeDtypeStruct((S, D), BF16), jax.ShapeDtypeStruct((1, D), F32)],
        compiler_params=_cp(("arbitrary",)), name=name,
    )(dxn, y, gv)


def _ffn_bwd_act(dy, wo, z, name):
    S, D = dy.shape
    Fd = wo.shape[0]
    tm = _div(S, 256, 8)

    def body(dy_ref, w_ref, g_ref, u_ref, dz_ref):
        da = _dot_nt(dy_ref[...], w_ref[...])
        gt = g_ref[...].astype(F32)
        sg = _sigmoid(gt)
        dz_ref[:, Fd:] = (da * gt * sg).astype(BF16)
        dz_ref[:, :Fd] = (da * u_ref[...].astype(F32) * (sg * (1.0 + gt * (1.0 - sg)))).astype(BF16)

    return pl.pallas_call(
        body, grid=(S // tm,),
        in_specs=[pl.BlockSpec((tm, D), lambda i: (i, 0)), _resident((Fd, D), lambda i: (0, 0)),
                  pl.BlockSpec((tm, Fd), lambda i: (i, 0)), pl.BlockSpec((tm, Fd), lambda i: (i, 1))],
        out_specs=pl.BlockSpec((tm, 2 * Fd), lambda i: (i, 0)), out_shape=jax.ShapeDtypeStruct((S, 2 * Fd), BF16),
        compiler_params=_cp(("parallel",)), name=name,
    )(dy, wo, z, z)


def _ffn_dh(dz, w4, name):
    S = dz.shape[0]
    ns, D, C = w4.shape
    tm = _div(S, 256, 8)

    def body(dz_ref, w_ref, o_ref):
        acc = _dot_nt(dz_ref[:, 0:C], w_ref[0])
        for s in range(1, ns):
            acc = acc + _dot_nt(dz_ref[:, s * C:(s + 1) * C], w_ref[s])
        o_ref[...] = acc

    return pl.pallas_call(
        body, grid=(S // tm,), in_specs=[pl.BlockSpec((tm, ns * C), lambda i: (i, 0)), _resident((ns, D, C), lambda i: (0, 0, 0))],
        out_specs=pl.BlockSpec((tm, D), lambda i: (i, 0)), out_shape=jax.ShapeDtypeStruct((S, D), F32),
        compiler_params=_cp(("parallel",)), name=name,
    )(dz, w4)


def _dw_ffn_in(h, dz, ns, name):
    S, D = h.shape
    C = dz.shape[1] // ns
    ts = _div(S, 512, 8)

    def body(h_ref, g_ref, o_ref):
        @pl.when(pl.program_id(1) == 0)
        def _():
            o_ref[...] = jnp.zeros_like(o_ref)

        o_ref[...] += _dot_tn(h_ref[...], g_ref[...])

    return pl.pallas_call(
        body, grid=(ns, S // ts),
        in_specs=[pl.BlockSpec((ts, D), lambda j, s: (s, 0)), pl.BlockSpec((ts, C), lambda j, s: (s, j))],
        out_specs=pl.BlockSpec((None, D, C), lambda j, s: (j, 0, 0)), out_shape=jax.ShapeDtypeStruct((ns, D, C), F32),
        compiler_params=_cp(("parallel", "arbitrary")), name=name,
    )(h, dz)


def _dw_ffn_out(z, dy, name):
    S, D = dy.shape
    Fd = z.shape[1] // 2
    tk = _div(Fd, 1408, LANE)
    nk = Fd // tk
    ts = _div(S, 512, 8)

    def body(g_ref, u_ref, dy_ref, o_ref):
        @pl.when(pl.program_id(1) == 0)
        def _():
            o_ref[...] = jnp.zeros_like(o_ref)

        act = _swiglu(g_ref[...].astype(F32), u_ref[...].astype(F32))
        o_ref[...] += _dot_tn(act, dy_ref[...])

    return pl.pallas_call(
        body, grid=(nk, S // ts),
        in_specs=[pl.BlockSpec((ts, tk), lambda i, s: (s, i)), pl.BlockSpec((ts, tk), lambda i, s: (s, nk + i)), pl.BlockSpec((ts, D), lambda i, s: (s, 0))],
        out_specs=pl.BlockSpec((tk, D), lambda i, s: (i, 0)), out_shape=jax.ShapeDtypeStruct((Fd, D), F32),
        compiler_params=_cp(("parallel", "arbitrary")), name=name,
    )(z, z, dy)


def _mix_out(yconv, ygla, wout, x, gv, name):
    S, Kc = yconv.shape
    Kg = ygla.shape[1]
    D = wout.shape[1]
    tm = _div(S, 512, 8)

    def body(a_ref, b_ref, w_ref, x_ref, gv_ref, y_ref, xn_ref):
        y = _dot(a_ref[...], w_ref[0:Kc, :]) + _dot(b_ref[...], w_ref[Kc:Kc + Kg, :])
        y_ref[...] = y
        xn_ref[...] = x_ref[...] + gv_ref[...] * y

    row = pl.BlockSpec((tm, D), lambda i: (i, 0))
    os_ = jax.ShapeDtypeStruct((S, D), F32)
    return pl.pallas_call(
        body, grid=(S // tm,),
        in_specs=[pl.BlockSpec((tm, Kc), lambda i: (i, 0)), pl.BlockSpec((tm, Kg), lambda i: (i, 0)), _resident((Kc + Kg, D), lambda i: (0, 0)), row,
                  pl.BlockSpec((1, D), lambda i: (0, 0))],
        out_specs=[row, row], out_shape=[os_, os_], compiler_params=_cp(("parallel",)), name=name,
    )(yconv, ygla, wout, x, gv)


def _ln_parts(yc, g, b):
    mu = jnp.mean(yc, axis=-1, keepdims=True)
    xc = yc - mu
    rs = lax.rsqrt(jnp.mean(xc * xc, axis=-1, keepdims=True) + EPS)
    xh = xc * rs
    return xh, rs, xh * g + b


def _conv_fwd(z, w_dw, b_dw, g_ln, b_ln, name):
    S = z.shape[0]
    W, C = w_dw.shape
    ts = _div(S, 512, HALO)
    hb = ts // HALO
    off = HALO - (W - 1)
    ca, cb = ZC_A // C, ZC_B // C

    def body(a_ref, b_ref, pa_ref, pb_ref, w_ref, bd_ref, g_ref, bl_ref, u_ref, yc_ref, o_ref, ext):
        keep = (pl.program_id(0) > 0).astype(F32)
        u = a_ref[...] * _sigmoid(b_ref[...])
        ext[pl.ds(0, HALO), :] = pa_ref[...] * _sigmoid(pb_ref[...]) * keep
        ext[pl.ds(HALO, ts), :] = u
        u_ref[...] = u
        acc = jnp.zeros((ts, C), F32)
        for j in range(W):
            acc = acc + w_ref[pl.ds(j, 1), :] * ext[pl.ds(off + j, ts), :]
        yc = acc + bd_ref[...]
        yc_ref[...] = yc
        _, _, ln = _ln_parts(yc, g_ref[...], bl_ref[...])
        o_ref[...] = (ln * _sigmoid(ln)).astype(BF16)

    cur = lambda col: pl.BlockSpec((ts, C), lambda i: (i, col))
    prev = lambda col: pl.BlockSpec((HALO, C), lambda i: (jnp.maximum(i * hb - 1, 0), col))
    vec = pl.BlockSpec((1, C), lambda i: (0, 0))
    row = pl.BlockSpec((ts, C), lambda i: (i, 0))
    fs = jax.ShapeDtypeStruct((S, C), F32)
    return pl.pallas_call(
        body, grid=(S // ts,),
        in_specs=[cur(ca), cur(cb), prev(ca), prev(cb), pl.BlockSpec((W, C), lambda i: (0, 0)), vec, vec, vec],
        out_specs=[row, row, row], out_shape=[fs, fs, jax.ShapeDtypeStruct((S, C), BF16)],
        scratch_shapes=[pltpu.VMEM((ts + HALO, C), F32)],
        compiler_params=_cp(("parallel",)), name=name,
    )(z, z, z, z, w_dw, b_dw, g_ln, b_ln)


def _conv_bwd(dycat, z, u, yc, w_dw, g_ln, b_ln, name):
    S = z.shape[0]
    W, C = w_dw.shape
    ts = _div(S, 512, HALO)
    hb = ts // HALO
    nblk = S // ts
    off = HALO - (W - 1)
    ca, cb = ZC_A // C, ZC_B // C

    def ln_silu_bwd(dy, ycv, g, b):
        xh, rs, ln = _ln_parts(ycv, g, b)
        sl = _sigmoid(ln)
        dln = dy * (sl * (1.0 + ln * (1.0 - sl)))
        dxh = dln * g
        dyc = rs * (dxh - jnp.mean(dxh, axis=-1, keepdims=True) - xh * jnp.mean(dxh * xh, axis=-1, keepdims=True))
        return dyc, dln, xh

    def body(dy_ref, ndy_ref, yc_ref, nyc_ref, u_ref, pu_ref, a_ref, b_ref, w_ref, g_ref, bl_ref,
             dab_ref, dw_ref, dbd_ref, dg_ref, dbl_ref, uext, dext):
        i = pl.program_id(0)

        @pl.when(i == 0)
        def _():
            dw_ref[...] = jnp.zeros_like(dw_ref)
            dbd_ref[...] = jnp.zeros_like(dbd_ref)
            dg_ref[...] = jnp.zeros_like(dg_ref)
            dbl_ref[...] = jnp.zeros_like(dbl_ref)

        g = g_ref[...]
        bl = bl_ref[...]
        dyc, dln, xh = ln_silu_bwd(dy_ref[...], yc_ref[...], g, bl)
        ndyc, _, _ = ln_silu_bwd(ndy_ref[...], nyc_ref[...], g, bl)
        dg_ref[...] += jnp.sum(dln * xh, axis=0, keepdims=True)
        dbl_ref[...] += jnp.sum(dln, axis=0, keepdims=True)
        dbd_ref[...] += jnp.sum(dyc, axis=0, keepdims=True)
        dext[pl.ds(0, ts), :] = dyc
        dext[pl.ds(ts, HALO), :] = ndyc * (i < nblk - 1).astype(F32)
        uext[pl.ds(0, HALO), :] = pu_ref[...] * (i > 0).astype(F32)
        uext[pl.ds(HALO, ts), :] = u_ref[...]
        du = jnp.zeros((ts, C), F32)
        for j in range(W):
            du = du + w_ref[pl.ds(j, 1), :] * dext[pl.ds(W - 1 - j, ts), :]
            dw_ref[pl.ds(j, 1), :] += jnp.sum(dyc * uext[pl.ds(off + j, ts), :], axis=0, keepdims=True)
        sb = _sigmoid(b_ref[...])
        dab_ref[:, 0:C] = (du * sb).astype(BF16)
        dab_ref[:, C:2 * C] = (du * a_ref[...] * sb * (1.0 - sb)).astype(BF16)

    row = pl.BlockSpec((ts, C), lambda i: (i, 0))
    nxt = pl.BlockSpec((HALO, C), lambda i: (jnp.minimum((i + 1) * hb, S // HALO - 1), 0))
    prv = pl.BlockSpec((HALO, C), lambda i: (jnp.maximum(i * hb - 1, 0), 0))
    vec = pl.BlockSpec((1, C), lambda i: (0, 0))
    wsp = pl.BlockSpec((W, C), lambda i: (0, 0))
    vs = jax.ShapeDtypeStruct((1, C), F32)
    return pl.pallas_call(
        body, grid=(nblk,),
        in_specs=[row, nxt, row, nxt, row, prv, pl.BlockSpec((ts, C), lambda i: (i, ca)), pl.BlockSpec((ts, C), lambda i: (i, cb)), wsp, vec, vec],
        out_specs=[pl.BlockSpec((ts, 2 * C), lambda i: (i, 0)), wsp, vec, vec, vec],
        out_shape=[jax.ShapeDtypeStruct((S, 2 * C), BF16), jax.ShapeDtypeStruct((W, C), F32), vs, vs, vs],
        scratch_shapes=[pltpu.VMEM((ts + HALO, C), F32), pltpu.VMEM((ts + HALO, C), F32)],
        compiler_params=_cp(("arbitrary",)), name=name,
    )(dycat, dycat, yc, yc, u, u, z, z, w_dw, g_ln, b_ln)


def _log_gate(zg):
    return (jnp.minimum(zg, 0.0) - jnp.log(1.0 + jnp.exp(-jnp.abs(zg)))) * (1.0 / GATE_TAU)


def _loggate(z, wgp, bgp, name):
    S = z.shape[0]
    N = wgp.shape[1]
    ts = _div(S, 512, 8)

    def body(g_ref, w_ref, b_ref, o_ref):
        o_ref[...] = _log_gate(_dot(g_ref[...], w_ref[...]) + b_ref[...])

    return pl.pallas_call(
        body, grid=(S // ts,),
        in_specs=[pl.BlockSpec((ts, LANE), lambda i: (i, ZC_G // LANE)), pl.BlockSpec((LANE, N), lambda i: (0, 0)), pl.BlockSpec((1, N), lambda i: (0, 0))],
        out_specs=pl.BlockSpec((ts, N), lambda i: (i, 0)), out_shape=jax.ShapeDtypeStruct((S, N), F32),
        compiler_params=_cp(("parallel",)), name=name,
    )(z, wgp, bgp)


def _loggate_bwd(dla, z, wgp, wgp_t, bgp, name):
    S = z.shape[0]
    N = wgp.shape[1]
    ts = _div(S, 512, 8)

    def body(dla_ref, g_ref, w_ref, wt_ref, b_ref, dg_ref, dw_ref, db_ref):
        @pl.when(pl.program_id(0) == 0)
        def _():
            dw_ref[...] = jnp.zeros_like(dw_ref)
            db_ref[...] = jnp.zeros_like(db_ref)

        glr = g_ref[...]
        zg = _dot(glr, w_ref[...]) + b_ref[...]
        dzg = dla_ref[...] * (1.0 / GATE_TAU) * (1.0 - _sigmoid(zg))
        dg_ref[...] = _dot(dzg, wt_ref[...]).astype(BF16)
        dw_ref[...] += _dot_tn(glr, dzg)
        db_ref[...] += jnp.sum(dzg, axis=0, keepdims=True)

    return pl.pallas_call(
        body, grid=(S // ts,),
        in_specs=[pl.BlockSpec((ts, N), lambda i: (i, 0)), pl.BlockSpec((ts, LANE), lambda i: (i, ZC_G // LANE)),
                  pl.BlockSpec((LANE, N), lambda i: (0, 0)), pl.BlockSpec((N, LANE), lambda i: (0, 0)), pl.BlockSpec((1, N), lambda i: (0, 0))],
        out_specs=[pl.BlockSpec((ts, LANE), lambda i: (i, 0)), pl.BlockSpec((LANE, N), lambda i: (0, 0)), pl.BlockSpec((1, N), lambda i: (0, 0))],
        out_shape=[jax.ShapeDtypeStruct((S, LANE), BF16), jax.ShapeDtypeStruct((LANE, N), F32), jax.ShapeDtypeStruct((1, N), F32)],
        compiler_params=_cp(("arbitrary",)), name=name,
    )(dla, z, wgp, wgp_t, bgp)


def _chunk_fwd_terms(q, k, la, tril):
    bc = _dot_exact(tril, la)
    bend = jnp.sum(la, axis=0, keepdims=True)
    eb = jnp.exp(bc)
    enb = jnp.exp(-bc)
    ee = jnp.exp(bend - bc)
    qs = q * (DK ** -0.5)
    return bend, eb, enb, ee, qs * eb, qs * enb, k * enb, k * eb, k * ee


def _gla_fwd(z, la, gn, name):
    S = z.shape[0]
    W = HEADS * LANE
    tb = _div(S, 512, CHUNK)
    cpb = tb // CHUNK

    def body(q_ref, k_ref, v_ref, r_ref, la_ref, gn_ref, o_ref, sp_ref, y_ref, st):
        @pl.when(pl.program_id(0) == 0)
        def _():
            st[...] = jnp.zeros_like(st)

        ri = lax.broadcasted_iota(jnp.int32, (CHUNK, CHUNK), 0)
        ci = lax.broadcasted_iota(jnp.int32, (CHUNK, CHUNK), 1)
        tri = ri >= ci
        tril = tri.astype(F32)

        def chunk(c, carry):
            rows = pl.ds(pl.multiple_of(c * CHUNK, CHUNK), CHUNK)
            for h in range(HEADS):
                ln = pl.ds(h * LANE, LANE)
                q, k, v, lav = q_ref[rows, ln], k_ref[rows, ln], v_ref[rows, ln], la_ref[rows, ln]
                bend, _, _, _, qf, qb, kb, kf, ke = _chunk_fwd_terms(q, k, lav, tril)
                att = jnp.where(tri, _dot_nt(qf, kb), _dot_nt(qb, kf))
                s_prev = st[h]
                o = _dot(att, v) + _dot_nt(qf, s_prev)
                sp_ref[h, c] = s_prev
                st[h] = s_prev * jnp.exp(bend) + _dot_tn(v, ke)
                o_ref[rows, ln] = o
                rms = lax.rsqrt(jnp.mean(o * o, axis=-1, keepdims=True) + EPS)
                rv = r_ref[rows, ln]
                y_ref[rows, ln] = (o * rms * gn_ref[pl.ds(h, 1), :] * (rv * _sigmoid(rv))).astype(BF16)
            return carry

        lax.fori_loop(0, cpb, chunk, 0)

    zb = lambda base: pl.BlockSpec((tb, W), lambda i: (i, base // W))
    hb_ = pl.BlockSpec((tb, W), lambda i: (i, 0))
    return pl.pallas_call(
        body, grid=(S // tb,),
        in_specs=[zb(ZC_Q), zb(ZC_K), zb(ZC_V), zb(ZC_R), hb_, pl.BlockSpec((HEADS, DV), lambda i: (0, 0))],
        out_specs=[hb_, pl.BlockSpec((HEADS, cpb, DV, DKP), lambda i: (0, i, 0, 0)), hb_],
        out_shape=[jax.ShapeDtypeStruct((S, D_GLA), F32), jax.ShapeDtypeStruct((HEADS, S // CHUNK, DV, DKP), F32),
                   jax.ShapeDtypeStruct((S, D_GLA), BF16)],
        scratch_shapes=[pltpu.VMEM((HEADS, DV, DKP), F32)],
        compiler_params=_cp(("arbitrary",)), name=name,
    )(z, z, z, z, la, gn)


def _gla_bwd(dycat, z, la, o_raw, sprev, gn, name):
    S = z.shape[0]
    W = HEADS * LANE
    tb = _div(S, 512, CHUNK)
    cpb = tb // CHUNK
    nb = S // tb

    def body(q_ref, k_ref, v_ref, r_ref, la_ref, o_ref, sp_ref, dy_ref, gn_ref, dz_ref, dla_ref, dgn_ref, dst):
        @pl.when(pl.program_id(0) == 0)
        def _():
            dst[...] = jnp.zeros_like(dst)
            dgn_ref[...] = jnp.zeros_like(dgn_ref)

        ri = lax.broadcasted_iota(jnp.int32, (CHUNK, CHUNK), 0)
        ci = lax.broadcasted_iota(jnp.int32, (CHUNK, CHUNK), 1)
        tri = ri >= ci
        tril = tri.astype(F32)
        triu = (ri <= ci).astype(F32)

        def chunk(cc, carry):
            c = cpb - 1 - cc
            rows = pl.ds(pl.multiple_of(c * CHUNK, CHUNK), CHUNK)
            for h in range(HEADS):
                ln = pl.ds(h * LANE, LANE)
                q, k, v, lav = q_ref[rows, ln], k_ref[rows, ln], v_ref[rows, ln], la_ref[rows, ln]
                bend, eb, enb, ee, qf, qb, kb, kf, ke = _chunk_fwd_terms(q, k, lav, tril)
                att = jnp.where(tri, _dot_nt(qf, kb), _dot_nt(qb, kf))
                s_prev = sp_ref[h, c]
                gdec = jnp.exp(bend)
                gn = gn_ref[pl.ds(h, 1), :]
                o = o_ref[rows, ln]
                rv = r_ref[rows, ln]
                dy = dy_ref[rows, ln]
                rms = lax.rsqrt(jnp.mean(o * o, axis=-1, keepdims=True) + EPS)
                oh = o * rms
                sg = _sigmoid(rv)
                sr = rv * sg
                dz_ref[rows, pl.ds(ZC_R + h * LANE, LANE)] = (dy * oh * gn * (sg * (1.0 + rv * (1.0 - sg)))).astype(BF16)
                dgn_ref[pl.ds(h, 1), :] += jnp.sum(dy * sr * oh, axis=0, keepdims=True)
                w = dy * sr * gn
                do = rms * (w - oh * jnp.mean(w * oh, axis=-1, keepdims=True))
                datt = _dot_nt(do, v)
                daf = jnp.where(tri, datt, 0.0)
                dab = jnp.where(tri, 0.0, datt)
                ds = dst[h]
                dz_ref[rows, pl.ds(ZC_V + h * LANE, LANE)] = (_dot_tn(att, do) + _dot_nt(ke, ds)).astype(BF16)
                dke = _dot(v, ds)
                dqf = _dot(daf, kb) + _dot(do, s_prev)
                dkb = _dot_tn(daf, qf)
                dqb = _dot(dab, kf)
                dkf = _dot_tn(dab, qb)
                dg = jnp.sum(ds * s_prev, axis=0, keepdims=True)
                dst[h] = ds * gdec + _dot_tn(do, qf)
                dz_ref[rows, pl.ds(ZC_Q + h * LANE, LANE)] = ((dqf * eb + dqb * enb) * (DK ** -0.5)).astype(BF16)
                dz_ref[rows, pl.ds(ZC_K + h * LANE, LANE)] = (dkb * enb + dkf * eb + dke * ee).astype(BF16)
                dbc = dqf * qf - dkb * kb - dqb * qb + dkf * kf - dke * ke
                dbend = jnp.sum(dke * ke, axis=0, keepdims=True) + dg * gdec
                dla_ref[rows, ln] = _dot_exact(triu, dbc) + dbend
            return carry

        lax.fori_loop(0, cpb, chunk, 0)

    zb = lambda base: pl.BlockSpec((tb, W), lambda i: (nb - 1 - i, base // W))
    hb_ = pl.BlockSpec((tb, W), lambda i: (nb - 1 - i, 0))
    return pl.pallas_call(
        body, grid=(nb,),
        in_specs=[zb(ZC_Q), zb(ZC_K), zb(ZC_V), zb(ZC_R), hb_, hb_,
                  pl.BlockSpec((HEADS, cpb, DV, DKP), lambda i: (0, nb - 1 - i, 0, 0)),
                  pl.BlockSpec((tb, W), lambda i: (nb - 1 - i, 1)),
                  pl.BlockSpec((HEADS, DV), lambda i: (0, 0))],
        out_specs=[pl.BlockSpec((tb, Z_GLA), lambda i: (nb - 1 - i, 0)), hb_, pl.BlockSpec((HEADS, DV), lambda i: (0, 0))],
        out_shape=[jax.ShapeDtypeStruct((S, Z_GLA), BF16), jax.ShapeDtypeStruct((S, HEADS * DKP), F32), jax.ShapeDtypeStruct((HEADS, DV), F32)],
        scratch_shapes=[pltpu.VMEM((HEADS, DV, DKP), F32)],
        compiler_params=_cp(("arbitrary",)), name=name,
    )(z, z, z, z, la, o_raw, sprev, dycat, gn)


def _mod_proj(c_all, w3, layer, b, name):
    B, D = c_all.shape
    N = w3.shape[2]
    tn = _div(N, 1024, LANE)

    def body(c_ref, w_ref, b_ref, o_ref):
        cv = c_ref[...]
        o_ref[...] = _dot(cv * _sigmoid(cv), w_ref[...]) + b_ref[...]

    return pl.pallas_call(
        body, grid=(N // tn,),
        in_specs=[pl.BlockSpec((B, D), lambda j: (0, 0)), pl.BlockSpec((None, D, tn), lambda j: (layer, 0, j)), pl.BlockSpec((1, tn), lambda j: (0, j))],
        out_specs=pl.BlockSpec((B, tn), lambda j: (0, j)), out_shape=jax.ShapeDtypeStruct((B, N), F32),
        compiler_params=_cp(("parallel",)), name=name,
    )(c_all, w3, b)


def _mod_wgrad(c_t, dm, name):
    D, B = c_t.shape
    N = dm.shape[1]
    tn = _div(N, 1024, LANE)

    def body(c_ref, d_ref, o_ref):
        cv = c_ref[...]
        ca = cv * _sigmoid(cv)
        acc = ca[:, 0:1] * d_ref[pl.ds(0, 1), :]
        for b in range(1, B):
            acc = acc + ca[:, b:b + 1] * d_ref[pl.ds(b, 1), :]
        o_ref[...] = acc

    return pl.pallas_call(
        body, grid=(N // tn,),
        in_specs=[pl.BlockSpec((D, B), lambda j: (0, 0)), pl.BlockSpec((B, tn), lambda j: (0, j))],
        out_specs=pl.BlockSpec((D, tn), lambda j: (0, j)), out_shape=jax.ShapeDtypeStruct((D, N), F32),
        compiler_params=_cp(("parallel",)), name=name,
    )(c_t, dm)


def _rowsum(xs, name):
    n, N = xs.shape
    tn = _div(N, 8192, LANE)

    def body(x_ref, o_ref):
        acc = x_ref[pl.ds(0, 1), :]
        for r in range(1, n):
            acc = acc + x_ref[pl.ds(r, 1), :]
        o_ref[...] = acc

    return pl.pallas_call(
        body, grid=(N // tn,), in_specs=[pl.BlockSpec((n, tn), lambda j: (0, j))],
        out_specs=pl.BlockSpec((1, tn), lambda j: (0, j)), out_shape=jax.ShapeDtypeStruct((1, N), F32),
        compiler_params=_cp(("parallel",)), name=name,
    )(xs)


def _adamw(w, g, m, v, name):
    R, C = w.shape
    tr = _div(R, max(8, (1 << 18) // C), 8)

    def body(w_ref, g_ref, m_ref, v_ref, d_ref, nm_ref, nv_ref):
        gv = g_ref[...]
        mn = ADAM_B1 * m_ref[...] + (1.0 - ADAM_B1) * gv
        vn = ADAM_B2 * v_ref[...] + (1.0 - ADAM_B2) * (gv * gv)
        m_hat = mn / (1.0 - ADAM_B1 ** ADAM_STEP)
        v_hat = vn / (1.0 - ADAM_B2 ** ADAM_STEP)
        d_ref[...] = -ADAM_LR * (m_hat / (jnp.sqrt(v_hat) + ADAM_EPS) + ADAM_WD * w_ref[...])
        nm_ref[...] = mn
        nv_ref[...] = vn

    blk = pl.BlockSpec((tr, C), lambda i: (i, 0))
    os_ = jax.ShapeDtypeStruct((R, C), F32)
    return pl.pallas_call(
        body, grid=(R // tr,), in_specs=[blk] * 4, out_specs=[blk] * 3, out_shape=[os_] * 3,
        compiler_params=_cp(("parallel",)), name=name,
    )(w, g, m, v)


def _place():
    return lax.axis_index("x"), lax.axis_index("y"), lax.axis_index("c")


def _other_chips(x, y):
    return [(1 - x, y), (x, 1 - y), (1 - x, 1 - y)]


def _half(c, rows):
    return pl.ds(c * (rows // 2), rows // 2)


_ANY = pl.BlockSpec(memory_space=pl.ANY)


def _ag_small(v, name):
    r, n = v.shape

    def body(v_ref, o_ref, send_sems, recv_sems):
        x, y, c = _place()
        me = 4 * x + 2 * y + c
        o_ref[pl.ds(me, 1)] = v_ref[...][None]
        peers = [(x ^ (k >> 2), y ^ ((k >> 1) & 1), c ^ (k & 1)) for k in range(1, 8)]
        copies = []
        for k, peer in enumerate(peers):
            cp = pltpu.make_async_remote_copy(
                src_ref=v_ref, dst_ref=o_ref.at[me], send_sem=send_sems.at[k], recv_sem=recv_sems.at[k],
                device_id=peer, device_id_type=MESH)
            cp.start()
            copies.append(cp)
        for cp in copies:
            cp.wait()

    return pl.pallas_call(
        body, out_shape=jax.ShapeDtypeStruct((8, r, n), v.dtype),
        in_specs=[pl.BlockSpec(memory_space=pltpu.VMEM)], out_specs=pl.BlockSpec(memory_space=pltpu.VMEM),
        scratch_shapes=[pltpu.SemaphoreType.DMA((7,)), pltpu.SemaphoreType.DMA((7,))],
        compiler_params=pltpu.CompilerParams(vmem_limit_bytes=VMEM_LIMIT), name=name,
    )(v)


def _ag_layer(shards, name):
    n = len(shards)

    def body(*refs):
        src, land = refs[:n], refs[n:2 * n]
        send_sems, recv_sems, local_sems = refs[2 * n:]
        x, y, c = _place()
        s_me = 2 * x + y
        chips = _other_chips(x, y)
        sibling = (x, y, 1 - c)
        mine = [pltpu.make_async_copy(src[i], land[i].at[s_me], local_sems.at[i]) for i in range(n)]
        for cp in mine:
            cp.start()

        def copy(k, i, s, h, to, from_src=False):
            rows = _half(h, src[i].shape[0])
            blk = land[i].at[s, rows]
            return pltpu.make_async_remote_copy(
                src_ref=src[i].at[rows] if from_src else blk, dst_ref=blk, send_sem=send_sems.at[k], recv_sem=recv_sems.at[k],
                device_id=to, device_id_type=MESH)

        first = [copy(3 * i + j, i, s_me, c, (*chip, c), from_src=True) for j, chip in enumerate(chips) for i in range(n)]
        for cp in first:
            cp.start()
        passed = []
        for j, (cx, cy) in enumerate(chips):
            for i in range(n):
                copy(3 * i + j, i, 2 * cx + cy, c, sibling).wait_recv()
                fw = copy(3 * n + 3 * i + j, i, 2 * cx + cy, c, sibling)
                fw.start()
                passed.append(fw)
        for j, (cx, cy) in enumerate(chips):
            for i in range(n):
                copy(3 * n + 3 * i + j, i, 2 * cx + cy, 1 - c, sibling).wait_recv()
        for cp in first + passed:
            cp.wait_send()
        for cp in mine:
            cp.wait()

    return pl.pallas_call(
        body, out_shape=[jax.ShapeDtypeStruct((N_CHIPS,) + s.shape, s.dtype) for s in shards],
        in_specs=[_ANY] * n, out_specs=[_ANY] * n,
        scratch_shapes=[pltpu.SemaphoreType.DMA((6 * n,)), pltpu.SemaphoreType.DMA((6 * n,)), pltpu.SemaphoreType.DMA((n,))],
        compiler_params=pltpu.CompilerParams(has_side_effects=True), name=name,
    )(*shards)


def _rs_sibling(gs, name):
    n = len(gs)

    def body(*refs):
        src, out = refs[:n], refs[n:2 * n]
        send_sems, recv_sems = refs[2 * n:]
        x, y, c = _place()
        copies = []
        for i in range(n):
            cp = pltpu.make_async_remote_copy(
                src_ref=src[i].at[:, _half(1 - c, src[i].shape[1])], dst_ref=out[i], send_sem=send_sems.at[i], recv_sem=recv_sems.at[i],
                device_id=(x, y, 1 - c), device_id_type=MESH)
            cp.start()
            copies.append(cp)
        for cp in copies:
            cp.wait()

    return pl.pallas_call(
        body, out_shape=[jax.ShapeDtypeStruct((N_CHIPS, g.shape[1] // 2, g.shape[2]), g.dtype) for g in gs],
        in_specs=[_ANY] * n, out_specs=[_ANY] * n,
        scratch_shapes=[pltpu.SemaphoreType.DMA((n,)), pltpu.SemaphoreType.DMA((n,))],
        compiler_params=pltpu.CompilerParams(has_side_effects=True), name=name,
    )(*gs)


def _rs_presum(g, sib, c_arr, name):
    ns, R, C = g.shape
    rh = R // 2
    tr = _div(rh, max(16, (1 << 19) // C), 16)
    nrb = rh // tr

    def body(c_ref, g_ref, s_ref, o_ref):
        o_ref[...] = (g_ref[...] + s_ref[...]).astype(BF16)

    return pl.pallas_call(
        body, out_shape=jax.ShapeDtypeStruct((ns, rh, C), BF16),
        grid_spec=pltpu.PrefetchScalarGridSpec(
            num_scalar_prefetch=1, grid=(ns, nrb),
            in_specs=[pl.BlockSpec((None, tr, C), lambda s, r, c_ref: (s, c_ref[0] * nrb + r, 0)),
                      pl.BlockSpec((None, tr, C), lambda s, r, c_ref: (s, r, 0))],
            out_specs=pl.BlockSpec((None, tr, C), lambda s, r, c_ref: (s, r, 0))),
        compiler_params=_cp(("parallel", "parallel")), name=name,
    )(c_arr, g, sib)


def _rs_chips(ps, name):
    n = len(ps)

    def body(*refs):
        src, out = refs[:n], refs[n:2 * n]
        send_sems, recv_sems = refs[2 * n:]
        x, y, c = _place()
        copies = []
        for j, (cx, cy) in enumerate(_other_chips(x, y)):
            for i in range(n):
                cp = pltpu.make_async_remote_copy(
                    src_ref=src[i].at[2 * cx + cy], dst_ref=out[i].at[j], send_sem=send_sems.at[3 * i + j], recv_sem=recv_sems.at[3 * i + j],
                    device_id=(cx, cy, c), device_id_type=MESH)
                cp.start()
                copies.append(cp)
        for cp in copies:
            cp.wait()

    return pl.pallas_call(
        body, out_shape=[jax.ShapeDtypeStruct((3,) + p.shape[1:], p.dtype) for p in ps],
        in_specs=[_ANY] * n, out_specs=[_ANY] * n,
        scratch_shapes=[pltpu.SemaphoreType.DMA((3 * n,)), pltpu.SemaphoreType.DMA((3 * n,))],
        compiler_params=pltpu.CompilerParams(has_side_effects=True), name=name,
    )(*ps)


def _rs_sum(g, sib, recv, full, layer, sc_arr, name):
    ns, R, C = g.shape
    rh = R // 2
    tr = _div(rh, max(16, (1 << 18) // C), 16)
    nrb = rh // tr

    def body(sc_ref, g_ref, s_ref, r_ref, f_ref, o_ref):
        acc = g_ref[...] + s_ref[...]
        for j in range(3):
            acc = acc + r_ref[j].astype(F32)
        o_ref[...] = acc

    return pl.pallas_call(
        body, out_shape=jax.ShapeDtypeStruct(full.shape, F32),
        grid_spec=pltpu.PrefetchScalarGridSpec(
            num_scalar_prefetch=1, grid=(nrb,),
            in_specs=[pl.BlockSpec((None, tr, C), lambda r, sc: (sc[0], sc[1] * nrb + r, 0)),
                      pl.BlockSpec((None, tr, C), lambda r, sc: (sc[0], r, 0)),
                      pl.BlockSpec((3, tr, C), lambda r, sc: (0, r, 0)),
                      _ANY],
            out_specs=pl.BlockSpec((None, tr, C), lambda r, sc: (layer, sc[1] * nrb + r, 0))),
        input_output_aliases={4: 0},
        compiler_params=_cp(("parallel",)), name=name,
    )(sc_arr, g, sib, recv, full)


def _rs_share(fulls, layer, name):
    n = len(fulls)

    def body(*refs):
        src, out = refs[:n], refs[n:2 * n]
        send_sems, recv_sems = refs[2 * n:]
        x, y, c = _place()
        copies = []
        for i in range(n):
            rows = out[i].shape[1]
            cp = pltpu.make_async_remote_copy(
                src_ref=out[i].at[layer, _half(c, rows)], dst_ref=out[i].at[layer, _half(c, rows)],
                send_sem=send_sems.at[i], recv_sem=recv_sems.at[i], device_id=(x, y, 1 - c), device_id_type=MESH)
            cp.start()
            copies.append(cp)
        for cp in copies:
            cp.wait()

    return pl.pallas_call(
        body, out_shape=[jax.ShapeDtypeStruct(f.shape, f.dtype) for f in fulls],
        in_specs=[_ANY] * n, out_specs=[_ANY] * n, input_output_aliases={i: i for i in range(n)},
        scratch_shapes=[pltpu.SemaphoreType.DMA((n,)), pltpu.SemaphoreType.DMA((n,))],
        compiler_params=pltpu.CompilerParams(has_side_effects=True), name=name,
    )(*fulls)


def _pad_heads(w):
    lead = w.shape[:-1]
    w4 = w.reshape(*lead, HEADS, DK)
    w4 = jnp.pad(w4, [(0, 0)] * len(lead) + [(0, 0), (0, DKP - DK)])
    return w4.reshape(*lead, HEADS * DKP)


def _unpad_heads(w):
    lead = w.shape[:-1]
    return w.reshape(*lead, HEADS, DKP)[..., :DK].reshape(*lead, HEADS * DK)


def _mix_weight(win4, n_cols):
    D = win4.shape[1]
    w = jnp.transpose(win4[:, :, :n_cols], (1, 0, 2)).reshape(D, N_CHIPS * n_cols)
    o = 2 * D_CONV
    hk = HEADS * DK
    ab = w[:, :o]
    q = _pad_heads(w[:, o:o + hk])
    k = _pad_heads(w[:, o + hk:o + 2 * hk])
    vr = w[:, o + 2 * hk:o + 2 * hk + 2 * D_GLA]
    glr = jnp.pad(w[:, o + 2 * hk + 2 * D_GLA:], ((0, 0), (0, LANE - GATE_RANK)))
    return jnp.concatenate([q, k, vr, ab, glr], axis=1)


def _mix_weight_grad(dgla, dab, dglr, n_cols, n_pad):
    D = dab.shape[0]
    hkp = HEADS * DKP
    w = jnp.concatenate([dab, _unpad_heads(dgla[:, :hkp]), _unpad_heads(dgla[:, hkp:2 * hkp]), dgla[:, 2 * hkp:], dglr[:, :GATE_RANK]], axis=1)
    w = jnp.pad(w.reshape(D, N_CHIPS, n_cols), ((0, 0), (0, 0), (0, n_pad - n_cols)))
    return jnp.transpose(w, (1, 0, 2))


_ARG_NAMES = ['x', 'c', 'w_ada', 'b_ada', 'g_norm_ffn1', 'w_ffn1_in', 'w_ffn1_out', 'g_norm_mix', 'w_in', 'w_dw', 'b_dw', 'g_conv_ln', 'b_conv_ln', 'w_gate_up', 'b_gate', 'g_gla_norm', 'w_out', 'g_norm_ffn2', 'w_ffn2_in', 'w_ffn2_out', 'g_norm_final', 'w_ada_final', 'b_ada_final']
_WEIGHTS = _ARG_NAMES[2:]
_BIG = ('w_ffn1_in', 'w_ffn1_out', 'w_in', 'w_out', 'w_ffn2_in', 'w_ffn2_out')
_SMALL = ('g_norm_ffn1', 'g_norm_mix', 'w_dw', 'b_dw', 'g_conv_ln', 'b_conv_ln', 'w_gate_up', 'b_gate', 'g_gla_norm', 'g_norm_ffn2', 'g_norm_final')


def _ffn_fwd(x, g, shift, scale, gv, w4, wo, tag):
    h = _normmod(x, g, shift, scale, f"normmod_{tag}")
    z = _ffn_up(h, w4, f"ffn_up_{tag}")
    y, xn = _ffn_down(z, wo, x, gv, f"ffn_down_{tag}")
    return xn, (x, h, z, y)


def _ffn_bwd(dxn, saved, g, scale, gv, w4, wo, tag):
    x, h, z, y = saved
    dy, dgv = _resid_bwd(dxn, y, gv, f"resid_bwd_{tag}")
    dz = _ffn_bwd_act(dy, wo, z, f"ffn_bwd_act_{tag}")
    dwo = _dw_ffn_out(z, dy, f"dw_out_{tag}")
    dh = _ffn_dh(dz, w4, f"ffn_dh_{tag}")
    dwi = _dw_ffn_in(h, dz, w4.shape[0], f"dw_in_{tag}")
    dx, dsh, dsc, dg = _normmod_bwd(x, dh, dxn, g, scale, f"normmod_bwd_{tag}")
    return dx, dict(dshift=dsh, dscale=dsc, dgv=dgv, dg=dg, dw_in=dwi, dw_out=dwo.reshape(N_CHIPS, -1, dwo.shape[1]))


def _mix_fwd(x, g, shift, scale, gv, wmix, w_dw, b_dw, g_ln, b_ln, wgp, bgp, gn, wout, tag):
    h = _normmod(x, g, shift, scale, f"normmod_{tag}")
    z = _mm([(h, wmix, 0)], F32, f"mix_in_{tag}")
    u, yc, yconv = _conv_fwd(z, w_dw, b_dw, g_ln, b_ln, f"conv_fwd_{tag}")
    la = _loggate(z, wgp, bgp, f"loggate_{tag}")
    o_raw, sprev, ygla = _gla_fwd(z, la, gn, f"gla_fwd_{tag}")
    y, xn = _mix_out(yconv, ygla, wout, x, gv, f"mix_out_{tag}")
    return xn, (x, h, z, u, yc, la, o_raw, sprev, yconv, ygla, y)


def _mix_bwd(dxn, saved, g, scale, gv, wmix, w_dw, g_ln, b_ln, wgp, bgp, gn, wout, n_cols, n_pad, tag):
    x, h, z, u, yc, la, o_raw, sprev, yconv, ygla, y = saved
    dy, dgv = _resid_bwd(dxn, y, gv, f"resid_bwd_{tag}")
    dycat = _mm([(dy, wout, 0)], F32, f"mix_dycat_{tag}", nt=True)
    dwout = _mm_tn_two(yconv, ygla, dy, f"dw_mixout_{tag}")
    dab, dwdw, dbdw, dgln, dbln = _conv_bwd(dycat, z, u, yc, w_dw, g_ln, b_ln, f"conv_bwd_{tag}")
    dgla, dla, dgn = _gla_bwd(dycat, z, la, o_raw, sprev, gn, f"gla_bwd_{tag}")
    dglr, dwgp, dbgp = _loggate_bwd(dla, z, wgp, wgp.T, bgp, f"loggate_bwd_{tag}")
    dh = _mm([(dgla, wmix, 0), (dab, wmix, ZC_A // (2 * D_CONV)), (dglr, wmix, ZC_G // LANE)], F32, f"mix_dh_{tag}", nt=True)
    dwin = _mix_weight_grad(_mm_tn(h, dgla, f"dw_mixin_gla_{tag}"), _mm_tn(h, dab, f"dw_mixin_conv_{tag}"), _mm_tn(h, dglr, f"dw_mixin_gate_{tag}"),
                            n_cols, n_pad)
    dx, dsh, dsc, dg = _normmod_bwd(x, dh, dxn, g, scale, f"normmod_bwd_{tag}")
    grads = dict(dshift=dsh, dscale=dsc, dgv=dgv, dg=dg, dw_in=dwin, dw_out=dwout.reshape(N_CHIPS, -1, dwout.shape[1]), dw_dw=dwdw, db_dw=dbdw,
                 dg_ln=dgln, db_ln=dbln, dw_gate=_unpad_heads(dwgp[:GATE_RANK]), db_gate=_unpad_heads(dbgp)[0], dgn=dgn)
    return dx, grads


def kernel(x, c, w_ada, b_ada, g_norm_ffn1, w_ffn1_in, w_ffn1_out, g_norm_mix, w_in, w_dw, b_dw, g_conv_ln, b_conv_ln, w_gate_up, b_gate, g_gla_norm, w_out, g_norm_ffn2, w_ffn2_in, w_ffn2_out, g_norm_final, w_ada_final, b_ada_final, loss_target, m_w_ada, m_b_ada, m_g_norm_ffn1, m_w_ffn1_in, m_w_ffn1_out, m_g_norm_mix, m_w_in, m_w_dw, m_b_dw, m_g_conv_ln, m_b_conv_ln, m_w_gate_up, m_b_gate, m_g_gla_norm, m_w_out, m_g_norm_ffn2, m_w_ffn2_in, m_w_ffn2_out, m_g_norm_final, m_w_ada_final, m_b_ada_final, v_w_ada, v_b_ada, v_g_norm_ffn1, v_w_ffn1_in, v_w_ffn1_out, v_g_norm_mix, v_w_in, v_w_dw, v_b_dw, v_g_conv_ln, v_b_conv_ln, v_w_gate_up, v_b_gate, v_g_gla_norm, v_w_out, v_g_norm_ffn2, v_w_ffn2_in, v_w_ffn2_out, v_g_norm_final, v_w_ada_final, v_b_ada_final):
    given = dict(locals())
    W = {n: given[n] for n in _WEIGHTS}
    M1 = {n: given["m_" + n] for n in _WEIGHTS}
    M2 = {n: given["v_" + n] for n in _WEIGHTS}
    xs = x[0]
    tgt = loss_target[0]
    S, D = xs.shape
    L = w_ada.shape[0]
    xi, yi, ci = _place()
    s_me = 2 * xi + yi
    b_me = 4 * xi + 2 * yi + ci
    nsh = w_ada.shape[2]
    nfin = w_ada_final.shape[1]
    n_cols = w_in.shape[2]
    n_pad = -(-n_cols // LANE) * LANE

    c_all = _ag_small(c.reshape(8, D // 8), "ag_c").reshape(8, D)
    parts = [_mod_proj(c_all, w_ada, l, lax.dynamic_slice(b_ada, (l, s_me * nsh), (1, nsh)), f"mod_proj_{l}") for l in range(L)]
    parts.append(_mod_proj(c_all, w_ada_final[None], 0, lax.dynamic_slice(b_ada_final, (s_me * nfin,), (nfin,))[None], "mod_proj_final"))
    mod_all = _ag_small(jnp.concatenate(parts, axis=1), "ag_mod")
    mine = [lax.dynamic_index_in_dim(lax.dynamic_index_in_dim(mod_all, 2 * s + ci, 0, False), b_me, 0, False) for s in range(N_CHIPS)]
    mods = [jnp.concatenate([mine[s][l * nsh:(l + 1) * nsh] for s in range(N_CHIPS)]).reshape(N_MOD, 1, D) for l in range(L)]
    fmod = jnp.concatenate([mine[s][L * nsh:] for s in range(N_CHIPS)]).reshape(2, 1, D)

    tiny = jnp.concatenate([w_dw.reshape(-1), w_gate_up.reshape(-1)])
    tiny_all = _ag_small(jnp.pad(tiny, (0, (-tiny.shape[0]) % (8 * LANE))).reshape(8, -1), "ag_tiny").reshape(8, -1)
    n_dw = w_dw.size
    dw_parts = [lax.dynamic_index_in_dim(tiny_all, 2 * s + ci, 0, False) for s in range(N_CHIPS)]
    w_dw_full = jnp.concatenate([p[:n_dw].reshape(w_dw.shape) for p in dw_parts], axis=2)
    w_gu_full = jnp.concatenate([p[n_dw:n_dw + w_gate_up.size].reshape(w_gate_up.shape) for p in dw_parts], axis=2)

    def layer_weights(l):
        shards = [W[n][l].astype(BF16) for n in _BIG]
        shards[2] = jnp.pad(shards[2], ((0, 0), (0, n_pad - n_cols)))
        wi1, wo1, win4, wout4, wi2, wo2 = _ag_layer(shards, f"ag_weights_l{l}")
        md = mods[l]
        return dict(
            wi1=wi1, wo1=wo1.reshape(-1, D), wi2=wi2, wo2=wo2.reshape(-1, D), wout=wout4.reshape(-1, D), wmix=_mix_weight(win4, n_cols),
            wgp=jnp.pad(_pad_heads(w_gu_full[l]), ((0, LANE - GATE_RANK), (0, 0))).astype(BF16), bgp=_pad_heads(b_gate[l])[None],
            gv1=0.5 * md[2], gv2=md[5], gv3=0.5 * md[8])

    saved, lw = [], []
    xcur = xs
    for l in range(L):
        md = mods[l]
        d = layer_weights(l)
        lw.append(d)
        xcur, s1 = _ffn_fwd(xcur, g_norm_ffn1[l][None], md[0], md[1], d["gv1"], d["wi1"], d["wo1"], f"ffn1_l{l}")
        xcur, s2 = _mix_fwd(xcur, g_norm_mix[l][None], md[3], md[4], d["gv2"], d["wmix"], w_dw_full[l], b_dw[l][None], g_conv_ln[l][None],
                            b_conv_ln[l][None], d["wgp"], d["bgp"], g_gla_norm[l], d["wout"], f"mix_l{l}")
        xcur, s3 = _ffn_fwd(xcur, g_norm_ffn2[l][None], md[6], md[7], d["gv3"], d["wi2"], d["wo2"], f"ffn2_l{l}")
        saved.append((s1, s2, s3))

    c_arr = jnp.stack([ci]).astype(jnp.int32)
    sc_arr = jnp.stack([s_me, ci]).astype(jnp.int32)
    fulls = [lax.empty((L,) + ((W[n].shape[1], n_pad) if n == 'w_in' else W[n].shape[1:]), F32) for n in _BIG]

    def reduce_scatter(gs, l):
        sibs = _rs_sibling(gs, f"rs_sibling_l{l}")
        recvs = _rs_chips([_rs_presum(g, sb, c_arr, f"rs_presum_{i}_l{l}") for i, (g, sb) in enumerate(zip(gs, sibs))], f"rs_chips_l{l}")
        summed = [_rs_sum(g, sb, rv, f, l, sc_arr, f"rs_sum_{i}_l{l}") for i, (g, sb, rv, f) in enumerate(zip(gs, sibs, recvs, fulls))]
        return _rs_share(summed, l, f"rs_share_l{l}")

    dh, sq = _final_loss(xcur, g_norm_final[None], fmod[0], fmod[1], tgt)
    loss_part = 0.5 / D * jnp.sum(sq)
    dx, dfsh, dfsc, dgfin = _normmod_bwd(xcur, dh, None, g_norm_final[None], fmod[1], "normmod_bwd_final")
    G = {n: [None] * L for n in _SMALL}
    dmods = [None] * L
    for l in reversed(range(L)):
        md, d = mods[l], lw[l]
        s1, s2, s3 = saved[l]
        dx, g3 = _ffn_bwd(dx, s3, g_norm_ffn2[l][None], md[7], d["gv3"], d["wi2"], d["wo2"], f"ffn2_l{l}")
        dx, g2 = _mix_bwd(dx, s2, g_norm_mix[l][None], md[4], d["gv2"], d["wmix"], w_dw_full[l], g_conv_ln[l][None], b_conv_ln[l][None],
                          d["wgp"], d["bgp"], g_gla_norm[l], d["wout"], n_cols, n_pad, f"mix_l{l}")
        dx, g1 = _ffn_bwd(dx, s1, g_norm_ffn1[l][None], md[1], d["gv1"], d["wi1"], d["wo1"], f"ffn1_l{l}")
        fulls = reduce_scatter([g1["dw_in"], g1["dw_out"], g2["dw_in"], g2["dw_out"], g3["dw_in"], g3["dw_out"]], l)
        dmods[l] = jnp.concatenate([g1["dshift"], g1["dscale"], 0.5 * g1["dgv"], g2["dshift"], g2["dscale"], g2["dgv"],
                                    g3["dshift"], g3["dscale"], 0.5 * g3["dgv"]], axis=1)[0]
        G["g_norm_ffn1"][l], G["g_norm_ffn2"][l], G["g_norm_mix"][l] = g1["dg"][0], g3["dg"][0], g2["dg"][0]
        G["w_dw"][l], G["b_dw"][l], G["g_conv_ln"][l], G["b_conv_ln"][l] = g2["dw_dw"], g2["db_dw"][0], g2["dg_ln"][0], g2["db_ln"][0]
        G["w_gate_up"][l], G["b_gate"][l], G["g_gla_norm"][l] = g2["dw_gate"], g2["db_gate"], g2["dgn"]
    grad_x = dx[None]
    gsm = {n: (f[:, :, :n_cols] if n == 'w_in' else f) for n, f in zip(_BIG, fulls)}

    small = [jnp.stack(G[n]).reshape(-1) for n in _SMALL if n != 'g_norm_final'] + [dgfin[0]]
    dmod_vec = jnp.concatenate(dmods + [dfsh[0], dfsc[0]])
    n_mod_vec = dmod_vec.shape[0]
    vec = jnp.concatenate([dmod_vec] + small + [loss_part[None]])
    n_vec = vec.shape[0]
    vec = jnp.pad(vec, (0, (-n_vec) % (8 * LANE)))
    vec_all = _ag_small(vec.reshape(8, -1), "ag_small_grads").reshape(8, -1)
    vec_sum = _rowsum(vec_all, "sum_small_grads")[0]
    loss = vec_sum[n_vec - 1]
    off = n_mod_vec
    for n in _SMALL:
        shp = {'w_dw': w_dw_full.shape, 'w_gate_up': w_gu_full.shape}.get(n, W[n].shape)
        cnt = 1
        for dd in shp:
            cnt *= dd
        gsm[n] = vec_sum[off:off + cnt].reshape(shp)
        off += cnt
    gsm['w_dw'] = lax.dynamic_slice_in_dim(gsm['w_dw'], s_me * w_dw.shape[2], w_dw.shape[2], 2)
    gsm['w_gate_up'] = lax.dynamic_slice_in_dim(gsm['w_gate_up'], s_me * w_gate_up.shape[2], w_gate_up.shape[2], 2)
    dmod_sum = vec_sum[:n_mod_vec]
    gsm['b_ada'] = dmod_sum[:L * N_MOD * D].reshape(L, N_MOD * D)
    gsm['b_ada_final'] = dmod_sum[L * N_MOD * D:]
    c_t = c_all.T
    dmod_rows = vec_all[:, :n_mod_vec]
    gsm['w_ada'] = jnp.stack([
        _mod_wgrad(c_t, lax.dynamic_slice_in_dim(dmod_rows, l * N_MOD * D + s_me * nsh, nsh, 1), f"dw_ada_{l}") for l in range(L)])
    gsm['w_ada_final'] = _mod_wgrad(c_t, lax.dynamic_slice_in_dim(dmod_rows, L * N_MOD * D + s_me * nfin, nfin, 1), "dw_ada_final")

    outs = {}
    small_names = [n for n in _WEIGHTS if W[n].size < 65536]
    for n in _WEIGHTS:
        if n in small_names:
            continue
        shp = W[n].shape
        v2 = lambda a: a.reshape(-1, shp[-1])
        d_, m_, v_ = _adamw(v2(W[n]), v2(gsm[n]), v2(M1[n]), v2(M2[n]), f"adamw_{n}")
        outs[n] = (d_.reshape(shp), m_.reshape(shp), v_.reshape(shp))
    flat = lambda dct: jnp.concatenate([dct[n].reshape(-1) for n in small_names])
    n_small = sum(W[n].size for n in small_names)
    v2 = lambda a: jnp.pad(a, (0, (-n_small) % (8 * LANE))).reshape(-1, LANE)
    d_, m_, v_ = _adamw(v2(flat(W)), v2(flat(gsm)), v2(flat(M1)), v2(flat(M2)), "adamw_small")

    def unflat(a):
        res, o = {}, 0
        a = a.reshape(-1)
        for n in small_names:
            res[n] = a[o:o + W[n].size].reshape(W[n].shape)
            o += W[n].size
        return res

    for n, dd, mm, vv in zip(small_names, unflat(d_).values(), unflat(m_).values(), unflat(v_).values()):
        outs[n] = (dd, mm, vv)

    return (loss, grad_x, *[gsm[n] for n in _WEIGHTS], *[outs[n][0] for n in _WEIGHTS], *[outs[n][1] for n in _WEIGHTS], *[outs[n][2] for n in _WEIGHTS])
```

```python
import jax
import jax.numpy as jnp
from jax import lax
from jax.experimental import pallas as pl
from jax.experimental.pallas import tpu as pltpu

F32 = jnp.float32
BF16 = jnp.bfloat16

CHUNK = 64
HEADS = 4
DK = 64
DV = 128
DKP = 128
GATE_RANK = 16
GATE_TAU = 16.0
N_MOD = 9
EPS = 1e-6
ADAM_LR = 0.001
ADAM_B1 = 0.9
ADAM_B2 = 0.999
ADAM_EPS = 1e-08
ADAM_WD = 0.01
ADAM_STEP = 10

LANE = 128
HALO = 32
VMEM_LIMIT = 52 * 1024 * 1024
MESH = pl.DeviceIdType.MESH
N_CHIPS = 4

D_CONV = 512
D_GLA = HEADS * DV
ZC_Q = 0
ZC_K = ZC_Q + HEADS * DKP
ZC_V = ZC_K + HEADS * DKP
ZC_R = ZC_V + D_GLA
ZC_A = ZC_R + D_GLA
ZC_B = ZC_A + D_CONV
ZC_G = ZC_B + D_CONV
Z_COLS = ZC_G + LANE
Z_GLA = ZC_A


def _div(n, target, mult):
    best = None
    d = mult
    while d <= min(n, target):
        if n % d == 0:
            best = d
        d += mult
    return n if best is None else best


def _cp(sem=None, **kw):
    return pltpu.CompilerParams(dimension_semantics=sem, vmem_limit_bytes=VMEM_LIMIT, **kw)


def _resident(shape, index_map):
    return pl.BlockSpec(shape, index_map, pipeline_mode=pl.Buffered(1))


def _sigmoid(x):
    return 0.5 * jnp.tanh(0.5 * x) + 0.5


def _dot(a, b):
    return jnp.dot(a.astype(BF16), b.astype(BF16), preferred_element_type=F32)


def _dot_nt(a, b):
    return lax.dot_general(a.astype(BF16), b.astype(BF16), (((1,), (1,)), ((), ())), preferred_element_type=F32)


def _dot_tn(a, b):
    return lax.dot_general(a.astype(BF16), b.astype(BF16), (((0,), (0,)), ((), ())), preferred_element_type=F32)


def _dot_exact(a, b):
    return jnp.dot(a, b, preferred_element_type=F32, precision=lax.Precision.HIGHEST)


def _normmod(x, g, shift, scale, name):
    S, D = x.shape
    tm = _div(S, 512, 8)

    def body(x_ref, g_ref, sh_ref, sc_ref, o_ref):
        xv = x_ref[...]
        r = lax.rsqrt(jnp.mean(xv * xv, axis=-1, keepdims=True) + EPS)
        o_ref[...] = ((xv * r) * g_ref[...] * (1.0 + sc_ref[...]) + sh_ref[...]).astype(o_ref.dtype)

    row = pl.BlockSpec((tm, D), lambda i: (i, 0))
    vec = pl.BlockSpec((1, D), lambda i: (0, 0))
    return pl.pallas_call(
        body, grid=(S // tm,), in_specs=[row, vec, vec, vec], out_specs=row,
        out_shape=jax.ShapeDtypeStruct((S, D), BF16), compiler_params=_cp(("parallel",)), name=name,
    )(x, g, shift, scale)


def _final_loss(x, g, shift, scale, tgt):
    S, D = x.shape
    tm = _div(S, 512, 8)

    def body(x_ref, g_ref, sh_ref, sc_ref, t_ref, dh_ref, sq_ref):
        @pl.when(pl.program_id(0) == 0)
        def _():
            sq_ref[...] = jnp.zeros_like(sq_ref)

        xv = x_ref[...]
        r = lax.rsqrt(jnp.mean(xv * xv, axis=-1, keepdims=True) + EPS)
        h = (xv * r) * g_ref[...] * (1.0 + sc_ref[...]) + sh_ref[...]
        e = h - t_ref[...]
        dh_ref[...] = e * (1.0 / D)
        sq_ref[...] += jnp.sum(e * e, axis=0, keepdims=True)

    row = pl.BlockSpec((tm, D), lambda i: (i, 0))
    vec = pl.BlockSpec((1, D), lambda i: (0, 0))
    return pl.pallas_call(
        body, grid=(S // tm,), in_specs=[row, vec, vec, vec, row], out_specs=[row, vec],
        out_shape=[jax.ShapeDtypeStruct((S, D), F32), jax.ShapeDtypeStruct((1, D), F32)],
        compiler_params=_cp(("arbitrary",)), name="final_loss",
    )(x, g, shift, scale, tgt)


def _normmod_bwd(x, dh, dres, g, scale, prev, name):
    S, D = x.shape
    tm = _div(S, 512, 8)
    with_res = dres is not None
    with_prev = prev is not None

    def body(*refs):
        refs = list(refs)
        x_ref, dh_ref = refs[:2]
        del refs[:2]
        dr_ref = refs.pop(0) if with_res else None
        g_ref, sc_ref = refs[:2]
        del refs[:2]
        if with_prev:
            y_ref, gvp_ref = refs[:2]
            del refs[:2]
        dx_ref, dsh_ref, dsc_ref, dg_ref = refs[:4]

        @pl.when(pl.program_id(0) == 0)
        def _():
            dsh_ref[...] = jnp.zeros_like(dsh_ref)
            dsc_ref[...] = jnp.zeros_like(dsc_ref)
            dg_ref[...] = jnp.zeros_like(dg_ref)
            if with_prev:
                refs[5][...] = jnp.zeros_like(refs[5])

        xv = x_ref[...]
        dh = dh_ref[...].astype(F32)
        gv = g_ref[...]
        r = lax.rsqrt(jnp.mean(xv * xv, axis=-1, keepdims=True) + EPS)
        xh = xv * r
        dsh_ref[...] += jnp.sum(dh, axis=0, keepdims=True)
        dsc_ref[...] += jnp.sum(dh * (xh * gv), axis=0, keepdims=True)
        dn = dh * (1.0 + sc_ref[...])
        dg_ref[...] += jnp.sum(dn * xh, axis=0, keepdims=True)
        dxh = dn * gv
        dx = r * (dxh - xh * jnp.mean(dxh * xh, axis=-1, keepdims=True))
        if with_res:
            dx = dx + dr_ref[...]
        dx_ref[...] = dx
        if with_prev:
            refs[4][...] = (gvp_ref[...] * dx).astype(BF16)
            refs[5][...] += jnp.sum(dx * y_ref[...], axis=0, keepdims=True)

    row = pl.BlockSpec((tm, D), lambda i: (i, 0))
    vec = pl.BlockSpec((1, D), lambda i: (0, 0))
    ins = [row, row] + [row] * with_res + [vec, vec] + [row, vec] * with_prev
    args = (x, dh) + ((dres,) if with_res else ()) + (g, scale) + (tuple(prev) if with_prev else ())
    vs = jax.ShapeDtypeStruct((1, D), F32)
    return pl.pallas_call(
        body, grid=(S // tm,), in_specs=ins, out_specs=[row, vec, vec, vec] + [row, vec] * with_prev,
        out_shape=[jax.ShapeDtypeStruct((S, D), F32), vs, vs, vs] + [jax.ShapeDtypeStruct((S, D), BF16), vs] * with_prev,
        compiler_params=_cp(("arbitrary",)), name=name,
    )(*args)


def _mm(pairs, out_dtype, name, nt=False):
    M = pairs[0][0].shape[0]
    N = pairs[0][1].shape[0] if nt else pairs[0][1].shape[1]
    ktot = sum(a.shape[1] for a, _, _ in pairs)
    tm = _div(M, 512 if ktot <= 4096 else 256, 8)
    n = len(pairs)

    def body(*refs):
        o_ref = refs[2 * n]
        dot = _dot_nt if nt else _dot
        acc = dot(refs[0][...], refs[1][...])
        for p in range(1, n):
            acc = acc + dot(refs[2 * p][...], refs[2 * p + 1][...])
        o_ref[...] = acc.astype(o_ref.dtype)

    ins, args = [], []
    for a, b, blk in pairs:
        k = a.shape[1]
        ins.append(pl.BlockSpec((tm, k), lambda i: (i, 0)))
        ins.append(_resident((N, k), lambda i, blk=blk: (0, blk)) if nt else _resident((k, N), lambda i: (0, 0)))
        args += [a, b]
    return pl.pallas_call(
        body, grid=(M // tm,), in_specs=ins, out_specs=pl.BlockSpec((tm, N), lambda i: (i, 0)),
        out_shape=jax.ShapeDtypeStruct((M, N), out_dtype), compiler_params=_cp(("parallel",)), name=name,
    )(*args)


def _mm_tn(a, g, name):
    S, Ka = a.shape
    N = g.shape[1]
    tk = _div(Ka, 1408, LANE)
    tn = _div(N, 1408, LANE)
    ts = _div(S, 512, 8)

    def body(a_ref, g_ref, o_ref):
        @pl.when(pl.program_id(2) == 0)
        def _():
            o_ref[...] = jnp.zeros_like(o_ref)

        o_ref[...] += _dot_tn(a_ref[...], g_ref[...])

    return pl.pallas_call(
        body, grid=(Ka // tk, N // tn, S // ts),
        in_specs=[pl.BlockSpec((ts, tk), lambda i, j, s: (s, i)), pl.BlockSpec((ts, tn), lambda i, j, s: (s, j))],
        out_specs=pl.BlockSpec((tk, tn), lambda i, j, s: (i, j)),
        out_shape=jax.ShapeDtypeStruct((Ka, N), F32),
        compiler_params=_cp(("parallel", "parallel", "arbitrary")), name=name,
    )(a, g)


def _mm_tn_two(a0, a1, g, name):
    S, K = a0.shape
    N = g.shape[1]
    ts = _div(S, 512, 8)

    def body(a0_ref, a1_ref, g_ref, o_ref):
        i = pl.program_id(0)

        @pl.when(pl.program_id(1) == 0)
        def _():
            o_ref[...] = jnp.zeros_like(o_ref)

        @pl.when(i == 0)
        def _():
            o_ref[...] += _dot_tn(a0_ref[...], g_ref[...])

        @pl.when(i == 1)
        def _():
            o_ref[...] += _dot_tn(a1_ref[...], g_ref[...])

    return pl.pallas_call(
        body, grid=(2, S // ts),
        in_specs=[pl.BlockSpec((ts, K), lambda i, s: (jnp.where(i == 0, s, 0), 0)),
                  pl.BlockSpec((ts, K), lambda i, s: (jnp.where(i == 1, s, 0), 0)),
                  pl.BlockSpec((ts, N), lambda i, s: (s, 0))],
        out_specs=pl.BlockSpec((K, N), lambda i, s: (i, 0)),
        out_shape=jax.ShapeDtypeStruct((2 * K, N), F32),
        compiler_params=_cp(("parallel", "arbitrary")), name=name,
    )(a0, a1, g)


def _ffn_up(h, w4, name):
    S, D = h.shape
    ns, _, C = w4.shape
    tm = _div(S, 256, 8)

    def body(h_ref, w_ref, z_ref):
        hv = h_ref[...]
        for s in range(ns):
            z_ref[:, s * C:(s + 1) * C] = _dot(hv, w_ref[s]).astype(BF16)

    return pl.pallas_call(
        body, grid=(S // tm,), in_specs=[pl.BlockSpec((tm, D), lambda i: (i, 0)), _resident((ns, D, C), lambda i: (0, 0, 0))],
        out_specs=pl.BlockSpec((tm, ns * C), lambda i: (i, 0)), out_shape=jax.ShapeDtypeStruct((S, ns * C), BF16),
        compiler_params=_cp(("parallel",)), name=name,
    )(h, w4)


def _swiglu(gt, up):
    return gt * _sigmoid(gt) * up


def _norm_rows(xv, g, shift, scale):
    r = lax.rsqrt(jnp.mean(xv * xv, axis=-1, keepdims=True) + EPS)
    return (xv * r) * g * (1.0 + scale) + shift


def _resid_outputs(y, x_ref, gv_ref, nxt_refs, out_refs):
    out_refs[0][...] = y
    xn = x_ref[...] + gv_ref[...] * y
    out_refs[1][...] = xn
    if nxt_refs:
        out_refs[2][...] = _norm_rows(xn, nxt_refs[0][...], nxt_refs[1][...], nxt_refs[2][...]).astype(BF16)


def _ffn_down(z, wo, x, gv, nxt, name):
    S = z.shape[0]
    Fd, D = wo.shape
    tm = _div(S, 256, 8)
    nn = 3 if nxt else 0

    def body(g_ref, u_ref, w_ref, x_ref, gv_ref, *rest):
        act = _swiglu(g_ref[...].astype(F32), u_ref[...].astype(F32))
        _resid_outputs(_dot(act, w_ref[...]), x_ref, gv_ref, rest[:nn], rest[nn:])

    row = pl.BlockSpec((tm, D), lambda i: (i, 0))
    vec = pl.BlockSpec((1, D), lambda i: (0, 0))
    os_ = jax.ShapeDtypeStruct((S, D), F32)
    return pl.pallas_call(
        body, grid=(S // tm,),
        in_specs=[pl.BlockSpec((tm, Fd), lambda i: (i, 0)), pl.BlockSpec((tm, Fd), lambda i: (i, 1)), _resident((Fd, D), lambda i: (0, 0)), row,
                  vec] + [vec] * nn,
        out_specs=[row, row] + [row] * (nn // 3), out_shape=[os_, os_] + [jax.ShapeDtypeStruct((S, D), BF16)] * (nn // 3),
        compiler_params=_cp(("parallel",)), name=name,
    )(z, z, wo, x, gv, *(nxt or ()))


def _ffn_bwd_act(dy, wo, z, name):
    S, D = dy.shape
    Fd = wo.shape[0]
    tm = _div(S, 256, 8)

    def body(dy_ref, w_ref, g_ref, u_ref, dz_ref):
        da = _dot_nt(dy_ref[...], w_ref[...])
        gt = g_ref[...].astype(F32)
        sg = _sigmoid(gt)
        dz_ref[:, Fd:] = (da * gt * sg).astype(BF16)
        dz_ref[:, :Fd] = (da * u_ref[...].astype(F32) * (sg * (1.0 + gt * (1.0 - sg)))).astype(BF16)

    return pl.pallas_call(
        body, grid=(S // tm,),
        in_specs=[pl.BlockSpec((tm, D), lambda i: (i, 0)), _resident((Fd, D), lambda i: (0, 0)),
                  pl.BlockSpec((tm, Fd), lambda i: (i, 0)), pl.BlockSpec((tm, Fd), lambda i: (i, 1))],
        out_specs=pl.BlockSpec((tm, 2 * Fd), lambda i: (i, 0)), out_shape=jax.ShapeDtypeStruct((S, 2 * Fd), BF16),
        compiler_params=_cp(("parallel",)), name=name,
    )(dy, wo, z, z)


def _ffn_dh(dz, w4, name):
    S = dz.shape[0]
    ns, D, C = w4.shape
    tm = _div(S, 256, 8)

    def body(dz_ref, w_ref, o_ref):
        acc = _dot_nt(dz_ref[:, 0:C], w_ref[0])
        for s in range(1, ns):
            acc = acc + _dot_nt(dz_ref[:, s * C:(s + 1) * C], w_ref[s])
        o_ref[...] = acc

    return pl.pallas_call(
        body, grid=(S // tm,), in_specs=[pl.BlockSpec((tm, ns * C), lambda i: (i, 0)), _resident((ns, D, C), lambda i: (0, 0, 0))],
        out_specs=pl.BlockSpec((tm, D), lambda i: (i, 0)), out_shape=jax.ShapeDtypeStruct((S, D), F32),
        compiler_params=_cp(("parallel",)), name=name,
    )(dz, w4)


def _dw_ffn_in(h, dz, ns, name):
    S, D = h.shape
    C = dz.shape[1] // ns
    ts = _div(S, 512, 8)

    def body(h_ref, g_ref, o_ref):
        @pl.when(pl.program_id(1) == 0)
        def _():
            o_ref[...] = jnp.zeros_like(o_ref)

        o_ref[...] += _dot_tn(h_ref[...], g_ref[...])

    return pl.pallas_call(
        body, grid=(ns, S // ts),
        in_specs=[pl.BlockSpec((ts, D), lambda j, s: (s, 0)), pl.BlockSpec((ts, C), lambda j, s: (s, j))],
        out_specs=pl.BlockSpec((None, D, C), lambda j, s: (j, 0, 0)), out_shape=jax.ShapeDtypeStruct((ns, D, C), F32),
        compiler_params=_cp(("parallel", "arbitrary")), name=name,
    )(h, dz)


def _dw_ffn_out(z, dy, name):
    S, D = dy.shape
    Fd = z.shape[1] // 2
    tk = _div(Fd, 1408, LANE)
    nk = Fd // tk
    ts = _div(S, 512, 8)

    def body(g_ref, u_ref, dy_ref, o_ref):
        @pl.when(pl.program_id(1) == 0)
        def _():
            o_ref[...] = jnp.zeros_like(o_ref)

        act = _swiglu(g_ref[...].astype(F32), u_ref[...].astype(F32))
        o_ref[...] += _dot_tn(act, dy_ref[...])

    return pl.pallas_call(
        body, grid=(nk, S // ts),
        in_specs=[pl.BlockSpec((ts, tk), lambda i, s: (s, i)), pl.BlockSpec((ts, tk), lambda i, s: (s, nk + i)), pl.BlockSpec((ts, D), lambda i, s: (s, 0))],
        out_specs=pl.BlockSpec((tk, D), lambda i, s: (i, 0)), out_shape=jax.ShapeDtypeStruct((Fd, D), F32),
        compiler_params=_cp(("parallel", "arbitrary")), name=name,
    )(z, z, dy)


def _mix_out(yconv, ygla, wout, x, gv, nxt, name):
    S, Kc = yconv.shape
    Kg = ygla.shape[1]
    D = wout.shape[1]
    tm = _div(S, 512, 8)
    nn = 3 if nxt else 0

    def body(a_ref, b_ref, w_ref, x_ref, gv_ref, *rest):
        y = _dot(a_ref[...], w_ref[0:Kc, :]) + _dot(b_ref[...], w_ref[Kc:Kc + Kg, :])
        _resid_outputs(y, x_ref, gv_ref, rest[:nn], rest[nn:])

    row = pl.BlockSpec((tm, D), lambda i: (i, 0))
    vec = pl.BlockSpec((1, D), lambda i: (0, 0))
    os_ = jax.ShapeDtypeStruct((S, D), F32)
    return pl.pallas_call(
        body, grid=(S // tm,),
        in_specs=[pl.BlockSpec((tm, Kc), lambda i: (i, 0)), pl.BlockSpec((tm, Kg), lambda i: (i, 0)), _resident((Kc + Kg, D), lambda i: (0, 0)), row,
                  vec] + [vec] * nn,
        out_specs=[row, row] + [row] * (nn // 3), out_shape=[os_, os_] + [jax.ShapeDtypeStruct((S, D), BF16)] * (nn // 3),
        compiler_params=_cp(("parallel",)), name=name,
    )(yconv, ygla, wout, x, gv, *(nxt or ()))


def _ln_parts(yc, g, b):
    mu = jnp.mean(yc, axis=-1, keepdims=True)
    xc = yc - mu
    rs = lax.rsqrt(jnp.mean(xc * xc, axis=-1, keepdims=True) + EPS)
    xh = xc * rs
    return xh, rs, xh * g + b


SUB = 8
CONV_ROWS = 32


def _shifted_copies(ext8, rows):
    for b in range(1, SUB):
        ext8[b, pl.ds(0, rows - SUB), :] = ext8[0, pl.ds(b, rows - SUB), :]


def _tap(o):
    return o % SUB, o - o % SUB


def _conv_fwd(z, w_dw, b_dw, g_ln, b_ln, name):
    S = z.shape[0]
    W, C = w_dw.shape
    ts = _div(S, 512, HALO)
    hb = ts // HALO
    off = HALO - (W - 1)
    ca, cb = ZC_A // C, ZC_B // C
    rb = CONV_ROWS

    def body(a_ref, b_ref, pa_ref, pb_ref, w_ref, bd_ref, g_ref, bl_ref, u_ref, yc_ref, o_ref, ext8):
        keep = (pl.program_id(0) > 0).astype(F32)
        u = a_ref[...] * _sigmoid(b_ref[...])
        ext8[0, pl.ds(0, HALO), :] = pa_ref[...] * _sigmoid(pb_ref[...]) * keep
        ext8[0, pl.ds(HALO, ts), :] = u
        u_ref[...] = u
        _shifted_copies(ext8, ts + HALO)

        def sub(i, carry):
            r0 = pl.multiple_of(i * rb, rb)
            acc = jnp.zeros((rb, C), F32)
            for j in range(W):
                b, a = _tap(off + j)
                acc = acc + w_ref[pl.ds(j, 1), :] * ext8[b, pl.ds(r0 + a, rb), :]
            yc = acc + bd_ref[...]
            yc_ref[pl.ds(r0, rb), :] = yc
            _, _, ln = _ln_parts(yc, g_ref[...], bl_ref[...])
            o_ref[pl.ds(r0, rb), :] = (ln * _sigmoid(ln)).astype(BF16)
            return carry

        lax.fori_loop(0, ts // rb, sub, 0)

    cur = lambda col: pl.BlockSpec((ts, C), lambda i: (i, col))
    prev = lambda col: pl.BlockSpec((HALO, C), lambda i: (jnp.maximum(i * hb - 1, 0), col))
    vec = pl.BlockSpec((1, C), lambda i: (0, 0))
    row = pl.BlockSpec((ts, C), lambda i: (i, 0))
    fs = jax.ShapeDtypeStruct((S, C), F32)
    return pl.pallas_call(
        body, grid=(S // ts,),
        in_specs=[cur(ca), cur(cb), prev(ca), prev(cb), pl.BlockSpec((W, C), lambda i: (0, 0)), vec, vec, vec],
        out_specs=[row, row, row], out_shape=[fs, fs, jax.ShapeDtypeStruct((S, C), BF16)],
        scratch_shapes=[pltpu.VMEM((SUB, ts + HALO, C), F32)],
        compiler_params=_cp(("parallel",)), name=name,
    )(z, z, z, z, w_dw, b_dw, g_ln, b_ln)


def _conv_bwd(dycat, z, u, yc, w_dw, g_ln, b_ln, name):
    S = z.shape[0]
    W, C = w_dw.shape
    ts = _div(S, 512, HALO)
    hb = ts // HALO
    nblk = S // ts
    off = HALO - (W - 1)
    ca, cb = ZC_A // C, ZC_B // C
    rb = CONV_ROWS

    def ln_silu_bwd(dy, ycv, g, b):
        xh, rs, ln = _ln_parts(ycv, g, b)
        sl = _sigmoid(ln)
        dln = dy * (sl * (1.0 + ln * (1.0 - sl)))
        dxh = dln * g
        dyc = rs * (dxh - jnp.mean(dxh, axis=-1, keepdims=True) - xh * jnp.mean(dxh * xh, axis=-1, keepdims=True))
        return dyc, dln, xh

    def body(dy_ref, ndy_ref, yc_ref, nyc_ref, u_ref, pu_ref, a_ref, b_ref, w_ref, g_ref, bl_ref,
             dab_ref, dw_ref, dbd_ref, dg_ref, dbl_ref, uext8, dext8, dwacc):
        i = pl.program_id(0)

        @pl.when(i == 0)
        def _():
            dwacc[...] = jnp.zeros_like(dwacc)
            dbd_ref[...] = jnp.zeros_like(dbd_ref)
            dg_ref[...] = jnp.zeros_like(dg_ref)
            dbl_ref[...] = jnp.zeros_like(dbl_ref)

        g = g_ref[...]
        bl = bl_ref[...]
        dyc, dln, xh = ln_silu_bwd(dy_ref[...], yc_ref[...], g, bl)
        ndyc, _, _ = ln_silu_bwd(ndy_ref[...], nyc_ref[...], g, bl)
        dg_ref[...] += jnp.sum(dln * xh, axis=0, keepdims=True)
        dbl_ref[...] += jnp.sum(dln, axis=0, keepdims=True)
        dbd_ref[...] += jnp.sum(dyc, axis=0, keepdims=True)
        dext8[0, pl.ds(0, ts), :] = dyc
        dext8[0, pl.ds(ts, HALO), :] = ndyc * (i < nblk - 1).astype(F32)
        uext8[0, pl.ds(0, HALO), :] = pu_ref[...] * (i > 0).astype(F32)
        uext8[0, pl.ds(HALO, ts), :] = u_ref[...]
        _shifted_copies(dext8, ts + HALO)
        _shifted_copies(uext8, ts + HALO)

        def sub(k, carry):
            r0 = pl.multiple_of(k * rb, rb)
            rows = pl.ds(r0, rb)
            dyt = dext8[0, rows, :]
            du = jnp.zeros((rb, C), F32)
            for j in range(W):
                b, a = _tap(W - 1 - j)
                du = du + w_ref[pl.ds(j, 1), :] * dext8[b, pl.ds(r0 + a, rb), :]
                b, a = _tap(off + j)
                p = dyt * uext8[b, pl.ds(r0 + a, rb), :]
                part = p[0:SUB]
                for q in range(1, rb // SUB):
                    part = part + p[q * SUB:(q + 1) * SUB]
                dwacc[j] += part
            sb = _sigmoid(b_ref[rows, :])
            dab_ref[rows, 0:C] = (du * sb).astype(BF16)
            dab_ref[rows, C:2 * C] = (du * a_ref[rows, :] * sb * (1.0 - sb)).astype(BF16)
            return carry

        lax.fori_loop(0, ts // rb, sub, 0)

        @pl.when(i == nblk - 1)
        def _():
            for j in range(W):
                dw_ref[pl.ds(j, 1), :] = jnp.sum(dwacc[j], axis=0, keepdims=True)

    row = pl.BlockSpec((ts, C), lambda i: (i, 0))
    nxt = pl.BlockSpec((HALO, C), lambda i: (jnp.minimum((i + 1) * hb, S // HALO - 1), 0))
    prv = pl.BlockSpec((HALO, C), lambda i: (jnp.maximum(i * hb - 1, 0), 0))
    vec = pl.BlockSpec((1, C), lambda i: (0, 0))
    wsp = pl.BlockSpec((W, C), lambda i: (0, 0))
    vs = jax.ShapeDtypeStruct((1, C), F32)
    return pl.pallas_call(
        body, grid=(nblk,),
        in_specs=[row, nxt, row, nxt, row, prv, pl.BlockSpec((ts, C), lambda i: (i, ca)), pl.BlockSpec((ts, C), lambda i: (i, cb)), wsp, vec, vec],
        out_specs=[pl.BlockSpec((ts, 2 * C), lambda i: (i, 0)), wsp, vec, vec, vec],
        out_shape=[jax.ShapeDtypeStruct((S, 2 * C), BF16), jax.ShapeDtypeStruct((W, C), F32), vs, vs, vs],
        scratch_shapes=[pltpu.VMEM((SUB, ts + HALO, C), F32), pltpu.VMEM((SUB, ts + HALO, C), F32), pltpu.VMEM((W, SUB, C), F32)],
        compiler_params=_cp(("arbitrary",)), name=name,
    )(dycat, dycat, yc, yc, u, u, z, z, w_dw, g_ln, b_ln)


def _log_gate(zg):
    return (jnp.minimum(zg, 0.0) - jnp.log(1.0 + jnp.exp(-jnp.abs(zg)))) * (1.0 / GATE_TAU)


def _loggate(z, wgp, bgp, name):
    S = z.shape[0]
    N = wgp.shape[1]
    ts = _div(S, 512, 8)

    def body(g_ref, w_ref, b_ref, o_ref):
        o_ref[...] = _log_gate(_dot(g_ref[...], w_ref[...]) + b_ref[...])

    return pl.pallas_call(
        body, grid=(S // ts,),
        in_specs=[pl.BlockSpec((ts, LANE), lambda i: (i, ZC_G // LANE)), pl.BlockSpec((LANE, N), lambda i: (0, 0)), pl.BlockSpec((1, N), lambda i: (0, 0))],
        out_specs=pl.BlockSpec((ts, N), lambda i: (i, 0)), out_shape=jax.ShapeDtypeStruct((S, N), F32),
        compiler_params=_cp(("parallel",)), name=name,
    )(z, wgp, bgp)


def _loggate_bwd(dla, z, wgp, wgp_t, bgp, name):
    S = z.shape[0]
    N = wgp.shape[1]
    ts = _div(S, 512, 8)

    def body(dla_ref, g_ref, w_ref, wt_ref, b_ref, dg_ref, dw_ref, db_ref):
        @pl.when(pl.program_id(0) == 0)
        def _():
            dw_ref[...] = jnp.zeros_like(dw_ref)
            db_ref[...] = jnp.zeros_like(db_ref)

        glr = g_ref[...]
        zg = _dot(glr, w_ref[...]) + b_ref[...]
        dzg = dla_ref[...] * (1.0 / GATE_TAU) * (1.0 - _sigmoid(zg))
        dg_ref[...] = _dot(dzg, wt_ref[...]).astype(BF16)
        dw_ref[...] += _dot_tn(glr, dzg)
        db_ref[...] += jnp.sum(dzg, axis=0, keepdims=True)

    return pl.pallas_call(
        body, grid=(S // ts,),
        in_specs=[pl.BlockSpec((ts, N), lambda i: (i, 0)), pl.BlockSpec((ts, LANE), lambda i: (i, ZC_G // LANE)),
                  pl.BlockSpec((LANE, N), lambda i: (0, 0)), pl.BlockSpec((N, LANE), lambda i: (0, 0)), pl.BlockSpec((1, N), lambda i: (0, 0))],
        out_specs=[pl.BlockSpec((ts, LANE), lambda i: (i, 0)), pl.BlockSpec((LANE, N), lambda i: (0, 0)), pl.BlockSpec((1, N), lambda i: (0, 0))],
        out_shape=[jax.ShapeDtypeStruct((S, LANE), BF16), jax.ShapeDtypeStruct((LANE, N), F32), jax.ShapeDtypeStruct((1, N), F32)],
        compiler_params=_cp(("arbitrary",)), name=name,
    )(dla, z, wgp, wgp_t, bgp)


def _chunk_fwd_terms(q, k, la, tril):
    bc = _dot_exact(tril, la)
    bend = jnp.sum(la, axis=0, keepdims=True)
    eb = jnp.exp(bc)
    enb = jnp.exp(-bc)
    ee = jnp.exp(bend - bc)
    qs = q * (DK ** -0.5)
    return bend, eb, enb, ee, qs * eb, qs * enb, k * enb, k * eb, k * ee


def _gla_fwd(z, la, gn, name):
    S = z.shape[0]
    W = HEADS * LANE
    tb = _div(S, 512, CHUNK)
    cpb = tb // CHUNK

    def body(q_ref, k_ref, v_ref, r_ref, la_ref, gn_ref, o_ref, sp_ref, y_ref, st):
        @pl.when(pl.program_id(0) == 0)
        def _():
            st[...] = jnp.zeros_like(st)

        ri = lax.broadcasted_iota(jnp.int32, (CHUNK, CHUNK), 0)
        ci = lax.broadcasted_iota(jnp.int32, (CHUNK, CHUNK), 1)
        tri = ri >= ci
        tril = tri.astype(F32)

        def chunk(c, carry):
            rows = pl.ds(pl.multiple_of(c * CHUNK, CHUNK), CHUNK)
            for h in range(HEADS):
                ln = pl.ds(h * LANE, LANE)
                q, k, v, lav = q_ref[rows, ln], k_ref[rows, ln], v_ref[rows, ln], la_ref[rows, ln]
                bend, _, _, _, qf, qb, kb, kf, ke = _chunk_fwd_terms(q, k, lav, tril)
                att = jnp.where(tri, _dot_nt(qf, kb), _dot_nt(qb, kf))
                s_prev = st[h]
                o = _dot(att, v) + _dot_nt(qf, s_prev)
                sp_ref[h, c] = s_prev
                st[h] = s_prev * jnp.exp(bend) + _dot_tn(v, ke)
                o_ref[rows, ln] = o
                rms = lax.rsqrt(jnp.mean(o * o, axis=-1, keepdims=True) + EPS)
                rv = r_ref[rows, ln]
                y_ref[rows, ln] = (o * rms * gn_ref[pl.ds(h, 1), :] * (rv * _sigmoid(rv))).astype(BF16)
            return carry

        lax.fori_loop(0, cpb, chunk, 0, unroll=2)

    zb = lambda base: pl.BlockSpec((tb, W), lambda i: (i, base // W))
    hb_ = pl.BlockSpec((tb, W), lambda i: (i, 0))
    return pl.pallas_call(
        body, grid=(S // tb,),
        in_specs=[zb(ZC_Q), zb(ZC_K), zb(ZC_V), zb(ZC_R), hb_, pl.BlockSpec((HEADS, DV), lambda i: (0, 0))],
        out_specs=[hb_, pl.BlockSpec((HEADS, cpb, DV, DKP), lambda i: (0, i, 0, 0)), hb_],
        out_shape=[jax.ShapeDtypeStruct((S, D_GLA), F32), jax.ShapeDtypeStruct((HEADS, S // CHUNK, DV, DKP), F32),
                   jax.ShapeDtypeStruct((S, D_GLA), BF16)],
        scratch_shapes=[pltpu.VMEM((HEADS, DV, DKP), F32)],
        compiler_params=_cp(("arbitrary",)), name=name,
    )(z, z, z, z, la, gn)


def _gla_bwd(dycat, z, la, o_raw, sprev, gn, name):
    S = z.shape[0]
    W = HEADS * LANE
    tb = _div(S, 512, CHUNK)
    cpb = tb // CHUNK
    nb = S // tb

    def body(q_ref, k_ref, v_ref, r_ref, la_ref, o_ref, sp_ref, dy_ref, gn_ref, dz_ref, dla_ref, dgn_ref, dst):
        @pl.when(pl.program_id(0) == 0)
        def _():
            dst[...] = jnp.zeros_like(dst)
            dgn_ref[...] = jnp.zeros_like(dgn_ref)

        ri = lax.broadcasted_iota(jnp.int32, (CHUNK, CHUNK), 0)
        ci = lax.broadcasted_iota(jnp.int32, (CHUNK, CHUNK), 1)
        tri = ri >= ci
        tril = tri.astype(F32)
        triu = (ri <= ci).astype(F32)

        def chunk(cc, carry):
            c = cpb - 1 - cc
            rows = pl.ds(pl.multiple_of(c * CHUNK, CHUNK), CHUNK)
            for h in range(HEADS):
                ln = pl.ds(h * LANE, LANE)
                q, k, v, lav = q_ref[rows, ln], k_ref[rows, ln], v_ref[rows, ln], la_ref[rows, ln]
                bend, eb, enb, ee, qf, qb, kb, kf, ke = _chunk_fwd_terms(q, k, lav, tril)
                att = jnp.where(tri, _dot_nt(qf, kb), _dot_nt(qb, kf))
                s_prev = sp_ref[h, c]
                gdec = jnp.exp(bend)
                gn = gn_ref[pl.ds(h, 1), :]
                o = o_ref[rows, ln]
                rv = r_ref[rows, ln]
                dy = dy_ref[rows, ln]
                rms = lax.rsqrt(jnp.mean(o * o, axis=-1, keepdims=True) + EPS)
                oh = o * rms
                sg = _sigmoid(rv)
                sr = rv * sg
                dz_ref[rows, pl.ds(ZC_R + h * LANE, LANE)] = (dy * oh * gn * (sg * (1.0 + rv * (1.0 - sg)))).astype(BF16)
                dgn_ref[pl.ds(h, 1), :] += jnp.sum(dy * sr * oh, axis=0, keepdims=True)
                w = dy * sr * gn
                do = rms * (w - oh * jnp.mean(w * oh, axis=-1, keepdims=True))
                datt = _dot_nt(do, v)
                daf = jnp.where(tri, datt, 0.0)
                dab = jnp.where(tri, 0.0, datt)
                ds = dst[h]
                dz_ref[rows, pl.ds(ZC_V + h * LANE, LANE)] = (_dot_tn(att, do) + _dot_nt(ke, ds)).astype(BF16)
                dke = _dot(v, ds)
                dqf = _dot(daf, kb) + _dot(do, s_prev)
                dkb = _dot_tn(daf, qf)
                dqb = _dot(dab, kf)
                dkf = _dot_tn(dab, qb)
                dg = jnp.sum(ds * s_prev, axis=0, keepdims=True)
                dst[h] = ds * gdec + _dot_tn(do, qf)
                dz_ref[rows, pl.ds(ZC_Q + h * LANE, LANE)] = ((dqf * eb + dqb * enb) * (DK ** -0.5)).astype(BF16)
                dz_ref[rows, pl.ds(ZC_K + h * LANE, LANE)] = (dkb * enb + dkf * eb + dke * ee).astype(BF16)
                dbc = dqf * qf - dkb * kb - dqb * qb + dkf * kf - dke * ke
                dbend = jnp.sum(dke * ke, axis=0, keepdims=True) + dg * gdec
                dla_ref[rows, ln] = _dot_exact(triu, dbc) + dbend
            return carry

        lax.fori_loop(0, cpb, chunk, 0, unroll=2)

    zb = lambda base: pl.BlockSpec((tb, W), lambda i: (nb - 1 - i, base // W))
    hb_ = pl.BlockSpec((tb, W), lambda i: (nb - 1 - i, 0))
    return pl.pallas_call(
        body, grid=(nb,),
        in_specs=[zb(ZC_Q), zb(ZC_K), zb(ZC_V), zb(ZC_R), hb_, hb_,
                  pl.BlockSpec((HEADS, cpb, DV, DKP), lambda i: (0, nb - 1 - i, 0, 0)),
                  pl.BlockSpec((tb, W), lambda i: (nb - 1 - i, 1)),
                  pl.BlockSpec((HEADS, DV), lambda i: (0, 0))],
        out_specs=[pl.BlockSpec((tb, Z_GLA), lambda i: (nb - 1 - i, 0)), hb_, pl.BlockSpec((HEADS, DV), lambda i: (0, 0))],
        out_shape=[jax.ShapeDtypeStruct((S, Z_GLA), BF16), jax.ShapeDtypeStruct((S, HEADS * DKP), F32), jax.ShapeDtypeStruct((HEADS, DV), F32)],
        scratch_shapes=[pltpu.VMEM((HEADS, DV, DKP), F32)],
        compiler_params=_cp(("arbitrary",)), name=name,
    )(z, z, z, z, la, o_raw, sprev, dycat, gn)


def _mod_proj(c_all, w3, layer, b, name):
    B, D = c_all.shape
    N = w3.shape[2]
    tn = _div(N, 1024, LANE)

    def body(c_ref, w_ref, b_ref, o_ref):
        cv = c_ref[...]
        o_ref[...] = _dot(cv * _sigmoid(cv), w_ref[...]) + b_ref[...]

    return pl.pallas_call(
        body, grid=(N // tn,),
        in_specs=[pl.BlockSpec((B, D), lambda j: (0, 0)), pl.BlockSpec((None, D, tn), lambda j: (layer, 0, j)), pl.BlockSpec((1, tn), lambda j: (0, j))],
        out_specs=pl.BlockSpec((B, tn), lambda j: (0, j)), out_shape=jax.ShapeDtypeStruct((B, N), F32),
        compiler_params=_cp(("parallel",)), name=name,
    )(c_all, w3, b)


def _mod_wgrad(c_t, dm, name):
    D, B = c_t.shape
    N = dm.shape[1]
    tn = _div(N, 1024, LANE)

    def body(c_ref, d_ref, o_ref):
        cv = c_ref[...]
        ca = cv * _sigmoid(cv)
        acc = ca[:, 0:1] * d_ref[pl.ds(0, 1), :]
        for b in range(1, B):
            acc = acc + ca[:, b:b + 1] * d_ref[pl.ds(b, 1), :]
        o_ref[...] = acc

    return pl.pallas_call(
        body, grid=(N // tn,),
        in_specs=[pl.BlockSpec((D, B), lambda j: (0, 0)), pl.BlockSpec((B, tn), lambda j: (0, j))],
        out_specs=pl.BlockSpec((D, tn), lambda j: (0, j)), out_shape=jax.ShapeDtypeStruct((D, N), F32),
        compiler_params=_cp(("parallel",)), name=name,
    )(c_t, dm)


def _rowsum(xs, name):
    n, N = xs.shape
    tn = _div(N, 8192, LANE)

    def body(x_ref, o_ref):
        acc = x_ref[pl.ds(0, 1), :]
        for r in range(1, n):
            acc = acc + x_ref[pl.ds(r, 1), :]
        o_ref[...] = acc

    return pl.pallas_call(
        body, grid=(N // tn,), in_specs=[pl.BlockSpec((n, tn), lambda j: (0, j))],
        out_specs=pl.BlockSpec((1, tn), lambda j: (0, j)), out_shape=jax.ShapeDtypeStruct((1, N), F32),
        compiler_params=_cp(("parallel",)), name=name,
    )(xs)


def _adamw(w, g, m, v, name):
    R, C = w.shape
    tr = _div(R, max(8, (1 << 18) // C), 8)

    def body(w_ref, g_ref, m_ref, v_ref, d_ref, nm_ref, nv_ref):
        gv = g_ref[...]
        mn = ADAM_B1 * m_ref[...] + (1.0 - ADAM_B1) * gv
        vn = ADAM_B2 * v_ref[...] + (1.0 - ADAM_B2) * (gv * gv)
        m_hat = mn / (1.0 - ADAM_B1 ** ADAM_STEP)
        v_hat = vn / (1.0 - ADAM_B2 ** ADAM_STEP)
        d_ref[...] = -ADAM_LR * (m_hat / (jnp.sqrt(v_hat) + ADAM_EPS) + ADAM_WD * w_ref[...])
        nm_ref[...] = mn
        nv_ref[...] = vn

    blk = pl.BlockSpec((tr, C), lambda i: (i, 0))
    os_ = jax.ShapeDtypeStruct((R, C), F32)
    return pl.pallas_call(
        body, grid=(R // tr,), in_specs=[blk] * 4, out_specs=[blk] * 3, out_shape=[os_] * 3,
        compiler_params=_cp(("parallel",)), name=name,
    )(w, g, m, v)


def _place():
    return lax.axis_index("x"), lax.axis_index("y"), lax.axis_index("c")


def _other_chips(x, y):
    return [(1 - x, y), (x, 1 - y), (1 - x, 1 - y)]


def _half(c, rows):
    return pl.ds(c * (rows // 2), rows // 2)


_ANY = pl.BlockSpec(memory_space=pl.ANY)


def _ag_small(v, name):
    r, n = v.shape

    def body(v_ref, o_ref, send_sems, recv_sems):
        x, y, c = _place()
        me = 4 * x + 2 * y + c
        o_ref[pl.ds(me, 1)] = v_ref[...][None]
        peers = [(x ^ (k >> 2), y ^ ((k >> 1) & 1), c ^ (k & 1)) for k in range(1, 8)]
        copies = []
        for k, peer in enumerate(peers):
            cp = pltpu.make_async_remote_copy(
                src_ref=v_ref, dst_ref=o_ref.at[me], send_sem=send_sems.at[k], recv_sem=recv_sems.at[k],
                device_id=peer, device_id_type=MESH)
            cp.start()
            copies.append(cp)
        for cp in copies:
            cp.wait()

    return pl.pallas_call(
        body, out_shape=jax.ShapeDtypeStruct((8, r, n), v.dtype),
        in_specs=[pl.BlockSpec(memory_space=pltpu.VMEM)], out_specs=pl.BlockSpec(memory_space=pltpu.VMEM),
        scratch_shapes=[pltpu.SemaphoreType.DMA((7,)), pltpu.SemaphoreType.DMA((7,))],
        compiler_params=pltpu.CompilerParams(vmem_limit_bytes=VMEM_LIMIT), name=name,
    )(v)


def _ag_layer(shards, name):
    n = len(shards)

    def body(*refs):
        src, land = refs[:n], refs[n:2 * n]
        send_sems, recv_sems, local_sems = refs[2 * n:]
        x, y, c = _place()
        s_me = 2 * x + y
        chips = _other_chips(x, y)
        sibling = (x, y, 1 - c)
        mine = [pltpu.make_async_copy(src[i], land[i].at[s_me], local_sems.at[i]) for i in range(n)]
        for cp in mine:
            cp.start()

        def copy(k, i, s, h, to, from_src=False):
            rows = _half(h, src[i].shape[0])
            blk = land[i].at[s, rows]
            return pltpu.make_async_remote_copy(
                src_ref=src[i].at[rows] if from_src else blk, dst_ref=blk, send_sem=send_sems.at[k], recv_sem=recv_sems.at[k],
                device_id=to, device_id_type=MESH)

        first = [copy(3 * i + j, i, s_me, c, (*chip, c), from_src=True) for j, chip in enumerate(chips) for i in range(n)]
        for cp in first:
            cp.start()
        passed = []
        for j, (cx, cy) in enumerate(chips):
            for i in range(n):
                copy(3 * i + j, i, 2 * cx + cy, c, sibling).wait_recv()
                fw = copy(3 * n + 3 * i + j, i, 2 * cx + cy, c, sibling)
                fw.start()
                passed.append(fw)
        for j, (cx, cy) in enumerate(chips):
            for i in range(n):
                copy(3 * n + 3 * i + j, i, 2 * cx + cy, 1 - c, sibling).wait_recv()
        for cp in first + passed:
            cp.wait_send()
        for cp in mine:
            cp.wait()

    return pl.pallas_call(
        body, out_shape=[jax.ShapeDtypeStruct((N_CHIPS,) + s.shape, s.dtype) for s in shards],
        in_specs=[_ANY] * n, out_specs=[_ANY] * n,
        scratch_shapes=[pltpu.SemaphoreType.DMA((6 * n,)), pltpu.SemaphoreType.DMA((6 * n,)), pltpu.SemaphoreType.DMA((n,))],
        compiler_params=pltpu.CompilerParams(has_side_effects=True), name=name,
    )(*shards)


def _rs_sibling(gs, name):
    n = len(gs)

    def body(*refs):
        src, out = refs[:n], refs[n:2 * n]
        send_sems, recv_sems = refs[2 * n:]
        x, y, c = _place()
        copies = []
        for i in range(n):
            cp = pltpu.make_async_remote_copy(
                src_ref=src[i].at[:, _half(1 - c, src[i].shape[1])], dst_ref=out[i], send_sem=send_sems.at[i], recv_sem=recv_sems.at[i],
                device_id=(x, y, 1 - c), device_id_type=MESH)
            cp.start()
            copies.append(cp)
        for cp in copies:
            cp.wait()

    return pl.pallas_call(
        body, out_shape=[jax.ShapeDtypeStruct((N_CHIPS, g.shape[1] // 2, g.shape[2]), g.dtype) for g in gs],
        in_specs=[_ANY] * n, out_specs=[_ANY] * n,
        scratch_shapes=[pltpu.SemaphoreType.DMA((n,)), pltpu.SemaphoreType.DMA((n,))],
        compiler_params=pltpu.CompilerParams(has_side_effects=True), name=name,
    )(*gs)


def _rs_presum(g, sib, c_arr, name):
    ns, R, C = g.shape
    rh = R // 2
    tr = _div(rh, max(16, (1 << 19) // C), 16)
    nrb = rh // tr

    def body(c_ref, g_ref, s_ref, o_ref):
        o_ref[...] = (g_ref[...] + s_ref[...]).astype(BF16)

    return pl.pallas_call(
        body, out_shape=jax.ShapeDtypeStruct((ns, rh, C), BF16),
        grid_spec=pltpu.PrefetchScalarGridSpec(
            num_scalar_prefetch=1, grid=(ns, nrb),
            in_specs=[pl.BlockSpec((None, tr, C), lambda s, r, c_ref: (s, c_ref[0] * nrb + r, 0)),
                      pl.BlockSpec((None, tr, C), lambda s, r, c_ref: (s, r, 0))],
            out_specs=pl.BlockSpec((None, tr, C), lambda s, r, c_ref: (s, r, 0))),
        compiler_params=_cp(("parallel", "parallel")), name=name,
    )(c_arr, g, sib)


def _rs_chips(ps, name):
    n = len(ps)

    def body(*refs):
        src, out = refs[:n], refs[n:2 * n]
        send_sems, recv_sems = refs[2 * n:]
        x, y, c = _place()
        copies = []
        for j, (cx, cy) in enumerate(_other_chips(x, y)):
            for i in range(n):
                cp = pltpu.make_async_remote_copy(
                    src_ref=src[i].at[2 * cx + cy], dst_ref=out[i].at[j], send_sem=send_sems.at[3 * i + j], recv_sem=recv_sems.at[3 * i + j],
                    device_id=(cx, cy, c), device_id_type=MESH)
                cp.start()
                copies.append(cp)
        for cp in copies:
            cp.wait()

    return pl.pallas_call(
        body, out_shape=[jax.ShapeDtypeStruct((3,) + p.shape[1:], p.dtype) for p in ps],
        in_specs=[_ANY] * n, out_specs=[_ANY] * n,
        scratch_shapes=[pltpu.SemaphoreType.DMA((3 * n,)), pltpu.SemaphoreType.DMA((3 * n,))],
        compiler_params=pltpu.CompilerParams(has_side_effects=True), name=name,
    )(*ps)


def _rs_sum(g, sib, recv, full, layer, sc_arr, name):
    ns, R, C = g.shape
    rh = R // 2
    tr = _div(rh, max(16, (1 << 18) // C), 16)
    nrb = rh // tr

    def body(sc_ref, g_ref, s_ref, r_ref, f_ref, o_ref):
        acc = g_ref[...] + s_ref[...]
        for j in range(3):
            acc = acc + r_ref[j].astype(F32)
        o_ref[...] = acc

    return pl.pallas_call(
        body, out_shape=jax.ShapeDtypeStruct(full.shape, F32),
        grid_spec=pltpu.PrefetchScalarGridSpec(
            num_scalar_prefetch=1, grid=(nrb,),
            in_specs=[pl.BlockSpec((None, tr, C), lambda r, sc: (sc[0], sc[1] * nrb + r, 0)),
                      pl.BlockSpec((None, tr, C), lambda r, sc: (sc[0], r, 0)),
                      pl.BlockSpec((3, tr, C), lambda r, sc: (0, r, 0)),
                      _ANY],
            out_specs=pl.BlockSpec((None, tr, C), lambda r, sc: (layer, sc[1] * nrb + r, 0))),
        input_output_aliases={4: 0},
        compiler_params=_cp(("parallel",)), name=name,
    )(sc_arr, g, sib, recv, full)


def _rs_share(fulls, layer, name):
    n = len(fulls)

    def body(*refs):
        src, out = refs[:n], refs[n:2 * n]
        send_sems, recv_sems = refs[2 * n:]
        x, y, c = _place()
        copies = []
        for i in range(n):
            rows = out[i].shape[1]
            cp = pltpu.make_async_remote_copy(
                src_ref=out[i].at[layer, _half(c, rows)], dst_ref=out[i].at[layer, _half(c, rows)],
                send_sem=send_sems.at[i], recv_sem=recv_sems.at[i], device_id=(x, y, 1 - c), device_id_type=MESH)
            cp.start()
            copies.append(cp)
        for cp in copies:
            cp.wait()

    return pl.pallas_call(
        body, out_shape=[jax.ShapeDtypeStruct(f.shape, f.dtype) for f in fulls],
        in_specs=[_ANY] * n, out_specs=[_ANY] * n, input_output_aliases={i: i for i in range(n)},
        scratch_shapes=[pltpu.SemaphoreType.DMA((n,)), pltpu.SemaphoreType.DMA((n,))],
        compiler_params=pltpu.CompilerParams(has_side_effects=True), name=name,
    )(*fulls)


_HBM = pl.BlockSpec(memory_space=pltpu.HBM)
_SEM = pl.BlockSpec(memory_space=pltpu.SEMAPHORE)
_EFFECT = pltpu.SideEffectType.DATAFLOW_SIDE_EFFECTING


def _in_hbm(a):
    return pltpu.with_memory_space_constraint(a, pltpu.HBM)


def _split_start(srcs, lands, copies_of, name):
    n = len(srcs)
    n_sem = 3 * n

    def body(*refs):
        for cp in copies_of(refs[:n], refs[n:2 * n], refs[2 * n], refs[2 * n + 1]):
            cp.start()
        refs[-1][...] = jnp.zeros_like(refs[-1])

    thru = [pltpu.HBM(a.shape, a.dtype) for a in list(srcs) + list(lands)]
    out = pl.pallas_call(
        body, name=name,
        out_shape=(pltpu.SemaphoreType.DMA((n_sem,)), pltpu.SemaphoreType.DMA((n_sem,)), *thru, jax.ShapeDtypeStruct((SUB, LANE), F32)),
        in_specs=[_HBM] * (2 * n), out_specs=(_SEM, _SEM, *([_HBM] * (2 * n)), pl.BlockSpec(memory_space=pltpu.VMEM)),
        input_output_aliases={i: 2 + i for i in range(2 * n)},
        compiler_params=pltpu.CompilerParams(has_side_effects=_EFFECT),
    )(*[_in_hbm(a) for a in list(srcs) + list(lands)])
    return out[0], out[1], list(out[2:2 + n]), list(out[2 + n:2 + 2 * n]), out[-1]


def _split_wait(send_sems, recv_sems, srcs, lands, after, copies_of, name):
    n = len(srcs)

    def body(*refs):
        for cp in copies_of(refs[:n], refs[n:2 * n], refs[2 * n], refs[2 * n + 1]):
            cp.wait_send()
            cp.wait_recv()

    out = pl.pallas_call(
        body, name=name, out_shape=[pltpu.HBM(a.shape, a.dtype) for a in list(srcs) + list(lands)],
        in_specs=[_HBM] * (2 * n) + [_SEM, _SEM, _ANY], out_specs=[_HBM] * (2 * n),
        input_output_aliases={i: i for i in range(2 * n)},
        compiler_params=pltpu.CompilerParams(has_side_effects=_EFFECT),
    )(*srcs, *lands, send_sems, recv_sems, after)
    return list(out[:n]), list(out[n:])


def _ag_half_copies(src, land, send_sems, recv_sems, landing_of_mine):
    x, y, c = _place()
    cps = []
    for j, (cx, cy) in enumerate(_other_chips(x, y)):
        for i in range(len(src)):
            rows = _half(c, src[i].shape[0])
            s = 2 * x + y if landing_of_mine else 2 * cx + cy
            cps.append(pltpu.make_async_remote_copy(
                src_ref=src[i].at[rows], dst_ref=land[i].at[s, rows], send_sem=send_sems.at[3 * i + j], recv_sem=recv_sems.at[3 * i + j],
                device_id=(cx, cy, c), device_id_type=MESH))
    return cps


def _ag_starts(src, land, send_sems, recv_sems):
    return _ag_half_copies(src, land, send_sems, recv_sems, True)


def _ag_waits(src, land, send_sems, recv_sems):
    return _ag_half_copies(src, land, send_sems, recv_sems, False)


def _ag_finish(shards, lands, name):
    n = len(shards)

    def body(*refs):
        src, land = refs[:n], refs[2 * n:3 * n]
        send_sems, recv_sems, local_sems = refs[3 * n:]
        x, y, c = _place()
        s_me = 2 * x + y
        sibling = (x, y, 1 - c)
        mine = [pltpu.make_async_copy(src[i], land[i].at[s_me], local_sems.at[i]) for i in range(n)]
        for cp in mine:
            cp.start()

        def copy(k, i, s, h):
            blk = land[i].at[s, _half(h, src[i].shape[0])]
            return pltpu.make_async_remote_copy(
                src_ref=blk, dst_ref=blk, send_sem=send_sems.at[k], recv_sem=recv_sems.at[k], device_id=sibling, device_id_type=MESH)

        chips = _other_chips(x, y)
        passed = [copy(3 * i + j, i, 2 * cx + cy, c) for j, (cx, cy) in enumerate(chips) for i in range(n)]
        for cp in passed:
            cp.start()
        for j, (cx, cy) in enumerate(chips):
            for i in range(n):
                copy(3 * i + j, i, 2 * cx + cy, 1 - c).wait_recv()
        for cp in passed:
            cp.wait_send()
        for cp in mine:
            cp.wait()

    return pl.pallas_call(
        body, out_shape=[jax.ShapeDtypeStruct(a.shape, a.dtype) for a in lands],
        in_specs=[_ANY] * (2 * n), out_specs=[_ANY] * n, input_output_aliases={n + i: i for i in range(n)},
        scratch_shapes=[pltpu.SemaphoreType.DMA((3 * n,)), pltpu.SemaphoreType.DMA((3 * n,)), pltpu.SemaphoreType.DMA((n,))],
        compiler_params=pltpu.CompilerParams(has_side_effects=True), name=name,
    )(*shards, *lands)


def _rs_chip_copies(src, land, send_sems, recv_sems):
    x, y, c = _place()
    return [pltpu.make_async_remote_copy(
        src_ref=src[i].at[2 * cx + cy], dst_ref=land[i].at[j], send_sem=send_sems.at[3 * i + j], recv_sem=recv_sems.at[3 * i + j],
        device_id=(cx, cy, c), device_id_type=MESH) for j, (cx, cy) in enumerate(_other_chips(x, y)) for i in range(len(src))]


def _pad_heads(w):
    lead = w.shape[:-1]
    w4 = w.reshape(*lead, HEADS, DK)
    w4 = jnp.pad(w4, [(0, 0)] * len(lead) + [(0, 0), (0, DKP - DK)])
    return w4.reshape(*lead, HEADS * DKP)


def _unpad_heads(w):
    lead = w.shape[:-1]
    return w.reshape(*lead, HEADS, DKP)[..., :DK].reshape(*lead, HEADS * DK)


def _mix_weight(win4, n_cols):
    D = win4.shape[1]
    w = jnp.transpose(win4[:, :, :n_cols], (1, 0, 2)).reshape(D, N_CHIPS * n_cols)
    o = 2 * D_CONV
    hk = HEADS * DK
    ab = w[:, :o]
    q = _pad_heads(w[:, o:o + hk])
    k = _pad_heads(w[:, o + hk:o + 2 * hk])
    vr = w[:, o + 2 * hk:o + 2 * hk + 2 * D_GLA]
    glr = jnp.pad(w[:, o + 2 * hk + 2 * D_GLA:], ((0, 0), (0, LANE - GATE_RANK)))
    return jnp.concatenate([q, k, vr, ab, glr], axis=1)


def _mix_weight_grad(dgla, dab, dglr, n_cols, n_pad):
    D = dab.shape[0]
    hkp = HEADS * DKP
    w = jnp.concatenate([dab, _unpad_heads(dgla[:, :hkp]), _unpad_heads(dgla[:, hkp:2 * hkp]), dgla[:, 2 * hkp:], dglr[:, :GATE_RANK]], axis=1)
    w = jnp.pad(w.reshape(D, N_CHIPS, n_cols), ((0, 0), (0, 0), (0, n_pad - n_cols)))
    return jnp.transpose(w, (1, 0, 2))


_ARG_NAMES = ['x', 'c', 'w_ada', 'b_ada', 'g_norm_ffn1', 'w_ffn1_in', 'w_ffn1_out', 'g_norm_mix', 'w_in', 'w_dw', 'b_dw', 'g_conv_ln', 'b_conv_ln', 'w_gate_up', 'b_gate', 'g_gla_norm', 'w_out', 'g_norm_ffn2', 'w_ffn2_in', 'w_ffn2_out', 'g_norm_final', 'w_ada_final', 'b_ada_final']
_WEIGHTS = _ARG_NAMES[2:]
_BIG = ('w_ffn1_in', 'w_ffn1_out', 'w_in', 'w_out', 'w_ffn2_in', 'w_ffn2_out')
_SMALL = ('g_norm_ffn1', 'g_norm_mix', 'w_dw', 'b_dw', 'g_conv_ln', 'b_conv_ln', 'w_gate_up', 'b_gate', 'g_gla_norm', 'g_norm_ffn2', 'g_norm_final')


def _ffn_fwd(x, h, gv, w4, wo, nxt, tag):
    z = _ffn_up(h, w4, f"ffn_up_{tag}")
    y, xn, *hn = _ffn_down(z, wo, x, gv, nxt, f"ffn_down_{tag}")
    return xn, (hn[0] if hn else None), y, (x, h, z)


def _ffn_bwd(dxn, dy, saved, g, scale, prev, w4, wo, tag):
    x, h, z = saved
    dz = _ffn_bwd_act(dy, wo, z, f"ffn_bwd_act_{tag}")
    dwo = _dw_ffn_out(z, dy, f"dw_out_{tag}")
    dh = _ffn_dh(dz, w4, f"ffn_dh_{tag}")
    dwi = _dw_ffn_in(h, dz, w4.shape[0], f"dw_in_{tag}")
    dx, dsh, dsc, dg, *pv = _normmod_bwd(x, dh, dxn, g, scale, prev, f"normmod_bwd_{tag}")
    return dx, pv, dict(dshift=dsh, dscale=dsc, dg=dg, dw_in=dwi, dw_out=dwo.reshape(N_CHIPS, -1, dwo.shape[1]))


def _mix_fwd(x, h, gv, wmix, w_dw, b_dw, g_ln, b_ln, wgp, bgp, gn, wout, nxt, tag):
    z = _mm([(h, wmix, 0)], F32, f"mix_in_{tag}")
    u, yc, yconv = _conv_fwd(z, w_dw, b_dw, g_ln, b_ln, f"conv_fwd_{tag}")
    la = _loggate(z, wgp, bgp, f"loggate_{tag}")
    o_raw, sprev, ygla = _gla_fwd(z, la, gn, f"gla_fwd_{tag}")
    y, xn, *hn = _mix_out(yconv, ygla, wout, x, gv, nxt, f"mix_out_{tag}")
    return xn, (hn[0] if hn else None), y, (x, h, z, u, yc, la, o_raw, sprev, yconv, ygla)


def _mix_bwd(dxn, dy, saved, g, scale, prev, wmix, w_dw, g_ln, b_ln, wgp, bgp, gn, wout, n_cols, n_pad, tag):
    x, h, z, u, yc, la, o_raw, sprev, yconv, ygla = saved
    dycat = _mm([(dy, wout, 0)], F32, f"mix_dycat_{tag}", nt=True)
    dwout = _mm_tn_two(yconv, ygla, dy, f"dw_mixout_{tag}")
    dab, dwdw, dbdw, dgln, dbln = _conv_bwd(dycat, z, u, yc, w_dw, g_ln, b_ln, f"conv_bwd_{tag}")
    dgla, dla, dgn = _gla_bwd(dycat, z, la, o_raw, sprev, gn, f"gla_bwd_{tag}")
    dglr, dwgp, dbgp = _loggate_bwd(dla, z, wgp, wgp.T, bgp, f"loggate_bwd_{tag}")
    dh = _mm([(dgla, wmix, 0), (dab, wmix, ZC_A // (2 * D_CONV)), (dglr, wmix, ZC_G // LANE)], F32, f"mix_dh_{tag}", nt=True)
    dwin = _mix_weight_grad(_mm_tn(h, dgla, f"dw_mixin_gla_{tag}"), _mm_tn(h, dab, f"dw_mixin_conv_{tag}"), _mm_tn(h, dglr, f"dw_mixin_gate_{tag}"),
                            n_cols, n_pad)
    dx, dsh, dsc, dg, *pv = _normmod_bwd(x, dh, dxn, g, scale, prev, f"normmod_bwd_{tag}")
    grads = dict(dshift=dsh, dscale=dsc, dg=dg, dw_in=dwin, dw_out=dwout.reshape(N_CHIPS, -1, dwout.shape[1]), dw_dw=dwdw, db_dw=dbdw,
                 dg_ln=dgln, db_ln=dbln, dw_gate=_unpad_heads(dwgp[:GATE_RANK]), db_gate=_unpad_heads(dbgp)[0], dgn=dgn)
    return dx, pv, grads


def kernel(x, c, w_ada, b_ada, g_norm_ffn1, w_ffn1_in, w_ffn1_out, g_norm_mix, w_in, w_dw, b_dw, g_conv_ln, b_conv_ln, w_gate_up, b_gate, g_gla_norm, w_out, g_norm_ffn2, w_ffn2_in, w_ffn2_out, g_norm_final, w_ada_final, b_ada_final, loss_target, m_w_ada, m_b_ada, m_g_norm_ffn1, m_w_ffn1_in, m_w_ffn1_out, m_g_norm_mix, m_w_in, m_w_dw, m_b_dw, m_g_conv_ln, m_b_conv_ln, m_w_gate_up, m_b_gate, m_g_gla_norm, m_w_out, m_g_norm_ffn2, m_w_ffn2_in, m_w_ffn2_out, m_g_norm_final, m_w_ada_final, m_b_ada_final, v_w_ada, v_b_ada, v_g_norm_ffn1, v_w_ffn1_in, v_w_ffn1_out, v_g_norm_mix, v_w_in, v_w_dw, v_b_dw, v_g_conv_ln, v_b_conv_ln, v_w_gate_up, v_b_gate, v_g_gla_norm, v_w_out, v_g_norm_ffn2, v_w_ffn2_in, v_w_ffn2_out, v_g_norm_final, v_w_ada_final, v_b_ada_final):
    given = dict(locals())
    W = {n: given[n] for n in _WEIGHTS}
    M1 = {n: given["m_" + n] for n in _WEIGHTS}
    M2 = {n: given["v_" + n] for n in _WEIGHTS}
    xs = x[0]
    tgt = loss_target[0]
    S, D = xs.shape
    L = w_ada.shape[0]
    xi, yi, ci = _place()
    s_me = 2 * xi + yi
    b_me = 4 * xi + 2 * yi + ci
    nsh = w_ada.shape[2]
    nfin = w_ada_final.shape[1]
    n_cols = w_in.shape[2]
    n_pad = -(-n_cols // LANE) * LANE

    c_all = _ag_small(c.reshape(8, D // 8), "ag_c").reshape(8, D)
    parts = [_mod_proj(c_all, w_ada, l, lax.dynamic_slice(b_ada, (l, s_me * nsh), (1, nsh)), f"mod_proj_{l}") for l in range(L)]
    parts.append(_mod_proj(c_all, w_ada_final[None], 0, lax.dynamic_slice(b_ada_final, (s_me * nfin,), (nfin,))[None], "mod_proj_final"))
    mod_all = _ag_small(jnp.concatenate(parts, axis=1), "ag_mod")
    mine = [lax.dynamic_index_in_dim(lax.dynamic_index_in_dim(mod_all, 2 * s + ci, 0, False), b_me, 0, False) for s in range(N_CHIPS)]
    mods = [jnp.concatenate([mine[s][l * nsh:(l + 1) * nsh] for s in range(N_CHIPS)]).reshape(N_MOD, 1, D) for l in range(L)]
    fmod = jnp.concatenate([mine[s][L * nsh:] for s in range(N_CHIPS)]).reshape(2, 1, D)

    tiny = jnp.concatenate([w_dw.reshape(-1), w_gate_up.reshape(-1)])
    tiny_all = _ag_small(jnp.pad(tiny, (0, (-tiny.shape[0]) % (8 * LANE))).reshape(8, -1), "ag_tiny").reshape(8, -1)
    n_dw = w_dw.size
    dw_parts = [lax.dynamic_index_in_dim(tiny_all, 2 * s + ci, 0, False) for s in range(N_CHIPS)]
    w_dw_full = jnp.concatenate([p[:n_dw].reshape(w_dw.shape) for p in dw_parts], axis=2)
    w_gu_full = jnp.concatenate([p[n_dw:n_dw + w_gate_up.size].reshape(w_gate_up.shape) for p in dw_parts], axis=2)

    def shards_of(l):
        shards = [W[n][l].astype(BF16) for n in _BIG]
        shards[2] = jnp.pad(shards[2], ((0, 0), (0, n_pad - n_cols)))
        return shards

    def layer_weights(l, lands):
        wi1, wo1, win4, wout4, wi2, wo2 = lands
        return dict(
            wi1=wi1, wo1=wo1.reshape(-1, D), wi2=wi2, wo2=wo2.reshape(-1, D), wout=wout4.reshape(-1, D), wmix=_mix_weight(win4, n_cols),
            wgp=jnp.pad(_pad_heads(w_gu_full[l]), ((0, LANE - GATE_RANK), (0, 0))).astype(BF16), bgp=_pad_heads(b_gate[l])[None])

    gnorm = (g_norm_ffn1, g_norm_mix, g_norm_ffn2)
    subs = [dict(l=l, j=j, tag=f"{('ffn1', 'mix', 'ffn2')[j]}_l{l}", g=gnorm[j][l][None], shift=mods[l][3 * j], scale=mods[l][3 * j + 1],
                 gv=mods[l][3 * j + 2] * (1.0 if j == 1 else 0.5)) for l in range(L) for j in range(3)]
    lw = [layer_weights(0, _ag_layer(shards_of(0), "ag_weights_l0"))]
    xcur = xs
    h = None
    for k, sb in enumerate(subs):
        l, j = sb["l"], sb["j"]
        if j == 0:
            if l + 1 < L:
                pend = _split_start(shards_of(l + 1), [lax.empty((N_CHIPS,) + s.shape, BF16) for s in shards_of(l + 1)], _ag_starts, f"ag_start_l{l + 1}")
                sb["g"] = sb["g"] + pend[4][0, 0]
            if l > 0:
                srcs, lands = _split_wait(pend_prev[0], pend_prev[1], pend_prev[2], pend_prev[3], xcur, _ag_waits, f"ag_wait_l{l}")
                lw.append(layer_weights(l, _ag_finish(srcs, lands, f"ag_finish_l{l}")))
            pend_prev = pend if l + 1 < L else None
        if h is None:
            h = _normmod(xcur, sb["g"], sb["shift"], sb["scale"], f"normmod_{sb['tag']}")
        d = lw[l]
        fuse_next = k + 1 < len(subs) and (subs[k + 1]["j"] != 0 or subs[k + 1]["l"] + 1 >= L)
        nxt = (subs[k + 1]["g"], subs[k + 1]["shift"], subs[k + 1]["scale"]) if fuse_next else None
        if j == 1:
            xcur, h, sb["y"], sb["saved"] = _mix_fwd(xcur, h, sb["gv"], d["wmix"], w_dw_full[l], b_dw[l][None], g_conv_ln[l][None],
                                                     b_conv_ln[l][None], d["wgp"], d["bgp"], g_gla_norm[l], d["wout"], nxt, sb["tag"])
        else:
            w4, wo = (d["wi1"], d["wo1"]) if j == 0 else (d["wi2"], d["wo2"])
            xcur, h, sb["y"], sb["saved"] = _ffn_fwd(xcur, h, sb["gv"], w4, wo, nxt, sb["tag"])

    c_arr = jnp.stack([ci]).astype(jnp.int32)
    sc_arr = jnp.stack([s_me, ci]).astype(jnp.int32)
    fulls = [lax.empty((L,) + ((W[n].shape[1], n_pad) if n == 'w_in' else W[n].shape[1:]), F32) for n in _BIG]

    def rs_begin(gs, l):
        sibs = _rs_sibling(gs, f"rs_sibling_l{l}")
        return gs, sibs, [_rs_presum(g, sb_, c_arr, f"rs_presum_{i}_l{l}") for i, (g, sb_) in enumerate(zip(gs, sibs))]

    def rs_end(gs, sibs, recvs, l):
        summed = [_rs_sum(g, sb_, rv, f, l, sc_arr, f"rs_sum_{i}_l{l}") for i, (g, sb_, rv, f) in enumerate(zip(gs, sibs, recvs, fulls))]
        return _rs_share(summed, l, f"rs_share_l{l}")

    dh, sq = _final_loss(xcur, g_norm_final[None], fmod[0], fmod[1], tgt)
    loss_part = 0.5 / D * jnp.sum(sq)
    dx, dfsh, dfsc, dgfin, dy, dgv = _normmod_bwd(xcur, dh, None, g_norm_final[None], fmod[1], (subs[-1]["y"], subs[-1]["gv"]), "normmod_bwd_final")
    G = {n: [None] * L for n in _SMALL}
    dmods = [None] * L
    in_flight = None
    for l in reversed(range(L)):
        gr = [None] * 3
        for j in reversed(range(3)):
            k = 3 * l + j
            sb, d = subs[k], lw[l]
            prev = (subs[k - 1]["y"], subs[k - 1]["gv"]) if k > 0 else None
            g_vec = sb["g"] if not (in_flight is not None and j == 2) else gnorm[2][l][None] + in_flight[6][0, 0]
            if j == 1:
                dx, pv, gr[j] = _mix_bwd(dx, dy, sb["saved"], g_vec, sb["scale"], prev, d["wmix"], w_dw_full[l], g_conv_ln[l][None], b_conv_ln[l][None],
                                         d["wgp"], d["bgp"], g_gla_norm[l], d["wout"], n_cols, n_pad, sb["tag"])
            else:
                w4, wo = (d["wi1"], d["wo1"]) if j == 0 else (d["wi2"], d["wo2"])
                dx, pv, gr[j] = _ffn_bwd(dx, dy, sb["saved"], g_vec, sb["scale"], prev, w4, wo, sb["tag"])
            gr[j]["dgv"] = dgv
            dy, dgv = pv if pv else (None, None)
        g1, g2, g3 = gr
        if in_flight is not None:
            gs_b, sibs_b, pend, l_b = in_flight[0], in_flight[1], in_flight[2:7], in_flight[7]
            fulls = rs_end(gs_b, sibs_b, _split_wait(pend[0], pend[1], pend[2], pend[3], dx, _rs_chip_copies, f"rs_wait_l{l_b}")[1], l_b)
            in_flight = None
        gs, sibs, ps = rs_begin([g1["dw_in"], g1["dw_out"], g2["dw_in"], g2["dw_out"], g3["dw_in"], g3["dw_out"]], l)
        if l > 0:
            pend = _split_start(ps, [lax.empty((3,) + p.shape[1:], BF16) for p in ps], _rs_chip_copies, f"rs_start_l{l}")
            in_flight = (gs, sibs, *pend, l)
        else:
            fulls = rs_end(gs, sibs, _rs_chips(ps, f"rs_chips_l{l}"), l)
        dmods[l] = jnp.concatenate([g1["dshift"], g1["dscale"], 0.5 * g1["dgv"], g2["dshift"], g2["dscale"], g2["dgv"],
                                    g3["dshift"], g3["dscale"], 0.5 * g3["dgv"]], axis=1)[0]
        G["g_norm_ffn1"][l], G["g_norm_ffn2"][l], G["g_norm_mix"][l] = g1["dg"][0], g3["dg"][0], g2["dg"][0]
        G["w_dw"][l], G["b_dw"][l], G["g_conv_ln"][l], G["b_conv_ln"][l] = g2["dw_dw"], g2["db_dw"][0], g2["dg_ln"][0], g2["db_ln"][0]
        G["w_gate_up"][l], G["b_gate"][l], G["g_gla_norm"][l] = g2["dw_gate"], g2["db_gate"], g2["dgn"]
    grad_x = dx[None]
    gsm = {n: (f[:, :, :n_cols] if n == 'w_in' else f) for n, f in zip(_BIG, fulls)}

    small = [jnp.stack(G[n]).reshape(-1) for n in _SMALL if n != 'g_norm_final'] + [dgfin[0]]
    dmod_vec = jnp.concatenate(dmods + [dfsh[0], dfsc[0]])
    n_mod_vec = dmod_vec.shape[0]
    vec = jnp.concatenate([dmod_vec] + small + [loss_part[None]])
    n_vec = vec.shape[0]
    vec = jnp.pad(vec, (0, (-n_vec) % (8 * LANE)))
    vec_all = _ag_small(vec.reshape(8, -1), "ag_small_grads").reshape(8, -1)
    vec_sum = _rowsum(vec_all, "sum_small_grads")[0]
    loss = vec_sum[n_vec - 1]
    off = n_mod_vec
    for n in _SMALL:
        shp = {'w_dw': w_dw_full.shape, 'w_gate_up': w_gu_full.shape}.get(n, W[n].shape)
        cnt = 1
        for dd in shp:
            cnt *= dd
        gsm[n] = vec_sum[off:off + cnt].reshape(shp)
        off += cnt
    gsm['w_dw'] = lax.dynamic_slice_in_dim(gsm['w_dw'], s_me * w_dw.shape[2], w_dw.shape[2], 2)
    gsm['w_gate_up'] = lax.dynamic_slice_in_dim(gsm['w_gate_up'], s_me * w_gate_up.shape[2], w_gate_up.shape[2], 2)
    dmod_sum = vec_sum[:n_mod_vec]
    gsm['b_ada'] = dmod_sum[:L * N_MOD * D].reshape(L, N_MOD * D)
    gsm['b_ada_final'] = dmod_sum[L * N_MOD * D:]
    c_t = c_all.T
    dmod_rows = vec_all[:, :n_mod_vec]
    gsm['w_ada'] = jnp.stack([
        _mod_wgrad(c_t, lax.dynamic_slice_in_dim(dmod_rows, l * N_MOD * D + s_me * nsh, nsh, 1), f"dw_ada_{l}") for l in range(L)])
    gsm['w_ada_final'] = _mod_wgrad(c_t, lax.dynamic_slice_in_dim(dmod_rows, L * N_MOD * D + s_me * nfin, nfin, 1), "dw_ada_final")

    outs = {}
    small_names = [n for n in _WEIGHTS if W[n].size < 65536]
    for n in _WEIGHTS:
        if n in small_names:
            continue
        shp = W[n].shape
        v2 = lambda a: a.reshape(-1, shp[-1])
        d_, m_, v_ = _adamw(v2(W[n]), v2(gsm[n]), v2(M1[n]), v2(M2[n]), f"adamw_{n}")
        outs[n] = (d_.reshape(shp), m_.reshape(shp), v_.reshape(shp))
    flat = lambda dct: jnp.concatenate([dct[n].reshape(-1) for n in small_names])
    n_small = sum(W[n].size for n in small_names)
    v2 = lambda a: jnp.pad(a, (0, (-n_small) % (8 * LANE))).reshape(-1, LANE)
    d_, m_, v_ = _adamw(v2(flat(W)), v2(flat(gsm)), v2(flat(M1)), v2(flat(M2)), "adamw_small")

    def unflat(a):
        res, o = {}, 0
        a = a.reshape(-1)
        for n in small_names:
            res[n] = a[o:o + W[n].size].reshape(W[n].shape)
            o += W[n].size
        return res

    for n, dd, mm, vv in zip(small_names, unflat(d_).values(), unflat(m_).values(), unflat(v_).values()):
        outs[n] = (dd, mm, vv)

    return (loss, grad_x, *[gsm[n] for n in _WEIGHTS], *[outs[n][0] for n in _WEIGHTS], *[outs[n][1] for n in _WEIGHTS], *[outs[n][2] for n in _WEIGHTS])
```

```python
import jax
import jax.numpy as jnp
from jax import lax
from jax.experimental import pallas as pl
from jax.experimental.pallas import tpu as pltpu

F32 = jnp.float32
BF16 = jnp.bfloat16

CHUNK = 64
HEADS = 4
DK = 64
DV = 128
DKP = 128
GATE_RANK = 16
GATE_TAU = 16.0
N_MOD = 9
EPS = 1e-6
ADAM_LR = 0.001
ADAM_B1 = 0.9
ADAM_B2 = 0.999
ADAM_EPS = 1e-08
ADAM_WD = 0.01
ADAM_STEP = 10

LANE = 128
HALO = 32
VMEM_LIMIT = 52 * 1024 * 1024
MESH = pl.DeviceIdType.MESH
N_CHIPS = 4

D_CONV = 512
D_GLA = HEADS * DV
ZC_Q = 0
ZC_K = ZC_Q + HEADS * DKP
ZC_V = ZC_K + HEADS * DKP
ZC_R = ZC_V + D_GLA
ZC_A = ZC_R + D_GLA
ZC_B = ZC_A + D_CONV
ZC_G = ZC_B + D_CONV
Z_COLS = ZC_G + LANE
Z_GLA = ZC_A


def _div(n, target, mult):
    best = None
    d = mult
    while d <= min(n, target):
        if n % d == 0:
            best = d
        d += mult
    return n if best is None else best


def _cp(sem=None, **kw):
    return pltpu.CompilerParams(dimension_semantics=sem, vmem_limit_bytes=VMEM_LIMIT, **kw)


def _resident(shape, index_map):
    return pl.BlockSpec(shape, index_map, pipeline_mode=pl.Buffered(1))


def _sigmoid(x):
    return 0.5 * jnp.tanh(0.5 * x) + 0.5


def _dot(a, b):
    return jnp.dot(a.astype(BF16), b.astype(BF16), preferred_element_type=F32)


def _dot_nt(a, b):
    return lax.dot_general(a.astype(BF16), b.astype(BF16), (((1,), (1,)), ((), ())), preferred_element_type=F32)


def _dot_tn(a, b):
    return lax.dot_general(a.astype(BF16), b.astype(BF16), (((0,), (0,)), ((), ())), preferred_element_type=F32)


def _dot_exact(a, b):
    return jnp.dot(a, b, preferred_element_type=F32, precision=lax.Precision.HIGHEST)


def _normmod(x, g, shift, scale, name):
    S, D = x.shape
    tm = _div(S, 512, 8)

    def body(x_ref, g_ref, sh_ref, sc_ref, o_ref):
        xv = x_ref[...]
        r = lax.rsqrt(jnp.mean(xv * xv, axis=-1, keepdims=True) + EPS)
        o_ref[...] = ((xv * r) * g_ref[...] * (1.0 + sc_ref[...]) + sh_ref[...]).astype(o_ref.dtype)

    row = pl.BlockSpec((tm, D), lambda i: (i, 0))
    vec = pl.BlockSpec((1, D), lambda i: (0, 0))
    return pl.pallas_call(
        body, grid=(S // tm,), in_specs=[row, vec, vec, vec], out_specs=row,
        out_shape=jax.ShapeDtypeStruct((S, D), BF16), compiler_params=_cp(("parallel",)), name=name,
    )(x, g, shift, scale)


def _final_loss(x, g, shift, scale, tgt):
    S, D = x.shape
    tm = _div(S, 512, 8)

    def body(x_ref, g_ref, sh_ref, sc_ref, t_ref, dh_ref, sq_ref):
        @pl.when(pl.program_id(0) == 0)
        def _():
            sq_ref[...] = jnp.zeros_like(sq_ref)

        xv = x_ref[...]
        r = lax.rsqrt(jnp.mean(xv * xv, axis=-1, keepdims=True) + EPS)
        h = (xv * r) * g_ref[...] * (1.0 + sc_ref[...]) + sh_ref[...]
        e = h - t_ref[...]
        dh_ref[...] = e * (1.0 / D)
        sq_ref[...] += jnp.sum(e * e, axis=0, keepdims=True)

    row = pl.BlockSpec((tm, D), lambda i: (i, 0))
    vec = pl.BlockSpec((1, D), lambda i: (0, 0))
    return pl.pallas_call(
        body, grid=(S // tm,), in_specs=[row, vec, vec, vec, row], out_specs=[row, vec],
        out_shape=[jax.ShapeDtypeStruct((S, D), F32), jax.ShapeDtypeStruct((1, D), F32)],
        compiler_params=_cp(("arbitrary",)), name="final_loss",
    )(x, g, shift, scale, tgt)


def _normmod_bwd(x, dh, dres, g, scale, prev, name):
    S, D = x.shape
    tm = _div(S, 512, 8)
    with_res = dres is not None
    with_prev = prev is not None

    def body(*refs):
        refs = list(refs)
        x_ref, dh_ref = refs[:2]
        del refs[:2]
        dr_ref = refs.pop(0) if with_res else None
        g_ref, sc_ref = refs[:2]
        del refs[:2]
        if with_prev:
            y_ref, gvp_ref = refs[:2]
            del refs[:2]
        dx_ref, dsh_ref, dsc_ref, dg_ref = refs[:4]

        @pl.when(pl.program_id(0) == 0)
        def _():
            dsh_ref[...] = jnp.zeros_like(dsh_ref)
            dsc_ref[...] = jnp.zeros_like(dsc_ref)
            dg_ref[...] = jnp.zeros_like(dg_ref)
            if with_prev:
                refs[5][...] = jnp.zeros_like(refs[5])

        xv = x_ref[...]
        dh = dh_ref[...].astype(F32)
        gv = g_ref[...]
        r = lax.rsqrt(jnp.mean(xv * xv, axis=-1, keepdims=True) + EPS)
        xh = xv * r
        dsh_ref[...] += jnp.sum(dh, axis=0, keepdims=True)
        dsc_ref[...] += jnp.sum(dh * (xh * gv), axis=0, keepdims=True)
        dn = dh * (1.0 + sc_ref[...])
        dg_ref[...] += jnp.sum(dn * xh, axis=0, keepdims=True)
        dxh = dn * gv
        dx = r * (dxh - xh * jnp.mean(dxh * xh, axis=-1, keepdims=True))
        if with_res:
            dx = dx + dr_ref[...]
        dx_ref[...] = dx
        if with_prev:
            refs[4][...] = (gvp_ref[...] * dx).astype(BF16)
            refs[5][...] += jnp.sum(dx * y_ref[...], axis=0, keepdims=True)

    row = pl.BlockSpec((tm, D), lambda i: (i, 0))
    vec = pl.BlockSpec((1, D), lambda i: (0, 0))
    ins = [row, row] + [row] * with_res + [vec, vec] + [row, vec] * with_prev
    args = (x, dh) + ((dres,) if with_res else ()) + (g, scale) + (tuple(prev) if with_prev else ())
    vs = jax.ShapeDtypeStruct((1, D), F32)
    return pl.pallas_call(
        body, grid=(S // tm,), in_specs=ins, out_specs=[row, vec, vec, vec] + [row, vec] * with_prev,
        out_shape=[jax.ShapeDtypeStruct((S, D), F32), vs, vs, vs] + [jax.ShapeDtypeStruct((S, D), BF16), vs] * with_prev,
        compiler_params=_cp(("arbitrary",)), name=name,
    )(*args)


def _mm(pairs, out_dtype, name, nt=False):
    M = pairs[0][0].shape[0]
    N = pairs[0][1].shape[0] if nt else pairs[0][1].shape[1]
    ktot = sum(a.shape[1] for a, _, _ in pairs)
    tm = _div(M, 512 if ktot <= 4096 else 256, 8)
    n = len(pairs)

    def body(*refs):
        o_ref = refs[2 * n]
        dot = _dot_nt if nt else _dot
        acc = dot(refs[0][...], refs[1][...])
        for p in range(1, n):
            acc = acc + dot(refs[2 * p][...], refs[2 * p + 1][...])
        o_ref[...] = acc.astype(o_ref.dtype)

    ins, args = [], []
    for a, b, blk in pairs:
        k = a.shape[1]
        ins.append(pl.BlockSpec((tm, k), lambda i: (i, 0)))
        ins.append(_resident((N, k), lambda i, blk=blk: (0, blk)) if nt else _resident((k, N), lambda i: (0, 0)))
        args += [a, b]
    return pl.pallas_call(
        body, grid=(M // tm,), in_specs=ins, out_specs=pl.BlockSpec((tm, N), lambda i: (i, 0)),
        out_shape=jax.ShapeDtypeStruct((M, N), out_dtype), compiler_params=_cp(("parallel",)), name=name,
    )(*args)


def _mm_tn(a, g, name):
    S, Ka = a.shape
    N = g.shape[1]
    tk = _div(Ka, 1408, LANE)
    tn = _div(N, 1408, LANE)
    ts = _div(S, 512, 8)

    def body(a_ref, g_ref, o_ref):
        @pl.when(pl.program_id(2) == 0)
        def _():
            o_ref[...] = jnp.zeros_like(o_ref)

        o_ref[...] += _dot_tn(a_ref[...], g_ref[...])

    return pl.pallas_call(
        body, grid=(Ka // tk, N // tn, S // ts),
        in_specs=[pl.BlockSpec((ts, tk), lambda i, j, s: (s, i)), pl.BlockSpec((ts, tn), lambda i, j, s: (s, j))],
        out_specs=pl.BlockSpec((tk, tn), lambda i, j, s: (i, j)),
        out_shape=jax.ShapeDtypeStruct((Ka, N), F32),
        compiler_params=_cp(("parallel", "parallel", "arbitrary")), name=name,
    )(a, g)


def _mm_tn_two(a0, a1, g, name):
    S, K = a0.shape
    N = g.shape[1]
    ts = _div(S, 512, 8)

    def body(a0_ref, a1_ref, g_ref, o_ref):
        i = pl.program_id(0)

        @pl.when(pl.program_id(1) == 0)
        def _():
            o_ref[...] = jnp.zeros_like(o_ref)

        @pl.when(i == 0)
        def _():
            o_ref[...] += _dot_tn(a0_ref[...], g_ref[...])

        @pl.when(i == 1)
        def _():
            o_ref[...] += _dot_tn(a1_ref[...], g_ref[...])

    return pl.pallas_call(
        body, grid=(2, S // ts),
        in_specs=[pl.BlockSpec((ts, K), lambda i, s: (jnp.where(i == 0, s, 0), 0)),
                  pl.BlockSpec((ts, K), lambda i, s: (jnp.where(i == 1, s, 0), 0)),
                  pl.BlockSpec((ts, N), lambda i, s: (s, 0))],
        out_specs=pl.BlockSpec((K, N), lambda i, s: (i, 0)),
        out_shape=jax.ShapeDtypeStruct((2 * K, N), F32),
        compiler_params=_cp(("parallel", "arbitrary")), name=name,
    )(a0, a1, g)


def _ffn_up(h, w4, name):
    S, D = h.shape
    ns, _, C = w4.shape
    tm = _div(S, 256, 8)

    def body(h_ref, w_ref, z_ref):
        hv = h_ref[...]
        for s in range(ns):
            z_ref[:, s * C:(s + 1) * C] = _dot(hv, w_ref[s]).astype(BF16)

    return pl.pallas_call(
        body, grid=(S // tm,), in_specs=[pl.BlockSpec((tm, D), lambda i: (i, 0)), _resident((ns, D, C), lambda i: (0, 0, 0))],
        out_specs=pl.BlockSpec((tm, ns * C), lambda i: (i, 0)), out_shape=jax.ShapeDtypeStruct((S, ns * C), BF16),
        compiler_params=_cp(("parallel",)), name=name,
    )(h, w4)


def _swiglu(gt, up):
    return gt * _sigmoid(gt) * up


def _norm_rows(xv, g, shift, scale):
    r = lax.rsqrt(jnp.mean(xv * xv, axis=-1, keepdims=True) + EPS)
    return (xv * r) * g * (1.0 + scale) + shift


def _resid_outputs(y, x_ref, gv_ref, nxt_refs, out_refs):
    out_refs[0][...] = y
    xn = x_ref[...] + gv_ref[...] * y
    out_refs[1][...] = xn
    if nxt_refs:
        out_refs[2][...] = _norm_rows(xn, nxt_refs[0][...], nxt_refs[1][...], nxt_refs[2][...]).astype(BF16)


def _ffn_down(z, wo, x, gv, nxt, name):
    S = z.shape[0]
    Fd, D = wo.shape
    tm = _div(S, 256, 8)
    nn = 3 if nxt else 0

    def body(g_ref, u_ref, w_ref, x_ref, gv_ref, *rest):
        act = _swiglu(g_ref[...].astype(F32), u_ref[...].astype(F32))
        _resid_outputs(_dot(act, w_ref[...]), x_ref, gv_ref, rest[:nn], rest[nn:])

    row = pl.BlockSpec((tm, D), lambda i: (i, 0))
    vec = pl.BlockSpec((1, D), lambda i: (0, 0))
    os_ = jax.ShapeDtypeStruct((S, D), F32)
    return pl.pallas_call(
        body, grid=(S // tm,),
        in_specs=[pl.BlockSpec((tm, Fd), lambda i: (i, 0)), pl.BlockSpec((tm, Fd), lambda i: (i, 1)), _resident((Fd, D), lambda i: (0, 0)), row,
                  vec] + [vec] * nn,
        out_specs=[row, row] + [row] * (nn // 3), out_shape=[os_, os_] + [jax.ShapeDtypeStruct((S, D), BF16)] * (nn // 3),
        compiler_params=_cp(("parallel",)), name=name,
    )(z, z, wo, x, gv, *(nxt or ()))


def _ffn_bwd_act(dy, wo, z, name):
    S, D = dy.shape
    Fd = wo.shape[0]
    tm = _div(S, 256, 8)

    def body(dy_ref, w_ref, g_ref, u_ref, dz_ref):
        da = _dot_nt(dy_ref[...], w_ref[...])
        gt = g_ref[...].astype(F32)
        sg = _sigmoid(gt)
        dz_ref[:, Fd:] = (da * gt * sg).astype(BF16)
        dz_ref[:, :Fd] = (da * u_ref[...].astype(F32) * (sg * (1.0 + gt * (1.0 - sg)))).astype(BF16)

    return pl.pallas_call(
        body, grid=(S // tm,),
        in_specs=[pl.BlockSpec((tm, D), lambda i: (i, 0)), _resident((Fd, D), lambda i: (0, 0)),
                  pl.BlockSpec((tm, Fd), lambda i: (i, 0)), pl.BlockSpec((tm, Fd), lambda i: (i, 1))],
        out_specs=pl.BlockSpec((tm, 2 * Fd), lambda i: (i, 0)), out_shape=jax.ShapeDtypeStruct((S, 2 * Fd), BF16),
        compiler_params=_cp(("parallel",)), name=name,
    )(dy, wo, z, z)


def _ffn_dh(dz, w4, name):
    S = dz.shape[0]
    ns, D, C = w4.shape
    tm = _div(S, 256, 8)

    def body(dz_ref, w_ref, o_ref):
        acc = _dot_nt(dz_ref[:, 0:C], w_ref[0])
        for s in range(1, ns):
            acc = acc + _dot_nt(dz_ref[:, s * C:(s + 1) * C], w_ref[s])
        o_ref[...] = acc

    return pl.pallas_call(
        body, grid=(S // tm,), in_specs=[pl.BlockSpec((tm, ns * C), lambda i: (i, 0)), _resident((ns, D, C), lambda i: (0, 0, 0))],
        out_specs=pl.BlockSpec((tm, D), lambda i: (i, 0)), out_shape=jax.ShapeDtypeStruct((S, D), F32),
        compiler_params=_cp(("parallel",)), name=name,
    )(dz, w4)


def _dw_ffn_in(h, dz, ns, name):
    S, D = h.shape
    C = dz.shape[1] // ns
    ts = _div(S, 512, 8)

    def body(h_ref, g_ref, o_ref):
        @pl.when(pl.program_id(1) == 0)
        def _():
            o_ref[...] = jnp.zeros_like(o_ref)

        o_ref[...] += _dot_tn(h_ref[...], g_ref[...])

    return pl.pallas_call(
        body, grid=(ns, S // ts),
        in_specs=[pl.BlockSpec((ts, D), lambda j, s: (s, 0)), pl.BlockSpec((ts, C), lambda j, s: (s, j))],
        out_specs=pl.BlockSpec((None, D, C), lambda j, s: (j, 0, 0)), out_shape=jax.ShapeDtypeStruct((ns, D, C), F32),
        compiler_params=_cp(("parallel", "arbitrary")), name=name,
    )(h, dz)


def _dw_ffn_out(z, dy, name):
    S, D = dy.shape
    Fd = z.shape[1] // 2
    tk = _div(Fd, 1408, LANE)
    nk = Fd // tk
    ts = _div(S, 512, 8)

    def body(g_ref, u_ref, dy_ref, o_ref):
        @pl.when(pl.program_id(1) == 0)
        def _():
            o_ref[...] = jnp.zeros_like(o_ref)

        act = _swiglu(g_ref[...].astype(F32), u_ref[...].astype(F32))
        o_ref[...] += _dot_tn(act, dy_ref[...])

    return pl.pallas_call(
        body, grid=(nk, S // ts),
        in_specs=[pl.BlockSpec((ts, tk), lambda i, s: (s, i)), pl.BlockSpec((ts, tk), lambda i, s: (s, nk + i)), pl.BlockSpec((ts, D), lambda i, s: (s, 0))],
        out_specs=pl.BlockSpec((tk, D), lambda i, s: (i, 0)), out_shape=jax.ShapeDtypeStruct((Fd, D), F32),
        compiler_params=_cp(("parallel", "arbitrary")), name=name,
    )(z, z, dy)


def _mix_out(yconv, ygla, wout, x, gv, nxt, name):
    S, Kc = yconv.shape
    Kg = ygla.shape[1]
    D = wout.shape[1]
    tm = _div(S, 512, 8)
    nn = 3 if nxt else 0

    def body(a_ref, b_ref, w_ref, x_ref, gv_ref, *rest):
        y = _dot(a_ref[...], w_ref[0:Kc, :]) + _dot(b_ref[...], w_ref[Kc:Kc + Kg, :])
        _resid_outputs(y, x_ref, gv_ref, rest[:nn], rest[nn:])

    row = pl.BlockSpec((tm, D), lambda i: (i, 0))
    vec = pl.BlockSpec((1, D), lambda i: (0, 0))
    os_ = jax.ShapeDtypeStruct((S, D), F32)
    return pl.pallas_call(
        body, grid=(S // tm,),
        in_specs=[pl.BlockSpec((tm, Kc), lambda i: (i, 0)), pl.BlockSpec((tm, Kg), lambda i: (i, 0)), _resident((Kc + Kg, D), lambda i: (0, 0)), row,
                  vec] + [vec] * nn,
        out_specs=[row, row] + [row] * (nn // 3), out_shape=[os_, os_] + [jax.ShapeDtypeStruct((S, D), BF16)] * (nn // 3),
        compiler_params=_cp(("parallel",)), name=name,
    )(yconv, ygla, wout, x, gv, *(nxt or ()))


def _ln_parts(yc, g, b):
    mu = jnp.mean(yc, axis=-1, keepdims=True)
    xc = yc - mu
    rs = lax.rsqrt(jnp.mean(xc * xc, axis=-1, keepdims=True) + EPS)
    xh = xc * rs
    return xh, rs, xh * g + b


SUB = 8
CONV_ROWS = 32


def _shifted_copies(ext8, rows):
    for b in range(1, SUB):
        ext8[b, pl.ds(0, rows - SUB), :] = ext8[0, pl.ds(b, rows - SUB), :]


def _tap(o):
    return o % SUB, o - o % SUB


def _conv_fwd(z, w_dw, b_dw, g_ln, b_ln, name):
    S = z.shape[0]
    W, C = w_dw.shape
    ts = _div(S, 512, HALO)
    hb = ts // HALO
    off = HALO - (W - 1)
    ca, cb = ZC_A // C, ZC_B // C
    rb = CONV_ROWS

    def body(a_ref, b_ref, pa_ref, pb_ref, w_ref, bd_ref, g_ref, bl_ref, u_ref, yc_ref, o_ref, ext8):
        keep = (pl.program_id(0) > 0).astype(F32)
        u = a_ref[...] * _sigmoid(b_ref[...])
        ext8[0, pl.ds(0, HALO), :] = pa_ref[...] * _sigmoid(pb_ref[...]) * keep
        ext8[0, pl.ds(HALO, ts), :] = u
        u_ref[...] = u
        _shifted_copies(ext8, ts + HALO)

        def sub(i, carry):
            r0 = pl.multiple_of(i * rb, rb)
            acc = jnp.zeros((rb, C), F32)
            for j in range(W):
                b, a = _tap(off + j)
                acc = acc + w_ref[pl.ds(j, 1), :] * ext8[b, pl.ds(r0 + a, rb), :]
            yc = acc + bd_ref[...]
            yc_ref[pl.ds(r0, rb), :] = yc
            _, _, ln = _ln_parts(yc, g_ref[...], bl_ref[...])
            o_ref[pl.ds(r0, rb), :] = (ln * _sigmoid(ln)).astype(BF16)
            return carry

        lax.fori_loop(0, ts // rb, sub, 0)

    cur = lambda col: pl.BlockSpec((ts, C), lambda i: (i, col))
    prev = lambda col: pl.BlockSpec((HALO, C), lambda i: (jnp.maximum(i * hb - 1, 0), col))
    vec = pl.BlockSpec((1, C), lambda i: (0, 0))
    row = pl.BlockSpec((ts, C), lambda i: (i, 0))
    fs = jax.ShapeDtypeStruct((S, C), F32)
    return pl.pallas_call(
        body, grid=(S // ts,),
        in_specs=[cur(ca), cur(cb), prev(ca), prev(cb), pl.BlockSpec((W, C), lambda i: (0, 0)), vec, vec, vec],
        out_specs=[row, row, row], out_shape=[fs, fs, jax.ShapeDtypeStruct((S, C), BF16)],
        scratch_shapes=[pltpu.VMEM((SUB, ts + HALO, C), F32)],
        compiler_params=_cp(("parallel",)), name=name,
    )(z, z, z, z, w_dw, b_dw, g_ln, b_ln)


def _conv_bwd(dycat, z, u, yc, w_dw, g_ln, b_ln, name):
    S = z.shape[0]
    W, C = w_dw.shape
    ts = _div(S, 512, HALO)
    hb = ts // HALO
    nblk = S // ts
    off = HALO - (W - 1)
    ca, cb = ZC_A // C, ZC_B // C
    rb = CONV_ROWS

    def ln_silu_bwd(dy, ycv, g, b):
        xh, rs, ln = _ln_parts(ycv, g, b)
        sl = _sigmoid(ln)
        dln = dy * (sl * (1.0 + ln * (1.0 - sl)))
        dxh = dln * g
        dyc = rs * (dxh - jnp.mean(dxh, axis=-1, keepdims=True) - xh * jnp.mean(dxh * xh, axis=-1, keepdims=True))
        return dyc, dln, xh

    def body(dy_ref, ndy_ref, yc_ref, nyc_ref, u_ref, pu_ref, a_ref, b_ref, w_ref, g_ref, bl_ref,
             dab_ref, dw_ref, dbd_ref, dg_ref, dbl_ref, uext8, dext8, dwacc):
        i = pl.program_id(0)

        @pl.when(i == 0)
        def _():
            dwacc[...] = jnp.zeros_like(dwacc)
            dbd_ref[...] = jnp.zeros_like(dbd_ref)
            dg_ref[...] = jnp.zeros_like(dg_ref)
            dbl_ref[...] = jnp.zeros_like(dbl_ref)

        g = g_ref[...]
        bl = bl_ref[...]
        dyc, dln, xh = ln_silu_bwd(dy_ref[...], yc_ref[...], g, bl)
        ndyc, _, _ = ln_silu_bwd(ndy_ref[...], nyc_ref[...], g, bl)
        dg_ref[...] += jnp.sum(dln * xh, axis=0, keepdims=True)
        dbl_ref[...] += jnp.sum(dln, axis=0, keepdims=True)
        dbd_ref[...] += jnp.sum(dyc, axis=0, keepdims=True)
        dext8[0, pl.ds(0, ts), :] = dyc
        dext8[0, pl.ds(ts, HALO), :] = ndyc * (i < nblk - 1).astype(F32)
        uext8[0, pl.ds(0, HALO), :] = pu_ref[...] * (i > 0).astype(F32)
        uext8[0, pl.ds(HALO, ts), :] = u_ref[...]
        _shifted_copies(dext8, ts + HALO)
        _shifted_copies(uext8, ts + HALO)

        def sub(k, carry):
            r0 = pl.multiple_of(k * rb, rb)
            rows = pl.ds(r0, rb)
            dyt = dext8[0, rows, :]
            du = jnp.zeros((rb, C), F32)
            for j in range(W):
                b, a = _tap(W - 1 - j)
                du = du + w_ref[pl.ds(j, 1), :] * dext8[b, pl.ds(r0 + a, rb), :]
                b, a = _tap(off + j)
                p = dyt * uext8[b, pl.ds(r0 + a, rb), :]
                part = p[0:SUB]
                for q in range(1, rb // SUB):
                    part = part + p[q * SUB:(q + 1) * SUB]
                dwacc[j] += part
            sb = _sigmoid(b_ref[rows, :])
            dab_ref[rows, 0:C] = (du * sb).astype(BF16)
            dab_ref[rows, C:2 * C] = (du * a_ref[rows, :] * sb * (1.0 - sb)).astype(BF16)
            return carry

        lax.fori_loop(0, ts // rb, sub, 0)

        @pl.when(i == nblk - 1)
        def _():
            for j in range(W):
                dw_ref[pl.ds(j, 1), :] = jnp.sum(dwacc[j], axis=0, keepdims=True)

    row = pl.BlockSpec((ts, C), lambda i: (i, 0))
    nxt = pl.BlockSpec((HALO, C), lambda i: (jnp.minimum((i + 1) * hb, S // HALO - 1), 0))
    prv = pl.BlockSpec((HALO, C), lambda i: (jnp.maximum(i * hb - 1, 0), 0))
    vec = pl.BlockSpec((1, C), lambda i: (0, 0))
    wsp = pl.BlockSpec((W, C), lambda i: (0, 0))
    vs = jax.ShapeDtypeStruct((1, C), F32)
    return pl.pallas_call(
        body, grid=(nblk,),
        in_specs=[row, nxt, row, nxt, row, prv, pl.BlockSpec((ts, C), lambda i: (i, ca)), pl.BlockSpec((ts, C), lambda i: (i, cb)), wsp, vec, vec],
        out_specs=[pl.BlockSpec((ts, 2 * C), lambda i: (i, 0)), wsp, vec, vec, vec],
        out_shape=[jax.ShapeDtypeStruct((S, 2 * C), BF16), jax.ShapeDtypeStruct((W, C), F32), vs, vs, vs],
        scratch_shapes=[pltpu.VMEM((SUB, ts + HALO, C), F32), pltpu.VMEM((SUB, ts + HALO, C), F32), pltpu.VMEM((W, SUB, C), F32)],
        compiler_params=_cp(("arbitrary",)), name=name,
    )(dycat, dycat, yc, yc, u, u, z, z, w_dw, g_ln, b_ln)


def _log_gate(zg):
    return (jnp.minimum(zg, 0.0) - jnp.log(1.0 + jnp.exp(-jnp.abs(zg)))) * (1.0 / GATE_TAU)


def _loggate(z, wgp, bgp, name):
    S = z.shape[0]
    N = wgp.shape[1]
    ts = _div(S, 512, 8)

    def body(g_ref, w_ref, b_ref, o_ref):
        o_ref[...] = _log_gate(_dot(g_ref[...], w_ref[...]) + b_ref[...])

    return pl.pallas_call(
        body, grid=(S // ts,),
        in_specs=[pl.BlockSpec((ts, LANE), lambda i: (i, ZC_G // LANE)), pl.BlockSpec((LANE, N), lambda i: (0, 0)), pl.BlockSpec((1, N), lambda i: (0, 0))],
        out_specs=pl.BlockSpec((ts, N), lambda i: (i, 0)), out_shape=jax.ShapeDtypeStruct((S, N), F32),
        compiler_params=_cp(("parallel",)), name=name,
    )(z, wgp, bgp)


def _loggate_bwd(dla, z, wgp, wgp_t, bgp, name):
    S = z.shape[0]
    N = wgp.shape[1]
    ts = _div(S, 512, 8)

    def body(dla_ref, g_ref, w_ref, wt_ref, b_ref, dg_ref, dw_ref, db_ref):
        @pl.when(pl.program_id(0) == 0)
        def _():
            dw_ref[...] = jnp.zeros_like(dw_ref)
            db_ref[...] = jnp.zeros_like(db_ref)

        glr = g_ref[...]
        zg = _dot(glr, w_ref[...]) + b_ref[...]
        dzg = dla_ref[...] * (1.0 / GATE_TAU) * (1.0 - _sigmoid(zg))
        dg_ref[...] = _dot(dzg, wt_ref[...]).astype(BF16)
        dw_ref[...] += _dot_tn(glr, dzg)
        db_ref[...] += jnp.sum(dzg, axis=0, keepdims=True)

    return pl.pallas_call(
        body, grid=(S // ts,),
        in_specs=[pl.BlockSpec((ts, N), lambda i: (i, 0)), pl.BlockSpec((ts, LANE), lambda i: (i, ZC_G // LANE)),
                  pl.BlockSpec((LANE, N), lambda i: (0, 0)), pl.BlockSpec((N, LANE), lambda i: (0, 0)), pl.BlockSpec((1, N), lambda i: (0, 0))],
        out_specs=[pl.BlockSpec((ts, LANE), lambda i: (i, 0)), pl.BlockSpec((LANE, N), lambda i: (0, 0)), pl.BlockSpec((1, N), lambda i: (0, 0))],
        out_shape=[jax.ShapeDtypeStruct((S, LANE), BF16), jax.ShapeDtypeStruct((LANE, N), F32), jax.ShapeDtypeStruct((1, N), F32)],
        compiler_params=_cp(("arbitrary",)), name=name,
    )(dla, z, wgp, wgp_t, bgp)


def _chunk_fwd_terms(q, k, la, tril):
    bc = _dot_exact(tril, la)
    bend = jnp.sum(la, axis=0, keepdims=True)
    eb = jnp.exp(bc)
    enb = jnp.exp(-bc)
    ee = jnp.exp(bend - bc)
    qs = q * (DK ** -0.5)
    return bend, eb, enb, ee, qs * eb, qs * enb, k * enb, k * eb, k * ee


def _gla_fwd(z, la, gn, name):
    S = z.shape[0]
    W = HEADS * LANE
    tb = _div(S, 512, CHUNK)
    cpb = tb // CHUNK

    def body(q_ref, k_ref, v_ref, r_ref, la_ref, gn_ref, o_ref, sp_ref, y_ref, st):
        @pl.when(pl.program_id(0) == 0)
        def _():
            st[...] = jnp.zeros_like(st)

        ri = lax.broadcasted_iota(jnp.int32, (CHUNK, CHUNK), 0)
        ci = lax.broadcasted_iota(jnp.int32, (CHUNK, CHUNK), 1)
        tri = ri >= ci
        tril = tri.astype(F32)

        def chunk(c, carry):
            rows = pl.ds(pl.multiple_of(c * CHUNK, CHUNK), CHUNK)
            for h in range(HEADS):
                ln = pl.ds(h * LANE, LANE)
                q, k, v, lav = q_ref[rows, ln], k_ref[rows, ln], v_ref[rows, ln], la_ref[rows, ln]
                bend, _, _, _, qf, qb, kb, kf, ke = _chunk_fwd_terms(q, k, lav, tril)
                att = jnp.where(tri, _dot_nt(qf, kb), _dot_nt(qb, kf))
                s_prev = st[h]
                o = _dot(att, v) + _dot_nt(qf, s_prev)
                sp_ref[h, c] = s_prev
                st[h] = s_prev * jnp.exp(bend) + _dot_tn(v, ke)
                o_ref[rows, ln] = o
                rms = lax.rsqrt(jnp.mean(o * o, axis=-1, keepdims=True) + EPS)
                rv = r_ref[rows, ln]
                y_ref[rows, ln] = (o * rms * gn_ref[pl.ds(h, 1), :] * (rv * _sigmoid(rv))).astype(BF16)
            return carry

        lax.fori_loop(0, cpb, chunk, 0, unroll=2)

    zb = lambda base: pl.BlockSpec((tb, W), lambda i: (i, base // W))
    hb_ = pl.BlockSpec((tb, W), lambda i: (i, 0))
    return pl.pallas_call(
        body, grid=(S // tb,),
        in_specs=[zb(ZC_Q), zb(ZC_K), zb(ZC_V), zb(ZC_R), hb_, pl.BlockSpec((HEADS, DV), lambda i: (0, 0))],
        out_specs=[hb_, pl.BlockSpec((HEADS, cpb, DV, DKP), lambda i: (0, i, 0, 0)), hb_],
        out_shape=[jax.ShapeDtypeStruct((S, D_GLA), F32), jax.ShapeDtypeStruct((HEADS, S // CHUNK, DV, DKP), F32),
                   jax.ShapeDtypeStruct((S, D_GLA), BF16)],
        scratch_shapes=[pltpu.VMEM((HEADS, DV, DKP), F32)],
        compiler_params=_cp(("arbitrary",)), name=name,
    )(z, z, z, z, la, gn)


def _gla_bwd(dycat, z, la, o_raw, sprev, gn, name):
    S = z.shape[0]
    W = HEADS * LANE
    tb = _div(S, 512, CHUNK)
    cpb = tb // CHUNK
    nb = S // tb

    def body(q_ref, k_ref, v_ref, r_ref, la_ref, o_ref, sp_ref, dy_ref, gn_ref, dz_ref, dla_ref, dgn_ref, dst):
        @pl.when(pl.program_id(0) == 0)
        def _():
            dst[...] = jnp.zeros_like(dst)
            dgn_ref[...] = jnp.zeros_like(dgn_ref)

        ri = lax.broadcasted_iota(jnp.int32, (CHUNK, CHUNK), 0)
        ci = lax.broadcasted_iota(jnp.int32, (CHUNK, CHUNK), 1)
        tri = ri >= ci
        tril = tri.astype(F32)
        triu = (ri <= ci).astype(F32)

        def chunk(cc, carry):
            c = cpb - 1 - cc
            rows = pl.ds(pl.multiple_of(c * CHUNK, CHUNK), CHUNK)
            for h in range(HEADS):
                ln = pl.ds(h * LANE, LANE)
                q, k, v, lav = q_ref[rows, ln], k_ref[rows, ln], v_ref[rows, ln], la_ref[rows, ln]
                bend, eb, enb, ee, qf, qb, kb, kf, ke = _chunk_fwd_terms(q, k, lav, tril)
                att = jnp.where(tri, _dot_nt(qf, kb), _dot_nt(qb, kf))
                s_prev = sp_ref[h, c]
                gdec = jnp.exp(bend)
                gn = gn_ref[pl.ds(h, 1), :]
                o = o_ref[rows, ln]
                rv = r_ref[rows, ln]
                dy = dy_ref[rows, ln]
                rms = lax.rsqrt(jnp.mean(o * o, axis=-1, keepdims=True) + EPS)
                oh = o * rms
                sg = _sigmoid(rv)
                sr = rv * sg
                dz_ref[rows, pl.ds(ZC_R + h * LANE, LANE)] = (dy * oh * gn * (sg * (1.0 + rv * (1.0 - sg)))).astype(BF16)
                dgn_ref[pl.ds(h, 1), :] += jnp.sum(dy * sr * oh, axis=0, keepdims=True)
                w = dy * sr * gn
                do = rms * (w - oh * jnp.mean(w * oh, axis=-1, keepdims=True))
                datt = _dot_nt(do, v)
                daf = jnp.where(tri, datt, 0.0)
                dab = jnp.where(tri, 0.0, datt)
                ds = dst[h]
                dz_ref[rows, pl.ds(ZC_V + h * LANE, LANE)] = (_dot_tn(att, do) + _dot_nt(ke, ds)).astype(BF16)
                dke = _dot(v, ds)
                dqf = _dot(daf, kb) + _dot(do, s_prev)
                dkb = _dot_tn(daf, qf)
                dqb = _dot(dab, kf)
                dkf = _dot_tn(dab, qb)
                dg = jnp.sum(ds * s_prev, axis=0, keepdims=True)
                dst[h] = ds * gdec + _dot_tn(do, qf)
                dz_ref[rows, pl.ds(ZC_Q + h * LANE, LANE)] = ((dqf * eb + dqb * enb) * (DK ** -0.5)).astype(BF16)
                dz_ref[rows, pl.ds(ZC_K + h * LANE, LANE)] = (dkb * enb + dkf * eb + dke * ee).astype(BF16)
                dbc = dqf * qf - dkb * kb - dqb * qb + dkf * kf - dke * ke
                dbend = jnp.sum(dke * ke, axis=0, keepdims=True) + dg * gdec
                dla_ref[rows, ln] = _dot_exact(triu, dbc) + dbend
            return carry

        lax.fori_loop(0, cpb, chunk, 0, unroll=2)

    zb = lambda base: pl.BlockSpec((tb, W), lambda i: (nb - 1 - i, base // W))
    hb_ = pl.BlockSpec((tb, W), lambda i: (nb - 1 - i, 0))
    return pl.pallas_call(
        body, grid=(nb,),
        in_specs=[zb(ZC_Q), zb(ZC_K), zb(ZC_V), zb(ZC_R), hb_, hb_,
                  pl.BlockSpec((HEADS, cpb, DV, DKP), lambda i: (0, nb - 1 - i, 0, 0)),
                  pl.BlockSpec((tb, W), lambda i: (nb - 1 - i, 1)),
                  pl.BlockSpec((HEADS, DV), lambda i: (0, 0))],
        out_specs=[pl.BlockSpec((tb, Z_GLA), lambda i: (nb - 1 - i, 0)), hb_, pl.BlockSpec((HEADS, DV), lambda i: (0, 0))],
        out_shape=[jax.ShapeDtypeStruct((S, Z_GLA), BF16), jax.ShapeDtypeStruct((S, HEADS * DKP), F32), jax.ShapeDtypeStruct((HEADS, DV), F32)],
        scratch_shapes=[pltpu.VMEM((HEADS, DV, DKP), F32)],
        compiler_params=_cp(("arbitrary",)), name=name,
    )(z, z, z, z, la, o_raw, sprev, dycat, gn)


def _mod_proj(c_all, w3, layer, b, name):
    B, D = c_all.shape
    N = w3.shape[2]
    tn = _div(N, 1024, LANE)

    def body(c_ref, w_ref, b_ref, o_ref):
        cv = c_ref[...]
        o_ref[...] = _dot(cv * _sigmoid(cv), w_ref[...]) + b_ref[...]

    return pl.pallas_call(
        body, grid=(N // tn,),
        in_specs=[pl.BlockSpec((B, D), lambda j: (0, 0)), pl.BlockSpec((None, D, tn), lambda j: (layer, 0, j)), pl.BlockSpec((1, tn), lambda j: (0, j))],
        out_specs=pl.BlockSpec((B, tn), lambda j: (0, j)), out_shape=jax.ShapeDtypeStruct((B, N), F32),
        compiler_params=_cp(("parallel",)), name=name,
    )(c_all, w3, b)


def _mod_wgrad(c_t, dm, name):
    D, B = c_t.shape
    N = dm.shape[1]
    tn = _div(N, 1024, LANE)

    def body(c_ref, d_ref, o_ref):
        cv = c_ref[...]
        ca = cv * _sigmoid(cv)
        acc = ca[:, 0:1] * d_ref[pl.ds(0, 1), :]
        for b in range(1, B):
            acc = acc + ca[:, b:b + 1] * d_ref[pl.ds(b, 1), :]
        o_ref[...] = acc

    return pl.pallas_call(
        body, grid=(N // tn,),
        in_specs=[pl.BlockSpec((D, B), lambda j: (0, 0)), pl.BlockSpec((B, tn), lambda j: (0, j))],
        out_specs=pl.BlockSpec((D, tn), lambda j: (0, j)), out_shape=jax.ShapeDtypeStruct((D, N), F32),
        compiler_params=_cp(("parallel",)), name=name,
    )(c_t, dm)


def _rowsum(xs, name):
    n, N = xs.shape
    tn = _div(N, 8192, LANE)

    def body(x_ref, o_ref):
        acc = x_ref[pl.ds(0, 1), :]
        for r in range(1, n):
            acc = acc + x_ref[pl.ds(r, 1), :]
        o_ref[...] = acc

    return pl.pallas_call(
        body, grid=(N // tn,), in_specs=[pl.BlockSpec((n, tn), lambda j: (0, j))],
        out_specs=pl.BlockSpec((1, tn), lambda j: (0, j)), out_shape=jax.ShapeDtypeStruct((1, N), F32),
        compiler_params=_cp(("parallel",)), name=name,
    )(xs)


def _adamw(w, g, m, v, name):
    R, C = w.shape
    tr = _div(R, max(8, (1 << 18) // C), 8)

    def body(w_ref, g_ref, m_ref, v_ref, d_ref, nm_ref, nv_ref):
        gv = g_ref[...]
        mn = ADAM_B1 * m_ref[...] + (1.0 - ADAM_B1) * gv
        vn = ADAM_B2 * v_ref[...] + (1.0 - ADAM_B2) * (gv * gv)
        m_hat = mn / (1.0 - ADAM_B1 ** ADAM_STEP)
        v_hat = vn / (1.0 - ADAM_B2 ** ADAM_STEP)
        d_ref[...] = -ADAM_LR * (m_hat / (jnp.sqrt(v_hat) + ADAM_EPS) + ADAM_WD * w_ref[...])
        nm_ref[...] = mn
        nv_ref[...] = vn

    blk = pl.BlockSpec((tr, C), lambda i: (i, 0))
    os_ = jax.ShapeDtypeStruct((R, C), F32)
    return pl.pallas_call(
        body, grid=(R // tr,), in_specs=[blk] * 4, out_specs=[blk] * 3, out_shape=[os_] * 3,
        compiler_params=_cp(("parallel",)), name=name,
    )(w, g, m, v)


def _place():
    return lax.axis_index("x"), lax.axis_index("y"), lax.axis_index("c")


def _other_chips(x, y):
    return [(1 - x, y), (x, 1 - y), (1 - x, 1 - y)]


def _half(c, rows):
    return pl.ds(c * (rows // 2), rows // 2)


_ANY = pl.BlockSpec(memory_space=pl.ANY)


def _ag_small(v, name):
    r, n = v.shape

    def body(v_ref, o_ref, send_sems, recv_sems):
        x, y, c = _place()
        me = 4 * x + 2 * y + c
        o_ref[pl.ds(me, 1)] = v_ref[...][None]
        peers = [(x ^ (k >> 2), y ^ ((k >> 1) & 1), c ^ (k & 1)) for k in range(1, 8)]
        copies = []
        for k, peer in enumerate(peers):
            cp = pltpu.make_async_remote_copy(
                src_ref=v_ref, dst_ref=o_ref.at[me], send_sem=send_sems.at[k], recv_sem=recv_sems.at[k],
                device_id=peer, device_id_type=MESH)
            cp.start()
            copies.append(cp)
        for cp in copies:
            cp.wait()

    return pl.pallas_call(
        body, out_shape=jax.ShapeDtypeStruct((8, r, n), v.dtype),
        in_specs=[pl.BlockSpec(memory_space=pltpu.VMEM)], out_specs=pl.BlockSpec(memory_space=pltpu.VMEM),
        scratch_shapes=[pltpu.SemaphoreType.DMA((7,)), pltpu.SemaphoreType.DMA((7,))],
        compiler_params=pltpu.CompilerParams(vmem_limit_bytes=VMEM_LIMIT), name=name,
    )(v)


def _ag_layer(lands, name):
    n = len(lands)

    def body(*refs):
        land = refs[n:2 * n]
        send_sems, recv_sems = refs[2 * n:]
        x, y, c = _place()
        s_me = 2 * x + y
        chips = _other_chips(x, y)
        sibling = (x, y, 1 - c)

        def copy(k, i, s, h, to):
            blk = land[i].at[s, _half(h, land[i].shape[1])]
            return pltpu.make_async_remote_copy(
                src_ref=blk, dst_ref=blk, send_sem=send_sems.at[k], recv_sem=recv_sems.at[k], device_id=to, device_id_type=MESH)

        first = [copy(3 * i + j, i, s_me, c, (*chip, c)) for j, chip in enumerate(chips) for i in range(n)]
        for cp in first:
            cp.start()
        passed = []
        for j, (cx, cy) in enumerate(chips):
            for i in range(n):
                copy(3 * i + j, i, 2 * cx + cy, c, sibling).wait_recv()
                fw = copy(3 * n + 3 * i + j, i, 2 * cx + cy, c, sibling)
                fw.start()
                passed.append(fw)
        for j, (cx, cy) in enumerate(chips):
            for i in range(n):
                copy(3 * n + 3 * i + j, i, 2 * cx + cy, 1 - c, sibling).wait_recv()
        for cp in first + passed:
            cp.wait_send()

    return pl.pallas_call(
        body, out_shape=[jax.ShapeDtypeStruct(a.shape, a.dtype) for a in lands],
        in_specs=[_ANY] * n, out_specs=[_ANY] * n, input_output_aliases={i: i for i in range(n)},
        scratch_shapes=[pltpu.SemaphoreType.DMA((6 * n,)), pltpu.SemaphoreType.DMA((6 * n,))],
        compiler_params=pltpu.CompilerParams(has_side_effects=True), name=name,
    )(*lands)


def _rs_sibling(gs, name):
    n = len(gs)

    def body(*refs):
        src, out = refs[:n], refs[n:2 * n]
        send_sems, recv_sems = refs[2 * n:]
        x, y, c = _place()
        copies = []
        for i in range(n):
            cp = pltpu.make_async_remote_copy(
                src_ref=src[i].at[:, _half(1 - c, src[i].shape[1])], dst_ref=out[i], send_sem=send_sems.at[i], recv_sem=recv_sems.at[i],
                device_id=(x, y, 1 - c), device_id_type=MESH)
            cp.start()
            copies.append(cp)
        for cp in copies:
            cp.wait()

    return pl.pallas_call(
        body, out_shape=[jax.ShapeDtypeStruct((N_CHIPS, g.shape[1] // 2, g.shape[2]), g.dtype) for g in gs],
        in_specs=[_ANY] * n, out_specs=[_ANY] * n,
        scratch_shapes=[pltpu.SemaphoreType.DMA((n,)), pltpu.SemaphoreType.DMA((n,))],
        compiler_params=pltpu.CompilerParams(has_side_effects=True), name=name,
    )(*gs)


def _rs_presum(g, sib, c_arr, name):
    ns, R, C = g.shape
    rh = R // 2
    tr = _div(rh, max(16, (1 << 19) // C), 16)
    nrb = rh // tr

    def body(c_ref, g_ref, s_ref, o_ref):
        o_ref[...] = (g_ref[...] + s_ref[...]).astype(BF16)

    return pl.pallas_call(
        body, out_shape=jax.ShapeDtypeStruct((ns, rh, C), BF16),
        grid_spec=pltpu.PrefetchScalarGridSpec(
            num_scalar_prefetch=1, grid=(ns, nrb),
            in_specs=[pl.BlockSpec((None, tr, C), lambda s, r, c_ref: (s, c_ref[0] * nrb + r, 0)),
                      pl.BlockSpec((None, tr, C), lambda s, r, c_ref: (s, r, 0))],
            out_specs=pl.BlockSpec((None, tr, C), lambda s, r, c_ref: (s, r, 0))),
        compiler_params=_cp(("parallel", "parallel")), name=name,
    )(c_arr, g, sib)


def _rs_chips(ps, name):
    n = len(ps)

    def body(*refs):
        src, out = refs[:n], refs[n:2 * n]
        send_sems, recv_sems = refs[2 * n:]
        x, y, c = _place()
        copies = []
        for j, (cx, cy) in enumerate(_other_chips(x, y)):
            for i in range(n):
                cp = pltpu.make_async_remote_copy(
                    src_ref=src[i].at[2 * cx + cy], dst_ref=out[i].at[j], send_sem=send_sems.at[3 * i + j], recv_sem=recv_sems.at[3 * i + j],
                    device_id=(cx, cy, c), device_id_type=MESH)
                cp.start()
                copies.append(cp)
        for cp in copies:
            cp.wait()

    return pl.pallas_call(
        body, out_shape=[jax.ShapeDtypeStruct((3,) + p.shape[1:], p.dtype) for p in ps],
        in_specs=[_ANY] * n, out_specs=[_ANY] * n,
        scratch_shapes=[pltpu.SemaphoreType.DMA((3 * n,)), pltpu.SemaphoreType.DMA((3 * n,))],
        compiler_params=pltpu.CompilerParams(has_side_effects=True), name=name,
    )(*ps)


def _rs_sum(g, sib, recv, full, layer, sc_arr, name):
    ns, R, C = g.shape
    rh = R // 2
    tr = _div(rh, max(16, (1 << 18) // C), 16)
    nrb = rh // tr

    def body(sc_ref, g_ref, s_ref, r_ref, f_ref, o_ref):
        acc = g_ref[...] + s_ref[...]
        for j in range(3):
            acc = acc + r_ref[j].astype(F32)
        o_ref[...] = acc

    return pl.pallas_call(
        body, out_shape=jax.ShapeDtypeStruct(full.shape, F32),
        grid_spec=pltpu.PrefetchScalarGridSpec(
            num_scalar_prefetch=1, grid=(nrb,),
            in_specs=[pl.BlockSpec((None, tr, C), lambda r, sc: (sc[0], sc[1] * nrb + r, 0)),
                      pl.BlockSpec((None, tr, C), lambda r, sc: (sc[0], r, 0)),
                      pl.BlockSpec((3, tr, C), lambda r, sc: (0, r, 0)),
                      _ANY],
            out_specs=pl.BlockSpec((None, tr, C), lambda r, sc: (layer, sc[1] * nrb + r, 0))),
        input_output_aliases={4: 0},
        compiler_params=_cp(("parallel",)), name=name,
    )(sc_arr, g, sib, recv, full)


def _rs_share(fulls, layer, name):
    n = len(fulls)

    def body(*refs):
        src, out = refs[:n], refs[n:2 * n]
        send_sems, recv_sems = refs[2 * n:]
        x, y, c = _place()
        copies = []
        for i in range(n):
            rows = out[i].shape[1]
            cp = pltpu.make_async_remote_copy(
                src_ref=out[i].at[layer, _half(c, rows)], dst_ref=out[i].at[layer, _half(c, rows)],
                send_sem=send_sems.at[i], recv_sem=recv_sems.at[i], device_id=(x, y, 1 - c), device_id_type=MESH)
            cp.start()
            copies.append(cp)
        for cp in copies:
            cp.wait()

    return pl.pallas_call(
        body, out_shape=[jax.ShapeDtypeStruct(f.shape, f.dtype) for f in fulls],
        in_specs=[_ANY] * n, out_specs=[_ANY] * n, input_output_aliases={i: i for i in range(n)},
        scratch_shapes=[pltpu.SemaphoreType.DMA((n,)), pltpu.SemaphoreType.DMA((n,))],
        compiler_params=pltpu.CompilerParams(has_side_effects=True), name=name,
    )(*fulls)


_HBM = pl.BlockSpec(memory_space=pltpu.HBM)
_SEM = pl.BlockSpec(memory_space=pltpu.SEMAPHORE)
_EFFECT = pltpu.SideEffectType.DATAFLOW_SIDE_EFFECTING


def _in_hbm(a):
    return pltpu.with_memory_space_constraint(a, pltpu.HBM)


def _split_start(bufs, n_sem, copies_of, name):
    nb = len(bufs)

    def body(*refs):
        for cp in copies_of(refs[:nb], refs[nb], refs[nb + 1]):
            cp.start()
        refs[-1][...] = jnp.zeros_like(refs[-1])

    out = pl.pallas_call(
        body, name=name,
        out_shape=(pltpu.SemaphoreType.DMA((n_sem,)), pltpu.SemaphoreType.DMA((n_sem,)), *[pltpu.HBM(a.shape, a.dtype) for a in bufs],
                   jax.ShapeDtypeStruct((SUB, LANE), F32)),
        in_specs=[_HBM] * nb, out_specs=(_SEM, _SEM, *([_HBM] * nb), pl.BlockSpec(memory_space=pltpu.VMEM)),
        input_output_aliases={i: 2 + i for i in range(nb)},
        compiler_params=pltpu.CompilerParams(has_side_effects=_EFFECT),
    )(*[_in_hbm(a) for a in bufs])
    return out[0], out[1], list(out[2:2 + nb]), out[-1]


def _split_wait(send_sems, recv_sems, bufs, after, copies_of, name):
    nb = len(bufs)

    def body(*refs):
        for cp in copies_of(refs[:nb], refs[nb], refs[nb + 1]):
            cp.wait_send()
            cp.wait_recv()

    return list(pl.pallas_call(
        body, name=name, out_shape=[pltpu.HBM(a.shape, a.dtype) for a in bufs],
        in_specs=[_HBM] * nb + [_SEM, _SEM, _ANY], out_specs=[_HBM] * nb,
        input_output_aliases={i: i for i in range(nb)},
        compiler_params=pltpu.CompilerParams(has_side_effects=_EFFECT),
    )(*bufs, send_sems, recv_sems, after))


def _ag_half_copies(land, send_sems, recv_sems, landing_of_mine):
    x, y, c = _place()
    cps = []
    for j, (cx, cy) in enumerate(_other_chips(x, y)):
        for i in range(len(land)):
            rows = _half(c, land[i].shape[1])
            s = 2 * x + y if landing_of_mine else 2 * cx + cy
            cps.append(pltpu.make_async_remote_copy(
                src_ref=land[i].at[2 * x + y, rows], dst_ref=land[i].at[s, rows], send_sem=send_sems.at[3 * i + j], recv_sem=recv_sems.at[3 * i + j],
                device_id=(cx, cy, c), device_id_type=MESH))
    return cps


def _ag_starts(land, send_sems, recv_sems):
    return _ag_half_copies(land, send_sems, recv_sems, True)


def _ag_waits(land, send_sems, recv_sems):
    return _ag_half_copies(land, send_sems, recv_sems, False)


def _ag_finish(lands, name):
    n = len(lands)

    def body(*refs):
        land = refs[n:2 * n]
        send_sems, recv_sems = refs[2 * n:]
        x, y, c = _place()
        sibling = (x, y, 1 - c)

        def copy(k, i, s, h):
            blk = land[i].at[s, _half(h, land[i].shape[1])]
            return pltpu.make_async_remote_copy(
                src_ref=blk, dst_ref=blk, send_sem=send_sems.at[k], recv_sem=recv_sems.at[k], device_id=sibling, device_id_type=MESH)

        chips = _other_chips(x, y)
        passed = [copy(3 * i + j, i, 2 * cx + cy, c) for j, (cx, cy) in enumerate(chips) for i in range(n)]
        for cp in passed:
            cp.start()
        for j, (cx, cy) in enumerate(chips):
            for i in range(n):
                copy(3 * i + j, i, 2 * cx + cy, 1 - c).wait_recv()
        for cp in passed:
            cp.wait_send()

    return pl.pallas_call(
        body, out_shape=[jax.ShapeDtypeStruct(a.shape, a.dtype) for a in lands],
        in_specs=[_ANY] * n, out_specs=[_ANY] * n, input_output_aliases={i: i for i in range(n)},
        scratch_shapes=[pltpu.SemaphoreType.DMA((3 * n,)), pltpu.SemaphoreType.DMA((3 * n,))],
        compiler_params=pltpu.CompilerParams(has_side_effects=True), name=name,
    )(*lands)


def _rs_chip_copies(bufs, send_sems, recv_sems):
    n = len(bufs) // 2
    x, y, c = _place()
    return [pltpu.make_async_remote_copy(
        src_ref=bufs[i].at[2 * cx + cy], dst_ref=bufs[n + i].at[j], send_sem=send_sems.at[3 * i + j], recv_sem=recv_sems.at[3 * i + j],
        device_id=(cx, cy, c), device_id_type=MESH) for j, (cx, cy) in enumerate(_other_chips(x, y)) for i in range(n)]


def _pad_heads(w):
    lead = w.shape[:-1]
    w4 = w.reshape(*lead, HEADS, DK)
    w4 = jnp.pad(w4, [(0, 0)] * len(lead) + [(0, 0), (0, DKP - DK)])
    return w4.reshape(*lead, HEADS * DKP)


def _unpad_heads(w):
    lead = w.shape[:-1]
    return w.reshape(*lead, HEADS, DKP)[..., :DK].reshape(*lead, HEADS * DK)


def _mix_weight(win4, n_cols):
    D = win4.shape[1]
    w = jnp.transpose(win4[:, :, :n_cols], (1, 0, 2)).reshape(D, N_CHIPS * n_cols)
    o = 2 * D_CONV
    hk = HEADS * DK
    ab = w[:, :o]
    q = _pad_heads(w[:, o:o + hk])
    k = _pad_heads(w[:, o + hk:o + 2 * hk])
    vr = w[:, o + 2 * hk:o + 2 * hk + 2 * D_GLA]
    glr = jnp.pad(w[:, o + 2 * hk + 2 * D_GLA:], ((0, 0), (0, LANE - GATE_RANK)))
    return jnp.concatenate([q, k, vr, ab, glr], axis=1)


def _mix_weight_grad(dgla, dab, dglr, n_cols, n_pad):
    D = dab.shape[0]
    hkp = HEADS * DKP
    w = jnp.concatenate([dab, _unpad_heads(dgla[:, :hkp]), _unpad_heads(dgla[:, hkp:2 * hkp]), dgla[:, 2 * hkp:], dglr[:, :GATE_RANK]], axis=1)
    w = jnp.pad(w.reshape(D, N_CHIPS, n_cols), ((0, 0), (0, 0), (0, n_pad - n_cols)))
    return jnp.transpose(w, (1, 0, 2))


_ARG_NAMES = ['x', 'c', 'w_ada', 'b_ada', 'g_norm_ffn1', 'w_ffn1_in', 'w_ffn1_out', 'g_norm_mix', 'w_in', 'w_dw', 'b_dw', 'g_conv_ln', 'b_conv_ln', 'w_gate_up', 'b_gate', 'g_gla_norm', 'w_out', 'g_norm_ffn2', 'w_ffn2_in', 'w_ffn2_out', 'g_norm_final', 'w_ada_final', 'b_ada_final']
_WEIGHTS = _ARG_NAMES[2:]
_BIG = ('w_ffn1_in', 'w_ffn1_out', 'w_in', 'w_out', 'w_ffn2_in', 'w_ffn2_out')
_SMALL = ('g_norm_ffn1', 'g_norm_mix', 'w_dw', 'b_dw', 'g_conv_ln', 'b_conv_ln', 'w_gate_up', 'b_gate', 'g_gla_norm', 'g_norm_ffn2', 'g_norm_final')


def _ffn_fwd(x, h, gv, w4, wo, nxt, tag):
    z = _ffn_up(h, w4, f"ffn_up_{tag}")
    y, xn, *hn = _ffn_down(z, wo, x, gv, nxt, f"ffn_down_{tag}")
    return xn, (hn[0] if hn else None), y, (x, h, z)


def _ffn_bwd(dxn, dy, saved, g, scale, prev, w4, wo, tag):
    x, h, z = saved
    dz = _ffn_bwd_act(dy, wo, z, f"ffn_bwd_act_{tag}")
    dwo = _dw_ffn_out(z, dy, f"dw_out_{tag}")
    dh = _ffn_dh(dz, w4, f"ffn_dh_{tag}")
    dwi = _dw_ffn_in(h, dz, w4.shape[0], f"dw_in_{tag}")
    dx, dsh, dsc, dg, *pv = _normmod_bwd(x, dh, dxn, g, scale, prev, f"normmod_bwd_{tag}")
    return dx, pv, dict(dshift=dsh, dscale=dsc, dg=dg, dw_in=dwi, dw_out=dwo.reshape(N_CHIPS, -1, dwo.shape[1]))


def _mix_fwd(x, h, gv, wmix, w_dw, b_dw, g_ln, b_ln, wgp, bgp, gn, wout, nxt, tag):
    z = _mm([(h, wmix, 0)], F32, f"mix_in_{tag}")
    u, yc, yconv = _conv_fwd(z, w_dw, b_dw, g_ln, b_ln, f"conv_fwd_{tag}")
    la = _loggate(z, wgp, bgp, f"loggate_{tag}")
    o_raw, sprev, ygla = _gla_fwd(z, la, gn, f"gla_fwd_{tag}")
    y, xn, *hn = _mix_out(yconv, ygla, wout, x, gv, nxt, f"mix_out_{tag}")
    return xn, (hn[0] if hn else None), y, (x, h, z, u, yc, la, o_raw, sprev, yconv, ygla)


def _mix_bwd(dxn, dy, saved, g, scale, prev, wmix, w_dw, g_ln, b_ln, wgp, bgp, gn, wout, n_cols, n_pad, tag):
    x, h, z, u, yc, la, o_raw, sprev, yconv, ygla = saved
    dycat = _mm([(dy, wout, 0)], F32, f"mix_dycat_{tag}", nt=True)
    dwout = _mm_tn_two(yconv, ygla, dy, f"dw_mixout_{tag}")
    dab, dwdw, dbdw, dgln, dbln = _conv_bwd(dycat, z, u, yc, w_dw, g_ln, b_ln, f"conv_bwd_{tag}")
    dgla, dla, dgn = _gla_bwd(dycat, z, la, o_raw, sprev, gn, f"gla_bwd_{tag}")
    dglr, dwgp, dbgp = _loggate_bwd(dla, z, wgp, wgp.T, bgp, f"loggate_bwd_{tag}")
    dh = _mm([(dgla, wmix, 0), (dab, wmix, ZC_A // (2 * D_CONV)), (dglr, wmix, ZC_G // LANE)], F32, f"mix_dh_{tag}", nt=True)
    dwin = _mix_weight_grad(_mm_tn(h, dgla, f"dw_mixin_gla_{tag}"), _mm_tn(h, dab, f"dw_mixin_conv_{tag}"), _mm_tn(h, dglr, f"dw_mixin_gate_{tag}"),
                            n_cols, n_pad)
    dx, dsh, dsc, dg, *pv = _normmod_bwd(x, dh, dxn, g, scale, prev, f"normmod_bwd_{tag}")
    grads = dict(dshift=dsh, dscale=dsc, dg=dg, dw_in=dwin, dw_out=dwout.reshape(N_CHIPS, -1, dwout.shape[1]), dw_dw=dwdw, db_dw=dbdw,
                 dg_ln=dgln, db_ln=dbln, dw_gate=_unpad_heads(dwgp[:GATE_RANK]), db_gate=_unpad_heads(dbgp)[0], dgn=dgn)
    return dx, pv, grads


def kernel(x, c, w_ada, b_ada, g_norm_ffn1, w_ffn1_in, w_ffn1_out, g_norm_mix, w_in, w_dw, b_dw, g_conv_ln, b_conv_ln, w_gate_up, b_gate, g_gla_norm, w_out, g_norm_ffn2, w_ffn2_in, w_ffn2_out, g_norm_final, w_ada_final, b_ada_final, loss_target, m_w_ada, m_b_ada, m_g_norm_ffn1, m_w_ffn1_in, m_w_ffn1_out, m_g_norm_mix, m_w_in, m_w_dw, m_b_dw, m_g_conv_ln, m_b_conv_ln, m_w_gate_up, m_b_gate, m_g_gla_norm, m_w_out, m_g_norm_ffn2, m_w_ffn2_in, m_w_ffn2_out, m_g_norm_final, m_w_ada_final, m_b_ada_final, v_w_ada, v_b_ada, v_g_norm_ffn1, v_w_ffn1_in, v_w_ffn1_out, v_g_norm_mix, v_w_in, v_w_dw, v_b_dw, v_g_conv_ln, v_b_conv_ln, v_w_gate_up, v_b_gate, v_g_gla_norm, v_w_out, v_g_norm_ffn2, v_w_ffn2_in, v_w_ffn2_out, v_g_norm_final, v_w_ada_final, v_b_ada_final):
    given = dict(locals())
    W = {n: given[n] for n in _WEIGHTS}
    M1 = {n: given["m_" + n] for n in _WEIGHTS}
    M2 = {n: given["v_" + n] for n in _WEIGHTS}
    xs = x[0]
    tgt = loss_target[0]
    S, D = xs.shape
    L = w_ada.shape[0]
    xi, yi, ci = _place()
    s_me = 2 * xi + yi
    b_me = 4 * xi + 2 * yi + ci
    nsh = w_ada.shape[2]
    nfin = w_ada_final.shape[1]
    n_cols = w_in.shape[2]
    n_pad = -(-n_cols // LANE) * LANE

    c_all = _ag_small(c.reshape(8, D // 8), "ag_c").reshape(8, D)
    parts = [_mod_proj(c_all, w_ada, l, lax.dynamic_slice(b_ada, (l, s_me * nsh), (1, nsh)), f"mod_proj_{l}") for l in range(L)]
    parts.append(_mod_proj(c_all, w_ada_final[None], 0, lax.dynamic_slice(b_ada_final, (s_me * nfin,), (nfin,))[None], "mod_proj_final"))
    mod_all = _ag_small(jnp.concatenate(parts, axis=1), "ag_mod")
    mine = [lax.dynamic_index_in_dim(lax.dynamic_index_in_dim(mod_all, 2 * s + ci, 0, False), b_me, 0, False) for s in range(N_CHIPS)]
    mods = [jnp.concatenate([mine[s][l * nsh:(l + 1) * nsh] for s in range(N_CHIPS)]).reshape(N_MOD, 1, D) for l in range(L)]
    fmod = jnp.concatenate([mine[s][L * nsh:] for s in range(N_CHIPS)]).reshape(2, 1, D)

    tiny = jnp.concatenate([w_dw.reshape(-1), w_gate_up.reshape(-1)])
    tiny_all = _ag_small(jnp.pad(tiny, (0, (-tiny.shape[0]) % (8 * LANE))).reshape(8, -1), "ag_tiny").reshape(8, -1)
    n_dw = w_dw.size
    dw_parts = [lax.dynamic_index_in_dim(tiny_all, 2 * s + ci, 0, False) for s in range(N_CHIPS)]
    w_dw_full = jnp.concatenate([p[:n_dw].reshape(w_dw.shape) for p in dw_parts], axis=2)
    w_gu_full = jnp.concatenate([p[n_dw:n_dw + w_gate_up.size].reshape(w_gate_up.shape) for p in dw_parts], axis=2)

    def lands_of(l):
        shards = [W[n][l].astype(BF16) for n in _BIG]
        shards[2] = jnp.pad(shards[2], ((0, 0), (0, n_pad - n_cols)))
        return [lax.dynamic_update_index_in_dim(lax.empty((N_CHIPS,) + s.shape, BF16), s, s_me, 0) for s in shards]

    def layer_weights(l, lands):
        wi1, wo1, win4, wout4, wi2, wo2 = lands
        return dict(
            wi1=wi1, wo1=wo1.reshape(-1, D), wi2=wi2, wo2=wo2.reshape(-1, D), wout=wout4.reshape(-1, D), wmix=_mix_weight(win4, n_cols),
            wgp=jnp.pad(_pad_heads(w_gu_full[l]), ((0, LANE - GATE_RANK), (0, 0))).astype(BF16), bgp=_pad_heads(b_gate[l])[None])

    gnorm = (g_norm_ffn1, g_norm_mix, g_norm_ffn2)
    subs = [dict(l=l, j=j, tag=f"{('ffn1', 'mix', 'ffn2')[j]}_l{l}", g=gnorm[j][l][None], shift=mods[l][3 * j], scale=mods[l][3 * j + 1],
                 gv=mods[l][3 * j + 2] * (1.0 if j == 1 else 0.5)) for l in range(L) for j in range(3)]
    lands = {l: lands_of(l) for l in range(L)}
    lands[0][0:2] = _ag_layer(lands[0][0:2], "ag_weights_l0_ffn1")
    ag_groups = [dict(l=0, items=[2, 3, 4, 5], need=1)] + [dict(l=l, items=list(range(len(_BIG))), need=3 * l) for l in range(1, L)]

    def ag_start(grp):
        bufs = [lands[grp["l"]][i] for i in grp["items"]]
        return _split_start(bufs, 3 * len(bufs), _ag_starts, f"ag_start_l{grp['l']}_{grp['items'][0]}")

    pend = ag_start(ag_groups[0])
    tok = pend[3][0, 0]
    gi = 0
    xcur = xs
    h = None
    for k, sb in enumerate(subs):
        l, j = sb["l"], sb["j"]
        if pend is not None and ag_groups[gi]["need"] == k:
            grp = ag_groups[gi]
            nm = f"l{grp['l']}_{grp['items'][0]}"
            done = _ag_finish(_split_wait(pend[0], pend[1], pend[2], xcur, _ag_waits, f"ag_wait_{nm}"), f"ag_finish_{nm}")
            for i, a in zip(grp["items"], done):
                lands[grp["l"]][i] = a
            gi += 1
            pend = ag_start(ag_groups[gi]) if gi < len(ag_groups) else None
            tok = pend[3][0, 0] if pend is not None else None
        if h is None:
            h = _normmod(xcur, sb["g"] if tok is None else sb["g"] + tok, sb["shift"], sb["scale"], f"normmod_{sb['tag']}")
            tok = None
        d = layer_weights(l, lands[l])
        nxt = (subs[k + 1]["g"], subs[k + 1]["shift"], subs[k + 1]["scale"]) if k + 1 < len(subs) else None
        gv = sb["gv"] if tok is None else sb["gv"] + tok
        tok = None
        if j == 1:
            xcur, h, sb["y"], sb["saved"] = _mix_fwd(xcur, h, gv, d["wmix"], w_dw_full[l], b_dw[l][None], g_conv_ln[l][None],
                                                     b_conv_ln[l][None], d["wgp"], d["bgp"], g_gla_norm[l], d["wout"], nxt, sb["tag"])
        else:
            w4, wo = (d["wi1"], d["wo1"]) if j == 0 else (d["wi2"], d["wo2"])
            xcur, h, sb["y"], sb["saved"] = _ffn_fwd(xcur, h, gv, w4, wo, nxt, sb["tag"])
    lw = [layer_weights(l, lands[l]) for l in range(L)]

    c_arr = jnp.stack([ci]).astype(jnp.int32)
    sc_arr = jnp.stack([s_me, ci]).astype(jnp.int32)
    fulls = [lax.empty((L,) + ((W[n].shape[1], n_pad) if n == 'w_in' else W[n].shape[1:]), F32) for n in _BIG]

    def rs_begin(gs, items, l):
        nm = f"l{l}_{items[0]}"
        sibs = _rs_sibling(gs, f"rs_sibling_{nm}")
        return sibs, [_rs_presum(g, sb_, c_arr, f"rs_presum_{i}_l{l}") for i, g, sb_ in zip(items, gs, sibs)]

    def rs_end(gs, sibs, recvs, items, l):
        summed = [_rs_sum(g, sb_, rv, fulls[i], l, sc_arr, f"rs_sum_{i}_l{l}") for i, g, sb_, rv in zip(items, gs, sibs, recvs)]
        for i, f in zip(items, _rs_share(summed, l, f"rs_share_l{l}_{items[0]}")):
            fulls[i] = f

    def rs_start(gs, items, l):
        sibs, ps = rs_begin(gs, items, l)
        pend = _split_start(ps + [lax.empty((3,) + p.shape[1:], BF16) for p in ps], 3 * len(ps), _rs_chip_copies, f"rs_start_l{l}_{items[0]}")
        return dict(gs=gs, sibs=sibs, pend=pend, items=items, l=l)

    def rs_finish(fl, after):
        pend, n = fl["pend"], len(fl["gs"])
        bufs = _split_wait(pend[0], pend[1], pend[2], after, _rs_chip_copies, f"rs_wait_l{fl['l']}_{fl['items'][0]}")
        rs_end(fl["gs"], fl["sibs"], bufs[n:], fl["items"], fl["l"])

    dh, sq = _final_loss(xcur, g_norm_final[None], fmod[0], fmod[1], tgt)
    loss_part = 0.5 / D * jnp.sum(sq)
    dx, dfsh, dfsc, dgfin, dy, dgv = _normmod_bwd(xcur, dh, None, g_norm_final[None], fmod[1], (subs[-1]["y"], subs[-1]["gv"]), "normmod_bwd_final")
    G = {n: [None] * L for n in _SMALL}
    dmods = [None] * L
    in_flight = None
    tok = None
    for l in reversed(range(L)):
        gr = [None] * 3
        for j in reversed(range(3)):
            k = 3 * l + j
            sb, d = subs[k], lw[l]
            prev = (subs[k - 1]["y"], subs[k - 1]["gv"]) if k > 0 else None
            g_vec = sb["g"] if tok is None else sb["g"] + tok
            tok = None
            if j == 1:
                dx, pv, gr[j] = _mix_bwd(dx, dy, sb["saved"], g_vec, sb["scale"], prev, d["wmix"], w_dw_full[l], g_conv_ln[l][None], b_conv_ln[l][None],
                                         d["wgp"], d["bgp"], g_gla_norm[l], d["wout"], n_cols, n_pad, sb["tag"])
            else:
                w4, wo = (d["wi1"], d["wo1"]) if j == 0 else (d["wi2"], d["wo2"])
                dx, pv, gr[j] = _ffn_bwd(dx, dy, sb["saved"], g_vec, sb["scale"], prev, w4, wo, sb["tag"])
            gr[j]["dgv"] = dgv
            dy, dgv = pv if pv else (None, None)
            if j == 1 and in_flight is not None:
                rs_finish(in_flight, dx)
                in_flight = None
            if j == 1 and l == 0:
                in_flight = rs_start([gr[1]["dw_in"], gr[1]["dw_out"], gr[2]["dw_in"], gr[2]["dw_out"]], [2, 3, 4, 5], l)
                tok = in_flight["pend"][3][0, 0]
        g1, g2, g3 = gr
        if l > 0:
            in_flight = rs_start([g1["dw_in"], g1["dw_out"], g2["dw_in"], g2["dw_out"], g3["dw_in"], g3["dw_out"]], list(range(len(_BIG))), l)
            tok = in_flight["pend"][3][0, 0]
        else:
            rs_finish(in_flight, dx)
            in_flight = None
            gs = [g1["dw_in"], g1["dw_out"]]
            sibs, ps = rs_begin(gs, [0, 1], l)
            rs_end(gs, sibs, _rs_chips(ps, f"rs_chips_l{l}_0"), [0, 1], l)
        dmods[l] = jnp.concatenate([g1["dshift"], g1["dscale"], 0.5 * g1["dgv"], g2["dshift"], g2["dscale"], g2["dgv"],
                                    g3["dshift"], g3["dscale"], 0.5 * g3["dgv"]], axis=1)[0]
        G["g_norm_ffn1"][l], G["g_norm_ffn2"][l], G["g_norm_mix"][l] = g1["dg"][0], g3["dg"][0], g2["dg"][0]
        G["w_dw"][l], G["b_dw"][l], G["g_conv_ln"][l], G["b_conv_ln"][l] = g2["dw_dw"], g2["db_dw"][0], g2["dg_ln"][0], g2["db_ln"][0]
        G["w_gate_up"][l], G["b_gate"][l], G["g_gla_norm"][l] = g2["dw_gate"], g2["db_gate"], g2["dgn"]
    grad_x = dx[None]
    gsm = {n: (f[:, :, :n_cols] if n == 'w_in' else f) for n, f in zip(_BIG, fulls)}

    small = [jnp.stack(G[n]).reshape(-1) for n in _SMALL if n != 'g_norm_final'] + [dgfin[0]]
    dmod_vec = jnp.concatenate(dmods + [dfsh[0], dfsc[0]])
    n_mod_vec = dmod_vec.shape[0]
    vec = jnp.concatenate([dmod_vec] + small + [loss_part[None]])
    n_vec = vec.shape[0]
    vec = jnp.pad(vec, (0, (-n_vec) % (8 * LANE)))
    vec_all = _ag_small(vec.reshape(8, -1), "ag_small_grads").reshape(8, -1)
    vec_sum = _rowsum(vec_all, "sum_small_grads")[0]
    loss = vec_sum[n_vec - 1]
    off = n_mod_vec
    for n in _SMALL:
        shp = {'w_dw': w_dw_full.shape, 'w_gate_up': w_gu_full.shape}.get(n, W[n].shape)
        cnt = 1
        for dd in shp:
            cnt *= dd
        gsm[n] = vec_sum[off:off + cnt].reshape(shp)
        off += cnt
    gsm['w_dw'] = lax.dynamic_slice_in_dim(gsm['w_dw'], s_me * w_dw.shape[2], w_dw.shape[2], 2)
    gsm['w_gate_up'] = lax.dynamic_slice_in_dim(gsm['w_gate_up'], s_me * w_gate_up.shape[2], w_gate_up.shape[2], 2)
    dmod_sum = vec_sum[:n_mod_vec]
    gsm['b_ada'] = dmod_sum[:L * N_MOD * D].reshape(L, N_MOD * D)
    gsm['b_ada_final'] = dmod_sum[L * N_MOD * D:]
    c_t = c_all.T
    dmod_rows = vec_all[:, :n_mod_vec]
    gsm['w_ada'] = jnp.stack([
        _mod_wgrad(c_t, lax.dynamic_slice_in_dim(dmod_rows, l * N_MOD * D + s_me * nsh, nsh, 1), f"dw_ada_{l}") for l in range(L)])
    gsm['w_ada_final'] = _mod_wgrad(c_t, lax.dynamic_slice_in_dim(dmod_rows, L * N_MOD * D + s_me * nfin, nfin, 1), "dw_ada_final")

    outs = {}
    small_names = [n for n in _WEIGHTS if W[n].size < 65536]
    for n in _WEIGHTS:
        if n in small_names:
            continue
        shp = W[n].shape
        v2 = lambda a: a.reshape(-1, shp[-1])
        d_, m_, v_ = _adamw(v2(W[n]), v2(gsm[n]), v2(M1[n]), v2(M2[n]), f"adamw_{n}")
        outs[n] = (d_.reshape(shp), m_.reshape(shp), v_.reshape(shp))
    flat = lambda dct: jnp.concatenate([dct[n].reshape(-1) for n in small_names])
    n_small = sum(W[n].size for n in small_names)
    v2 = lambda a: jnp.pad(a, (0, (-n_small) % (8 * LANE))).reshape(-1, LANE)
    d_, m_, v_ = _adamw(v2(flat(W)), v2(flat(gsm)), v2(flat(M1)), v2(flat(M2)), "adamw_small")

    def unflat(a):
        res, o = {}, 0
        a = a.reshape(-1)
        for n in small_names:
            res[n] = a[o:o + W[n].size].reshape(W[n].shape)
            o += W[n].size
        return res

    for n, dd, mm, vv in zip(small_names, unflat(d_).values(), unflat(m_).values(), unflat(v_).values()):
        outs[n] = (dd, mm, vv)

    return (loss, grad_x, *[gsm[n] for n in _WEIGHTS], *[outs[n][0] for n in _WEIGHTS], *[outs[n][1] for n in _WEIGHTS], *[outs[n][2] for n in _WEIGHTS])
```

```python
import jax
import jax.numpy as jnp
from jax import lax
from jax.experimental import pallas as pl
from jax.experimental.pallas import tpu as pltpu

F32 = jnp.float32
BF16 = jnp.bfloat16

CHUNK = 64
HEADS = 4
DK = 64
DV = 128
DKP = 128
GATE_RANK = 16
GATE_TAU = 16.0
N_MOD = 9
EPS = 1e-6
ADAM_LR = 0.001
ADAM_B1 = 0.9
ADAM_B2 = 0.999
ADAM_EPS = 1e-08
ADAM_WD = 0.01
ADAM_STEP = 10

LANE = 128
HALO = 32
VMEM_LIMIT = 52 * 1024 * 1024
MESH = pl.DeviceIdType.MESH
N_CHIPS = 4

D_CONV = 512
D_GLA = HEADS * DV
ZC_Q = 0
ZC_K = ZC_Q + HEADS * DKP
ZC_V = ZC_K + HEADS * DKP
ZC_R = ZC_V + D_GLA
ZC_A = ZC_R + D_GLA
ZC_B = ZC_A + D_CONV
ZC_G = ZC_B + D_CONV
Z_COLS = ZC_G + LANE
Z_GLA = ZC_A


def _div(n, target, mult):
    best = None
    d = mult
    while d <= min(n, target):
        if n % d == 0:
            best = d
        d += mult
    return n if best is None else best


def _cp(sem=None, **kw):
    return pltpu.CompilerParams(dimension_semantics=sem, vmem_limit_bytes=VMEM_LIMIT, **kw)


def _resident(shape, index_map):
    return pl.BlockSpec(shape, index_map, pipeline_mode=pl.Buffered(1))


def _sigmoid(x):
    return 0.5 * jnp.tanh(0.5 * x) + 0.5


def _dot(a, b):
    return jnp.dot(a.astype(BF16), b.astype(BF16), preferred_element_type=F32)


def _dot_nt(a, b):
    return lax.dot_general(a.astype(BF16), b.astype(BF16), (((1,), (1,)), ((), ())), preferred_element_type=F32)


def _dot_tn(a, b):
    return lax.dot_general(a.astype(BF16), b.astype(BF16), (((0,), (0,)), ((), ())), preferred_element_type=F32)


def _dot_exact(a, b):
    return jnp.dot(a, b, preferred_element_type=F32, precision=lax.Precision.HIGHEST)


def _normmod(x, g, shift, scale, name):
    S, D = x.shape
    tm = _div(S, 512, 8)

    def body(x_ref, g_ref, sh_ref, sc_ref, o_ref):
        xv = x_ref[...]
        r = lax.rsqrt(jnp.mean(xv * xv, axis=-1, keepdims=True) + EPS)
        o_ref[...] = ((xv * r) * g_ref[...] * (1.0 + sc_ref[...]) + sh_ref[...]).astype(o_ref.dtype)

    row = pl.BlockSpec((tm, D), lambda i: (i, 0))
    vec = pl.BlockSpec((1, D), lambda i: (0, 0))
    return pl.pallas_call(
        body, grid=(S // tm,), in_specs=[row, vec, vec, vec], out_specs=row,
        out_shape=jax.ShapeDtypeStruct((S, D), BF16), compiler_params=_cp(("parallel",)), name=name,
    )(x, g, shift, scale)


def _final_loss(x, g, shift, scale, tgt):
    S, D = x.shape
    tm = _div(S, 512, 8)

    def body(x_ref, g_ref, sh_ref, sc_ref, t_ref, dh_ref, sq_ref):
        @pl.when(pl.program_id(0) == 0)
        def _():
            sq_ref[...] = jnp.zeros_like(sq_ref)

        xv = x_ref[...]
        r = lax.rsqrt(jnp.mean(xv * xv, axis=-1, keepdims=True) + EPS)
        h = (xv * r) * g_ref[...] * (1.0 + sc_ref[...]) + sh_ref[...]
        e = h - t_ref[...]
        dh_ref[...] = e * (1.0 / D)
        sq_ref[...] += jnp.sum(e * e, axis=0, keepdims=True)

    row = pl.BlockSpec((tm, D), lambda i: (i, 0))
    vec = pl.BlockSpec((1, D), lambda i: (0, 0))
    return pl.pallas_call(
        body, grid=(S // tm,), in_specs=[row, vec, vec, vec, row], out_specs=[row, vec],
        out_shape=[jax.ShapeDtypeStruct((S, D), F32), jax.ShapeDtypeStruct((1, D), F32)],
        compiler_params=_cp(("arbitrary",)), name="final_loss",
    )(x, g, shift, scale, tgt)


def _normmod_bwd_rows(xv, dh, dres, gv, sc, prev):
    r = lax.rsqrt(jnp.mean(xv * xv, axis=-1, keepdims=True) + EPS)
    xh = xv * r
    dsh = jnp.sum(dh, axis=0, keepdims=True)
    dsc = jnp.sum(dh * (xh * gv), axis=0, keepdims=True)
    dn = dh * (1.0 + sc)
    dg = jnp.sum(dn * xh, axis=0, keepdims=True)
    dxh = dn * gv
    dx = r * (dxh - xh * jnp.mean(dxh * xh, axis=-1, keepdims=True))
    if dres is not None:
        dx = dx + dres
    if prev is None:
        return dx, dsh, dsc, dg
    y, gvp = prev
    return dx, dsh, dsc, dg, (gvp * dx).astype(BF16), jnp.sum(dx * y, axis=0, keepdims=True)


def _normmod_bwd(x, dh, dres, g, scale, prev, name):
    S, D = x.shape
    tm = _div(S, 512, 8)
    with_res = dres is not None
    with_prev = prev is not None

    def body(*refs):
        refs = list(refs)
        x_ref, dh_ref = refs[:2]
        del refs[:2]
        dr_ref = refs.pop(0) if with_res else None
        g_ref, sc_ref = refs[:2]
        del refs[:2]
        pv = None
        if with_prev:
            pv = (refs[0][...], refs[1][...])
            del refs[:2]

        @pl.when(pl.program_id(0) == 0)
        def _():
            for o in refs[1:4] + refs[5:]:
                o[...] = jnp.zeros_like(o)

        res = _normmod_bwd_rows(x_ref[...], dh_ref[...].astype(F32), dr_ref[...] if with_res else None, g_ref[...], sc_ref[...], pv)
        refs[0][...] = res[0]
        for o, v in zip(refs[1:4], res[1:4]):
            o[...] += v
        if with_prev:
            refs[4][...] = res[4]
            refs[5][...] += res[5]

    row = pl.BlockSpec((tm, D), lambda i: (i, 0))
    vec = pl.BlockSpec((1, D), lambda i: (0, 0))
    ins = [row, row] + [row] * with_res + [vec, vec] + [row, vec] * with_prev
    args = (x, dh) + ((dres,) if with_res else ()) + (g, scale) + (tuple(prev) if with_prev else ())
    vs = jax.ShapeDtypeStruct((1, D), F32)
    return pl.pallas_call(
        body, grid=(S // tm,), in_specs=ins, out_specs=[row, vec, vec, vec] + [row, vec] * with_prev,
        out_shape=[jax.ShapeDtypeStruct((S, D), F32), vs, vs, vs] + [jax.ShapeDtypeStruct((S, D), BF16), vs] * with_prev,
        compiler_params=_cp(("arbitrary",)), name=name,
    )(*args)


def _mm(pairs, out_dtype, name, nt=False):
    M = pairs[0][0].shape[0]
    N = pairs[0][1].shape[0] if nt else pairs[0][1].shape[1]
    ktot = sum(a.shape[1] for a, _, _ in pairs)
    tm = _div(M, 512 if ktot <= 4096 else 256, 8)
    n = len(pairs)

    def body(*refs):
        o_ref = refs[2 * n]
        dot = _dot_nt if nt else _dot
        acc = dot(refs[0][...], refs[1][...])
        for p in range(1, n):
            acc = acc + dot(refs[2 * p][...], refs[2 * p + 1][...])
        o_ref[...] = acc.astype(o_ref.dtype)

    ins, args = [], []
    for a, b, blk in pairs:
        k = a.shape[1]
        ins.append(pl.BlockSpec((tm, k), lambda i: (i, 0)))
        ins.append(_resident((N, k), lambda i, blk=blk: (0, blk)) if nt else _resident((k, N), lambda i: (0, 0)))
        args += [a, b]
    return pl.pallas_call(
        body, grid=(M // tm,), in_specs=ins, out_specs=pl.BlockSpec((tm, N), lambda i: (i, 0)),
        out_shape=jax.ShapeDtypeStruct((M, N), out_dtype), compiler_params=_cp(("parallel",)), name=name,
    )(*args)


def _mm_tn(a, g, name):
    S, Ka = a.shape
    N = g.shape[1]
    tk = _div(Ka, 1408, LANE)
    tn = _div(N, 1408, LANE)
    ts = _div(S, 512, 8)

    def body(a_ref, g_ref, o_ref):
        @pl.when(pl.program_id(2) == 0)
        def _():
            o_ref[...] = jnp.zeros_like(o_ref)

        o_ref[...] += _dot_tn(a_ref[...], g_ref[...])

    return pl.pallas_call(
        body, grid=(Ka // tk, N // tn, S // ts),
        in_specs=[pl.BlockSpec((ts, tk), lambda i, j, s: (s, i)), pl.BlockSpec((ts, tn), lambda i, j, s: (s, j))],
        out_specs=pl.BlockSpec((tk, tn), lambda i, j, s: (i, j)),
        out_shape=jax.ShapeDtypeStruct((Ka, N), F32),
        compiler_params=_cp(("parallel", "parallel", "arbitrary")), name=name,
    )(a, g)


def _mm_tn_two(a0, a1, g, name):
    S, K = a0.shape
    N = g.shape[1]
    ts = _div(S, 512, 8)

    def body(a0_ref, a1_ref, g_ref, o_ref):
        i = pl.program_id(0)

        @pl.when(pl.program_id(1) == 0)
        def _():
            o_ref[...] = jnp.zeros_like(o_ref)

        @pl.when(i == 0)
        def _():
            o_ref[...] += _dot_tn(a0_ref[...], g_ref[...])

        @pl.when(i == 1)
        def _():
            o_ref[...] += _dot_tn(a1_ref[...], g_ref[...])

    return pl.pallas_call(
        body, grid=(2, S // ts),
        in_specs=[pl.BlockSpec((ts, K), lambda i, s: (jnp.where(i == 0, s, 0), 0)),
                  pl.BlockSpec((ts, K), lambda i, s: (jnp.where(i == 1, s, 0), 0)),
                  pl.BlockSpec((ts, N), lambda i, s: (s, 0))],
        out_specs=pl.BlockSpec((K, N), lambda i, s: (i, 0)),
        out_shape=jax.ShapeDtypeStruct((2 * K, N), F32),
        compiler_params=_cp(("parallel", "arbitrary")), name=name,
    )(a0, a1, g)


def _swiglu(gt, up):
    return gt * _sigmoid(gt) * up


def _ffn_up(h, w4, name):
    S, D = h.shape
    ns, _, C = w4.shape
    hs = ns // 2
    tm = _div(S, 256, 8)

    def body(h_ref, w_ref, z_ref, a_ref):
        hv = h_ref[...]
        for s in range(hs):
            gt = _dot(hv, w_ref[s])
            up = _dot(hv, w_ref[hs + s])
            z_ref[:, s * C:(s + 1) * C] = gt.astype(BF16)
            z_ref[:, (hs + s) * C:(hs + s + 1) * C] = up.astype(BF16)
            a_ref[:, s * C:(s + 1) * C] = _swiglu(gt, up).astype(BF16)

    return pl.pallas_call(
        body, grid=(S // tm,), in_specs=[pl.BlockSpec((tm, D), lambda i: (i, 0)), _resident((ns, D, C), lambda i: (0, 0, 0))],
        out_specs=[pl.BlockSpec((tm, ns * C), lambda i: (i, 0)), pl.BlockSpec((tm, hs * C), lambda i: (i, 0))],
        out_shape=[jax.ShapeDtypeStruct((S, ns * C), BF16), jax.ShapeDtypeStruct((S, hs * C), BF16)],
        compiler_params=_cp(("parallel",)), name=name,
    )(h, w4)


def _norm_rows(xv, g, shift, scale):
    r = lax.rsqrt(jnp.mean(xv * xv, axis=-1, keepdims=True) + EPS)
    return (xv * r) * g * (1.0 + scale) + shift


def _resid_outputs(y, x_ref, gv_ref, nxt_refs, out_refs):
    out_refs[0][...] = y
    xn = x_ref[...] + gv_ref[...] * y
    out_refs[1][...] = xn
    if nxt_refs:
        out_refs[2][...] = _norm_rows(xn, nxt_refs[0][...], nxt_refs[1][...], nxt_refs[2][...]).astype(BF16)


def _ffn_down(act, wo, x, gv, nxt, name):
    S = act.shape[0]
    Fd, D = wo.shape
    tm = _div(S, 512, 8)
    nn = 3 if nxt else 0

    def body(a_ref, w_ref, x_ref, gv_ref, *rest):
        _resid_outputs(_dot(a_ref[...], w_ref[...]), x_ref, gv_ref, rest[:nn], rest[nn:])

    row = pl.BlockSpec((tm, D), lambda i: (i, 0))
    vec = pl.BlockSpec((1, D), lambda i: (0, 0))
    os_ = jax.ShapeDtypeStruct((S, D), F32)
    return pl.pallas_call(
        body, grid=(S // tm,),
        in_specs=[pl.BlockSpec((tm, Fd), lambda i: (i, 0)), _resident((Fd, D), lambda i: (0, 0)), row, vec] + [vec] * nn,
        out_specs=[row, row] + [row] * (nn // 3), out_shape=[os_, os_] + [jax.ShapeDtypeStruct((S, D), BF16)] * (nn // 3),
        compiler_params=_cp(("parallel",)), name=name,
    )(act, wo, x, gv, *(nxt or ()))


def _ffn_bwd_act(dy, wo, z, name):
    S, D = dy.shape
    Fd = wo.shape[0]
    tm = _div(S, 256, 8)

    def body(dy_ref, w_ref, g_ref, u_ref, dz_ref):
        da = _dot_nt(dy_ref[...], w_ref[...])
        gt = g_ref[...].astype(F32)
        sg = _sigmoid(gt)
        dz_ref[:, Fd:] = (da * gt * sg).astype(BF16)
        dz_ref[:, :Fd] = (da * u_ref[...].astype(F32) * (sg * (1.0 + gt * (1.0 - sg)))).astype(BF16)

    return pl.pallas_call(
        body, grid=(S // tm,),
        in_specs=[pl.BlockSpec((tm, D), lambda i: (i, 0)), _resident((Fd, D), lambda i: (0, 0)),
                  pl.BlockSpec((tm, Fd), lambda i: (i, 0)), pl.BlockSpec((tm, Fd), lambda i: (i, 1))],
        out_specs=pl.BlockSpec((tm, 2 * Fd), lambda i: (i, 0)), out_shape=jax.ShapeDtypeStruct((S, 2 * Fd), BF16),
        compiler_params=_cp(("parallel",)), name=name,
    )(dy, wo, z, z)


def _dw_ffn_in(h, dz, ns, name):
    S, D = h.shape
    C = dz.shape[1] // ns
    ts = _div(S, 512, 8)

    def body(h_ref, g_ref, o_ref):
        @pl.when(pl.program_id(1) == 0)
        def _():
            o_ref[...] = jnp.zeros_like(o_ref)

        o_ref[...] += _dot_tn(h_ref[...], g_ref[...])

    return pl.pallas_call(
        body, grid=(ns, S // ts),
        in_specs=[pl.BlockSpec((ts, D), lambda j, s: (s, 0)), pl.BlockSpec((ts, C), lambda j, s: (s, j))],
        out_specs=pl.BlockSpec((None, D, C), lambda j, s: (j, 0, 0)), out_shape=jax.ShapeDtypeStruct((ns, D, C), F32),
        compiler_params=_cp(("parallel", "arbitrary")), name=name,
    )(h, dz)


def _dh_normmod_bwd(pairs, x, dres, g, scale, prev, name):
    S, D = x.shape
    tm = _div(S, 256, 8)
    n = len(pairs)
    with_prev = prev is not None

    def body(*refs):
        refs = list(refs)
        mm = refs[:2 * n]
        x_ref, dr_ref, g_ref, sc_ref = refs[2 * n:2 * n + 4]
        outs = refs[2 * n + 4 + 2 * with_prev:]

        @pl.when(pl.program_id(0) == 0)
        def _():
            for o in outs[1:4] + outs[5:]:
                o[...] = jnp.zeros_like(o)

        dh = _dot_nt(mm[0][...], mm[1][...])
        for p in range(1, n):
            dh = dh + _dot_nt(mm[2 * p][...], mm[2 * p + 1][...])
        pv = (refs[2 * n + 4][...], refs[2 * n + 5][...]) if with_prev else None
        res = _normmod_bwd_rows(x_ref[...], dh, dr_ref[...], g_ref[...], sc_ref[...], pv)
        outs[0][...] = res[0]
        for o, v in zip(outs[1:4], res[1:4]):
            o[...] += v
        if with_prev:
            outs[4][...] = res[4]
            outs[5][...] += res[5]

    row = pl.BlockSpec((tm, D), lambda i: (i, 0))
    vec = pl.BlockSpec((1, D), lambda i: (0, 0))
    ins, args = [], []
    for a, b, a_blk, b_blk, k in pairs:
        ins.append(pl.BlockSpec((tm, k), lambda i, a_blk=a_blk: (i, a_blk)))
        ins.append(_resident((None, D, k), lambda i, b_blk=b_blk: (b_blk, 0, 0)) if b.ndim == 3 else _resident((D, k), lambda i, b_blk=b_blk: (0, b_blk)))
        args += [a, b]
    ins += [row, row, vec, vec] + [row, vec] * with_prev
    args += [x, dres, g, scale] + (list(prev) if with_prev else [])
    vs = jax.ShapeDtypeStruct((1, D), F32)
    return pl.pallas_call(
        body, grid=(S // tm,), in_specs=ins, out_specs=[row, vec, vec, vec] + [row, vec] * with_prev,
        out_shape=[jax.ShapeDtypeStruct((S, D), F32), vs, vs, vs] + [jax.ShapeDtypeStruct((S, D), BF16), vs] * with_prev,
        compiler_params=_cp(("arbitrary",)), name=name,
    )(*args)


def _mix_out(yconv, ygla, wout, x, gv, nxt, name):
    S, Kc = yconv.shape
    Kg = ygla.shape[1]
    D = wout.shape[1]
    tm = _div(S, 512, 8)
    nn = 3 if nxt else 0

    def body(a_ref, b_ref, w_ref, x_ref, gv_ref, *rest):
        y = _dot(a_ref[...], w_ref[0:Kc, :]) + _dot(b_ref[...], w_ref[Kc:Kc + Kg, :])
        _resid_outputs(y, x_ref, gv_ref, rest[:nn], rest[nn:])

    row = pl.BlockSpec((tm, D), lambda i: (i, 0))
    vec = pl.BlockSpec((1, D), lambda i: (0, 0))
    os_ = jax.ShapeDtypeStruct((S, D), F32)
    return pl.pallas_call(
        body, grid=(S // tm,),
        in_specs=[pl.BlockSpec((tm, Kc), lambda i: (i, 0)), pl.BlockSpec((tm, Kg), lambda i: (i, 0)), _resident((Kc + Kg, D), lambda i: (0, 0)), row,
                  vec] + [vec] * nn,
        out_specs=[row, row] + [row] * (nn // 3), out_shape=[os_, os_] + [jax.ShapeDtypeStruct((S, D), BF16)] * (nn // 3),
        compiler_params=_cp(("parallel",)), name=name,
    )(yconv, ygla, wout, x, gv, *(nxt or ()))


def _ln_parts(yc, g, b):
    mu = jnp.mean(yc, axis=-1, keepdims=True)
    xc = yc - mu
    rs = lax.rsqrt(jnp.mean(xc * xc, axis=-1, keepdims=True) + EPS)
    xh = xc * rs
    return xh, rs, xh * g + b


SUB = 8
CONV_ROWS = 32


def _shifted_copies(ext8, rows):
    for b in range(1, SUB):
        ext8[b, pl.ds(0, rows - SUB), :] = ext8[0, pl.ds(b, rows - SUB), :]


def _tap(o):
    return o % SUB, o - o % SUB


def _conv_fwd(z, w_dw, b_dw, g_ln, b_ln, name):
    S = z.shape[0]
    W, C = w_dw.shape
    ts = _div(S, 512, HALO)
    hb = ts // HALO
    off = HALO - (W - 1)
    ca, cb = ZC_A // C, ZC_B // C
    rb = CONV_ROWS

    def body(a_ref, b_ref, pa_ref, pb_ref, w_ref, bd_ref, g_ref, bl_ref, u_ref, yc_ref, o_ref, ext8):
        keep = (pl.program_id(0) > 0).astype(F32)
        u = a_ref[...] * _sigmoid(b_ref[...])
        ext8[0, pl.ds(0, HALO), :] = pa_ref[...] * _sigmoid(pb_ref[...]) * keep
        ext8[0, pl.ds(HALO, ts), :] = u
        u_ref[...] = u
        _shifted_copies(ext8, ts + HALO)

        def sub(i, carry):
            r0 = pl.multiple_of(i * rb, rb)
            acc = jnp.zeros((rb, C), F32)
            for j in range(W):
                b, a = _tap(off + j)
                acc = acc + w_ref[pl.ds(j, 1), :] * ext8[b, pl.ds(r0 + a, rb), :]
            yc = acc + bd_ref[...]
            yc_ref[pl.ds(r0, rb), :] = yc
            _, _, ln = _ln_parts(yc, g_ref[...], bl_ref[...])
            o_ref[pl.ds(r0, rb), :] = (ln * _sigmoid(ln)).astype(BF16)
            return carry

        lax.fori_loop(0, ts // rb, sub, 0)

    cur = lambda col: pl.BlockSpec((ts, C), lambda i: (i, col))
    prev = lambda col: pl.BlockSpec((HALO, C), lambda i: (jnp.maximum(i * hb - 1, 0), col))
    vec = pl.BlockSpec((1, C), lambda i: (0, 0))
    row = pl.BlockSpec((ts, C), lambda i: (i, 0))
    fs = jax.ShapeDtypeStruct((S, C), F32)
    return pl.pallas_call(
        body, grid=(S // ts,),
        in_specs=[cur(ca), cur(cb), prev(ca), prev(cb), pl.BlockSpec((W, C), lambda i: (0, 0)), vec, vec, vec],
        out_specs=[row, row, row], out_shape=[fs, fs, jax.ShapeDtypeStruct((S, C), BF16)],
        scratch_shapes=[pltpu.VMEM((SUB, ts + HALO, C), F32)],
        compiler_params=_cp(("parallel",)), name=name,
    )(z, z, z, z, w_dw, b_dw, g_ln, b_ln)


def _conv_bwd(dycat, z, u, yc, w_dw, g_ln, b_ln, name):
    S = z.shape[0]
    W, C = w_dw.shape
    ts = _div(S, 512, HALO)
    hb = ts // HALO
    nblk = S // ts
    off = HALO - (W - 1)
    ca, cb = ZC_A // C, ZC_B // C
    rb = CONV_ROWS

    def ln_silu_bwd(dy, ycv, g, b):
        xh, rs, ln = _ln_parts(ycv, g, b)
        sl = _sigmoid(ln)
        dln = dy * (sl * (1.0 + ln * (1.0 - sl)))
        dxh = dln * g
        dyc = rs * (dxh - jnp.mean(dxh, axis=-1, keepdims=True) - xh * jnp.mean(dxh * xh, axis=-1, keepdims=True))
        return dyc, dln, xh

    def body(dy_ref, ndy_ref, yc_ref, nyc_ref, u_ref, pu_ref, a_ref, b_ref, w_ref, g_ref, bl_ref,
             dab_ref, dw_ref, dbd_ref, dg_ref, dbl_ref, uext8, dext8, dwacc):
        i = pl.program_id(0)

        @pl.when(i == 0)
        def _():
            dwacc[...] = jnp.zeros_like(dwacc)
            dbd_ref[...] = jnp.zeros_like(dbd_ref)
            dg_ref[...] = jnp.zeros_like(dg_ref)
            dbl_ref[...] = jnp.zeros_like(dbl_ref)

        g = g_ref[...]
        bl = bl_ref[...]
        dyc, dln, xh = ln_silu_bwd(dy_ref[...], yc_ref[...], g, bl)
        ndyc, _, _ = ln_silu_bwd(ndy_ref[...], nyc_ref[...], g, bl)
        dg_ref[...] += jnp.sum(dln * xh, axis=0, keepdims=True)
        dbl_ref[...] += jnp.sum(dln, axis=0, keepdims=True)
        dbd_ref[...] += jnp.sum(dyc, axis=0, keepdims=True)
        dext8[0, pl.ds(0, ts), :] = dyc
        dext8[0, pl.ds(ts, HALO), :] = ndyc * (i < nblk - 1).astype(F32)
        uext8[0, pl.ds(0, HALO), :] = pu_ref[...] * (i > 0).astype(F32)
        uext8[0, pl.ds(HALO, ts), :] = u_ref[...]
        _shifted_copies(dext8, ts + HALO)
        _shifted_copies(uext8, ts + HALO)

        def sub(k, carry):
            r0 = pl.multiple_of(k * rb, rb)
            rows = pl.ds(r0, rb)
            dyt = dext8[0, rows, :]
            du = jnp.zeros((rb, C), F32)
            for j in range(W):
                b, a = _tap(W - 1 - j)
                du = du + w_ref[pl.ds(j, 1), :] * dext8[b, pl.ds(r0 + a, rb), :]
                b, a = _tap(off + j)
                p = dyt * uext8[b, pl.ds(r0 + a, rb), :]
                part = p[0:SUB]
                for q in range(1, rb // SUB):
                    part = part + p[q * SUB:(q + 1) * SUB]
                dwacc[j] += part
            sb = _sigmoid(b_ref[rows, :])
            dab_ref[rows, 0:C] = (du * sb).astype(BF16)
            dab_ref[rows, C:2 * C] = (du * a_ref[rows, :] * sb * (1.0 - sb)).astype(BF16)
            return carry

        lax.fori_loop(0, ts // rb, sub, 0)

        @pl.when(i == nblk - 1)
        def _():
            for j in range(W):
                dw_ref[pl.ds(j, 1), :] = jnp.sum(dwacc[j], axis=0, keepdims=True)

    row = pl.BlockSpec((ts, C), lambda i: (i, 0))
    nxt = pl.BlockSpec((HALO, C), lambda i: (jnp.minimum((i + 1) * hb, S // HALO - 1), 0))
    prv = pl.BlockSpec((HALO, C), lambda i: (jnp.maximum(i * hb - 1, 0), 0))
    vec = pl.BlockSpec((1, C), lambda i: (0, 0))
    wsp = pl.BlockSpec((W, C), lambda i: (0, 0))
    vs = jax.ShapeDtypeStruct((1, C), F32)
    return pl.pallas_call(
        body, grid=(nblk,),
        in_specs=[row, nxt, row, nxt, row, prv, pl.BlockSpec((ts, C), lambda i: (i, ca)), pl.BlockSpec((ts, C), lambda i: (i, cb)), wsp, vec, vec],
        out_specs=[pl.BlockSpec((ts, 2 * C), lambda i: (i, 0)), wsp, vec, vec, vec],
        out_shape=[jax.ShapeDtypeStruct((S, 2 * C), BF16), jax.ShapeDtypeStruct((W, C), F32), vs, vs, vs],
        scratch_shapes=[pltpu.VMEM((SUB, ts + HALO, C), F32), pltpu.VMEM((SUB, ts + HALO, C), F32), pltpu.VMEM((W, SUB, C), F32)],
        compiler_params=_cp(("arbitrary",)), name=name,
    )(dycat, dycat, yc, yc, u, u, z, z, w_dw, g_ln, b_ln)


def _log_gate(zg):
    return (jnp.minimum(zg, 0.0) - jnp.log(1.0 + jnp.exp(-jnp.abs(zg)))) * (1.0 / GATE_TAU)


def _loggate(z, wgp, bgp, name):
    S = z.shape[0]
    N = wgp.shape[1]
    ts = _div(S, 512, 8)

    def body(g_ref, w_ref, b_ref, o_ref):
        o_ref[...] = _log_gate(_dot(g_ref[...], w_ref[...]) + b_ref[...])

    return pl.pallas_call(
        body, grid=(S // ts,),
        in_specs=[pl.BlockSpec((ts, LANE), lambda i: (i, ZC_G // LANE)), pl.BlockSpec((LANE, N), lambda i: (0, 0)), pl.BlockSpec((1, N), lambda i: (0, 0))],
        out_specs=pl.BlockSpec((ts, N), lambda i: (i, 0)), out_shape=jax.ShapeDtypeStruct((S, N), F32),
        compiler_params=_cp(("parallel",)), name=name,
    )(z, wgp, bgp)


def _loggate_bwd(dla, z, wgp, wgp_t, bgp, name):
    S = z.shape[0]
    N = wgp.shape[1]
    ts = _div(S, 512, 8)

    def body(dla_ref, g_ref, w_ref, wt_ref, b_ref, dg_ref, dw_ref, db_ref):
        @pl.when(pl.program_id(0) == 0)
        def _():
            dw_ref[...] = jnp.zeros_like(dw_ref)
            db_ref[...] = jnp.zeros_like(db_ref)

        glr = g_ref[...]
        zg = _dot(glr, w_ref[...]) + b_ref[...]
        dzg = dla_ref[...] * (1.0 / GATE_TAU) * (1.0 - _sigmoid(zg))
        dg_ref[...] = _dot(dzg, wt_ref[...]).astype(BF16)
        dw_ref[...] += _dot_tn(glr, dzg)
        db_ref[...] += jnp.sum(dzg, axis=0, keepdims=True)

    return pl.pallas_call(
        body, grid=(S // ts,),
        in_specs=[pl.BlockSpec((ts, N), lambda i: (i, 0)), pl.BlockSpec((ts, LANE), lambda i: (i, ZC_G // LANE)),
                  pl.BlockSpec((LANE, N), lambda i: (0, 0)), pl.BlockSpec((N, LANE), lambda i: (0, 0)), pl.BlockSpec((1, N), lambda i: (0, 0))],
        out_specs=[pl.BlockSpec((ts, LANE), lambda i: (i, 0)), pl.BlockSpec((LANE, N), lambda i: (0, 0)), pl.BlockSpec((1, N), lambda i: (0, 0))],
        out_shape=[jax.ShapeDtypeStruct((S, LANE), BF16), jax.ShapeDtypeStruct((LANE, N), F32), jax.ShapeDtypeStruct((1, N), F32)],
        compiler_params=_cp(("arbitrary",)), name=name,
    )(dla, z, wgp, wgp_t, bgp)


def _chunk_fwd_terms(q, k, la, tril):
    bc = _dot_exact(tril, la)
    bend = jnp.sum(la, axis=0, keepdims=True)
    eb = jnp.exp(bc)
    enb = jnp.exp(-bc)
    ee = jnp.exp(bend - bc)
    qs = q * (DK ** -0.5)
    return bend, eb, enb, ee, qs * eb, qs * enb, k * enb, k * eb, k * ee


def _gla_fwd(z, la, gn, name):
    S = z.shape[0]
    W = HEADS * LANE
    tb = _div(S, 512, CHUNK)
    cpb = tb // CHUNK

    def body(q_ref, k_ref, v_ref, r_ref, la_ref, gn_ref, o_ref, sp_ref, y_ref, st):
        @pl.when(pl.program_id(0) == 0)
        def _():
            st[...] = jnp.zeros_like(st)

        ri = lax.broadcasted_iota(jnp.int32, (CHUNK, CHUNK), 0)
        ci = lax.broadcasted_iota(jnp.int32, (CHUNK, CHUNK), 1)
        tri = ri >= ci
        tril = tri.astype(F32)

        def chunk(c, carry):
            rows = pl.ds(pl.multiple_of(c * CHUNK, CHUNK), CHUNK)
            for h in range(HEADS):
                ln = pl.ds(h * LANE, LANE)
                q, k, v, lav = q_ref[rows, ln], k_ref[rows, ln], v_ref[rows, ln], la_ref[rows, ln]
                bend, _, _, _, qf, qb, kb, kf, ke = _chunk_fwd_terms(q, k, lav, tril)
                att = jnp.where(tri, _dot_nt(qf, kb), _dot_nt(qb, kf))
                s_prev = st[h]
                o = _dot(att, v) + _dot_nt(qf, s_prev)
                sp_ref[h, c] = s_prev
                st[h] = s_prev * jnp.exp(bend) + _dot_tn(v, ke)
                o_ref[rows, ln] = o
                rms = lax.rsqrt(jnp.mean(o * o, axis=-1, keepdims=True) + EPS)
                rv = r_ref[rows, ln]
                y_ref[rows, ln] = (o * rms * gn_ref[pl.ds(h, 1), :] * (rv * _sigmoid(rv))).astype(BF16)
            return carry

        lax.fori_loop(0, cpb, chunk, 0, unroll=2)

    zb = lambda base: pl.BlockSpec((tb, W), lambda i: (i, base // W))
    hb_ = pl.BlockSpec((tb, W), lambda i: (i, 0))
    return pl.pallas_call(
        body, grid=(S // tb,),
        in_specs=[zb(ZC_Q), zb(ZC_K), zb(ZC_V), zb(ZC_R), hb_, pl.BlockSpec((HEADS, DV), lambda i: (0, 0))],
        out_specs=[hb_, pl.BlockSpec((HEADS, cpb, DV, DKP), lambda i: (0, i, 0, 0)), hb_],
        out_shape=[jax.ShapeDtypeStruct((S, D_GLA), F32), jax.ShapeDtypeStruct((HEADS, S // CHUNK, DV, DKP), F32),
                   jax.ShapeDtypeStruct((S, D_GLA), BF16)],
        scratch_shapes=[pltpu.VMEM((HEADS, DV, DKP), F32)],
        compiler_params=_cp(("arbitrary",)), name=name,
    )(z, z, z, z, la, gn)


def _gla_bwd(dycat, z, la, o_raw, sprev, gn, name):
    S = z.shape[0]
    W = HEADS * LANE
    tb = _div(S, 512, CHUNK)
    cpb = tb // CHUNK
    nb = S // tb

    def body(q_ref, k_ref, v_ref, r_ref, la_ref, o_ref, sp_ref, dy_ref, gn_ref, dz_ref, dla_ref, dgn_ref, dst):
        @pl.when(pl.program_id(0) == 0)
        def _():
            dst[...] = jnp.zeros_like(dst)
            dgn_ref[...] = jnp.zeros_like(dgn_ref)

        ri = lax.broadcasted_iota(jnp.int32, (CHUNK, CHUNK), 0)
        ci = lax.broadcasted_iota(jnp.int32, (CHUNK, CHUNK), 1)
        tri = ri >= ci
        tril = tri.astype(F32)
        triu = (ri <= ci).astype(F32)

        def chunk(cc, carry):
            c = cpb - 1 - cc
            rows = pl.ds(pl.multiple_of(c * CHUNK, CHUNK), CHUNK)
            for h in range(HEADS):
                ln = pl.ds(h * LANE, LANE)
                q, k, v, lav = q_ref[rows, ln], k_ref[rows, ln], v_ref[rows, ln], la_ref[rows, ln]
                bend, eb, enb, ee, qf, qb, kb, kf, ke = _chunk_fwd_terms(q, k, lav, tril)
                att = jnp.where(tri, _dot_nt(qf, kb), _dot_nt(qb, kf))
                s_prev = sp_ref[h, c]
                gdec = jnp.exp(bend)
                gn = gn_ref[pl.ds(h, 1), :]
                o = o_ref[rows, ln]
                rv = r_ref[rows, ln]
                dy = dy_ref[rows, ln]
                rms = lax.rsqrt(jnp.mean(o * o, axis=-1, keepdims=True) + EPS)
                oh = o * rms
                sg = _sigmoid(rv)
                sr = rv * sg
                dz_ref[rows, pl.ds(ZC_R + h * LANE, LANE)] = (dy * oh * gn * (sg * (1.0 + rv * (1.0 - sg)))).astype(BF16)
                dgn_ref[pl.ds(h, 1), :] += jnp.sum(dy * sr * oh, axis=0, keepdims=True)
                w = dy * sr * gn
                do = rms * (w - oh * jnp.mean(w * oh, axis=-1, keepdims=True))
                datt = _dot_nt(do, v)
                daf = jnp.where(tri, datt, 0.0)
                dab = jnp.where(tri, 0.0, datt)
                ds = dst[h]
                dz_ref[rows, pl.ds(ZC_V + h * LANE, LANE)] = (_dot_tn(att, do) + _dot_nt(ke, ds)).astype(BF16)
                dke = _dot(v, ds)
                dqf = _dot(daf, kb) + _dot(do, s_prev)
                dkb = _dot_tn(daf, qf)
                dqb = _dot(dab, kf)
                dkf = _dot_tn(dab, qb)
                dg = jnp.sum(ds * s_prev, axis=0, keepdims=True)
                dst[h] = ds * gdec + _dot_tn(do, qf)
                dz_ref[rows, pl.ds(ZC_Q + h * LANE, LANE)] = ((dqf * eb + dqb * enb) * (DK ** -0.5)).astype(BF16)
                dz_ref[rows, pl.ds(ZC_K + h * LANE, LANE)] = (dkb * enb + dkf * eb + dke * ee).astype(BF16)
                dbc = dqf * qf - dkb * kb - dqb * qb + dkf * kf - dke * ke
                dbend = jnp.sum(dke * ke, axis=0, keepdims=True) + dg * gdec
                dla_ref[rows, ln] = _dot_exact(triu, dbc) + dbend
            return carry

        lax.fori_loop(0, cpb, chunk, 0, unroll=2)

    zb = lambda base: pl.BlockSpec((tb, W), lambda i: (nb - 1 - i, base // W))
    hb_ = pl.BlockSpec((tb, W), lambda i: (nb - 1 - i, 0))
    return pl.pallas_call(
        body, grid=(nb,),
        in_specs=[zb(ZC_Q), zb(ZC_K), zb(ZC_V), zb(ZC_R), hb_, hb_,
                  pl.BlockSpec((HEADS, cpb, DV, DKP), lambda i: (0, nb - 1 - i, 0, 0)),
                  pl.BlockSpec((tb, W), lambda i: (nb - 1 - i, 1)),
                  pl.BlockSpec((HEADS, DV), lambda i: (0, 0))],
        out_specs=[pl.BlockSpec((tb, Z_GLA), lambda i: (nb - 1 - i, 0)), hb_, pl.BlockSpec((HEADS, DV), lambda i: (0, 0))],
        out_shape=[jax.ShapeDtypeStruct((S, Z_GLA), BF16), jax.ShapeDtypeStruct((S, HEADS * DKP), F32), jax.ShapeDtypeStruct((HEADS, DV), F32)],
        scratch_shapes=[pltpu.VMEM((HEADS, DV, DKP), F32)],
        compiler_params=_cp(("arbitrary",)), name=name,
    )(z, z, z, z, la, o_raw, sprev, dycat, gn)


def _mod_proj(c_all, w3, layer, b, name):
    B, D = c_all.shape
    N = w3.shape[2]
    tn = _div(N, 1024, LANE)

    def body(c_ref, w_ref, b_ref, o_ref):
        cv = c_ref[...]
        o_ref[...] = _dot(cv * _sigmoid(cv), w_ref[...]) + b_ref[...]

    return pl.pallas_call(
        body, grid=(N // tn,),
        in_specs=[pl.BlockSpec((B, D), lambda j: (0, 0)), pl.BlockSpec((None, D, tn), lambda j: (layer, 0, j)), pl.BlockSpec((1, tn), lambda j: (0, j))],
        out_specs=pl.BlockSpec((B, tn), lambda j: (0, j)), out_shape=jax.ShapeDtypeStruct((B, N), F32),
        compiler_params=_cp(("parallel",)), name=name,
    )(c_all, w3, b)


def _mod_wgrad(c_t, dm, name):
    D, B = c_t.shape
    N = dm.shape[1]
    tn = _div(N, 1024, LANE)

    def body(c_ref, d_ref, o_ref):
        cv = c_ref[...]
        ca = cv * _sigmoid(cv)
        acc = ca[:, 0:1] * d_ref[pl.ds(0, 1), :]
        for b in range(1, B):
            acc = acc + ca[:, b:b + 1] * d_ref[pl.ds(b, 1), :]
        o_ref[...] = acc

    return pl.pallas_call(
        body, grid=(N // tn,),
        in_specs=[pl.BlockSpec((D, B), lambda j: (0, 0)), pl.BlockSpec((B, tn), lambda j: (0, j))],
        out_specs=pl.BlockSpec((D, tn), lambda j: (0, j)), out_shape=jax.ShapeDtypeStruct((D, N), F32),
        compiler_params=_cp(("parallel",)), name=name,
    )(c_t, dm)


def _rowsum(xs, name):
    n, N = xs.shape
    tn = _div(N, 8192, LANE)

    def body(x_ref, o_ref):
        acc = x_ref[pl.ds(0, 1), :]
        for r in range(1, n):
            acc = acc + x_ref[pl.ds(r, 1), :]
        o_ref[...] = acc

    return pl.pallas_call(
        body, grid=(N // tn,), in_specs=[pl.BlockSpec((n, tn), lambda j: (0, j))],
        out_specs=pl.BlockSpec((1, tn), lambda j: (0, j)), out_shape=jax.ShapeDtypeStruct((1, N), F32),
        compiler_params=_cp(("parallel",)), name=name,
    )(xs)


def _adamw(w, g, m, v, name):
    R, C = w.shape
    tr = _div(R, max(8, (1 << 18) // C), 8)

    def body(w_ref, g_ref, m_ref, v_ref, d_ref, nm_ref, nv_ref):
        gv = g_ref[...]
        mn = ADAM_B1 * m_ref[...] + (1.0 - ADAM_B1) * gv
        vn = ADAM_B2 * v_ref[...] + (1.0 - ADAM_B2) * (gv * gv)
        m_hat = mn / (1.0 - ADAM_B1 ** ADAM_STEP)
        v_hat = vn / (1.0 - ADAM_B2 ** ADAM_STEP)
        d_ref[...] = -ADAM_LR * (m_hat / (jnp.sqrt(v_hat) + ADAM_EPS) + ADAM_WD * w_ref[...])
        nm_ref[...] = mn
        nv_ref[...] = vn

    blk = pl.BlockSpec((tr, C), lambda i: (i, 0))
    os_ = jax.ShapeDtypeStruct((R, C), F32)
    return pl.pallas_call(
        body, grid=(R // tr,), in_specs=[blk] * 4, out_specs=[blk] * 3, out_shape=[os_] * 3,
        compiler_params=_cp(("parallel",)), name=name,
    )(w, g, m, v)


def _place():
    return lax.axis_index("x"), lax.axis_index("y"), lax.axis_index("c")


def _other_chips(x, y):
    return [(1 - x, y), (x, 1 - y), (1 - x, 1 - y)]


def _half(c, rows):
    return pl.ds(c * (rows // 2), rows // 2)


_ANY = pl.BlockSpec(memory_space=pl.ANY)


def _ag_small(v, name):
    r, n = v.shape

    def body(v_ref, o_ref, send_sems, recv_sems):
        x, y, c = _place()
        me = 4 * x + 2 * y + c
        o_ref[pl.ds(me, 1)] = v_ref[...][None]
        peers = [(x ^ (k >> 2), y ^ ((k >> 1) & 1), c ^ (k & 1)) for k in range(1, 8)]
        copies = []
        for k, peer in enumerate(peers):
            cp = pltpu.make_async_remote_copy(
                src_ref=v_ref, dst_ref=o_ref.at[me], send_sem=send_sems.at[k], recv_sem=recv_sems.at[k],
                device_id=peer, device_id_type=MESH)
            cp.start()
            copies.append(cp)
        for cp in copies:
            cp.wait()

    return pl.pallas_call(
        body, out_shape=jax.ShapeDtypeStruct((8, r, n), v.dtype),
        in_specs=[pl.BlockSpec(memory_space=pltpu.VMEM)], out_specs=pl.BlockSpec(memory_space=pltpu.VMEM),
        scratch_shapes=[pltpu.SemaphoreType.DMA((7,)), pltpu.SemaphoreType.DMA((7,))],
        compiler_params=pltpu.CompilerParams(vmem_limit_bytes=VMEM_LIMIT), name=name,
    )(v)


def _rs_sibling(gs, name):
    n = len(gs)

    def body(*refs):
        src, out = refs[:n], refs[n:2 * n]
        send_sems, recv_sems = refs[2 * n:]
        x, y, c = _place()
        copies = []
        for i in range(n):
            cp = pltpu.make_async_remote_copy(
                src_ref=src[i].at[:, _half(1 - c, src[i].shape[1])], dst_ref=out[i], send_sem=send_sems.at[i], recv_sem=recv_sems.at[i],
                device_id=(x, y, 1 - c), device_id_type=MESH)
            cp.start()
            copies.append(cp)
        for cp in copies:
            cp.wait()

    return pl.pallas_call(
        body, out_shape=[jax.ShapeDtypeStruct((N_CHIPS, g.shape[1] // 2, g.shape[2]), g.dtype) for g in gs],
        in_specs=[_ANY] * n, out_specs=[_ANY] * n,
        scratch_shapes=[pltpu.SemaphoreType.DMA((n,)), pltpu.SemaphoreType.DMA((n,))],
        compiler_params=pltpu.CompilerParams(has_side_effects=True), name=name,
    )(*gs)


def _rs_presum(g, sib, c_arr, name):
    ns, R, C = g.shape
    rh = R // 2
    tr = _div(rh, max(16, (1 << 19) // C), 16)
    nrb = rh // tr

    def body(c_ref, g_ref, s_ref, o_ref):
        o_ref[...] = (g_ref[...] + s_ref[...]).astype(BF16)

    return pl.pallas_call(
        body, out_shape=jax.ShapeDtypeStruct((ns, rh, C), BF16),
        grid_spec=pltpu.PrefetchScalarGridSpec(
            num_scalar_prefetch=1, grid=(ns, nrb),
            in_specs=[pl.BlockSpec((None, tr, C), lambda s, r, c_ref: (s, c_ref[0] * nrb + r, 0)),
                      pl.BlockSpec((None, tr, C), lambda s, r, c_ref: (s, r, 0))],
            out_specs=pl.BlockSpec((None, tr, C), lambda s, r, c_ref: (s, r, 0))),
        compiler_params=_cp(("parallel", "parallel")), name=name,
    )(c_arr, g, sib)


def _rs_sum(g, sib, recv, full, layer, sc_arr, name):
    ns, R, C = g.shape
    rh = R // 2
    tr = _div(rh, max(16, (1 << 18) // C), 16)
    nrb = rh // tr

    def body(sc_ref, g_ref, s_ref, r_ref, f_ref, o_ref):
        acc = g_ref[...] + s_ref[...]
        for j in range(3):
            acc = acc + r_ref[j].astype(F32)
        o_ref[...] = acc

    return pl.pallas_call(
        body, out_shape=jax.ShapeDtypeStruct(full.shape, F32),
        grid_spec=pltpu.PrefetchScalarGridSpec(
            num_scalar_prefetch=1, grid=(nrb,),
            in_specs=[pl.BlockSpec((None, tr, C), lambda r, sc: (sc[0], sc[1] * nrb + r, 0)),
                      pl.BlockSpec((None, tr, C), lambda r, sc: (sc[0], r, 0)),
                      pl.BlockSpec((3, tr, C), lambda r, sc: (0, r, 0)),
                      _ANY],
            out_specs=pl.BlockSpec((None, tr, C), lambda r, sc: (layer, sc[1] * nrb + r, 0))),
        input_output_aliases={4: 0},
        compiler_params=_cp(("parallel",)), name=name,
    )(sc_arr, g, sib, recv, full)


def _rs_share(fulls, layer, name):
    n = len(fulls)

    def body(*refs):
        src, out = refs[:n], refs[n:2 * n]
        send_sems, recv_sems = refs[2 * n:]
        x, y, c = _place()
        copies = []
        for i in range(n):
            rows = out[i].shape[1]
            cp = pltpu.make_async_remote_copy(
                src_ref=out[i].at[layer, _half(c, rows)], dst_ref=out[i].at[layer, _half(c, rows)],
                send_sem=send_sems.at[i], recv_sem=recv_sems.at[i], device_id=(x, y, 1 - c), device_id_type=MESH)
            cp.start()
            copies.append(cp)
        for cp in copies:
            cp.wait()

    return pl.pallas_call(
        body, out_shape=[jax.ShapeDtypeStruct(f.shape, f.dtype) for f in fulls],
        in_specs=[_ANY] * n, out_specs=[_ANY] * n, input_output_aliases={i: i for i in range(n)},
        scratch_shapes=[pltpu.SemaphoreType.DMA((n,)), pltpu.SemaphoreType.DMA((n,))],
        compiler_params=pltpu.CompilerParams(has_side_effects=True), name=name,
    )(*fulls)


_HBM = pl.BlockSpec(memory_space=pltpu.HBM)
_SEM = pl.BlockSpec(memory_space=pltpu.SEMAPHORE)
_EFFECT = pltpu.SideEffectType.DATAFLOW_SIDE_EFFECTING


def _in_hbm(a):
    return pltpu.with_memory_space_constraint(a, pltpu.HBM)


def _split_start(bufs, n_sem, copies_of, name):
    nb = len(bufs)

    def body(*refs):
        for cp in copies_of(refs[:nb], refs[nb], refs[nb + 1]):
            cp.start()
        refs[-1][...] = jnp.zeros_like(refs[-1])

    out = pl.pallas_call(
        body, name=name,
        out_shape=(pltpu.SemaphoreType.DMA((n_sem,)), pltpu.SemaphoreType.DMA((n_sem,)), *[pltpu.HBM(a.shape, a.dtype) for a in bufs],
                   jax.ShapeDtypeStruct((SUB, LANE), F32)),
        in_specs=[_HBM] * nb, out_specs=(_SEM, _SEM, *([_HBM] * nb), pl.BlockSpec(memory_space=pltpu.VMEM)),
        input_output_aliases={i: 2 + i for i in range(nb)},
        compiler_params=pltpu.CompilerParams(has_side_effects=_EFFECT),
    )(*[_in_hbm(a) for a in bufs])
    return out[0], out[1], list(out[2:2 + nb]), out[-1]


def _split_wait(send_sems, recv_sems, bufs, after, copies_of, name):
    nb = len(bufs)

    def body(*refs):
        for cp in copies_of(refs[:nb], refs[nb], refs[nb + 1]):
            cp.wait_send()
            cp.wait_recv()

    return list(pl.pallas_call(
        body, name=name, out_shape=[pltpu.HBM(a.shape, a.dtype) for a in bufs],
        in_specs=[_HBM] * nb + [_SEM, _SEM, _ANY], out_specs=[_HBM] * nb,
        input_output_aliases={i: i for i in range(nb)},
        compiler_params=pltpu.CompilerParams(has_side_effects=_EFFECT),
    )(*bufs, send_sems, recv_sems, after))


def _ag_half_copies(land, send_sems, recv_sems, landing_of_mine):
    x, y, c = _place()
    cps = []
    for j, (cx, cy) in enumerate(_other_chips(x, y)):
        for i in range(len(land)):
            rows = _half(c, land[i].shape[1])
            s = 2 * x + y if landing_of_mine else 2 * cx + cy
            cps.append(pltpu.make_async_remote_copy(
                src_ref=land[i].at[2 * x + y, rows], dst_ref=land[i].at[s, rows], send_sem=send_sems.at[3 * i + j], recv_sem=recv_sems.at[3 * i + j],
                device_id=(cx, cy, c), device_id_type=MESH))
    return cps


def _ag_starts(land, send_sems, recv_sems):
    return _ag_half_copies(land, send_sems, recv_sems, True)


def _ag_waits(land, send_sems, recv_sems):
    return _ag_half_copies(land, send_sems, recv_sems, False)


def _ag_finish(lands, name):
    n = len(lands)

    def body(*refs):
        land = refs[n:2 * n]
        send_sems, recv_sems = refs[2 * n:]
        x, y, c = _place()
        sibling = (x, y, 1 - c)

        def copy(k, i, s, h):
            blk = land[i].at[s, _half(h, land[i].shape[1])]
            return pltpu.make_async_remote_copy(
                src_ref=blk, dst_ref=blk, send_sem=send_sems.at[k], recv_sem=recv_sems.at[k], device_id=sibling, device_id_type=MESH)

        chips = _other_chips(x, y)
        passed = [copy(3 * i + j, i, 2 * cx + cy, c) for j, (cx, cy) in enumerate(chips) for i in range(n)]
        for cp in passed:
            cp.start()
        for j, (cx, cy) in enumerate(chips):
            for i in range(n):
                copy(3 * i + j, i, 2 * cx + cy, 1 - c).wait_recv()
        for cp in passed:
            cp.wait_send()

    return pl.pallas_call(
        body, out_shape=[jax.ShapeDtypeStruct(a.shape, a.dtype) for a in lands],
        in_specs=[_ANY] * n, out_specs=[_ANY] * n, input_output_aliases={i: i for i in range(n)},
        scratch_shapes=[pltpu.SemaphoreType.DMA((3 * n,)), pltpu.SemaphoreType.DMA((3 * n,))],
        compiler_params=pltpu.CompilerParams(has_side_effects=True), name=name,
    )(*lands)


def _rs_chip_copies(bufs, send_sems, recv_sems):
    n = len(bufs) // 2
    x, y, c = _place()
    return [pltpu.make_async_remote_copy(
        src_ref=bufs[i].at[2 * cx + cy], dst_ref=bufs[n + i].at[j], send_sem=send_sems.at[3 * i + j], recv_sem=recv_sems.at[3 * i + j],
        device_id=(cx, cy, c), device_id_type=MESH) for j, (cx, cy) in enumerate(_other_chips(x, y)) for i in range(n)]


def _pad_heads(w):
    lead = w.shape[:-1]
    w4 = w.reshape(*lead, HEADS, DK)
    w4 = jnp.pad(w4, [(0, 0)] * len(lead) + [(0, 0), (0, DKP - DK)])
    return w4.reshape(*lead, HEADS * DKP)


def _unpad_heads(w):
    lead = w.shape[:-1]
    return w.reshape(*lead, HEADS, DKP)[..., :DK].reshape(*lead, HEADS * DK)


def _mix_weight(win4, n_cols):
    D = win4.shape[1]
    w = jnp.transpose(win4[:, :, :n_cols], (1, 0, 2)).reshape(D, N_CHIPS * n_cols)
    o = 2 * D_CONV
    hk = HEADS * DK
    ab = w[:, :o]
    q = _pad_heads(w[:, o:o + hk])
    k = _pad_heads(w[:, o + hk:o + 2 * hk])
    vr = w[:, o + 2 * hk:o + 2 * hk + 2 * D_GLA]
    glr = jnp.pad(w[:, o + 2 * hk + 2 * D_GLA:], ((0, 0), (0, LANE - GATE_RANK)))
    return jnp.concatenate([q, k, vr, ab, glr], axis=1)


def _mix_weight_grad(dgla, dab, dglr, n_cols, n_pad):
    D = dab.shape[0]
    hkp = HEADS * DKP
    w = jnp.concatenate([dab, _unpad_heads(dgla[:, :hkp]), _unpad_heads(dgla[:, hkp:2 * hkp]), dgla[:, 2 * hkp:], dglr[:, :GATE_RANK]], axis=1)
    w = jnp.pad(w.reshape(D, N_CHIPS, n_cols), ((0, 0), (0, 0), (0, n_pad - n_cols)))
    return jnp.transpose(w, (1, 0, 2))


_ARG_NAMES = ['x', 'c', 'w_ada', 'b_ada', 'g_norm_ffn1', 'w_ffn1_in', 'w_ffn1_out', 'g_norm_mix', 'w_in', 'w_dw', 'b_dw', 'g_conv_ln', 'b_conv_ln', 'w_gate_up', 'b_gate', 'g_gla_norm', 'w_out', 'g_norm_ffn2', 'w_ffn2_in', 'w_ffn2_out', 'g_norm_final', 'w_ada_final', 'b_ada_final']
_WEIGHTS = _ARG_NAMES[2:]
_BIG = ('w_ffn1_in', 'w_ffn1_out', 'w_in', 'w_out', 'w_ffn2_in', 'w_ffn2_out')
_SMALL = ('g_norm_ffn1', 'g_norm_mix', 'w_dw', 'b_dw', 'g_conv_ln', 'b_conv_ln', 'w_gate_up', 'b_gate', 'g_gla_norm', 'g_norm_ffn2', 'g_norm_final')


def _ffn_fwd(x, h, gv, w4, wo, nxt, tag):
    z, act = _ffn_up(h, w4, f"ffn_up_{tag}")
    y, xn, *hn = _ffn_down(act, wo, x, gv, nxt, f"ffn_down_{tag}")
    return xn, (hn[0] if hn else None), y, (x, h, z, act)


def _ffn_bwd(dxn, dy, saved, g, scale, prev, w4, wo, tag):
    x, h, z, act = saved
    ns, _, C = w4.shape
    dz = _ffn_bwd_act(dy, wo, z, f"ffn_bwd_act_{tag}")
    dwo = _mm_tn(act, dy, f"dw_out_{tag}")
    dx, dsh, dsc, dg, *pv = _dh_normmod_bwd([(dz, w4, s, s, C) for s in range(ns)], x, dxn, g, scale, prev, f"ffn_dh_{tag}")
    dwi = _dw_ffn_in(h, dz, ns, f"dw_in_{tag}")
    return dx, pv, dict(dshift=dsh, dscale=dsc, dg=dg, dw_in=dwi, dw_out=dwo.reshape(N_CHIPS, -1, dwo.shape[1]))


def _mix_fwd(x, h, gv, wmix, w_dw, b_dw, g_ln, b_ln, wgp, bgp, gn, wout, nxt, tag):
    z = _mm([(h, wmix, 0)], F32, f"mix_in_{tag}")
    u, yc, yconv = _conv_fwd(z, w_dw, b_dw, g_ln, b_ln, f"conv_fwd_{tag}")
    la = _loggate(z, wgp, bgp, f"loggate_{tag}")
    o_raw, sprev, ygla = _gla_fwd(z, la, gn, f"gla_fwd_{tag}")
    y, xn, *hn = _mix_out(yconv, ygla, wout, x, gv, nxt, f"mix_out_{tag}")
    return xn, (hn[0] if hn else None), y, (x, h, z, u, yc, la, o_raw, sprev, yconv, ygla)


def _mix_bwd(dxn, dy, saved, g, scale, prev, wmix, w_dw, g_ln, b_ln, wgp, bgp, gn, wout, n_cols, n_pad, tag):
    x, h, z, u, yc, la, o_raw, sprev, yconv, ygla = saved
    dycat = _mm([(dy, wout, 0)], F32, f"mix_dycat_{tag}", nt=True)
    dwout = _mm_tn_two(yconv, ygla, dy, f"dw_mixout_{tag}")
    dab, dwdw, dbdw, dgln, dbln = _conv_bwd(dycat, z, u, yc, w_dw, g_ln, b_ln, f"conv_bwd_{tag}")
    dgla, dla, dgn = _gla_bwd(dycat, z, la, o_raw, sprev, gn, f"gla_bwd_{tag}")
    dglr, dwgp, dbgp = _loggate_bwd(dla, z, wgp, wgp.T, bgp, f"loggate_bwd_{tag}")
    dx, dsh, dsc, dg, *pv = _dh_normmod_bwd(
        [(dgla, wmix, 0, 0, Z_GLA), (dab, wmix, 0, ZC_A // (2 * D_CONV), 2 * D_CONV), (dglr, wmix, 0, ZC_G // LANE, LANE)],
        x, dxn, g, scale, prev, f"mix_dh_{tag}")
    dwin = _mix_weight_grad(_mm_tn(h, dgla, f"dw_mixin_gla_{tag}"), _mm_tn(h, dab, f"dw_mixin_conv_{tag}"), _mm_tn(h, dglr, f"dw_mixin_gate_{tag}"),
                            n_cols, n_pad)
    grads = dict(dshift=dsh, dscale=dsc, dg=dg, dw_in=dwin, dw_out=dwout.reshape(N_CHIPS, -1, dwout.shape[1]), dw_dw=dwdw, db_dw=dbdw,
                 dg_ln=dgln, db_ln=dbln, dw_gate=_unpad_heads(dwgp[:GATE_RANK]), db_gate=_unpad_heads(dbgp)[0], dgn=dgn)
    return dx, pv, grads


def kernel(x, c, w_ada, b_ada, g_norm_ffn1, w_ffn1_in, w_ffn1_out, g_norm_mix, w_in, w_dw, b_dw, g_conv_ln, b_conv_ln, w_gate_up, b_gate, g_gla_norm, w_out, g_norm_ffn2, w_ffn2_in, w_ffn2_out, g_norm_final, w_ada_final, b_ada_final, loss_target, m_w_ada, m_b_ada, m_g_norm_ffn1, m_w_ffn1_in, m_w_ffn1_out, m_g_norm_mix, m_w_in, m_w_dw, m_b_dw, m_g_conv_ln, m_b_conv_ln, m_w_gate_up, m_b_gate, m_g_gla_norm, m_w_out, m_g_norm_ffn2, m_w_ffn2_in, m_w_ffn2_out, m_g_norm_final, m_w_ada_final, m_b_ada_final, v_w_ada, v_b_ada, v_g_norm_ffn1, v_w_ffn1_in, v_w_ffn1_out, v_g_norm_mix, v_w_in, v_w_dw, v_b_dw, v_g_conv_ln, v_b_conv_ln, v_w_gate_up, v_b_gate, v_g_gla_norm, v_w_out, v_g_norm_ffn2, v_w_ffn2_in, v_w_ffn2_out, v_g_norm_final, v_w_ada_final, v_b_ada_final):
    given = dict(locals())
    W = {n: given[n] for n in _WEIGHTS}
    M1 = {n: given["m_" + n] for n in _WEIGHTS}
    M2 = {n: given["v_" + n] for n in _WEIGHTS}
    xs = x[0]
    tgt = loss_target[0]
    S, D = xs.shape
    L = w_ada.shape[0]
    xi, yi, ci = _place()
    s_me = 2 * xi + yi
    b_me = 4 * xi + 2 * yi + ci
    nsh = w_ada.shape[2]
    nfin = w_ada_final.shape[1]
    n_cols = w_in.shape[2]
    n_pad = -(-n_cols // LANE) * LANE

    def lands_of(l):
        shards = [W[n][l].astype(BF16) for n in _BIG]
        shards[2] = jnp.pad(shards[2], ((0, 0), (0, n_pad - n_cols)))
        return [lax.dynamic_update_index_in_dim(lax.empty((N_CHIPS,) + s.shape, BF16), s, s_me, 0) for s in shards]

    lands = {l: lands_of(l) for l in range(L)}
    ag_groups = [dict(l=0, items=[0, 1], need=0), dict(l=0, items=[2, 3, 4, 5], need=1)]
    ag_groups += [dict(l=l, items=list(range(len(_BIG))), need=3 * l) for l in range(1, L)]

    def ag_start(grp):
        bufs = [lands[grp["l"]][i] for i in grp["items"]]
        return _split_start(bufs, 3 * len(bufs), _ag_starts, f"ag_start_l{grp['l']}_{grp['items'][0]}")

    pend = ag_start(ag_groups[0])
    tok = pend[3][0, 0]

    c_all = _ag_small(c.reshape(8, D // 8) + tok, "ag_c").reshape(8, D)
    tok = None
    parts = [_mod_proj(c_all, w_ada, l, lax.dynamic_slice(b_ada, (l, s_me * nsh), (1, nsh)), f"mod_proj_{l}") for l in range(L)]
    parts.append(_mod_proj(c_all, w_ada_final[None], 0, lax.dynamic_slice(b_ada_final, (s_me * nfin,), (nfin,))[None], "mod_proj_final"))
    mod_all = _ag_small(jnp.concatenate(parts, axis=1), "ag_mod")
    mine = [lax.dynamic_index_in_dim(lax.dynamic_index_in_dim(mod_all, 2 * s + ci, 0, False), b_me, 0, False) for s in range(N_CHIPS)]
    mods = [jnp.concatenate([mine[s][l * nsh:(l + 1) * nsh] for s in range(N_CHIPS)]).reshape(N_MOD, 1, D) for l in range(L)]
    fmod = jnp.concatenate([mine[s][L * nsh:] for s in range(N_CHIPS)]).reshape(2, 1, D)

    tiny = jnp.concatenate([w_dw.reshape(-1), w_gate_up.reshape(-1)])
    tiny_all = _ag_small(jnp.pad(tiny, (0, (-tiny.shape[0]) % (8 * LANE))).reshape(8, -1), "ag_tiny").reshape(8, -1)
    n_dw = w_dw.size
    dw_parts = [lax.dynamic_index_in_dim(tiny_all, 2 * s + ci, 0, False) for s in range(N_CHIPS)]
    w_dw_full = jnp.concatenate([p[:n_dw].reshape(w_dw.shape) for p in dw_parts], axis=2)
    w_gu_full = jnp.concatenate([p[n_dw:n_dw + w_gate_up.size].reshape(w_gate_up.shape) for p in dw_parts], axis=2)

    def layer_weights(l, lands):
        wi1, wo1, win4, wout4, wi2, wo2 = lands
        return dict(
            wi1=wi1, wo1=wo1.reshape(-1, D), wi2=wi2, wo2=wo2.reshape(-1, D), wout=wout4.reshape(-1, D), wmix=_mix_weight(win4, n_cols),
            wgp=jnp.pad(_pad_heads(w_gu_full[l]), ((0, LANE - GATE_RANK), (0, 0))).astype(BF16), bgp=_pad_heads(b_gate[l])[None])

    gnorm = (g_norm_ffn1, g_norm_mix, g_norm_ffn2)
    subs = [dict(l=l, j=j, tag=f"{('ffn1', 'mix', 'ffn2')[j]}_l{l}", g=gnorm[j][l][None], shift=mods[l][3 * j], scale=mods[l][3 * j + 1],
                 gv=mods[l][3 * j + 2] * (1.0 if j == 1 else 0.5)) for l in range(L) for j in range(3)]
    gi = 0
    xcur = xs
    h = None
    for k, sb in enumerate(subs):
        l, j = sb["l"], sb["j"]
        if pend is not None and ag_groups[gi]["need"] == k:
            grp = ag_groups[gi]
            nm = f"l{grp['l']}_{grp['items'][0]}"
            after = xcur if k > 0 else sb["shift"]
            done = _ag_finish(_split_wait(pend[0], pend[1], pend[2], after, _ag_waits, f"ag_wait_{nm}"), f"ag_finish_{nm}")
            for i, a in zip(grp["items"], done):
                lands[grp["l"]][i] = a
            gi += 1
            pend = ag_start(ag_groups[gi]) if gi < len(ag_groups) else None
            tok = pend[3][0, 0] if pend is not None else None
        if h is None:
            h = _normmod(xcur, sb["g"] if tok is None else sb["g"] + tok, sb["shift"], sb["scale"], f"normmod_{sb['tag']}")
            tok = None
        d = layer_weights(l, lands[l])
        nxt = (subs[k + 1]["g"], subs[k + 1]["shift"], subs[k + 1]["scale"]) if k + 1 < len(subs) else None
        gv = sb["gv"] if tok is None else sb["gv"] + tok
        tok = None
        if j == 1:
            xcur, h, sb["y"], sb["saved"] = _mix_fwd(xcur, h, gv, d["wmix"], w_dw_full[l], b_dw[l][None], g_conv_ln[l][None],
                                                     b_conv_ln[l][None], d["wgp"], d["bgp"], g_gla_norm[l], d["wout"], nxt, sb["tag"])
        else:
            w4, wo = (d["wi1"], d["wo1"]) if j == 0 else (d["wi2"], d["wo2"])
            xcur, h, sb["y"], sb["saved"] = _ffn_fwd(xcur, h, gv, w4, wo, nxt, sb["tag"])
    lw = [layer_weights(l, lands[l]) for l in range(L)]

    c_arr = jnp.stack([ci]).astype(jnp.int32)
    sc_arr = jnp.stack([s_me, ci]).astype(jnp.int32)
    fulls = [lax.empty((L,) + ((W[n].shape[1], n_pad) if n == 'w_in' else W[n].shape[1:]), F32) for n in _BIG]

    def rs_begin(gs, items, l):
        nm = f"l{l}_{items[0]}"
        sibs = _rs_sibling(gs, f"rs_sibling_{nm}")
        return sibs, [_rs_presum(g, sb_, c_arr, f"rs_presum_{i}_l{l}") for i, g, sb_ in zip(items, gs, sibs)]

    def rs_end(gs, sibs, recvs, items, l):
        summed = [_rs_sum(g, sb_, rv, fulls[i], l, sc_arr, f"rs_sum_{i}_l{l}") for i, g, sb_, rv in zip(items, gs, sibs, recvs)]
        for i, f in zip(items, _rs_share(summed, l, f"rs_share_l{l}_{items[0]}")):
            fulls[i] = f

    def rs_start(gs, items, l):
        sibs, ps = rs_begin(gs, items, l)
        pend = _split_start(ps + [lax.empty((3,) + p.shape[1:], BF16) for p in ps], 3 * len(ps), _rs_chip_copies, f"rs_start_l{l}_{items[0]}")
        return dict(gs=gs, sibs=sibs, pend=pend, items=items, l=l)

    def rs_finish(fl, after):
        pend, n = fl["pend"], len(fl["gs"])
        bufs = _split_wait(pend[0], pend[1], pend[2], after, _rs_chip_copies, f"rs_wait_l{fl['l']}_{fl['items'][0]}")
        rs_end(fl["gs"], fl["sibs"], bufs[n:], fl["items"], fl["l"])

    dh, sq = _final_loss(xcur, g_norm_final[None], fmod[0], fmod[1], tgt)
    loss_part = 0.5 / D * jnp.sum(sq)
    dx, dfsh, dfsc, dgfin, dy, dgv = _normmod_bwd(xcur, dh, None, g_norm_final[None], fmod[1], (subs[-1]["y"], subs[-1]["gv"]), "normmod_bwd_final")
    G = {n: [None] * L for n in _SMALL}
    dmods = [None] * L
    in_flight = None
    tok = None
    for l in reversed(range(L)):
        gr = [None] * 3
        for j in reversed(range(3)):
            k = 3 * l + j
            sb, d = subs[k], lw[l]
            prev = (subs[k - 1]["y"], subs[k - 1]["gv"]) if k > 0 else None
            g_vec = sb["g"] if tok is None else sb["g"] + tok
            tok = None
            if j == 1:
                dx, pv, gr[j] = _mix_bwd(dx, dy, sb["saved"], g_vec, sb["scale"], prev, d["wmix"], w_dw_full[l], g_conv_ln[l][None], b_conv_ln[l][None],
                                         d["wgp"], d["bgp"], g_gla_norm[l], d["wout"], n_cols, n_pad, sb["tag"])
            else:
                w4, wo = (d["wi1"], d["wo1"]) if j == 0 else (d["wi2"], d["wo2"])
                dx, pv, gr[j] = _ffn_bwd(dx, dy, sb["saved"], g_vec, sb["scale"], prev, w4, wo, sb["tag"])
            gr[j]["dgv"] = dgv
            dy, dgv = pv if pv else (None, None)
            if j == 1 and in_flight is not None:
                rs_finish(in_flight, dx)
                in_flight = None
            if j == 1 and l == 0:
                in_flight = rs_start([gr[1]["dw_in"], gr[1]["dw_out"], gr[2]["dw_in"], gr[2]["dw_out"]], [2, 3, 4, 5], l)
                tok = in_flight["pend"][3][0, 0]
        g1, g2, g3 = gr
        if l > 0:
            in_flight = rs_start([g1["dw_in"], g1["dw_out"], g2["dw_in"], g2["dw_out"], g3["dw_in"], g3["dw_out"]], list(range(len(_BIG))), l)
            tok = in_flight["pend"][3][0, 0]
        else:
            last = rs_start([g1["dw_in"], g1["dw_out"]], [0, 1], l)
            rs_finish(in_flight, dx)
            in_flight = None
        dmods[l] = jnp.concatenate([g1["dshift"], g1["dscale"], 0.5 * g1["dgv"], g2["dshift"], g2["dscale"], g2["dgv"],
                                    g3["dshift"], g3["dscale"], 0.5 * g3["dgv"]], axis=1)[0]
        G["g_norm_ffn1"][l], G["g_norm_ffn2"][l], G["g_norm_mix"][l] = g1["dg"][0], g3["dg"][0], g2["dg"][0]
        G["w_dw"][l], G["b_dw"][l], G["g_conv_ln"][l], G["b_conv_ln"][l] = g2["dw_dw"], g2["db_dw"][0], g2["dg_ln"][0], g2["db_ln"][0]
        G["w_gate_up"][l], G["b_gate"][l], G["g_gla_norm"][l] = g2["dw_gate"], g2["db_gate"], g2["dgn"]
    grad_x = dx[None]
    gsm = {}

    small = [jnp.stack(G[n]).reshape(-1) for n in _SMALL if n != 'g_norm_final'] + [dgfin[0]]
    dmod_vec = jnp.concatenate(dmods + [dfsh[0], dfsc[0]])
    n_mod_vec = dmod_vec.shape[0]
    vec = jnp.concatenate([dmod_vec] + small + [loss_part[None]])
    n_vec = vec.shape[0]
    vec = jnp.pad(vec, (0, (-n_vec) % (8 * LANE)))
    vec_all = _ag_small(vec.reshape(8, -1), "ag_small_grads").reshape(8, -1)
    vec_sum = _rowsum(vec_all, "sum_small_grads")[0]
    loss = vec_sum[n_vec - 1]
    off = n_mod_vec
    for n in _SMALL:
        shp = {'w_dw': w_dw_full.shape, 'w_gate_up': w_gu_full.shape}.get(n, W[n].shape)
        cnt = 1
        for dd in shp:
            cnt *= dd
        gsm[n] = vec_sum[off:off + cnt].reshape(shp)
        off += cnt
    gsm['w_dw'] = lax.dynamic_slice_in_dim(gsm['w_dw'], s_me * w_dw.shape[2], w_dw.shape[2], 2)
    gsm['w_gate_up'] = lax.dynamic_slice_in_dim(gsm['w_gate_up'], s_me * w_gate_up.shape[2], w_gate_up.shape[2], 2)
    dmod_sum = vec_sum[:n_mod_vec]
    gsm['b_ada'] = dmod_sum[:L * N_MOD * D].reshape(L, N_MOD * D)
    gsm['b_ada_final'] = dmod_sum[L * N_MOD * D:]
    c_t = c_all.T
    dmod_rows = vec_all[:, :n_mod_vec]
    gsm['w_ada'] = jnp.stack([
        _mod_wgrad(c_t, lax.dynamic_slice_in_dim(dmod_rows, l * N_MOD * D + s_me * nsh, nsh, 1), f"dw_ada_{l}") for l in range(L)])
    gsm['w_ada_final'] = _mod_wgrad(c_t, lax.dynamic_slice_in_dim(dmod_rows, L * N_MOD * D + s_me * nfin, nfin, 1), "dw_ada_final")
    rs_finish(last, gsm['w_ada_final'])
    gsm.update({n: (f[:, :, :n_cols] if n == 'w_in' else f) for n, f in zip(_BIG, fulls)})

    outs = {}
    small_names = [n for n in _WEIGHTS if W[n].size < 65536]
    for n in _WEIGHTS:
        if n in small_names:
            continue
        shp = W[n].shape
        v2 = lambda a: a.reshape(-1, shp[-1])
        d_, m_, v_ = _adamw(v2(W[n]), v2(gsm[n]), v2(M1[n]), v2(M2[n]), f"adamw_{n}")
        outs[n] = (d_.reshape(shp), m_.reshape(shp), v_.reshape(shp))
    flat = lambda dct: jnp.concatenate([dct[n].reshape(-1) for n in small_names])
    n_small = sum(W[n].size for n in small_names)
    v2 = lambda a: jnp.pad(a, (0, (-n_small) % (8 * LANE))).reshape(-1, LANE)
    d_, m_, v_ = _adamw(v2(flat(W)), v2(flat(gsm)), v2(flat(M1)), v2(flat(M2)), "adamw_small")

    def unflat(a):
        res, o = {}, 0
        a = a.reshape(-1)
        for n in small_names:
            res[n] = a[o:o + W[n].size].reshape(W[n].shape)
            o += W[n].size
        return res

    for n, dd, mm, vv in zip(small_names, unflat(d_).values(), unflat(m_).values(), unflat(v_).values()):
        outs[n] = (dd, mm, vv)

    return (loss, grad_x, *[gsm[n] for n in _WEIGHTS], *[outs[n][0] for n in _WEIGHTS], *[outs[n][1] for n in _WEIGHTS], *[outs[n][2] for n in _WEIGHTS])
```

```python
import jax
import jax.numpy as jnp
from jax import lax
from jax.experimental import pallas as pl
from jax.experimental.pallas import tpu as pltpu

F32 = jnp.float32
BF16 = jnp.bfloat16

CHUNK = 64
HEADS = 4
DK = 64
DV = 128
DKP = 128
GATE_RANK = 16
GATE_TAU = 16.0
N_MOD = 9
EPS = 1e-6
ADAM_LR = 0.001
ADAM_B1 = 0.9
ADAM_B2 = 0.999
ADAM_EPS = 1e-08
ADAM_WD = 0.01
ADAM_STEP = 10

LANE = 128
HALO = 32
VMEM_LIMIT = 52 * 1024 * 1024
MESH = pl.DeviceIdType.MESH
N_CHIPS = 4

D_CONV = 512
D_GLA = HEADS * DV
ZC_Q = 0
ZC_K = ZC_Q + HEADS * DKP
ZC_V = ZC_K + HEADS * DKP
ZC_R = ZC_V + D_GLA
ZC_A = ZC_R + D_GLA
ZC_B = ZC_A + D_CONV
ZC_G = ZC_B + D_CONV
Z_COLS = ZC_G + LANE
Z_GLA = ZC_A


def _div(n, target, mult):
    best = None
    d = mult
    while d <= min(n, target):
        if n % d == 0:
            best = d
        d += mult
    return n if best is None else best


def _cp(sem=None, **kw):
    return pltpu.CompilerParams(dimension_semantics=sem, vmem_limit_bytes=VMEM_LIMIT, **kw)


def _resident(shape, index_map):
    return pl.BlockSpec(shape, index_map, pipeline_mode=pl.Buffered(1))


def _sigmoid(x):
    return 0.5 * jnp.tanh(0.5 * x) + 0.5


def _dot(a, b):
    return jnp.dot(a.astype(BF16), b.astype(BF16), preferred_element_type=F32)


def _dot_nt(a, b):
    return lax.dot_general(a.astype(BF16), b.astype(BF16), (((1,), (1,)), ((), ())), preferred_element_type=F32)


def _dot_tn(a, b):
    return lax.dot_general(a.astype(BF16), b.astype(BF16), (((0,), (0,)), ((), ())), preferred_element_type=F32)


def _dot_exact(a, b):
    return jnp.dot(a, b, preferred_element_type=F32, precision=lax.Precision.HIGHEST)


def _normmod(x, g, shift, scale, name):
    S, D = x.shape
    tm = _div(S, 512, 8)

    def body(x_ref, g_ref, sh_ref, sc_ref, o_ref):
        xv = x_ref[...]
        r = lax.rsqrt(jnp.mean(xv * xv, axis=-1, keepdims=True) + EPS)
        o_ref[...] = ((xv * r) * g_ref[...] * (1.0 + sc_ref[...]) + sh_ref[...]).astype(o_ref.dtype)

    row = pl.BlockSpec((tm, D), lambda i: (i, 0))
    vec = pl.BlockSpec((1, D), lambda i: (0, 0))
    return pl.pallas_call(
        body, grid=(S // tm,), in_specs=[row, vec, vec, vec], out_specs=row,
        out_shape=jax.ShapeDtypeStruct((S, D), BF16), compiler_params=_cp(("parallel",)), name=name,
    )(x, g, shift, scale)


def _final_loss(x, g, shift, scale, tgt):
    S, D = x.shape
    tm = _div(S, 512, 8)

    def body(x_ref, g_ref, sh_ref, sc_ref, t_ref, dh_ref, sq_ref):
        @pl.when(pl.program_id(0) == 0)
        def _():
            sq_ref[...] = jnp.zeros_like(sq_ref)

        xv = x_ref[...]
        r = lax.rsqrt(jnp.mean(xv * xv, axis=-1, keepdims=True) + EPS)
        h = (xv * r) * g_ref[...] * (1.0 + sc_ref[...]) + sh_ref[...]
        e = h - t_ref[...]
        dh_ref[...] = e * (1.0 / D)
        sq_ref[...] += jnp.sum(e * e, axis=0, keepdims=True)

    row = pl.BlockSpec((tm, D), lambda i: (i, 0))
    vec = pl.BlockSpec((1, D), lambda i: (0, 0))
    return pl.pallas_call(
        body, grid=(S // tm,), in_specs=[row, vec, vec, vec, row], out_specs=[row, vec],
        out_shape=[jax.ShapeDtypeStruct((S, D), F32), jax.ShapeDtypeStruct((1, D), F32)],
        compiler_params=_cp(("arbitrary",)), name="final_loss",
    )(x, g, shift, scale, tgt)


def _normmod_bwd_rows(xv, dh, dres, gv, sc, prev):
    r = lax.rsqrt(jnp.mean(xv * xv, axis=-1, keepdims=True) + EPS)
    xh = xv * r
    dsh = jnp.sum(dh, axis=0, keepdims=True)
    dsc = jnp.sum(dh * (xh * gv), axis=0, keepdims=True)
    dn = dh * (1.0 + sc)
    dg = jnp.sum(dn * xh, axis=0, keepdims=True)
    dxh = dn * gv
    dx = r * (dxh - xh * jnp.mean(dxh * xh, axis=-1, keepdims=True))
    if dres is not None:
        dx = dx + dres
    if prev is None:
        return dx, dsh, dsc, dg
    y, gvp = prev
    return dx, dsh, dsc, dg, (gvp * dx).astype(BF16), jnp.sum(dx * y, axis=0, keepdims=True)


def _normmod_bwd(x, dh, dres, g, scale, prev, name):
    S, D = x.shape
    tm = _div(S, 512, 8)
    with_res = dres is not None
    with_prev = prev is not None

    def body(*refs):
        refs = list(refs)
        x_ref, dh_ref = refs[:2]
        del refs[:2]
        dr_ref = refs.pop(0) if with_res else None
        g_ref, sc_ref = refs[:2]
        del refs[:2]
        pv = None
        if with_prev:
            pv = (refs[0][...], refs[1][...])
            del refs[:2]

        @pl.when(pl.program_id(0) == 0)
        def _():
            for o in refs[1:4] + refs[5:]:
                o[...] = jnp.zeros_like(o)

        res = _normmod_bwd_rows(x_ref[...], dh_ref[...].astype(F32), dr_ref[...] if with_res else None, g_ref[...], sc_ref[...], pv)
        refs[0][...] = res[0]
        for o, v in zip(refs[1:4], res[1:4]):
            o[...] += v
        if with_prev:
            refs[4][...] = res[4]
            refs[5][...] += res[5]

    row = pl.BlockSpec((tm, D), lambda i: (i, 0))
    vec = pl.BlockSpec((1, D), lambda i: (0, 0))
    ins = [row, row] + [row] * with_res + [vec, vec] + [row, vec] * with_prev
    args = (x, dh) + ((dres,) if with_res else ()) + (g, scale) + (tuple(prev) if with_prev else ())
    vs = jax.ShapeDtypeStruct((1, D), F32)
    return pl.pallas_call(
        body, grid=(S // tm,), in_specs=ins, out_specs=[row, vec, vec, vec] + [row, vec] * with_prev,
        out_shape=[jax.ShapeDtypeStruct((S, D), F32), vs, vs, vs] + [jax.ShapeDtypeStruct((S, D), BF16), vs] * with_prev,
        compiler_params=_cp(("arbitrary",)), name=name,
    )(*args)


def _mm(pairs, out_dtype, name, nt=False):
    M = pairs[0][0].shape[0]
    N = pairs[0][1].shape[0] if nt else pairs[0][1].shape[1]
    ktot = sum(a.shape[1] for a, _, _ in pairs)
    tm = _div(M, 512 if ktot <= 4096 else 256, 8)
    n = len(pairs)

    def body(*refs):
        o_ref = refs[2 * n]
        dot = _dot_nt if nt else _dot
        acc = dot(refs[0][...], refs[1][...])
        for p in range(1, n):
            acc = acc + dot(refs[2 * p][...], refs[2 * p + 1][...])
        o_ref[...] = acc.astype(o_ref.dtype)

    ins, args = [], []
    for a, b, blk in pairs:
        k = a.shape[1]
        ins.append(pl.BlockSpec((tm, k), lambda i: (i, 0)))
        ins.append(_resident((N, k), lambda i, blk=blk: (0, blk)) if nt else _resident((k, N), lambda i: (0, 0)))
        args += [a, b]
    return pl.pallas_call(
        body, grid=(M // tm,), in_specs=ins, out_specs=pl.BlockSpec((tm, N), lambda i: (i, 0)),
        out_shape=jax.ShapeDtypeStruct((M, N), out_dtype), compiler_params=_cp(("parallel",)), name=name,
    )(*args)


def _mm_tn(a, g, name):
    S, Ka = a.shape
    N = g.shape[1]
    tk = _div(Ka, 1408, LANE)
    tn = _div(N, 1408, LANE)
    ts = _div(S, 512, 8)

    def body(a_ref, g_ref, o_ref):
        @pl.when(pl.program_id(2) == 0)
        def _():
            o_ref[...] = jnp.zeros_like(o_ref)

        o_ref[...] += _dot_tn(a_ref[...], g_ref[...])

    return pl.pallas_call(
        body, grid=(Ka // tk, N // tn, S // ts),
        in_specs=[pl.BlockSpec((ts, tk), lambda i, j, s: (s, i)), pl.BlockSpec((ts, tn), lambda i, j, s: (s, j))],
        out_specs=pl.BlockSpec((tk, tn), lambda i, j, s: (i, j)),
        out_shape=jax.ShapeDtypeStruct((Ka, N), F32),
        compiler_params=_cp(("parallel", "parallel", "arbitrary")), name=name,
    )(a, g)


def _mm_tn_two(a0, a1, g, name):
    S, K = a0.shape
    N = g.shape[1]
    ts = _div(S, 512, 8)

    def body(a0_ref, a1_ref, g_ref, o_ref):
        i = pl.program_id(0)

        @pl.when(pl.program_id(1) == 0)
        def _():
            o_ref[...] = jnp.zeros_like(o_ref)

        @pl.when(i == 0)
        def _():
            o_ref[...] += _dot_tn(a0_ref[...], g_ref[...])

        @pl.when(i == 1)
        def _():
            o_ref[...] += _dot_tn(a1_ref[...], g_ref[...])

    return pl.pallas_call(
        body, grid=(2, S // ts),
        in_specs=[pl.BlockSpec((ts, K), lambda i, s: (jnp.where(i == 0, s, 0), 0)),
                  pl.BlockSpec((ts, K), lambda i, s: (jnp.where(i == 1, s, 0), 0)),
                  pl.BlockSpec((ts, N), lambda i, s: (s, 0))],
        out_specs=pl.BlockSpec((K, N), lambda i, s: (i, 0)),
        out_shape=jax.ShapeDtypeStruct((2 * K, N), F32),
        compiler_params=_cp(("parallel", "arbitrary")), name=name,
    )(a0, a1, g)


def _swiglu(gt, up):
    return gt * _sigmoid(gt) * up


def _ffn_up(h, w4, name):
    S, D = h.shape
    ns, _, C = w4.shape
    hs = ns // 2
    tm = _div(S, 256, 8)

    def body(h_ref, w_ref, z_ref, a_ref):
        hv = h_ref[...]
        for s in range(hs):
            gt = _dot(hv, w_ref[s])
            up = _dot(hv, w_ref[hs + s])
            z_ref[:, s * C:(s + 1) * C] = gt.astype(BF16)
            z_ref[:, (hs + s) * C:(hs + s + 1) * C] = up.astype(BF16)
            a_ref[:, s * C:(s + 1) * C] = _swiglu(gt, up).astype(BF16)

    return pl.pallas_call(
        body, grid=(S // tm,), in_specs=[pl.BlockSpec((tm, D), lambda i: (i, 0)), _resident((ns, D, C), lambda i: (0, 0, 0))],
        out_specs=[pl.BlockSpec((tm, ns * C), lambda i: (i, 0)), pl.BlockSpec((tm, hs * C), lambda i: (i, 0))],
        out_shape=[jax.ShapeDtypeStruct((S, ns * C), BF16), jax.ShapeDtypeStruct((S, hs * C), BF16)],
        compiler_params=_cp(("parallel",)), name=name,
    )(h, w4)


def _norm_rows(xv, g, shift, scale):
    r = lax.rsqrt(jnp.mean(xv * xv, axis=-1, keepdims=True) + EPS)
    return (xv * r) * g * (1.0 + scale) + shift


def _resid_outputs(y, x_ref, gv_ref, nxt_refs, out_refs):
    out_refs[0][...] = y
    xn = x_ref[...] + gv_ref[...] * y
    out_refs[1][...] = xn
    if nxt_refs:
        out_refs[2][...] = _norm_rows(xn, nxt_refs[0][...], nxt_refs[1][...], nxt_refs[2][...]).astype(BF16)


def _ffn_down(act, wo, x, gv, nxt, name):
    S = act.shape[0]
    Fd, D = wo.shape
    tm = _div(S, 512, 8)
    nn = 3 if nxt else 0

    def body(a_ref, w_ref, x_ref, gv_ref, *rest):
        _resid_outputs(_dot(a_ref[...], w_ref[...]), x_ref, gv_ref, rest[:nn], rest[nn:])

    row = pl.BlockSpec((tm, D), lambda i: (i, 0))
    vec = pl.BlockSpec((1, D), lambda i: (0, 0))
    os_ = jax.ShapeDtypeStruct((S, D), F32)
    return pl.pallas_call(
        body, grid=(S // tm,),
        in_specs=[pl.BlockSpec((tm, Fd), lambda i: (i, 0)), _resident((Fd, D), lambda i: (0, 0)), row, vec] + [vec] * nn,
        out_specs=[row, row] + [row] * (nn // 3), out_shape=[os_, os_] + [jax.ShapeDtypeStruct((S, D), BF16)] * (nn // 3),
        compiler_params=_cp(("parallel",)), name=name,
    )(act, wo, x, gv, *(nxt or ()))


def _ffn_bwd_act(dy, wo, z, name):
    S, D = dy.shape
    Fd = wo.shape[0]
    tm = _div(S, 256, 8)

    def body(dy_ref, w_ref, g_ref, u_ref, dz_ref):
        da = _dot_nt(dy_ref[...], w_ref[...])
        gt = g_ref[...].astype(F32)
        sg = _sigmoid(gt)
        dz_ref[:, Fd:] = (da * gt * sg).astype(BF16)
        dz_ref[:, :Fd] = (da * u_ref[...].astype(F32) * (sg * (1.0 + gt * (1.0 - sg)))).astype(BF16)

    return pl.pallas_call(
        body, grid=(S // tm,),
        in_specs=[pl.BlockSpec((tm, D), lambda i: (i, 0)), _resident((Fd, D), lambda i: (0, 0)),
                  pl.BlockSpec((tm, Fd), lambda i: (i, 0)), pl.BlockSpec((tm, Fd), lambda i: (i, 1))],
        out_specs=pl.BlockSpec((tm, 2 * Fd), lambda i: (i, 0)), out_shape=jax.ShapeDtypeStruct((S, 2 * Fd), BF16),
        compiler_params=_cp(("parallel",)), name=name,
    )(dy, wo, z, z)


def _dw_ffn_in(h, dz, ns, name):
    S, D = h.shape
    C = dz.shape[1] // ns
    ts = _div(S, 512, 8)

    def body(h_ref, g_ref, o_ref):
        @pl.when(pl.program_id(1) == 0)
        def _():
            o_ref[...] = jnp.zeros_like(o_ref)

        o_ref[...] += _dot_tn(h_ref[...], g_ref[...])

    return pl.pallas_call(
        body, grid=(ns, S // ts),
        in_specs=[pl.BlockSpec((ts, D), lambda j, s: (s, 0)), pl.BlockSpec((ts, C), lambda j, s: (s, j))],
        out_specs=pl.BlockSpec((None, D, C), lambda j, s: (j, 0, 0)), out_shape=jax.ShapeDtypeStruct((ns, D, C), F32),
        compiler_params=_cp(("parallel", "arbitrary")), name=name,
    )(h, dz)


def _dh_normmod_bwd(pairs, x, dres, g, scale, prev, name):
    S, D = x.shape
    tm = _div(S, 256, 8)
    n = len(pairs)
    with_prev = prev is not None

    def body(*refs):
        refs = list(refs)
        mm = refs[:2 * n]
        x_ref, dr_ref, g_ref, sc_ref = refs[2 * n:2 * n + 4]
        outs = refs[2 * n + 4 + 2 * with_prev:]

        @pl.when(pl.program_id(0) == 0)
        def _():
            for o in outs[1:4] + outs[5:]:
                o[...] = jnp.zeros_like(o)

        dh = _dot_nt(mm[0][...], mm[1][...])
        for p in range(1, n):
            dh = dh + _dot_nt(mm[2 * p][...], mm[2 * p + 1][...])
        pv = (refs[2 * n + 4][...], refs[2 * n + 5][...]) if with_prev else None
        res = _normmod_bwd_rows(x_ref[...], dh, dr_ref[...], g_ref[...], sc_ref[...], pv)
        outs[0][...] = res[0]
        for o, v in zip(outs[1:4], res[1:4]):
            o[...] += v
        if with_prev:
            outs[4][...] = res[4]
            outs[5][...] += res[5]

    row = pl.BlockSpec((tm, D), lambda i: (i, 0))
    vec = pl.BlockSpec((1, D), lambda i: (0, 0))
    ins, args = [], []
    for a, b, a_blk, b_blk, k in pairs:
        ins.append(pl.BlockSpec((tm, k), lambda i, a_blk=a_blk: (i, a_blk)))
        ins.append(_resident((None, D, k), lambda i, b_blk=b_blk: (b_blk, 0, 0)) if b.ndim == 3 else _resident((D, k), lambda i, b_blk=b_blk: (0, b_blk)))
        args += [a, b]
    ins += [row, row, vec, vec] + [row, vec] * with_prev
    args += [x, dres, g, scale] + (list(prev) if with_prev else [])
    vs = jax.ShapeDtypeStruct((1, D), F32)
    return pl.pallas_call(
        body, grid=(S // tm,), in_specs=ins, out_specs=[row, vec, vec, vec] + [row, vec] * with_prev,
        out_shape=[jax.ShapeDtypeStruct((S, D), F32), vs, vs, vs] + [jax.ShapeDtypeStruct((S, D), BF16), vs] * with_prev,
        compiler_params=_cp(("arbitrary",)), name=name,
    )(*args)


def _mix_out(yconv, ygla, wout, x, gv, nxt, name):
    S, Kc = yconv.shape
    Kg = ygla.shape[1]
    D = wout.shape[1]
    tm = _div(S, 512, 8)
    nn = 3 if nxt else 0

    def body(a_ref, b_ref, w_ref, x_ref, gv_ref, *rest):
        y = _dot(a_ref[...], w_ref[0:Kc, :]) + _dot(b_ref[...], w_ref[Kc:Kc + Kg, :])
        _resid_outputs(y, x_ref, gv_ref, rest[:nn], rest[nn:])

    row = pl.BlockSpec((tm, D), lambda i: (i, 0))
    vec = pl.BlockSpec((1, D), lambda i: (0, 0))
    os_ = jax.ShapeDtypeStruct((S, D), F32)
    return pl.pallas_call(
        body, grid=(S // tm,),
        in_specs=[pl.BlockSpec((tm, Kc), lambda i: (i, 0)), pl.BlockSpec((tm, Kg), lambda i: (i, 0)), _resident((Kc + Kg, D), lambda i: (0, 0)), row,
                  vec] + [vec] * nn,
        out_specs=[row, row] + [row] * (nn // 3), out_shape=[os_, os_] + [jax.ShapeDtypeStruct((S, D), BF16)] * (nn // 3),
        compiler_params=_cp(("parallel",)), name=name,
    )(yconv, ygla, wout, x, gv, *(nxt or ()))


def _ln_parts(yc, g, b):
    mu = jnp.mean(yc, axis=-1, keepdims=True)
    xc = yc - mu
    rs = lax.rsqrt(jnp.mean(xc * xc, axis=-1, keepdims=True) + EPS)
    xh = xc * rs
    return xh, rs, xh * g + b


SUB = 8
CONV_ROWS = 32


def _shifted_copies(ext8, rows):
    for b in range(1, SUB):
        ext8[b, pl.ds(0, rows - SUB), :] = ext8[0, pl.ds(b, rows - SUB), :]


def _tap(o):
    return o % SUB, o - o % SUB


def _conv_fwd(z, w_dw, b_dw, g_ln, b_ln, name):
    S = z.shape[0]
    W, C = w_dw.shape
    ts = _div(S, 512, HALO)
    hb = ts // HALO
    off = HALO - (W - 1)
    ca, cb = ZC_A // C, ZC_B // C
    rb = CONV_ROWS

    def body(a_ref, b_ref, pa_ref, pb_ref, w_ref, bd_ref, g_ref, bl_ref, u_ref, yc_ref, o_ref, ext8):
        keep = (pl.program_id(0) > 0).astype(F32)
        u = a_ref[...] * _sigmoid(b_ref[...])
        ext8[0, pl.ds(0, HALO), :] = pa_ref[...] * _sigmoid(pb_ref[...]) * keep
        ext8[0, pl.ds(HALO, ts), :] = u
        u_ref[...] = u
        _shifted_copies(ext8, ts + HALO)

        def sub(i, carry):
            r0 = pl.multiple_of(i * rb, rb)
            acc = jnp.zeros((rb, C), F32)
            for j in range(W):
                b, a = _tap(off + j)
                acc = acc + w_ref[pl.ds(j, 1), :] * ext8[b, pl.ds(r0 + a, rb), :]
            yc = acc + bd_ref[...]
            yc_ref[pl.ds(r0, rb), :] = yc
            _, _, ln = _ln_parts(yc, g_ref[...], bl_ref[...])
            o_ref[pl.ds(r0, rb), :] = (ln * _sigmoid(ln)).astype(BF16)
            return carry

        lax.fori_loop(0, ts // rb, sub, 0)

    cur = lambda col: pl.BlockSpec((ts, C), lambda i: (i, col))
    prev = lambda col: pl.BlockSpec((HALO, C), lambda i: (jnp.maximum(i * hb - 1, 0), col))
    vec = pl.BlockSpec((1, C), lambda i: (0, 0))
    row = pl.BlockSpec((ts, C), lambda i: (i, 0))
    fs = jax.ShapeDtypeStruct((S, C), F32)
    return pl.pallas_call(
        body, grid=(S // ts,),
        in_specs=[cur(ca), cur(cb), prev(ca), prev(cb), pl.BlockSpec((W, C), lambda i: (0, 0)), vec, vec, vec],
        out_specs=[row, row, row], out_shape=[fs, fs, jax.ShapeDtypeStruct((S, C), BF16)],
        scratch_shapes=[pltpu.VMEM((SUB, ts + HALO, C), F32)],
        compiler_params=_cp(("parallel",)), name=name,
    )(z, z, z, z, w_dw, b_dw, g_ln, b_ln)


def _conv_bwd(dycat, z, u, yc, w_dw, g_ln, b_ln, name):
    S = z.shape[0]
    W, C = w_dw.shape
    ts = _div(S, 512, HALO)
    hb = ts // HALO
    nblk = S // ts
    off = HALO - (W - 1)
    ca, cb = ZC_A // C, ZC_B // C
    rb = CONV_ROWS

    def ln_silu_bwd(dy, ycv, g, b):
        xh, rs, ln = _ln_parts(ycv, g, b)
        sl = _sigmoid(ln)
        dln = dy * (sl * (1.0 + ln * (1.0 - sl)))
        dxh = dln * g
        dyc = rs * (dxh - jnp.mean(dxh, axis=-1, keepdims=True) - xh * jnp.mean(dxh * xh, axis=-1, keepdims=True))
        return dyc, dln, xh

    def body(dy_ref, ndy_ref, yc_ref, nyc_ref, u_ref, pu_ref, a_ref, b_ref, w_ref, g_ref, bl_ref,
             dab_ref, dw_ref, dbd_ref, dg_ref, dbl_ref, uext8, dext8, dwacc):
        i = pl.program_id(0)

        @pl.when(i == 0)
        def _():
            dwacc[...] = jnp.zeros_like(dwacc)
            dbd_ref[...] = jnp.zeros_like(dbd_ref)
            dg_ref[...] = jnp.zeros_like(dg_ref)
            dbl_ref[...] = jnp.zeros_like(dbl_ref)

        g = g_ref[...]
        bl = bl_ref[...]
        dyc, dln, xh = ln_silu_bwd(dy_ref[...], yc_ref[...], g, bl)
        ndyc, _, _ = ln_silu_bwd(ndy_ref[...], nyc_ref[...], g, bl)
        dg_ref[...] += jnp.sum(dln * xh, axis=0, keepdims=True)
        dbl_ref[...] += jnp.sum(dln, axis=0, keepdims=True)
        dbd_ref[...] += jnp.sum(dyc, axis=0, keepdims=True)
        dext8[0, pl.ds(0, ts), :] = dyc
        dext8[0, pl.ds(ts, HALO), :] = ndyc * (i < nblk - 1).astype(F32)
        uext8[0, pl.ds(0, HALO), :] = pu_ref[...] * (i > 0).astype(F32)
        uext8[0, pl.ds(HALO, ts), :] = u_ref[...]
        _shifted_copies(dext8, ts + HALO)
        _shifted_copies(uext8, ts + HALO)

        def sub(k, carry):
            r0 = pl.multiple_of(k * rb, rb)
            rows = pl.ds(r0, rb)
            dyt = dext8[0, rows, :]
            du = jnp.zeros((rb, C), F32)
            for j in range(W):
                b, a = _tap(W - 1 - j)
                du = du + w_ref[pl.ds(j, 1), :] * dext8[b, pl.ds(r0 + a, rb), :]
                b, a = _tap(off + j)
                p = dyt * uext8[b, pl.ds(r0 + a, rb), :]
                part = p[0:SUB]
                for q in range(1, rb // SUB):
                    part = part + p[q * SUB:(q + 1) * SUB]
                dwacc[j] += part
            sb = _sigmoid(b_ref[rows, :])
            dab_ref[rows, 0:C] = (du * sb).astype(BF16)
            dab_ref[rows, C:2 * C] = (du * a_ref[rows, :] * sb * (1.0 - sb)).astype(BF16)
            return carry

        lax.fori_loop(0, ts // rb, sub, 0)

        @pl.when(i == nblk - 1)
        def _():
            for j in range(W):
                dw_ref[pl.ds(j, 1), :] = jnp.sum(dwacc[j], axis=0, keepdims=True)

    row = pl.BlockSpec((ts, C), lambda i: (i, 0))
    nxt = pl.BlockSpec((HALO, C), lambda i: (jnp.minimum((i + 1) * hb, S // HALO - 1), 0))
    prv = pl.BlockSpec((HALO, C), lambda i: (jnp.maximum(i * hb - 1, 0), 0))
    vec = pl.BlockSpec((1, C), lambda i: (0, 0))
    wsp = pl.BlockSpec((W, C), lambda i: (0, 0))
    vs = jax.ShapeDtypeStruct((1, C), F32)
    return pl.pallas_call(
        body, grid=(nblk,),
        in_specs=[row, nxt, row, nxt, row, prv, pl.BlockSpec((ts, C), lambda i: (i, ca)), pl.BlockSpec((ts, C), lambda i: (i, cb)), wsp, vec, vec],
        out_specs=[pl.BlockSpec((ts, 2 * C), lambda i: (i, 0)), wsp, vec, vec, vec],
        out_shape=[jax.ShapeDtypeStruct((S, 2 * C), BF16), jax.ShapeDtypeStruct((W, C), F32), vs, vs, vs],
        scratch_shapes=[pltpu.VMEM((SUB, ts + HALO, C), F32), pltpu.VMEM((SUB, ts + HALO, C), F32), pltpu.VMEM((W, SUB, C), F32)],
        compiler_params=_cp(("arbitrary",)), name=name,
    )(dycat, dycat, yc, yc, u, u, z, z, w_dw, g_ln, b_ln)


def _log_gate(zg):
    return (jnp.minimum(zg, 0.0) - jnp.log(1.0 + jnp.exp(-jnp.abs(zg)))) * (1.0 / GATE_TAU)


def _loggate(z, wgp, bgp, name):
    S = z.shape[0]
    N = wgp.shape[1]
    ts = _div(S, 512, 8)

    def body(g_ref, w_ref, b_ref, o_ref):
        o_ref[...] = _log_gate(_dot(g_ref[...], w_ref[...]) + b_ref[...])

    return pl.pallas_call(
        body, grid=(S // ts,),
        in_specs=[pl.BlockSpec((ts, LANE), lambda i: (i, ZC_G // LANE)), pl.BlockSpec((LANE, N), lambda i: (0, 0)), pl.BlockSpec((1, N), lambda i: (0, 0))],
        out_specs=pl.BlockSpec((ts, N), lambda i: (i, 0)), out_shape=jax.ShapeDtypeStruct((S, N), F32),
        compiler_params=_cp(("parallel",)), name=name,
    )(z, wgp, bgp)


def _loggate_bwd(dla, z, wgp, wgp_t, bgp, name):
    S = z.shape[0]
    N = wgp.shape[1]
    ts = _div(S, 512, 8)

    def body(dla_ref, g_ref, w_ref, wt_ref, b_ref, dg_ref, dw_ref, db_ref):
        @pl.when(pl.program_id(0) == 0)
        def _():
            dw_ref[...] = jnp.zeros_like(dw_ref)
            db_ref[...] = jnp.zeros_like(db_ref)

        glr = g_ref[...]
        zg = _dot(glr, w_ref[...]) + b_ref[...]
        dzg = dla_ref[...] * (1.0 / GATE_TAU) * (1.0 - _sigmoid(zg))
        dg_ref[...] = _dot(dzg, wt_ref[...]).astype(BF16)
        dw_ref[...] += _dot_tn(glr, dzg)
        db_ref[...] += jnp.sum(dzg, axis=0, keepdims=True)

    return pl.pallas_call(
        body, grid=(S // ts,),
        in_specs=[pl.BlockSpec((ts, N), lambda i: (i, 0)), pl.BlockSpec((ts, LANE), lambda i: (i, ZC_G // LANE)),
                  pl.BlockSpec((LANE, N), lambda i: (0, 0)), pl.BlockSpec((N, LANE), lambda i: (0, 0)), pl.BlockSpec((1, N), lambda i: (0, 0))],
        out_specs=[pl.BlockSpec((ts, LANE), lambda i: (i, 0)), pl.BlockSpec((LANE, N), lambda i: (0, 0)), pl.BlockSpec((1, N), lambda i: (0, 0))],
        out_shape=[jax.ShapeDtypeStruct((S, LANE), BF16), jax.ShapeDtypeStruct((LANE, N), F32), jax.ShapeDtypeStruct((1, N), F32)],
        compiler_params=_cp(("arbitrary",)), name=name,
    )(dla, z, wgp, wgp_t, bgp)


def _bdot(a, b, ca, cb):
    return lax.dot_general(a.astype(BF16), b.astype(BF16), (((ca,), (cb,)), ((0,), (0,))), preferred_element_type=F32)


def _bdot_exact(a, b):
    return lax.dot_general(a, b, (((2,), (1,)), ((0,), (0,))), preferred_element_type=F32, precision=lax.Precision.HIGHEST)


def _tiles(ref, cpb):
    return jnp.stack([ref[pl.ds(c * CHUNK, CHUNK), pl.ds(h * LANE, LANE)] for c in range(cpb) for h in range(HEADS)])


def _tri_masks(n):
    ri = lax.broadcasted_iota(jnp.int32, (n, CHUNK, CHUNK), 1)
    ci = lax.broadcasted_iota(jnp.int32, (n, CHUNK, CHUNK), 2)
    return ri >= ci, (ri >= ci).astype(F32), (ri <= ci).astype(F32)


def _chunk_fwd_terms(q, k, la, tril):
    bc = _bdot_exact(tril, la)
    bend = jnp.sum(la, axis=1, keepdims=True)
    eb = jnp.exp(bc)
    enb = jnp.exp(-bc)
    ee = jnp.exp(bend - bc)
    qs = q * (DK ** -0.5)
    return bend, eb, enb, ee, qs * eb, qs * enb, k * enb, k * eb, k * ee


def _gla_fwd(z, la, gn, name):
    S = z.shape[0]
    W = HEADS * LANE
    tb = _div(S, 512, CHUNK)
    cpb = tb // CHUNK

    def body(q_ref, k_ref, v_ref, r_ref, la_ref, gn_ref, o_ref, sp_ref, y_ref, st):
        @pl.when(pl.program_id(0) == 0)
        def _():
            st[...] = jnp.zeros_like(st)

        tri, tril, _ = _tri_masks(cpb * HEADS)
        q, k, v, rv, lav = (_tiles(r, cpb) for r in (q_ref, k_ref, v_ref, r_ref, la_ref))
        bend, _, _, _, qf, qb, kb, kf, ke = _chunk_fwd_terms(q, k, lav, tril)
        att = jnp.where(tri, _bdot(qf, kb, 2, 2), _bdot(qb, kf, 2, 2))
        o_intra = _bdot(att, v, 2, 1)
        u = _bdot(v, ke, 1, 1)
        gdec = jnp.exp(bend)
        s_prev = [None] * (cpb * HEADS)
        for h in range(HEADS):
            s = st[h]
            for c in range(cpb):
                b = c * HEADS + h
                s_prev[b] = s
                s = s * gdec[b] + u[b]
            st[h] = s
        s_prev = jnp.stack(s_prev)
        o = o_intra + _bdot(qf, s_prev, 2, 2)
        rms = lax.rsqrt(jnp.mean(o * o, axis=-1, keepdims=True) + EPS)
        gn = jnp.stack([gn_ref[pl.ds(h, 1), :] for _ in range(cpb) for h in range(HEADS)])
        y = (o * rms * gn * (rv * _sigmoid(rv))).astype(BF16)
        for c in range(cpb):
            for h in range(HEADS):
                b = c * HEADS + h
                rows, ln = pl.ds(c * CHUNK, CHUNK), pl.ds(h * LANE, LANE)
                o_ref[rows, ln] = o[b]
                y_ref[rows, ln] = y[b]
                sp_ref[h, c] = s_prev[b]

    zb = lambda base: pl.BlockSpec((tb, W), lambda i: (i, base // W))
    hb_ = pl.BlockSpec((tb, W), lambda i: (i, 0))
    return pl.pallas_call(
        body, grid=(S // tb,),
        in_specs=[zb(ZC_Q), zb(ZC_K), zb(ZC_V), zb(ZC_R), hb_, pl.BlockSpec((HEADS, DV), lambda i: (0, 0))],
        out_specs=[hb_, pl.BlockSpec((HEADS, cpb, DV, DKP), lambda i: (0, i, 0, 0)), hb_],
        out_shape=[jax.ShapeDtypeStruct((S, D_GLA), F32), jax.ShapeDtypeStruct((HEADS, S // CHUNK, DV, DKP), F32),
                   jax.ShapeDtypeStruct((S, D_GLA), BF16)],
        scratch_shapes=[pltpu.VMEM((HEADS, DV, DKP), F32)],
        compiler_params=_cp(("arbitrary",)), name=name,
    )(z, z, z, z, la, gn)


def _gla_bwd(dycat, z, la, o_raw, sprev, gn, name):
    S = z.shape[0]
    W = HEADS * LANE
    tb = _div(S, 512, CHUNK)
    cpb = tb // CHUNK
    nb = S // tb

    def body(q_ref, k_ref, v_ref, r_ref, la_ref, o_ref, sp_ref, dy_ref, gn_ref, dz_ref, dla_ref, dgn_ref, dst):
        @pl.when(pl.program_id(0) == 0)
        def _():
            dst[...] = jnp.zeros_like(dst)
            dgn_ref[...] = jnp.zeros_like(dgn_ref)

        nt = cpb * HEADS
        tri, tril, triu = _tri_masks(nt)
        q, k, v, rv, lav, o, dy = (_tiles(r, cpb) for r in (q_ref, k_ref, v_ref, r_ref, la_ref, o_ref, dy_ref))
        bend, eb, enb, ee, qf, qb, kb, kf, ke = _chunk_fwd_terms(q, k, lav, tril)
        att = jnp.where(tri, _bdot(qf, kb, 2, 2), _bdot(qb, kf, 2, 2))
        s_prev = jnp.stack([sp_ref[h, c] for c in range(cpb) for h in range(HEADS)])
        gdec = jnp.exp(bend)
        gn = jnp.stack([gn_ref[pl.ds(h, 1), :] for _ in range(cpb) for h in range(HEADS)])
        rms = lax.rsqrt(jnp.mean(o * o, axis=-1, keepdims=True) + EPS)
        oh = o * rms
        sg = _sigmoid(rv)
        sr = rv * sg
        d_r = (dy * oh * gn * (sg * (1.0 + rv * (1.0 - sg)))).astype(BF16)
        dgn = jnp.sum(dy * sr * oh, axis=1, keepdims=True)
        w = dy * sr * gn
        do = rms * (w - oh * jnp.mean(w * oh, axis=-1, keepdims=True))
        p = _bdot(do, qf, 1, 1)
        ds = [None] * nt
        for h in range(HEADS):
            s = dst[h]
            for c in reversed(range(cpb)):
                b = c * HEADS + h
                ds[b] = s
                s = s * gdec[b] + p[b]
            dst[h] = s
            dgn_ref[pl.ds(h, 1), :] += sum(dgn[c * HEADS + h] for c in range(cpb))
        ds = jnp.stack(ds)
        datt = _bdot(do, v, 2, 2)
        daf = jnp.where(tri, datt, 0.0)
        dab = jnp.where(tri, 0.0, datt)
        d_v = (_bdot(att, do, 1, 1) + _bdot(ke, ds, 2, 2)).astype(BF16)
        dke = _bdot(v, ds, 2, 1)
        dqf = _bdot(daf, kb, 2, 1) + _bdot(do, s_prev, 2, 1)
        dkb = _bdot(daf, qf, 1, 1)
        dqb = _bdot(dab, kf, 2, 1)
        dkf = _bdot(dab, qb, 1, 1)
        dg = jnp.sum(ds * s_prev, axis=1, keepdims=True)
        d_q = ((dqf * eb + dqb * enb) * (DK ** -0.5)).astype(BF16)
        d_k = (dkb * enb + dkf * eb + dke * ee).astype(BF16)
        dbc = dqf * qf - dkb * kb - dqb * qb + dkf * kf - dke * ke
        dbend = jnp.sum(dke * ke, axis=1, keepdims=True) + dg * gdec
        dla = _bdot_exact(triu, dbc) + dbend
        for c in range(cpb):
            for h in range(HEADS):
                b = c * HEADS + h
                rows = pl.ds(c * CHUNK, CHUNK)
                for base, val in ((ZC_Q, d_q), (ZC_K, d_k), (ZC_V, d_v), (ZC_R, d_r)):
                    dz_ref[rows, pl.ds(base + h * LANE, LANE)] = val[b]
                dla_ref[rows, pl.ds(h * LANE, LANE)] = dla[b]

    zb = lambda base: pl.BlockSpec((tb, W), lambda i: (nb - 1 - i, base // W))
    hb_ = pl.BlockSpec((tb, W), lambda i: (nb - 1 - i, 0))
    return pl.pallas_call(
        body, grid=(nb,),
        in_specs=[zb(ZC_Q), zb(ZC_K), zb(ZC_V), zb(ZC_R), hb_, hb_,
                  pl.BlockSpec((HEADS, cpb, DV, DKP), lambda i: (0, nb - 1 - i, 0, 0)),
                  pl.BlockSpec((tb, W), lambda i: (nb - 1 - i, 1)),
                  pl.BlockSpec((HEADS, DV), lambda i: (0, 0))],
        out_specs=[pl.BlockSpec((tb, Z_GLA), lambda i: (nb - 1 - i, 0)), hb_, pl.BlockSpec((HEADS, DV), lambda i: (0, 0))],
        out_shape=[jax.ShapeDtypeStruct((S, Z_GLA), BF16), jax.ShapeDtypeStruct((S, HEADS * DKP), F32), jax.ShapeDtypeStruct((HEADS, DV), F32)],
        scratch_shapes=[pltpu.VMEM((HEADS, DV, DKP), F32)],
        compiler_params=_cp(("arbitrary",)), name=name,
    )(z, z, z, z, la, o_raw, sprev, dycat, gn)


def _mod_proj(c_all, w3, layer, b, name):
    B, D = c_all.shape
    N = w3.shape[2]
    tn = _div(N, 1024, LANE)

    def body(c_ref, w_ref, b_ref, o_ref):
        cv = c_ref[...]
        o_ref[...] = _dot(cv * _sigmoid(cv), w_ref[...]) + b_ref[...]

    return pl.pallas_call(
        body, grid=(N // tn,),
        in_specs=[pl.BlockSpec((B, D), lambda j: (0, 0)), pl.BlockSpec((None, D, tn), lambda j: (layer, 0, j)), pl.BlockSpec((1, tn), lambda j: (0, j))],
        out_specs=pl.BlockSpec((B, tn), lambda j: (0, j)), out_shape=jax.ShapeDtypeStruct((B, N), F32),
        compiler_params=_cp(("parallel",)), name=name,
    )(c_all, w3, b)


def _mod_wgrad(c_t, dm, name):
    D, B = c_t.shape
    N = dm.shape[1]
    tn = _div(N, 1024, LANE)

    def body(c_ref, d_ref, o_ref):
        cv = c_ref[...]
        ca = cv * _sigmoid(cv)
        acc = ca[:, 0:1] * d_ref[pl.ds(0, 1), :]
        for b in range(1, B):
            acc = acc + ca[:, b:b + 1] * d_ref[pl.ds(b, 1), :]
        o_ref[...] = acc

    return pl.pallas_call(
        body, grid=(N // tn,),
        in_specs=[pl.BlockSpec((D, B), lambda j: (0, 0)), pl.BlockSpec((B, tn), lambda j: (0, j))],
        out_specs=pl.BlockSpec((D, tn), lambda j: (0, j)), out_shape=jax.ShapeDtypeStruct((D, N), F32),
        compiler_params=_cp(("parallel",)), name=name,
    )(c_t, dm)


def _rowsum(xs, name):
    n, N = xs.shape
    tn = _div(N, 8192, LANE)

    def body(x_ref, o_ref):
        acc = x_ref[pl.ds(0, 1), :]
        for r in range(1, n):
            acc = acc + x_ref[pl.ds(r, 1), :]
        o_ref[...] = acc

    return pl.pallas_call(
        body, grid=(N // tn,), in_specs=[pl.BlockSpec((n, tn), lambda j: (0, j))],
        out_specs=pl.BlockSpec((1, tn), lambda j: (0, j)), out_shape=jax.ShapeDtypeStruct((1, N), F32),
        compiler_params=_cp(("parallel",)), name=name,
    )(xs)


def _adamw(w, g, m, v, name):
    R, C = w.shape
    tr = _div(R, max(8, (1 << 18) // C), 8)

    def body(w_ref, g_ref, m_ref, v_ref, d_ref, nm_ref, nv_ref):
        gv = g_ref[...]
        mn = ADAM_B1 * m_ref[...] + (1.0 - ADAM_B1) * gv
        vn = ADAM_B2 * v_ref[...] + (1.0 - ADAM_B2) * (gv * gv)
        m_hat = mn / (1.0 - ADAM_B1 ** ADAM_STEP)
        v_hat = vn / (1.0 - ADAM_B2 ** ADAM_STEP)
        d_ref[...] = -ADAM_LR * (m_hat / (jnp.sqrt(v_hat) + ADAM_EPS) + ADAM_WD * w_ref[...])
        nm_ref[...] = mn
        nv_ref[...] = vn

    blk = pl.BlockSpec((tr, C), lambda i: (i, 0))
    os_ = jax.ShapeDtypeStruct((R, C), F32)
    return pl.pallas_call(
        body, grid=(R // tr,), in_specs=[blk] * 4, out_specs=[blk] * 3, out_shape=[os_] * 3,
        compiler_params=_cp(("parallel",)), name=name,
    )(w, g, m, v)


def _place():
    return lax.axis_index("x"), lax.axis_index("y"), lax.axis_index("c")


def _other_chips(x, y):
    return [(1 - x, y), (x, 1 - y), (1 - x, 1 - y)]


def _half(c, rows):
    return pl.ds(c * (rows // 2), rows // 2)


_ANY = pl.BlockSpec(memory_space=pl.ANY)


def _ag_small(v, name):
    r, n = v.shape

    def body(v_ref, o_ref, send_sems, recv_sems):
        x, y, c = _place()
        me = 4 * x + 2 * y + c
        o_ref[pl.ds(me, 1)] = v_ref[...][None]
        peers = [(x ^ (k >> 2), y ^ ((k >> 1) & 1), c ^ (k & 1)) for k in range(1, 8)]
        copies = []
        for k, peer in enumerate(peers):
            cp = pltpu.make_async_remote_copy(
                src_ref=v_ref, dst_ref=o_ref.at[me], send_sem=send_sems.at[k], recv_sem=recv_sems.at[k],
                device_id=peer, device_id_type=MESH)
            cp.start()
            copies.append(cp)
        for cp in copies:
            cp.wait()

    return pl.pallas_call(
        body, out_shape=jax.ShapeDtypeStruct((8, r, n), v.dtype),
        in_specs=[pl.BlockSpec(memory_space=pltpu.VMEM)], out_specs=pl.BlockSpec(memory_space=pltpu.VMEM),
        scratch_shapes=[pltpu.SemaphoreType.DMA((7,)), pltpu.SemaphoreType.DMA((7,))],
        compiler_params=pltpu.CompilerParams(vmem_limit_bytes=VMEM_LIMIT), name=name,
    )(v)


def _rs_sibling(gs, name):
    n = len(gs)

    def body(*refs):
        src, out = refs[:n], refs[n:2 * n]
        send_sems, recv_sems = refs[2 * n:]
        x, y, c = _place()
        copies = []
        for i in range(n):
            cp = pltpu.make_async_remote_copy(
                src_ref=src[i].at[:, _half(1 - c, src[i].shape[1])], dst_ref=out[i], send_sem=send_sems.at[i], recv_sem=recv_sems.at[i],
                device_id=(x, y, 1 - c), device_id_type=MESH)
            cp.start()
            copies.append(cp)
        for cp in copies:
            cp.wait()

    return pl.pallas_call(
        body, out_shape=[jax.ShapeDtypeStruct((N_CHIPS, g.shape[1] // 2, g.shape[2]), g.dtype) for g in gs],
        in_specs=[_ANY] * n, out_specs=[_ANY] * n,
        scratch_shapes=[pltpu.SemaphoreType.DMA((n,)), pltpu.SemaphoreType.DMA((n,))],
        compiler_params=pltpu.CompilerParams(has_side_effects=True), name=name,
    )(*gs)


def _rs_presum(g, sib, c_arr, name):
    ns, R, C = g.shape
    rh = R // 2
    tr = _div(rh, max(16, (1 << 19) // C), 16)
    nrb = rh // tr

    def body(c_ref, g_ref, s_ref, o_ref):
        o_ref[...] = (g_ref[...] + s_ref[...]).astype(BF16)

    return pl.pallas_call(
        body, out_shape=jax.ShapeDtypeStruct((ns, rh, C), BF16),
        grid_spec=pltpu.PrefetchScalarGridSpec(
            num_scalar_prefetch=1, grid=(ns, nrb),
            in_specs=[pl.BlockSpec((None, tr, C), lambda s, r, c_ref: (s, c_ref[0] * nrb + r, 0)),
                      pl.BlockSpec((None, tr, C), lambda s, r, c_ref: (s, r, 0))],
            out_specs=pl.BlockSpec((None, tr, C), lambda s, r, c_ref: (s, r, 0))),
        compiler_params=_cp(("parallel", "parallel")), name=name,
    )(c_arr, g, sib)


def _rs_sum(g, sib, recv, full, layer, sc_arr, name):
    ns, R, C = g.shape
    rh = R // 2
    tr = _div(rh, max(16, (1 << 18) // C), 16)
    nrb = rh // tr

    def body(sc_ref, g_ref, s_ref, r_ref, f_ref, o_ref):
        acc = g_ref[...] + s_ref[...]
        for j in range(3):
            acc = acc + r_ref[j].astype(F32)
        o_ref[...] = acc

    return pl.pallas_call(
        body, out_shape=jax.ShapeDtypeStruct(full.shape, F32),
        grid_spec=pltpu.PrefetchScalarGridSpec(
            num_scalar_prefetch=1, grid=(nrb,),
            in_specs=[pl.BlockSpec((None, tr, C), lambda r, sc: (sc[0], sc[1] * nrb + r, 0)),
                      pl.BlockSpec((None, tr, C), lambda r, sc: (sc[0], r, 0)),
                      pl.BlockSpec((3, tr, C), lambda r, sc: (0, r, 0)),
                      _ANY],
            out_specs=pl.BlockSpec((None, tr, C), lambda r, sc: (layer, sc[1] * nrb + r, 0))),
        input_output_aliases={4: 0},
        compiler_params=_cp(("parallel",)), name=name,
    )(sc_arr, g, sib, recv, full)


def _rs_share(fulls, layer, name):
    n = len(fulls)

    def body(*refs):
        src, out = refs[:n], refs[n:2 * n]
        send_sems, recv_sems = refs[2 * n:]
        x, y, c = _place()
        copies = []
        for i in range(n):
            rows = out[i].shape[1]
            cp = pltpu.make_async_remote_copy(
                src_ref=out[i].at[layer, _half(c, rows)], dst_ref=out[i].at[layer, _half(c, rows)],
                send_sem=send_sems.at[i], recv_sem=recv_sems.at[i], device_id=(x, y, 1 - c), device_id_type=MESH)
            cp.start()
            copies.append(cp)
        for cp in copies:
            cp.wait()

    return pl.pallas_call(
        body, out_shape=[jax.ShapeDtypeStruct(f.shape, f.dtype) for f in fulls],
        in_specs=[_ANY] * n, out_specs=[_ANY] * n, input_output_aliases={i: i for i in range(n)},
        scratch_shapes=[pltpu.SemaphoreType.DMA((n,)), pltpu.SemaphoreType.DMA((n,))],
        compiler_params=pltpu.CompilerParams(has_side_effects=True), name=name,
    )(*fulls)


_HBM = pl.BlockSpec(memory_space=pltpu.HBM)
_SEM = pl.BlockSpec(memory_space=pltpu.SEMAPHORE)
_EFFECT = pltpu.SideEffectType.DATAFLOW_SIDE_EFFECTING


def _in_hbm(a):
    return pltpu.with_memory_space_constraint(a, pltpu.HBM)


def _split_start(bufs, n_sem, copies_of, name):
    nb = len(bufs)

    def body(*refs):
        for cp in copies_of(refs[:nb], refs[nb], refs[nb + 1]):
            cp.start()
        refs[-1][...] = jnp.zeros_like(refs[-1])

    out = pl.pallas_call(
        body, name=name,
        out_shape=(pltpu.SemaphoreType.DMA((n_sem,)), pltpu.SemaphoreType.DMA((n_sem,)), *[pltpu.HBM(a.shape, a.dtype) for a in bufs],
                   jax.ShapeDtypeStruct((SUB, LANE), F32)),
        in_specs=[_HBM] * nb, out_specs=(_SEM, _SEM, *([_HBM] * nb), pl.BlockSpec(memory_space=pltpu.VMEM)),
        input_output_aliases={i: 2 + i for i in range(nb)},
        compiler_params=pltpu.CompilerParams(has_side_effects=_EFFECT),
    )(*[_in_hbm(a) for a in bufs])
    return out[0], out[1], list(out[2:2 + nb]), out[-1]


def _split_wait(send_sems, recv_sems, bufs, after, copies_of, name):
    nb = len(bufs)

    def body(*refs):
        for cp in copies_of(refs[:nb], refs[nb], refs[nb + 1]):
            cp.wait_send()
            cp.wait_recv()

    return list(pl.pallas_call(
        body, name=name, out_shape=[pltpu.HBM(a.shape, a.dtype) for a in bufs],
        in_specs=[_HBM] * nb + [_SEM, _SEM, _ANY], out_specs=[_HBM] * nb,
        input_output_aliases={i: i for i in range(nb)},
        compiler_params=pltpu.CompilerParams(has_side_effects=_EFFECT),
    )(*bufs, send_sems, recv_sems, after))


def _ag_half_copies(land, send_sems, recv_sems, landing_of_mine):
    x, y, c = _place()
    cps = []
    for j, (cx, cy) in enumerate(_other_chips(x, y)):
        for i in range(len(land)):
            rows = _half(c, land[i].shape[1])
            s = 2 * x + y if landing_of_mine else 2 * cx + cy
            cps.append(pltpu.make_async_remote_copy(
                src_ref=land[i].at[2 * x + y, rows], dst_ref=land[i].at[s, rows], send_sem=send_sems.at[3 * i + j], recv_sem=recv_sems.at[3 * i + j],
                device_id=(cx, cy, c), device_id_type=MESH))
    return cps


def _ag_starts(land, send_sems, recv_sems):
    return _ag_half_copies(land, send_sems, recv_sems, True)


def _ag_waits(land, send_sems, recv_sems):
    return _ag_half_copies(land, send_sems, recv_sems, False)


def _ag_finish(lands, name):
    n = len(lands)

    def body(*refs):
        land = refs[n:2 * n]
        send_sems, recv_sems = refs[2 * n:]
        x, y, c = _place()
        sibling = (x, y, 1 - c)

        def copy(k, i, s, h):
            blk = land[i].at[s, _half(h, land[i].shape[1])]
            return pltpu.make_async_remote_copy(
                src_ref=blk, dst_ref=blk, send_sem=send_sems.at[k], recv_sem=recv_sems.at[k], device_id=sibling, device_id_type=MESH)

        chips = _other_chips(x, y)
        passed = [copy(3 * i + j, i, 2 * cx + cy, c) for j, (cx, cy) in enumerate(chips) for i in range(n)]
        for cp in passed:
            cp.start()
        for j, (cx, cy) in enumerate(chips):
            for i in range(n):
                copy(3 * i + j, i, 2 * cx + cy, 1 - c).wait_recv()
        for cp in passed:
            cp.wait_send()

    return pl.pallas_call(
        body, out_shape=[jax.ShapeDtypeStruct(a.shape, a.dtype) for a in lands],
        in_specs=[_ANY] * n, out_specs=[_ANY] * n, input_output_aliases={i: i for i in range(n)},
        scratch_shapes=[pltpu.SemaphoreType.DMA((3 * n,)), pltpu.SemaphoreType.DMA((3 * n,))],
        compiler_params=pltpu.CompilerParams(has_side_effects=True), name=name,
    )(*lands)


def _rs_chip_copies(bufs, send_sems, recv_sems):
    n = len(bufs) // 2
    x, y, c = _place()
    return [pltpu.make_async_remote_copy(
        src_ref=bufs[i].at[2 * cx + cy], dst_ref=bufs[n + i].at[j], send_sem=send_sems.at[3 * i + j], recv_sem=recv_sems.at[3 * i + j],
        device_id=(cx, cy, c), device_id_type=MESH) for j, (cx, cy) in enumerate(_other_chips(x, y)) for i in range(n)]


def _pad_heads(w):
    lead = w.shape[:-1]
    w4 = w.reshape(*lead, HEADS, DK)
    w4 = jnp.pad(w4, [(0, 0)] * len(lead) + [(0, 0), (0, DKP - DK)])
    return w4.reshape(*lead, HEADS * DKP)


def _unpad_heads(w):
    lead = w.shape[:-1]
    return w.reshape(*lead, HEADS, DKP)[..., :DK].reshape(*lead, HEADS * DK)


def _mix_weight(win4, n_cols):
    D = win4.shape[1]
    w = jnp.transpose(win4[:, :, :n_cols], (1, 0, 2)).reshape(D, N_CHIPS * n_cols)
    o = 2 * D_CONV
    hk = HEADS * DK
    ab = w[:, :o]
    q = _pad_heads(w[:, o:o + hk])
    k = _pad_heads(w[:, o + hk:o + 2 * hk])
    vr = w[:, o + 2 * hk:o + 2 * hk + 2 * D_GLA]
    glr = jnp.pad(w[:, o + 2 * hk + 2 * D_GLA:], ((0, 0), (0, LANE - GATE_RANK)))
    return jnp.concatenate([q, k, vr, ab, glr], axis=1)


def _mix_weight_grad(dgla, dab, dglr, n_cols, n_pad):
    D = dab.shape[0]
    hkp = HEADS * DKP
    w = jnp.concatenate([dab, _unpad_heads(dgla[:, :hkp]), _unpad_heads(dgla[:, hkp:2 * hkp]), dgla[:, 2 * hkp:], dglr[:, :GATE_RANK]], axis=1)
    w = jnp.pad(w.reshape(D, N_CHIPS, n_cols), ((0, 0), (0, 0), (0, n_pad - n_cols)))
    return jnp.transpose(w, (1, 0, 2))


_ARG_NAMES = ['x', 'c', 'w_ada', 'b_ada', 'g_norm_ffn1', 'w_ffn1_in', 'w_ffn1_out', 'g_norm_mix', 'w_in', 'w_dw', 'b_dw', 'g_conv_ln', 'b_conv_ln', 'w_gate_up', 'b_gate', 'g_gla_norm', 'w_out', 'g_norm_ffn2', 'w_ffn2_in', 'w_ffn2_out', 'g_norm_final', 'w_ada_final', 'b_ada_final']
_WEIGHTS = _ARG_NAMES[2:]
_BIG = ('w_ffn1_in', 'w_ffn1_out', 'w_in', 'w_out', 'w_ffn2_in', 'w_ffn2_out')
_SMALL = ('g_norm_ffn1', 'g_norm_mix', 'w_dw', 'b_dw', 'g_conv_ln', 'b_conv_ln', 'w_gate_up', 'b_gate', 'g_gla_norm', 'g_norm_ffn2', 'g_norm_final')


def _ffn_fwd(x, h, gv, w4, wo, nxt, tag):
    z, act = _ffn_up(h, w4, f"ffn_up_{tag}")
    y, xn, *hn = _ffn_down(act, wo, x, gv, nxt, f"ffn_down_{tag}")
    return xn, (hn[0] if hn else None), y, (x, h, z, act)


def _ffn_bwd(dxn, dy, saved, g, scale, prev, w4, wo, tag):
    x, h, z, act = saved
    ns, _, C = w4.shape
    dz = _ffn_bwd_act(dy, wo, z, f"ffn_bwd_act_{tag}")
    dwo = _mm_tn(act, dy, f"dw_out_{tag}")
    dx, dsh, dsc, dg, *pv = _dh_normmod_bwd([(dz, w4, s, s, C) for s in range(ns)], x, dxn, g, scale, prev, f"ffn_dh_{tag}")
    dwi = _dw_ffn_in(h, dz, ns, f"dw_in_{tag}")
    return dx, pv, dict(dshift=dsh, dscale=dsc, dg=dg, dw_in=dwi, dw_out=dwo.reshape(N_CHIPS, -1, dwo.shape[1]))


def _mix_fwd(x, h, gv, wmix, w_dw, b_dw, g_ln, b_ln, wgp, bgp, gn, wout, nxt, tag):
    z = _mm([(h, wmix, 0)], F32, f"mix_in_{tag}")
    u, yc, yconv = _conv_fwd(z, w_dw, b_dw, g_ln, b_ln, f"conv_fwd_{tag}")
    la = _loggate(z, wgp, bgp, f"loggate_{tag}")
    o_raw, sprev, ygla = _gla_fwd(z, la, gn, f"gla_fwd_{tag}")
    y, xn, *hn = _mix_out(yconv, ygla, wout, x, gv, nxt, f"mix_out_{tag}")
    return xn, (hn[0] if hn else None), y, (x, h, z, u, yc, la, o_raw, sprev, yconv, ygla)


def _mix_bwd(dxn, dy, saved, g, scale, prev, wmix, w_dw, g_ln, b_ln, wgp, bgp, gn, wout, n_cols, n_pad, tag):
    x, h, z, u, yc, la, o_raw, sprev, yconv, ygla = saved
    dycat = _mm([(dy, wout, 0)], F32, f"mix_dycat_{tag}", nt=True)
    dwout = _mm_tn_two(yconv, ygla, dy, f"dw_mixout_{tag}")
    dab, dwdw, dbdw, dgln, dbln = _conv_bwd(dycat, z, u, yc, w_dw, g_ln, b_ln, f"conv_bwd_{tag}")
    dgla, dla, dgn = _gla_bwd(dycat, z, la, o_raw, sprev, gn, f"gla_bwd_{tag}")
    dglr, dwgp, dbgp = _loggate_bwd(dla, z, wgp, wgp.T, bgp, f"loggate_bwd_{tag}")
    dx, dsh, dsc, dg, *pv = _dh_normmod_bwd(
        [(dgla, wmix, 0, 0, Z_GLA), (dab, wmix, 0, ZC_A // (2 * D_CONV), 2 * D_CONV), (dglr, wmix, 0, ZC_G // LANE, LANE)],
        x, dxn, g, scale, prev, f"mix_dh_{tag}")
    dwin = _mix_weight_grad(_mm_tn(h, dgla, f"dw_mixin_gla_{tag}"), _mm_tn(h, dab, f"dw_mixin_conv_{tag}"), _mm_tn(h, dglr, f"dw_mixin_gate_{tag}"),
                            n_cols, n_pad)
    grads = dict(dshift=dsh, dscale=dsc, dg=dg, dw_in=dwin, dw_out=dwout.reshape(N_CHIPS, -1, dwout.shape[1]), dw_dw=dwdw, db_dw=dbdw,
                 dg_ln=dgln, db_ln=dbln, dw_gate=_unpad_heads(dwgp[:GATE_RANK]), db_gate=_unpad_heads(dbgp)[0], dgn=dgn)
    return dx, pv, grads


def kernel(x, c, w_ada, b_ada, g_norm_ffn1, w_ffn1_in, w_ffn1_out, g_norm_mix, w_in, w_dw, b_dw, g_conv_ln, b_conv_ln, w_gate_up, b_gate, g_gla_norm, w_out, g_norm_ffn2, w_ffn2_in, w_ffn2_out, g_norm_final, w_ada_final, b_ada_final, loss_target, m_w_ada, m_b_ada, m_g_norm_ffn1, m_w_ffn1_in, m_w_ffn1_out, m_g_norm_mix, m_w_in, m_w_dw, m_b_dw, m_g_conv_ln, m_b_conv_ln, m_w_gate_up, m_b_gate, m_g_gla_norm, m_w_out, m_g_norm_ffn2, m_w_ffn2_in, m_w_ffn2_out, m_g_norm_final, m_w_ada_final, m_b_ada_final, v_w_ada, v_b_ada, v_g_norm_ffn1, v_w_ffn1_in, v_w_ffn1_out, v_g_norm_mix, v_w_in, v_w_dw, v_b_dw, v_g_conv_ln, v_b_conv_ln, v_w_gate_up, v_b_gate, v_g_gla_norm, v_w_out, v_g_norm_ffn2, v_w_ffn2_in, v_w_ffn2_out, v_g_norm_final, v_w_ada_final, v_b_ada_final):
    given = dict(locals())
    W = {n: given[n] for n in _WEIGHTS}
    M1 = {n: given["m_" + n] for n in _WEIGHTS}
    M2 = {n: given["v_" + n] for n in _WEIGHTS}
    xs = x[0]
    tgt = loss_target[0]
    S, D = xs.shape
    L = w_ada.shape[0]
    xi, yi, ci = _place()
    s_me = 2 * xi + yi
    b_me = 4 * xi + 2 * yi + ci
    nsh = w_ada.shape[2]
    nfin = w_ada_final.shape[1]
    n_cols = w_in.shape[2]
    n_pad = -(-n_cols // LANE) * LANE

    def lands_of(l):
        shards = [W[n][l].astype(BF16) for n in _BIG]
        shards[2] = jnp.pad(shards[2], ((0, 0), (0, n_pad - n_cols)))
        return [lax.dynamic_update_index_in_dim(lax.empty((N_CHIPS,) + s.shape, BF16), s, s_me, 0) for s in shards]

    lands = {l: lands_of(l) for l in range(L)}
    ag_groups = [dict(l=0, items=[0, 1], need=0), dict(l=0, items=[2, 3, 4, 5], need=1)]
    ag_groups += [dict(l=l, items=list(range(len(_BIG))), need=3 * l) for l in range(1, L)]

    def ag_start(grp):
        bufs = [lands[grp["l"]][i] for i in grp["items"]]
        return _split_start(bufs, 3 * len(bufs), _ag_starts, f"ag_start_l{grp['l']}_{grp['items'][0]}")

    pend = ag_start(ag_groups[0])
    tok = pend[3][0, 0]

    c_all = _ag_small(c.reshape(8, D // 8) + tok, "ag_c").reshape(8, D)
    tok = None
    parts = [_mod_proj(c_all, w_ada, l, lax.dynamic_slice(b_ada, (l, s_me * nsh), (1, nsh)), f"mod_proj_{l}") for l in range(L)]
    parts.append(_mod_proj(c_all, w_ada_final[None], 0, lax.dynamic_slice(b_ada_final, (s_me * nfin,), (nfin,))[None], "mod_proj_final"))
    mod_all = _ag_small(jnp.concatenate(parts, axis=1), "ag_mod")
    mine = [lax.dynamic_index_in_dim(lax.dynamic_index_in_dim(mod_all, 2 * s + ci, 0, False), b_me, 0, False) for s in range(N_CHIPS)]
    mods = [jnp.concatenate([mine[s][l * nsh:(l + 1) * nsh] for s in range(N_CHIPS)]).reshape(N_MOD, 1, D) for l in range(L)]
    fmod = jnp.concatenate([mine[s][L * nsh:] for s in range(N_CHIPS)]).reshape(2, 1, D)

    tiny = jnp.concatenate([w_dw.reshape(-1), w_gate_up.reshape(-1)])
    tiny_all = _ag_small(jnp.pad(tiny, (0, (-tiny.shape[0]) % (8 * LANE))).reshape(8, -1), "ag_tiny").reshape(8, -1)
    n_dw = w_dw.size
    dw_parts = [lax.dynamic_index_in_dim(tiny_all, 2 * s + ci, 0, False) for s in range(N_CHIPS)]
    w_dw_full = jnp.concatenate([p[:n_dw].reshape(w_dw.shape) for p in dw_parts], axis=2)
    w_gu_full = jnp.concatenate([p[n_dw:n_dw + w_gate_up.size].reshape(w_gate_up.shape) for p in dw_parts], axis=2)

    def layer_weights(l, lands):
        wi1, wo1, win4, wout4, wi2, wo2 = lands
        return dict(
            wi1=wi1, wo1=wo1.reshape(-1, D), wi2=wi2, wo2=wo2.reshape(-1, D), wout=wout4.reshape(-1, D), wmix=_mix_weight(win4, n_cols),
            wgp=jnp.pad(_pad_heads(w_gu_full[l]), ((0, LANE - GATE_RANK), (0, 0))).astype(BF16), bgp=_pad_heads(b_gate[l])[None])

    gnorm = (g_norm_ffn1, g_norm_mix, g_norm_ffn2)
    subs = [dict(l=l, j=j, tag=f"{('ffn1', 'mix', 'ffn2')[j]}_l{l}", g=gnorm[j][l][None], shift=mods[l][3 * j], scale=mods[l][3 * j + 1],
                 gv=mods[l][3 * j + 2] * (1.0 if j == 1 else 0.5)) for l in range(L) for j in range(3)]
    gi = 0
    xcur = xs
    h = None
    for k, sb in enumerate(subs):
        l, j = sb["l"], sb["j"]
        if pend is not None and ag_groups[gi]["need"] == k:
            grp = ag_groups[gi]
            nm = f"l{grp['l']}_{grp['items'][0]}"
            after = xcur if k > 0 else sb["shift"]
            done = _ag_finish(_split_wait(pend[0], pend[1], pend[2], after, _ag_waits, f"ag_wait_{nm}"), f"ag_finish_{nm}")
            for i, a in zip(grp["items"], done):
                lands[grp["l"]][i] = a
            gi += 1
            pend = ag_start(ag_groups[gi]) if gi < len(ag_groups) else None
            tok = pend[3][0, 0] if pend is not None else None
        if h is None:
            h = _normmod(xcur, sb["g"] if tok is None else sb["g"] + tok, sb["shift"], sb["scale"], f"normmod_{sb['tag']}")
            tok = None
        d = layer_weights(l, lands[l])
        nxt = (subs[k + 1]["g"], subs[k + 1]["shift"], subs[k + 1]["scale"]) if k + 1 < len(subs) else None
        gv = sb["gv"] if tok is None else sb["gv"] + tok
        tok = None
        if j == 1:
            xcur, h, sb["y"], sb["saved"] = _mix_fwd(xcur, h, gv, d["wmix"], w_dw_full[l], b_dw[l][None], g_conv_ln[l][None],
                                                     b_conv_ln[l][None], d["wgp"], d["bgp"], g_gla_norm[l], d["wout"], nxt, sb["tag"])
        else:
            w4, wo = (d["wi1"], d["wo1"]) if j == 0 else (d["wi2"], d["wo2"])
            xcur, h, sb["y"], sb["saved"] = _ffn_fwd(xcur, h, gv, w4, wo, nxt, sb["tag"])
    lw = [layer_weights(l, lands[l]) for l in range(L)]

    c_arr = jnp.stack([ci]).astype(jnp.int32)
    sc_arr = jnp.stack([s_me, ci]).astype(jnp.int32)
    fulls = [lax.empty((L,) + ((W[n].shape[1], n_pad) if n == 'w_in' else W[n].shape[1:]), F32) for n in _BIG]

    def rs_begin(gs, items, l):
        nm = f"l{l}_{items[0]}"
        sibs = _rs_sibling(gs, f"rs_sibling_{nm}")
        return sibs, [_rs_presum(g, sb_, c_arr, f"rs_presum_{i}_l{l}") for i, g, sb_ in zip(items, gs, sibs)]

    def rs_end(gs, sibs, recvs, items, l):
        summed = [_rs_sum(g, sb_, rv, fulls[i], l, sc_arr, f"rs_sum_{i}_l{l}") for i, g, sb_, rv in zip(items, gs, sibs, recvs)]
        for i, f in zip(items, _rs_share(summed, l, f"rs_share_l{l}_{items[0]}")):
            fulls[i] = f

    def rs_start(gs, items, l):
        sibs, ps = rs_begin(gs, items, l)
        pend = _split_start(ps + [lax.empty((3,) + p.shape[1:], BF16) for p in ps], 3 * len(ps), _rs_chip_copies, f"rs_start_l{l}_{items[0]}")
        return dict(gs=gs, sibs=sibs, pend=pend, items=items, l=l)

    def rs_finish(fl, after):
        pend, n = fl["pend"], len(fl["gs"])
        bufs = _split_wait(pend[0], pend[1], pend[2], after, _rs_chip_copies, f"rs_wait_l{fl['l']}_{fl['items'][0]}")
        rs_end(fl["gs"], fl["sibs"], bufs[n:], fl["items"], fl["l"])

    dh, sq = _final_loss(xcur, g_norm_final[None], fmod[0], fmod[1], tgt)
    loss_part = 0.5 / D * jnp.sum(sq)
    dx, dfsh, dfsc, dgfin, dy, dgv = _normmod_bwd(xcur, dh, None, g_norm_final[None], fmod[1], (subs[-1]["y"], subs[-1]["gv"]), "normmod_bwd_final")
    G = {n: [None] * L for n in _SMALL}
    dmods = [None] * L
    in_flight = None
    tok = None
    for l in reversed(range(L)):
        gr = [None] * 3
        for j in reversed(range(3)):
            k = 3 * l + j
            sb, d = subs[k], lw[l]
            prev = (subs[k - 1]["y"], subs[k - 1]["gv"]) if k > 0 else None
            g_vec = sb["g"] if tok is None else sb["g"] + tok
            tok = None
            if j == 1:
                dx, pv, gr[j] = _mix_bwd(dx, dy, sb["saved"], g_vec, sb["scale"], prev, d["wmix"], w_dw_full[l], g_conv_ln[l][None], b_conv_ln[l][None],
                                         d["wgp"], d["bgp"], g_gla_norm[l], d["wout"], n_cols, n_pad, sb["tag"])
            else:
                w4, wo = (d["wi1"], d["wo1"]) if j == 0 else (d["wi2"], d["wo2"])
                dx, pv, gr[j] = _ffn_bwd(dx, dy, sb["saved"], g_vec, sb["scale"], prev, w4, wo, sb["tag"])
            gr[j]["dgv"] = dgv
            dy, dgv = pv if pv else (None, None)
            if j == 1 and in_flight is not None:
                rs_finish(in_flight, dx)
                in_flight = None
            if j == 1 and l == 0:
                in_flight = rs_start([gr[1]["dw_in"], gr[1]["dw_out"], gr[2]["dw_in"], gr[2]["dw_out"]], [2, 3, 4, 5], l)
                tok = in_flight["pend"][3][0, 0]
        g1, g2, g3 = gr
        if l > 0:
            in_flight = rs_start([g1["dw_in"], g1["dw_out"], g2["dw_in"], g2["dw_out"], g3["dw_in"], g3["dw_out"]], list(range(len(_BIG))), l)
            tok = in_flight["pend"][3][0, 0]
        else:
            last = rs_start([g1["dw_in"], g1["dw_out"]], [0, 1], l)
            rs_finish(in_flight, dx)
            in_flight = None
        dmods[l] = jnp.concatenate([g1["dshift"], g1["dscale"], 0.5 * g1["dgv"], g2["dshift"], g2["dscale"], g2["dgv"],
                                    g3["dshift"], g3["dscale"], 0.5 * g3["dgv"]], axis=1)[0]
        G["g_norm_ffn1"][l], G["g_norm_ffn2"][l], G["g_norm_mix"][l] = g1["dg"][0], g3["dg"][0], g2["dg"][0]
        G["w_dw"][l], G["b_dw"][l], G["g_conv_ln"][l], G["b_conv_ln"][l] = g2["dw_dw"], g2["db_dw"][0], g2["dg_ln"][0], g2["db_ln"][0]
        G["w_gate_up"][l], G["b_gate"][l], G["g_gla_norm"][l] = g2["dw_gate"], g2["db_gate"], g2["dgn"]
    grad_x = dx[None]
    gsm = {}

    small = [jnp.stack(G[n]).reshape(-1) for n in _SMALL if n != 'g_norm_final'] + [dgfin[0]]
    dmod_vec = jnp.concatenate(dmods + [dfsh[0], dfsc[0]])
    n_mod_vec = dmod_vec.shape[0]
    vec = jnp.concatenate([dmod_vec] + small + [loss_part[None]])
    n_vec = vec.shape[0]
    vec = jnp.pad(vec, (0, (-n_vec) % (8 * LANE)))
    vec_all = _ag_small(vec.reshape(8, -1), "ag_small_grads").reshape(8, -1)
    vec_sum = _rowsum(vec_all, "sum_small_grads")[0]
    loss = vec_sum[n_vec - 1]
    off = n_mod_vec
    for n in _SMALL:
        shp = {'w_dw': w_dw_full.shape, 'w_gate_up': w_gu_full.shape}.get(n, W[n].shape)
        cnt = 1
        for dd in shp:
            cnt *= dd
        gsm[n] = vec_sum[off:off + cnt].reshape(shp)
        off += cnt
    gsm['w_dw'] = lax.dynamic_slice_in_dim(gsm['w_dw'], s_me * w_dw.shape[2], w_dw.shape[2], 2)
    gsm['w_gate_up'] = lax.dynamic_slice_in_dim(gsm['w_gate_up'], s_me * w_gate_up.shape[2], w_gate_up.shape[2], 2)
    dmod_sum = vec_sum[:n_mod_vec]
    gsm['b_ada'] = dmod_sum[:L * N_MOD * D].reshape(L, N_MOD * D)
    gsm['b_ada_final'] = dmod_sum[L * N_MOD * D:]
    c_t = c_all.T
    dmod_rows = vec_all[:, :n_mod_vec]
    gsm['w_ada'] = jnp.stack([
        _mod_wgrad(c_t, lax.dynamic_slice_in_dim(dmod_rows, l * N_MOD * D + s_me * nsh, nsh, 1), f"dw_ada_{l}") for l in range(L)])
    gsm['w_ada_final'] = _mod_wgrad(c_t, lax.dynamic_slice_in_dim(dmod_rows, L * N_MOD * D + s_me * nfin, nfin, 1), "dw_ada_final")
    rs_finish(last, gsm['w_ada_final'])
    gsm.update({n: (f[:, :, :n_cols] if n == 'w_in' else f) for n, f in zip(_BIG, fulls)})

    outs = {}
    small_names = [n for n in _WEIGHTS if W[n].size < 65536]
    for n in _WEIGHTS:
        if n in small_names:
            continue
        shp = W[n].shape
        v2 = lambda a: a.reshape(-1, shp[-1])
        d_, m_, v_ = _adamw(v2(W[n]), v2(gsm[n]), v2(M1[n]), v2(M2[n]), f"adamw_{n}")
        outs[n] = (d_.reshape(shp), m_.reshape(shp), v_.reshape(shp))
    flat = lambda dct: jnp.concatenate([dct[n].reshape(-1) for n in small_names])
    n_small = sum(W[n].size for n in small_names)
    v2 = lambda a: jnp.pad(a, (0, (-n_small) % (8 * LANE))).reshape(-1, LANE)
    d_, m_, v_ = _adamw(v2(flat(W)), v2(flat(gsm)), v2(flat(M1)), v2(flat(M2)), "adamw_small")

    def unflat(a):
        res, o = {}, 0
        a = a.reshape(-1)
        for n in small_names:
            res[n] = a[o:o + W[n].size].reshape(W[n].shape)
            o += W[n].size
        return res

    for n, dd, mm, vv in zip(small_names, unflat(d_).values(), unflat(m_).values(), unflat(v_).values()):
        outs[n] = (dd, mm, vv)

    return (loss, grad_x, *[gsm[n] for n in _WEIGHTS], *[outs[n][0] for n in _WEIGHTS], *[outs[n][1] for n in _WEIGHTS], *[outs[n][2] for n in _WEIGHTS])
```

```python
import jax
import jax.numpy as jnp
from jax import lax
from jax.experimental import pallas as pl
from jax.experimental.pallas import tpu as pltpu

F32 = jnp.float32
BF16 = jnp.bfloat16

CHUNK = 64
HEADS = 4
DK = 64
DV = 128
DKP = 128
GATE_RANK = 16
GATE_TAU = 16.0
N_MOD = 9
EPS = 1e-6
ADAM_LR = 0.001
ADAM_B1 = 0.9
ADAM_B2 = 0.999
ADAM_EPS = 1e-08
ADAM_WD = 0.01
ADAM_STEP = 10

LANE = 128
HALO = 32
VMEM_LIMIT = 52 * 1024 * 1024
MESH = pl.DeviceIdType.MESH
N_CHIPS = 4

D_CONV = 512
D_GLA = HEADS * DV
ZC_Q = 0
ZC_K = ZC_Q + HEADS * DKP
ZC_V = ZC_K + HEADS * DKP
ZC_R = ZC_V + D_GLA
ZC_A = ZC_R + D_GLA
ZC_B = ZC_A + D_CONV
ZC_G = ZC_B + D_CONV
Z_COLS = ZC_G + LANE
Z_GLA = ZC_A


def _div(n, target, mult):
    best = None
    d = mult
    while d <= min(n, target):
        if n % d == 0:
            best = d
        d += mult
    return n if best is None else best


def _cp(sem=None, **kw):
    return pltpu.CompilerParams(dimension_semantics=sem, vmem_limit_bytes=VMEM_LIMIT, **kw)


def _resident(shape, index_map):
    return pl.BlockSpec(shape, index_map, pipeline_mode=pl.Buffered(1))


def _sigmoid(x):
    return 0.5 * jnp.tanh(0.5 * x) + 0.5


def _dot(a, b):
    return jnp.dot(a.astype(BF16), b.astype(BF16), preferred_element_type=F32)


def _dot_nt(a, b):
    return lax.dot_general(a.astype(BF16), b.astype(BF16), (((1,), (1,)), ((), ())), preferred_element_type=F32)


def _dot_tn(a, b):
    return lax.dot_general(a.astype(BF16), b.astype(BF16), (((0,), (0,)), ((), ())), preferred_element_type=F32)


def _dot_exact(a, b):
    return jnp.dot(a, b, preferred_element_type=F32, precision=lax.Precision.HIGHEST)


def _normmod(x, g, shift, scale, name):
    S, D = x.shape
    tm = _div(S, 512, 8)

    def body(x_ref, g_ref, sh_ref, sc_ref, o_ref):
        xv = x_ref[...]
        r = lax.rsqrt(jnp.mean(xv * xv, axis=-1, keepdims=True) + EPS)
        o_ref[...] = ((xv * r) * g_ref[...] * (1.0 + sc_ref[...]) + sh_ref[...]).astype(o_ref.dtype)

    row = pl.BlockSpec((tm, D), lambda i: (i, 0))
    vec = pl.BlockSpec((1, D), lambda i: (0, 0))
    return pl.pallas_call(
        body, grid=(S // tm,), in_specs=[row, vec, vec, vec], out_specs=row,
        out_shape=jax.ShapeDtypeStruct((S, D), BF16), compiler_params=_cp(("parallel",)), name=name,
    )(x, g, shift, scale)


def _final_loss(x, g, shift, scale, tgt):
    S, D = x.shape
    tm = _div(S, 512, 8)

    def body(x_ref, g_ref, sh_ref, sc_ref, t_ref, dh_ref, sq_ref):
        @pl.when(pl.program_id(0) == 0)
        def _():
            sq_ref[...] = jnp.zeros_like(sq_ref)

        xv = x_ref[...]
        r = lax.rsqrt(jnp.mean(xv * xv, axis=-1, keepdims=True) + EPS)
        h = (xv * r) * g_ref[...] * (1.0 + sc_ref[...]) + sh_ref[...]
        e = h - t_ref[...]
        dh_ref[...] = e * (1.0 / D)
        sq_ref[...] += jnp.sum(e * e, axis=0, keepdims=True)

    row = pl.BlockSpec((tm, D), lambda i: (i, 0))
    vec = pl.BlockSpec((1, D), lambda i: (0, 0))
    return pl.pallas_call(
        body, grid=(S // tm,), in_specs=[row, vec, vec, vec, row], out_specs=[row, vec],
        out_shape=[jax.ShapeDtypeStruct((S, D), F32), jax.ShapeDtypeStruct((1, D), F32)],
        compiler_params=_cp(("arbitrary",)), name="final_loss",
    )(x, g, shift, scale, tgt)


def _normmod_bwd_rows(xv, dh, dres, gv, sc, prev):
    r = lax.rsqrt(jnp.mean(xv * xv, axis=-1, keepdims=True) + EPS)
    xh = xv * r
    dsh = jnp.sum(dh, axis=0, keepdims=True)
    dsc = jnp.sum(dh * (xh * gv), axis=0, keepdims=True)
    dn = dh * (1.0 + sc)
    dg = jnp.sum(dn * xh, axis=0, keepdims=True)
    dxh = dn * gv
    dx = r * (dxh - xh * jnp.mean(dxh * xh, axis=-1, keepdims=True))
    if dres is not None:
        dx = dx + dres
    if prev is None:
        return dx, dsh, dsc, dg
    y, gvp = prev
    return dx, dsh, dsc, dg, (gvp * dx).astype(BF16), jnp.sum(dx * y, axis=0, keepdims=True)


def _normmod_bwd(x, dh, dres, g, scale, prev, name):
    S, D = x.shape
    tm = _div(S, 512, 8)
    with_res = dres is not None
    with_prev = prev is not None

    def body(*refs):
        refs = list(refs)
        x_ref, dh_ref = refs[:2]
        del refs[:2]
        dr_ref = refs.pop(0) if with_res else None
        g_ref, sc_ref = refs[:2]
        del refs[:2]
        pv = None
        if with_prev:
            pv = (refs[0][...], refs[1][...])
            del refs[:2]

        @pl.when(pl.program_id(0) == 0)
        def _():
            for o in refs[1:4] + refs[5:]:
                o[...] = jnp.zeros_like(o)

        res = _normmod_bwd_rows(x_ref[...], dh_ref[...].astype(F32), dr_ref[...] if with_res else None, g_ref[...], sc_ref[...], pv)
        refs[0][...] = res[0]
        for o, v in zip(refs[1:4], res[1:4]):
            o[...] += v
        if with_prev:
            refs[4][...] = res[4]
            refs[5][...] += res[5]

    row = pl.BlockSpec((tm, D), lambda i: (i, 0))
    vec = pl.BlockSpec((1, D), lambda i: (0, 0))
    ins = [row, row] + [row] * with_res + [vec, vec] + [row, vec] * with_prev
    args = (x, dh) + ((dres,) if with_res else ()) + (g, scale) + (tuple(prev) if with_prev else ())
    vs = jax.ShapeDtypeStruct((1, D), F32)
    return pl.pallas_call(
        body, grid=(S // tm,), in_specs=ins, out_specs=[row, vec, vec, vec] + [row, vec] * with_prev,
        out_shape=[jax.ShapeDtypeStruct((S, D), F32), vs, vs, vs] + [jax.ShapeDtypeStruct((S, D), BF16), vs] * with_prev,
        compiler_params=_cp(("arbitrary",)), name=name,
    )(*args)


def _mm(pairs, out_dtype, name, nt=False):
    M = pairs[0][0].shape[0]
    N = pairs[0][1].shape[0] if nt else pairs[0][1].shape[1]
    ktot = sum(a.shape[1] for a, _, _ in pairs)
    tm = _div(M, 512 if ktot <= 4096 else 256, 8)
    n = len(pairs)

    def body(*refs):
        o_ref = refs[2 * n]
        dot = _dot_nt if nt else _dot
        acc = dot(refs[0][...], refs[1][...])
        for p in range(1, n):
            acc = acc + dot(refs[2 * p][...], refs[2 * p + 1][...])
        o_ref[...] = acc.astype(o_ref.dtype)

    ins, args = [], []
    for a, b, blk in pairs:
        k = a.shape[1]
        ins.append(pl.BlockSpec((tm, k), lambda i: (i, 0)))
        ins.append(_resident((N, k), lambda i, blk=blk: (0, blk)) if nt else _resident((k, N), lambda i: (0, 0)))
        args += [a, b]
    return pl.pallas_call(
        body, grid=(M // tm,), in_specs=ins, out_specs=pl.BlockSpec((tm, N), lambda i: (i, 0)),
        out_shape=jax.ShapeDtypeStruct((M, N), out_dtype), compiler_params=_cp(("parallel",)), name=name,
    )(*args)


TN_ROWS = 2048


def _mm_tn(a, g, name):
    S, Ka = a.shape
    N = g.shape[1]
    tk = _div(Ka, 1408, LANE)
    tn = _div(N, 1408, LANE)
    ts = _div(S, TN_ROWS, 8)

    def body(a_ref, g_ref, o_ref):
        @pl.when(pl.program_id(2) == 0)
        def _():
            o_ref[...] = jnp.zeros_like(o_ref)

        o_ref[...] += _dot_tn(a_ref[...], g_ref[...])

    return pl.pallas_call(
        body, grid=(Ka // tk, N // tn, S // ts),
        in_specs=[pl.BlockSpec((ts, tk), lambda i, j, s: (s, i)), pl.BlockSpec((ts, tn), lambda i, j, s: (s, j))],
        out_specs=pl.BlockSpec((tk, tn), lambda i, j, s: (i, j)),
        out_shape=jax.ShapeDtypeStruct((Ka, N), F32),
        compiler_params=_cp(("parallel", "parallel", "arbitrary")), name=name,
    )(a, g)


def _mm_tn_two(a0, a1, g, name):
    S, K = a0.shape
    N = g.shape[1]
    ts = _div(S, TN_ROWS, 8)

    def body(a0_ref, a1_ref, g_ref, o_ref):
        i = pl.program_id(0)

        @pl.when(pl.program_id(1) == 0)
        def _():
            o_ref[...] = jnp.zeros_like(o_ref)

        @pl.when(i == 0)
        def _():
            o_ref[...] += _dot_tn(a0_ref[...], g_ref[...])

        @pl.when(i == 1)
        def _():
            o_ref[...] += _dot_tn(a1_ref[...], g_ref[...])

    return pl.pallas_call(
        body, grid=(2, S // ts),
        in_specs=[pl.BlockSpec((ts, K), lambda i, s: (jnp.where(i == 0, s, 0), 0)),
                  pl.BlockSpec((ts, K), lambda i, s: (jnp.where(i == 1, s, 0), 0)),
                  pl.BlockSpec((ts, N), lambda i, s: (s, 0))],
        out_specs=pl.BlockSpec((K, N), lambda i, s: (i, 0)),
        out_shape=jax.ShapeDtypeStruct((2 * K, N), F32),
        compiler_params=_cp(("parallel", "arbitrary")), name=name,
    )(a0, a1, g)


def _swiglu(gt, up):
    return gt * _sigmoid(gt) * up


def _ffn_up(h, w4, name):
    S, D = h.shape
    ns, _, C = w4.shape
    hs = ns // 2
    tm = _div(S, 256, 8)

    def body(h_ref, w_ref, z_ref, a_ref):
        hv = h_ref[...]
        for s in range(hs):
            gt = _dot(hv, w_ref[s])
            up = _dot(hv, w_ref[hs + s])
            sg = _sigmoid(gt)
            silu = gt * sg
            z_ref[:, s * C:(s + 1) * C] = (up * (sg * (1.0 + gt * (1.0 - sg)))).astype(BF16)
            z_ref[:, (hs + s) * C:(hs + s + 1) * C] = silu.astype(BF16)
            a_ref[:, s * C:(s + 1) * C] = (silu * up).astype(BF16)

    return pl.pallas_call(
        body, grid=(S // tm,), in_specs=[pl.BlockSpec((tm, D), lambda i: (i, 0)), _resident((ns, D, C), lambda i: (0, 0, 0))],
        out_specs=[pl.BlockSpec((tm, ns * C), lambda i: (i, 0)), pl.BlockSpec((tm, hs * C), lambda i: (i, 0))],
        out_shape=[jax.ShapeDtypeStruct((S, ns * C), BF16), jax.ShapeDtypeStruct((S, hs * C), BF16)],
        compiler_params=_cp(("parallel",)), name=name,
    )(h, w4)


def _norm_rows(xv, g, shift, scale):
    r = lax.rsqrt(jnp.mean(xv * xv, axis=-1, keepdims=True) + EPS)
    return (xv * r) * g * (1.0 + scale) + shift


def _resid_outputs(y, x_ref, gv_ref, nxt_refs, out_refs):
    out_refs[0][...] = y
    xn = x_ref[...] + gv_ref[...] * y
    out_refs[1][...] = xn
    if nxt_refs:
        out_refs[2][...] = _norm_rows(xn, nxt_refs[0][...], nxt_refs[1][...], nxt_refs[2][...]).astype(BF16)


def _ffn_down(act, wo, x, gv, nxt, name):
    S = act.shape[0]
    Fd, D = wo.shape
    tm = _div(S, 512, 8)
    nn = 3 if nxt else 0

    def body(a_ref, w_ref, x_ref, gv_ref, *rest):
        _resid_outputs(_dot(a_ref[...], w_ref[...]), x_ref, gv_ref, rest[:nn], rest[nn:])

    row = pl.BlockSpec((tm, D), lambda i: (i, 0))
    vec = pl.BlockSpec((1, D), lambda i: (0, 0))
    os_ = jax.ShapeDtypeStruct((S, D), F32)
    return pl.pallas_call(
        body, grid=(S // tm,),
        in_specs=[pl.BlockSpec((tm, Fd), lambda i: (i, 0)), _resident((Fd, D), lambda i: (0, 0)), row, vec] + [vec] * nn,
        out_specs=[row, row] + [row] * (nn // 3), out_shape=[os_, os_] + [jax.ShapeDtypeStruct((S, D), BF16)] * (nn // 3),
        compiler_params=_cp(("parallel",)), name=name,
    )(act, wo, x, gv, *(nxt or ()))


def _ffn_bwd_act(dy, wo, z, name):
    S, D = dy.shape
    Fd = wo.shape[0]
    tm = _div(S, 256, 8)

    def body(dy_ref, w_ref, g_ref, u_ref, dz_ref):
        da = _dot_nt(dy_ref[...], w_ref[...])
        dz_ref[:, :Fd] = (da * g_ref[...].astype(F32)).astype(BF16)
        dz_ref[:, Fd:] = (da * u_ref[...].astype(F32)).astype(BF16)

    return pl.pallas_call(
        body, grid=(S // tm,),
        in_specs=[pl.BlockSpec((tm, D), lambda i: (i, 0)), _resident((Fd, D), lambda i: (0, 0)),
                  pl.BlockSpec((tm, Fd), lambda i: (i, 0)), pl.BlockSpec((tm, Fd), lambda i: (i, 1))],
        out_specs=pl.BlockSpec((tm, 2 * Fd), lambda i: (i, 0)), out_shape=jax.ShapeDtypeStruct((S, 2 * Fd), BF16),
        compiler_params=_cp(("parallel",)), name=name,
    )(dy, wo, z, z)


def _dw_ffn_in(h, dz, ns, name):
    S, D = h.shape
    C = dz.shape[1] // ns
    ts = _div(S, TN_ROWS, 8)

    def body(h_ref, g_ref, o_ref):
        @pl.when(pl.program_id(1) == 0)
        def _():
            o_ref[...] = jnp.zeros_like(o_ref)

        o_ref[...] += _dot_tn(h_ref[...], g_ref[...])

    return pl.pallas_call(
        body, grid=(ns, S // ts),
        in_specs=[pl.BlockSpec((ts, D), lambda j, s: (s, 0)), pl.BlockSpec((ts, C), lambda j, s: (s, j))],
        out_specs=pl.BlockSpec((None, D, C), lambda j, s: (j, 0, 0)), out_shape=jax.ShapeDtypeStruct((ns, D, C), F32),
        compiler_params=_cp(("parallel", "arbitrary")), name=name,
    )(h, dz)


def _dh_normmod_bwd(pairs, x, dres, g, scale, prev, name):
    S, D = x.shape
    tm = _div(S, 256, 8)
    n = len(pairs)
    with_prev = prev is not None

    def body(*refs):
        refs = list(refs)
        mm = refs[:2 * n]
        x_ref, dr_ref, g_ref, sc_ref = refs[2 * n:2 * n + 4]
        outs = refs[2 * n + 4 + 2 * with_prev:]

        @pl.when(pl.program_id(0) == 0)
        def _():
            for o in outs[1:4] + outs[5:]:
                o[...] = jnp.zeros_like(o)

        dh = _dot_nt(mm[0][...], mm[1][...])
        for p in range(1, n):
            dh = dh + _dot_nt(mm[2 * p][...], mm[2 * p + 1][...])
        pv = (refs[2 * n + 4][...], refs[2 * n + 5][...]) if with_prev else None
        res = _normmod_bwd_rows(x_ref[...], dh, dr_ref[...], g_ref[...], sc_ref[...], pv)
        outs[0][...] = res[0]
        for o, v in zip(outs[1:4], res[1:4]):
            o[...] += v
        if with_prev:
            outs[4][...] = res[4]
            outs[5][...] += res[5]

    row = pl.BlockSpec((tm, D), lambda i: (i, 0))
    vec = pl.BlockSpec((1, D), lambda i: (0, 0))
    ins, args = [], []
    for a, b, a_blk, b_blk, k in pairs:
        ins.append(pl.BlockSpec((tm, k), lambda i, a_blk=a_blk: (i, a_blk)))
        ins.append(_resident((None, D, k), lambda i, b_blk=b_blk: (b_blk, 0, 0)) if b.ndim == 3 else _resident((D, k), lambda i, b_blk=b_blk: (0, b_blk)))
        args += [a, b]
    ins += [row, row, vec, vec] + [row, vec] * with_prev
    args += [x, dres, g, scale] + (list(prev) if with_prev else [])
    vs = jax.ShapeDtypeStruct((1, D), F32)
    return pl.pallas_call(
        body, grid=(S // tm,), in_specs=ins, out_specs=[row, vec, vec, vec] + [row, vec] * with_prev,
        out_shape=[jax.ShapeDtypeStruct((S, D), F32), vs, vs, vs] + [jax.ShapeDtypeStruct((S, D), BF16), vs] * with_prev,
        compiler_params=_cp(("arbitrary",)), name=name,
    )(*args)


def _mix_out(yconv, ygla, wout, x, gv, nxt, name):
    S, Kc = yconv.shape
    Kg = ygla.shape[1]
    D = wout.shape[1]
    tm = _div(S, 512, 8)
    nn = 3 if nxt else 0

    def body(a_ref, b_ref, w_ref, x_ref, gv_ref, *rest):
        y = _dot(a_ref[...], w_ref[0:Kc, :]) + _dot(b_ref[...], w_ref[Kc:Kc + Kg, :])
        _resid_outputs(y, x_ref, gv_ref, rest[:nn], rest[nn:])

    row = pl.BlockSpec((tm, D), lambda i: (i, 0))
    vec = pl.BlockSpec((1, D), lambda i: (0, 0))
    os_ = jax.ShapeDtypeStruct((S, D), F32)
    return pl.pallas_call(
        body, grid=(S // tm,),
        in_specs=[pl.BlockSpec((tm, Kc), lambda i: (i, 0)), pl.BlockSpec((tm, Kg), lambda i: (i, 0)), _resident((Kc + Kg, D), lambda i: (0, 0)), row,
                  vec] + [vec] * nn,
        out_specs=[row, row] + [row] * (nn // 3), out_shape=[os_, os_] + [jax.ShapeDtypeStruct((S, D), BF16)] * (nn // 3),
        compiler_params=_cp(("parallel",)), name=name,
    )(yconv, ygla, wout, x, gv, *(nxt or ()))


def _ln_parts(yc, g, b):
    mu = jnp.mean(yc, axis=-1, keepdims=True)
    xc = yc - mu
    rs = lax.rsqrt(jnp.mean(xc * xc, axis=-1, keepdims=True) + EPS)
    xh = xc * rs
    return xh, rs, xh * g + b


SUB = 8
CONV_ROWS = 32


def _shifted_copies(ext8, rows):
    for b in range(1, SUB):
        ext8[b, pl.ds(0, rows - SUB), :] = ext8[0, pl.ds(b, rows - SUB), :]


def _tap(o):
    return o % SUB, o - o % SUB


def _conv_fwd(z, w_dw, b_dw, g_ln, b_ln, name):
    S = z.shape[0]
    W, C = w_dw.shape
    ts = _div(S, 512, HALO)
    hb = ts // HALO
    off = HALO - (W - 1)
    ca, cb = ZC_A // C, ZC_B // C
    rb = CONV_ROWS

    def body(a_ref, b_ref, pa_ref, pb_ref, w_ref, bd_ref, g_ref, bl_ref, u_ref, yc_ref, o_ref, ext8):
        keep = (pl.program_id(0) > 0).astype(F32)
        u = a_ref[...] * _sigmoid(b_ref[...])
        ext8[0, pl.ds(0, HALO), :] = pa_ref[...] * _sigmoid(pb_ref[...]) * keep
        ext8[0, pl.ds(HALO, ts), :] = u
        u_ref[...] = u
        _shifted_copies(ext8, ts + HALO)

        def sub(i, carry):
            r0 = pl.multiple_of(i * rb, rb)
            acc = jnp.zeros((rb, C), F32)
            for j in range(W):
                b, a = _tap(off + j)
                acc = acc + w_ref[pl.ds(j, 1), :] * ext8[b, pl.ds(r0 + a, rb), :]
            yc = acc + bd_ref[...]
            yc_ref[pl.ds(r0, rb), :] = yc
            _, _, ln = _ln_parts(yc, g_ref[...], bl_ref[...])
            o_ref[pl.ds(r0, rb), :] = (ln * _sigmoid(ln)).astype(BF16)
            return carry

        lax.fori_loop(0, ts // rb, sub, 0)

    cur = lambda col: pl.BlockSpec((ts, C), lambda i: (i, col))
    prev = lambda col: pl.BlockSpec((HALO, C), lambda i: (jnp.maximum(i * hb - 1, 0), col))
    vec = pl.BlockSpec((1, C), lambda i: (0, 0))
    row = pl.BlockSpec((ts, C), lambda i: (i, 0))
    fs = jax.ShapeDtypeStruct((S, C), F32)
    return pl.pallas_call(
        body, grid=(S // ts,),
        in_specs=[cur(ca), cur(cb), prev(ca), prev(cb), pl.BlockSpec((W, C), lambda i: (0, 0)), vec, vec, vec],
        out_specs=[row, row, row], out_shape=[fs, fs, jax.ShapeDtypeStruct((S, C), BF16)],
        scratch_shapes=[pltpu.VMEM((SUB, ts + HALO, C), F32)],
        compiler_params=_cp(("parallel",)), name=name,
    )(z, z, z, z, w_dw, b_dw, g_ln, b_ln)


def _conv_bwd(dycat, z, u, yc, w_dw, g_ln, b_ln, name):
    S = z.shape[0]
    W, C = w_dw.shape
    ts = _div(S, 512, HALO)
    hb = ts // HALO
    nblk = S // ts
    off = HALO - (W - 1)
    ca, cb = ZC_A // C, ZC_B // C
    rb = CONV_ROWS

    def ln_silu_bwd(dy, ycv, g, b):
        xh, rs, ln = _ln_parts(ycv, g, b)
        sl = _sigmoid(ln)
        dln = dy * (sl * (1.0 + ln * (1.0 - sl)))
        dxh = dln * g
        dyc = rs * (dxh - jnp.mean(dxh, axis=-1, keepdims=True) - xh * jnp.mean(dxh * xh, axis=-1, keepdims=True))
        return dyc, dln, xh

    def body(dy_ref, ndy_ref, yc_ref, nyc_ref, u_ref, pu_ref, a_ref, b_ref, w_ref, g_ref, bl_ref,
             dab_ref, dw_ref, dbd_ref, dg_ref, dbl_ref, uext8, dext8, dwacc):
        i = pl.program_id(0)

        @pl.when(i == 0)
        def _():
            dwacc[...] = jnp.zeros_like(dwacc)
            dbd_ref[...] = jnp.zeros_like(dbd_ref)
            dg_ref[...] = jnp.zeros_like(dg_ref)
            dbl_ref[...] = jnp.zeros_like(dbl_ref)

        g = g_ref[...]
        bl = bl_ref[...]
        dyc, dln, xh = ln_silu_bwd(dy_ref[...], yc_ref[...], g, bl)
        ndyc, _, _ = ln_silu_bwd(ndy_ref[...], nyc_ref[...], g, bl)
        dg_ref[...] += jnp.sum(dln * xh, axis=0, keepdims=True)
        dbl_ref[...] += jnp.sum(dln, axis=0, keepdims=True)
        dbd_ref[...] += jnp.sum(dyc, axis=0, keepdims=True)
        dext8[0, pl.ds(0, ts), :] = dyc
        dext8[0, pl.ds(ts, HALO), :] = ndyc * (i < nblk - 1).astype(F32)
        uext8[0, pl.ds(0, HALO), :] = pu_ref[...] * (i > 0).astype(F32)
        uext8[0, pl.ds(HALO, ts), :] = u_ref[...]
        _shifted_copies(dext8, ts + HALO)
        _shifted_copies(uext8, ts + HALO)

        def sub(k, carry):
            r0 = pl.multiple_of(k * rb, rb)
            rows = pl.ds(r0, rb)
            dyt = dext8[0, rows, :]
            du = jnp.zeros((rb, C), F32)
            for j in range(W):
                b, a = _tap(W - 1 - j)
                du = du + w_ref[pl.ds(j, 1), :] * dext8[b, pl.ds(r0 + a, rb), :]
                b, a = _tap(off + j)
                p = dyt * uext8[b, pl.ds(r0 + a, rb), :]
                part = p[0:SUB]
                for q in range(1, rb // SUB):
                    part = part + p[q * SUB:(q + 1) * SUB]
                dwacc[j] += part
            sb = _sigmoid(b_ref[rows, :])
            dab_ref[rows, 0:C] = (du * sb).astype(BF16)
            dab_ref[rows, C:2 * C] = (du * a_ref[rows, :] * sb * (1.0 - sb)).astype(BF16)
            return carry

        lax.fori_loop(0, ts // rb, sub, 0)

        @pl.when(i == nblk - 1)
        def _():
            for j in range(W):
                dw_ref[pl.ds(j, 1), :] = jnp.sum(dwacc[j], axis=0, keepdims=True)

    row = pl.BlockSpec((ts, C), lambda i: (i, 0))
    nxt = pl.BlockSpec((HALO, C), lambda i: (jnp.minimum((i + 1) * hb, S // HALO - 1), 0))
    prv = pl.BlockSpec((HALO, C), lambda i: (jnp.maximum(i * hb - 1, 0), 0))
    vec = pl.BlockSpec((1, C), lambda i: (0, 0))
    wsp = pl.BlockSpec((W, C), lambda i: (0, 0))
    vs = jax.ShapeDtypeStruct((1, C), F32)
    return pl.pallas_call(
        body, grid=(nblk,),
        in_specs=[row, nxt, row, nxt, row, prv, pl.BlockSpec((ts, C), lambda i: (i, ca)), pl.BlockSpec((ts, C), lambda i: (i, cb)), wsp, vec, vec],
        out_specs=[pl.BlockSpec((ts, 2 * C), lambda i: (i, 0)), wsp, vec, vec, vec],
        out_shape=[jax.ShapeDtypeStruct((S, 2 * C), BF16), jax.ShapeDtypeStruct((W, C), F32), vs, vs, vs],
        scratch_shapes=[pltpu.VMEM((SUB, ts + HALO, C), F32), pltpu.VMEM((SUB, ts + HALO, C), F32), pltpu.VMEM((W, SUB, C), F32)],
        compiler_params=_cp(("arbitrary",)), name=name,
    )(dycat, dycat, yc, yc, u, u, z, z, w_dw, g_ln, b_ln)


def _log_gate(zg):
    return (jnp.minimum(zg, 0.0) - jnp.log(1.0 + jnp.exp(-jnp.abs(zg)))) * (1.0 / GATE_TAU)


def _loggate(z, wgp, bgp, name):
    S = z.shape[0]
    N = wgp.shape[1]
    ts = _div(S, 512, 8)

    def body(g_ref, w_ref, b_ref, o_ref):
        o_ref[...] = _log_gate(_dot(g_ref[...], w_ref[...]) + b_ref[...])

    return pl.pallas_call(
        body, grid=(S // ts,),
        in_specs=[pl.BlockSpec((ts, LANE), lambda i: (i, ZC_G // LANE)), pl.BlockSpec((LANE, N), lambda i: (0, 0)), pl.BlockSpec((1, N), lambda i: (0, 0))],
        out_specs=pl.BlockSpec((ts, N), lambda i: (i, 0)), out_shape=jax.ShapeDtypeStruct((S, N), F32),
        compiler_params=_cp(("parallel",)), name=name,
    )(z, wgp, bgp)


def _loggate_bwd(dla, z, wgp, wgp_t, bgp, name):
    S = z.shape[0]
    N = wgp.shape[1]
    ts = _div(S, 512, 8)

    def body(dla_ref, g_ref, w_ref, wt_ref, b_ref, dg_ref, dw_ref, db_ref):
        @pl.when(pl.program_id(0) == 0)
        def _():
            dw_ref[...] = jnp.zeros_like(dw_ref)
            db_ref[...] = jnp.zeros_like(db_ref)

        glr = g_ref[...]
        zg = _dot(glr, w_ref[...]) + b_ref[...]
        dzg = dla_ref[...] * (1.0 / GATE_TAU) * (1.0 - _sigmoid(zg))
        dg_ref[...] = _dot(dzg, wt_ref[...]).astype(BF16)
        dw_ref[...] += _dot_tn(glr, dzg)
        db_ref[...] += jnp.sum(dzg, axis=0, keepdims=True)

    return pl.pallas_call(
        body, grid=(S // ts,),
        in_specs=[pl.BlockSpec((ts, N), lambda i: (i, 0)), pl.BlockSpec((ts, LANE), lambda i: (i, ZC_G // LANE)),
                  pl.BlockSpec((LANE, N), lambda i: (0, 0)), pl.BlockSpec((N, LANE), lambda i: (0, 0)), pl.BlockSpec((1, N), lambda i: (0, 0))],
        out_specs=[pl.BlockSpec((ts, LANE), lambda i: (i, 0)), pl.BlockSpec((LANE, N), lambda i: (0, 0)), pl.BlockSpec((1, N), lambda i: (0, 0))],
        out_shape=[jax.ShapeDtypeStruct((S, LANE), BF16), jax.ShapeDtypeStruct((LANE, N), F32), jax.ShapeDtypeStruct((1, N), F32)],
        compiler_params=_cp(("arbitrary",)), name=name,
    )(dla, z, wgp, wgp_t, bgp)


def _bdot(a, b, ca, cb):
    return lax.dot_general(a.astype(BF16), b.astype(BF16), (((ca,), (cb,)), ((0,), (0,))), preferred_element_type=F32)


def _bdot_exact(a, b):
    return lax.dot_general(a, b, (((2,), (1,)), ((0,), (0,))), preferred_element_type=F32, precision=lax.Precision.HIGHEST)


def _tiles(ref, cpb):
    return jnp.stack([ref[pl.ds(c * CHUNK, CHUNK), pl.ds(h * LANE, LANE)] for c in range(cpb) for h in range(HEADS)])


def _tri_masks(n):
    ri = lax.broadcasted_iota(jnp.int32, (n, CHUNK, CHUNK), 1)
    ci = lax.broadcasted_iota(jnp.int32, (n, CHUNK, CHUNK), 2)
    return ri >= ci, (ri >= ci).astype(F32), (ri <= ci).astype(F32)


def _chunk_fwd_terms(q, k, la, tril):
    bc = _bdot_exact(tril, la)
    bend = jnp.sum(la, axis=1, keepdims=True)
    eb = jnp.exp(bc)
    enb = jnp.exp(-bc)
    ee = jnp.exp(bend - bc)
    qs = q * (DK ** -0.5)
    return bend, eb, enb, ee, qs * eb, qs * enb, k * enb, k * eb, k * ee


def _gla_fwd(z, la, gn, name):
    S = z.shape[0]
    W = HEADS * LANE
    tb = _div(S, 512, CHUNK)
    cpb = tb // CHUNK

    def body(q_ref, k_ref, v_ref, r_ref, la_ref, gn_ref, o_ref, sp_ref, y_ref, st):
        @pl.when(pl.program_id(0) == 0)
        def _():
            st[...] = jnp.zeros_like(st)

        tri, tril, _ = _tri_masks(cpb * HEADS)
        q, k, v, rv, lav = (_tiles(r, cpb) for r in (q_ref, k_ref, v_ref, r_ref, la_ref))
        bend, _, _, _, qf, qb, kb, kf, ke = _chunk_fwd_terms(q, k, lav, tril)
        att = jnp.where(tri, _bdot(qf, kb, 2, 2), _bdot(qb, kf, 2, 2))
        o_intra = _bdot(att, v, 2, 1)
        u = _bdot(v, ke, 1, 1)
        gdec = jnp.exp(bend)
        s_prev = [None] * (cpb * HEADS)
        for h in range(HEADS):
            s = st[h]
            for c in range(cpb):
                b = c * HEADS + h
                s_prev[b] = s
                s = s * gdec[b] + u[b]
            st[h] = s
        s_prev = jnp.stack(s_prev)
        o = o_intra + _bdot(qf, s_prev, 2, 2)
        rms = lax.rsqrt(jnp.mean(o * o, axis=-1, keepdims=True) + EPS)
        gn = jnp.stack([gn_ref[pl.ds(h, 1), :] for _ in range(cpb) for h in range(HEADS)])
        y = (o * rms * gn * (rv * _sigmoid(rv))).astype(BF16)
        for c in range(cpb):
            for h in range(HEADS):
                b = c * HEADS + h
                rows, ln = pl.ds(c * CHUNK, CHUNK), pl.ds(h * LANE, LANE)
                o_ref[rows, ln] = o[b]
                y_ref[rows, ln] = y[b]
                sp_ref[h, c] = s_prev[b]

    zb = lambda base: pl.BlockSpec((tb, W), lambda i: (i, base // W))
    hb_ = pl.BlockSpec((tb, W), lambda i: (i, 0))
    return pl.pallas_call(
        body, grid=(S // tb,),
        in_specs=[zb(ZC_Q), zb(ZC_K), zb(ZC_V), zb(ZC_R), hb_, pl.BlockSpec((HEADS, DV), lambda i: (0, 0))],
        out_specs=[hb_, pl.BlockSpec((HEADS, cpb, DV, DKP), lambda i: (0, i, 0, 0)), hb_],
        out_shape=[jax.ShapeDtypeStruct((S, D_GLA), F32), jax.ShapeDtypeStruct((HEADS, S // CHUNK, DV, DKP), F32),
                   jax.ShapeDtypeStruct((S, D_GLA), BF16)],
        scratch_shapes=[pltpu.VMEM((HEADS, DV, DKP), F32)],
        compiler_params=_cp(("arbitrary",)), name=name,
    )(z, z, z, z, la, gn)


def _gla_bwd(dycat, z, la, o_raw, sprev, gn, name):
    S = z.shape[0]
    W = HEADS * LANE
    tb = _div(S, 512, CHUNK)
    cpb = tb // CHUNK
    nb = S // tb

    def body(q_ref, k_ref, v_ref, r_ref, la_ref, o_ref, sp_ref, dy_ref, gn_ref, dz_ref, dla_ref, dgn_ref, dst):
        @pl.when(pl.program_id(0) == 0)
        def _():
            dst[...] = jnp.zeros_like(dst)
            dgn_ref[...] = jnp.zeros_like(dgn_ref)

        nt = cpb * HEADS
        tri, tril, triu = _tri_masks(nt)
        q, k, v, rv, lav, o, dy = (_tiles(r, cpb) for r in (q_ref, k_ref, v_ref, r_ref, la_ref, o_ref, dy_ref))
        bend, eb, enb, ee, qf, qb, kb, kf, ke = _chunk_fwd_terms(q, k, lav, tril)
        att = jnp.where(tri, _bdot(qf, kb, 2, 2), _bdot(qb, kf, 2, 2))
        s_prev = jnp.stack([sp_ref[h, c] for c in range(cpb) for h in range(HEADS)])
        gdec = jnp.exp(bend)
        gn = jnp.stack([gn_ref[pl.ds(h, 1), :] for _ in range(cpb) for h in range(HEADS)])
        rms = lax.rsqrt(jnp.mean(o * o, axis=-1, keepdims=True) + EPS)
        oh = o * rms
        sg = _sigmoid(rv)
        sr = rv * sg
        d_r = (dy * oh * gn * (sg * (1.0 + rv * (1.0 - sg)))).astype(BF16)
        dgn = jnp.sum(dy * sr * oh, axis=1, keepdims=True)
        w = dy * sr * gn
        do = rms * (w - oh * jnp.mean(w * oh, axis=-1, keepdims=True))
        p = _bdot(do, qf, 1, 1)
        ds = [None] * nt
        for h in range(HEADS):
            s = dst[h]
            for c in reversed(range(cpb)):
                b = c * HEADS + h
                ds[b] = s
                s = s * gdec[b] + p[b]
            dst[h] = s
            dgn_ref[pl.ds(h, 1), :] += sum(dgn[c * HEADS + h] for c in range(cpb))
        ds = jnp.stack(ds)
        datt = _bdot(do, v, 2, 2)
        daf = jnp.where(tri, datt, 0.0)
        dab = jnp.where(tri, 0.0, datt)
        d_v = (_bdot(att, do, 1, 1) + _bdot(ke, ds, 2, 2)).astype(BF16)
        dke = _bdot(v, ds, 2, 1)
        dqf = _bdot(daf, kb, 2, 1) + _bdot(do, s_prev, 2, 1)
        dkb = _bdot(daf, qf, 1, 1)
        dqb = _bdot(dab, kf, 2, 1)
        dkf = _bdot(dab, qb, 1, 1)
        dg = jnp.sum(ds * s_prev, axis=1, keepdims=True)
        d_q = ((dqf * eb + dqb * enb) * (DK ** -0.5)).astype(BF16)
        d_k = (dkb * enb + dkf * eb + dke * ee).astype(BF16)
        dbc = dqf * qf - dkb * kb - dqb * qb + dkf * kf - dke * ke
        dbend = jnp.sum(dke * ke, axis=1, keepdims=True) + dg * gdec
        dla = _bdot_exact(triu, dbc) + dbend
        for c in range(cpb):
            for h in range(HEADS):
                b = c * HEADS + h
                rows = pl.ds(c * CHUNK, CHUNK)
                for base, val in ((ZC_Q, d_q), (ZC_K, d_k), (ZC_V, d_v), (ZC_R, d_r)):
                    dz_ref[rows, pl.ds(base + h * LANE, LANE)] = val[b]
                dla_ref[rows, pl.ds(h * LANE, LANE)] = dla[b]

    zb = lambda base: pl.BlockSpec((tb, W), lambda i: (nb - 1 - i, base // W))
    hb_ = pl.BlockSpec((tb, W), lambda i: (nb - 1 - i, 0))
    return pl.pallas_call(
        body, grid=(nb,),
        in_specs=[zb(ZC_Q), zb(ZC_K), zb(ZC_V), zb(ZC_R), hb_, hb_,
                  pl.BlockSpec((HEADS, cpb, DV, DKP), lambda i: (0, nb - 1 - i, 0, 0)),
                  pl.BlockSpec((tb, W), lambda i: (nb - 1 - i, 1)),
                  pl.BlockSpec((HEADS, DV), lambda i: (0, 0))],
        out_specs=[pl.BlockSpec((tb, Z_GLA), lambda i: (nb - 1 - i, 0)), hb_, pl.BlockSpec((HEADS, DV), lambda i: (0, 0))],
        out_shape=[jax.ShapeDtypeStruct((S, Z_GLA), BF16), jax.ShapeDtypeStruct((S, HEADS * DKP), F32), jax.ShapeDtypeStruct((HEADS, DV), F32)],
        scratch_shapes=[pltpu.VMEM((HEADS, DV, DKP), F32)],
        compiler_params=_cp(("arbitrary",)), name=name,
    )(z, z, z, z, la, o_raw, sprev, dycat, gn)


def _mod_proj(c_all, w3, layer, b, name):
    B, D = c_all.shape
    N = w3.shape[2]
    tn = _div(N, 1024, LANE)

    def body(c_ref, w_ref, b_ref, o_ref):
        cv = c_ref[...]
        o_ref[...] = _dot(cv * _sigmoid(cv), w_ref[...]) + b_ref[...]

    return pl.pallas_call(
        body, grid=(N // tn,),
        in_specs=[pl.BlockSpec((B, D), lambda j: (0, 0)), pl.BlockSpec((None, D, tn), lambda j: (layer, 0, j)), pl.BlockSpec((1, tn), lambda j: (0, j))],
        out_specs=pl.BlockSpec((B, tn), lambda j: (0, j)), out_shape=jax.ShapeDtypeStruct((B, N), F32),
        compiler_params=_cp(("parallel",)), name=name,
    )(c_all, w3, b)


def _mod_wgrad(c_t, dm, name):
    D, B = c_t.shape
    N = dm.shape[1]
    tn = _div(N, 1024, LANE)

    def body(c_ref, d_ref, o_ref):
        cv = c_ref[...]
        ca = cv * _sigmoid(cv)
        acc = ca[:, 0:1] * d_ref[pl.ds(0, 1), :]
        for b in range(1, B):
            acc = acc + ca[:, b:b + 1] * d_ref[pl.ds(b, 1), :]
        o_ref[...] = acc

    return pl.pallas_call(
        body, grid=(N // tn,),
        in_specs=[pl.BlockSpec((D, B), lambda j: (0, 0)), pl.BlockSpec((B, tn), lambda j: (0, j))],
        out_specs=pl.BlockSpec((D, tn), lambda j: (0, j)), out_shape=jax.ShapeDtypeStruct((D, N), F32),
        compiler_params=_cp(("parallel",)), name=name,
    )(c_t, dm)


def _rowsum(xs, name):
    n, N = xs.shape
    tn = _div(N, 8192, LANE)

    def body(x_ref, o_ref):
        acc = x_ref[pl.ds(0, 1), :]
        for r in range(1, n):
            acc = acc + x_ref[pl.ds(r, 1), :]
        o_ref[...] = acc

    return pl.pallas_call(
        body, grid=(N // tn,), in_specs=[pl.BlockSpec((n, tn), lambda j: (0, j))],
        out_specs=pl.BlockSpec((1, tn), lambda j: (0, j)), out_shape=jax.ShapeDtypeStruct((1, N), F32),
        compiler_params=_cp(("parallel",)), name=name,
    )(xs)


def _adamw(w, g, m, v, name, copy_grad=False):
    R, C = w.shape
    tr = _div(R, max(8, (1 << 18) // C), 8)

    def body(w_ref, g_ref, m_ref, v_ref, d_ref, nm_ref, nv_ref, *g_out):
        gv = g_ref[...]
        if copy_grad:
            g_out[0][...] = gv
        mn = ADAM_B1 * m_ref[...] + (1.0 - ADAM_B1) * gv
        vn = ADAM_B2 * v_ref[...] + (1.0 - ADAM_B2) * (gv * gv)
        m_hat = mn / (1.0 - ADAM_B1 ** ADAM_STEP)
        v_hat = vn / (1.0 - ADAM_B2 ** ADAM_STEP)
        d_ref[...] = -ADAM_LR * (m_hat / (jnp.sqrt(v_hat) + ADAM_EPS) + ADAM_WD * w_ref[...])
        nm_ref[...] = mn
        nv_ref[...] = vn

    blk = pl.BlockSpec((tr, C), lambda i: (i, 0))
    os_ = jax.ShapeDtypeStruct((R, C), F32)
    n_out = 4 if copy_grad else 3
    return pl.pallas_call(
        body, grid=(R // tr,), in_specs=[blk] * 4, out_specs=[blk] * n_out, out_shape=[os_] * n_out,
        compiler_params=_cp(("parallel",)), name=name,
    )(w, g, m, v)


def _place():
    return lax.axis_index("x"), lax.axis_index("y"), lax.axis_index("c")


def _other_chips(x, y):
    return [(1 - x, y), (x, 1 - y), (1 - x, 1 - y)]


def _half(c, rows):
    return pl.ds(c * (rows // 2), rows // 2)


_ANY = pl.BlockSpec(memory_space=pl.ANY)


def _ag_small(v, name):
    r, n = v.shape

    def body(v_ref, o_ref, send_sems, recv_sems):
        x, y, c = _place()
        me = 4 * x + 2 * y + c
        o_ref[pl.ds(me, 1)] = v_ref[...][None]
        peers = [(x ^ (k >> 2), y ^ ((k >> 1) & 1), c ^ (k & 1)) for k in range(1, 8)]
        copies = []
        for k, peer in enumerate(peers):
            cp = pltpu.make_async_remote_copy(
                src_ref=v_ref, dst_ref=o_ref.at[me], send_sem=send_sems.at[k], recv_sem=recv_sems.at[k],
                device_id=peer, device_id_type=MESH)
            cp.start()
            copies.append(cp)
        for cp in copies:
            cp.wait()

    return pl.pallas_call(
        body, out_shape=jax.ShapeDtypeStruct((8, r, n), v.dtype),
        in_specs=[pl.BlockSpec(memory_space=pltpu.VMEM)], out_specs=pl.BlockSpec(memory_space=pltpu.VMEM),
        scratch_shapes=[pltpu.SemaphoreType.DMA((7,)), pltpu.SemaphoreType.DMA((7,))],
        compiler_params=pltpu.CompilerParams(vmem_limit_bytes=VMEM_LIMIT), name=name,
    )(v)


def _rs_sibling(gs, name):
    n = len(gs)

    def body(*refs):
        src, out = refs[:n], refs[n:2 * n]
        send_sems, recv_sems = refs[2 * n:]
        x, y, c = _place()
        copies = []
        for i in range(n):
            cp = pltpu.make_async_remote_copy(
                src_ref=src[i].at[:, _half(1 - c, src[i].shape[1])], dst_ref=out[i], send_sem=send_sems.at[i], recv_sem=recv_sems.at[i],
                device_id=(x, y, 1 - c), device_id_type=MESH)
            cp.start()
            copies.append(cp)
        for cp in copies:
            cp.wait()

    return pl.pallas_call(
        body, out_shape=[jax.ShapeDtypeStruct((N_CHIPS, g.shape[1] // 2, g.shape[2]), g.dtype) for g in gs],
        in_specs=[_ANY] * n, out_specs=[_ANY] * n,
        scratch_shapes=[pltpu.SemaphoreType.DMA((n,)), pltpu.SemaphoreType.DMA((n,))],
        compiler_params=pltpu.CompilerParams(has_side_effects=True), name=name,
    )(*gs)


def _rs_presum(g, sib, c_arr, name):
    ns, R, C = g.shape
    rh = R // 2
    tr = _div(rh, max(16, (1 << 19) // C), 16)
    nrb = rh // tr

    def body(c_ref, g_ref, s_ref, o_ref):
        o_ref[...] = (g_ref[...] + s_ref[...]).astype(BF16)

    return pl.pallas_call(
        body, out_shape=jax.ShapeDtypeStruct((ns, rh, C), BF16),
        grid_spec=pltpu.PrefetchScalarGridSpec(
            num_scalar_prefetch=1, grid=(ns, nrb),
            in_specs=[pl.BlockSpec((None, tr, C), lambda s, r, c_ref: (s, c_ref[0] * nrb + r, 0)),
                      pl.BlockSpec((None, tr, C), lambda s, r, c_ref: (s, r, 0))],
            out_specs=pl.BlockSpec((None, tr, C), lambda s, r, c_ref: (s, r, 0))),
        compiler_params=_cp(("parallel", "parallel")), name=name,
    )(c_arr, g, sib)


def _rs_sum(g, sib, recv, full, layer, sc_arr, name):
    ns, R, C = g.shape
    rh = R // 2
    tr = _div(rh, max(16, (1 << 18) // C), 16)
    nrb = rh // tr

    def body(sc_ref, g_ref, s_ref, r_ref, f_ref, o_ref):
        acc = g_ref[...] + s_ref[...]
        for j in range(3):
            acc = acc + r_ref[j].astype(F32)
        o_ref[...] = acc

    return pl.pallas_call(
        body, out_shape=jax.ShapeDtypeStruct(full.shape, F32),
        grid_spec=pltpu.PrefetchScalarGridSpec(
            num_scalar_prefetch=1, grid=(nrb,),
            in_specs=[pl.BlockSpec((None, tr, C), lambda r, sc: (sc[0], sc[1] * nrb + r, 0)),
                      pl.BlockSpec((None, tr, C), lambda r, sc: (sc[0], r, 0)),
                      pl.BlockSpec((3, tr, C), lambda r, sc: (0, r, 0)),
                      _ANY],
            out_specs=pl.BlockSpec((None, tr, C), lambda r, sc: (layer, sc[1] * nrb + r, 0))),
        input_output_aliases={4: 0},
        compiler_params=_cp(("parallel",)), name=name,
    )(sc_arr, g, sib, recv, full)


def _rs_share(fulls, layer, name):
    n = len(fulls)

    def body(*refs):
        src, out = refs[:n], refs[n:2 * n]
        send_sems, recv_sems = refs[2 * n:]
        x, y, c = _place()
        copies = []
        for i in range(n):
            rows = out[i].shape[1]
            cp = pltpu.make_async_remote_copy(
                src_ref=out[i].at[layer, _half(c, rows)], dst_ref=out[i].at[layer, _half(c, rows)],
                send_sem=send_sems.at[i], recv_sem=recv_sems.at[i], device_id=(x, y, 1 - c), device_id_type=MESH)
            cp.start()
            copies.append(cp)
        for cp in copies:
            cp.wait()

    return pl.pallas_call(
        body, out_shape=[jax.ShapeDtypeStruct(f.shape, f.dtype) for f in fulls],
        in_specs=[_ANY] * n, out_specs=[_ANY] * n, input_output_aliases={i: i for i in range(n)},
        scratch_shapes=[pltpu.SemaphoreType.DMA((n,)), pltpu.SemaphoreType.DMA((n,))],
        compiler_params=pltpu.CompilerParams(has_side_effects=True), name=name,
    )(*fulls)


_HBM = pl.BlockSpec(memory_space=pltpu.HBM)
_SEM = pl.BlockSpec(memory_space=pltpu.SEMAPHORE)
_EFFECT = pltpu.SideEffectType.DATAFLOW_SIDE_EFFECTING


def _in_hbm(a):
    return pltpu.with_memory_space_constraint(a, pltpu.HBM)


def _split_start(bufs, n_sem, copies_of, name):
    nb = len(bufs)

    def body(*refs):
        for cp in copies_of(refs[:nb], refs[nb], refs[nb + 1]):
            cp.start()
        refs[-1][...] = jnp.zeros_like(refs[-1])

    out = pl.pallas_call(
        body, name=name,
        out_shape=(pltpu.SemaphoreType.DMA((n_sem,)), pltpu.SemaphoreType.DMA((n_sem,)), *[pltpu.HBM(a.shape, a.dtype) for a in bufs],
                   jax.ShapeDtypeStruct((SUB, LANE), F32)),
        in_specs=[_HBM] * nb, out_specs=(_SEM, _SEM, *([_HBM] * nb), pl.BlockSpec(memory_space=pltpu.VMEM)),
        input_output_aliases={i: 2 + i for i in range(nb)},
        compiler_params=pltpu.CompilerParams(has_side_effects=_EFFECT),
    )(*[_in_hbm(a) for a in bufs])
    return out[0], out[1], list(out[2:2 + nb]), out[-1]


def _split_wait(send_sems, recv_sems, bufs, after, copies_of, name):
    nb = len(bufs)

    def body(*refs):
        for cp in copies_of(refs[:nb], refs[nb], refs[nb + 1]):
            cp.wait_send()
            cp.wait_recv()

    return list(pl.pallas_call(
        body, name=name, out_shape=[pltpu.HBM(a.shape, a.dtype) for a in bufs],
        in_specs=[_HBM] * nb + [_SEM, _SEM, _ANY], out_specs=[_HBM] * nb,
        input_output_aliases={i: i for i in range(nb)},
        compiler_params=pltpu.CompilerParams(has_side_effects=_EFFECT),
    )(*bufs, send_sems, recv_sems, after))


def _ag_half_copies(land, send_sems, recv_sems, landing_of_mine):
    x, y, c = _place()
    cps = []
    for j, (cx, cy) in enumerate(_other_chips(x, y)):
        for i in range(len(land)):
            rows = _half(c, land[i].shape[1])
            s = 2 * x + y if landing_of_mine else 2 * cx + cy
            cps.append(pltpu.make_async_remote_copy(
                src_ref=land[i].at[2 * x + y, rows], dst_ref=land[i].at[s, rows], send_sem=send_sems.at[3 * i + j], recv_sem=recv_sems.at[3 * i + j],
                device_id=(cx, cy, c), device_id_type=MESH))
    return cps


def _ag_starts(land, send_sems, recv_sems):
    return _ag_half_copies(land, send_sems, recv_sems, True)


def _ag_waits(land, send_sems, recv_sems):
    return _ag_half_copies(land, send_sems, recv_sems, False)


def _ag_finish(lands, name):
    n = len(lands)

    def body(*refs):
        land = refs[n:2 * n]
        send_sems, recv_sems = refs[2 * n:]
        x, y, c = _place()
        sibling = (x, y, 1 - c)

        def copy(k, i, s, h):
            blk = land[i].at[s, _half(h, land[i].shape[1])]
            return pltpu.make_async_remote_copy(
                src_ref=blk, dst_ref=blk, send_sem=send_sems.at[k], recv_sem=recv_sems.at[k], device_id=sibling, device_id_type=MESH)

        chips = _other_chips(x, y)
        passed = [copy(3 * i + j, i, 2 * cx + cy, c) for j, (cx, cy) in enumerate(chips) for i in range(n)]
        for cp in passed:
            cp.start()
        for j, (cx, cy) in enumerate(chips):
            for i in range(n):
                copy(3 * i + j, i, 2 * cx + cy, 1 - c).wait_recv()
        for cp in passed:
            cp.wait_send()

    return pl.pallas_call(
        body, out_shape=[jax.ShapeDtypeStruct(a.shape, a.dtype) for a in lands],
        in_specs=[_ANY] * n, out_specs=[_ANY] * n, input_output_aliases={i: i for i in range(n)},
        scratch_shapes=[pltpu.SemaphoreType.DMA((3 * n,)), pltpu.SemaphoreType.DMA((3 * n,))],
        compiler_params=pltpu.CompilerParams(has_side_effects=True), name=name,
    )(*lands)


def _rs_chip_copies(bufs, send_sems, recv_sems):
    n = len(bufs) // 2
    x, y, c = _place()
    return [pltpu.make_async_remote_copy(
        src_ref=bufs[i].at[2 * cx + cy], dst_ref=bufs[n + i].at[j], send_sem=send_sems.at[3 * i + j], recv_sem=recv_sems.at[3 * i + j],
        device_id=(cx, cy, c), device_id_type=MESH) for j, (cx, cy) in enumerate(_other_chips(x, y)) for i in range(n)]


def _pad_heads(w):
    lead = w.shape[:-1]
    w4 = w.reshape(*lead, HEADS, DK)
    w4 = jnp.pad(w4, [(0, 0)] * len(lead) + [(0, 0), (0, DKP - DK)])
    return w4.reshape(*lead, HEADS * DKP)


def _unpad_heads(w):
    lead = w.shape[:-1]
    return w.reshape(*lead, HEADS, DKP)[..., :DK].reshape(*lead, HEADS * DK)


def _mix_weight(win4, n_cols):
    D = win4.shape[1]
    w = jnp.transpose(win4[:, :, :n_cols], (1, 0, 2)).reshape(D, N_CHIPS * n_cols)
    o = 2 * D_CONV
    hk = HEADS * DK
    ab = w[:, :o]
    q = _pad_heads(w[:, o:o + hk])
    k = _pad_heads(w[:, o + hk:o + 2 * hk])
    vr = w[:, o + 2 * hk:o + 2 * hk + 2 * D_GLA]
    glr = jnp.pad(w[:, o + 2 * hk + 2 * D_GLA:], ((0, 0), (0, LANE - GATE_RANK)))
    return jnp.concatenate([q, k, vr, ab, glr], axis=1)


def _mix_weight_grad(dgla, dab, dglr, n_cols, n_pad):
    D = dab.shape[0]
    hkp = HEADS * DKP
    w = jnp.concatenate([dab, _unpad_heads(dgla[:, :hkp]), _unpad_heads(dgla[:, hkp:2 * hkp]), dgla[:, 2 * hkp:], dglr[:, :GATE_RANK]], axis=1)
    w = jnp.pad(w.reshape(D, N_CHIPS, n_cols), ((0, 0), (0, 0), (0, n_pad - n_cols)))
    return jnp.transpose(w, (1, 0, 2))


_ARG_NAMES = ['x', 'c', 'w_ada', 'b_ada', 'g_norm_ffn1', 'w_ffn1_in', 'w_ffn1_out', 'g_norm_mix', 'w_in', 'w_dw', 'b_dw', 'g_conv_ln', 'b_conv_ln', 'w_gate_up', 'b_gate', 'g_gla_norm', 'w_out', 'g_norm_ffn2', 'w_ffn2_in', 'w_ffn2_out', 'g_norm_final', 'w_ada_final', 'b_ada_final']
_WEIGHTS = _ARG_NAMES[2:]
_BIG = ('w_ffn1_in', 'w_ffn1_out', 'w_in', 'w_out', 'w_ffn2_in', 'w_ffn2_out')
_SMALL = ('g_norm_ffn1', 'g_norm_mix', 'w_dw', 'b_dw', 'g_conv_ln', 'b_conv_ln', 'w_gate_up', 'b_gate', 'g_gla_norm', 'g_norm_ffn2', 'g_norm_final')


def _ffn_fwd(x, h, gv, w4, wo, nxt, tag):
    z, act = _ffn_up(h, w4, f"ffn_up_{tag}")
    y, xn, *hn = _ffn_down(act, wo, x, gv, nxt, f"ffn_down_{tag}")
    return xn, (hn[0] if hn else None), y, (x, h, z, act)


def _ffn_bwd(dxn, dy, saved, g, scale, prev, w4, wo, tag):
    x, h, z, act = saved
    ns, _, C = w4.shape
    dz = _ffn_bwd_act(dy, wo, z, f"ffn_bwd_act_{tag}")
    dwo = _mm_tn(act, dy, f"dw_out_{tag}")
    dx, dsh, dsc, dg, *pv = _dh_normmod_bwd([(dz, w4, s, s, C) for s in range(ns)], x, dxn, g, scale, prev, f"ffn_dh_{tag}")
    dwi = _dw_ffn_in(h, dz, ns, f"dw_in_{tag}")
    return dx, pv, dict(dshift=dsh, dscale=dsc, dg=dg, dw_in=dwi, dw_out=dwo.reshape(N_CHIPS, -1, dwo.shape[1]))


def _mix_fwd(x, h, gv, wmix, w_dw, b_dw, g_ln, b_ln, wgp, bgp, gn, wout, nxt, tag):
    z = _mm([(h, wmix, 0)], F32, f"mix_in_{tag}")
    u, yc, yconv = _conv_fwd(z, w_dw, b_dw, g_ln, b_ln, f"conv_fwd_{tag}")
    la = _loggate(z, wgp, bgp, f"loggate_{tag}")
    o_raw, sprev, ygla = _gla_fwd(z, la, gn, f"gla_fwd_{tag}")
    y, xn, *hn = _mix_out(yconv, ygla, wout, x, gv, nxt, f"mix_out_{tag}")
    return xn, (hn[0] if hn else None), y, (x, h, z, u, yc, la, o_raw, sprev, yconv, ygla)


def _mix_bwd(dxn, dy, saved, g, scale, prev, wmix, w_dw, g_ln, b_ln, wgp, bgp, gn, wout, n_cols, n_pad, tag):
    x, h, z, u, yc, la, o_raw, sprev, yconv, ygla = saved
    dycat = _mm([(dy, wout, 0)], F32, f"mix_dycat_{tag}", nt=True)
    dwout = _mm_tn_two(yconv, ygla, dy, f"dw_mixout_{tag}")
    dab, dwdw, dbdw, dgln, dbln = _conv_bwd(dycat, z, u, yc, w_dw, g_ln, b_ln, f"conv_bwd_{tag}")
    dgla, dla, dgn = _gla_bwd(dycat, z, la, o_raw, sprev, gn, f"gla_bwd_{tag}")
    dglr, dwgp, dbgp = _loggate_bwd(dla, z, wgp, wgp.T, bgp, f"loggate_bwd_{tag}")
    dx, dsh, dsc, dg, *pv = _dh_normmod_bwd(
        [(dgla, wmix, 0, 0, Z_GLA), (dab, wmix, 0, ZC_A // (2 * D_CONV), 2 * D_CONV), (dglr, wmix, 0, ZC_G // LANE, LANE)],
        x, dxn, g, scale, prev, f"mix_dh_{tag}")
    dwin = _mix_weight_grad(_mm_tn(h, dgla, f"dw_mixin_gla_{tag}"), _mm_tn(h, dab, f"dw_mixin_conv_{tag}"), _mm_tn(h, dglr, f"dw_mixin_gate_{tag}"),
                            n_cols, n_pad)
    grads = dict(dshift=dsh, dscale=dsc, dg=dg, dw_in=dwin, dw_out=dwout.reshape(N_CHIPS, -1, dwout.shape[1]), dw_dw=dwdw, db_dw=dbdw,
                 dg_ln=dgln, db_ln=dbln, dw_gate=_unpad_heads(dwgp[:GATE_RANK]), db_gate=_unpad_heads(dbgp)[0], dgn=dgn)
    return dx, pv, grads


def kernel(x, c, w_ada, b_ada, g_norm_ffn1, w_ffn1_in, w_ffn1_out, g_norm_mix, w_in, w_dw, b_dw, g_conv_ln, b_conv_ln, w_gate_up, b_gate, g_gla_norm, w_out, g_norm_ffn2, w_ffn2_in, w_ffn2_out, g_norm_final, w_ada_final, b_ada_final, loss_target, m_w_ada, m_b_ada, m_g_norm_ffn1, m_w_ffn1_in, m_w_ffn1_out, m_g_norm_mix, m_w_in, m_w_dw, m_b_dw, m_g_conv_ln, m_b_conv_ln, m_w_gate_up, m_b_gate, m_g_gla_norm, m_w_out, m_g_norm_ffn2, m_w_ffn2_in, m_w_ffn2_out, m_g_norm_final, m_w_ada_final, m_b_ada_final, v_w_ada, v_b_ada, v_g_norm_ffn1, v_w_ffn1_in, v_w_ffn1_out, v_g_norm_mix, v_w_in, v_w_dw, v_b_dw, v_g_conv_ln, v_b_conv_ln, v_w_gate_up, v_b_gate, v_g_gla_norm, v_w_out, v_g_norm_ffn2, v_w_ffn2_in, v_w_ffn2_out, v_g_norm_final, v_w_ada_final, v_b_ada_final):
    given = dict(locals())
    W = {n: given[n] for n in _WEIGHTS}
    M1 = {n: given["m_" + n] for n in _WEIGHTS}
    M2 = {n: given["v_" + n] for n in _WEIGHTS}
    xs = x[0]
    tgt = loss_target[0]
    S, D = xs.shape
    L = w_ada.shape[0]
    xi, yi, ci = _place()
    s_me = 2 * xi + yi
    b_me = 4 * xi + 2 * yi + ci
    nsh = w_ada.shape[2]
    nfin = w_ada_final.shape[1]
    n_cols = w_in.shape[2]
    n_pad = -(-n_cols // LANE) * LANE

    def lands_of(l):
        shards = [W[n][l].astype(BF16) for n in _BIG]
        shards[2] = jnp.pad(shards[2], ((0, 0), (0, n_pad - n_cols)))
        return [lax.dynamic_update_index_in_dim(lax.empty((N_CHIPS,) + s.shape, BF16), s, s_me, 0) for s in shards]

    lands = {l: lands_of(l) for l in range(L)}
    ag_groups = [dict(l=0, items=[0, 1], need=0), dict(l=0, items=[2, 3, 4, 5], need=1)]
    ag_groups += [dict(l=l, items=list(range(len(_BIG))), need=3 * l) for l in range(1, L)]

    def ag_start(grp):
        bufs = [lands[grp["l"]][i] for i in grp["items"]]
        return _split_start(bufs, 3 * len(bufs), _ag_starts, f"ag_start_l{grp['l']}_{grp['items'][0]}")

    pend = ag_start(ag_groups[0])
    tok = pend[3][0, 0]

    c_all = _ag_small(c.reshape(8, D // 8) + tok, "ag_c").reshape(8, D)
    tok = None
    parts = [_mod_proj(c_all, w_ada, l, lax.dynamic_slice(b_ada, (l, s_me * nsh), (1, nsh)), f"mod_proj_{l}") for l in range(L)]
    parts.append(_mod_proj(c_all, w_ada_final[None], 0, lax.dynamic_slice(b_ada_final, (s_me * nfin,), (nfin,))[None], "mod_proj_final"))
    mod_all = _ag_small(jnp.concatenate(parts, axis=1), "ag_mod")
    mine = [lax.dynamic_index_in_dim(lax.dynamic_index_in_dim(mod_all, 2 * s + ci, 0, False), b_me, 0, False) for s in range(N_CHIPS)]
    mods = [jnp.concatenate([mine[s][l * nsh:(l + 1) * nsh] for s in range(N_CHIPS)]).reshape(N_MOD, 1, D) for l in range(L)]
    fmod = jnp.concatenate([mine[s][L * nsh:] for s in range(N_CHIPS)]).reshape(2, 1, D)

    tiny = jnp.concatenate([w_dw.reshape(-1), w_gate_up.reshape(-1)])
    tiny_all = _ag_small(jnp.pad(tiny, (0, (-tiny.shape[0]) % (8 * LANE))).reshape(8, -1), "ag_tiny").reshape(8, -1)
    n_dw = w_dw.size
    dw_parts = [lax.dynamic_index_in_dim(tiny_all, 2 * s + ci, 0, False) for s in range(N_CHIPS)]
    w_dw_full = jnp.concatenate([p[:n_dw].reshape(w_dw.shape) for p in dw_parts], axis=2)
    w_gu_full = jnp.concatenate([p[n_dw:n_dw + w_gate_up.size].reshape(w_gate_up.shape) for p in dw_parts], axis=2)

    def layer_weights(l, lands):
        wi1, wo1, win4, wout4, wi2, wo2 = lands
        return dict(
            wi1=wi1, wo1=wo1.reshape(-1, D), wi2=wi2, wo2=wo2.reshape(-1, D), wout=wout4.reshape(-1, D), wmix=_mix_weight(win4, n_cols),
            wgp=jnp.pad(_pad_heads(w_gu_full[l]), ((0, LANE - GATE_RANK), (0, 0))).astype(BF16), bgp=_pad_heads(b_gate[l])[None])

    gnorm = (g_norm_ffn1, g_norm_mix, g_norm_ffn2)
    subs = [dict(l=l, j=j, tag=f"{('ffn1', 'mix', 'ffn2')[j]}_l{l}", g=gnorm[j][l][None], shift=mods[l][3 * j], scale=mods[l][3 * j + 1],
                 gv=mods[l][3 * j + 2] * (1.0 if j == 1 else 0.5)) for l in range(L) for j in range(3)]
    gi = 0
    xcur = xs
    h = None
    for k, sb in enumerate(subs):
        l, j = sb["l"], sb["j"]
        if pend is not None and ag_groups[gi]["need"] == k:
            grp = ag_groups[gi]
            nm = f"l{grp['l']}_{grp['items'][0]}"
            after = xcur if k > 0 else sb["shift"]
            done = _ag_finish(_split_wait(pend[0], pend[1], pend[2], after, _ag_waits, f"ag_wait_{nm}"), f"ag_finish_{nm}")
            for i, a in zip(grp["items"], done):
                lands[grp["l"]][i] = a
            gi += 1
            pend = ag_start(ag_groups[gi]) if gi < len(ag_groups) else None
            tok = pend[3][0, 0] if pend is not None else None
        if h is None:
            h = _normmod(xcur, sb["g"] if tok is None else sb["g"] + tok, sb["shift"], sb["scale"], f"normmod_{sb['tag']}")
            tok = None
        d = layer_weights(l, lands[l])
        nxt = (subs[k + 1]["g"], subs[k + 1]["shift"], subs[k + 1]["scale"]) if k + 1 < len(subs) else None
        gv = sb["gv"] if tok is None else sb["gv"] + tok
        tok = None
        if j == 1:
            xcur, h, sb["y"], sb["saved"] = _mix_fwd(xcur, h, gv, d["wmix"], w_dw_full[l], b_dw[l][None], g_conv_ln[l][None],
                                                     b_conv_ln[l][None], d["wgp"], d["bgp"], g_gla_norm[l], d["wout"], nxt, sb["tag"])
        else:
            w4, wo = (d["wi1"], d["wo1"]) if j == 0 else (d["wi2"], d["wo2"])
            xcur, h, sb["y"], sb["saved"] = _ffn_fwd(xcur, h, gv, w4, wo, nxt, sb["tag"])
    lw = [layer_weights(l, lands[l]) for l in range(L)]

    c_arr = jnp.stack([ci]).astype(jnp.int32)
    sc_arr = jnp.stack([s_me, ci]).astype(jnp.int32)
    fulls = [lax.empty((L,) + ((W[n].shape[1], n_pad) if n == 'w_in' else W[n].shape[1:]), F32) for n in _BIG]

    def rs_begin(gs, items, l):
        nm = f"l{l}_{items[0]}"
        sibs = _rs_sibling(gs, f"rs_sibling_{nm}")
        return sibs, [_rs_presum(g, sb_, c_arr, f"rs_presum_{i}_l{l}") for i, g, sb_ in zip(items, gs, sibs)]

    def rs_end(gs, sibs, recvs, items, l):
        summed = [_rs_sum(g, sb_, rv, fulls[i], l, sc_arr, f"rs_sum_{i}_l{l}") for i, g, sb_, rv in zip(items, gs, sibs, recvs)]
        for i, f in zip(items, _rs_share(summed, l, f"rs_share_l{l}_{items[0]}")):
            fulls[i] = f

    def rs_start(gs, items, l):
        sibs, ps = rs_begin(gs, items, l)
        pend = _split_start(ps + [lax.empty((3,) + p.shape[1:], BF16) for p in ps], 3 * len(ps), _rs_chip_copies, f"rs_start_l{l}_{items[0]}")
        return dict(gs=gs, sibs=sibs, pend=pend, items=items, l=l)

    def rs_finish(fl, after):
        pend, n = fl["pend"], len(fl["gs"])
        bufs = _split_wait(pend[0], pend[1], pend[2], after, _rs_chip_copies, f"rs_wait_l{fl['l']}_{fl['items'][0]}")
        rs_end(fl["gs"], fl["sibs"], bufs[n:], fl["items"], fl["l"])

    dh, sq = _final_loss(xcur, g_norm_final[None], fmod[0], fmod[1], tgt)
    loss_part = 0.5 / D * jnp.sum(sq)
    dx, dfsh, dfsc, dgfin, dy, dgv = _normmod_bwd(xcur, dh, None, g_norm_final[None], fmod[1], (subs[-1]["y"], subs[-1]["gv"]), "normmod_bwd_final")
    G = {n: [None] * L for n in _SMALL}
    dmods = [None] * L
    in_flight = None
    tok = None
    for l in reversed(range(L)):
        gr = [None] * 3
        for j in reversed(range(3)):
            k = 3 * l + j
            sb, d = subs[k], lw[l]
            prev = (subs[k - 1]["y"], subs[k - 1]["gv"]) if k > 0 else None
            g_vec = sb["g"] if tok is None else sb["g"] + tok
            tok = None
            if j == 1:
                dx, pv, gr[j] = _mix_bwd(dx, dy, sb["saved"], g_vec, sb["scale"], prev, d["wmix"], w_dw_full[l], g_conv_ln[l][None], b_conv_ln[l][None],
                                         d["wgp"], d["bgp"], g_gla_norm[l], d["wout"], n_cols, n_pad, sb["tag"])
            else:
                w4, wo = (d["wi1"], d["wo1"]) if j == 0 else (d["wi2"], d["wo2"])
                dx, pv, gr[j] = _ffn_bwd(dx, dy, sb["saved"], g_vec, sb["scale"], prev, w4, wo, sb["tag"])
            gr[j]["dgv"] = dgv
            dy, dgv = pv if pv else (None, None)
            if j == 1 and in_flight is not None:
                rs_finish(in_flight, dx)
                in_flight = None
            if j == 1 and l == 0:
                in_flight = rs_start([gr[1]["dw_in"], gr[1]["dw_out"], gr[2]["dw_in"], gr[2]["dw_out"]], [2, 3, 4, 5], l)
                tok = in_flight["pend"][3][0, 0]
        g1, g2, g3 = gr
        if l > 0:
            in_flight = rs_start([g1["dw_in"], g1["dw_out"], g2["dw_in"], g2["dw_out"], g3["dw_in"], g3["dw_out"]], list(range(len(_BIG))), l)
            tok = in_flight["pend"][3][0, 0]
        else:
            last = rs_start([g1["dw_in"], g1["dw_out"]], [0, 1], l)
            rs_finish(in_flight, dx)
            in_flight = None
        dmods[l] = jnp.concatenate([g1["dshift"], g1["dscale"], 0.5 * g1["dgv"], g2["dshift"], g2["dscale"], g2["dgv"],
                                    g3["dshift"], g3["dscale"], 0.5 * g3["dgv"]], axis=1)[0]
        G["g_norm_ffn1"][l], G["g_norm_ffn2"][l], G["g_norm_mix"][l] = g1["dg"][0], g3["dg"][0], g2["dg"][0]
        G["w_dw"][l], G["b_dw"][l], G["g_conv_ln"][l], G["b_conv_ln"][l] = g2["dw_dw"], g2["db_dw"][0], g2["dg_ln"][0], g2["db_ln"][0]
        G["w_gate_up"][l], G["b_gate"][l], G["g_gla_norm"][l] = g2["dw_gate"], g2["db_gate"], g2["dgn"]
    grad_x = dx[None]
    gsm = {}

    small = [jnp.stack(G[n]).reshape(-1) for n in _SMALL if n != 'g_norm_final'] + [dgfin[0]]
    dmod_vec = jnp.concatenate(dmods + [dfsh[0], dfsc[0]])
    n_mod_vec = dmod_vec.shape[0]
    vec = jnp.concatenate([dmod_vec] + small + [loss_part[None]])
    n_vec = vec.shape[0]
    vec = jnp.pad(vec, (0, (-n_vec) % (8 * LANE)))
    vec_all = _ag_small(vec.reshape(8, -1), "ag_small_grads").reshape(8, -1)
    vec_sum = _rowsum(vec_all, "sum_small_grads")[0]
    loss = vec_sum[n_vec - 1]
    off = n_mod_vec
    for n in _SMALL:
        shp = {'w_dw': w_dw_full.shape, 'w_gate_up': w_gu_full.shape}.get(n, W[n].shape)
        cnt = 1
        for dd in shp:
            cnt *= dd
        gsm[n] = vec_sum[off:off + cnt].reshape(shp)
        off += cnt
    gsm['w_dw'] = lax.dynamic_slice_in_dim(gsm['w_dw'], s_me * w_dw.shape[2], w_dw.shape[2], 2)
    gsm['w_gate_up'] = lax.dynamic_slice_in_dim(gsm['w_gate_up'], s_me * w_gate_up.shape[2], w_gate_up.shape[2], 2)
    dmod_sum = vec_sum[:n_mod_vec]
    gsm['b_ada'] = dmod_sum[:L * N_MOD * D].reshape(L, N_MOD * D)
    gsm['b_ada_final'] = dmod_sum[L * N_MOD * D:]
    c_t = c_all.T
    dmod_rows = vec_all[:, :n_mod_vec]
    gsm['w_ada'] = jnp.stack([
        _mod_wgrad(c_t, lax.dynamic_slice_in_dim(dmod_rows, l * N_MOD * D + s_me * nsh, nsh, 1), f"dw_ada_{l}") for l in range(L)])
    gsm['w_ada_final'] = _mod_wgrad(c_t, lax.dynamic_slice_in_dim(dmod_rows, L * N_MOD * D + s_me * nfin, nfin, 1), "dw_ada_final")
    rs_finish(last, gsm['w_ada_final'])
    gsm.update({n: (f[:, :, :n_cols] if n == 'w_in' else f) for n, f in zip(_BIG, fulls)})

    outs = {}
    small_names = [n for n in _WEIGHTS if W[n].size < 65536]
    for n in _WEIGHTS:
        if n in small_names:
            continue
        shp = W[n].shape
        v2 = lambda a: a.reshape(-1, shp[-1])
        from_rs = n in _BIG and n != 'w_in'
        d_, m_, v_, *g_ = _adamw(v2(W[n]), v2(gsm[n]), v2(M1[n]), v2(M2[n]), f"adamw_{n}", copy_grad=from_rs)
        outs[n] = (d_.reshape(shp), m_.reshape(shp), v_.reshape(shp))
        if from_rs:
            gsm[n] = g_[0].reshape(shp)
    flat = lambda dct: jnp.concatenate([dct[n].reshape(-1) for n in small_names])
    n_small = sum(W[n].size for n in small_names)
    v2 = lambda a: jnp.pad(a, (0, (-n_small) % (8 * LANE))).reshape(-1, LANE)
    d_, m_, v_ = _adamw(v2(flat(W)), v2(flat(gsm)), v2(flat(M1)), v2(flat(M2)), "adamw_small")

    def unflat(a):
        res, o = {}, 0
        a = a.reshape(-1)
        for n in small_names:
            res[n] = a[o:o + W[n].size].reshape(W[n].shape)
            o += W[n].size
        return res

    for n, dd, mm, vv in zip(small_names, unflat(d_).values(), unflat(m_).values(), unflat(v_).values()):
        outs[n] = (dd, mm, vv)

    return (loss, grad_x, *[gsm[n] for n in _WEIGHTS], *[outs[n][0] for n in _WEIGHTS], *[outs[n][1] for n in _WEIGHTS], *[outs[n][2] for n in _WEIGHTS])
```

```python
import jax
import jax.numpy as jnp
from jax import lax
from jax.experimental import pallas as pl
from jax.experimental.pallas import tpu as pltpu

F32 = jnp.float32
BF16 = jnp.bfloat16

CHUNK = 64
HEADS = 4
DK = 64
DV = 128
DKP = 128
GATE_RANK = 16
GATE_TAU = 16.0
N_MOD = 9
EPS = 1e-6
ADAM_LR = 0.001
ADAM_B1 = 0.9
ADAM_B2 = 0.999
ADAM_EPS = 1e-08
ADAM_WD = 0.01
ADAM_STEP = 10

LANE = 128
HALO = 32
VMEM_LIMIT = 52 * 1024 * 1024
MESH = pl.DeviceIdType.MESH
N_CHIPS = 4

D_CONV = 512
D_GLA = HEADS * DV
ZC_Q = 0
ZC_K = ZC_Q + HEADS * DKP
ZC_V = ZC_K + HEADS * DKP
ZC_R = ZC_V + D_GLA
ZC_A = ZC_R + D_GLA
ZC_B = ZC_A + D_CONV
ZC_G = ZC_B + D_CONV
Z_COLS = ZC_G + LANE
Z_GLA = ZC_A


def _div(n, target, mult):
    best = None
    d = mult
    while d <= min(n, target):
        if n % d == 0:
            best = d
        d += mult
    return n if best is None else best


def _cp(sem=None, **kw):
    return pltpu.CompilerParams(dimension_semantics=sem, vmem_limit_bytes=VMEM_LIMIT, **kw)


def _resident(shape, index_map):
    return pl.BlockSpec(shape, index_map, pipeline_mode=pl.Buffered(1))


def _sigmoid(x):
    return 0.5 * jnp.tanh(0.5 * x) + 0.5


def _dot(a, b):
    return jnp.dot(a.astype(BF16), b.astype(BF16), preferred_element_type=F32)


def _dot_nt(a, b):
    return lax.dot_general(a.astype(BF16), b.astype(BF16), (((1,), (1,)), ((), ())), preferred_element_type=F32)


def _dot_tn(a, b):
    return lax.dot_general(a.astype(BF16), b.astype(BF16), (((0,), (0,)), ((), ())), preferred_element_type=F32)


def _dot_exact(a, b):
    return jnp.dot(a, b, preferred_element_type=F32, precision=lax.Precision.HIGHEST)


def _normmod(x, g, shift, scale, name):
    S, D = x.shape
    tm = _div(S, 512, 8)

    def body(x_ref, g_ref, sh_ref, sc_ref, o_ref):
        xv = x_ref[...]
        r = lax.rsqrt(jnp.mean(xv * xv, axis=-1, keepdims=True) + EPS)
        o_ref[...] = ((xv * r) * g_ref[...] * (1.0 + sc_ref[...]) + sh_ref[...]).astype(o_ref.dtype)

    row = pl.BlockSpec((tm, D), lambda i: (i, 0))
    vec = pl.BlockSpec((1, D), lambda i: (0, 0))
    return pl.pallas_call(
        body, grid=(S // tm,), in_specs=[row, vec, vec, vec], out_specs=row,
        out_shape=jax.ShapeDtypeStruct((S, D), BF16), compiler_params=_cp(("parallel",)), name=name,
    )(x, g, shift, scale)


def _loss_head(x, g, shift, scale, tgt, prev):
    S, D = x.shape
    tm = _div(S, 512, 8)

    def body(x_ref, g_ref, sh_ref, sc_ref, t_ref, y_ref, gvp_ref, sq_ref, dx_ref, dsh_ref, dsc_ref, dg_ref, dy_ref, dgv_ref):
        sums = (sq_ref, dsh_ref, dsc_ref, dg_ref, dgv_ref)

        @pl.when(pl.program_id(0) == 0)
        def _():
            for o in sums:
                o[...] = jnp.zeros_like(o)

        xv = x_ref[...]
        e = _norm_rows(xv, g_ref[...], sh_ref[...], sc_ref[...]) - t_ref[...]
        dx, dsh, dsc, dg, dy, dgv = _normmod_bwd_rows(xv, e * (1.0 / D), None, g_ref[...], sc_ref[...], (y_ref[...], gvp_ref[...]))
        dx_ref[...] = dx
        dy_ref[...] = dy
        for o, v in zip(sums, (jnp.sum(e * e, axis=0, keepdims=True), dsh, dsc, dg, dgv)):
            o[...] += v

    row = pl.BlockSpec((tm, D), lambda i: (i, 0))
    vec = pl.BlockSpec((1, D), lambda i: (0, 0))
    vs = jax.ShapeDtypeStruct((1, D), F32)
    return pl.pallas_call(
        body, grid=(S // tm,), in_specs=[row, vec, vec, vec, row, row, vec], out_specs=[vec, row, vec, vec, vec, row, vec],
        out_shape=[vs, jax.ShapeDtypeStruct((S, D), F32), vs, vs, vs, jax.ShapeDtypeStruct((S, D), BF16), vs],
        compiler_params=_cp(("arbitrary",)), name="loss_head",
    )(x, g, shift, scale, tgt, *prev)


def _normmod_bwd_rows(xv, dh, dres, gv, sc, prev):
    r = lax.rsqrt(jnp.mean(xv * xv, axis=-1, keepdims=True) + EPS)
    xh = xv * r
    dsh = jnp.sum(dh, axis=0, keepdims=True)
    dsc = jnp.sum(dh * (xh * gv), axis=0, keepdims=True)
    dn = dh * (1.0 + sc)
    dg = jnp.sum(dn * xh, axis=0, keepdims=True)
    dxh = dn * gv
    dx = r * (dxh - xh * jnp.mean(dxh * xh, axis=-1, keepdims=True))
    if dres is not None:
        dx = dx + dres
    if prev is None:
        return dx, dsh, dsc, dg
    y, gvp = prev
    return dx, dsh, dsc, dg, (gvp * dx).astype(BF16), jnp.sum(dx * y, axis=0, keepdims=True)


def _mm(pairs, out_dtype, name, nt=False):
    M = pairs[0][0].shape[0]
    N = pairs[0][1].shape[0] if nt else pairs[0][1].shape[1]
    ktot = sum(a.shape[1] for a, _, _ in pairs)
    tm = _div(M, 512 if ktot <= 4096 else 256, 8)
    n = len(pairs)

    def body(*refs):
        o_ref = refs[2 * n]
        dot = _dot_nt if nt else _dot
        acc = dot(refs[0][...], refs[1][...])
        for p in range(1, n):
            acc = acc + dot(refs[2 * p][...], refs[2 * p + 1][...])
        o_ref[...] = acc.astype(o_ref.dtype)

    ins, args = [], []
    for a, b, blk in pairs:
        k = a.shape[1]
        ins.append(pl.BlockSpec((tm, k), lambda i: (i, 0)))
        ins.append(_resident((N, k), lambda i, blk=blk: (0, blk)) if nt else _resident((k, N), lambda i: (0, 0)))
        args += [a, b]
    return pl.pallas_call(
        body, grid=(M // tm,), in_specs=ins, out_specs=pl.BlockSpec((tm, N), lambda i: (i, 0)),
        out_shape=jax.ShapeDtypeStruct((M, N), out_dtype), compiler_params=_cp(("parallel",)), name=name,
    )(*args)


TN_ROWS = 2048


def _mm_tn(a, g, name):
    S, Ka = a.shape
    N = g.shape[1]
    tk = _div(Ka, 1408, LANE)
    tn = _div(N, 1408, LANE)
    ts = _div(S, TN_ROWS, 8)

    def body(a_ref, g_ref, o_ref):
        @pl.when(pl.program_id(2) == 0)
        def _():
            o_ref[...] = jnp.zeros_like(o_ref)

        o_ref[...] += _dot_tn(a_ref[...], g_ref[...])

    return pl.pallas_call(
        body, grid=(Ka // tk, N // tn, S // ts),
        in_specs=[pl.BlockSpec((ts, tk), lambda i, j, s: (s, i)), pl.BlockSpec((ts, tn), lambda i, j, s: (s, j))],
        out_specs=pl.BlockSpec((tk, tn), lambda i, j, s: (i, j)),
        out_shape=jax.ShapeDtypeStruct((Ka, N), F32),
        compiler_params=_cp(("parallel", "parallel", "arbitrary")), name=name,
    )(a, g)


def _mm_tn_two(a0, a1, g, name):
    S, K = a0.shape
    N = g.shape[1]
    ts = _div(S, TN_ROWS, 8)

    def body(a0_ref, a1_ref, g_ref, o_ref):
        i = pl.program_id(0)

        @pl.when(pl.program_id(1) == 0)
        def _():
            o_ref[...] = jnp.zeros_like(o_ref)

        @pl.when(i == 0)
        def _():
            o_ref[...] += _dot_tn(a0_ref[...], g_ref[...])

        @pl.when(i == 1)
        def _():
            o_ref[...] += _dot_tn(a1_ref[...], g_ref[...])

    return pl.pallas_call(
        body, grid=(2, S // ts),
        in_specs=[pl.BlockSpec((ts, K), lambda i, s: (jnp.where(i == 0, s, 0), 0)),
                  pl.BlockSpec((ts, K), lambda i, s: (jnp.where(i == 1, s, 0), 0)),
                  pl.BlockSpec((ts, N), lambda i, s: (s, 0))],
        out_specs=pl.BlockSpec((K, N), lambda i, s: (i, 0)),
        out_shape=jax.ShapeDtypeStruct((2 * K, N), F32),
        compiler_params=_cp(("parallel", "arbitrary")), name=name,
    )(a0, a1, g)


def _swiglu(gt, up):
    return gt * _sigmoid(gt) * up


def _ffn_up(h, w4, name):
    S, D = h.shape
    ns, _, C = w4.shape
    hs = ns // 2
    tm = _div(S, 256, 8)

    def body(h_ref, w_ref, z_ref, a_ref):
        hv = h_ref[...]
        for s in range(hs):
            gt = _dot(hv, w_ref[s])
            up = _dot(hv, w_ref[hs + s])
            sg = _sigmoid(gt)
            silu = gt * sg
            z_ref[:, s * C:(s + 1) * C] = (up * (sg * (1.0 + gt * (1.0 - sg)))).astype(BF16)
            z_ref[:, (hs + s) * C:(hs + s + 1) * C] = silu.astype(BF16)
            a_ref[:, s * C:(s + 1) * C] = (silu * up).astype(BF16)

    return pl.pallas_call(
        body, grid=(S // tm,), in_specs=[pl.BlockSpec((tm, D), lambda i: (i, 0)), _resident((ns, D, C), lambda i: (0, 0, 0))],
        out_specs=[pl.BlockSpec((tm, ns * C), lambda i: (i, 0)), pl.BlockSpec((tm, hs * C), lambda i: (i, 0))],
        out_shape=[jax.ShapeDtypeStruct((S, ns * C), BF16), jax.ShapeDtypeStruct((S, hs * C), BF16)],
        compiler_params=_cp(("parallel",)), name=name,
    )(h, w4)


def _norm_rows(xv, g, shift, scale):
    r = lax.rsqrt(jnp.mean(xv * xv, axis=-1, keepdims=True) + EPS)
    return (xv * r) * g * (1.0 + scale) + shift


def _resid_outputs(y, x_ref, gv_ref, nxt_refs, out_refs):
    out_refs[0][...] = y
    xn = x_ref[...] + gv_ref[...] * y
    out_refs[1][...] = xn
    if nxt_refs:
        out_refs[2][...] = _norm_rows(xn, nxt_refs[0][...], nxt_refs[1][...], nxt_refs[2][...]).astype(BF16)


def _ffn_down(act, wo, x, gv, nxt, name):
    S = act.shape[0]
    Fd, D = wo.shape
    tm = _div(S, 512, 8)
    nn = 3 if nxt else 0

    def body(a_ref, w_ref, x_ref, gv_ref, *rest):
        _resid_outputs(_dot(a_ref[...], w_ref[...]), x_ref, gv_ref, rest[:nn], rest[nn:])

    row = pl.BlockSpec((tm, D), lambda i: (i, 0))
    vec = pl.BlockSpec((1, D), lambda i: (0, 0))
    os_ = jax.ShapeDtypeStruct((S, D), F32)
    return pl.pallas_call(
        body, grid=(S // tm,),
        in_specs=[pl.BlockSpec((tm, Fd), lambda i: (i, 0)), _resident((Fd, D), lambda i: (0, 0)), row, vec] + [vec] * nn,
        out_specs=[row, row] + [row] * (nn // 3), out_shape=[os_, os_] + [jax.ShapeDtypeStruct((S, D), BF16)] * (nn // 3),
        compiler_params=_cp(("parallel",)), name=name,
    )(act, wo, x, gv, *(nxt or ()))


def _dw_ffn_in(h, dz, ns, name):
    S, D = h.shape
    C = dz.shape[1] // ns
    ts = _div(S, TN_ROWS, 8)

    def body(h_ref, g_ref, o_ref):
        @pl.when(pl.program_id(1) == 0)
        def _():
            o_ref[...] = jnp.zeros_like(o_ref)

        o_ref[...] += _dot_tn(h_ref[...], g_ref[...])

    return pl.pallas_call(
        body, grid=(ns, S // ts),
        in_specs=[pl.BlockSpec((ts, D), lambda j, s: (s, 0)), pl.BlockSpec((ts, C), lambda j, s: (s, j))],
        out_specs=pl.BlockSpec((None, D, C), lambda j, s: (j, 0, 0)), out_shape=jax.ShapeDtypeStruct((ns, D, C), F32),
        compiler_params=_cp(("parallel", "arbitrary")), name=name,
    )(h, dz)


def _dh_normmod_bwd(pairs, x, dres, g, scale, prev, name):
    S, D = x.shape
    tm = _div(S, 256, 8)
    n = len(pairs)
    with_prev = prev is not None

    def body(*refs):
        refs = list(refs)
        mm = refs[:2 * n]
        x_ref, dr_ref, g_ref, sc_ref = refs[2 * n:2 * n + 4]
        outs = refs[2 * n + 4 + 2 * with_prev:]

        @pl.when(pl.program_id(0) == 0)
        def _():
            for o in outs[1:4] + outs[5:]:
                o[...] = jnp.zeros_like(o)

        dh = _dot_nt(mm[0][...], mm[1][...])
        for p in range(1, n):
            dh = dh + _dot_nt(mm[2 * p][...], mm[2 * p + 1][...])
        pv = (refs[2 * n + 4][...], refs[2 * n + 5][...]) if with_prev else None
        res = _normmod_bwd_rows(x_ref[...], dh, dr_ref[...], g_ref[...], sc_ref[...], pv)
        outs[0][...] = res[0]
        for o, v in zip(outs[1:4], res[1:4]):
            o[...] += v
        if with_prev:
            outs[4][...] = res[4]
            outs[5][...] += res[5]

    row = pl.BlockSpec((tm, D), lambda i: (i, 0))
    vec = pl.BlockSpec((1, D), lambda i: (0, 0))
    ins, args = [], []
    for a, b, a_blk, b_blk, k in pairs:
        ins.append(pl.BlockSpec((tm, k), lambda i, a_blk=a_blk: (i, a_blk)))
        ins.append(_resident((None, D, k), lambda i, b_blk=b_blk: (b_blk, 0, 0)) if b.ndim == 3 else _resident((D, k), lambda i, b_blk=b_blk: (0, b_blk)))
        args += [a, b]
    ins += [row, row, vec, vec] + [row, vec] * with_prev
    args += [x, dres, g, scale] + (list(prev) if with_prev else [])
    vs = jax.ShapeDtypeStruct((1, D), F32)
    return pl.pallas_call(
        body, grid=(S // tm,), in_specs=ins, out_specs=[row, vec, vec, vec] + [row, vec] * with_prev,
        out_shape=[jax.ShapeDtypeStruct((S, D), F32), vs, vs, vs] + [jax.ShapeDtypeStruct((S, D), BF16), vs] * with_prev,
        compiler_params=_cp(("arbitrary",)), name=name,
    )(*args)


def _ffn_bwd_core(dy, wo, z, w4, x, dres, g, scale, prev, name):
    S, D = x.shape
    Fd = wo.shape[0]
    ns, _, C = w4.shape
    tm = _div(S, 256, 8)
    with_prev = prev is not None

    def body(dy_ref, wo_ref, q_ref, p_ref, w4_ref, x_ref, dr_ref, g_ref, sc_ref, *rest):
        outs = rest[2 * with_prev:]
        dz_ref, outs = outs[0], outs[1:]

        @pl.when(pl.program_id(0) == 0)
        def _():
            for o in outs[1:4] + outs[5:]:
                o[...] = jnp.zeros_like(o)

        da = _dot_nt(dy_ref[...], wo_ref[...])
        dz_ref[:, :Fd] = (da * q_ref[...].astype(F32)).astype(BF16)
        dz_ref[:, Fd:] = (da * p_ref[...].astype(F32)).astype(BF16)
        dh = _dot_nt(dz_ref[:, 0:C], w4_ref[0])
        for s in range(1, ns):
            dh = dh + _dot_nt(dz_ref[:, s * C:(s + 1) * C], w4_ref[s])
        pv = (rest[0][...], rest[1][...]) if with_prev else None
        res = _normmod_bwd_rows(x_ref[...], dh, dr_ref[...], g_ref[...], sc_ref[...], pv)
        outs[0][...] = res[0]
        for o, v in zip(outs[1:4], res[1:4]):
            o[...] += v
        if with_prev:
            outs[4][...] = res[4]
            outs[5][...] += res[5]

    row = pl.BlockSpec((tm, D), lambda i: (i, 0))
    vec = pl.BlockSpec((1, D), lambda i: (0, 0))
    wide = pl.BlockSpec((tm, 2 * Fd), lambda i: (i, 0))
    vs = jax.ShapeDtypeStruct((1, D), F32)
    return pl.pallas_call(
        body, grid=(S // tm,),
        in_specs=[row, _resident((Fd, D), lambda i: (0, 0)), pl.BlockSpec((tm, Fd), lambda i: (i, 0)), pl.BlockSpec((tm, Fd), lambda i: (i, 1)),
                  _resident((ns, D, C), lambda i: (0, 0, 0)), row, row, vec, vec] + [row, vec] * with_prev,
        out_specs=[wide, row, vec, vec, vec] + [row, vec] * with_prev,
        out_shape=[jax.ShapeDtypeStruct((S, 2 * Fd), BF16), jax.ShapeDtypeStruct((S, D), F32), vs, vs, vs] + [jax.ShapeDtypeStruct((S, D), BF16), vs] * with_prev,
        compiler_params=_cp(("arbitrary",)), name=name,
    )(dy, wo, z, z, w4, x, dres, g, scale, *(prev or ()))


def _mix_out(yconv, ygla, wout, x, gv, nxt, name):
    S, Kc = yconv.shape
    Kg = ygla.shape[1]
    D = wout.shape[1]
    tm = _div(S, 512, 8)
    nn = 3 if nxt else 0

    def body(a_ref, b_ref, w_ref, x_ref, gv_ref, *rest):
        y = _dot(a_ref[...], w_ref[0:Kc, :]) + _dot(b_ref[...], w_ref[Kc:Kc + Kg, :])
        _resid_outputs(y, x_ref, gv_ref, rest[:nn], rest[nn:])

    row = pl.BlockSpec((tm, D), lambda i: (i, 0))
    vec = pl.BlockSpec((1, D), lambda i: (0, 0))
    os_ = jax.ShapeDtypeStruct((S, D), F32)
    return pl.pallas_call(
        body, grid=(S // tm,),
        in_specs=[pl.BlockSpec((tm, Kc), lambda i: (i, 0)), pl.BlockSpec((tm, Kg), lambda i: (i, 0)), _resident((Kc + Kg, D), lambda i: (0, 0)), row,
                  vec] + [vec] * nn,
        out_specs=[row, row] + [row] * (nn // 3), out_shape=[os_, os_] + [jax.ShapeDtypeStruct((S, D), BF16)] * (nn // 3),
        compiler_params=_cp(("parallel",)), name=name,
    )(yconv, ygla, wout, x, gv, *(nxt or ()))


def _ln_parts(yc, g, b):
    mu = jnp.mean(yc, axis=-1, keepdims=True)
    xc = yc - mu
    rs = lax.rsqrt(jnp.mean(xc * xc, axis=-1, keepdims=True) + EPS)
    xh = xc * rs
    return xh, rs, xh * g + b


SUB = 8
CONV_ROWS = 32


def _shifted_copies(ext8, rows):
    for b in range(1, SUB):
        ext8[b, pl.ds(0, rows - SUB), :] = ext8[0, pl.ds(b, rows - SUB), :]


def _tap(o):
    return o % SUB, o - o % SUB


def _conv_fwd(z, w_dw, b_dw, g_ln, b_ln, name):
    S = z.shape[0]
    W, C = w_dw.shape
    ts = _div(S, 512, HALO)
    hb = ts // HALO
    off = HALO - (W - 1)
    ca, cb = ZC_A // C, ZC_B // C
    rb = CONV_ROWS

    def body(a_ref, b_ref, pa_ref, pb_ref, w_ref, bd_ref, g_ref, bl_ref, u_ref, yc_ref, o_ref, ext8):
        keep = (pl.program_id(0) > 0).astype(F32)
        u = a_ref[...] * _sigmoid(b_ref[...])
        ext8[0, pl.ds(0, HALO), :] = pa_ref[...] * _sigmoid(pb_ref[...]) * keep
        ext8[0, pl.ds(HALO, ts), :] = u
        u_ref[...] = u
        _shifted_copies(ext8, ts + HALO)

        def sub(i, carry):
            r0 = pl.multiple_of(i * rb, rb)
            acc = jnp.zeros((rb, C), F32)
            for j in range(W):
                b, a = _tap(off + j)
                acc = acc + w_ref[pl.ds(j, 1), :] * ext8[b, pl.ds(r0 + a, rb), :]
            yc = acc + bd_ref[...]
            yc_ref[pl.ds(r0, rb), :] = yc
            _, _, ln = _ln_parts(yc, g_ref[...], bl_ref[...])
            o_ref[pl.ds(r0, rb), :] = (ln * _sigmoid(ln)).astype(BF16)
            return carry

        lax.fori_loop(0, ts // rb, sub, 0)

    cur = lambda col: pl.BlockSpec((ts, C), lambda i: (i, col))
    prev = lambda col: pl.BlockSpec((HALO, C), lambda i: (jnp.maximum(i * hb - 1, 0), col))
    vec = pl.BlockSpec((1, C), lambda i: (0, 0))
    row = pl.BlockSpec((ts, C), lambda i: (i, 0))
    fs = jax.ShapeDtypeStruct((S, C), F32)
    return pl.pallas_call(
        body, grid=(S // ts,),
        in_specs=[cur(ca), cur(cb), prev(ca), prev(cb), pl.BlockSpec((W, C), lambda i: (0, 0)), vec, vec, vec],
        out_specs=[row, row, row], out_shape=[fs, fs, jax.ShapeDtypeStruct((S, C), BF16)],
        scratch_shapes=[pltpu.VMEM((SUB, ts + HALO, C), F32)],
        compiler_params=_cp(("parallel",)), name=name,
    )(z, z, z, z, w_dw, b_dw, g_ln, b_ln)


def _conv_bwd(dycat, z, u, yc, w_dw, g_ln, b_ln, name):
    S = z.shape[0]
    W, C = w_dw.shape
    ts = _div(S, 512, HALO)
    hb = ts // HALO
    nblk = S // ts
    off = HALO - (W - 1)
    ca, cb = ZC_A // C, ZC_B // C
    rb = CONV_ROWS

    def ln_silu_bwd(dy, ycv, g, b):
        xh, rs, ln = _ln_parts(ycv, g, b)
        sl = _sigmoid(ln)
        dln = dy * (sl * (1.0 + ln * (1.0 - sl)))
        dxh = dln * g
        dyc = rs * (dxh - jnp.mean(dxh, axis=-1, keepdims=True) - xh * jnp.mean(dxh * xh, axis=-1, keepdims=True))
        return dyc, dln, xh

    def body(dy_ref, ndy_ref, yc_ref, nyc_ref, u_ref, pu_ref, a_ref, b_ref, w_ref, g_ref, bl_ref,
             dab_ref, dw_ref, dbd_ref, dg_ref, dbl_ref, uext8, dext8, dwacc):
        i = pl.program_id(0)

        @pl.when(i == 0)
        def _():
            dwacc[...] = jnp.zeros_like(dwacc)
            dbd_ref[...] = jnp.zeros_like(dbd_ref)
            dg_ref[...] = jnp.zeros_like(dg_ref)
            dbl_ref[...] = jnp.zeros_like(dbl_ref)

        g = g_ref[...]
        bl = bl_ref[...]
        dyc, dln, xh = ln_silu_bwd(dy_ref[...], yc_ref[...], g, bl)
        ndyc, _, _ = ln_silu_bwd(ndy_ref[...], nyc_ref[...], g, bl)
        dg_ref[...] += jnp.sum(dln * xh, axis=0, keepdims=True)
        dbl_ref[...] += jnp.sum(dln, axis=0, keepdims=True)
        dbd_ref[...] += jnp.sum(dyc, axis=0, keepdims=True)
        dext8[0, pl.ds(0, ts), :] = dyc
        dext8[0, pl.ds(ts, HALO), :] = ndyc * (i < nblk - 1).astype(F32)
        uext8[0, pl.ds(0, HALO), :] = pu_ref[...] * (i > 0).astype(F32)
        uext8[0, pl.ds(HALO, ts), :] = u_ref[...]
        _shifted_copies(dext8, ts + HALO)
        _shifted_copies(uext8, ts + HALO)

        def sub(k, carry):
            r0 = pl.multiple_of(k * rb, rb)
            rows = pl.ds(r0, rb)
            dyt = dext8[0, rows, :]
            du = jnp.zeros((rb, C), F32)
            for j in range(W):
                b, a = _tap(W - 1 - j)
                du = du + w_ref[pl.ds(j, 1), :] * dext8[b, pl.ds(r0 + a, rb), :]
                b, a = _tap(off + j)
                p = dyt * uext8[b, pl.ds(r0 + a, rb), :]
                part = p[0:SUB]
                for q in range(1, rb // SUB):
                    part = part + p[q * SUB:(q + 1) * SUB]
                dwacc[j] += part
            sb = _sigmoid(b_ref[rows, :])
            dab_ref[rows, 0:C] = (du * sb).astype(BF16)
            dab_ref[rows, C:2 * C] = (du * a_ref[rows, :] * sb * (1.0 - sb)).astype(BF16)
            return carry

        lax.fori_loop(0, ts // rb, sub, 0)

        @pl.when(i == nblk - 1)
        def _():
            for j in range(W):
                dw_ref[pl.ds(j, 1), :] = jnp.sum(dwacc[j], axis=0, keepdims=True)

    row = pl.BlockSpec((ts, C), lambda i: (i, 0))
    nxt = pl.BlockSpec((HALO, C), lambda i: (jnp.minimum((i + 1) * hb, S // HALO - 1), 0))
    prv = pl.BlockSpec((HALO, C), lambda i: (jnp.maximum(i * hb - 1, 0), 0))
    vec = pl.BlockSpec((1, C), lambda i: (0, 0))
    wsp = pl.BlockSpec((W, C), lambda i: (0, 0))
    vs = jax.ShapeDtypeStruct((1, C), F32)
    return pl.pallas_call(
        body, grid=(nblk,),
        in_specs=[row, nxt, row, nxt, row, prv, pl.BlockSpec((ts, C), lambda i: (i, ca)), pl.BlockSpec((ts, C), lambda i: (i, cb)), wsp, vec, vec],
        out_specs=[pl.BlockSpec((ts, 2 * C), lambda i: (i, 0)), wsp, vec, vec, vec],
        out_shape=[jax.ShapeDtypeStruct((S, 2 * C), BF16), jax.ShapeDtypeStruct((W, C), F32), vs, vs, vs],
        scratch_shapes=[pltpu.VMEM((SUB, ts + HALO, C), F32), pltpu.VMEM((SUB, ts + HALO, C), F32), pltpu.VMEM((W, SUB, C), F32)],
        compiler_params=_cp(("arbitrary",)), name=name,
    )(dycat, dycat, yc, yc, u, u, z, z, w_dw, g_ln, b_ln)


def _log_gate(zg):
    return (jnp.minimum(zg, 0.0) - jnp.log(1.0 + jnp.exp(-jnp.abs(zg)))) * (1.0 / GATE_TAU)


def _loggate(z, wgp, bgp, name):
    S = z.shape[0]
    N = wgp.shape[1]
    ts = _div(S, 512, 8)

    def body(g_ref, w_ref, b_ref, o_ref):
        o_ref[...] = _log_gate(_dot(g_ref[...], w_ref[...]) + b_ref[...])

    return pl.pallas_call(
        body, grid=(S // ts,),
        in_specs=[pl.BlockSpec((ts, LANE), lambda i: (i, ZC_G // LANE)), pl.BlockSpec((LANE, N), lambda i: (0, 0)), pl.BlockSpec((1, N), lambda i: (0, 0))],
        out_specs=pl.BlockSpec((ts, N), lambda i: (i, 0)), out_shape=jax.ShapeDtypeStruct((S, N), F32),
        compiler_params=_cp(("parallel",)), name=name,
    )(z, wgp, bgp)


def _loggate_bwd(dla, z, wgp, wgp_t, bgp, name):
    S = z.shape[0]
    N = wgp.shape[1]
    ts = _div(S, 512, 8)

    def body(dla_ref, g_ref, w_ref, wt_ref, b_ref, dg_ref, dw_ref, db_ref):
        @pl.when(pl.program_id(0) == 0)
        def _():
            dw_ref[...] = jnp.zeros_like(dw_ref)
            db_ref[...] = jnp.zeros_like(db_ref)

        glr = g_ref[...]
        zg = _dot(glr, w_ref[...]) + b_ref[...]
        dzg = dla_ref[...] * (1.0 / GATE_TAU) * (1.0 - _sigmoid(zg))
        dg_ref[...] = _dot(dzg, wt_ref[...]).astype(BF16)
        dw_ref[...] += _dot_tn(glr, dzg)
        db_ref[...] += jnp.sum(dzg, axis=0, keepdims=True)

    return pl.pallas_call(
        body, grid=(S // ts,),
        in_specs=[pl.BlockSpec((ts, N), lambda i: (i, 0)), pl.BlockSpec((ts, LANE), lambda i: (i, ZC_G // LANE)),
                  pl.BlockSpec((LANE, N), lambda i: (0, 0)), pl.BlockSpec((N, LANE), lambda i: (0, 0)), pl.BlockSpec((1, N), lambda i: (0, 0))],
        out_specs=[pl.BlockSpec((ts, LANE), lambda i: (i, 0)), pl.BlockSpec((LANE, N), lambda i: (0, 0)), pl.BlockSpec((1, N), lambda i: (0, 0))],
        out_shape=[jax.ShapeDtypeStruct((S, LANE), BF16), jax.ShapeDtypeStruct((LANE, N), F32), jax.ShapeDtypeStruct((1, N), F32)],
        compiler_params=_cp(("arbitrary",)), name=name,
    )(dla, z, wgp, wgp_t, bgp)


def _bdot(a, b, ca, cb):
    return lax.dot_general(a.astype(BF16), b.astype(BF16), (((ca,), (cb,)), ((0,), (0,))), preferred_element_type=F32)


def _bdot_exact(a, b):
    return lax.dot_general(a, b, (((2,), (1,)), ((0,), (0,))), preferred_element_type=F32, precision=lax.Precision.HIGHEST)


def _tiles(ref, cpb):
    return jnp.stack([ref[pl.ds(c * CHUNK, CHUNK), pl.ds(h * LANE, LANE)] for c in range(cpb) for h in range(HEADS)])


def _tri_masks(n):
    ri = lax.broadcasted_iota(jnp.int32, (n, CHUNK, CHUNK), 1)
    ci = lax.broadcasted_iota(jnp.int32, (n, CHUNK, CHUNK), 2)
    return ri >= ci, (ri >= ci).astype(F32), (ri <= ci).astype(F32)


def _chunk_fwd_terms(q, k, la, tril):
    bc = _bdot_exact(tril, la)
    bend = jnp.sum(la, axis=1, keepdims=True)
    eb = jnp.exp(bc)
    enb = jnp.exp(-bc)
    ee = jnp.exp(bend - bc)
    qs = q * (DK ** -0.5)
    return bend, eb, enb, ee, qs * eb, qs * enb, k * enb, k * eb, k * ee


def _gla_fwd(z, la, gn, name):
    S = z.shape[0]
    W = HEADS * LANE
    tb = _div(S, 512, CHUNK)
    cpb = tb // CHUNK

    def body(q_ref, k_ref, v_ref, r_ref, la_ref, gn_ref, o_ref, sp_ref, y_ref, st):
        @pl.when(pl.program_id(0) == 0)
        def _():
            st[...] = jnp.zeros_like(st)

        tri, tril, _ = _tri_masks(cpb * HEADS)
        q, k, v, rv, lav = (_tiles(r, cpb) for r in (q_ref, k_ref, v_ref, r_ref, la_ref))
        bend, _, _, _, qf, qb, kb, kf, ke = _chunk_fwd_terms(q, k, lav, tril)
        att = jnp.where(tri, _bdot(qf, kb, 2, 2), _bdot(qb, kf, 2, 2))
        o_intra = _bdot(att, v, 2, 1)
        u = _bdot(v, ke, 1, 1)
        gdec = jnp.exp(bend)
        s_prev = [None] * (cpb * HEADS)
        for h in range(HEADS):
            s = st[h]
            for c in range(cpb):
                b = c * HEADS + h
                s_prev[b] = s
                s = s * gdec[b] + u[b]
            st[h] = s
        s_prev = jnp.stack(s_prev)
        o = o_intra + _bdot(qf, s_prev, 2, 2)
        rms = lax.rsqrt(jnp.mean(o * o, axis=-1, keepdims=True) + EPS)
        gn = jnp.stack([gn_ref[pl.ds(h, 1), :] for _ in range(cpb) for h in range(HEADS)])
        y = (o * rms * gn * (rv * _sigmoid(rv))).astype(BF16)
        for c in range(cpb):
            for h in range(HEADS):
                b = c * HEADS + h
                rows, ln = pl.ds(c * CHUNK, CHUNK), pl.ds(h * LANE, LANE)
                o_ref[rows, ln] = o[b]
                y_ref[rows, ln] = y[b]
                sp_ref[h, c] = s_prev[b]

    zb = lambda base: pl.BlockSpec((tb, W), lambda i: (i, base // W))
    hb_ = pl.BlockSpec((tb, W), lambda i: (i, 0))
    return pl.pallas_call(
        body, grid=(S // tb,),
        in_specs=[zb(ZC_Q), zb(ZC_K), zb(ZC_V), zb(ZC_R), hb_, pl.BlockSpec((HEADS, DV), lambda i: (0, 0))],
        out_specs=[hb_, pl.BlockSpec((HEADS, cpb, DV, DKP), lambda i: (0, i, 0, 0)), hb_],
        out_shape=[jax.ShapeDtypeStruct((S, D_GLA), F32), jax.ShapeDtypeStruct((HEADS, S // CHUNK, DV, DKP), F32),
                   jax.ShapeDtypeStruct((S, D_GLA), BF16)],
        scratch_shapes=[pltpu.VMEM((HEADS, DV, DKP), F32)],
        compiler_params=_cp(("arbitrary",)), name=name,
    )(z, z, z, z, la, gn)


def _gla_bwd(dycat, z, la, o_raw, sprev, gn, name):
    S = z.shape[0]
    W = HEADS * LANE
    tb = _div(S, 512, CHUNK)
    cpb = tb // CHUNK
    nb = S // tb

    def body(q_ref, k_ref, v_ref, r_ref, la_ref, o_ref, sp_ref, dy_ref, gn_ref, dz_ref, dla_ref, dgn_ref, dst):
        @pl.when(pl.program_id(0) == 0)
        def _():
            dst[...] = jnp.zeros_like(dst)
            dgn_ref[...] = jnp.zeros_like(dgn_ref)

        nt = cpb * HEADS
        tri, tril, triu = _tri_masks(nt)
        q, k, v, rv, lav, o, dy = (_tiles(r, cpb) for r in (q_ref, k_ref, v_ref, r_ref, la_ref, o_ref, dy_ref))
        bend, eb, enb, ee, qf, qb, kb, kf, ke = _chunk_fwd_terms(q, k, lav, tril)
        att = jnp.where(tri, _bdot(qf, kb, 2, 2), _bdot(qb, kf, 2, 2))
        s_prev = jnp.stack([sp_ref[h, c] for c in range(cpb) for h in range(HEADS)])
        gdec = jnp.exp(bend)
        gn = jnp.stack([gn_ref[pl.ds(h, 1), :] for _ in range(cpb) for h in range(HEADS)])
        rms = lax.rsqrt(jnp.mean(o * o, axis=-1, keepdims=True) + EPS)
        oh = o * rms
        sg = _sigmoid(rv)
        sr = rv * sg
        d_r = (dy * oh * gn * (sg * (1.0 + rv * (1.0 - sg)))).astype(BF16)
        dgn = jnp.sum(dy * sr * oh, axis=1, keepdims=True)
        w = dy * sr * gn
        do = rms * (w - oh * jnp.mean(w * oh, axis=-1, keepdims=True))
        p = _bdot(do, qf, 1, 1)
        ds = [None] * nt
        for h in range(HEADS):
            s = dst[h]
            for c in reversed(range(cpb)):
                b = c * HEADS + h
                ds[b] = s
                s = s * gdec[b] + p[b]
            dst[h] = s
            dgn_ref[pl.ds(h, 1), :] += sum(dgn[c * HEADS + h] for c in range(cpb))
        ds = jnp.stack(ds)
        datt = _bdot(do, v, 2, 2)
        daf = jnp.where(tri, datt, 0.0)
        dab = jnp.where(tri, 0.0, datt)
        d_v = (_bdot(att, do, 1, 1) + _bdot(ke, ds, 2, 2)).astype(BF16)
        dke = _bdot(v, ds, 2, 1)
        dqf = _bdot(daf, kb, 2, 1) + _bdot(do, s_prev, 2, 1)
        dkb = _bdot(daf, qf, 1, 1)
        dqb = _bdot(dab, kf, 2, 1)
        dkf = _bdot(dab, qb, 1, 1)
        dg = jnp.sum(ds * s_prev, axis=1, keepdims=True)
        d_q = ((dqf * eb + dqb * enb) * (DK ** -0.5)).astype(BF16)
        d_k = (dkb * enb + dkf * eb + dke * ee).astype(BF16)
        dbc = dqf * qf - dkb * kb - dqb * qb + dkf * kf - dke * ke
        dbend = jnp.sum(dke * ke, axis=1, keepdims=True) + dg * gdec
        dla = _bdot_exact(triu, dbc) + dbend
        for c in range(cpb):
            for h in range(HEADS):
                b = c * HEADS + h
                rows = pl.ds(c * CHUNK, CHUNK)
                for base, val in ((ZC_Q, d_q), (ZC_K, d_k), (ZC_V, d_v), (ZC_R, d_r)):
                    dz_ref[rows, pl.ds(base + h * LANE, LANE)] = val[b]
                dla_ref[rows, pl.ds(h * LANE, LANE)] = dla[b]

    zb = lambda base: pl.BlockSpec((tb, W), lambda i: (nb - 1 - i, base // W))
    hb_ = pl.BlockSpec((tb, W), lambda i: (nb - 1 - i, 0))
    return pl.pallas_call(
        body, grid=(nb,),
        in_specs=[zb(ZC_Q), zb(ZC_K), zb(ZC_V), zb(ZC_R), hb_, hb_,
                  pl.BlockSpec((HEADS, cpb, DV, DKP), lambda i: (0, nb - 1 - i, 0, 0)),
                  pl.BlockSpec((tb, W), lambda i: (nb - 1 - i, 1)),
                  pl.BlockSpec((HEADS, DV), lambda i: (0, 0))],
        out_specs=[pl.BlockSpec((tb, Z_GLA), lambda i: (nb - 1 - i, 0)), hb_, pl.BlockSpec((HEADS, DV), lambda i: (0, 0))],
        out_shape=[jax.ShapeDtypeStruct((S, Z_GLA), BF16), jax.ShapeDtypeStruct((S, HEADS * DKP), F32), jax.ShapeDtypeStruct((HEADS, DV), F32)],
        scratch_shapes=[pltpu.VMEM((HEADS, DV, DKP), F32)],
        compiler_params=_cp(("arbitrary",)), name=name,
    )(z, z, z, z, la, o_raw, sprev, dycat, gn)


def _mod_proj(c_all, w3, layer, b, name):
    B, D = c_all.shape
    N = w3.shape[2]
    tn = _div(N, 1024, LANE)

    def body(c_ref, w_ref, b_ref, o_ref):
        cv = c_ref[...]
        o_ref[...] = _dot(cv * _sigmoid(cv), w_ref[...]) + b_ref[...]

    return pl.pallas_call(
        body, grid=(N // tn,),
        in_specs=[pl.BlockSpec((B, D), lambda j: (0, 0)), pl.BlockSpec((None, D, tn), lambda j: (layer, 0, j)), pl.BlockSpec((1, tn), lambda j: (0, j))],
        out_specs=pl.BlockSpec((B, tn), lambda j: (0, j)), out_shape=jax.ShapeDtypeStruct((B, N), F32),
        compiler_params=_cp(("parallel",)), name=name,
    )(c_all, w3, b)


def _mod_wgrad(c_t, dm, name):
    D, B = c_t.shape
    N = dm.shape[1]
    tn = _div(N, 1024, LANE)

    def body(c_ref, d_ref, o_ref):
        cv = c_ref[...]
        ca = cv * _sigmoid(cv)
        acc = ca[:, 0:1] * d_ref[pl.ds(0, 1), :]
        for b in range(1, B):
            acc = acc + ca[:, b:b + 1] * d_ref[pl.ds(b, 1), :]
        o_ref[...] = acc

    return pl.pallas_call(
        body, grid=(N // tn,),
        in_specs=[pl.BlockSpec((D, B), lambda j: (0, 0)), pl.BlockSpec((B, tn), lambda j: (0, j))],
        out_specs=pl.BlockSpec((D, tn), lambda j: (0, j)), out_shape=jax.ShapeDtypeStruct((D, N), F32),
        compiler_params=_cp(("parallel",)), name=name,
    )(c_t, dm)


def _rowsum(xs, name):
    n, N = xs.shape
    tn = _div(N, 8192, LANE)

    def body(x_ref, o_ref):
        acc = x_ref[pl.ds(0, 1), :]
        for r in range(1, n):
            acc = acc + x_ref[pl.ds(r, 1), :]
        o_ref[...] = acc

    return pl.pallas_call(
        body, grid=(N // tn,), in_specs=[pl.BlockSpec((n, tn), lambda j: (0, j))],
        out_specs=pl.BlockSpec((1, tn), lambda j: (0, j)), out_shape=jax.ShapeDtypeStruct((1, N), F32),
        compiler_params=_cp(("parallel",)), name=name,
    )(xs)


def _adamw(w, g, m, v, name, copy_grad=False):
    R, C = w.shape
    tr = _div(R, max(8, (1 << 18) // C), 8)

    def body(w_ref, g_ref, m_ref, v_ref, d_ref, nm_ref, nv_ref, *g_out):
        gv = g_ref[...]
        if copy_grad:
            g_out[0][...] = gv
        mn = ADAM_B1 * m_ref[...] + (1.0 - ADAM_B1) * gv
        vn = ADAM_B2 * v_ref[...] + (1.0 - ADAM_B2) * (gv * gv)
        m_hat = mn / (1.0 - ADAM_B1 ** ADAM_STEP)
        v_hat = vn / (1.0 - ADAM_B2 ** ADAM_STEP)
        d_ref[...] = -ADAM_LR * (m_hat / (jnp.sqrt(v_hat) + ADAM_EPS) + ADAM_WD * w_ref[...])
        nm_ref[...] = mn
        nv_ref[...] = vn

    blk = pl.BlockSpec((tr, C), lambda i: (i, 0))
    os_ = jax.ShapeDtypeStruct((R, C), F32)
    n_out = 4 if copy_grad else 3
    return pl.pallas_call(
        body, grid=(R // tr,), in_specs=[blk] * 4, out_specs=[blk] * n_out, out_shape=[os_] * n_out,
        compiler_params=_cp(("parallel",)), name=name,
    )(w, g, m, v)


def _place():
    return lax.axis_index("x"), lax.axis_index("y"), lax.axis_index("c")


def _other_chips(x, y):
    return [(1 - x, y), (x, 1 - y), (1 - x, 1 - y)]


def _half(c, rows):
    return pl.ds(c * (rows // 2), rows // 2)


_ANY = pl.BlockSpec(memory_space=pl.ANY)


def _ag_small(v, name):
    r, n = v.shape

    def body(v_ref, o_ref, send_sems, recv_sems):
        x, y, c = _place()
        me = 4 * x + 2 * y + c
        o_ref[pl.ds(me, 1)] = v_ref[...][None]
        peers = [(x ^ (k >> 2), y ^ ((k >> 1) & 1), c ^ (k & 1)) for k in range(1, 8)]
        copies = []
        for k, peer in enumerate(peers):
            cp = pltpu.make_async_remote_copy(
                src_ref=v_ref, dst_ref=o_ref.at[me], send_sem=send_sems.at[k], recv_sem=recv_sems.at[k],
                device_id=peer, device_id_type=MESH)
            cp.start()
            copies.append(cp)
        for cp in copies:
            cp.wait()

    return pl.pallas_call(
        body, out_shape=jax.ShapeDtypeStruct((8, r, n), v.dtype),
        in_specs=[pl.BlockSpec(memory_space=pltpu.VMEM)], out_specs=pl.BlockSpec(memory_space=pltpu.VMEM),
        scratch_shapes=[pltpu.SemaphoreType.DMA((7,)), pltpu.SemaphoreType.DMA((7,))],
        compiler_params=pltpu.CompilerParams(vmem_limit_bytes=VMEM_LIMIT), name=name,
    )(v)


def _rs_sibling(gs, name):
    n = len(gs)

    def body(*refs):
        src, out = refs[:n], refs[n:2 * n]
        send_sems, recv_sems = refs[2 * n:]
        x, y, c = _place()
        copies = []
        for i in range(n):
            cp = pltpu.make_async_remote_copy(
                src_ref=src[i].at[:, _half(1 - c, src[i].shape[1])], dst_ref=out[i], send_sem=send_sems.at[i], recv_sem=recv_sems.at[i],
                device_id=(x, y, 1 - c), device_id_type=MESH)
            cp.start()
            copies.append(cp)
        for cp in copies:
            cp.wait()

    return pl.pallas_call(
        body, out_shape=[jax.ShapeDtypeStruct((N_CHIPS, g.shape[1] // 2, g.shape[2]), g.dtype) for g in gs],
        in_specs=[_ANY] * n, out_specs=[_ANY] * n,
        scratch_shapes=[pltpu.SemaphoreType.DMA((n,)), pltpu.SemaphoreType.DMA((n,))],
        compiler_params=pltpu.CompilerParams(has_side_effects=True), name=name,
    )(*gs)


def _rs_presum(g, sib, c_arr, name):
    ns, R, C = g.shape
    rh = R // 2
    tr = _div(rh, max(16, (1 << 19) // C), 16)
    nrb = rh // tr

    def body(c_ref, g_ref, s_ref, o_ref):
        o_ref[...] = (g_ref[...] + s_ref[...]).astype(BF16)

    return pl.pallas_call(
        body, out_shape=jax.ShapeDtypeStruct((ns, rh, C), BF16),
        grid_spec=pltpu.PrefetchScalarGridSpec(
            num_scalar_prefetch=1, grid=(ns, nrb),
            in_specs=[pl.BlockSpec((None, tr, C), lambda s, r, c_ref: (s, c_ref[0] * nrb + r, 0)),
                      pl.BlockSpec((None, tr, C), lambda s, r, c_ref: (s, r, 0))],
            out_specs=pl.BlockSpec((None, tr, C), lambda s, r, c_ref: (s, r, 0))),
        compiler_params=_cp(("parallel", "parallel")), name=name,
    )(c_arr, g, sib)


def _rs_sum(g, sib, recv, full, layer, sc_arr, name):
    ns, R, C = g.shape
    rh = R // 2
    tr = _div(rh, max(16, (1 << 18) // C), 16)
    nrb = rh // tr

    def body(sc_ref, g_ref, s_ref, r_ref, f_ref, o_ref):
        acc = g_ref[...] + s_ref[...]
        for j in range(3):
            acc = acc + r_ref[j].astype(F32)
        o_ref[...] = acc

    return pl.pallas_call(
        body, out_shape=jax.ShapeDtypeStruct(full.shape, F32),
        grid_spec=pltpu.PrefetchScalarGridSpec(
            num_scalar_prefetch=1, grid=(nrb,),
            in_specs=[pl.BlockSpec((None, tr, C), lambda r, sc: (sc[0], sc[1] * nrb + r, 0)),
                      pl.BlockSpec((None, tr, C), lambda r, sc: (sc[0], r, 0)),
                      pl.BlockSpec((3, tr, C), lambda r, sc: (0, r, 0)),
                      _ANY],
            out_specs=pl.BlockSpec((None, tr, C), lambda r, sc: (layer, sc[1] * nrb + r, 0))),
        input_output_aliases={4: 0},
        compiler_params=_cp(("parallel",)), name=name,
    )(sc_arr, g, sib, recv, full)


def _rs_share(fulls, layer, name):
    n = len(fulls)

    def body(*refs):
        src, out = refs[:n], refs[n:2 * n]
        send_sems, recv_sems = refs[2 * n:]
        x, y, c = _place()
        copies = []
        for i in range(n):
            rows = out[i].shape[1]
            cp = pltpu.make_async_remote_copy(
                src_ref=out[i].at[layer, _half(c, rows)], dst_ref=out[i].at[layer, _half(c, rows)],
                send_sem=send_sems.at[i], recv_sem=recv_sems.at[i], device_id=(x, y, 1 - c), device_id_type=MESH)
            cp.start()
            copies.append(cp)
        for cp in copies:
            cp.wait()

    return pl.pallas_call(
        body, out_shape=[jax.ShapeDtypeStruct(f.shape, f.dtype) for f in fulls],
        in_specs=[_ANY] * n, out_specs=[_ANY] * n, input_output_aliases={i: i for i in range(n)},
        scratch_shapes=[pltpu.SemaphoreType.DMA((n,)), pltpu.SemaphoreType.DMA((n,))],
        compiler_params=pltpu.CompilerParams(has_side_effects=True), name=name,
    )(*fulls)


_HBM = pl.BlockSpec(memory_space=pltpu.HBM)
_SEM = pl.BlockSpec(memory_space=pltpu.SEMAPHORE)
_EFFECT = pltpu.SideEffectType.DATAFLOW_SIDE_EFFECTING


def _in_hbm(a):
    return pltpu.with_memory_space_constraint(a, pltpu.HBM)


def _split_start(bufs, n_sem, copies_of, name):
    nb = len(bufs)

    def body(*refs):
        for cp in copies_of(refs[:nb], refs[nb], refs[nb + 1]):
            cp.start()
        refs[-1][...] = jnp.zeros_like(refs[-1])

    out = pl.pallas_call(
        body, name=name,
        out_shape=(pltpu.SemaphoreType.DMA((n_sem,)), pltpu.SemaphoreType.DMA((n_sem,)), *[pltpu.HBM(a.shape, a.dtype) for a in bufs],
                   jax.ShapeDtypeStruct((SUB, LANE), F32)),
        in_specs=[_HBM] * nb, out_specs=(_SEM, _SEM, *([_HBM] * nb), pl.BlockSpec(memory_space=pltpu.VMEM)),
        input_output_aliases={i: 2 + i for i in range(nb)},
        compiler_params=pltpu.CompilerParams(has_side_effects=_EFFECT),
    )(*[_in_hbm(a) for a in bufs])
    return out[0], out[1], list(out[2:2 + nb]), out[-1]


def _split_wait(send_sems, recv_sems, bufs, after, copies_of, name):
    nb = len(bufs)

    def body(*refs):
        for cp in copies_of(refs[:nb], refs[nb], refs[nb + 1]):
            cp.wait_send()
            cp.wait_recv()

    return list(pl.pallas_call(
        body, name=name, out_shape=[pltpu.HBM(a.shape, a.dtype) for a in bufs],
        in_specs=[_HBM] * nb + [_SEM, _SEM, _ANY], out_specs=[_HBM] * nb,
        input_output_aliases={i: i for i in range(nb)},
        compiler_params=pltpu.CompilerParams(has_side_effects=_EFFECT),
    )(*bufs, send_sems, recv_sems, after))


def _ag_half_copies(land, send_sems, recv_sems, landing_of_mine):
    x, y, c = _place()
    cps = []
    for j, (cx, cy) in enumerate(_other_chips(x, y)):
        for i in range(len(land)):
            rows = _half(c, land[i].shape[1])
            s = 2 * x + y if landing_of_mine else 2 * cx + cy
            cps.append(pltpu.make_async_remote_copy(
                src_ref=land[i].at[2 * x + y, rows], dst_ref=land[i].at[s, rows], send_sem=send_sems.at[3 * i + j], recv_sem=recv_sems.at[3 * i + j],
                device_id=(cx, cy, c), device_id_type=MESH))
    return cps


def _ag_starts(land, send_sems, recv_sems):
    return _ag_half_copies(land, send_sems, recv_sems, True)


def _ag_waits(land, send_sems, recv_sems):
    return _ag_half_copies(land, send_sems, recv_sems, False)


def _ag_finish(lands, name):
    n = len(lands)

    def body(*refs):
        land = refs[n:2 * n]
        send_sems, recv_sems = refs[2 * n:]
        x, y, c = _place()
        sibling = (x, y, 1 - c)

        def copy(k, i, s, h):
            blk = land[i].at[s, _half(h, land[i].shape[1])]
            return pltpu.make_async_remote_copy(
                src_ref=blk, dst_ref=blk, send_sem=send_sems.at[k], recv_sem=recv_sems.at[k], device_id=sibling, device_id_type=MESH)

        chips = _other_chips(x, y)
        passed = [copy(3 * i + j, i, 2 * cx + cy, c) for j, (cx, cy) in enumerate(chips) for i in range(n)]
        for cp in passed:
            cp.start()
        for j, (cx, cy) in enumerate(chips):
            for i in range(n):
                copy(3 * i + j, i, 2 * cx + cy, 1 - c).wait_recv()
        for cp in passed:
            cp.wait_send()

    return pl.pallas_call(
        body, out_shape=[jax.ShapeDtypeStruct(a.shape, a.dtype) for a in lands],
        in_specs=[_ANY] * n, out_specs=[_ANY] * n, input_output_aliases={i: i for i in range(n)},
        scratch_shapes=[pltpu.SemaphoreType.DMA((3 * n,)), pltpu.SemaphoreType.DMA((3 * n,))],
        compiler_params=pltpu.CompilerParams(has_side_effects=True), name=name,
    )(*lands)


def _rs_chip_copies(bufs, send_sems, recv_sems):
    n = len(bufs) // 2
    x, y, c = _place()
    return [pltpu.make_async_remote_copy(
        src_ref=bufs[i].at[2 * cx + cy], dst_ref=bufs[n + i].at[j], send_sem=send_sems.at[3 * i + j], recv_sem=recv_sems.at[3 * i + j],
        device_id=(cx, cy, c), device_id_type=MESH) for j, (cx, cy) in enumerate(_other_chips(x, y)) for i in range(n)]


def _pad_heads(w):
    lead = w.shape[:-1]
    w4 = w.reshape(*lead, HEADS, DK)
    w4 = jnp.pad(w4, [(0, 0)] * len(lead) + [(0, 0), (0, DKP - DK)])
    return w4.reshape(*lead, HEADS * DKP)


def _unpad_heads(w):
    lead = w.shape[:-1]
    return w.reshape(*lead, HEADS, DKP)[..., :DK].reshape(*lead, HEADS * DK)


def _mix_weight(win4, n_cols):
    D = win4.shape[1]
    w = jnp.transpose(win4[:, :, :n_cols], (1, 0, 2)).reshape(D, N_CHIPS * n_cols)
    o = 2 * D_CONV
    hk = HEADS * DK
    ab = w[:, :o]
    q = _pad_heads(w[:, o:o + hk])
    k = _pad_heads(w[:, o + hk:o + 2 * hk])
    vr = w[:, o + 2 * hk:o + 2 * hk + 2 * D_GLA]
    glr = jnp.pad(w[:, o + 2 * hk + 2 * D_GLA:], ((0, 0), (0, LANE - GATE_RANK)))
    return jnp.concatenate([q, k, vr, ab, glr], axis=1)


def _mix_weight_grad(dgla, dab, dglr, n_cols, n_pad):
    D = dab.shape[0]
    hkp = HEADS * DKP
    w = jnp.concatenate([dab, _unpad_heads(dgla[:, :hkp]), _unpad_heads(dgla[:, hkp:2 * hkp]), dgla[:, 2 * hkp:], dglr[:, :GATE_RANK]], axis=1)
    w = jnp.pad(w.reshape(D, N_CHIPS, n_cols), ((0, 0), (0, 0), (0, n_pad - n_cols)))
    return jnp.transpose(w, (1, 0, 2))


_ARG_NAMES = ['x', 'c', 'w_ada', 'b_ada', 'g_norm_ffn1', 'w_ffn1_in', 'w_ffn1_out', 'g_norm_mix', 'w_in', 'w_dw', 'b_dw', 'g_conv_ln', 'b_conv_ln', 'w_gate_up', 'b_gate', 'g_gla_norm', 'w_out', 'g_norm_ffn2', 'w_ffn2_in', 'w_ffn2_out', 'g_norm_final', 'w_ada_final', 'b_ada_final']
_WEIGHTS = _ARG_NAMES[2:]
_BIG = ('w_ffn1_in', 'w_ffn1_out', 'w_in', 'w_out', 'w_ffn2_in', 'w_ffn2_out')
_SMALL = ('g_norm_ffn1', 'g_norm_mix', 'w_dw', 'b_dw', 'g_conv_ln', 'b_conv_ln', 'w_gate_up', 'b_gate', 'g_gla_norm', 'g_norm_ffn2', 'g_norm_final')


def _ffn_fwd(x, h, gv, w4, wo, nxt, tag):
    z, act = _ffn_up(h, w4, f"ffn_up_{tag}")
    y, xn, *hn = _ffn_down(act, wo, x, gv, nxt, f"ffn_down_{tag}")
    return xn, (hn[0] if hn else None), y, (x, h, z, act)


def _ffn_bwd(dxn, dy, saved, g, scale, prev, w4, wo, tag):
    x, h, z, act = saved
    ns = w4.shape[0]
    dwo = _mm_tn(act, dy, f"dw_out_{tag}")
    dz, dx, dsh, dsc, dg, *pv = _ffn_bwd_core(dy, wo, z, w4, x, dxn, g, scale, prev, f"ffn_bwd_{tag}")
    dwi = _dw_ffn_in(h, dz, ns, f"dw_in_{tag}")
    return dx, pv, dict(dshift=dsh, dscale=dsc, dg=dg, dw_in=dwi, dw_out=dwo.reshape(N_CHIPS, -1, dwo.shape[1]))


def _mix_fwd(x, h, gv, wmix, w_dw, b_dw, g_ln, b_ln, wgp, bgp, gn, wout, nxt, tag):
    z = _mm([(h, wmix, 0)], F32, f"mix_in_{tag}")
    u, yc, yconv = _conv_fwd(z, w_dw, b_dw, g_ln, b_ln, f"conv_fwd_{tag}")
    la = _loggate(z, wgp, bgp, f"loggate_{tag}")
    o_raw, sprev, ygla = _gla_fwd(z, la, gn, f"gla_fwd_{tag}")
    y, xn, *hn = _mix_out(yconv, ygla, wout, x, gv, nxt, f"mix_out_{tag}")
    return xn, (hn[0] if hn else None), y, (x, h, z, u, yc, la, o_raw, sprev, yconv, ygla)


def _mix_bwd(dxn, dy, saved, g, scale, prev, wmix, w_dw, g_ln, b_ln, wgp, bgp, gn, wout, n_cols, n_pad, tag):
    x, h, z, u, yc, la, o_raw, sprev, yconv, ygla = saved
    dycat = _mm([(dy, wout, 0)], F32, f"mix_dycat_{tag}", nt=True)
    dwout = _mm_tn_two(yconv, ygla, dy, f"dw_mixout_{tag}")
    dab, dwdw, dbdw, dgln, dbln = _conv_bwd(dycat, z, u, yc, w_dw, g_ln, b_ln, f"conv_bwd_{tag}")
    dgla, dla, dgn = _gla_bwd(dycat, z, la, o_raw, sprev, gn, f"gla_bwd_{tag}")
    dglr, dwgp, dbgp = _loggate_bwd(dla, z, wgp, wgp.T, bgp, f"loggate_bwd_{tag}")
    dx, dsh, dsc, dg, *pv = _dh_normmod_bwd(
        [(dgla, wmix, 0, 0, Z_GLA), (dab, wmix, 0, ZC_A // (2 * D_CONV), 2 * D_CONV), (dglr, wmix, 0, ZC_G // LANE, LANE)],
        x, dxn, g, scale, prev, f"mix_dh_{tag}")
    dwin = _mix_weight_grad(_mm_tn(h, dgla, f"dw_mixin_gla_{tag}"), _mm_tn(h, dab, f"dw_mixin_conv_{tag}"), _mm_tn(h, dglr, f"dw_mixin_gate_{tag}"),
                            n_cols, n_pad)
    grads = dict(dshift=dsh, dscale=dsc, dg=dg, dw_in=dwin, dw_out=dwout.reshape(N_CHIPS, -1, dwout.shape[1]), dw_dw=dwdw, db_dw=dbdw,
                 dg_ln=dgln, db_ln=dbln, dw_gate=_unpad_heads(dwgp[:GATE_RANK]), db_gate=_unpad_heads(dbgp)[0], dgn=dgn)
    return dx, pv, grads


def kernel(x, c, w_ada, b_ada, g_norm_ffn1, w_ffn1_in, w_ffn1_out, g_norm_mix, w_in, w_dw, b_dw, g_conv_ln, b_conv_ln, w_gate_up, b_gate, g_gla_norm, w_out, g_norm_ffn2, w_ffn2_in, w_ffn2_out, g_norm_final, w_ada_final, b_ada_final, loss_target, m_w_ada, m_b_ada, m_g_norm_ffn1, m_w_ffn1_in, m_w_ffn1_out, m_g_norm_mix, m_w_in, m_w_dw, m_b_dw, m_g_conv_ln, m_b_conv_ln, m_w_gate_up, m_b_gate, m_g_gla_norm, m_w_out, m_g_norm_ffn2, m_w_ffn2_in, m_w_ffn2_out, m_g_norm_final, m_w_ada_final, m_b_ada_final, v_w_ada, v_b_ada, v_g_norm_ffn1, v_w_ffn1_in, v_w_ffn1_out, v_g_norm_mix, v_w_in, v_w_dw, v_b_dw, v_g_conv_ln, v_b_conv_ln, v_w_gate_up, v_b_gate, v_g_gla_norm, v_w_out, v_g_norm_ffn2, v_w_ffn2_in, v_w_ffn2_out, v_g_norm_final, v_w_ada_final, v_b_ada_final):
    given = dict(locals())
    W = {n: given[n] for n in _WEIGHTS}
    M1 = {n: given["m_" + n] for n in _WEIGHTS}
    M2 = {n: given["v_" + n] for n in _WEIGHTS}
    xs = x[0]
    tgt = loss_target[0]
    S, D = xs.shape
    L = w_ada.shape[0]
    xi, yi, ci = _place()
    s_me = 2 * xi + yi
    b_me = 4 * xi + 2 * yi + ci
    nsh = w_ada.shape[2]
    nfin = w_ada_final.shape[1]
    n_cols = w_in.shape[2]
    n_pad = -(-n_cols // LANE) * LANE

    def lands_of(l):
        shards = [W[n][l].astype(BF16) for n in _BIG]
        shards[2] = jnp.pad(shards[2], ((0, 0), (0, n_pad - n_cols)))
        return [lax.dynamic_update_index_in_dim(lax.empty((N_CHIPS,) + s.shape, BF16), s, s_me, 0) for s in shards]

    lands = {l: lands_of(l) for l in range(L)}
    ag_groups = [dict(l=0, items=[0, 1], need=0), dict(l=0, items=[2, 3, 4, 5], need=1)]
    ag_groups += [dict(l=l, items=list(range(len(_BIG))), need=3 * l) for l in range(1, L)]

    def ag_start(grp):
        bufs = [lands[grp["l"]][i] for i in grp["items"]]
        return _split_start(bufs, 3 * len(bufs), _ag_starts, f"ag_start_l{grp['l']}_{grp['items'][0]}")

    pend = ag_start(ag_groups[0])
    tok = pend[3][0, 0]

    c_all = _ag_small(c.reshape(8, D // 8) + tok, "ag_c").reshape(8, D)
    tok = None
    parts = [_mod_proj(c_all, w_ada, l, lax.dynamic_slice(b_ada, (l, s_me * nsh), (1, nsh)), f"mod_proj_{l}") for l in range(L)]
    parts.append(_mod_proj(c_all, w_ada_final[None], 0, lax.dynamic_slice(b_ada_final, (s_me * nfin,), (nfin,))[None], "mod_proj_final"))
    mod_all = _ag_small(jnp.concatenate(parts, axis=1), "ag_mod")
    mine = [lax.dynamic_index_in_dim(lax.dynamic_index_in_dim(mod_all, 2 * s + ci, 0, False), b_me, 0, False) for s in range(N_CHIPS)]
    mods = [jnp.concatenate([mine[s][l * nsh:(l + 1) * nsh] for s in range(N_CHIPS)]).reshape(N_MOD, 1, D) for l in range(L)]
    fmod = jnp.concatenate([mine[s][L * nsh:] for s in range(N_CHIPS)]).reshape(2, 1, D)

    tiny = jnp.concatenate([w_dw.reshape(-1), w_gate_up.reshape(-1)])
    tiny_all = _ag_small(jnp.pad(tiny, (0, (-tiny.shape[0]) % (8 * LANE))).reshape(8, -1), "ag_tiny").reshape(8, -1)
    n_dw = w_dw.size
    dw_parts = [lax.dynamic_index_in_dim(tiny_all, 2 * s + ci, 0, False) for s in range(N_CHIPS)]
    w_dw_full = jnp.concatenate([p[:n_dw].reshape(w_dw.shape) for p in dw_parts], axis=2)
    w_gu_full = jnp.concatenate([p[n_dw:n_dw + w_gate_up.size].reshape(w_gate_up.shape) for p in dw_parts], axis=2)

    def layer_weights(l, lands):
        wi1, wo1, win4, wout4, wi2, wo2 = lands
        return dict(
            wi1=wi1, wo1=wo1.reshape(-1, D), wi2=wi2, wo2=wo2.reshape(-1, D), wout=wout4.reshape(-1, D), wmix=_mix_weight(win4, n_cols),
            wgp=jnp.pad(_pad_heads(w_gu_full[l]), ((0, LANE - GATE_RANK), (0, 0))).astype(BF16), bgp=_pad_heads(b_gate[l])[None])

    gnorm = (g_norm_ffn1, g_norm_mix, g_norm_ffn2)
    subs = [dict(l=l, j=j, tag=f"{('ffn1', 'mix', 'ffn2')[j]}_l{l}", g=gnorm[j][l][None], shift=mods[l][3 * j], scale=mods[l][3 * j + 1],
                 gv=mods[l][3 * j + 2] * (1.0 if j == 1 else 0.5)) for l in range(L) for j in range(3)]
    gi = 0
    xcur = xs
    h = None
    for k, sb in enumerate(subs):
        l, j = sb["l"], sb["j"]
        if pend is not None and ag_groups[gi]["need"] == k:
            grp = ag_groups[gi]
            nm = f"l{grp['l']}_{grp['items'][0]}"
            after = xcur if k > 0 else sb["shift"]
            done = _ag_finish(_split_wait(pend[0], pend[1], pend[2], after, _ag_waits, f"ag_wait_{nm}"), f"ag_finish_{nm}")
            for i, a in zip(grp["items"], done):
                lands[grp["l"]][i] = a
            gi += 1
            pend = ag_start(ag_groups[gi]) if gi < len(ag_groups) else None
            tok = pend[3][0, 0] if pend is not None else None
        if h is None:
            h = _normmod(xcur, sb["g"] if tok is None else sb["g"] + tok, sb["shift"], sb["scale"], f"normmod_{sb['tag']}")
            tok = None
        d = layer_weights(l, lands[l])
        nxt = (subs[k + 1]["g"], subs[k + 1]["shift"], subs[k + 1]["scale"]) if k + 1 < len(subs) else None
        gv = sb["gv"] if tok is None else sb["gv"] + tok
        tok = None
        if j == 1:
            xcur, h, sb["y"], sb["saved"] = _mix_fwd(xcur, h, gv, d["wmix"], w_dw_full[l], b_dw[l][None], g_conv_ln[l][None],
                                                     b_conv_ln[l][None], d["wgp"], d["bgp"], g_gla_norm[l], d["wout"], nxt, sb["tag"])
        else:
            w4, wo = (d["wi1"], d["wo1"]) if j == 0 else (d["wi2"], d["wo2"])
            xcur, h, sb["y"], sb["saved"] = _ffn_fwd(xcur, h, gv, w4, wo, nxt, sb["tag"])
    lw = [layer_weights(l, lands[l]) for l in range(L)]

    c_arr = jnp.stack([ci]).astype(jnp.int32)
    sc_arr = jnp.stack([s_me, ci]).astype(jnp.int32)
    fulls = [lax.empty((L,) + ((W[n].shape[1], n_pad) if n == 'w_in' else W[n].shape[1:]), F32) for n in _BIG]

    def rs_begin(gs, items, l):
        nm = f"l{l}_{items[0]}"
        sibs = _rs_sibling(gs, f"rs_sibling_{nm}")
        return sibs, [_rs_presum(g, sb_, c_arr, f"rs_presum_{i}_l{l}") for i, g, sb_ in zip(items, gs, sibs)]

    def rs_end(gs, sibs, recvs, items, l):
        summed = [_rs_sum(g, sb_, rv, fulls[i], l, sc_arr, f"rs_sum_{i}_l{l}") for i, g, sb_, rv in zip(items, gs, sibs, recvs)]
        for i, f in zip(items, _rs_share(summed, l, f"rs_share_l{l}_{items[0]}")):
            fulls[i] = f

    def rs_start(gs, items, l):
        sibs, ps = rs_begin(gs, items, l)
        pend = _split_start(ps + [lax.empty((3,) + p.shape[1:], BF16) for p in ps], 3 * len(ps), _rs_chip_copies, f"rs_start_l{l}_{items[0]}")
        return dict(gs=gs, sibs=sibs, pend=pend, items=items, l=l)

    def rs_finish(fl, after):
        pend, n = fl["pend"], len(fl["gs"])
        bufs = _split_wait(pend[0], pend[1], pend[2], after, _rs_chip_copies, f"rs_wait_l{fl['l']}_{fl['items'][0]}")
        rs_end(fl["gs"], fl["sibs"], bufs[n:], fl["items"], fl["l"])

    sq, dx, dfsh, dfsc, dgfin, dy, dgv = _loss_head(xcur, g_norm_final[None], fmod[0], fmod[1], tgt, (subs[-1]["y"], subs[-1]["gv"]))
    loss_part = 0.5 / D * jnp.sum(sq)
    G = {n: [None] * L for n in _SMALL}
    dmods = [None] * L
    in_flight = None
    tok = None
    for l in reversed(range(L)):
        gr = [None] * 3
        for j in reversed(range(3)):
            k = 3 * l + j
            sb, d = subs[k], lw[l]
            prev = (subs[k - 1]["y"], subs[k - 1]["gv"]) if k > 0 else None
            g_vec = sb["g"] if tok is None else sb["g"] + tok
            tok = None
            if j == 1:
                dx, pv, gr[j] = _mix_bwd(dx, dy, sb["saved"], g_vec, sb["scale"], prev, d["wmix"], w_dw_full[l], g_conv_ln[l][None], b_conv_ln[l][None],
                                         d["wgp"], d["bgp"], g_gla_norm[l], d["wout"], n_cols, n_pad, sb["tag"])
            else:
                w4, wo = (d["wi1"], d["wo1"]) if j == 0 else (d["wi2"], d["wo2"])
                dx, pv, gr[j] = _ffn_bwd(dx, dy, sb["saved"], g_vec, sb["scale"], prev, w4, wo, sb["tag"])
            gr[j]["dgv"] = dgv
            dy, dgv = pv if pv else (None, None)
            if j == 1 and in_flight is not None:
                rs_finish(in_flight, dx)
                in_flight = None
            if j == 1 and l == 0:
                in_flight = rs_start([gr[1]["dw_in"], gr[1]["dw_out"], gr[2]["dw_in"], gr[2]["dw_out"]], [2, 3, 4, 5], l)
                tok = in_flight["pend"][3][0, 0]
        g1, g2, g3 = gr
        if l > 0:
            in_flight = rs_start([g1["dw_in"], g1["dw_out"], g2["dw_in"], g2["dw_out"], g3["dw_in"], g3["dw_out"]], list(range(len(_BIG))), l)
            tok = in_flight["pend"][3][0, 0]
        else:
            last = rs_start([g1["dw_in"], g1["dw_out"]], [0, 1], l)
            rs_finish(in_flight, dx)
            in_flight = None
        dmods[l] = jnp.concatenate([g1["dshift"], g1["dscale"], 0.5 * g1["dgv"], g2["dshift"], g2["dscale"], g2["dgv"],
                                    g3["dshift"], g3["dscale"], 0.5 * g3["dgv"]], axis=1)[0]
        G["g_norm_ffn1"][l], G["g_norm_ffn2"][l], G["g_norm_mix"][l] = g1["dg"][0], g3["dg"][0], g2["dg"][0]
        G["w_dw"][l], G["b_dw"][l], G["g_conv_ln"][l], G["b_conv_ln"][l] = g2["dw_dw"], g2["db_dw"][0], g2["dg_ln"][0], g2["db_ln"][0]
        G["w_gate_up"][l], G["b_gate"][l], G["g_gla_norm"][l] = g2["dw_gate"], g2["db_gate"], g2["dgn"]
    grad_x = dx[None]
    gsm = {}

    small = [jnp.stack(G[n]).reshape(-1) for n in _SMALL if n != 'g_norm_final'] + [dgfin[0]]
    dmod_vec = jnp.concatenate(dmods + [dfsh[0], dfsc[0]])
    n_mod_vec = dmod_vec.shape[0]
    vec = jnp.concatenate([dmod_vec] + small + [loss_part[None]])
    n_vec = vec.shape[0]
    vec = jnp.pad(vec, (0, (-n_vec) % (8 * LANE)))
    vec_all = _ag_small(vec.reshape(8, -1), "ag_small_grads").reshape(8, -1)
    vec_sum = _rowsum(vec_all, "sum_small_grads")[0]
    loss = vec_sum[n_vec - 1]
    off = n_mod_vec
    for n in _SMALL:
        shp = {'w_dw': w_dw_full.shape, 'w_gate_up': w_gu_full.shape}.get(n, W[n].shape)
        cnt = 1
        for dd in shp:
            cnt *= dd
        gsm[n] = vec_sum[off:off + cnt].reshape(shp)
        off += cnt
    gsm['w_dw'] = lax.dynamic_slice_in_dim(gsm['w_dw'], s_me * w_dw.shape[2], w_dw.shape[2], 2)
    gsm['w_gate_up'] = lax.dynamic_slice_in_dim(gsm['w_gate_up'], s_me * w_gate_up.shape[2], w_gate_up.shape[2], 2)
    dmod_sum = vec_sum[:n_mod_vec]
    gsm['b_ada'] = dmod_sum[:L * N_MOD * D].reshape(L, N_MOD * D)
    gsm['b_ada_final'] = dmod_sum[L * N_MOD * D:]
    c_t = c_all.T
    dmod_rows = vec_all[:, :n_mod_vec]
    gsm['w_ada'] = jnp.stack([
        _mod_wgrad(c_t, lax.dynamic_slice_in_dim(dmod_rows, l * N_MOD * D + s_me * nsh, nsh, 1), f"dw_ada_{l}") for l in range(L)])
    gsm['w_ada_final'] = _mod_wgrad(c_t, lax.dynamic_slice_in_dim(dmod_rows, L * N_MOD * D + s_me * nfin, nfin, 1), "dw_ada_final")
    rs_finish(last, gsm['w_ada_final'])
    gsm.update({n: (f[:, :, :n_cols] if n == 'w_in' else f) for n, f in zip(_BIG, fulls)})

    outs = {}
    small_names = [n for n in _WEIGHTS if W[n].size < 65536]
    for n in _WEIGHTS:
        if n in small_names:
            continue
        shp = W[n].shape
        v2 = lambda a: a.reshape(-1, shp[-1])
        from_rs = n in _BIG and n != 'w_in'
        d_, m_, v_, *g_ = _adamw(v2(W[n]), v2(gsm[n]), v2(M1[n]), v2(M2[n]), f"adamw_{n}", copy_grad=from_rs)
        outs[n] = (d_.reshape(shp), m_.reshape(shp), v_.reshape(shp))
        if from_rs:
            gsm[n] = g_[0].reshape(shp)
    flat = lambda dct: jnp.concatenate([dct[n].reshape(-1) for n in small_names])
    n_small = sum(W[n].size for n in small_names)
    v2 = lambda a: jnp.pad(a, (0, (-n_small) % (8 * LANE))).reshape(-1, LANE)
    d_, m_, v_ = _adamw(v2(flat(W)), v2(flat(gsm)), v2(flat(M1)), v2(flat(M2)), "adamw_small")

    def unflat(a):
        res, o = {}, 0
        a = a.reshape(-1)
        for n in small_names:
            res[n] = a[o:o + W[n].size].reshape(W[n].shape)
            o += W[n].size
        return res

    for n, dd, mm, vv in zip(small_names, unflat(d_).values(), unflat(m_).values(), unflat(v_).values()):
        outs[n] = (dd, mm, vv)

    return (loss, grad_x, *[gsm[n] for n in _WEIGHTS], *[outs[n][0] for n in _WEIGHTS], *[outs[n][1] for n in _WEIGHTS], *[outs[n][2] for n in _WEIGHTS])
```

```python
import jax
import jax.numpy as jnp
from jax import lax
from jax.experimental import pallas as pl
from jax.experimental.pallas import tpu as pltpu

F32 = jnp.float32
BF16 = jnp.bfloat16

CHUNK = 64
HEADS = 4
DK = 64
DV = 128
DKP = 128
GATE_RANK = 16
GATE_TAU = 16.0
N_MOD = 9
EPS = 1e-6
ADAM_LR = 0.001
ADAM_B1 = 0.9
ADAM_B2 = 0.999
ADAM_EPS = 1e-08
ADAM_WD = 0.01
ADAM_STEP = 10

LANE = 128
HALO = 32
VMEM_LIMIT = 52 * 1024 * 1024
MESH = pl.DeviceIdType.MESH
N_CHIPS = 4

D_CONV = 512
D_GLA = HEADS * DV
ZC_Q = 0
ZC_K = ZC_Q + HEADS * DKP
ZC_V = ZC_K + HEADS * DKP
ZC_R = ZC_V + D_GLA
ZC_A = ZC_R + D_GLA
ZC_B = ZC_A + D_CONV
ZC_G = ZC_B + D_CONV
Z_COLS = ZC_G + LANE
Z_GLA = ZC_A


def _div(n, target, mult):
    best = None
    d = mult
    while d <= min(n, target):
        if n % d == 0:
            best = d
        d += mult
    return n if best is None else best


def _cp(sem=None, **kw):
    return pltpu.CompilerParams(dimension_semantics=sem, vmem_limit_bytes=VMEM_LIMIT, **kw)


def _resident(shape, index_map):
    return pl.BlockSpec(shape, index_map, pipeline_mode=pl.Buffered(1))


def _sigmoid(x):
    return 0.5 * jnp.tanh(0.5 * x) + 0.5


def _dot(a, b):
    return jnp.dot(a.astype(BF16), b.astype(BF16), preferred_element_type=F32)


def _dot_nt(a, b):
    return lax.dot_general(a.astype(BF16), b.astype(BF16), (((1,), (1,)), ((), ())), preferred_element_type=F32)


def _dot_tn(a, b):
    return lax.dot_general(a.astype(BF16), b.astype(BF16), (((0,), (0,)), ((), ())), preferred_element_type=F32)


def _dot_exact(a, b):
    return jnp.dot(a, b, preferred_element_type=F32, precision=lax.Precision.HIGHEST)


def _normmod(x, g, shift, scale, name):
    S, D = x.shape
    tm = _div(S, 512, 8)

    def body(x_ref, g_ref, sh_ref, sc_ref, o_ref):
        xv = x_ref[...]
        r = lax.rsqrt(jnp.mean(xv * xv, axis=-1, keepdims=True) + EPS)
        o_ref[...] = ((xv * r) * g_ref[...] * (1.0 + sc_ref[...]) + sh_ref[...]).astype(o_ref.dtype)

    row = pl.BlockSpec((tm, D), lambda i: (i, 0))
    vec = pl.BlockSpec((1, D), lambda i: (0, 0))
    return pl.pallas_call(
        body, grid=(S // tm,), in_specs=[row, vec, vec, vec], out_specs=row,
        out_shape=jax.ShapeDtypeStruct((S, D), BF16), compiler_params=_cp(("parallel",)), name=name,
    )(x, g, shift, scale)


def _loss_head(x, g, shift, scale, tgt, prev):
    S, D = x.shape
    tm = _div(S, 512, 8)

    def body(x_ref, g_ref, sh_ref, sc_ref, t_ref, y_ref, gvp_ref, sq_ref, dx_ref, dsh_ref, dsc_ref, dg_ref, dy_ref, dgv_ref):
        sums = (sq_ref, dsh_ref, dsc_ref, dg_ref, dgv_ref)

        @pl.when(pl.program_id(0) == 0)
        def _():
            for o in sums:
                o[...] = jnp.zeros_like(o)

        xv = x_ref[...]
        e = _norm_rows(xv, g_ref[...], sh_ref[...], sc_ref[...]) - t_ref[...]
        dx, dsh, dsc, dg, dy, dgv = _normmod_bwd_rows(xv, e * (1.0 / D), None, g_ref[...], sc_ref[...], (y_ref[...], gvp_ref[...]))
        dx_ref[...] = dx
        dy_ref[...] = dy
        for o, v in zip(sums, (jnp.sum(e * e, axis=0, keepdims=True), dsh, dsc, dg, dgv)):
            o[...] += v

    row = pl.BlockSpec((tm, D), lambda i: (i, 0))
    vec = pl.BlockSpec((1, D), lambda i: (0, 0))
    vs = jax.ShapeDtypeStruct((1, D), F32)
    return pl.pallas_call(
        body, grid=(S // tm,), in_specs=[row, vec, vec, vec, row, row, vec], out_specs=[vec, row, vec, vec, vec, row, vec],
        out_shape=[vs, jax.ShapeDtypeStruct((S, D), F32), vs, vs, vs, jax.ShapeDtypeStruct((S, D), BF16), vs],
        compiler_params=_cp(("arbitrary",)), name="loss_head",
    )(x, g, shift, scale, tgt, *prev)


def _normmod_bwd_rows(xv, dh, dres, gv, sc, prev):
    r = lax.rsqrt(jnp.mean(xv * xv, axis=-1, keepdims=True) + EPS)
    xh = xv * r
    dsh = jnp.sum(dh, axis=0, keepdims=True)
    dsc = jnp.sum(dh * (xh * gv), axis=0, keepdims=True)
    dn = dh * (1.0 + sc)
    dg = jnp.sum(dn * xh, axis=0, keepdims=True)
    dxh = dn * gv
    dx = r * (dxh - xh * jnp.mean(dxh * xh, axis=-1, keepdims=True))
    if dres is not None:
        dx = dx + dres
    if prev is None:
        return dx, dsh, dsc, dg
    y, gvp = prev
    return dx, dsh, dsc, dg, (gvp * dx).astype(BF16), jnp.sum(dx * y, axis=0, keepdims=True)


def _mm(pairs, out_dtype, name, nt=False):
    M = pairs[0][0].shape[0]
    N = pairs[0][1].shape[0] if nt else pairs[0][1].shape[1]
    ktot = sum(a.shape[1] for a, _, _ in pairs)
    tm = _div(M, 512 if ktot <= 4096 else 256, 8)
    n = len(pairs)

    def body(*refs):
        o_ref = refs[2 * n]
        dot = _dot_nt if nt else _dot
        acc = dot(refs[0][...], refs[1][...])
        for p in range(1, n):
            acc = acc + dot(refs[2 * p][...], refs[2 * p + 1][...])
        o_ref[...] = acc.astype(o_ref.dtype)

    ins, args = [], []
    for a, b, blk in pairs:
        k = a.shape[1]
        ins.append(pl.BlockSpec((tm, k), lambda i: (i, 0)))
        ins.append(_resident((N, k), lambda i, blk=blk: (0, blk)) if nt else _resident((k, N), lambda i: (0, 0)))
        args += [a, b]
    return pl.pallas_call(
        body, grid=(M // tm,), in_specs=ins, out_specs=pl.BlockSpec((tm, N), lambda i: (i, 0)),
        out_shape=jax.ShapeDtypeStruct((M, N), out_dtype), compiler_params=_cp(("parallel",)), name=name,
    )(*args)


TN_ROWS = 2048


def _mm_tn(a, g, name):
    S, Ka = a.shape
    N = g.shape[1]
    tk = _div(Ka, 1408, LANE)
    tn = _div(N, 1408, LANE)
    ts = _div(S, TN_ROWS, 8)

    def body(a_ref, g_ref, o_ref):
        @pl.when(pl.program_id(2) == 0)
        def _():
            o_ref[...] = jnp.zeros_like(o_ref)

        o_ref[...] += _dot_tn(a_ref[...], g_ref[...])

    return pl.pallas_call(
        body, grid=(Ka // tk, N // tn, S // ts),
        in_specs=[pl.BlockSpec((ts, tk), lambda i, j, s: (s, i)), pl.BlockSpec((ts, tn), lambda i, j, s: (s, j))],
        out_specs=pl.BlockSpec((tk, tn), lambda i, j, s: (i, j)),
        out_shape=jax.ShapeDtypeStruct((Ka, N), F32),
        compiler_params=_cp(("parallel", "parallel", "arbitrary")), name=name,
    )(a, g)


def _mm_tn_two(a0, a1, g, name):
    S, K = a0.shape
    N = g.shape[1]
    ts = _div(S, TN_ROWS, 8)

    def body(a0_ref, a1_ref, g_ref, o_ref):
        i = pl.program_id(0)

        @pl.when(pl.program_id(1) == 0)
        def _():
            o_ref[...] = jnp.zeros_like(o_ref)

        @pl.when(i == 0)
        def _():
            o_ref[...] += _dot_tn(a0_ref[...], g_ref[...])

        @pl.when(i == 1)
        def _():
            o_ref[...] += _dot_tn(a1_ref[...], g_ref[...])

    return pl.pallas_call(
        body, grid=(2, S // ts),
        in_specs=[pl.BlockSpec((ts, K), lambda i, s: (jnp.where(i == 0, s, 0), 0)),
                  pl.BlockSpec((ts, K), lambda i, s: (jnp.where(i == 1, s, 0), 0)),
                  pl.BlockSpec((ts, N), lambda i, s: (s, 0))],
        out_specs=pl.BlockSpec((K, N), lambda i, s: (i, 0)),
        out_shape=jax.ShapeDtypeStruct((2 * K, N), F32),
        compiler_params=_cp(("parallel", "arbitrary")), name=name,
    )(a0, a1, g)


def _swiglu(gt, up):
    return gt * _sigmoid(gt) * up


def _ffn_up(h, w4, name):
    S, D = h.shape
    ns, _, C = w4.shape
    hs = ns // 2
    tm = _div(S, 256, 8)

    def body(h_ref, w_ref, z_ref, a_ref):
        hv = h_ref[...]
        for s in range(hs):
            gt = _dot(hv, w_ref[s])
            up = _dot(hv, w_ref[hs + s])
            sg = _sigmoid(gt)
            silu = gt * sg
            z_ref[:, s * C:(s + 1) * C] = (up * (sg * (1.0 + gt * (1.0 - sg)))).astype(BF16)
            z_ref[:, (hs + s) * C:(hs + s + 1) * C] = silu.astype(BF16)
            a_ref[:, s * C:(s + 1) * C] = (silu * up).astype(BF16)

    return pl.pallas_call(
        body, grid=(S // tm,), in_specs=[pl.BlockSpec((tm, D), lambda i: (i, 0)), _resident((ns, D, C), lambda i: (0, 0, 0))],
        out_specs=[pl.BlockSpec((tm, ns * C), lambda i: (i, 0)), pl.BlockSpec((tm, hs * C), lambda i: (i, 0))],
        out_shape=[jax.ShapeDtypeStruct((S, ns * C), BF16), jax.ShapeDtypeStruct((S, hs * C), BF16)],
        compiler_params=_cp(("parallel",)), name=name,
    )(h, w4)


def _norm_rows(xv, g, shift, scale):
    r = lax.rsqrt(jnp.mean(xv * xv, axis=-1, keepdims=True) + EPS)
    return (xv * r) * g * (1.0 + scale) + shift


def _resid_outputs(y, x_ref, gv_ref, nxt_refs, out_refs):
    out_refs[0][...] = y
    xn = x_ref[...] + gv_ref[...] * y
    out_refs[1][...] = xn
    if nxt_refs:
        out_refs[2][...] = _norm_rows(xn, nxt_refs[0][...], nxt_refs[1][...], nxt_refs[2][...]).astype(BF16)


def _ffn_down(act, wo, x, gv, nxt, name):
    S = act.shape[0]
    Fd, D = wo.shape
    tm = _div(S, 512, 8)
    nn = 3 if nxt else 0

    def body(a_ref, w_ref, x_ref, gv_ref, *rest):
        _resid_outputs(_dot(a_ref[...], w_ref[...]), x_ref, gv_ref, rest[:nn], rest[nn:])

    row = pl.BlockSpec((tm, D), lambda i: (i, 0))
    vec = pl.BlockSpec((1, D), lambda i: (0, 0))
    os_ = jax.ShapeDtypeStruct((S, D), F32)
    return pl.pallas_call(
        body, grid=(S // tm,),
        in_specs=[pl.BlockSpec((tm, Fd), lambda i: (i, 0)), _resident((Fd, D), lambda i: (0, 0)), row, vec] + [vec] * nn,
        out_specs=[row, row] + [row] * (nn // 3), out_shape=[os_, os_] + [jax.ShapeDtypeStruct((S, D), BF16)] * (nn // 3),
        compiler_params=_cp(("parallel",)), name=name,
    )(act, wo, x, gv, *(nxt or ()))


def _dw_ffn_in(h, dz, ns, name):
    S, D = h.shape
    C = dz.shape[1] // ns
    ts = _div(S, TN_ROWS, 8)

    def body(h_ref, g_ref, o_ref):
        @pl.when(pl.program_id(1) == 0)
        def _():
            o_ref[...] = jnp.zeros_like(o_ref)

        o_ref[...] += _dot_tn(h_ref[...], g_ref[...])

    return pl.pallas_call(
        body, grid=(ns, S // ts),
        in_specs=[pl.BlockSpec((ts, D), lambda j, s: (s, 0)), pl.BlockSpec((ts, C), lambda j, s: (s, j))],
        out_specs=pl.BlockSpec((None, D, C), lambda j, s: (j, 0, 0)), out_shape=jax.ShapeDtypeStruct((ns, D, C), F32),
        compiler_params=_cp(("parallel", "arbitrary")), name=name,
    )(h, dz)


def _dh_normmod_bwd(pairs, x, dres, g, scale, prev, name):
    S, D = x.shape
    tm = _div(S, 256, 8)
    n = len(pairs)
    with_prev = prev is not None

    def body(*refs):
        refs = list(refs)
        mm = refs[:2 * n]
        x_ref, dr_ref, g_ref, sc_ref = refs[2 * n:2 * n + 4]
        outs = refs[2 * n + 4 + 2 * with_prev:]

        @pl.when(pl.program_id(0) == 0)
        def _():
            for o in outs[1:4] + outs[5:]:
                o[...] = jnp.zeros_like(o)

        dh = _dot_nt(mm[0][...], mm[1][...])
        for p in range(1, n):
            dh = dh + _dot_nt(mm[2 * p][...], mm[2 * p + 1][...])
        pv = (refs[2 * n + 4][...], refs[2 * n + 5][...]) if with_prev else None
        res = _normmod_bwd_rows(x_ref[...], dh, dr_ref[...], g_ref[...], sc_ref[...], pv)
        outs[0][...] = res[0]
        for o, v in zip(outs[1:4], res[1:4]):
            o[...] += v
        if with_prev:
            outs[4][...] = res[4]
            outs[5][...] += res[5]

    row = pl.BlockSpec((tm, D), lambda i: (i, 0))
    vec = pl.BlockSpec((1, D), lambda i: (0, 0))
    ins, args = [], []
    for a, b, a_blk, b_blk, k in pairs:
        ins.append(pl.BlockSpec((tm, k), lambda i, a_blk=a_blk: (i, a_blk)))
        ins.append(_resident((None, D, k), lambda i, b_blk=b_blk: (b_blk, 0, 0)) if b.ndim == 3 else _resident((D, k), lambda i, b_blk=b_blk: (0, b_blk)))
        args += [a, b]
    ins += [row, row, vec, vec] + [row, vec] * with_prev
    args += [x, dres, g, scale] + (list(prev) if with_prev else [])
    vs = jax.ShapeDtypeStruct((1, D), F32)
    return pl.pallas_call(
        body, grid=(S // tm,), in_specs=ins, out_specs=[row, vec, vec, vec] + [row, vec] * with_prev,
        out_shape=[jax.ShapeDtypeStruct((S, D), F32), vs, vs, vs] + [jax.ShapeDtypeStruct((S, D), BF16), vs] * with_prev,
        compiler_params=_cp(("arbitrary",)), name=name,
    )(*args)


def _ffn_bwd_core(dy, wo, z, w4, x, dres, g, scale, prev, name):
    S, D = x.shape
    Fd = wo.shape[0]
    ns, _, C = w4.shape
    tm = _div(S, 256, 8)
    with_prev = prev is not None

    def body(dy_ref, wo_ref, q_ref, p_ref, w4_ref, x_ref, dr_ref, g_ref, sc_ref, *rest):
        outs = rest[2 * with_prev:]
        dz_ref, outs = outs[0], outs[1:]

        @pl.when(pl.program_id(0) == 0)
        def _():
            for o in outs[1:4] + outs[5:]:
                o[...] = jnp.zeros_like(o)

        da = _dot_nt(dy_ref[...], wo_ref[...])
        dz_ref[:, :Fd] = (da * q_ref[...].astype(F32)).astype(BF16)
        dz_ref[:, Fd:] = (da * p_ref[...].astype(F32)).astype(BF16)
        dh = _dot_nt(dz_ref[:, 0:C], w4_ref[0])
        for s in range(1, ns):
            dh = dh + _dot_nt(dz_ref[:, s * C:(s + 1) * C], w4_ref[s])
        pv = (rest[0][...], rest[1][...]) if with_prev else None
        res = _normmod_bwd_rows(x_ref[...], dh, dr_ref[...], g_ref[...], sc_ref[...], pv)
        outs[0][...] = res[0]
        for o, v in zip(outs[1:4], res[1:4]):
            o[...] += v
        if with_prev:
            outs[4][...] = res[4]
            outs[5][...] += res[5]

    row = pl.BlockSpec((tm, D), lambda i: (i, 0))
    vec = pl.BlockSpec((1, D), lambda i: (0, 0))
    wide = pl.BlockSpec((tm, 2 * Fd), lambda i: (i, 0))
    vs = jax.ShapeDtypeStruct((1, D), F32)
    return pl.pallas_call(
        body, grid=(S // tm,),
        in_specs=[row, _resident((Fd, D), lambda i: (0, 0)), pl.BlockSpec((tm, Fd), lambda i: (i, 0)), pl.BlockSpec((tm, Fd), lambda i: (i, 1)),
                  _resident((ns, D, C), lambda i: (0, 0, 0)), row, row, vec, vec] + [row, vec] * with_prev,
        out_specs=[wide, row, vec, vec, vec] + [row, vec] * with_prev,
        out_shape=[jax.ShapeDtypeStruct((S, 2 * Fd), BF16), jax.ShapeDtypeStruct((S, D), F32), vs, vs, vs] + [jax.ShapeDtypeStruct((S, D), BF16), vs] * with_prev,
        compiler_params=_cp(("arbitrary",)), name=name,
    )(dy, wo, z, z, w4, x, dres, g, scale, *(prev or ()))


def _mix_out(yconv, ygla, wout, x, gv, nxt, name):
    S, Kc = yconv.shape
    Kg = ygla.shape[1]
    D = wout.shape[1]
    tm = _div(S, 512, 8)
    nn = 3 if nxt else 0

    def body(a_ref, b_ref, w_ref, x_ref, gv_ref, *rest):
        y = _dot(a_ref[...], w_ref[0:Kc, :]) + _dot(b_ref[...], w_ref[Kc:Kc + Kg, :])
        _resid_outputs(y, x_ref, gv_ref, rest[:nn], rest[nn:])

    row = pl.BlockSpec((tm, D), lambda i: (i, 0))
    vec = pl.BlockSpec((1, D), lambda i: (0, 0))
    os_ = jax.ShapeDtypeStruct((S, D), F32)
    return pl.pallas_call(
        body, grid=(S // tm,),
        in_specs=[pl.BlockSpec((tm, Kc), lambda i: (i, 0)), pl.BlockSpec((tm, Kg), lambda i: (i, 0)), _resident((Kc + Kg, D), lambda i: (0, 0)), row,
                  vec] + [vec] * nn,
        out_specs=[row, row] + [row] * (nn // 3), out_shape=[os_, os_] + [jax.ShapeDtypeStruct((S, D), BF16)] * (nn // 3),
        compiler_params=_cp(("parallel",)), name=name,
    )(yconv, ygla, wout, x, gv, *(nxt or ()))


def _ln_parts(yc, g, b):
    mu = jnp.mean(yc, axis=-1, keepdims=True)
    xc = yc - mu
    rs = lax.rsqrt(jnp.mean(xc * xc, axis=-1, keepdims=True) + EPS)
    xh = xc * rs
    return xh, rs, xh * g + b


SUB = 8
CONV_ROWS = 32


def _shifted_copies(ext8, rows):
    for b in range(1, SUB):
        ext8[b, pl.ds(0, rows - SUB), :] = ext8[0, pl.ds(b, rows - SUB), :]


def _tap(o):
    return o % SUB, o - o % SUB


def _conv_fwd(z, w_dw, b_dw, g_ln, b_ln, name):
    S = z.shape[0]
    W, C = w_dw.shape
    ts = _div(S, 512, HALO)
    hb = ts // HALO
    off = HALO - (W - 1)
    ca, cb = ZC_A // C, ZC_B // C
    rb = 2 * CONV_ROWS

    def body(a_ref, b_ref, pa_ref, pb_ref, w_ref, bd_ref, g_ref, bl_ref, u_ref, yc_ref, o_ref, ext8):
        keep = (pl.program_id(0) > 0).astype(F32)
        u = a_ref[...] * _sigmoid(b_ref[...])
        ext8[0, pl.ds(0, HALO), :] = pa_ref[...] * _sigmoid(pb_ref[...]) * keep
        ext8[0, pl.ds(HALO, ts), :] = u
        u_ref[...] = u
        _shifted_copies(ext8, ts + HALO)

        for lg in range(C // LANE):
            lanes = pl.ds(lg * LANE, LANE)
            taps = [jnp.broadcast_to(w_ref[pl.ds(j, 1), lanes], (SUB, LANE)) for j in range(W)]
            bias = jnp.broadcast_to(bd_ref[:, lanes], (SUB, LANE))

            def sub(i, carry, lanes=lanes, taps=taps, bias=bias):
                r0 = pl.multiple_of(i * rb, rb)
                accs = [bias] * (rb // SUB)
                for j in range(W):
                    b, a = _tap(off + j)
                    for r in range(rb // SUB):
                        accs[r] = accs[r] + taps[j] * ext8[b, pl.ds(r0 + a + r * SUB, SUB), lanes]
                for r in range(rb // SUB):
                    yc_ref[pl.ds(r0 + r * SUB, SUB), lanes] = accs[r]
                return carry

            lax.fori_loop(0, ts // rb, sub, 0)
        _, _, ln = _ln_parts(yc_ref[...], g_ref[...], bl_ref[...])
        o_ref[...] = (ln * _sigmoid(ln)).astype(BF16)

    cur = lambda col: pl.BlockSpec((ts, C), lambda i: (i, col))
    prev = lambda col: pl.BlockSpec((HALO, C), lambda i: (jnp.maximum(i * hb - 1, 0), col))
    vec = pl.BlockSpec((1, C), lambda i: (0, 0))
    row = pl.BlockSpec((ts, C), lambda i: (i, 0))
    fs = jax.ShapeDtypeStruct((S, C), F32)
    return pl.pallas_call(
        body, grid=(S // ts,),
        in_specs=[cur(ca), cur(cb), prev(ca), prev(cb), pl.BlockSpec((W, C), lambda i: (0, 0)), vec, vec, vec],
        out_specs=[row, row, row], out_shape=[fs, fs, jax.ShapeDtypeStruct((S, C), BF16)],
        scratch_shapes=[pltpu.VMEM((SUB, ts + HALO, C), F32)],
        compiler_params=_cp(("parallel",)), name=name,
    )(z, z, z, z, w_dw, b_dw, g_ln, b_ln)


def _conv_bwd(dycat, z, u, yc, w_dw, g_ln, b_ln, name):
    S = z.shape[0]
    W, C = w_dw.shape
    ts = _div(S, 512, HALO)
    hb = ts // HALO
    nblk = S // ts
    off = HALO - (W - 1)
    ca, cb = ZC_A // C, ZC_B // C
    rb = CONV_ROWS

    def ln_silu_bwd(dy, ycv, g, b):
        xh, rs, ln = _ln_parts(ycv, g, b)
        sl = _sigmoid(ln)
        dln = dy * (sl * (1.0 + ln * (1.0 - sl)))
        dxh = dln * g
        dyc = rs * (dxh - jnp.mean(dxh, axis=-1, keepdims=True) - xh * jnp.mean(dxh * xh, axis=-1, keepdims=True))
        return dyc, dln, xh

    def body(dy_ref, ndy_ref, yc_ref, nyc_ref, u_ref, pu_ref, a_ref, b_ref, w_ref, g_ref, bl_ref,
             dab_ref, dw_ref, dbd_ref, dg_ref, dbl_ref, uext8, dext8, dwacc):
        i = pl.program_id(0)

        @pl.when(i == 0)
        def _():
            dwacc[...] = jnp.zeros_like(dwacc)
            dbd_ref[...] = jnp.zeros_like(dbd_ref)
            dg_ref[...] = jnp.zeros_like(dg_ref)
            dbl_ref[...] = jnp.zeros_like(dbl_ref)

        g = g_ref[...]
        bl = bl_ref[...]
        dyc, dln, xh = ln_silu_bwd(dy_ref[...], yc_ref[...], g, bl)
        ndyc, _, _ = ln_silu_bwd(ndy_ref[...], nyc_ref[...], g, bl)
        dg_ref[...] += jnp.sum(dln * xh, axis=0, keepdims=True)
        dbl_ref[...] += jnp.sum(dln, axis=0, keepdims=True)
        dbd_ref[...] += jnp.sum(dyc, axis=0, keepdims=True)
        dext8[0, pl.ds(0, ts), :] = dyc
        dext8[0, pl.ds(ts, HALO), :] = ndyc * (i < nblk - 1).astype(F32)
        uext8[0, pl.ds(0, HALO), :] = pu_ref[...] * (i > 0).astype(F32)
        uext8[0, pl.ds(HALO, ts), :] = u_ref[...]
        _shifted_copies(dext8, ts + HALO)
        _shifted_copies(uext8, ts + HALO)

        def sub(k, carry):
            r0 = pl.multiple_of(k * rb, rb)
            rows = pl.ds(r0, rb)
            dyt = dext8[0, rows, :]
            du = jnp.zeros((rb, C), F32)
            for j in range(W):
                b, a = _tap(W - 1 - j)
                du = du + w_ref[pl.ds(j, 1), :] * dext8[b, pl.ds(r0 + a, rb), :]
                b, a = _tap(off + j)
                p = dyt * uext8[b, pl.ds(r0 + a, rb), :]
                part = p[0:SUB]
                for q in range(1, rb // SUB):
                    part = part + p[q * SUB:(q + 1) * SUB]
                dwacc[j] += part
            sb = _sigmoid(b_ref[rows, :])
            dab_ref[rows, 0:C] = (du * sb).astype(BF16)
            dab_ref[rows, C:2 * C] = (du * a_ref[rows, :] * sb * (1.0 - sb)).astype(BF16)
            return carry

        lax.fori_loop(0, ts // rb, sub, 0)

        @pl.when(i == nblk - 1)
        def _():
            for j in range(W):
                dw_ref[pl.ds(j, 1), :] = jnp.sum(dwacc[j], axis=0, keepdims=True)

    row = pl.BlockSpec((ts, C), lambda i: (i, 0))
    nxt = pl.BlockSpec((HALO, C), lambda i: (jnp.minimum((i + 1) * hb, S // HALO - 1), 0))
    prv = pl.BlockSpec((HALO, C), lambda i: (jnp.maximum(i * hb - 1, 0), 0))
    vec = pl.BlockSpec((1, C), lambda i: (0, 0))
    wsp = pl.BlockSpec((W, C), lambda i: (0, 0))
    vs = jax.ShapeDtypeStruct((1, C), F32)
    return pl.pallas_call(
        body, grid=(nblk,),
        in_specs=[row, nxt, row, nxt, row, prv, pl.BlockSpec((ts, C), lambda i: (i, ca)), pl.BlockSpec((ts, C), lambda i: (i, cb)), wsp, vec, vec],
        out_specs=[pl.BlockSpec((ts, 2 * C), lambda i: (i, 0)), wsp, vec, vec, vec],
        out_shape=[jax.ShapeDtypeStruct((S, 2 * C), BF16), jax.ShapeDtypeStruct((W, C), F32), vs, vs, vs],
        scratch_shapes=[pltpu.VMEM((SUB, ts + HALO, C), F32), pltpu.VMEM((SUB, ts + HALO, C), F32), pltpu.VMEM((W, SUB, C), F32)],
        compiler_params=_cp(("arbitrary",)), name=name,
    )(dycat, dycat, yc, yc, u, u, z, z, w_dw, g_ln, b_ln)


def _log_gate(zg):
    return (jnp.minimum(zg, 0.0) - jnp.log(1.0 + jnp.exp(-jnp.abs(zg)))) * (1.0 / GATE_TAU)


def _loggate(z, wgp, bgp, name):
    S = z.shape[0]
    N = wgp.shape[1]
    ts = _div(S, 512, 8)

    def body(g_ref, w_ref, b_ref, o_ref):
        o_ref[...] = _log_gate(_dot(g_ref[...], w_ref[...]) + b_ref[...])

    return pl.pallas_call(
        body, grid=(S // ts,),
        in_specs=[pl.BlockSpec((ts, LANE), lambda i: (i, ZC_G // LANE)), pl.BlockSpec((LANE, N), lambda i: (0, 0)), pl.BlockSpec((1, N), lambda i: (0, 0))],
        out_specs=pl.BlockSpec((ts, N), lambda i: (i, 0)), out_shape=jax.ShapeDtypeStruct((S, N), F32),
        compiler_params=_cp(("parallel",)), name=name,
    )(z, wgp, bgp)


def _loggate_bwd(dla, z, wgp, wgp_t, bgp, name):
    S = z.shape[0]
    N = wgp.shape[1]
    ts = _div(S, 512, 8)

    def body(dla_ref, g_ref, w_ref, wt_ref, b_ref, dg_ref, dw_ref, db_ref):
        @pl.when(pl.program_id(0) == 0)
        def _():
            dw_ref[...] = jnp.zeros_like(dw_ref)
            db_ref[...] = jnp.zeros_like(db_ref)

        glr = g_ref[...]
        zg = _dot(glr, w_ref[...]) + b_ref[...]
        dzg = dla_ref[...] * (1.0 / GATE_TAU) * (1.0 - _sigmoid(zg))
        dg_ref[...] = _dot(dzg, wt_ref[...]).astype(BF16)
        dw_ref[...] += _dot_tn(glr, dzg)
        db_ref[...] += jnp.sum(dzg, axis=0, keepdims=True)

    return pl.pallas_call(
        body, grid=(S // ts,),
        in_specs=[pl.BlockSpec((ts, N), lambda i: (i, 0)), pl.BlockSpec((ts, LANE), lambda i: (i, ZC_G // LANE)),
                  pl.BlockSpec((LANE, N), lambda i: (0, 0)), pl.BlockSpec((N, LANE), lambda i: (0, 0)), pl.BlockSpec((1, N), lambda i: (0, 0))],
        out_specs=[pl.BlockSpec((ts, LANE), lambda i: (i, 0)), pl.BlockSpec((LANE, N), lambda i: (0, 0)), pl.BlockSpec((1, N), lambda i: (0, 0))],
        out_shape=[jax.ShapeDtypeStruct((S, LANE), BF16), jax.ShapeDtypeStruct((LANE, N), F32), jax.ShapeDtypeStruct((1, N), F32)],
        compiler_params=_cp(("arbitrary",)), name=name,
    )(dla, z, wgp, wgp_t, bgp)


def _bdot(a, b, ca, cb):
    return lax.dot_general(a.astype(BF16), b.astype(BF16), (((ca,), (cb,)), ((0,), (0,))), preferred_element_type=F32)


def _bdot_exact(a, b):
    return lax.dot_general(a, b, (((2,), (1,)), ((0,), (0,))), preferred_element_type=F32, precision=lax.Precision.HIGHEST)


def _tiles(ref, cpb):
    return jnp.stack([ref[pl.ds(c * CHUNK, CHUNK), pl.ds(h * LANE, LANE)] for c in range(cpb) for h in range(HEADS)])


def _tri_masks(n):
    ri = lax.broadcasted_iota(jnp.int32, (n, CHUNK, CHUNK), 1)
    ci = lax.broadcasted_iota(jnp.int32, (n, CHUNK, CHUNK), 2)
    return ri >= ci, (ri >= ci).astype(F32), (ri <= ci).astype(F32)


def _chunk_fwd_terms(q, k, la, tril):
    bc = _bdot_exact(tril, la)
    bend = jnp.sum(la, axis=1, keepdims=True)
    eb = jnp.exp(bc)
    enb = jnp.exp(-bc)
    ee = jnp.exp(bend - bc)
    qs = q * (DK ** -0.5)
    return bend, eb, enb, ee, qs * eb, qs * enb, k * enb, k * eb, k * ee


def _gla_fwd(z, la, gn, name):
    S = z.shape[0]
    W = HEADS * LANE
    tb = _div(S, 512, CHUNK)
    cpb = tb // CHUNK

    def body(q_ref, k_ref, v_ref, r_ref, la_ref, gn_ref, o_ref, sp_ref, y_ref, st):
        @pl.when(pl.program_id(0) == 0)
        def _():
            st[...] = jnp.zeros_like(st)

        tri, tril, _ = _tri_masks(cpb * HEADS)
        q, k, v, rv, lav = (_tiles(r, cpb) for r in (q_ref, k_ref, v_ref, r_ref, la_ref))
        bend, _, _, _, qf, qb, kb, kf, ke = _chunk_fwd_terms(q, k, lav, tril)
        att = jnp.where(tri, _bdot(qf, kb, 2, 2), _bdot(qb, kf, 2, 2))
        o_intra = _bdot(att, v, 2, 1)
        u = _bdot(v, ke, 1, 1)
        gdec = jnp.exp(bend)
        s_prev = [None] * (cpb * HEADS)
        for h in range(HEADS):
            s = st[h]
            for c in range(cpb):
                b = c * HEADS + h
                s_prev[b] = s
                s = s * gdec[b] + u[b]
            st[h] = s
        s_prev = jnp.stack(s_prev)
        o = o_intra + _bdot(qf, s_prev, 2, 2)
        rms = lax.rsqrt(jnp.mean(o * o, axis=-1, keepdims=True) + EPS)
        gn = jnp.stack([gn_ref[pl.ds(h, 1), :] for _ in range(cpb) for h in range(HEADS)])
        y = (o * rms * gn * (rv * _sigmoid(rv))).astype(BF16)
        for c in range(cpb):
            for h in range(HEADS):
                b = c * HEADS + h
                rows, ln = pl.ds(c * CHUNK, CHUNK), pl.ds(h * LANE, LANE)
                o_ref[rows, ln] = o[b]
                y_ref[rows, ln] = y[b]
                sp_ref[h, c] = s_prev[b]

    zb = lambda base: pl.BlockSpec((tb, W), lambda i: (i, base // W))
    hb_ = pl.BlockSpec((tb, W), lambda i: (i, 0))
    return pl.pallas_call(
        body, grid=(S // tb,),
        in_specs=[zb(ZC_Q), zb(ZC_K), zb(ZC_V), zb(ZC_R), hb_, pl.BlockSpec((HEADS, DV), lambda i: (0, 0))],
        out_specs=[hb_, pl.BlockSpec((HEADS, cpb, DV, DKP), lambda i: (0, i, 0, 0)), hb_],
        out_shape=[jax.ShapeDtypeStruct((S, D_GLA), F32), jax.ShapeDtypeStruct((HEADS, S // CHUNK, DV, DKP), F32),
                   jax.ShapeDtypeStruct((S, D_GLA), BF16)],
        scratch_shapes=[pltpu.VMEM((HEADS, DV, DKP), F32)],
        compiler_params=_cp(("arbitrary",)), name=name,
    )(z, z, z, z, la, gn)


def _gla_bwd(dycat, z, la, o_raw, sprev, gn, name):
    S = z.shape[0]
    W = HEADS * LANE
    tb = _div(S, 512, CHUNK)
    cpb = tb // CHUNK
    nb = S // tb

    def body(q_ref, k_ref, v_ref, r_ref, la_ref, o_ref, sp_ref, dy_ref, gn_ref, dz_ref, dla_ref, dgn_ref, dst):
        @pl.when(pl.program_id(0) == 0)
        def _():
            dst[...] = jnp.zeros_like(dst)
            dgn_ref[...] = jnp.zeros_like(dgn_ref)

        nt = cpb * HEADS
        tri, tril, triu = _tri_masks(nt)
        q, k, v, rv, lav, o, dy = (_tiles(r, cpb) for r in (q_ref, k_ref, v_ref, r_ref, la_ref, o_ref, dy_ref))
        bend, eb, enb, ee, qf, qb, kb, kf, ke = _chunk_fwd_terms(q, k, lav, tril)
        att = jnp.where(tri, _bdot(qf, kb, 2, 2), _bdot(qb, kf, 2, 2))
        s_prev = jnp.stack([sp_ref[h, c] for c in range(cpb) for h in range(HEADS)])
        gdec = jnp.exp(bend)
        gn = jnp.stack([gn_ref[pl.ds(h, 1), :] for _ in range(cpb) for h in range(HEADS)])
        rms = lax.rsqrt(jnp.mean(o * o, axis=-1, keepdims=True) + EPS)
        oh = o * rms
        sg = _sigmoid(rv)
        sr = rv * sg
        d_r = (dy * oh * gn * (sg * (1.0 + rv * (1.0 - sg)))).astype(BF16)
        dgn = jnp.sum(dy * sr * oh, axis=1, keepdims=True)
        w = dy * sr * gn
        do = rms * (w - oh * jnp.mean(w * oh, axis=-1, keepdims=True))
        p = _bdot(do, qf, 1, 1)
        ds = [None] * nt
        for h in range(HEADS):
            s = dst[h]
            for c in reversed(range(cpb)):
                b = c * HEADS + h
                ds[b] = s
                s = s * gdec[b] + p[b]
            dst[h] = s
            dgn_ref[pl.ds(h, 1), :] += sum(dgn[c * HEADS + h] for c in range(cpb))
        ds = jnp.stack(ds)
        datt = _bdot(do, v, 2, 2)
        daf = jnp.where(tri, datt, 0.0)
        dab = jnp.where(tri, 0.0, datt)
        d_v = (_bdot(att, do, 1, 1) + _bdot(ke, ds, 2, 2)).astype(BF16)
        dke = _bdot(v, ds, 2, 1)
        dqf = _bdot(daf, kb, 2, 1) + _bdot(do, s_prev, 2, 1)
        dkb = _bdot(daf, qf, 1, 1)
        dqb = _bdot(dab, kf, 2, 1)
        dkf = _bdot(dab, qb, 1, 1)
        dg = jnp.sum(ds * s_prev, axis=1, keepdims=True)
        d_q = ((dqf * eb + dqb * enb) * (DK ** -0.5)).astype(BF16)
        d_k = (dkb * enb + dkf * eb + dke * ee).astype(BF16)
        dbc = dqf * qf - dkb * kb - dqb * qb + dkf * kf - dke * ke
        dbend = jnp.sum(dke * ke, axis=1, keepdims=True) + dg * gdec
        dla = _bdot_exact(triu, dbc) + dbend
        for c in range(cpb):
            for h in range(HEADS):
                b = c * HEADS + h
                rows = pl.ds(c * CHUNK, CHUNK)
                for base, val in ((ZC_Q, d_q), (ZC_K, d_k), (ZC_V, d_v), (ZC_R, d_r)):
                    dz_ref[rows, pl.ds(base + h * LANE, LANE)] = val[b]
                dla_ref[rows, pl.ds(h * LANE, LANE)] = dla[b]

    zb = lambda base: pl.BlockSpec((tb, W), lambda i: (nb - 1 - i, base // W))
    hb_ = pl.BlockSpec((tb, W), lambda i: (nb - 1 - i, 0))
    return pl.pallas_call(
        body, grid=(nb,),
        in_specs=[zb(ZC_Q), zb(ZC_K), zb(ZC_V), zb(ZC_R), hb_, hb_,
                  pl.BlockSpec((HEADS, cpb, DV, DKP), lambda i: (0, nb - 1 - i, 0, 0)),
                  pl.BlockSpec((tb, W), lambda i: (nb - 1 - i, 1)),
                  pl.BlockSpec((HEADS, DV), lambda i: (0, 0))],
        out_specs=[pl.BlockSpec((tb, Z_GLA), lambda i: (nb - 1 - i, 0)), hb_, pl.BlockSpec((HEADS, DV), lambda i: (0, 0))],
        out_shape=[jax.ShapeDtypeStruct((S, Z_GLA), BF16), jax.ShapeDtypeStruct((S, HEADS * DKP), F32), jax.ShapeDtypeStruct((HEADS, DV), F32)],
        scratch_shapes=[pltpu.VMEM((HEADS, DV, DKP), F32)],
        compiler_params=_cp(("arbitrary",)), name=name,
    )(z, z, z, z, la, o_raw, sprev, dycat, gn)


def _mod_proj(c_all, w3, layer, b, name):
    B, D = c_all.shape
    N = w3.shape[2]
    tn = _div(N, 1024, LANE)

    def body(c_ref, w_ref, b_ref, o_ref):
        cv = c_ref[...]
        o_ref[...] = _dot(cv * _sigmoid(cv), w_ref[...]) + b_ref[...]

    return pl.pallas_call(
        body, grid=(N // tn,),
        in_specs=[pl.BlockSpec((B, D), lambda j: (0, 0)), pl.BlockSpec((None, D, tn), lambda j: (layer, 0, j)), pl.BlockSpec((1, tn), lambda j: (0, j))],
        out_specs=pl.BlockSpec((B, tn), lambda j: (0, j)), out_shape=jax.ShapeDtypeStruct((B, N), F32),
        compiler_params=_cp(("parallel",)), name=name,
    )(c_all, w3, b)


def _mod_wgrad(c_t, dm, name):
    D, B = c_t.shape
    N = dm.shape[1]
    tn = _div(N, 1024, LANE)

    def body(c_ref, d_ref, o_ref):
        cv = c_ref[...]
        ca = cv * _sigmoid(cv)
        acc = ca[:, 0:1] * d_ref[pl.ds(0, 1), :]
        for b in range(1, B):
            acc = acc + ca[:, b:b + 1] * d_ref[pl.ds(b, 1), :]
        o_ref[...] = acc

    return pl.pallas_call(
        body, grid=(N // tn,),
        in_specs=[pl.BlockSpec((D, B), lambda j: (0, 0)), pl.BlockSpec((B, tn), lambda j: (0, j))],
        out_specs=pl.BlockSpec((D, tn), lambda j: (0, j)), out_shape=jax.ShapeDtypeStruct((D, N), F32),
        compiler_params=_cp(("parallel",)), name=name,
    )(c_t, dm)


def _rowsum(xs, name):
    n, N = xs.shape
    tn = _div(N, 8192, LANE)

    def body(x_ref, o_ref):
        acc = x_ref[pl.ds(0, 1), :]
        for r in range(1, n):
            acc = acc + x_ref[pl.ds(r, 1), :]
        o_ref[...] = acc

    return pl.pallas_call(
        body, grid=(N // tn,), in_specs=[pl.BlockSpec((n, tn), lambda j: (0, j))],
        out_specs=pl.BlockSpec((1, tn), lambda j: (0, j)), out_shape=jax.ShapeDtypeStruct((1, N), F32),
        compiler_params=_cp(("parallel",)), name=name,
    )(xs)


def _adamw(w, g, m, v, name, copy_grad=False):
    R, C = w.shape
    tr = _div(R, max(8, (1 << 18) // C), 8)

    def body(w_ref, g_ref, m_ref, v_ref, d_ref, nm_ref, nv_ref, *g_out):
        gv = g_ref[...]
        if copy_grad:
            g_out[0][...] = gv
        mn = ADAM_B1 * m_ref[...] + (1.0 - ADAM_B1) * gv
        vn = ADAM_B2 * v_ref[...] + (1.0 - ADAM_B2) * (gv * gv)
        m_hat = mn / (1.0 - ADAM_B1 ** ADAM_STEP)
        v_hat = vn / (1.0 - ADAM_B2 ** ADAM_STEP)
        d_ref[...] = -ADAM_LR * (m_hat / (jnp.sqrt(v_hat) + ADAM_EPS) + ADAM_WD * w_ref[...])
        nm_ref[...] = mn
        nv_ref[...] = vn

    blk = pl.BlockSpec((tr, C), lambda i: (i, 0))
    os_ = jax.ShapeDtypeStruct((R, C), F32)
    n_out = 4 if copy_grad else 3
    return pl.pallas_call(
        body, grid=(R // tr,), in_specs=[blk] * 4, out_specs=[blk] * n_out, out_shape=[os_] * n_out,
        compiler_params=_cp(("parallel",)), name=name,
    )(w, g, m, v)


def _place():
    return lax.axis_index("x"), lax.axis_index("y"), lax.axis_index("c")


def _other_chips(x, y):
    return [(1 - x, y), (x, 1 - y), (1 - x, 1 - y)]


def _half(c, rows):
    return pl.ds(c * (rows // 2), rows // 2)


_ANY = pl.BlockSpec(memory_space=pl.ANY)


def _ag_small(v, name):
    r, n = v.shape

    def body(v_ref, o_ref, send_sems, recv_sems):
        x, y, c = _place()
        me = 4 * x + 2 * y + c
        o_ref[pl.ds(me, 1)] = v_ref[...][None]
        peers = [(x ^ (k >> 2), y ^ ((k >> 1) & 1), c ^ (k & 1)) for k in range(1, 8)]
        copies = []
        for k, peer in enumerate(peers):
            cp = pltpu.make_async_remote_copy(
                src_ref=v_ref, dst_ref=o_ref.at[me], send_sem=send_sems.at[k], recv_sem=recv_sems.at[k],
                device_id=peer, device_id_type=MESH)
            cp.start()
            copies.append(cp)
        for cp in copies:
            cp.wait()

    return pl.pallas_call(
        body, out_shape=jax.ShapeDtypeStruct((8, r, n), v.dtype),
        in_specs=[pl.BlockSpec(memory_space=pltpu.VMEM)], out_specs=pl.BlockSpec(memory_space=pltpu.VMEM),
        scratch_shapes=[pltpu.SemaphoreType.DMA((7,)), pltpu.SemaphoreType.DMA((7,))],
        compiler_params=pltpu.CompilerParams(vmem_limit_bytes=VMEM_LIMIT), name=name,
    )(v)


def _rs_sibling(gs, name):
    n = len(gs)

    def body(*refs):
        src, out = refs[:n], refs[n:2 * n]
        send_sems, recv_sems = refs[2 * n:]
        x, y, c = _place()
        copies = []
        for i in range(n):
            cp = pltpu.make_async_remote_copy(
                src_ref=src[i].at[:, _half(1 - c, src[i].shape[1])], dst_ref=out[i], send_sem=send_sems.at[i], recv_sem=recv_sems.at[i],
                device_id=(x, y, 1 - c), device_id_type=MESH)
            cp.start()
            copies.append(cp)
        for cp in copies:
            cp.wait()

    return pl.pallas_call(
        body, out_shape=[jax.ShapeDtypeStruct((N_CHIPS, g.shape[1] // 2, g.shape[2]), g.dtype) for g in gs],
        in_specs=[_ANY] * n, out_specs=[_ANY] * n,
        scratch_shapes=[pltpu.SemaphoreType.DMA((n,)), pltpu.SemaphoreType.DMA((n,))],
        compiler_params=pltpu.CompilerParams(has_side_effects=True), name=name,
    )(*gs)


def _rs_presum(g, sib, c_arr, name):
    ns, R, C = g.shape
    rh = R // 2
    tr = _div(rh, max(16, (1 << 19) // C), 16)
    nrb = rh // tr

    def body(c_ref, g_ref, s_ref, o_ref):
        o_ref[...] = (g_ref[...] + s_ref[...]).astype(BF16)

    return pl.pallas_call(
        body, out_shape=jax.ShapeDtypeStruct((ns, rh, C), BF16),
        grid_spec=pltpu.PrefetchScalarGridSpec(
            num_scalar_prefetch=1, grid=(ns, nrb),
            in_specs=[pl.BlockSpec((None, tr, C), lambda s, r, c_ref: (s, c_ref[0] * nrb + r, 0)),
                      pl.BlockSpec((None, tr, C), lambda s, r, c_ref: (s, r, 0))],
            out_specs=pl.BlockSpec((None, tr, C), lambda s, r, c_ref: (s, r, 0))),
        compiler_params=_cp(("parallel", "parallel")), name=name,
    )(c_arr, g, sib)


def _rs_sum(g, sib, recv, full, layer, sc_arr, name):
    ns, R, C = g.shape
    rh = R // 2
    tr = _div(rh, max(16, (1 << 18) // C), 16)
    nrb = rh // tr

    def body(sc_ref, g_ref, s_ref, r_ref, f_ref, o_ref):
        acc = g_ref[...] + s_ref[...]
        for j in range(3):
            acc = acc + r_ref[j].astype(F32)
        o_ref[...] = acc

    return pl.pallas_call(
        body, out_shape=jax.ShapeDtypeStruct(full.shape, F32),
        grid_spec=pltpu.PrefetchScalarGridSpec(
            num_scalar_prefetch=1, grid=(nrb,),
            in_specs=[pl.BlockSpec((None, tr, C), lambda r, sc: (sc[0], sc[1] * nrb + r, 0)),
                      pl.BlockSpec((None, tr, C), lambda r, sc: (sc[0], r, 0)),
                      pl.BlockSpec((3, tr, C), lambda r, sc: (0, r, 0)),
                      _ANY],
            out_specs=pl.BlockSpec((None, tr, C), lambda r, sc: (layer, sc[1] * nrb + r, 0))),
        input_output_aliases={4: 0},
        compiler_params=_cp(("parallel",)), name=name,
    )(sc_arr, g, sib, recv, full)


def _rs_share(fulls, layer, name):
    n = len(fulls)

    def body(*refs):
        src, out = refs[:n], refs[n:2 * n]
        send_sems, recv_sems = refs[2 * n:]
        x, y, c = _place()
        copies = []
        for i in range(n):
            rows = out[i].shape[1]
            cp = pltpu.make_async_remote_copy(
                src_ref=out[i].at[layer, _half(c, rows)], dst_ref=out[i].at[layer, _half(c, rows)],
                send_sem=send_sems.at[i], recv_sem=recv_sems.at[i], device_id=(x, y, 1 - c), device_id_type=MESH)
            cp.start()
            copies.append(cp)
        for cp in copies:
            cp.wait()

    return pl.pallas_call(
        body, out_shape=[jax.ShapeDtypeStruct(f.shape, f.dtype) for f in fulls],
        in_specs=[_ANY] * n, out_specs=[_ANY] * n, input_output_aliases={i: i for i in range(n)},
        scratch_shapes=[pltpu.SemaphoreType.DMA((n,)), pltpu.SemaphoreType.DMA((n,))],
        compiler_params=pltpu.CompilerParams(has_side_effects=True), name=name,
    )(*fulls)


_HBM = pl.BlockSpec(memory_space=pltpu.HBM)
_SEM = pl.BlockSpec(memory_space=pltpu.SEMAPHORE)
_EFFECT = pltpu.SideEffectType.DATAFLOW_SIDE_EFFECTING


def _in_hbm(a):
    return pltpu.with_memory_space_constraint(a, pltpu.HBM)


def _split_start(bufs, n_sem, copies_of, name):
    nb = len(bufs)

    def body(*refs):
        for cp in copies_of(refs[:nb], refs[nb], refs[nb + 1]):
            cp.start()
        refs[-1][...] = jnp.zeros_like(refs[-1])

    out = pl.pallas_call(
        body, name=name,
        out_shape=(pltpu.SemaphoreType.DMA((n_sem,)), pltpu.SemaphoreType.DMA((n_sem,)), *[pltpu.HBM(a.shape, a.dtype) for a in bufs],
                   jax.ShapeDtypeStruct((SUB, LANE), F32)),
        in_specs=[_HBM] * nb, out_specs=(_SEM, _SEM, *([_HBM] * nb), pl.BlockSpec(memory_space=pltpu.VMEM)),
        input_output_aliases={i: 2 + i for i in range(nb)},
        compiler_params=pltpu.CompilerParams(has_side_effects=_EFFECT),
    )(*[_in_hbm(a) for a in bufs])
    return out[0], out[1], list(out[2:2 + nb]), out[-1]


def _split_wait(send_sems, recv_sems, bufs, after, copies_of, name):
    nb = len(bufs)

    def body(*refs):
        for cp in copies_of(refs[:nb], refs[nb], refs[nb + 1]):
            cp.wait_send()
            cp.wait_recv()

    return list(pl.pallas_call(
        body, name=name, out_shape=[pltpu.HBM(a.shape, a.dtype) for a in bufs],
        in_specs=[_HBM] * nb + [_SEM, _SEM, _ANY], out_specs=[_HBM] * nb,
        input_output_aliases={i: i for i in range(nb)},
        compiler_params=pltpu.CompilerParams(has_side_effects=_EFFECT),
    )(*bufs, send_sems, recv_sems, after))


def _ag_half_copies(land, send_sems, recv_sems, landing_of_mine):
    x, y, c = _place()
    cps = []
    for j, (cx, cy) in enumerate(_other_chips(x, y)):
        for i in range(len(land)):
            rows = _half(c, land[i].shape[1])
            s = 2 * x + y if landing_of_mine else 2 * cx + cy
            cps.append(pltpu.make_async_remote_copy(
                src_ref=land[i].at[2 * x + y, rows], dst_ref=land[i].at[s, rows], send_sem=send_sems.at[3 * i + j], recv_sem=recv_sems.at[3 * i + j],
                device_id=(cx, cy, c), device_id_type=MESH))
    return cps


def _ag_starts(land, send_sems, recv_sems):
    return _ag_half_copies(land, send_sems, recv_sems, True)


def _ag_waits(land, send_sems, recv_sems):
    return _ag_half_copies(land, send_sems, recv_sems, False)


def _ag_finish(lands, name):
    n = len(lands)

    def body(*refs):
        land = refs[n:2 * n]
        send_sems, recv_sems = refs[2 * n:]
        x, y, c = _place()
        sibling = (x, y, 1 - c)

        def copy(k, i, s, h):
            blk = land[i].at[s, _half(h, land[i].shape[1])]
            return pltpu.make_async_remote_copy(
                src_ref=blk, dst_ref=blk, send_sem=send_sems.at[k], recv_sem=recv_sems.at[k], device_id=sibling, device_id_type=MESH)

        chips = _other_chips(x, y)
        passed = [copy(3 * i + j, i, 2 * cx + cy, c) for j, (cx, cy) in enumerate(chips) for i in range(n)]
        for cp in passed:
            cp.start()
        for j, (cx, cy) in enumerate(chips):
            for i in range(n):
                copy(3 * i + j, i, 2 * cx + cy, 1 - c).wait_recv()
        for cp in passed:
            cp.wait_send()

    return pl.pallas_call(
        body, out_shape=[jax.ShapeDtypeStruct(a.shape, a.dtype) for a in lands],
        in_specs=[_ANY] * n, out_specs=[_ANY] * n, input_output_aliases={i: i for i in range(n)},
        scratch_shapes=[pltpu.SemaphoreType.DMA((3 * n,)), pltpu.SemaphoreType.DMA((3 * n,))],
        compiler_params=pltpu.CompilerParams(has_side_effects=True), name=name,
    )(*lands)


def _rs_chip_copies(bufs, send_sems, recv_sems):
    n = len(bufs) // 2
    x, y, c = _place()
    return [pltpu.make_async_remote_copy(
        src_ref=bufs[i].at[2 * cx + cy], dst_ref=bufs[n + i].at[j], send_sem=send_sems.at[3 * i + j], recv_sem=recv_sems.at[3 * i + j],
        device_id=(cx, cy, c), device_id_type=MESH) for j, (cx, cy) in enumerate(_other_chips(x, y)) for i in range(n)]


def _pad_heads(w):
    lead = w.shape[:-1]
    w4 = w.reshape(*lead, HEADS, DK)
    w4 = jnp.pad(w4, [(0, 0)] * len(lead) + [(0, 0), (0, DKP - DK)])
    return w4.reshape(*lead, HEADS * DKP)


def _unpad_heads(w):
    lead = w.shape[:-1]
    return w.reshape(*lead, HEADS, DKP)[..., :DK].reshape(*lead, HEADS * DK)


def _mix_weight(win4, n_cols):
    D = win4.shape[1]
    w = jnp.transpose(win4[:, :, :n_cols], (1, 0, 2)).reshape(D, N_CHIPS * n_cols)
    o = 2 * D_CONV
    hk = HEADS * DK
    ab = w[:, :o]
    q = _pad_heads(w[:, o:o + hk])
    k = _pad_heads(w[:, o + hk:o + 2 * hk])
    vr = w[:, o + 2 * hk:o + 2 * hk + 2 * D_GLA]
    glr = jnp.pad(w[:, o + 2 * hk + 2 * D_GLA:], ((0, 0), (0, LANE - GATE_RANK)))
    return jnp.concatenate([q, k, vr, ab, glr], axis=1)


def _mix_weight_grad(dgla, dab, dglr, n_cols, n_pad):
    D = dab.shape[0]
    hkp = HEADS * DKP
    w = jnp.concatenate([dab, _unpad_heads(dgla[:, :hkp]), _unpad_heads(dgla[:, hkp:2 * hkp]), dgla[:, 2 * hkp:], dglr[:, :GATE_RANK]], axis=1)
    w = jnp.pad(w.reshape(D, N_CHIPS, n_cols), ((0, 0), (0, 0), (0, n_pad - n_cols)))
    return jnp.transpose(w, (1, 0, 2))


_ARG_NAMES = ['x', 'c', 'w_ada', 'b_ada', 'g_norm_ffn1', 'w_ffn1_in', 'w_ffn1_out', 'g_norm_mix', 'w_in', 'w_dw', 'b_dw', 'g_conv_ln', 'b_conv_ln', 'w_gate_up', 'b_gate', 'g_gla_norm', 'w_out', 'g_norm_ffn2', 'w_ffn2_in', 'w_ffn2_out', 'g_norm_final', 'w_ada_final', 'b_ada_final']
_WEIGHTS = _ARG_NAMES[2:]
_BIG = ('w_ffn1_in', 'w_ffn1_out', 'w_in', 'w_out', 'w_ffn2_in', 'w_ffn2_out')
_SMALL = ('g_norm_ffn1', 'g_norm_mix', 'w_dw', 'b_dw', 'g_conv_ln', 'b_conv_ln', 'w_gate_up', 'b_gate', 'g_gla_norm', 'g_norm_ffn2', 'g_norm_final')


def _ffn_fwd(x, h, gv, w4, wo, nxt, tag):
    z, act = _ffn_up(h, w4, f"ffn_up_{tag}")
    y, xn, *hn = _ffn_down(act, wo, x, gv, nxt, f"ffn_down_{tag}")
    return xn, (hn[0] if hn else None), y, (x, h, z, act)


def _ffn_bwd(dxn, dy, saved, g, scale, prev, w4, wo, tag):
    x, h, z, act = saved
    ns = w4.shape[0]
    dwo = _mm_tn(act, dy, f"dw_out_{tag}")
    dz, dx, dsh, dsc, dg, *pv = _ffn_bwd_core(dy, wo, z, w4, x, dxn, g, scale, prev, f"ffn_bwd_{tag}")
    dwi = _dw_ffn_in(h, dz, ns, f"dw_in_{tag}")
    return dx, pv, dict(dshift=dsh, dscale=dsc, dg=dg, dw_in=dwi, dw_out=dwo.reshape(N_CHIPS, -1, dwo.shape[1]))


def _mix_fwd(x, h, gv, wmix, w_dw, b_dw, g_ln, b_ln, wgp, bgp, gn, wout, nxt, tag):
    z = _mm([(h, wmix, 0)], F32, f"mix_in_{tag}")
    u, yc, yconv = _conv_fwd(z, w_dw, b_dw, g_ln, b_ln, f"conv_fwd_{tag}")
    la = _loggate(z, wgp, bgp, f"loggate_{tag}")
    o_raw, sprev, ygla = _gla_fwd(z, la, gn, f"gla_fwd_{tag}")
    y, xn, *hn = _mix_out(yconv, ygla, wout, x, gv, nxt, f"mix_out_{tag}")
    return xn, (hn[0] if hn else None), y, (x, h, z, u, yc, la, o_raw, sprev, yconv, ygla)


def _mix_bwd(dxn, dy, saved, g, scale, prev, wmix, w_dw, g_ln, b_ln, wgp, bgp, gn, wout, n_cols, n_pad, tag):
    x, h, z, u, yc, la, o_raw, sprev, yconv, ygla = saved
    dycat = _mm([(dy, wout, 0)], F32, f"mix_dycat_{tag}", nt=True)
    dwout = _mm_tn_two(yconv, ygla, dy, f"dw_mixout_{tag}")
    dab, dwdw, dbdw, dgln, dbln = _conv_bwd(dycat, z, u, yc, w_dw, g_ln, b_ln, f"conv_bwd_{tag}")
    dgla, dla, dgn = _gla_bwd(dycat, z, la, o_raw, sprev, gn, f"gla_bwd_{tag}")
    dglr, dwgp, dbgp = _loggate_bwd(dla, z, wgp, wgp.T, bgp, f"loggate_bwd_{tag}")
    dx, dsh, dsc, dg, *pv = _dh_normmod_bwd(
        [(dgla, wmix, 0, 0, Z_GLA), (dab, wmix, 0, ZC_A // (2 * D_CONV), 2 * D_CONV), (dglr, wmix, 0, ZC_G // LANE, LANE)],
        x, dxn, g, scale, prev, f"mix_dh_{tag}")
    dwin = _mix_weight_grad(_mm_tn(h, dgla, f"dw_mixin_gla_{tag}"), _mm_tn(h, dab, f"dw_mixin_conv_{tag}"), _mm_tn(h, dglr, f"dw_mixin_gate_{tag}"),
                            n_cols, n_pad)
    grads = dict(dshift=dsh, dscale=dsc, dg=dg, dw_in=dwin, dw_out=dwout.reshape(N_CHIPS, -1, dwout.shape[1]), dw_dw=dwdw, db_dw=dbdw,
                 dg_ln=dgln, db_ln=dbln, dw_gate=_unpad_heads(dwgp[:GATE_RANK]), db_gate=_unpad_heads(dbgp)[0], dgn=dgn)
    return dx, pv, grads


def kernel(x, c, w_ada, b_ada, g_norm_ffn1, w_ffn1_in, w_ffn1_out, g_norm_mix, w_in, w_dw, b_dw, g_conv_ln, b_conv_ln, w_gate_up, b_gate, g_gla_norm, w_out, g_norm_ffn2, w_ffn2_in, w_ffn2_out, g_norm_final, w_ada_final, b_ada_final, loss_target, m_w_ada, m_b_ada, m_g_norm_ffn1, m_w_ffn1_in, m_w_ffn1_out, m_g_norm_mix, m_w_in, m_w_dw, m_b_dw, m_g_conv_ln, m_b_conv_ln, m_w_gate_up, m_b_gate, m_g_gla_norm, m_w_out, m_g_norm_ffn2, m_w_ffn2_in, m_w_ffn2_out, m_g_norm_final, m_w_ada_final, m_b_ada_final, v_w_ada, v_b_ada, v_g_norm_ffn1, v_w_ffn1_in, v_w_ffn1_out, v_g_norm_mix, v_w_in, v_w_dw, v_b_dw, v_g_conv_ln, v_b_conv_ln, v_w_gate_up, v_b_gate, v_g_gla_norm, v_w_out, v_g_norm_ffn2, v_w_ffn2_in, v_w_ffn2_out, v_g_norm_final, v_w_ada_final, v_b_ada_final):
    given = dict(locals())
    W = {n: given[n] for n in _WEIGHTS}
    M1 = {n: given["m_" + n] for n in _WEIGHTS}
    M2 = {n: given["v_" + n] for n in _WEIGHTS}
    xs = x[0]
    tgt = loss_target[0]
    S, D = xs.shape
    L = w_ada.shape[0]
    xi, yi, ci = _place()
    s_me = 2 * xi + yi
    b_me = 4 * xi + 2 * yi + ci
    nsh = w_ada.shape[2]
    nfin = w_ada_final.shape[1]
    n_cols = w_in.shape[2]
    n_pad = -(-n_cols // LANE) * LANE

    def lands_of(l):
        shards = [W[n][l].astype(BF16) for n in _BIG]
        shards[2] = jnp.pad(shards[2], ((0, 0), (0, n_pad - n_cols)))
        return [lax.dynamic_update_index_in_dim(lax.empty((N_CHIPS,) + s.shape, BF16), s, s_me, 0) for s in shards]

    lands = {l: lands_of(l) for l in range(L)}
    ag_groups = [dict(l=0, items=[0, 1], need=0), dict(l=0, items=[2, 3, 4, 5], need=1)]
    ag_groups += [dict(l=l, items=list(range(len(_BIG))), need=3 * l) for l in range(1, L)]

    def ag_start(grp):
        bufs = [lands[grp["l"]][i] for i in grp["items"]]
        return _split_start(bufs, 3 * len(bufs), _ag_starts, f"ag_start_l{grp['l']}_{grp['items'][0]}")

    pend = ag_start(ag_groups[0])
    tok = pend[3][0, 0]

    c_all = _ag_small(c.reshape(8, D // 8) + tok, "ag_c").reshape(8, D)
    tok = None
    parts = [_mod_proj(c_all, w_ada, l, lax.dynamic_slice(b_ada, (l, s_me * nsh), (1, nsh)), f"mod_proj_{l}") for l in range(L)]
    parts.append(_mod_proj(c_all, w_ada_final[None], 0, lax.dynamic_slice(b_ada_final, (s_me * nfin,), (nfin,))[None], "mod_proj_final"))
    mod_all = _ag_small(jnp.concatenate(parts, axis=1), "ag_mod")
    mine = [lax.dynamic_index_in_dim(lax.dynamic_index_in_dim(mod_all, 2 * s + ci, 0, False), b_me, 0, False) for s in range(N_CHIPS)]
    mods = [jnp.concatenate([mine[s][l * nsh:(l + 1) * nsh] for s in range(N_CHIPS)]).reshape(N_MOD, 1, D) for l in range(L)]
    fmod = jnp.concatenate([mine[s][L * nsh:] for s in range(N_CHIPS)]).reshape(2, 1, D)

    tiny = jnp.concatenate([w_dw.reshape(-1), w_gate_up.reshape(-1)])
    tiny_all = _ag_small(jnp.pad(tiny, (0, (-tiny.shape[0]) % (8 * LANE))).reshape(8, -1), "ag_tiny").reshape(8, -1)
    n_dw = w_dw.size
    dw_parts = [lax.dynamic_index_in_dim(tiny_all, 2 * s + ci, 0, False) for s in range(N_CHIPS)]
    w_dw_full = jnp.concatenate([p[:n_dw].reshape(w_dw.shape) for p in dw_parts], axis=2)
    w_gu_full = jnp.concatenate([p[n_dw:n_dw + w_gate_up.size].reshape(w_gate_up.shape) for p in dw_parts], axis=2)

    def layer_weights(l, lands):
        wi1, wo1, win4, wout4, wi2, wo2 = lands
        return dict(
            wi1=wi1, wo1=wo1.reshape(-1, D), wi2=wi2, wo2=wo2.reshape(-1, D), wout=wout4.reshape(-1, D), wmix=_mix_weight(win4, n_cols),
            wgp=jnp.pad(_pad_heads(w_gu_full[l]), ((0, LANE - GATE_RANK), (0, 0))).astype(BF16), bgp=_pad_heads(b_gate[l])[None])

    gnorm = (g_norm_ffn1, g_norm_mix, g_norm_ffn2)
    subs = [dict(l=l, j=j, tag=f"{('ffn1', 'mix', 'ffn2')[j]}_l{l}", g=gnorm[j][l][None], shift=mods[l][3 * j], scale=mods[l][3 * j + 1],
                 gv=mods[l][3 * j + 2] * (1.0 if j == 1 else 0.5)) for l in range(L) for j in range(3)]
    gi = 0
    xcur = xs
    h = None
    for k, sb in enumerate(subs):
        l, j = sb["l"], sb["j"]
        if pend is not None and ag_groups[gi]["need"] == k:
            grp = ag_groups[gi]
            nm = f"l{grp['l']}_{grp['items'][0]}"
            after = xcur if k > 0 else sb["shift"]
            done = _ag_finish(_split_wait(pend[0], pend[1], pend[2], after, _ag_waits, f"ag_wait_{nm}"), f"ag_finish_{nm}")
            for i, a in zip(grp["items"], done):
                lands[grp["l"]][i] = a
            gi += 1
            pend = ag_start(ag_groups[gi]) if gi < len(ag_groups) else None
            tok = pend[3][0, 0] if pend is not None else None
        if h is None:
            h = _normmod(xcur, sb["g"] if tok is None else sb["g"] + tok, sb["shift"], sb["scale"], f"normmod_{sb['tag']}")
            tok = None
        d = layer_weights(l, lands[l])
        nxt = (subs[k + 1]["g"], subs[k + 1]["shift"], subs[k + 1]["scale"]) if k + 1 < len(subs) else None
        gv = sb["gv"] if tok is None else sb["gv"] + tok
        tok = None
        if j == 1:
            xcur, h, sb["y"], sb["saved"] = _mix_fwd(xcur, h, gv, d["wmix"], w_dw_full[l], b_dw[l][None], g_conv_ln[l][None],
                                                     b_conv_ln[l][None], d["wgp"], d["bgp"], g_gla_norm[l], d["wout"], nxt, sb["tag"])
        else:
            w4, wo = (d["wi1"], d["wo1"]) if j == 0 else (d["wi2"], d["wo2"])
            xcur, h, sb["y"], sb["saved"] = _ffn_fwd(xcur, h, gv, w4, wo, nxt, sb["tag"])
    lw = [layer_weights(l, lands[l]) for l in range(L)]

    c_arr = jnp.stack([ci]).astype(jnp.int32)
    sc_arr = jnp.stack([s_me, ci]).astype(jnp.int32)
    fulls = [lax.empty((L,) + ((W[n].shape[1], n_pad) if n == 'w_in' else W[n].shape[1:]), F32) for n in _BIG]

    def rs_begin(gs, items, l):
        nm = f"l{l}_{items[0]}"
        sibs = _rs_sibling(gs, f"rs_sibling_{nm}")
        return sibs, [_rs_presum(g, sb_, c_arr, f"rs_presum_{i}_l{l}") for i, g, sb_ in zip(items, gs, sibs)]

    def rs_end(gs, sibs, recvs, items, l):
        summed = [_rs_sum(g, sb_, rv, fulls[i], l, sc_arr, f"rs_sum_{i}_l{l}") for i, g, sb_, rv in zip(items, gs, sibs, recvs)]
        for i, f in zip(items, _rs_share(summed, l, f"rs_share_l{l}_{items[0]}")):
            fulls[i] = f

    def rs_start(gs, items, l):
        sibs, ps = rs_begin(gs, items, l)
        pend = _split_start(ps + [lax.empty((3,) + p.shape[1:], BF16) for p in ps], 3 * len(ps), _rs_chip_copies, f"rs_start_l{l}_{items[0]}")
        return dict(gs=gs, sibs=sibs, pend=pend, items=items, l=l)

    def rs_finish(fl, after):
        pend, n = fl["pend"], len(fl["gs"])
        bufs = _split_wait(pend[0], pend[1], pend[2], after, _rs_chip_copies, f"rs_wait_l{fl['l']}_{fl['items'][0]}")
        rs_end(fl["gs"], fl["sibs"], bufs[n:], fl["items"], fl["l"])

    sq, dx, dfsh, dfsc, dgfin, dy, dgv = _loss_head(xcur, g_norm_final[None], fmod[0], fmod[1], tgt, (subs[-1]["y"], subs[-1]["gv"]))
    loss_part = 0.5 / D * jnp.sum(sq)
    G = {n: [None] * L for n in _SMALL}
    dmods = [None] * L
    in_flight = None
    tok = None
    for l in reversed(range(L)):
        gr = [None] * 3
        for j in reversed(range(3)):
            k = 3 * l + j
            sb, d = subs[k], lw[l]
            prev = (subs[k - 1]["y"], subs[k - 1]["gv"]) if k > 0 else None
            g_vec = sb["g"] if tok is None else sb["g"] + tok
            tok = None
            if j == 1:
                dx, pv, gr[j] = _mix_bwd(dx, dy, sb["saved"], g_vec, sb["scale"], prev, d["wmix"], w_dw_full[l], g_conv_ln[l][None], b_conv_ln[l][None],
                                         d["wgp"], d["bgp"], g_gla_norm[l], d["wout"], n_cols, n_pad, sb["tag"])
            else:
                w4, wo = (d["wi1"], d["wo1"]) if j == 0 else (d["wi2"], d["wo2"])
                dx, pv, gr[j] = _ffn_bwd(dx, dy, sb["saved"], g_vec, sb["scale"], prev, w4, wo, sb["tag"])
            gr[j]["dgv"] = dgv
            dy, dgv = pv if pv else (None, None)
            if j == 1 and in_flight is not None:
                rs_finish(in_flight, dx)
                in_flight = None
            if j == 1 and l == 0:
                in_flight = rs_start([gr[1]["dw_in"], gr[1]["dw_out"], gr[2]["dw_in"], gr[2]["dw_out"]], [2, 3, 4, 5], l)
                tok = in_flight["pend"][3][0, 0]
        g1, g2, g3 = gr
        if l > 0:
            in_flight = rs_start([g1["dw_in"], g1["dw_out"], g2["dw_in"], g2["dw_out"], g3["dw_in"], g3["dw_out"]], list(range(len(_BIG))), l)
            tok = in_flight["pend"][3][0, 0]
        else:
            last = rs_start([g1["dw_in"], g1["dw_out"]], [0, 1], l)
            rs_finish(in_flight, dx)
            in_flight = None
        dmods[l] = jnp.concatenate([g1["dshift"], g1["dscale"], 0.5 * g1["dgv"], g2["dshift"], g2["dscale"], g2["dgv"],
                                    g3["dshift"], g3["dscale"], 0.5 * g3["dgv"]], axis=1)[0]
        G["g_norm_ffn1"][l], G["g_norm_ffn2"][l], G["g_norm_mix"][l] = g1["dg"][0], g3["dg"][0], g2["dg"][0]
        G["w_dw"][l], G["b_dw"][l], G["g_conv_ln"][l], G["b_conv_ln"][l] = g2["dw_dw"], g2["db_dw"][0], g2["dg_ln"][0], g2["db_ln"][0]
        G["w_gate_up"][l], G["b_gate"][l], G["g_gla_norm"][l] = g2["dw_gate"], g2["db_gate"], g2["dgn"]
    grad_x = dx[None]
    gsm = {}

    small = [jnp.stack(G[n]).reshape(-1) for n in _SMALL if n != 'g_norm_final'] + [dgfin[0]]
    dmod_vec = jnp.concatenate(dmods + [dfsh[0], dfsc[0]])
    n_mod_vec = dmod_vec.shape[0]
    vec = jnp.concatenate([dmod_vec] + small + [loss_part[None]])
    n_vec = vec.shape[0]
    vec = jnp.pad(vec, (0, (-n_vec) % (8 * LANE)))
    vec_all = _ag_small(vec.reshape(8, -1), "ag_small_grads").reshape(8, -1)
    vec_sum = _rowsum(vec_all, "sum_small_grads")[0]
    loss = vec_sum[n_vec - 1]
    off = n_mod_vec
    for n in _SMALL:
        shp = {'w_dw': w_dw_full.shape, 'w_gate_up': w_gu_full.shape}.get(n, W[n].shape)
        cnt = 1
        for dd in shp:
            cnt *= dd
        gsm[n] = vec_sum[off:off + cnt].reshape(shp)
        off += cnt
    gsm['w_dw'] = lax.dynamic_slice_in_dim(gsm['w_dw'], s_me * w_dw.shape[2], w_dw.shape[2], 2)
    gsm['w_gate_up'] = lax.dynamic_slice_in_dim(gsm['w_gate_up'], s_me * w_gate_up.shape[2], w_gate_up.shape[2], 2)
    dmod_sum = vec_sum[:n_mod_vec]
    gsm['b_ada'] = dmod_sum[:L * N_MOD * D].reshape(L, N_MOD * D)
    gsm['b_ada_final'] = dmod_sum[L * N_MOD * D:]
    c_t = c_all.T
    dmod_rows = vec_all[:, :n_mod_vec]
    gsm['w_ada'] = jnp.stack([
        _mod_wgrad(c_t, lax.dynamic_slice_in_dim(dmod_rows, l * N_MOD * D + s_me * nsh, nsh, 1), f"dw_ada_{l}") for l in range(L)])
    gsm['w_ada_final'] = _mod_wgrad(c_t, lax.dynamic_slice_in_dim(dmod_rows, L * N_MOD * D + s_me * nfin, nfin, 1), "dw_ada_final")
    gsm.update({n: (f[:, :, :n_cols] if n == 'w_in' else f) for n, f in zip(_BIG[2:], fulls[2:])})

    outs = {}
    small_names = [n for n in _WEIGHTS if W[n].size < 65536]
    for n in [m for m in _WEIGHTS if m not in _BIG[:2]] + list(_BIG[:2]):
        if n in small_names:
            continue
        if n == _BIG[0]:
            rs_finish(last, outs['w_ada_final'][0])
            gsm.update(dict(zip(_BIG[:2], fulls[:2])))
        shp = W[n].shape
        v2 = lambda a: a.reshape(-1, shp[-1])
        from_rs = n in _BIG and n != 'w_in'
        d_, m_, v_, *g_ = _adamw(v2(W[n]), v2(gsm[n]), v2(M1[n]), v2(M2[n]), f"adamw_{n}", copy_grad=from_rs)
        outs[n] = (d_.reshape(shp), m_.reshape(shp), v_.reshape(shp))
        if from_rs:
            gsm[n] = g_[0].reshape(shp)
    flat = lambda dct: jnp.concatenate([dct[n].reshape(-1) for n in small_names])
    n_small = sum(W[n].size for n in small_names)
    v2 = lambda a: jnp.pad(a, (0, (-n_small) % (8 * LANE))).reshape(-1, LANE)
    d_, m_, v_ = _adamw(v2(flat(W)), v2(flat(gsm)), v2(flat(M1)), v2(flat(M2)), "adamw_small")

    def unflat(a):
        res, o = {}, 0
        a = a.reshape(-1)
        for n in small_names:
            res[n] = a[o:o + W[n].size].reshape(W[n].shape)
            o += W[n].size
        return res

    for n, dd, mm, vv in zip(small_names, unflat(d_).values(), unflat(m_).values(), unflat(v_).values()):
        outs[n] = (dd, mm, vv)

    return (loss, grad_x, *[gsm[n] for n in _WEIGHTS], *[outs[n][0] for n in _WEIGHTS], *[outs[n][1] for n in _WEIGHTS], *[outs[n][2] for n in _WEIGHTS])
```

```python
import jax
import jax.numpy as jnp
from jax import lax
from jax.experimental import pallas as pl
from jax.experimental.pallas import tpu as pltpu

F32 = jnp.float32
BF16 = jnp.bfloat16

CHUNK = 64
HEADS = 4
DK = 64
DV = 128
DKP = 128
GATE_RANK = 16
GATE_TAU = 16.0
N_MOD = 9
EPS = 1e-6
ADAM_LR = 0.001
ADAM_B1 = 0.9
ADAM_B2 = 0.999
ADAM_EPS = 1e-08
ADAM_WD = 0.01
ADAM_STEP = 10

LANE = 128
HALO = 32
VMEM_LIMIT = 52 * 1024 * 1024
MESH = pl.DeviceIdType.MESH
N_CHIPS = 4

D_CONV = 512
D_GLA = HEADS * DV
ZC_Q = 0
ZC_K = ZC_Q + HEADS * DKP
ZC_V = ZC_K + HEADS * DKP
ZC_R = ZC_V + D_GLA
ZC_A = ZC_R + D_GLA
ZC_B = ZC_A + D_CONV
ZC_G = ZC_B + D_CONV
Z_COLS = ZC_G + LANE
Z_GLA = ZC_A


def _div(n, target, mult):
    best = None
    d = mult
    while d <= min(n, target):
        if n % d == 0:
            best = d
        d += mult
    return n if best is None else best


def _cp(sem=None, **kw):
    return pltpu.CompilerParams(dimension_semantics=sem, vmem_limit_bytes=VMEM_LIMIT, **kw)


def _resident(shape, index_map):
    return pl.BlockSpec(shape, index_map, pipeline_mode=pl.Buffered(1))


def _sigmoid(x):
    return 0.5 * jnp.tanh(0.5 * x) + 0.5


def _dot(a, b):
    return jnp.dot(a.astype(BF16), b.astype(BF16), preferred_element_type=F32)


def _dot_nt(a, b):
    return lax.dot_general(a.astype(BF16), b.astype(BF16), (((1,), (1,)), ((), ())), preferred_element_type=F32)


def _dot_tn(a, b):
    return lax.dot_general(a.astype(BF16), b.astype(BF16), (((0,), (0,)), ((), ())), preferred_element_type=F32)


def _dot_exact(a, b):
    return jnp.dot(a, b, preferred_element_type=F32, precision=lax.Precision.HIGHEST)


def _norm_rows(xv, g, shift, scale):
    r = lax.rsqrt(jnp.mean(xv * xv, axis=-1, keepdims=True) + EPS)
    return (xv * r) * g * (1.0 + scale) + shift


def _loss_head(x, g, shift, scale, tgt, prev):
    S, D = x.shape
    tm = _div(S, 512, 8)

    def body(x_ref, g_ref, sh_ref, sc_ref, t_ref, y_ref, gvp_ref, sq_ref, dx_ref, dsh_ref, dsc_ref, dg_ref, dy_ref, dgv_ref):
        sums = (sq_ref, dsh_ref, dsc_ref, dg_ref, dgv_ref)

        @pl.when(pl.program_id(0) == 0)
        def _():
            for o in sums:
                o[...] = jnp.zeros_like(o)

        xv = x_ref[...]
        e = _norm_rows(xv, g_ref[...], sh_ref[...], sc_ref[...]) - t_ref[...]
        dx, dsh, dsc, dg, dy, dgv = _normmod_bwd_rows(xv, e * (1.0 / D), None, g_ref[...], sc_ref[...], (y_ref[...], gvp_ref[...]))
        dx_ref[...] = dx
        dy_ref[...] = dy
        for o, v in zip(sums, (jnp.sum(e * e, axis=0, keepdims=True), dsh, dsc, dg, dgv)):
            o[...] += v

    row = pl.BlockSpec((tm, D), lambda i: (i, 0))
    vec = pl.BlockSpec((1, D), lambda i: (0, 0))
    vs = jax.ShapeDtypeStruct((1, D), F32)
    return pl.pallas_call(
        body, grid=(S // tm,), in_specs=[row, vec, vec, vec, row, row, vec], out_specs=[vec, row, vec, vec, vec, row, vec],
        out_shape=[vs, jax.ShapeDtypeStruct((S, D), F32), vs, vs, vs, jax.ShapeDtypeStruct((S, D), BF16), vs],
        compiler_params=_cp(("arbitrary",)), name="loss_head",
    )(x, g, shift, scale, tgt, *prev)


def _normmod_bwd_rows(xv, dh, dres, gv, sc, prev):
    r = lax.rsqrt(jnp.mean(xv * xv, axis=-1, keepdims=True) + EPS)
    xh = xv * r
    dsh = jnp.sum(dh, axis=0, keepdims=True)
    dsc = jnp.sum(dh * (xh * gv), axis=0, keepdims=True)
    dn = dh * (1.0 + sc)
    dg = jnp.sum(dn * xh, axis=0, keepdims=True)
    dxh = dn * gv
    dx = r * (dxh - xh * jnp.mean(dxh * xh, axis=-1, keepdims=True))
    if dres is not None:
        dx = dx + dres
    if prev is None:
        return dx, dsh, dsc, dg
    y, gvp = prev
    return dx, dsh, dsc, dg, (gvp * dx).astype(BF16), jnp.sum(dx * y, axis=0, keepdims=True)


def _mm(pairs, out_dtype, name, nt=False):
    M = pairs[0][0].shape[0]
    N = pairs[0][1].shape[0] if nt else pairs[0][1].shape[1]
    ktot = sum(a.shape[1] for a, _, _ in pairs)
    tm = _div(M, 512 if ktot <= 4096 else 256, 8)
    n = len(pairs)

    def body(*refs):
        o_ref = refs[2 * n]
        dot = _dot_nt if nt else _dot
        acc = dot(refs[0][...], refs[1][...])
        for p in range(1, n):
            acc = acc + dot(refs[2 * p][...], refs[2 * p + 1][...])
        o_ref[...] = acc.astype(o_ref.dtype)

    ins, args = [], []
    for a, b, blk in pairs:
        k = a.shape[1]
        ins.append(pl.BlockSpec((tm, k), lambda i: (i, 0)))
        ins.append(_resident((N, k), lambda i, blk=blk: (0, blk)) if nt else _resident((k, N), lambda i: (0, 0)))
        args += [a, b]
    return pl.pallas_call(
        body, grid=(M // tm,), in_specs=ins, out_specs=pl.BlockSpec((tm, N), lambda i: (i, 0)),
        out_shape=jax.ShapeDtypeStruct((M, N), out_dtype), compiler_params=_cp(("parallel",)), name=name,
    )(*args)


TN_ROWS = 2048


def _mm_tn(a, g, name):
    S, Ka = a.shape
    N = g.shape[1]
    tk = _div(Ka, 1408, LANE)
    tn = _div(N, 1408, LANE)
    ts = _div(S, TN_ROWS, 8)

    def body(a_ref, g_ref, o_ref):
        @pl.when(pl.program_id(2) == 0)
        def _():
            o_ref[...] = jnp.zeros_like(o_ref)

        o_ref[...] += _dot_tn(a_ref[...], g_ref[...])

    return pl.pallas_call(
        body, grid=(Ka // tk, N // tn, S // ts),
        in_specs=[pl.BlockSpec((ts, tk), lambda i, j, s: (s, i)), pl.BlockSpec((ts, tn), lambda i, j, s: (s, j))],
        out_specs=pl.BlockSpec((tk, tn), lambda i, j, s: (i, j)),
        out_shape=jax.ShapeDtypeStruct((Ka, N), F32),
        compiler_params=_cp(("parallel", "parallel", "arbitrary")), name=name,
    )(a, g)


def _mm_tn_two(a0, a1, g, name):
    S, K = a0.shape
    N = g.shape[1]
    ts = _div(S, TN_ROWS, 8)

    def body(a0_ref, a1_ref, g_ref, o_ref):
        i = pl.program_id(0)

        @pl.when(pl.program_id(1) == 0)
        def _():
            o_ref[...] = jnp.zeros_like(o_ref)

        @pl.when(i == 0)
        def _():
            o_ref[...] += _dot_tn(a0_ref[...], g_ref[...])

        @pl.when(i == 1)
        def _():
            o_ref[...] += _dot_tn(a1_ref[...], g_ref[...])

    return pl.pallas_call(
        body, grid=(2, S // ts),
        in_specs=[pl.BlockSpec((ts, K), lambda i, s: (jnp.where(i == 0, s, 0), 0)),
                  pl.BlockSpec((ts, K), lambda i, s: (jnp.where(i == 1, s, 0), 0)),
                  pl.BlockSpec((ts, N), lambda i, s: (s, 0))],
        out_specs=pl.BlockSpec((K, N), lambda i, s: (i, 0)),
        out_shape=jax.ShapeDtypeStruct((2 * K, N), F32),
        compiler_params=_cp(("parallel", "arbitrary")), name=name,
    )(a0, a1, g)


def _swiglu(gt, up):
    return gt * _sigmoid(gt) * up


def _ffn_up(h, w4, name, norm=None):
    S, D = h.shape
    ns, _, C = w4.shape
    hs = ns // 2
    tm = _div(S, 256, 8)
    nn = 3 if norm else 0

    def body(h_ref, w_ref, *rest):
        z_ref, a_ref = rest[nn:nn + 2]
        if norm:
            hv = _norm_rows(h_ref[...], rest[0][...], rest[1][...], rest[2][...]).astype(BF16)
            rest[nn + 2][...] = hv
        else:
            hv = h_ref[...]
        for s in range(hs):
            gt = _dot(hv, w_ref[s])
            up = _dot(hv, w_ref[hs + s])
            sg = _sigmoid(gt)
            silu = gt * sg
            z_ref[:, s * C:(s + 1) * C] = (up * (sg * (1.0 + gt * (1.0 - sg)))).astype(BF16)
            z_ref[:, (hs + s) * C:(hs + s + 1) * C] = silu.astype(BF16)
            a_ref[:, s * C:(s + 1) * C] = (silu * up).astype(BF16)

    row = pl.BlockSpec((tm, D), lambda i: (i, 0))
    vec = pl.BlockSpec((1, D), lambda i: (0, 0))
    return pl.pallas_call(
        body, grid=(S // tm,), in_specs=[row, _resident((ns, D, C), lambda i: (0, 0, 0))] + [vec] * nn,
        out_specs=[pl.BlockSpec((tm, ns * C), lambda i: (i, 0)), pl.BlockSpec((tm, hs * C), lambda i: (i, 0))] + [row] * (nn // 3),
        out_shape=[jax.ShapeDtypeStruct((S, ns * C), BF16), jax.ShapeDtypeStruct((S, hs * C), BF16)] + [jax.ShapeDtypeStruct((S, D), BF16)] * (nn // 3),
        compiler_params=_cp(("parallel",)), name=name,
    )(h, w4, *(norm or ()))


def _resid_outputs(y, x_ref, gv_ref, nxt_refs, out_refs):
    out_refs[0][...] = y
    xn = x_ref[...] + gv_ref[...] * y
    out_refs[1][...] = xn
    if nxt_refs:
        out_refs[2][...] = _norm_rows(xn, nxt_refs[0][...], nxt_refs[1][...], nxt_refs[2][...]).astype(BF16)


def _ffn_down(act, wo, x, gv, nxt, name):
    S = act.shape[0]
    Fd, D = wo.shape
    tm = _div(S, 512, 8)
    nn = 3 if nxt else 0

    def body(a_ref, w_ref, x_ref, gv_ref, *rest):
        _resid_outputs(_dot(a_ref[...], w_ref[...]), x_ref, gv_ref, rest[:nn], rest[nn:])

    row = pl.BlockSpec((tm, D), lambda i: (i, 0))
    vec = pl.BlockSpec((1, D), lambda i: (0, 0))
    os_ = jax.ShapeDtypeStruct((S, D), F32)
    return pl.pallas_call(
        body, grid=(S // tm,),
        in_specs=[pl.BlockSpec((tm, Fd), lambda i: (i, 0)), _resident((Fd, D), lambda i: (0, 0)), row, vec] + [vec] * nn,
        out_specs=[row, row] + [row] * (nn // 3), out_shape=[os_, os_] + [jax.ShapeDtypeStruct((S, D), BF16)] * (nn // 3),
        compiler_params=_cp(("parallel",)), name=name,
    )(act, wo, x, gv, *(nxt or ()))


def _dw_ffn_in(h, dz, ns, name):
    S, D = h.shape
    C = dz.shape[1] // ns
    ts = _div(S, TN_ROWS, 8)

    def body(h_ref, g_ref, o_ref):
        @pl.when(pl.program_id(1) == 0)
        def _():
            o_ref[...] = jnp.zeros_like(o_ref)

        o_ref[...] += _dot_tn(h_ref[...], g_ref[...])

    return pl.pallas_call(
        body, grid=(ns, S // ts),
        in_specs=[pl.BlockSpec((ts, D), lambda j, s: (s, 0)), pl.BlockSpec((ts, C), lambda j, s: (s, j))],
        out_specs=pl.BlockSpec((None, D, C), lambda j, s: (j, 0, 0)), out_shape=jax.ShapeDtypeStruct((ns, D, C), F32),
        compiler_params=_cp(("parallel", "arbitrary")), name=name,
    )(h, dz)


def _dh_normmod_bwd(pairs, x, dres, g, scale, prev, name):
    S, D = x.shape
    tm = _div(S, 256, 8)
    n = len(pairs)
    with_prev = prev is not None

    def body(*refs):
        refs = list(refs)
        mm = refs[:2 * n]
        x_ref, dr_ref, g_ref, sc_ref = refs[2 * n:2 * n + 4]
        outs = refs[2 * n + 4 + 2 * with_prev:]

        @pl.when(pl.program_id(0) == 0)
        def _():
            for o in outs[1:4] + outs[5:]:
                o[...] = jnp.zeros_like(o)

        dh = _dot_nt(mm[0][...], mm[1][...])
        for p in range(1, n):
            dh = dh + _dot_nt(mm[2 * p][...], mm[2 * p + 1][...])
        pv = (refs[2 * n + 4][...], refs[2 * n + 5][...]) if with_prev else None
        res = _normmod_bwd_rows(x_ref[...], dh, dr_ref[...], g_ref[...], sc_ref[...], pv)
        outs[0][...] = res[0]
        for o, v in zip(outs[1:4], res[1:4]):
            o[...] += v
        if with_prev:
            outs[4][...] = res[4]
            outs[5][...] += res[5]

    row = pl.BlockSpec((tm, D), lambda i: (i, 0))
    vec = pl.BlockSpec((1, D), lambda i: (0, 0))
    ins, args = [], []
    for a, b, a_blk, b_blk, k in pairs:
        ins.append(pl.BlockSpec((tm, k), lambda i, a_blk=a_blk: (i, a_blk)))
        ins.append(_resident((None, D, k), lambda i, b_blk=b_blk: (b_blk, 0, 0)) if b.ndim == 3 else _resident((D, k), lambda i, b_blk=b_blk: (0, b_blk)))
        args += [a, b]
    ins += [row, row, vec, vec] + [row, vec] * with_prev
    args += [x, dres, g, scale] + (list(prev) if with_prev else [])
    vs = jax.ShapeDtypeStruct((1, D), F32)
    return pl.pallas_call(
        body, grid=(S // tm,), in_specs=ins, out_specs=[row, vec, vec, vec] + [row, vec] * with_prev,
        out_shape=[jax.ShapeDtypeStruct((S, D), F32), vs, vs, vs] + [jax.ShapeDtypeStruct((S, D), BF16), vs] * with_prev,
        compiler_params=_cp(("arbitrary",)), name=name,
    )(*args)


def _ffn_bwd_core(dy, wo, z, w4, x, dres, g, scale, prev, name):
    S, D = x.shape
    Fd = wo.shape[0]
    ns, _, C = w4.shape
    tm = _div(S, 256, 8)
    with_prev = prev is not None

    def body(dy_ref, wo_ref, q_ref, p_ref, w4_ref, x_ref, dr_ref, g_ref, sc_ref, *rest):
        outs = rest[2 * with_prev:]
        dz_ref, outs = outs[0], outs[1:]

        @pl.when(pl.program_id(0) == 0)
        def _():
            for o in outs[1:4] + outs[5:]:
                o[...] = jnp.zeros_like(o)

        da = _dot_nt(dy_ref[...], wo_ref[...])
        dz_ref[:, :Fd] = (da * q_ref[...].astype(F32)).astype(BF16)
        dz_ref[:, Fd:] = (da * p_ref[...].astype(F32)).astype(BF16)
        dh = _dot_nt(dz_ref[:, 0:C], w4_ref[0])
        for s in range(1, ns):
            dh = dh + _dot_nt(dz_ref[:, s * C:(s + 1) * C], w4_ref[s])
        pv = (rest[0][...], rest[1][...]) if with_prev else None
        res = _normmod_bwd_rows(x_ref[...], dh, dr_ref[...], g_ref[...], sc_ref[...], pv)
        outs[0][...] = res[0]
        for o, v in zip(outs[1:4], res[1:4]):
            o[...] += v
        if with_prev:
            outs[4][...] = res[4]
            outs[5][...] += res[5]

    row = pl.BlockSpec((tm, D), lambda i: (i, 0))
    vec = pl.BlockSpec((1, D), lambda i: (0, 0))
    wide = pl.BlockSpec((tm, 2 * Fd), lambda i: (i, 0))
    vs = jax.ShapeDtypeStruct((1, D), F32)
    return pl.pallas_call(
        body, grid=(S // tm,),
        in_specs=[row, _resident((Fd, D), lambda i: (0, 0)), pl.BlockSpec((tm, Fd), lambda i: (i, 0)), pl.BlockSpec((tm, Fd), lambda i: (i, 1)),
                  _resident((ns, D, C), lambda i: (0, 0, 0)), row, row, vec, vec] + [row, vec] * with_prev,
        out_specs=[wide, row, vec, vec, vec] + [row, vec] * with_prev,
        out_shape=[jax.ShapeDtypeStruct((S, 2 * Fd), BF16), jax.ShapeDtypeStruct((S, D), F32), vs, vs, vs] + [jax.ShapeDtypeStruct((S, D), BF16), vs] * with_prev,
        compiler_params=_cp(("arbitrary",)), name=name,
    )(dy, wo, z, z, w4, x, dres, g, scale, *(prev or ()))


def _mix_out(yconv, ygla, wout, x, gv, nxt, name):
    S, Kc = yconv.shape
    Kg = ygla.shape[1]
    D = wout.shape[1]
    tm = _div(S, 512, 8)
    nn = 3 if nxt else 0

    def body(a_ref, b_ref, w_ref, x_ref, gv_ref, *rest):
        y = _dot(a_ref[...], w_ref[0:Kc, :]) + _dot(b_ref[...], w_ref[Kc:Kc + Kg, :])
        _resid_outputs(y, x_ref, gv_ref, rest[:nn], rest[nn:])

    row = pl.BlockSpec((tm, D), lambda i: (i, 0))
    vec = pl.BlockSpec((1, D), lambda i: (0, 0))
    os_ = jax.ShapeDtypeStruct((S, D), F32)
    return pl.pallas_call(
        body, grid=(S // tm,),
        in_specs=[pl.BlockSpec((tm, Kc), lambda i: (i, 0)), pl.BlockSpec((tm, Kg), lambda i: (i, 0)), _resident((Kc + Kg, D), lambda i: (0, 0)), row,
                  vec] + [vec] * nn,
        out_specs=[row, row] + [row] * (nn // 3), out_shape=[os_, os_] + [jax.ShapeDtypeStruct((S, D), BF16)] * (nn // 3),
        compiler_params=_cp(("parallel",)), name=name,
    )(yconv, ygla, wout, x, gv, *(nxt or ()))


def _ln_parts(yc, g, b):
    mu = jnp.mean(yc, axis=-1, keepdims=True)
    xc = yc - mu
    rs = lax.rsqrt(jnp.mean(xc * xc, axis=-1, keepdims=True) + EPS)
    xh = xc * rs
    return xh, rs, xh * g + b


SUB = 8
CONV_ROWS = 32


def _shifted_copies(ext8, rows):
    for b in range(1, SUB):
        ext8[b, pl.ds(0, rows - SUB), :] = ext8[0, pl.ds(b, rows - SUB), :]


def _tap(o):
    return o % SUB, o - o % SUB


def _conv_fwd(z, w_dw, b_dw, g_ln, b_ln, name):
    S = z.shape[0]
    W, C = w_dw.shape
    ts = _div(S, 512, HALO)
    hb = ts // HALO
    off = HALO - (W - 1)
    ca, cb = ZC_A // C, ZC_B // C
    rb = 2 * CONV_ROWS

    def body(a_ref, b_ref, pa_ref, pb_ref, w_ref, bd_ref, g_ref, bl_ref, u_ref, yc_ref, o_ref, ext8):
        keep = (pl.program_id(0) > 0).astype(F32)
        u = a_ref[...] * _sigmoid(b_ref[...])
        ext8[0, pl.ds(0, HALO), :] = pa_ref[...] * _sigmoid(pb_ref[...]) * keep
        ext8[0, pl.ds(HALO, ts), :] = u
        u_ref[...] = u
        _shifted_copies(ext8, ts + HALO)

        for lg in range(C // LANE):
            lanes = pl.ds(lg * LANE, LANE)
            taps = [jnp.broadcast_to(w_ref[pl.ds(j, 1), lanes], (SUB, LANE)) for j in range(W)]
            bias = jnp.broadcast_to(bd_ref[:, lanes], (SUB, LANE))

            def sub(i, carry, lanes=lanes, taps=taps, bias=bias):
                r0 = pl.multiple_of(i * rb, rb)
                accs = [bias] * (rb // SUB)
                for j in range(W):
                    b, a = _tap(off + j)
                    for r in range(rb // SUB):
                        accs[r] = accs[r] + taps[j] * ext8[b, pl.ds(r0 + a + r * SUB, SUB), lanes]
                for r in range(rb // SUB):
                    yc_ref[pl.ds(r0 + r * SUB, SUB), lanes] = accs[r]
                return carry

            lax.fori_loop(0, ts // rb, sub, 0)
        _, _, ln = _ln_parts(yc_ref[...], g_ref[...], bl_ref[...])
        o_ref[...] = (ln * _sigmoid(ln)).astype(BF16)

    cur = lambda col: pl.BlockSpec((ts, C), lambda i: (i, col))
    prev = lambda col: pl.BlockSpec((HALO, C), lambda i: (jnp.maximum(i * hb - 1, 0), col))
    vec = pl.BlockSpec((1, C), lambda i: (0, 0))
    row = pl.BlockSpec((ts, C), lambda i: (i, 0))
    fs = jax.ShapeDtypeStruct((S, C), F32)
    return pl.pallas_call(
        body, grid=(S // ts,),
        in_specs=[cur(ca), cur(cb), prev(ca), prev(cb), pl.BlockSpec((W, C), lambda i: (0, 0)), vec, vec, vec],
        out_specs=[row, row, row], out_shape=[fs, fs, jax.ShapeDtypeStruct((S, C), BF16)],
        scratch_shapes=[pltpu.VMEM((SUB, ts + HALO, C), F32)],
        compiler_params=_cp(("parallel",)), name=name,
    )(z, z, z, z, w_dw, b_dw, g_ln, b_ln)


def _conv_bwd(dycat, z, u, yc, w_dw, g_ln, b_ln, name):
    S = z.shape[0]
    W, C = w_dw.shape
    ts = _div(S, 512, HALO)
    hb = ts // HALO
    nblk = S // ts
    off = HALO - (W - 1)
    ca, cb = ZC_A // C, ZC_B // C
    rb = CONV_ROWS

    def ln_silu_bwd(dy, ycv, g, b):
        xh, rs, ln = _ln_parts(ycv, g, b)
        sl = _sigmoid(ln)
        dln = dy * (sl * (1.0 + ln * (1.0 - sl)))
        dxh = dln * g
        dyc = rs * (dxh - jnp.mean(dxh, axis=-1, keepdims=True) - xh * jnp.mean(dxh * xh, axis=-1, keepdims=True))
        return dyc, dln, xh

    def body(dy_ref, ndy_ref, yc_ref, nyc_ref, u_ref, pu_ref, a_ref, b_ref, w_ref, g_ref, bl_ref,
             dab_ref, dw_ref, dbd_ref, dg_ref, dbl_ref, uext8, dext8, dwacc):
        i = pl.program_id(0)

        @pl.when(i == 0)
        def _():
            dwacc[...] = jnp.zeros_like(dwacc)
            dbd_ref[...] = jnp.zeros_like(dbd_ref)
            dg_ref[...] = jnp.zeros_like(dg_ref)
            dbl_ref[...] = jnp.zeros_like(dbl_ref)

        g = g_ref[...]
        bl = bl_ref[...]
        dyc, dln, xh = ln_silu_bwd(dy_ref[...], yc_ref[...], g, bl)
        ndyc, _, _ = ln_silu_bwd(ndy_ref[...], nyc_ref[...], g, bl)
        dg_ref[...] += jnp.sum(dln * xh, axis=0, keepdims=True)
        dbl_ref[...] += jnp.sum(dln, axis=0, keepdims=True)
        dbd_ref[...] += jnp.sum(dyc, axis=0, keepdims=True)
        dext8[0, pl.ds(0, ts), :] = dyc
        dext8[0, pl.ds(ts, HALO), :] = ndyc * (i < nblk - 1).astype(F32)
        uext8[0, pl.ds(0, HALO), :] = pu_ref[...] * (i > 0).astype(F32)
        uext8[0, pl.ds(HALO, ts), :] = u_ref[...]
        _shifted_copies(dext8, ts + HALO)
        _shifted_copies(uext8, ts + HALO)

        def sub(k, carry):
            r0 = pl.multiple_of(k * rb, rb)
            rows = pl.ds(r0, rb)
            dyt = dext8[0, rows, :]
            du = jnp.zeros((rb, C), F32)
            for j in range(W):
                b, a = _tap(W - 1 - j)
                du = du + w_ref[pl.ds(j, 1), :] * dext8[b, pl.ds(r0 + a, rb), :]
                b, a = _tap(off + j)
                p = dyt * uext8[b, pl.ds(r0 + a, rb), :]
                part = p[0:SUB]
                for q in range(1, rb // SUB):
                    part = part + p[q * SUB:(q + 1) * SUB]
                dwacc[j] += part
            sb = _sigmoid(b_ref[rows, :])
            dab_ref[rows, 0:C] = (du * sb).astype(BF16)
            dab_ref[rows, C:2 * C] = (du * a_ref[rows, :] * sb * (1.0 - sb)).astype(BF16)
            return carry

        lax.fori_loop(0, ts // rb, sub, 0)

        @pl.when(i == nblk - 1)
        def _():
            for j in range(W):
                dw_ref[pl.ds(j, 1), :] = jnp.sum(dwacc[j], axis=0, keepdims=True)

    row = pl.BlockSpec((ts, C), lambda i: (i, 0))
    nxt = pl.BlockSpec((HALO, C), lambda i: (jnp.minimum((i + 1) * hb, S // HALO - 1), 0))
    prv = pl.BlockSpec((HALO, C), lambda i: (jnp.maximum(i * hb - 1, 0), 0))
    vec = pl.BlockSpec((1, C), lambda i: (0, 0))
    wsp = pl.BlockSpec((W, C), lambda i: (0, 0))
    vs = jax.ShapeDtypeStruct((1, C), F32)
    return pl.pallas_call(
        body, grid=(nblk,),
        in_specs=[row, nxt, row, nxt, row, prv, pl.BlockSpec((ts, C), lambda i: (i, ca)), pl.BlockSpec((ts, C), lambda i: (i, cb)), wsp, vec, vec],
        out_specs=[pl.BlockSpec((ts, 2 * C), lambda i: (i, 0)), wsp, vec, vec, vec],
        out_shape=[jax.ShapeDtypeStruct((S, 2 * C), BF16), jax.ShapeDtypeStruct((W, C), F32), vs, vs, vs],
        scratch_shapes=[pltpu.VMEM((SUB, ts + HALO, C), F32), pltpu.VMEM((SUB, ts + HALO, C), F32), pltpu.VMEM((W, SUB, C), F32)],
        compiler_params=_cp(("arbitrary",)), name=name,
    )(dycat, dycat, yc, yc, u, u, z, z, w_dw, g_ln, b_ln)


def _log_gate(zg):
    return (jnp.minimum(zg, 0.0) - jnp.log(1.0 + jnp.exp(-jnp.abs(zg)))) * (1.0 / GATE_TAU)


def _loggate(z, wgp, bgp, name):
    S = z.shape[0]
    N = wgp.shape[1]
    ts = _div(S, 512, 8)

    def body(g_ref, w_ref, b_ref, o_ref):
        o_ref[...] = _log_gate(_dot(g_ref[...], w_ref[...]) + b_ref[...])

    return pl.pallas_call(
        body, grid=(S // ts,),
        in_specs=[pl.BlockSpec((ts, LANE), lambda i: (i, ZC_G // LANE)), pl.BlockSpec((LANE, N), lambda i: (0, 0)), pl.BlockSpec((1, N), lambda i: (0, 0))],
        out_specs=pl.BlockSpec((ts, N), lambda i: (i, 0)), out_shape=jax.ShapeDtypeStruct((S, N), F32),
        compiler_params=_cp(("parallel",)), name=name,
    )(z, wgp, bgp)


def _loggate_bwd(dla, z, wgp, wgp_t, bgp, name):
    S = z.shape[0]
    N = wgp.shape[1]
    ts = _div(S, 512, 8)

    def body(dla_ref, g_ref, w_ref, wt_ref, b_ref, dg_ref, dw_ref, db_ref):
        @pl.when(pl.program_id(0) == 0)
        def _():
            dw_ref[...] = jnp.zeros_like(dw_ref)
            db_ref[...] = jnp.zeros_like(db_ref)

        glr = g_ref[...]
        zg = _dot(glr, w_ref[...]) + b_ref[...]
        dzg = dla_ref[...] * (1.0 / GATE_TAU) * (1.0 - _sigmoid(zg))
        dg_ref[...] = _dot(dzg, wt_ref[...]).astype(BF16)
        dw_ref[...] += _dot_tn(glr, dzg)
        db_ref[...] += jnp.sum(dzg, axis=0, keepdims=True)

    return pl.pallas_call(
        body, grid=(S // ts,),
        in_specs=[pl.BlockSpec((ts, N), lambda i: (i, 0)), pl.BlockSpec((ts, LANE), lambda i: (i, ZC_G // LANE)),
                  pl.BlockSpec((LANE, N), lambda i: (0, 0)), pl.BlockSpec((N, LANE), lambda i: (0, 0)), pl.BlockSpec((1, N), lambda i: (0, 0))],
        out_specs=[pl.BlockSpec((ts, LANE), lambda i: (i, 0)), pl.BlockSpec((LANE, N), lambda i: (0, 0)), pl.BlockSpec((1, N), lambda i: (0, 0))],
        out_shape=[jax.ShapeDtypeStruct((S, LANE), BF16), jax.ShapeDtypeStruct((LANE, N), F32), jax.ShapeDtypeStruct((1, N), F32)],
        compiler_params=_cp(("arbitrary",)), name=name,
    )(dla, z, wgp, wgp_t, bgp)


def _bdot(a, b, ca, cb):
    return lax.dot_general(a.astype(BF16), b.astype(BF16), (((ca,), (cb,)), ((0,), (0,))), preferred_element_type=F32)


def _bdot_exact(a, b):
    return lax.dot_general(a, b, (((2,), (1,)), ((0,), (0,))), preferred_element_type=F32, precision=lax.Precision.HIGHEST)


def _tiles(ref, cpb):
    return jnp.stack([ref[pl.ds(c * CHUNK, CHUNK), pl.ds(h * LANE, LANE)] for c in range(cpb) for h in range(HEADS)])


def _tri_masks(n):
    ri = lax.broadcasted_iota(jnp.int32, (n, CHUNK, CHUNK), 1)
    ci = lax.broadcasted_iota(jnp.int32, (n, CHUNK, CHUNK), 2)
    return ri >= ci, (ri >= ci).astype(F32), (ri <= ci).astype(F32)


def _chunk_fwd_terms(q, k, la, tril):
    bc = _bdot_exact(tril, la)
    bend = jnp.sum(la, axis=1, keepdims=True)
    eb = jnp.exp(bc)
    enb = jnp.exp(-bc)
    ee = jnp.exp(bend - bc)
    qs = q * (DK ** -0.5)
    return bend, eb, enb, ee, qs * eb, qs * enb, k * enb, k * eb, k * ee


def _gla_fwd(z, la, gn, name):
    S = z.shape[0]
    W = HEADS * LANE
    tb = _div(S, 512, CHUNK)
    cpb = tb // CHUNK

    def body(q_ref, k_ref, v_ref, r_ref, la_ref, gn_ref, o_ref, sp_ref, y_ref, st):
        @pl.when(pl.program_id(0) == 0)
        def _():
            st[...] = jnp.zeros_like(st)

        tri, tril, _ = _tri_masks(cpb * HEADS)
        q, k, v, rv, lav = (_tiles(r, cpb) for r in (q_ref, k_ref, v_ref, r_ref, la_ref))
        bend, _, _, _, qf, qb, kb, kf, ke = _chunk_fwd_terms(q, k, lav, tril)
        att = jnp.where(tri, _bdot(qf, kb, 2, 2), _bdot(qb, kf, 2, 2))
        o_intra = _bdot(att, v, 2, 1)
        u = _bdot(v, ke, 1, 1)
        gdec = jnp.exp(bend)
        s_prev = [None] * (cpb * HEADS)
        for h in range(HEADS):
            s = st[h]
            for c in range(cpb):
                b = c * HEADS + h
                s_prev[b] = s
                s = s * gdec[b] + u[b]
            st[h] = s
        s_prev = jnp.stack(s_prev)
        o = o_intra + _bdot(qf, s_prev, 2, 2)
        rms = lax.rsqrt(jnp.mean(o * o, axis=-1, keepdims=True) + EPS)
        gn = jnp.stack([gn_ref[pl.ds(h, 1), :] for _ in range(cpb) for h in range(HEADS)])
        y = (o * rms * gn * (rv * _sigmoid(rv))).astype(BF16)
        for c in range(cpb):
            for h in range(HEADS):
                b = c * HEADS + h
                rows, ln = pl.ds(c * CHUNK, CHUNK), pl.ds(h * LANE, LANE)
                o_ref[rows, ln] = o[b]
                y_ref[rows, ln] = y[b]
                sp_ref[h, c] = s_prev[b]

    zb = lambda base: pl.BlockSpec((tb, W), lambda i: (i, base // W))
    hb_ = pl.BlockSpec((tb, W), lambda i: (i, 0))
    return pl.pallas_call(
        body, grid=(S // tb,),
        in_specs=[zb(ZC_Q), zb(ZC_K), zb(ZC_V), zb(ZC_R), hb_, pl.BlockSpec((HEADS, DV), lambda i: (0, 0))],
        out_specs=[hb_, pl.BlockSpec((HEADS, cpb, DV, DKP), lambda i: (0, i, 0, 0)), hb_],
        out_shape=[jax.ShapeDtypeStruct((S, D_GLA), F32), jax.ShapeDtypeStruct((HEADS, S // CHUNK, DV, DKP), F32),
                   jax.ShapeDtypeStruct((S, D_GLA), BF16)],
        scratch_shapes=[pltpu.VMEM((HEADS, DV, DKP), F32)],
        compiler_params=_cp(("arbitrary",)), name=name,
    )(z, z, z, z, la, gn)


def _gla_bwd(dycat, z, la, o_raw, sprev, gn, name):
    S = z.shape[0]
    W = HEADS * LANE
    tb = _div(S, 512, CHUNK)
    cpb = tb // CHUNK
    nb = S // tb

    def body(q_ref, k_ref, v_ref, r_ref, la_ref, o_ref, sp_ref, dy_ref, gn_ref, dz_ref, dla_ref, dgn_ref, dst):
        @pl.when(pl.program_id(0) == 0)
        def _():
            dst[...] = jnp.zeros_like(dst)
            dgn_ref[...] = jnp.zeros_like(dgn_ref)

        nt = cpb * HEADS
        tri, tril, triu = _tri_masks(nt)
        q, k, v, rv, lav, o, dy = (_tiles(r, cpb) for r in (q_ref, k_ref, v_ref, r_ref, la_ref, o_ref, dy_ref))
        bend, eb, enb, ee, qf, qb, kb, kf, ke = _chunk_fwd_terms(q, k, lav, tril)
        att = jnp.where(tri, _bdot(qf, kb, 2, 2), _bdot(qb, kf, 2, 2))
        s_prev = jnp.stack([sp_ref[h, c] for c in range(cpb) for h in range(HEADS)])
        gdec = jnp.exp(bend)
        gn = jnp.stack([gn_ref[pl.ds(h, 1), :] for _ in range(cpb) for h in range(HEADS)])
        rms = lax.rsqrt(jnp.mean(o * o, axis=-1, keepdims=True) + EPS)
        oh = o * rms
        sg = _sigmoid(rv)
        sr = rv * sg
        d_r = (dy * oh * gn * (sg * (1.0 + rv * (1.0 - sg)))).astype(BF16)
        dgn = jnp.sum(dy * sr * oh, axis=1, keepdims=True)
        w = dy * sr * gn
        do = rms * (w - oh * jnp.mean(w * oh, axis=-1, keepdims=True))
        p = _bdot(do, qf, 1, 1)
        ds = [None] * nt
        for h in range(HEADS):
            s = dst[h]
            for c in reversed(range(cpb)):
                b = c * HEADS + h
                ds[b] = s
                s = s * gdec[b] + p[b]
            dst[h] = s
            dgn_ref[pl.ds(h, 1), :] += sum(dgn[c * HEADS + h] for c in range(cpb))
        ds = jnp.stack(ds)
        datt = _bdot(do, v, 2, 2)
        daf = jnp.where(tri, datt, 0.0)
        dab = jnp.where(tri, 0.0, datt)
        d_v = (_bdot(att, do, 1, 1) + _bdot(ke, ds, 2, 2)).astype(BF16)
        dke = _bdot(v, ds, 2, 1)
        dqf = _bdot(daf, kb, 2, 1) + _bdot(do, s_prev, 2, 1)
        dkb = _bdot(daf, qf, 1, 1)
        dqb = _bdot(dab, kf, 2, 1)
        dkf = _bdot(dab, qb, 1, 1)
        dg = jnp.sum(ds * s_prev, axis=1, keepdims=True)
        d_q = ((dqf * eb + dqb * enb) * (DK ** -0.5)).astype(BF16)
        d_k = (dkb * enb + dkf * eb + dke * ee).astype(BF16)
        dbc = dqf * qf - dkb * kb - dqb * qb + dkf * kf - dke * ke
        dbend = jnp.sum(dke * ke, axis=1, keepdims=True) + dg * gdec
        dla = _bdot_exact(triu, dbc) + dbend
        for c in range(cpb):
            for h in range(HEADS):
                b = c * HEADS + h
                rows = pl.ds(c * CHUNK, CHUNK)
                for base, val in ((ZC_Q, d_q), (ZC_K, d_k), (ZC_V, d_v), (ZC_R, d_r)):
                    dz_ref[rows, pl.ds(base + h * LANE, LANE)] = val[b]
                dla_ref[rows, pl.ds(h * LANE, LANE)] = dla[b]

    zb = lambda base: pl.BlockSpec((tb, W), lambda i: (nb - 1 - i, base // W))
    hb_ = pl.BlockSpec((tb, W), lambda i: (nb - 1 - i, 0))
    return pl.pallas_call(
        body, grid=(nb,),
        in_specs=[zb(ZC_Q), zb(ZC_K), zb(ZC_V), zb(ZC_R), hb_, hb_,
                  pl.BlockSpec((HEADS, cpb, DV, DKP), lambda i: (0, nb - 1 - i, 0, 0)),
                  pl.BlockSpec((tb, W), lambda i: (nb - 1 - i, 1)),
                  pl.BlockSpec((HEADS, DV), lambda i: (0, 0))],
        out_specs=[pl.BlockSpec((tb, Z_GLA), lambda i: (nb - 1 - i, 0)), hb_, pl.BlockSpec((HEADS, DV), lambda i: (0, 0))],
        out_shape=[jax.ShapeDtypeStruct((S, Z_GLA), BF16), jax.ShapeDtypeStruct((S, HEADS * DKP), F32), jax.ShapeDtypeStruct((HEADS, DV), F32)],
        scratch_shapes=[pltpu.VMEM((HEADS, DV, DKP), F32)],
        compiler_params=_cp(("arbitrary",)), name=name,
    )(z, z, z, z, la, o_raw, sprev, dycat, gn)


def _mod_proj(c_all, w3, layer, b, name):
    B, D = c_all.shape
    N = w3.shape[2]
    tn = _div(N, 1024, LANE)

    def body(c_ref, w_ref, b_ref, o_ref):
        cv = c_ref[...]
        o_ref[...] = _dot(cv * _sigmoid(cv), w_ref[...]) + b_ref[...]

    return pl.pallas_call(
        body, grid=(N // tn,),
        in_specs=[pl.BlockSpec((B, D), lambda j: (0, 0)), pl.BlockSpec((None, D, tn), lambda j: (layer, 0, j)), pl.BlockSpec((1, tn), lambda j: (0, j))],
        out_specs=pl.BlockSpec((B, tn), lambda j: (0, j)), out_shape=jax.ShapeDtypeStruct((B, N), F32),
        compiler_params=_cp(("parallel",)), name=name,
    )(c_all, w3, b)


def _mod_wgrad(c_t, dm, name):
    D, B = c_t.shape
    N = dm.shape[1]
    tn = _div(N, 1024, LANE)

    def body(c_ref, d_ref, o_ref):
        cv = c_ref[...]
        ca = cv * _sigmoid(cv)
        acc = ca[:, 0:1] * d_ref[pl.ds(0, 1), :]
        for b in range(1, B):
            acc = acc + ca[:, b:b + 1] * d_ref[pl.ds(b, 1), :]
        o_ref[...] = acc

    return pl.pallas_call(
        body, grid=(N // tn,),
        in_specs=[pl.BlockSpec((D, B), lambda j: (0, 0)), pl.BlockSpec((B, tn), lambda j: (0, j))],
        out_specs=pl.BlockSpec((D, tn), lambda j: (0, j)), out_shape=jax.ShapeDtypeStruct((D, N), F32),
        compiler_params=_cp(("parallel",)), name=name,
    )(c_t, dm)


def _rowsum(xs, name):
    n, N = xs.shape
    tn = _div(N, 8192, LANE)

    def body(x_ref, o_ref):
        acc = x_ref[pl.ds(0, 1), :]
        for r in range(1, n):
            acc = acc + x_ref[pl.ds(r, 1), :]
        o_ref[...] = acc

    return pl.pallas_call(
        body, grid=(N // tn,), in_specs=[pl.BlockSpec((n, tn), lambda j: (0, j))],
        out_specs=pl.BlockSpec((1, tn), lambda j: (0, j)), out_shape=jax.ShapeDtypeStruct((1, N), F32),
        compiler_params=_cp(("parallel",)), name=name,
    )(xs)


def _adamw(w, g, m, v, name, copy_grad=False):
    R, C = w.shape
    tr = _div(R, max(8, (1 << 18) // C), 8)

    def body(w_ref, g_ref, m_ref, v_ref, d_ref, nm_ref, nv_ref, *g_out):
        gv = g_ref[...]
        if copy_grad:
            g_out[0][...] = gv
        mn = ADAM_B1 * m_ref[...] + (1.0 - ADAM_B1) * gv
        vn = ADAM_B2 * v_ref[...] + (1.0 - ADAM_B2) * (gv * gv)
        m_hat = mn / (1.0 - ADAM_B1 ** ADAM_STEP)
        v_hat = vn / (1.0 - ADAM_B2 ** ADAM_STEP)
        d_ref[...] = -ADAM_LR * (m_hat / (jnp.sqrt(v_hat) + ADAM_EPS) + ADAM_WD * w_ref[...])
        nm_ref[...] = mn
        nv_ref[...] = vn

    blk = pl.BlockSpec((tr, C), lambda i: (i, 0))
    os_ = jax.ShapeDtypeStruct((R, C), F32)
    n_out = 4 if copy_grad else 3
    return pl.pallas_call(
        body, grid=(R // tr,), in_specs=[blk] * 4, out_specs=[blk] * n_out, out_shape=[os_] * n_out,
        compiler_params=_cp(("parallel",)), name=name,
    )(w, g, m, v)


def _place():
    return lax.axis_index("x"), lax.axis_index("y"), lax.axis_index("c")


def _other_chips(x, y):
    return [(1 - x, y), (x, 1 - y), (1 - x, 1 - y)]


def _half(c, rows):
    return pl.ds(c * (rows // 2), rows // 2)


_ANY = pl.BlockSpec(memory_space=pl.ANY)


def _ag_small(v, name):
    r, n = v.shape

    def body(v_ref, o_ref, send_sems, recv_sems):
        x, y, c = _place()
        me = 4 * x + 2 * y + c
        o_ref[pl.ds(me, 1)] = v_ref[...][None]
        peers = [(x ^ (k >> 2), y ^ ((k >> 1) & 1), c ^ (k & 1)) for k in range(1, 8)]
        copies = []
        for k, peer in enumerate(peers):
            cp = pltpu.make_async_remote_copy(
                src_ref=v_ref, dst_ref=o_ref.at[me], send_sem=send_sems.at[k], recv_sem=recv_sems.at[k],
                device_id=peer, device_id_type=MESH)
            cp.start()
            copies.append(cp)
        for cp in copies:
            cp.wait()

    return pl.pallas_call(
        body, out_shape=jax.ShapeDtypeStruct((8, r, n), v.dtype),
        in_specs=[pl.BlockSpec(memory_space=pltpu.VMEM)], out_specs=pl.BlockSpec(memory_space=pltpu.VMEM),
        scratch_shapes=[pltpu.SemaphoreType.DMA((7,)), pltpu.SemaphoreType.DMA((7,))],
        compiler_params=pltpu.CompilerParams(vmem_limit_bytes=VMEM_LIMIT), name=name,
    )(v)


def _rs_sibling(gs, name):
    n = len(gs)

    def body(*refs):
        src, out = refs[:n], refs[n:2 * n]
        send_sems, recv_sems = refs[2 * n:]
        x, y, c = _place()
        copies = []
        for i in range(n):
            cp = pltpu.make_async_remote_copy(
                src_ref=src[i].at[:, _half(1 - c, src[i].shape[1])], dst_ref=out[i], send_sem=send_sems.at[i], recv_sem=recv_sems.at[i],
                device_id=(x, y, 1 - c), device_id_type=MESH)
            cp.start()
            copies.append(cp)
        for cp in copies:
            cp.wait()

    return pl.pallas_call(
        body, out_shape=[jax.ShapeDtypeStruct((N_CHIPS, g.shape[1] // 2, g.shape[2]), g.dtype) for g in gs],
        in_specs=[_ANY] * n, out_specs=[_ANY] * n,
        scratch_shapes=[pltpu.SemaphoreType.DMA((n,)), pltpu.SemaphoreType.DMA((n,))],
        compiler_params=pltpu.CompilerParams(has_side_effects=True), name=name,
    )(*gs)


def _rs_presum(g, sib, c_arr, name):
    ns, R, C = g.shape
    rh = R // 2
    tr = _div(rh, max(16, (1 << 19) // C), 16)
    nrb = rh // tr

    def body(c_ref, g_ref, s_ref, o_ref):
        o_ref[...] = (g_ref[...] + s_ref[...]).astype(BF16)

    return pl.pallas_call(
        body, out_shape=jax.ShapeDtypeStruct((ns, rh, C), BF16),
        grid_spec=pltpu.PrefetchScalarGridSpec(
            num_scalar_prefetch=1, grid=(ns, nrb),
            in_specs=[pl.BlockSpec((None, tr, C), lambda s, r, c_ref: (s, c_ref[0] * nrb + r, 0)),
                      pl.BlockSpec((None, tr, C), lambda s, r, c_ref: (s, r, 0))],
            out_specs=pl.BlockSpec((None, tr, C), lambda s, r, c_ref: (s, r, 0))),
        compiler_params=_cp(("parallel", "parallel")), name=name,
    )(c_arr, g, sib)


def _rs_sum(g, sib, recv, full, layer, sc_arr, name):
    ns, R, C = g.shape
    rh = R // 2
    tr = _div(rh, max(16, (1 << 18) // C), 16)
    nrb = rh // tr

    def body(sc_ref, g_ref, s_ref, r_ref, f_ref, o_ref):
        acc = g_ref[...] + s_ref[...]
        for j in range(3):
            acc = acc + r_ref[j].astype(F32)
        o_ref[...] = acc

    return pl.pallas_call(
        body, out_shape=jax.ShapeDtypeStruct(full.shape, F32),
        grid_spec=pltpu.PrefetchScalarGridSpec(
            num_scalar_prefetch=1, grid=(nrb,),
            in_specs=[pl.BlockSpec((None, tr, C), lambda r, sc: (sc[0], sc[1] * nrb + r, 0)),
                      pl.BlockSpec((None, tr, C), lambda r, sc: (sc[0], r, 0)),
                      pl.BlockSpec((3, tr, C), lambda r, sc: (0, r, 0)),
                      _ANY],
            out_specs=pl.BlockSpec((None, tr, C), lambda r, sc: (layer, sc[1] * nrb + r, 0))),
        input_output_aliases={4: 0},
        compiler_params=_cp(("parallel",)), name=name,
    )(sc_arr, g, sib, recv, full)


def _rs_share(fulls, layer, name):
    n = len(fulls)

    def body(*refs):
        src, out = refs[:n], refs[n:2 * n]
        send_sems, recv_sems = refs[2 * n:]
        x, y, c = _place()
        copies = []
        for i in range(n):
            rows = out[i].shape[1]
            cp = pltpu.make_async_remote_copy(
                src_ref=out[i].at[layer, _half(c, rows)], dst_ref=out[i].at[layer, _half(c, rows)],
                send_sem=send_sems.at[i], recv_sem=recv_sems.at[i], device_id=(x, y, 1 - c), device_id_type=MESH)
            cp.start()
            copies.append(cp)
        for cp in copies:
            cp.wait()

    return pl.pallas_call(
        body, out_shape=[jax.ShapeDtypeStruct(f.shape, f.dtype) for f in fulls],
        in_specs=[_ANY] * n, out_specs=[_ANY] * n, input_output_aliases={i: i for i in range(n)},
        scratch_shapes=[pltpu.SemaphoreType.DMA((n,)), pltpu.SemaphoreType.DMA((n,))],
        compiler_params=pltpu.CompilerParams(has_side_effects=True), name=name,
    )(*fulls)


_HBM = pl.BlockSpec(memory_space=pltpu.HBM)
_SEM = pl.BlockSpec(memory_space=pltpu.SEMAPHORE)
_EFFECT = pltpu.SideEffectType.DATAFLOW_SIDE_EFFECTING


def _in_hbm(a):
    return pltpu.with_memory_space_constraint(a, pltpu.HBM)


def _split_start(bufs, n_sem, copies_of, name):
    nb = len(bufs)

    def body(*refs):
        for cp in copies_of(refs[:nb], refs[nb], refs[nb + 1]):
            cp.start()
        refs[-1][...] = jnp.zeros_like(refs[-1])

    out = pl.pallas_call(
        body, name=name,
        out_shape=(pltpu.SemaphoreType.DMA((n_sem,)), pltpu.SemaphoreType.DMA((n_sem,)), *[pltpu.HBM(a.shape, a.dtype) for a in bufs],
                   jax.ShapeDtypeStruct((SUB, LANE), F32)),
        in_specs=[_HBM] * nb, out_specs=(_SEM, _SEM, *([_HBM] * nb), pl.BlockSpec(memory_space=pltpu.VMEM)),
        input_output_aliases={i: 2 + i for i in range(nb)},
        compiler_params=pltpu.CompilerParams(has_side_effects=_EFFECT),
    )(*[_in_hbm(a) for a in bufs])
    return out[0], out[1], list(out[2:2 + nb]), out[-1]


def _split_wait(send_sems, recv_sems, bufs, after, copies_of, name):
    nb = len(bufs)

    def body(*refs):
        for cp in copies_of(refs[:nb], refs[nb], refs[nb + 1]):
            cp.wait_send()
            cp.wait_recv()

    return list(pl.pallas_call(
        body, name=name, out_shape=[pltpu.HBM(a.shape, a.dtype) for a in bufs],
        in_specs=[_HBM] * nb + [_SEM, _SEM, _ANY], out_specs=[_HBM] * nb,
        input_output_aliases={i: i for i in range(nb)},
        compiler_params=pltpu.CompilerParams(has_side_effects=_EFFECT),
    )(*bufs, send_sems, recv_sems, after))


def _ag_half_copies(land, send_sems, recv_sems, landing_of_mine):
    x, y, c = _place()
    cps = []
    for j, (cx, cy) in enumerate(_other_chips(x, y)):
        for i in range(len(land)):
            rows = _half(c, land[i].shape[1])
            s = 2 * x + y if landing_of_mine else 2 * cx + cy
            cps.append(pltpu.make_async_remote_copy(
                src_ref=land[i].at[2 * x + y, rows], dst_ref=land[i].at[s, rows], send_sem=send_sems.at[3 * i + j], recv_sem=recv_sems.at[3 * i + j],
                device_id=(cx, cy, c), device_id_type=MESH))
    return cps


def _ag_starts(land, send_sems, recv_sems):
    return _ag_half_copies(land, send_sems, recv_sems, True)


def _ag_waits(land, send_sems, recv_sems):
    return _ag_half_copies(land, send_sems, recv_sems, False)


def _ag_finish(lands, name):
    n = len(lands)

    def body(*refs):
        land = refs[n:2 * n]
        send_sems, recv_sems = refs[2 * n:]
        x, y, c = _place()
        sibling = (x, y, 1 - c)

        def copy(k, i, s, h):
            blk = land[i].at[s, _half(h, land[i].shape[1])]
            return pltpu.make_async_remote_copy(
                src_ref=blk, dst_ref=blk, send_sem=send_sems.at[k], recv_sem=recv_sems.at[k], device_id=sibling, device_id_type=MESH)

        chips = _other_chips(x, y)
        passed = [copy(3 * i + j, i, 2 * cx + cy, c) for j, (cx, cy) in enumerate(chips) for i in range(n)]
        for cp in passed:
            cp.start()
        for j, (cx, cy) in enumerate(chips):
            for i in range(n):
                copy(3 * i + j, i, 2 * cx + cy, 1 - c).wait_recv()
        for cp in passed:
            cp.wait_send()

    return pl.pallas_call(
        body, out_shape=[jax.ShapeDtypeStruct(a.shape, a.dtype) for a in lands],
        in_specs=[_ANY] * n, out_specs=[_ANY] * n, input_output_aliases={i: i for i in range(n)},
        scratch_shapes=[pltpu.SemaphoreType.DMA((3 * n,)), pltpu.SemaphoreType.DMA((3 * n,))],
        compiler_params=pltpu.CompilerParams(has_side_effects=True), name=name,
    )(*lands)


def _rs_chip_copies(bufs, send_sems, recv_sems):
    n = len(bufs) // 2
    x, y, c = _place()
    return [pltpu.make_async_remote_copy(
        src_ref=bufs[i].at[2 * cx + cy], dst_ref=bufs[n + i].at[j], send_sem=send_sems.at[3 * i + j], recv_sem=recv_sems.at[3 * i + j],
        device_id=(cx, cy, c), device_id_type=MESH) for j, (cx, cy) in enumerate(_other_chips(x, y)) for i in range(n)]


def _pad_heads(w):
    lead = w.shape[:-1]
    w4 = w.reshape(*lead, HEADS, DK)
    w4 = jnp.pad(w4, [(0, 0)] * len(lead) + [(0, 0), (0, DKP - DK)])
    return w4.reshape(*lead, HEADS * DKP)


def _unpad_heads(w):
    lead = w.shape[:-1]
    return w.reshape(*lead, HEADS, DKP)[..., :DK].reshape(*lead, HEADS * DK)


def _mix_weight(win4, n_cols):
    D = win4.shape[1]
    w = jnp.transpose(win4[:, :, :n_cols], (1, 0, 2)).reshape(D, N_CHIPS * n_cols)
    o = 2 * D_CONV
    hk = HEADS * DK
    ab = w[:, :o]
    q = _pad_heads(w[:, o:o + hk])
    k = _pad_heads(w[:, o + hk:o + 2 * hk])
    vr = w[:, o + 2 * hk:o + 2 * hk + 2 * D_GLA]
    glr = jnp.pad(w[:, o + 2 * hk + 2 * D_GLA:], ((0, 0), (0, LANE - GATE_RANK)))
    return jnp.concatenate([q, k, vr, ab, glr], axis=1)


def _mix_weight_grad(dgla, dab, dglr, n_cols, n_pad):
    D = dab.shape[0]
    hkp = HEADS * DKP
    w = jnp.concatenate([dab, _unpad_heads(dgla[:, :hkp]), _unpad_heads(dgla[:, hkp:2 * hkp]), dgla[:, 2 * hkp:], dglr[:, :GATE_RANK]], axis=1)
    w = jnp.pad(w.reshape(D, N_CHIPS, n_cols), ((0, 0), (0, 0), (0, n_pad - n_cols)))
    return jnp.transpose(w, (1, 0, 2))


_ARG_NAMES = ['x', 'c', 'w_ada', 'b_ada', 'g_norm_ffn1', 'w_ffn1_in', 'w_ffn1_out', 'g_norm_mix', 'w_in', 'w_dw', 'b_dw', 'g_conv_ln', 'b_conv_ln', 'w_gate_up', 'b_gate', 'g_gla_norm', 'w_out', 'g_norm_ffn2', 'w_ffn2_in', 'w_ffn2_out', 'g_norm_final', 'w_ada_final', 'b_ada_final']
_WEIGHTS = _ARG_NAMES[2:]
_BIG = ('w_ffn1_in', 'w_ffn1_out', 'w_in', 'w_out', 'w_ffn2_in', 'w_ffn2_out')
_SMALL = ('g_norm_ffn1', 'g_norm_mix', 'w_dw', 'b_dw', 'g_conv_ln', 'b_conv_ln', 'w_gate_up', 'b_gate', 'g_gla_norm', 'g_norm_ffn2', 'g_norm_final')


def _ffn_fwd(x, h, gv, w4, wo, nxt, tag):
    if isinstance(h, tuple):
        z, act, h = _ffn_up(x, w4, f"ffn_up_{tag}", norm=h)
    else:
        z, act = _ffn_up(h, w4, f"ffn_up_{tag}")
    y, xn, *hn = _ffn_down(act, wo, x, gv, nxt, f"ffn_down_{tag}")
    return xn, (hn[0] if hn else None), y, (x, h, z, act)


def _ffn_bwd(dxn, dy, saved, g, scale, prev, w4, wo, tag):
    x, h, z, act = saved
    ns = w4.shape[0]
    dwo = _mm_tn(act, dy, f"dw_out_{tag}")
    dz, dx, dsh, dsc, dg, *pv = _ffn_bwd_core(dy, wo, z, w4, x, dxn, g, scale, prev, f"ffn_bwd_{tag}")
    dwi = _dw_ffn_in(h, dz, ns, f"dw_in_{tag}")
    return dx, pv, dict(dshift=dsh, dscale=dsc, dg=dg, dw_in=dwi, dw_out=dwo.reshape(N_CHIPS, -1, dwo.shape[1]))


def _mix_fwd(x, h, gv, wmix, w_dw, b_dw, g_ln, b_ln, wgp, bgp, gn, wout, nxt, tag):
    z = _mm([(h, wmix, 0)], F32, f"mix_in_{tag}")
    u, yc, yconv = _conv_fwd(z, w_dw, b_dw, g_ln, b_ln, f"conv_fwd_{tag}")
    la = _loggate(z, wgp, bgp, f"loggate_{tag}")
    o_raw, sprev, ygla = _gla_fwd(z, la, gn, f"gla_fwd_{tag}")
    y, xn, *hn = _mix_out(yconv, ygla, wout, x, gv, nxt, f"mix_out_{tag}")
    return xn, (hn[0] if hn else None), y, (x, h, z, u, yc, la, o_raw, sprev, yconv, ygla)


def _mix_bwd(dxn, dy, saved, g, scale, prev, wmix, w_dw, g_ln, b_ln, wgp, bgp, gn, wout, n_cols, n_pad, tag):
    x, h, z, u, yc, la, o_raw, sprev, yconv, ygla = saved
    dycat = _mm([(dy, wout, 0)], F32, f"mix_dycat_{tag}", nt=True)
    dwout = _mm_tn_two(yconv, ygla, dy, f"dw_mixout_{tag}")
    dab, dwdw, dbdw, dgln, dbln = _conv_bwd(dycat, z, u, yc, w_dw, g_ln, b_ln, f"conv_bwd_{tag}")
    dgla, dla, dgn = _gla_bwd(dycat, z, la, o_raw, sprev, gn, f"gla_bwd_{tag}")
    dglr, dwgp, dbgp = _loggate_bwd(dla, z, wgp, wgp.T, bgp, f"loggate_bwd_{tag}")
    dx, dsh, dsc, dg, *pv = _dh_normmod_bwd(
        [(dgla, wmix, 0, 0, Z_GLA), (dab, wmix, 0, ZC_A // (2 * D_CONV), 2 * D_CONV), (dglr, wmix, 0, ZC_G // LANE, LANE)],
        x, dxn, g, scale, prev, f"mix_dh_{tag}")
    dwin = _mix_weight_grad(_mm_tn(h, dgla, f"dw_mixin_gla_{tag}"), _mm_tn(h, dab, f"dw_mixin_conv_{tag}"), _mm_tn(h, dglr, f"dw_mixin_gate_{tag}"),
                            n_cols, n_pad)
    grads = dict(dshift=dsh, dscale=dsc, dg=dg, dw_in=dwin, dw_out=dwout.reshape(N_CHIPS, -1, dwout.shape[1]), dw_dw=dwdw, db_dw=dbdw,
                 dg_ln=dgln, db_ln=dbln, dw_gate=_unpad_heads(dwgp[:GATE_RANK]), db_gate=_unpad_heads(dbgp)[0], dgn=dgn)
    return dx, pv, grads


def kernel(x, c, w_ada, b_ada, g_norm_ffn1, w_ffn1_in, w_ffn1_out, g_norm_mix, w_in, w_dw, b_dw, g_conv_ln, b_conv_ln, w_gate_up, b_gate, g_gla_norm, w_out, g_norm_ffn2, w_ffn2_in, w_ffn2_out, g_norm_final, w_ada_final, b_ada_final, loss_target, m_w_ada, m_b_ada, m_g_norm_ffn1, m_w_ffn1_in, m_w_ffn1_out, m_g_norm_mix, m_w_in, m_w_dw, m_b_dw, m_g_conv_ln, m_b_conv_ln, m_w_gate_up, m_b_gate, m_g_gla_norm, m_w_out, m_g_norm_ffn2, m_w_ffn2_in, m_w_ffn2_out, m_g_norm_final, m_w_ada_final, m_b_ada_final, v_w_ada, v_b_ada, v_g_norm_ffn1, v_w_ffn1_in, v_w_ffn1_out, v_g_norm_mix, v_w_in, v_w_dw, v_b_dw, v_g_conv_ln, v_b_conv_ln, v_w_gate_up, v_b_gate, v_g_gla_norm, v_w_out, v_g_norm_ffn2, v_w_ffn2_in, v_w_ffn2_out, v_g_norm_final, v_w_ada_final, v_b_ada_final):
    given = dict(locals())
    W = {n: given[n] for n in _WEIGHTS}
    M1 = {n: given["m_" + n] for n in _WEIGHTS}
    M2 = {n: given["v_" + n] for n in _WEIGHTS}
    xs = x[0]
    tgt = loss_target[0]
    S, D = xs.shape
    L = w_ada.shape[0]
    xi, yi, ci = _place()
    s_me = 2 * xi + yi
    b_me = 4 * xi + 2 * yi + ci
    nsh = w_ada.shape[2]
    nfin = w_ada_final.shape[1]
    n_cols = w_in.shape[2]
    n_pad = -(-n_cols // LANE) * LANE

    def lands_of(l):
        shards = [W[n][l].astype(BF16) for n in _BIG]
        shards[2] = jnp.pad(shards[2], ((0, 0), (0, n_pad - n_cols)))
        return [lax.dynamic_update_index_in_dim(lax.empty((N_CHIPS,) + s.shape, BF16), s, s_me, 0) for s in shards]

    lands = {l: lands_of(l) for l in range(L)}
    ag_groups = [dict(l=0, items=[0, 1], need=0), dict(l=0, items=[2, 3, 4, 5], need=1)]
    ag_groups += [dict(l=l, items=list(range(len(_BIG))), need=3 * l) for l in range(1, L)]

    def ag_start(grp):
        bufs = [lands[grp["l"]][i] for i in grp["items"]]
        return _split_start(bufs, 3 * len(bufs), _ag_starts, f"ag_start_l{grp['l']}_{grp['items'][0]}")

    pend = ag_start(ag_groups[0])
    tok = pend[3][0, 0]

    c_all = _ag_small(c.reshape(8, D // 8) + tok, "ag_c").reshape(8, D)
    tok = None
    parts = [_mod_proj(c_all, w_ada, l, lax.dynamic_slice(b_ada, (l, s_me * nsh), (1, nsh)), f"mod_proj_{l}") for l in range(L)]
    parts.append(_mod_proj(c_all, w_ada_final[None], 0, lax.dynamic_slice(b_ada_final, (s_me * nfin,), (nfin,))[None], "mod_proj_final"))
    mod_all = _ag_small(jnp.concatenate(parts, axis=1), "ag_mod")
    mine = [lax.dynamic_index_in_dim(lax.dynamic_index_in_dim(mod_all, 2 * s + ci, 0, False), b_me, 0, False) for s in range(N_CHIPS)]
    mods = [jnp.concatenate([mine[s][l * nsh:(l + 1) * nsh] for s in range(N_CHIPS)]).reshape(N_MOD, 1, D) for l in range(L)]
    fmod = jnp.concatenate([mine[s][L * nsh:] for s in range(N_CHIPS)]).reshape(2, 1, D)

    tiny = jnp.concatenate([w_dw.reshape(-1), w_gate_up.reshape(-1)])
    tiny_all = _ag_small(jnp.pad(tiny, (0, (-tiny.shape[0]) % (8 * LANE))).reshape(8, -1), "ag_tiny").reshape(8, -1)
    n_dw = w_dw.size
    dw_parts = [lax.dynamic_index_in_dim(tiny_all, 2 * s + ci, 0, False) for s in range(N_CHIPS)]
    w_dw_full = jnp.concatenate([p[:n_dw].reshape(w_dw.shape) for p in dw_parts], axis=2)
    w_gu_full = jnp.concatenate([p[n_dw:n_dw + w_gate_up.size].reshape(w_gate_up.shape) for p in dw_parts], axis=2)

    def layer_weights(l, lands):
        wi1, wo1, win4, wout4, wi2, wo2 = lands
        return dict(
            wi1=wi1, wo1=wo1.reshape(-1, D), wi2=wi2, wo2=wo2.reshape(-1, D), wout=wout4.reshape(-1, D), wmix=_mix_weight(win4, n_cols),
            wgp=jnp.pad(_pad_heads(w_gu_full[l]), ((0, LANE - GATE_RANK), (0, 0))).astype(BF16), bgp=_pad_heads(b_gate[l])[None])

    gnorm = (g_norm_ffn1, g_norm_mix, g_norm_ffn2)
    subs = [dict(l=l, j=j, tag=f"{('ffn1', 'mix', 'ffn2')[j]}_l{l}", g=gnorm[j][l][None], shift=mods[l][3 * j], scale=mods[l][3 * j + 1],
                 gv=mods[l][3 * j + 2] * (1.0 if j == 1 else 0.5)) for l in range(L) for j in range(3)]
    gi = 0
    xcur = xs
    h = None
    for k, sb in enumerate(subs):
        l, j = sb["l"], sb["j"]
        if pend is not None and ag_groups[gi]["need"] == k:
            grp = ag_groups[gi]
            nm = f"l{grp['l']}_{grp['items'][0]}"
            after = xcur if k > 0 else sb["shift"]
            done = _ag_finish(_split_wait(pend[0], pend[1], pend[2], after, _ag_waits, f"ag_wait_{nm}"), f"ag_finish_{nm}")
            for i, a in zip(grp["items"], done):
                lands[grp["l"]][i] = a
            gi += 1
            pend = ag_start(ag_groups[gi]) if gi < len(ag_groups) else None
            tok = pend[3][0, 0] if pend is not None else None
        if h is None:
            h = (sb["g"] if tok is None else sb["g"] + tok, sb["shift"], sb["scale"])
            tok = None
        d = layer_weights(l, lands[l])
        nxt = (subs[k + 1]["g"], subs[k + 1]["shift"], subs[k + 1]["scale"]) if k + 1 < len(subs) else None
        gv = sb["gv"] if tok is None else sb["gv"] + tok
        tok = None
        if j == 1:
            xcur, h, sb["y"], sb["saved"] = _mix_fwd(xcur, h, gv, d["wmix"], w_dw_full[l], b_dw[l][None], g_conv_ln[l][None],
                                                     b_conv_ln[l][None], d["wgp"], d["bgp"], g_gla_norm[l], d["wout"], nxt, sb["tag"])
        else:
            w4, wo = (d["wi1"], d["wo1"]) if j == 0 else (d["wi2"], d["wo2"])
            xcur, h, sb["y"], sb["saved"] = _ffn_fwd(xcur, h, gv, w4, wo, nxt, sb["tag"])
    lw = [layer_weights(l, lands[l]) for l in range(L)]

    c_arr = jnp.stack([ci]).astype(jnp.int32)
    sc_arr = jnp.stack([s_me, ci]).astype(jnp.int32)
    fulls = [lax.empty((L,) + ((W[n].shape[1], n_pad) if n == 'w_in' else W[n].shape[1:]), F32) for n in _BIG]

    def rs_begin(gs, items, l):
        nm = f"l{l}_{items[0]}"
        sibs = _rs_sibling(gs, f"rs_sibling_{nm}")
        return sibs, [_rs_presum(g, sb_, c_arr, f"rs_presum_{i}_l{l}") for i, g, sb_ in zip(items, gs, sibs)]

    def rs_end(gs, sibs, recvs, items, l):
        summed = [_rs_sum(g, sb_, rv, fulls[i], l, sc_arr, f"rs_sum_{i}_l{l}") for i, g, sb_, rv in zip(items, gs, sibs, recvs)]
        for i, f in zip(items, _rs_share(summed, l, f"rs_share_l{l}_{items[0]}")):
            fulls[i] = f

    def rs_start(gs, items, l):
        sibs, ps = rs_begin(gs, items, l)
        pend = _split_start(ps + [lax.empty((3,) + p.shape[1:], BF16) for p in ps], 3 * len(ps), _rs_chip_copies, f"rs_start_l{l}_{items[0]}")
        return dict(gs=gs, sibs=sibs, pend=pend, items=items, l=l)

    def rs_finish(fl, after):
        pend, n = fl["pend"], len(fl["gs"])
        bufs = _split_wait(pend[0], pend[1], pend[2], after, _rs_chip_copies, f"rs_wait_l{fl['l']}_{fl['items'][0]}")
        rs_end(fl["gs"], fl["sibs"], bufs[n:], fl["items"], fl["l"])

    sq, dx, dfsh, dfsc, dgfin, dy, dgv = _loss_head(xcur, g_norm_final[None], fmod[0], fmod[1], tgt, (subs[-1]["y"], subs[-1]["gv"]))
    loss_part = 0.5 / D * jnp.sum(sq)
    G = {n: [None] * L for n in _SMALL}
    dmods = [None] * L
    in_flight = None
    tok = None
    for l in reversed(range(L)):
        gr = [None] * 3
        for j in reversed(range(3)):
            k = 3 * l + j
            sb, d = subs[k], lw[l]
            prev = (subs[k - 1]["y"], subs[k - 1]["gv"]) if k > 0 else None
            g_vec = sb["g"] if tok is None else sb["g"] + tok
            tok = None
            if j == 1:
                dx, pv, gr[j] = _mix_bwd(dx, dy, sb["saved"], g_vec, sb["scale"], prev, d["wmix"], w_dw_full[l], g_conv_ln[l][None], b_conv_ln[l][None],
                                         d["wgp"], d["bgp"], g_gla_norm[l], d["wout"], n_cols, n_pad, sb["tag"])
            else:
                w4, wo = (d["wi1"], d["wo1"]) if j == 0 else (d["wi2"], d["wo2"])
                dx, pv, gr[j] = _ffn_bwd(dx, dy, sb["saved"], g_vec, sb["scale"], prev, w4, wo, sb["tag"])
            gr[j]["dgv"] = dgv
            dy, dgv = pv if pv else (None, None)
            if j == 1 and in_flight is not None:
                rs_finish(in_flight, dx)
                in_flight = None
            if j == 1 and l == 0:
                in_flight = rs_start([gr[1]["dw_in"], gr[1]["dw_out"], gr[2]["dw_in"], gr[2]["dw_out"]], [2, 3, 4, 5], l)
                tok = in_flight["pend"][3][0, 0]
        g1, g2, g3 = gr
        if l > 0:
            in_flight = rs_start([g1["dw_in"], g1["dw_out"], g2["dw_in"], g2["dw_out"], g3["dw_in"], g3["dw_out"]], list(range(len(_BIG))), l)
            tok = in_flight["pend"][3][0, 0]
        else:
            last = rs_start([g1["dw_in"], g1["dw_out"]], [0, 1], l)
            rs_finish(in_flight, dx)
            in_flight = None
        dmods[l] = jnp.concatenate([g1["dshift"], g1["dscale"], 0.5 * g1["dgv"], g2["dshift"], g2["dscale"], g2["dgv"],
                                    g3["dshift"], g3["dscale"], 0.5 * g3["dgv"]], axis=1)[0]
        G["g_norm_ffn1"][l], G["g_norm_ffn2"][l], G["g_norm_mix"][l] = g1["dg"][0], g3["dg"][0], g2["dg"][0]
        G["w_dw"][l], G["b_dw"][l], G["g_conv_ln"][l], G["b_conv_ln"][l] = g2["dw_dw"], g2["db_dw"][0], g2["dg_ln"][0], g2["db_ln"][0]
        G["w_gate_up"][l], G["b_gate"][l], G["g_gla_norm"][l] = g2["dw_gate"], g2["db_gate"], g2["dgn"]
    grad_x = dx[None]
    gsm = {}

    small = [jnp.stack(G[n]).reshape(-1) for n in _SMALL if n != 'g_norm_final'] + [dgfin[0]]
    dmod_vec = jnp.concatenate(dmods + [dfsh[0], dfsc[0]])
    n_mod_vec = dmod_vec.shape[0]
    vec = jnp.concatenate([dmod_vec] + small + [loss_part[None]]) + last["pend"][3][0, 0]
    n_vec = vec.shape[0]
    vec = jnp.pad(vec, (0, (-n_vec) % (8 * LANE)))
    vec_all = _ag_small(vec.reshape(8, -1), "ag_small_grads").reshape(8, -1)
    vec_sum = _rowsum(vec_all, "sum_small_grads")[0]
    loss = vec_sum[n_vec - 1]
    off = n_mod_vec
    for n in _SMALL:
        shp = {'w_dw': w_dw_full.shape, 'w_gate_up': w_gu_full.shape}.get(n, W[n].shape)
        cnt = 1
        for dd in shp:
            cnt *= dd
        gsm[n] = vec_sum[off:off + cnt].reshape(shp)
        off += cnt
    gsm['w_dw'] = lax.dynamic_slice_in_dim(gsm['w_dw'], s_me * w_dw.shape[2], w_dw.shape[2], 2)
    gsm['w_gate_up'] = lax.dynamic_slice_in_dim(gsm['w_gate_up'], s_me * w_gate_up.shape[2], w_gate_up.shape[2], 2)
    dmod_sum = vec_sum[:n_mod_vec]
    gsm['b_ada'] = dmod_sum[:L * N_MOD * D].reshape(L, N_MOD * D)
    gsm['b_ada_final'] = dmod_sum[L * N_MOD * D:]
    c_t = c_all.T
    dmod_rows = vec_all[:, :n_mod_vec]
    gsm['w_ada'] = jnp.stack([
        _mod_wgrad(c_t, lax.dynamic_slice_in_dim(dmod_rows, l * N_MOD * D + s_me * nsh, nsh, 1), f"dw_ada_{l}") for l in range(L)])
    gsm['w_ada_final'] = _mod_wgrad(c_t, lax.dynamic_slice_in_dim(dmod_rows, L * N_MOD * D + s_me * nfin, nfin, 1), "dw_ada_final")
    gsm.update({n: (f[:, :, :n_cols] if n == 'w_in' else f) for n, f in zip(_BIG[2:], fulls[2:])})

    outs = {}
    small_names = [n for n in _WEIGHTS if W[n].size < 65536]
    for n in [m for m in _WEIGHTS if m not in _BIG[:2]] + list(_BIG[:2]):
        if n in small_names:
            continue
        if n == _BIG[0]:
            rs_finish(last, outs['w_ada_final'][0])
            gsm.update(dict(zip(_BIG[:2], fulls[:2])))
        shp = W[n].shape
        v2 = lambda a: a.reshape(-1, shp[-1])
        from_rs = n in _BIG and n != 'w_in'
        d_, m_, v_, *g_ = _adamw(v2(W[n]), v2(gsm[n]), v2(M1[n]), v2(M2[n]), f"adamw_{n}", copy_grad=from_rs)
        outs[n] = (d_.reshape(shp), m_.reshape(shp), v_.reshape(shp))
        if from_rs:
            gsm[n] = g_[0].reshape(shp)
    flat = lambda dct: jnp.concatenate([dct[n].reshape(-1) for n in small_names])
    n_small = sum(W[n].size for n in small_names)
    v2 = lambda a: jnp.pad(a, (0, (-n_small) % (8 * LANE))).reshape(-1, LANE)
    d_, m_, v_ = _adamw(v2(flat(W)), v2(flat(gsm)), v2(flat(M1)), v2(flat(M2)), "adamw_small")

    def unflat(a):
        res, o = {}, 0
        a = a.reshape(-1)
        for n in small_names:
            res[n] = a[o:o + W[n].size].reshape(W[n].shape)
            o += W[n].size
        return res

    for n, dd, mm, vv in zip(small_names, unflat(d_).values(), unflat(m_).values(), unflat(v_).values()):
        outs[n] = (dd, mm, vv)

    return (loss, grad_x, *[gsm[n] for n in _WEIGHTS], *[outs[n][0] for n in _WEIGHTS], *[outs[n][1] for n in _WEIGHTS], *[outs[n][2] for n in _WEIGHTS])
```

```python
import jax
import jax.numpy as jnp
from jax import lax
from jax.experimental import pallas as pl
from jax.experimental.pallas import tpu as pltpu

F32 = jnp.float32
BF16 = jnp.bfloat16

CHUNK = 64
HEADS = 4
DK = 64
DV = 128
DKP = 128
GATE_RANK = 16
GATE_TAU = 16.0
N_MOD = 9
EPS = 1e-6
ADAM_LR = 0.001
ADAM_B1 = 0.9
ADAM_B2 = 0.999
ADAM_EPS = 1e-08
ADAM_WD = 0.01
ADAM_STEP = 10

LANE = 128
HALO = 32
VMEM_LIMIT = 52 * 1024 * 1024
MESH = pl.DeviceIdType.MESH
N_CHIPS = 4

D_CONV = 512
D_GLA = HEADS * DV
ZC_Q = 0
ZC_K = ZC_Q + HEADS * DKP
ZC_V = ZC_K + HEADS * DKP
ZC_R = ZC_V + D_GLA
ZC_A = ZC_R + D_GLA
ZC_B = ZC_A + D_CONV
ZC_G = ZC_B + D_CONV
Z_COLS = ZC_G + LANE
Z_GLA = ZC_A


def _div(n, target, mult):
    best = None
    d = mult
    while d <= min(n, target):
        if n % d == 0:
            best = d
        d += mult
    return n if best is None else best


def _cp(sem=None, **kw):
    return pltpu.CompilerParams(dimension_semantics=sem, vmem_limit_bytes=VMEM_LIMIT, **kw)


def _resident(shape, index_map):
    return pl.BlockSpec(shape, index_map, pipeline_mode=pl.Buffered(1))


def _sigmoid(x):
    return 0.5 * jnp.tanh(0.5 * x) + 0.5


def _dot(a, b):
    return jnp.dot(a.astype(BF16), b.astype(BF16), preferred_element_type=F32)


def _dot_nt(a, b):
    return lax.dot_general(a.astype(BF16), b.astype(BF16), (((1,), (1,)), ((), ())), preferred_element_type=F32)


def _dot_tn(a, b):
    return lax.dot_general(a.astype(BF16), b.astype(BF16), (((0,), (0,)), ((), ())), preferred_element_type=F32)


def _dot_exact(a, b):
    return jnp.dot(a, b, preferred_element_type=F32, precision=lax.Precision.HIGHEST)


def _norm_rows(xv, g, shift, scale):
    r = lax.rsqrt(jnp.mean(xv * xv, axis=-1, keepdims=True) + EPS)
    return (xv * r) * g * (1.0 + scale) + shift


def _loss_head(x, g, shift, scale, tgt, prev):
    S, D = x.shape
    tm = _div(S, 512, 8)

    def body(x_ref, g_ref, sh_ref, sc_ref, t_ref, y_ref, gvp_ref, sq_ref, dx_ref, dsh_ref, dsc_ref, dg_ref, dy_ref, dgv_ref):
        sums = (sq_ref, dsh_ref, dsc_ref, dg_ref, dgv_ref)

        @pl.when(pl.program_id(0) == 0)
        def _():
            for o in sums:
                o[...] = jnp.zeros_like(o)

        xv = x_ref[...]
        e = _norm_rows(xv, g_ref[...], sh_ref[...], sc_ref[...]) - t_ref[...]
        dx, dsh, dsc, dg, dy, dgv = _normmod_bwd_rows(xv, e * (1.0 / D), None, g_ref[...], sc_ref[...], (y_ref[...], gvp_ref[...]))
        dx_ref[...] = dx
        dy_ref[...] = dy
        for o, v in zip(sums, (jnp.sum(e * e, axis=0, keepdims=True), dsh, dsc, dg, dgv)):
            o[...] += v

    row = pl.BlockSpec((tm, D), lambda i: (i, 0))
    vec = pl.BlockSpec((1, D), lambda i: (0, 0))
    vs = jax.ShapeDtypeStruct((1, D), F32)
    return pl.pallas_call(
        body, grid=(S // tm,), in_specs=[row, vec, vec, vec, row, row, vec], out_specs=[vec, row, vec, vec, vec, row, vec],
        out_shape=[vs, jax.ShapeDtypeStruct((S, D), F32), vs, vs, vs, jax.ShapeDtypeStruct((S, D), BF16), vs],
        compiler_params=_cp(("arbitrary",)), name="loss_head",
    )(x, g, shift, scale, tgt, *prev)


def _normmod_bwd_rows(xv, dh, dres, gv, sc, prev):
    r = lax.rsqrt(jnp.mean(xv * xv, axis=-1, keepdims=True) + EPS)
    xh = xv * r
    dsh = jnp.sum(dh, axis=0, keepdims=True)
    dsc = jnp.sum(dh * (xh * gv), axis=0, keepdims=True)
    dn = dh * (1.0 + sc)
    dg = jnp.sum(dn * xh, axis=0, keepdims=True)
    dxh = dn * gv
    dx = r * (dxh - xh * jnp.mean(dxh * xh, axis=-1, keepdims=True))
    if dres is not None:
        dx = dx + dres
    if prev is None:
        return dx, dsh, dsc, dg
    y, gvp = prev
    return dx, dsh, dsc, dg, (gvp * dx).astype(BF16), jnp.sum(dx * y, axis=0, keepdims=True)


def _mm(pairs, out_dtype, name, nt=False):
    M = pairs[0][0].shape[0]
    N = pairs[0][1].shape[0] if nt else pairs[0][1].shape[1]
    ktot = sum(a.shape[1] for a, _, _ in pairs)
    tm = _div(M, 512 if ktot <= 4096 else 256, 8)
    n = len(pairs)

    def body(*refs):
        o_ref = refs[2 * n]
        dot = _dot_nt if nt else _dot
        acc = dot(refs[0][...], refs[1][...])
        for p in range(1, n):
            acc = acc + dot(refs[2 * p][...], refs[2 * p + 1][...])
        o_ref[...] = acc.astype(o_ref.dtype)

    ins, args = [], []
    for a, b, blk in pairs:
        k = a.shape[1]
        ins.append(pl.BlockSpec((tm, k), lambda i: (i, 0)))
        ins.append(_resident((N, k), lambda i, blk=blk: (0, blk)) if nt else _resident((k, N), lambda i: (0, 0)))
        args += [a, b]
    return pl.pallas_call(
        body, grid=(M // tm,), in_specs=ins, out_specs=pl.BlockSpec((tm, N), lambda i: (i, 0)),
        out_shape=jax.ShapeDtypeStruct((M, N), out_dtype), compiler_params=_cp(("parallel",)), name=name,
    )(*args)


TN_ROWS = 2048


def _mm_tn(a, g, name):
    S, Ka = a.shape
    N = g.shape[1]
    tk = _div(Ka, 1408, LANE)
    tn = _div(N, 1408, LANE)
    ts = _div(S, TN_ROWS, 8)

    def body(a_ref, g_ref, o_ref):
        @pl.when(pl.program_id(2) == 0)
        def _():
            o_ref[...] = jnp.zeros_like(o_ref)

        o_ref[...] += _dot_tn(a_ref[...], g_ref[...])

    return pl.pallas_call(
        body, grid=(Ka // tk, N // tn, S // ts),
        in_specs=[pl.BlockSpec((ts, tk), lambda i, j, s: (s, i)), pl.BlockSpec((ts, tn), lambda i, j, s: (s, j))],
        out_specs=pl.BlockSpec((tk, tn), lambda i, j, s: (i, j)),
        out_shape=jax.ShapeDtypeStruct((Ka, N), F32),
        compiler_params=_cp(("parallel", "parallel", "arbitrary")), name=name,
    )(a, g)


def _mm_tn_two(a0, a1, g, name):
    S, K = a0.shape
    N = g.shape[1]
    ts = _div(S, TN_ROWS, 8)

    def body(a0_ref, a1_ref, g_ref, o_ref):
        i = pl.program_id(0)

        @pl.when(pl.program_id(1) == 0)
        def _():
            o_ref[...] = jnp.zeros_like(o_ref)

        @pl.when(i == 0)
        def _():
            o_ref[...] += _dot_tn(a0_ref[...], g_ref[...])

        @pl.when(i == 1)
        def _():
            o_ref[...] += _dot_tn(a1_ref[...], g_ref[...])

    return pl.pallas_call(
        body, grid=(2, S // ts),
        in_specs=[pl.BlockSpec((ts, K), lambda i, s: (jnp.where(i == 0, s, 0), 0)),
                  pl.BlockSpec((ts, K), lambda i, s: (jnp.where(i == 1, s, 0), 0)),
                  pl.BlockSpec((ts, N), lambda i, s: (s, 0))],
        out_specs=pl.BlockSpec((K, N), lambda i, s: (i, 0)),
        out_shape=jax.ShapeDtypeStruct((2 * K, N), F32),
        compiler_params=_cp(("parallel", "arbitrary")), name=name,
    )(a0, a1, g)


def _swiglu(gt, up):
    return gt * _sigmoid(gt) * up


def _ffn_up(h, w4, name, norm=None):
    S, D = h.shape
    ns, _, C = w4.shape
    hs = ns // 2
    tm = _div(S, 256, 8)
    nn = 3 if norm else 0

    def body(h_ref, w_ref, *rest):
        z_ref, a_ref = rest[nn:nn + 2]
        if norm:
            hv = _norm_rows(h_ref[...], rest[0][...], rest[1][...], rest[2][...]).astype(BF16)
            rest[nn + 2][...] = hv
        else:
            hv = h_ref[...]
        for s in range(hs):
            gt = _dot(hv, w_ref[s])
            up = _dot(hv, w_ref[hs + s])
            sg = _sigmoid(gt)
            silu = gt * sg
            z_ref[:, s * C:(s + 1) * C] = (up * (sg * (1.0 + gt * (1.0 - sg)))).astype(BF16)
            z_ref[:, (hs + s) * C:(hs + s + 1) * C] = silu.astype(BF16)
            a_ref[:, s * C:(s + 1) * C] = (silu * up).astype(BF16)

    row = pl.BlockSpec((tm, D), lambda i: (i, 0))
    vec = pl.BlockSpec((1, D), lambda i: (0, 0))
    return pl.pallas_call(
        body, grid=(S // tm,), in_specs=[row, _resident((ns, D, C), lambda i: (0, 0, 0))] + [vec] * nn,
        out_specs=[pl.BlockSpec((tm, ns * C), lambda i: (i, 0)), pl.BlockSpec((tm, hs * C), lambda i: (i, 0))] + [row] * (nn // 3),
        out_shape=[jax.ShapeDtypeStruct((S, ns * C), BF16), jax.ShapeDtypeStruct((S, hs * C), BF16)] + [jax.ShapeDtypeStruct((S, D), BF16)] * (nn // 3),
        compiler_params=_cp(("parallel",)), name=name,
    )(h, w4, *(norm or ()))


def _resid_outputs(y, x_ref, gv_ref, nxt_refs, out_refs):
    out_refs[0][...] = y
    xn = x_ref[...] + gv_ref[...] * y
    out_refs[1][...] = xn
    if nxt_refs:
        out_refs[2][...] = _norm_rows(xn, nxt_refs[0][...], nxt_refs[1][...], nxt_refs[2][...]).astype(BF16)


def _ffn_down(act, wo, x, gv, nxt, name):
    S = act.shape[0]
    Fd, D = wo.shape
    tm = _div(S, 512, 8)
    nn = 3 if nxt else 0

    def body(a_ref, w_ref, x_ref, gv_ref, *rest):
        _resid_outputs(_dot(a_ref[...], w_ref[...]), x_ref, gv_ref, rest[:nn], rest[nn:])

    row = pl.BlockSpec((tm, D), lambda i: (i, 0))
    vec = pl.BlockSpec((1, D), lambda i: (0, 0))
    os_ = jax.ShapeDtypeStruct((S, D), F32)
    return pl.pallas_call(
        body, grid=(S // tm,),
        in_specs=[pl.BlockSpec((tm, Fd), lambda i: (i, 0)), _resident((Fd, D), lambda i: (0, 0)), row, vec] + [vec] * nn,
        out_specs=[row, row] + [row] * (nn // 3), out_shape=[os_, os_] + [jax.ShapeDtypeStruct((S, D), BF16)] * (nn // 3),
        compiler_params=_cp(("parallel",)), name=name,
    )(act, wo, x, gv, *(nxt or ()))


def _dw_ffn_in(h, dz, ns, name):
    S, D = h.shape
    C = dz.shape[1] // ns
    ts = _div(S, TN_ROWS, 8)

    def body(h_ref, g_ref, o_ref):
        @pl.when(pl.program_id(1) == 0)
        def _():
            o_ref[...] = jnp.zeros_like(o_ref)

        o_ref[...] += _dot_tn(h_ref[...], g_ref[...])

    return pl.pallas_call(
        body, grid=(ns, S // ts),
        in_specs=[pl.BlockSpec((ts, D), lambda j, s: (s, 0)), pl.BlockSpec((ts, C), lambda j, s: (s, j))],
        out_specs=pl.BlockSpec((None, D, C), lambda j, s: (j, 0, 0)), out_shape=jax.ShapeDtypeStruct((ns, D, C), F32),
        compiler_params=_cp(("parallel", "arbitrary")), name=name,
    )(h, dz)


def _dh_normmod_bwd(pairs, x, dres, g, scale, prev, name):
    S, D = x.shape
    tm = _div(S, 256, 8)
    n = len(pairs)
    with_prev = prev is not None

    def body(*refs):
        refs = list(refs)
        mm = refs[:2 * n]
        x_ref, dr_ref, g_ref, sc_ref = refs[2 * n:2 * n + 4]
        outs = refs[2 * n + 4 + 2 * with_prev:]

        @pl.when(pl.program_id(0) == 0)
        def _():
            for o in outs[1:4] + outs[5:]:
                o[...] = jnp.zeros_like(o)

        dh = _dot_nt(mm[0][...], mm[1][...])
        for p in range(1, n):
            dh = dh + _dot_nt(mm[2 * p][...], mm[2 * p + 1][...])
        pv = (refs[2 * n + 4][...], refs[2 * n + 5][...]) if with_prev else None
        res = _normmod_bwd_rows(x_ref[...], dh, dr_ref[...], g_ref[...], sc_ref[...], pv)
        outs[0][...] = res[0]
        for o, v in zip(outs[1:4], res[1:4]):
            o[...] += v
        if with_prev:
            outs[4][...] = res[4]
            outs[5][...] += res[5]

    row = pl.BlockSpec((tm, D), lambda i: (i, 0))
    vec = pl.BlockSpec((1, D), lambda i: (0, 0))
    ins, args = [], []
    for a, b, a_blk, b_blk, k in pairs:
        ins.append(pl.BlockSpec((tm, k), lambda i, a_blk=a_blk: (i, a_blk)))
        ins.append(_resident((None, D, k), lambda i, b_blk=b_blk: (b_blk, 0, 0)) if b.ndim == 3 else _resident((D, k), lambda i, b_blk=b_blk: (0, b_blk)))
        args += [a, b]
    ins += [row, row, vec, vec] + [row, vec] * with_prev
    args += [x, dres, g, scale] + (list(prev) if with_prev else [])
    vs = jax.ShapeDtypeStruct((1, D), F32)
    return pl.pallas_call(
        body, grid=(S // tm,), in_specs=ins, out_specs=[row, vec, vec, vec] + [row, vec] * with_prev,
        out_shape=[jax.ShapeDtypeStruct((S, D), F32), vs, vs, vs] + [jax.ShapeDtypeStruct((S, D), BF16), vs] * with_prev,
        compiler_params=_cp(("arbitrary",)), name=name,
    )(*args)


def _ffn_bwd_core(dy, wo, z, w4, x, dres, g, scale, prev, name):
    S, D = x.shape
    Fd = wo.shape[0]
    ns, _, C = w4.shape
    tm = _div(S, 256, 8)
    with_prev = prev is not None

    def body(dy_ref, wo_ref, q_ref, p_ref, w4_ref, x_ref, dr_ref, g_ref, sc_ref, *rest):
        outs = rest[2 * with_prev:]
        dz_ref, outs = outs[0], outs[1:]

        @pl.when(pl.program_id(0) == 0)
        def _():
            for o in outs[1:4] + outs[5:]:
                o[...] = jnp.zeros_like(o)

        da = _dot_nt(dy_ref[...], wo_ref[...])
        dz_ref[:, :Fd] = (da * q_ref[...].astype(F32)).astype(BF16)
        dz_ref[:, Fd:] = (da * p_ref[...].astype(F32)).astype(BF16)
        dh = _dot_nt(dz_ref[:, 0:C], w4_ref[0])
        for s in range(1, ns):
            dh = dh + _dot_nt(dz_ref[:, s * C:(s + 1) * C], w4_ref[s])
        pv = (rest[0][...], rest[1][...]) if with_prev else None
        res = _normmod_bwd_rows(x_ref[...], dh, dr_ref[...], g_ref[...], sc_ref[...], pv)
        outs[0][...] = res[0]
        for o, v in zip(outs[1:4], res[1:4]):
            o[...] += v
        if with_prev:
            outs[4][...] = res[4]
            outs[5][...] += res[5]

    row = pl.BlockSpec((tm, D), lambda i: (i, 0))
    vec = pl.BlockSpec((1, D), lambda i: (0, 0))
    wide = pl.BlockSpec((tm, 2 * Fd), lambda i: (i, 0))
    vs = jax.ShapeDtypeStruct((1, D), F32)
    return pl.pallas_call(
        body, grid=(S // tm,),
        in_specs=[row, _resident((Fd, D), lambda i: (0, 0)), pl.BlockSpec((tm, Fd), lambda i: (i, 0)), pl.BlockSpec((tm, Fd), lambda i: (i, 1)),
                  _resident((ns, D, C), lambda i: (0, 0, 0)), row, row, vec, vec] + [row, vec] * with_prev,
        out_specs=[wide, row, vec, vec, vec] + [row, vec] * with_prev,
        out_shape=[jax.ShapeDtypeStruct((S, 2 * Fd), BF16), jax.ShapeDtypeStruct((S, D), F32), vs, vs, vs] + [jax.ShapeDtypeStruct((S, D), BF16), vs] * with_prev,
        compiler_params=_cp(("arbitrary",)), name=name,
    )(dy, wo, z, z, w4, x, dres, g, scale, *(prev or ()))


def _mix_out(yconv, ygla, wout, x, gv, nxt, name):
    S, Kc = yconv.shape
    Kg = ygla.shape[1]
    D = wout.shape[1]
    tm = _div(S, 512, 8)
    nn = 3 if nxt else 0

    def body(a_ref, b_ref, w_ref, x_ref, gv_ref, *rest):
        y = _dot(a_ref[...], w_ref[0:Kc, :]) + _dot(b_ref[...], w_ref[Kc:Kc + Kg, :])
        _resid_outputs(y, x_ref, gv_ref, rest[:nn], rest[nn:])

    row = pl.BlockSpec((tm, D), lambda i: (i, 0))
    vec = pl.BlockSpec((1, D), lambda i: (0, 0))
    os_ = jax.ShapeDtypeStruct((S, D), F32)
    return pl.pallas_call(
        body, grid=(S // tm,),
        in_specs=[pl.BlockSpec((tm, Kc), lambda i: (i, 0)), pl.BlockSpec((tm, Kg), lambda i: (i, 0)), _resident((Kc + Kg, D), lambda i: (0, 0)), row,
                  vec] + [vec] * nn,
        out_specs=[row, row] + [row] * (nn // 3), out_shape=[os_, os_] + [jax.ShapeDtypeStruct((S, D), BF16)] * (nn // 3),
        compiler_params=_cp(("parallel",)), name=name,
    )(yconv, ygla, wout, x, gv, *(nxt or ()))


def _ln_parts(yc, g, b):
    mu = jnp.mean(yc, axis=-1, keepdims=True)
    xc = yc - mu
    rs = lax.rsqrt(jnp.mean(xc * xc, axis=-1, keepdims=True) + EPS)
    xh = xc * rs
    return xh, rs, xh * g + b


SUB = 8
CONV_ROWS = 32


def _shifted_copies(ext8, rows):
    for b in range(1, SUB):
        ext8[b, pl.ds(0, rows - SUB), :] = ext8[0, pl.ds(b, rows - SUB), :]


def _tap(o):
    return o % SUB, o - o % SUB


def _conv_fwd(z, w_dw, b_dw, g_ln, b_ln, name):
    S = z.shape[0]
    W, C = w_dw.shape
    ts = _div(S, 512, HALO)
    hb = ts // HALO
    off = HALO - (W - 1)
    ca, cb = ZC_A // C, ZC_B // C
    rb = 2 * CONV_ROWS

    def body(a_ref, b_ref, pa_ref, pb_ref, w_ref, bd_ref, g_ref, bl_ref, u_ref, yc_ref, o_ref, ext8):
        keep = (pl.program_id(0) > 0).astype(F32)
        u = a_ref[...] * _sigmoid(b_ref[...])
        ext8[0, pl.ds(0, HALO), :] = pa_ref[...] * _sigmoid(pb_ref[...]) * keep
        ext8[0, pl.ds(HALO, ts), :] = u
        u_ref[...] = u
        _shifted_copies(ext8, ts + HALO)

        for lg in range(C // LANE):
            lanes = pl.ds(lg * LANE, LANE)
            taps = [jnp.broadcast_to(w_ref[pl.ds(j, 1), lanes], (SUB, LANE)) for j in range(W)]
            bias = jnp.broadcast_to(bd_ref[:, lanes], (SUB, LANE))

            def sub(i, carry, lanes=lanes, taps=taps, bias=bias):
                r0 = pl.multiple_of(i * rb, rb)
                accs = [bias] * (rb // SUB)
                for j in range(W):
                    b, a = _tap(off + j)
                    for r in range(rb // SUB):
                        accs[r] = accs[r] + taps[j] * ext8[b, pl.ds(r0 + a + r * SUB, SUB), lanes]
                for r in range(rb // SUB):
                    yc_ref[pl.ds(r0 + r * SUB, SUB), lanes] = accs[r]
                return carry

            lax.fori_loop(0, ts // rb, sub, 0)
        _, _, ln = _ln_parts(yc_ref[...], g_ref[...], bl_ref[...])
        o_ref[...] = (ln * _sigmoid(ln)).astype(BF16)

    cur = lambda col: pl.BlockSpec((ts, C), lambda i: (i, col))
    prev = lambda col: pl.BlockSpec((HALO, C), lambda i: (jnp.maximum(i * hb - 1, 0), col))
    vec = pl.BlockSpec((1, C), lambda i: (0, 0))
    row = pl.BlockSpec((ts, C), lambda i: (i, 0))
    fs = jax.ShapeDtypeStruct((S, C), F32)
    return pl.pallas_call(
        body, grid=(S // ts,),
        in_specs=[cur(ca), cur(cb), prev(ca), prev(cb), pl.BlockSpec((W, C), lambda i: (0, 0)), vec, vec, vec],
        out_specs=[row, row, row], out_shape=[fs, fs, jax.ShapeDtypeStruct((S, C), BF16)],
        scratch_shapes=[pltpu.VMEM((SUB, ts + HALO, C), F32)],
        compiler_params=_cp(("parallel",)), name=name,
    )(z, z, z, z, w_dw, b_dw, g_ln, b_ln)


def _conv_bwd(dycat, z, u, yc, w_dw, g_ln, b_ln, name):
    S = z.shape[0]
    W, C = w_dw.shape
    ts = _div(S, 512, HALO)
    hb = ts // HALO
    nblk = S // ts
    off = HALO - (W - 1)
    ca, cb = ZC_A // C, ZC_B // C
    rb = CONV_ROWS

    def ln_silu_bwd(dy, ycv, g, b):
        xh, rs, ln = _ln_parts(ycv, g, b)
        sl = _sigmoid(ln)
        dln = dy * (sl * (1.0 + ln * (1.0 - sl)))
        dxh = dln * g
        dyc = rs * (dxh - jnp.mean(dxh, axis=-1, keepdims=True) - xh * jnp.mean(dxh * xh, axis=-1, keepdims=True))
        return dyc, dln, xh

    def body(dy_ref, ndy_ref, yc_ref, nyc_ref, u_ref, pu_ref, a_ref, b_ref, w_ref, g_ref, bl_ref,
             dab_ref, dw_ref, dbd_ref, dg_ref, dbl_ref, uext8, dext8, dwacc):
        i = pl.program_id(0)

        @pl.when(i == 0)
        def _():
            dwacc[...] = jnp.zeros_like(dwacc)
            dbd_ref[...] = jnp.zeros_like(dbd_ref)
            dg_ref[...] = jnp.zeros_like(dg_ref)
            dbl_ref[...] = jnp.zeros_like(dbl_ref)

        g = g_ref[...]
        bl = bl_ref[...]
        dyc, dln, xh = ln_silu_bwd(dy_ref[...], yc_ref[...], g, bl)
        ndyc, _, _ = ln_silu_bwd(ndy_ref[...], nyc_ref[...], g, bl)
        dg_ref[...] += jnp.sum(dln * xh, axis=0, keepdims=True)
        dbl_ref[...] += jnp.sum(dln, axis=0, keepdims=True)
        dbd_ref[...] += jnp.sum(dyc, axis=0, keepdims=True)
        dext8[0, pl.ds(0, ts), :] = dyc
        dext8[0, pl.ds(ts, HALO), :] = ndyc * (i < nblk - 1).astype(F32)
        uext8[0, pl.ds(0, HALO), :] = pu_ref[...] * (i > 0).astype(F32)
        uext8[0, pl.ds(HALO, ts), :] = u_ref[...]
        _shifted_copies(dext8, ts + HALO)
        _shifted_copies(uext8, ts + HALO)

        def sub(k, carry):
            r0 = pl.multiple_of(k * rb, rb)
            rows = pl.ds(r0, rb)
            dyt = dext8[0, rows, :]
            du = jnp.zeros((rb, C), F32)
            for j in range(W):
                b, a = _tap(W - 1 - j)
                du = du + w_ref[pl.ds(j, 1), :] * dext8[b, pl.ds(r0 + a, rb), :]
                b, a = _tap(off + j)
                p = dyt * uext8[b, pl.ds(r0 + a, rb), :]
                part = p[0:SUB]
                for q in range(1, rb // SUB):
                    part = part + p[q * SUB:(q + 1) * SUB]
                dwacc[j] += part
            sb = _sigmoid(b_ref[rows, :])
            dab_ref[rows, 0:C] = (du * sb).astype(BF16)
            dab_ref[rows, C:2 * C] = (du * a_ref[rows, :] * sb * (1.0 - sb)).astype(BF16)
            return carry

        lax.fori_loop(0, ts // rb, sub, 0)

        @pl.when(i == nblk - 1)
        def _():
            for j in range(W):
                dw_ref[pl.ds(j, 1), :] = jnp.sum(dwacc[j], axis=0, keepdims=True)

    row = pl.BlockSpec((ts, C), lambda i: (i, 0))
    nxt = pl.BlockSpec((HALO, C), lambda i: (jnp.minimum((i + 1) * hb, S // HALO - 1), 0))
    prv = pl.BlockSpec((HALO, C), lambda i: (jnp.maximum(i * hb - 1, 0), 0))
    vec = pl.BlockSpec((1, C), lambda i: (0, 0))
    wsp = pl.BlockSpec((W, C), lambda i: (0, 0))
    vs = jax.ShapeDtypeStruct((1, C), F32)
    return pl.pallas_call(
        body, grid=(nblk,),
        in_specs=[row, nxt, row, nxt, row, prv, pl.BlockSpec((ts, C), lambda i: (i, ca)), pl.BlockSpec((ts, C), lambda i: (i, cb)), wsp, vec, vec],
        out_specs=[pl.BlockSpec((ts, 2 * C), lambda i: (i, 0)), wsp, vec, vec, vec],
        out_shape=[jax.ShapeDtypeStruct((S, 2 * C), BF16), jax.ShapeDtypeStruct((W, C), F32), vs, vs, vs],
        scratch_shapes=[pltpu.VMEM((SUB, ts + HALO, C), F32), pltpu.VMEM((SUB, ts + HALO, C), F32), pltpu.VMEM((W, SUB, C), F32)],
        compiler_params=_cp(("arbitrary",)), name=name,
    )(dycat, dycat, yc, yc, u, u, z, z, w_dw, g_ln, b_ln)


def _log_gate(zg):
    return (jnp.minimum(zg, 0.0) - jnp.log(1.0 + jnp.exp(-jnp.abs(zg)))) * (1.0 / GATE_TAU)


def _loggate(z, wgp, bgp, name):
    S = z.shape[0]
    N = wgp.shape[1]
    ts = _div(S, 512, 8)

    def body(g_ref, w_ref, b_ref, o_ref):
        o_ref[...] = _log_gate(_dot(g_ref[...], w_ref[...]) + b_ref[...])

    return pl.pallas_call(
        body, grid=(S // ts,),
        in_specs=[pl.BlockSpec((ts, LANE), lambda i: (i, ZC_G // LANE)), pl.BlockSpec((LANE, N), lambda i: (0, 0)), pl.BlockSpec((1, N), lambda i: (0, 0))],
        out_specs=pl.BlockSpec((ts, N), lambda i: (i, 0)), out_shape=jax.ShapeDtypeStruct((S, N), F32),
        compiler_params=_cp(("parallel",)), name=name,
    )(z, wgp, bgp)


def _loggate_bwd(dla, z, wgp, wgp_t, bgp, name):
    S = z.shape[0]
    N = wgp.shape[1]
    ts = _div(S, 512, 8)

    def body(dla_ref, g_ref, w_ref, wt_ref, b_ref, dg_ref, dw_ref, db_ref):
        @pl.when(pl.program_id(0) == 0)
        def _():
            dw_ref[...] = jnp.zeros_like(dw_ref)
            db_ref[...] = jnp.zeros_like(db_ref)

        glr = g_ref[...]
        zg = _dot(glr, w_ref[...]) + b_ref[...]
        dzg = dla_ref[...] * (1.0 / GATE_TAU) * (1.0 - _sigmoid(zg))
        dg_ref[...] = _dot(dzg, wt_ref[...]).astype(BF16)
        dw_ref[...] += _dot_tn(glr, dzg)
        db_ref[...] += jnp.sum(dzg, axis=0, keepdims=True)

    return pl.pallas_call(
        body, grid=(S // ts,),
        in_specs=[pl.BlockSpec((ts, N), lambda i: (i, 0)), pl.BlockSpec((ts, LANE), lambda i: (i, ZC_G // LANE)),
                  pl.BlockSpec((LANE, N), lambda i: (0, 0)), pl.BlockSpec((N, LANE), lambda i: (0, 0)), pl.BlockSpec((1, N), lambda i: (0, 0))],
        out_specs=[pl.BlockSpec((ts, LANE), lambda i: (i, 0)), pl.BlockSpec((LANE, N), lambda i: (0, 0)), pl.BlockSpec((1, N), lambda i: (0, 0))],
        out_shape=[jax.ShapeDtypeStruct((S, LANE), BF16), jax.ShapeDtypeStruct((LANE, N), F32), jax.ShapeDtypeStruct((1, N), F32)],
        compiler_params=_cp(("arbitrary",)), name=name,
    )(dla, z, wgp, wgp_t, bgp)


def _bdot(a, b, ca, cb):
    return lax.dot_general(a.astype(BF16), b.astype(BF16), (((ca,), (cb,)), ((0,), (0,))), preferred_element_type=F32)


def _bdot_exact(a, b):
    return lax.dot_general(a, b, (((2,), (1,)), ((0,), (0,))), preferred_element_type=F32, precision=lax.Precision.HIGHEST)


def _tiles(ref, cpb):
    return jnp.stack([ref[pl.ds(c * CHUNK, CHUNK), pl.ds(h * LANE, LANE)] for c in range(cpb) for h in range(HEADS)])


def _tri_masks(n):
    ri = lax.broadcasted_iota(jnp.int32, (n, CHUNK, CHUNK), 1)
    ci = lax.broadcasted_iota(jnp.int32, (n, CHUNK, CHUNK), 2)
    return ri >= ci, (ri >= ci).astype(F32), (ri <= ci).astype(F32)


def _chunk_fwd_terms(q, k, la, tril):
    bc = _bdot_exact(tril, la)
    bend = jnp.sum(la, axis=1, keepdims=True)
    eb = jnp.exp(bc)
    enb = jnp.exp(-bc)
    ee = jnp.exp(bend - bc)
    qs = q * (DK ** -0.5)
    return bend, eb, enb, ee, qs * eb, qs * enb, k * enb, k * eb, k * ee


def _gla_fwd(z, la, gn, name):
    S = z.shape[0]
    W = HEADS * LANE
    tb = _div(S, 512, CHUNK)
    cpb = tb // CHUNK

    def body(q_ref, k_ref, v_ref, r_ref, la_ref, gn_ref, o_ref, sp_ref, y_ref, st):
        @pl.when(pl.program_id(0) == 0)
        def _():
            st[...] = jnp.zeros_like(st)

        tri, tril, _ = _tri_masks(cpb * HEADS)
        q, k, v, rv, lav = (_tiles(r, cpb) for r in (q_ref, k_ref, v_ref, r_ref, la_ref))
        bend, _, _, _, qf, qb, kb, kf, ke = _chunk_fwd_terms(q, k, lav, tril)
        att = jnp.where(tri, _bdot(qf, kb, 2, 2), _bdot(qb, kf, 2, 2))
        o_intra = _bdot(att, v, 2, 1)
        u = _bdot(v, ke, 1, 1)
        gdec = jnp.exp(bend)
        s_prev = [None] * (cpb * HEADS)
        for h in range(HEADS):
            s = st[h]
            for c in range(cpb):
                b = c * HEADS + h
                s_prev[b] = s
                s = s * gdec[b] + u[b]
            st[h] = s
        s_prev = jnp.stack(s_prev)
        o = o_intra + _bdot(qf, s_prev, 2, 2)
        rms = lax.rsqrt(jnp.mean(o * o, axis=-1, keepdims=True) + EPS)
        gn = jnp.stack([gn_ref[pl.ds(h, 1), :] for _ in range(cpb) for h in range(HEADS)])
        y = (o * rms * gn * (rv * _sigmoid(rv))).astype(BF16)
        for c in range(cpb):
            for h in range(HEADS):
                b = c * HEADS + h
                rows, ln = pl.ds(c * CHUNK, CHUNK), pl.ds(h * LANE, LANE)
                o_ref[rows, ln] = o[b]
                y_ref[rows, ln] = y[b]
                sp_ref[h, c] = s_prev[b]

    zb = lambda base: pl.BlockSpec((tb, W), lambda i: (i, base // W))
    hb_ = pl.BlockSpec((tb, W), lambda i: (i, 0))
    return pl.pallas_call(
        body, grid=(S // tb,),
        in_specs=[zb(ZC_Q), zb(ZC_K), zb(ZC_V), zb(ZC_R), hb_, pl.BlockSpec((HEADS, DV), lambda i: (0, 0))],
        out_specs=[hb_, pl.BlockSpec((HEADS, cpb, DV, DKP), lambda i: (0, i, 0, 0)), hb_],
        out_shape=[jax.ShapeDtypeStruct((S, D_GLA), F32), jax.ShapeDtypeStruct((HEADS, S // CHUNK, DV, DKP), F32),
                   jax.ShapeDtypeStruct((S, D_GLA), BF16)],
        scratch_shapes=[pltpu.VMEM((HEADS, DV, DKP), F32)],
        compiler_params=_cp(("arbitrary",)), name=name,
    )(z, z, z, z, la, gn)


def _gla_bwd(dycat, z, la, o_raw, sprev, gn, name):
    S = z.shape[0]
    W = HEADS * LANE
    tb = _div(S, 512, CHUNK)
    cpb = tb // CHUNK
    nb = S // tb

    def body(q_ref, k_ref, v_ref, r_ref, la_ref, o_ref, sp_ref, dy_ref, gn_ref, dz_ref, dla_ref, dgn_ref, dst):
        @pl.when(pl.program_id(0) == 0)
        def _():
            dst[...] = jnp.zeros_like(dst)
            dgn_ref[...] = jnp.zeros_like(dgn_ref)

        nt = cpb * HEADS
        tri, tril, triu = _tri_masks(nt)
        q, k, v, rv, lav, o, dy = (_tiles(r, cpb) for r in (q_ref, k_ref, v_ref, r_ref, la_ref, o_ref, dy_ref))
        bend, eb, enb, ee, qf, qb, kb, kf, ke = _chunk_fwd_terms(q, k, lav, tril)
        att = jnp.where(tri, _bdot(qf, kb, 2, 2), _bdot(qb, kf, 2, 2))
        s_prev = jnp.stack([sp_ref[h, c] for c in range(cpb) for h in range(HEADS)])
        gdec = jnp.exp(bend)
        gn = jnp.stack([gn_ref[pl.ds(h, 1), :] for _ in range(cpb) for h in range(HEADS)])
        rms = lax.rsqrt(jnp.mean(o * o, axis=-1, keepdims=True) + EPS)
        oh = o * rms
        sg = _sigmoid(rv)
        sr = rv * sg
        d_r = (dy * oh * gn * (sg * (1.0 + rv * (1.0 - sg)))).astype(BF16)
        dgn = jnp.sum(dy * sr * oh, axis=1, keepdims=True)
        w = dy * sr * gn
        do = rms * (w - oh * jnp.mean(w * oh, axis=-1, keepdims=True))
        p = _bdot(do, qf, 1, 1)
        ds = [None] * nt
        for h in range(HEADS):
            s = dst[h]
            for c in reversed(range(cpb)):
                b = c * HEADS + h
                ds[b] = s
                s = s * gdec[b] + p[b]
            dst[h] = s
            dgn_ref[pl.ds(h, 1), :] += sum(dgn[c * HEADS + h] for c in range(cpb))
        ds = jnp.stack(ds)
        datt = _bdot(do, v, 2, 2)
        daf = jnp.where(tri, datt, 0.0)
        dab = jnp.where(tri, 0.0, datt)
        d_v = (_bdot(att, do, 1, 1) + _bdot(ke, ds, 2, 2)).astype(BF16)
        dke = _bdot(v, ds, 2, 1)
        dqf = _bdot(daf, kb, 2, 1) + _bdot(do, s_prev, 2, 1)
        dkb = _bdot(daf, qf, 1, 1)
        dqb = _bdot(dab, kf, 2, 1)
        dkf = _bdot(dab, qb, 1, 1)
        dg = jnp.sum(ds * s_prev, axis=1, keepdims=True)
        d_q = ((dqf * eb + dqb * enb) * (DK ** -0.5)).astype(BF16)
        d_k = (dkb * enb + dkf * eb + dke * ee).astype(BF16)
        dbc = dqf * qf - dkb * kb - dqb * qb + dkf * kf - dke * ke
        dbend = jnp.sum(dke * ke, axis=1, keepdims=True) + dg * gdec
        dla = _bdot_exact(triu, dbc) + dbend
        for c in range(cpb):
            for h in range(HEADS):
                b = c * HEADS + h
                rows = pl.ds(c * CHUNK, CHUNK)
                for base, val in ((ZC_Q, d_q), (ZC_K, d_k), (ZC_V, d_v), (ZC_R, d_r)):
                    dz_ref[rows, pl.ds(base + h * LANE, LANE)] = val[b]
                dla_ref[rows, pl.ds(h * LANE, LANE)] = dla[b]

    zb = lambda base: pl.BlockSpec((tb, W), lambda i: (nb - 1 - i, base // W))
    hb_ = pl.BlockSpec((tb, W), lambda i: (nb - 1 - i, 0))
    return pl.pallas_call(
        body, grid=(nb,),
        in_specs=[zb(ZC_Q), zb(ZC_K), zb(ZC_V), zb(ZC_R), hb_, hb_,
                  pl.BlockSpec((HEADS, cpb, DV, DKP), lambda i: (0, nb - 1 - i, 0, 0)),
                  pl.BlockSpec((tb, W), lambda i: (nb - 1 - i, 1)),
                  pl.BlockSpec((HEADS, DV), lambda i: (0, 0))],
        out_specs=[pl.BlockSpec((tb, Z_GLA), lambda i: (nb - 1 - i, 0)), hb_, pl.BlockSpec((HEADS, DV), lambda i: (0, 0))],
        out_shape=[jax.ShapeDtypeStruct((S, Z_GLA), BF16), jax.ShapeDtypeStruct((S, HEADS * DKP), F32), jax.ShapeDtypeStruct((HEADS, DV), F32)],
        scratch_shapes=[pltpu.VMEM((HEADS, DV, DKP), F32)],
        compiler_params=_cp(("arbitrary",)), name=name,
    )(z, z, z, z, la, o_raw, sprev, dycat, gn)


def _mod_proj(c_all, w3, layer, b, name):
    B, D = c_all.shape
    N = w3.shape[2]
    tn = _div(N, 1024, LANE)

    def body(c_ref, w_ref, b_ref, o_ref):
        cv = c_ref[...]
        o_ref[...] = _dot(cv * _sigmoid(cv), w_ref[...]) + b_ref[...]

    return pl.pallas_call(
        body, grid=(N // tn,),
        in_specs=[pl.BlockSpec((B, D), lambda j: (0, 0)), pl.BlockSpec((None, D, tn), lambda j: (layer, 0, j)), pl.BlockSpec((1, tn), lambda j: (0, j))],
        out_specs=pl.BlockSpec((B, tn), lambda j: (0, j)), out_shape=jax.ShapeDtypeStruct((B, N), F32),
        compiler_params=_cp(("parallel",)), name=name,
    )(c_all, w3, b)


def _mod_wgrad(c_t, dm, name):
    D, B = c_t.shape
    N = dm.shape[1]
    tn = _div(N, 1024, LANE)

    def body(c_ref, d_ref, o_ref):
        cv = c_ref[...]
        ca = cv * _sigmoid(cv)
        acc = ca[:, 0:1] * d_ref[pl.ds(0, 1), :]
        for b in range(1, B):
            acc = acc + ca[:, b:b + 1] * d_ref[pl.ds(b, 1), :]
        o_ref[...] = acc

    return pl.pallas_call(
        body, grid=(N // tn,),
        in_specs=[pl.BlockSpec((D, B), lambda j: (0, 0)), pl.BlockSpec((B, tn), lambda j: (0, j))],
        out_specs=pl.BlockSpec((D, tn), lambda j: (0, j)), out_shape=jax.ShapeDtypeStruct((D, N), F32),
        compiler_params=_cp(("parallel",)), name=name,
    )(c_t, dm)


def _rowsum(xs, name):
    n, N = xs.shape
    tn = _div(N, 8192, LANE)

    def body(x_ref, o_ref):
        acc = x_ref[pl.ds(0, 1), :]
        for r in range(1, n):
            acc = acc + x_ref[pl.ds(r, 1), :]
        o_ref[...] = acc

    return pl.pallas_call(
        body, grid=(N // tn,), in_specs=[pl.BlockSpec((n, tn), lambda j: (0, j))],
        out_specs=pl.BlockSpec((1, tn), lambda j: (0, j)), out_shape=jax.ShapeDtypeStruct((1, N), F32),
        compiler_params=_cp(("parallel",)), name=name,
    )(xs)


def _adamw(w, g, m, v, name, copy_grad=False):
    R, C = w.shape
    tr = _div(R, max(8, (1 << 18) // C), 8)

    def body(w_ref, g_ref, m_ref, v_ref, d_ref, nm_ref, nv_ref, *g_out):
        gv = g_ref[...]
        if copy_grad:
            g_out[0][...] = gv
        mn = ADAM_B1 * m_ref[...] + (1.0 - ADAM_B1) * gv
        vn = ADAM_B2 * v_ref[...] + (1.0 - ADAM_B2) * (gv * gv)
        m_hat = mn / (1.0 - ADAM_B1 ** ADAM_STEP)
        v_hat = vn / (1.0 - ADAM_B2 ** ADAM_STEP)
        d_ref[...] = -ADAM_LR * (m_hat / (jnp.sqrt(v_hat) + ADAM_EPS) + ADAM_WD * w_ref[...])
        nm_ref[...] = mn
        nv_ref[...] = vn

    blk = pl.BlockSpec((tr, C), lambda i: (i, 0))
    os_ = jax.ShapeDtypeStruct((R, C), F32)
    n_out = 4 if copy_grad else 3
    return pl.pallas_call(
        body, grid=(R // tr,), in_specs=[blk] * 4, out_specs=[blk] * n_out, out_shape=[os_] * n_out,
        compiler_params=_cp(("parallel",)), name=name,
    )(w, g, m, v)


def _place():
    return lax.axis_index("x"), lax.axis_index("y"), lax.axis_index("c")


def _other_chips(x, y):
    return [(1 - x, y), (x, 1 - y), (1 - x, 1 - y)]


def _half(c, rows):
    return pl.ds(c * (rows // 2), rows // 2)


_ANY = pl.BlockSpec(memory_space=pl.ANY)


def _ag_small(v, name):
    r, n = v.shape

    def body(v_ref, o_ref, send_sems, recv_sems):
        x, y, c = _place()
        me = 4 * x + 2 * y + c
        o_ref[pl.ds(me, 1)] = v_ref[...][None]
        peers = [(x ^ (k >> 2), y ^ ((k >> 1) & 1), c ^ (k & 1)) for k in range(1, 8)]
        copies = []
        for k, peer in enumerate(peers):
            cp = pltpu.make_async_remote_copy(
                src_ref=v_ref, dst_ref=o_ref.at[me], send_sem=send_sems.at[k], recv_sem=recv_sems.at[k],
                device_id=peer, device_id_type=MESH)
            cp.start()
            copies.append(cp)
        for cp in copies:
            cp.wait()

    return pl.pallas_call(
        body, out_shape=jax.ShapeDtypeStruct((8, r, n), v.dtype),
        in_specs=[pl.BlockSpec(memory_space=pltpu.VMEM)], out_specs=pl.BlockSpec(memory_space=pltpu.VMEM),
        scratch_shapes=[pltpu.SemaphoreType.DMA((7,)), pltpu.SemaphoreType.DMA((7,))],
        compiler_params=pltpu.CompilerParams(vmem_limit_bytes=VMEM_LIMIT), name=name,
    )(v)


def _rs_sibling_copies(bufs, send_sems, recv_sems):
    n = len(bufs) // 2
    x, y, c = _place()
    return [pltpu.make_async_remote_copy(
        src_ref=bufs[i].at[:, _half(1 - c, bufs[i].shape[1])], dst_ref=bufs[n + i], send_sem=send_sems.at[i], recv_sem=recv_sems.at[i],
        device_id=(x, y, 1 - c), device_id_type=MESH) for i in range(n)]


def _rs_sibling(gs, name, after=None):
    n = len(gs)

    def body(*refs):
        copies = _rs_sibling_copies(refs[:n] + refs[n + 1:2 * n + 1], refs[2 * n + 1], refs[2 * n + 2])
        for cp in copies:
            cp.start()
        for cp in copies:
            cp.wait()

    return pl.pallas_call(
        body, out_shape=[jax.ShapeDtypeStruct((N_CHIPS, g.shape[1] // 2, g.shape[2]), g.dtype) for g in gs],
        in_specs=[_ANY] * (n + 1), out_specs=[_ANY] * n,
        scratch_shapes=[pltpu.SemaphoreType.DMA((n,)), pltpu.SemaphoreType.DMA((n,))],
        compiler_params=pltpu.CompilerParams(has_side_effects=True), name=name,
    )(*gs, gs[0] if after is None else after)


def _rs_presum(g, sib, c_arr, name):
    ns, R, C = g.shape
    rh = R // 2
    tr = _div(rh, max(16, (1 << 19) // C), 16)
    nrb = rh // tr

    def body(c_ref, g_ref, s_ref, o_ref):
        o_ref[...] = (g_ref[...] + s_ref[...]).astype(BF16)

    return pl.pallas_call(
        body, out_shape=jax.ShapeDtypeStruct((ns, rh, C), BF16),
        grid_spec=pltpu.PrefetchScalarGridSpec(
            num_scalar_prefetch=1, grid=(ns, nrb),
            in_specs=[pl.BlockSpec((None, tr, C), lambda s, r, c_ref: (s, c_ref[0] * nrb + r, 0)),
                      pl.BlockSpec((None, tr, C), lambda s, r, c_ref: (s, r, 0))],
            out_specs=pl.BlockSpec((None, tr, C), lambda s, r, c_ref: (s, r, 0))),
        compiler_params=_cp(("parallel", "parallel")), name=name,
    )(c_arr, g, sib)


def _rs_sum(g, sib, recv, full, layer, sc_arr, name):
    ns, R, C = g.shape
    rh = R // 2
    tr = _div(rh, max(16, (1 << 18) // C), 16)
    nrb = rh // tr

    def body(sc_ref, g_ref, s_ref, r_ref, f_ref, o_ref):
        acc = g_ref[...] + s_ref[...]
        for j in range(3):
            acc = acc + r_ref[j].astype(F32)
        o_ref[...] = acc

    return pl.pallas_call(
        body, out_shape=jax.ShapeDtypeStruct(full.shape, F32),
        grid_spec=pltpu.PrefetchScalarGridSpec(
            num_scalar_prefetch=1, grid=(nrb,),
            in_specs=[pl.BlockSpec((None, tr, C), lambda r, sc: (sc[0], sc[1] * nrb + r, 0)),
                      pl.BlockSpec((None, tr, C), lambda r, sc: (sc[0], r, 0)),
                      pl.BlockSpec((3, tr, C), lambda r, sc: (0, r, 0)),
                      _ANY],
            out_specs=pl.BlockSpec((None, tr, C), lambda r, sc: (layer, sc[1] * nrb + r, 0))),
        input_output_aliases={4: 0},
        compiler_params=_cp(("parallel",)), name=name,
    )(sc_arr, g, sib, recv, full)


def _rs_share(fulls, layer, name):
    n = len(fulls)

    def body(*refs):
        src, out = refs[:n], refs[n:2 * n]
        send_sems, recv_sems = refs[2 * n:]
        x, y, c = _place()
        copies = []
        for i in range(n):
            rows = out[i].shape[1]
            cp = pltpu.make_async_remote_copy(
                src_ref=out[i].at[layer, _half(c, rows)], dst_ref=out[i].at[layer, _half(c, rows)],
                send_sem=send_sems.at[i], recv_sem=recv_sems.at[i], device_id=(x, y, 1 - c), device_id_type=MESH)
            cp.start()
            copies.append(cp)
        for cp in copies:
            cp.wait()

    return pl.pallas_call(
        body, out_shape=[jax.ShapeDtypeStruct(f.shape, f.dtype) for f in fulls],
        in_specs=[_ANY] * n, out_specs=[_ANY] * n, input_output_aliases={i: i for i in range(n)},
        scratch_shapes=[pltpu.SemaphoreType.DMA((n,)), pltpu.SemaphoreType.DMA((n,))],
        compiler_params=pltpu.CompilerParams(has_side_effects=True), name=name,
    )(*fulls)


_HBM = pl.BlockSpec(memory_space=pltpu.HBM)
_SEM = pl.BlockSpec(memory_space=pltpu.SEMAPHORE)
_EFFECT = pltpu.SideEffectType.DATAFLOW_SIDE_EFFECTING


def _in_hbm(a):
    return pltpu.with_memory_space_constraint(a, pltpu.HBM)


def _split_start(bufs, n_sem, copies_of, name):
    nb = len(bufs)

    def body(*refs):
        for cp in copies_of(refs[:nb], refs[nb], refs[nb + 1]):
            cp.start()
        refs[-1][...] = jnp.zeros_like(refs[-1])

    out = pl.pallas_call(
        body, name=name,
        out_shape=(pltpu.SemaphoreType.DMA((n_sem,)), pltpu.SemaphoreType.DMA((n_sem,)), *[pltpu.HBM(a.shape, a.dtype) for a in bufs],
                   jax.ShapeDtypeStruct((SUB, LANE), F32)),
        in_specs=[_HBM] * nb, out_specs=(_SEM, _SEM, *([_HBM] * nb), pl.BlockSpec(memory_space=pltpu.VMEM)),
        input_output_aliases={i: 2 + i for i in range(nb)},
        compiler_params=pltpu.CompilerParams(has_side_effects=_EFFECT),
    )(*[_in_hbm(a) for a in bufs])
    return out[0], out[1], list(out[2:2 + nb]), out[-1]


def _split_wait(send_sems, recv_sems, bufs, after, copies_of, name):
    nb = len(bufs)

    def body(*refs):
        for cp in copies_of(refs[:nb], refs[nb], refs[nb + 1]):
            cp.wait_send()
            cp.wait_recv()

    return list(pl.pallas_call(
        body, name=name, out_shape=[pltpu.HBM(a.shape, a.dtype) for a in bufs],
        in_specs=[_HBM] * nb + [_SEM, _SEM, _ANY], out_specs=[_HBM] * nb,
        input_output_aliases={i: i for i in range(nb)},
        compiler_params=pltpu.CompilerParams(has_side_effects=_EFFECT),
    )(*bufs, send_sems, recv_sems, after))


def _ag_half_copies(land, send_sems, recv_sems, landing_of_mine):
    x, y, c = _place()
    cps = []
    for j, (cx, cy) in enumerate(_other_chips(x, y)):
        for i in range(len(land)):
            rows = _half(c, land[i].shape[1])
            s = 2 * x + y if landing_of_mine else 2 * cx + cy
            cps.append(pltpu.make_async_remote_copy(
                src_ref=land[i].at[2 * x + y, rows], dst_ref=land[i].at[s, rows], send_sem=send_sems.at[3 * i + j], recv_sem=recv_sems.at[3 * i + j],
                device_id=(cx, cy, c), device_id_type=MESH))
    return cps


def _ag_starts(land, send_sems, recv_sems):
    return _ag_half_copies(land, send_sems, recv_sems, True)


def _ag_waits(land, send_sems, recv_sems):
    return _ag_half_copies(land, send_sems, recv_sems, False)


def _ag_finish(lands, name):
    n = len(lands)

    def body(*refs):
        land = refs[n:2 * n]
        send_sems, recv_sems = refs[2 * n:]
        x, y, c = _place()
        sibling = (x, y, 1 - c)

        def copy(k, i, s, h):
            blk = land[i].at[s, _half(h, land[i].shape[1])]
            return pltpu.make_async_remote_copy(
                src_ref=blk, dst_ref=blk, send_sem=send_sems.at[k], recv_sem=recv_sems.at[k], device_id=sibling, device_id_type=MESH)

        chips = _other_chips(x, y)
        passed = [copy(3 * i + j, i, 2 * cx + cy, c) for j, (cx, cy) in enumerate(chips) for i in range(n)]
        for cp in passed:
            cp.start()
        for j, (cx, cy) in enumerate(chips):
            for i in range(n):
                copy(3 * i + j, i, 2 * cx + cy, 1 - c).wait_recv()
        for cp in passed:
            cp.wait_send()

    return pl.pallas_call(
        body, out_shape=[jax.ShapeDtypeStruct(a.shape, a.dtype) for a in lands],
        in_specs=[_ANY] * n, out_specs=[_ANY] * n, input_output_aliases={i: i for i in range(n)},
        scratch_shapes=[pltpu.SemaphoreType.DMA((3 * n,)), pltpu.SemaphoreType.DMA((3 * n,))],
        compiler_params=pltpu.CompilerParams(has_side_effects=True), name=name,
    )(*lands)


def _rs_chip_copies(bufs, send_sems, recv_sems):
    n = len(bufs) // 2
    x, y, c = _place()
    return [pltpu.make_async_remote_copy(
        src_ref=bufs[i].at[2 * cx + cy], dst_ref=bufs[n + i].at[j], send_sem=send_sems.at[3 * i + j], recv_sem=recv_sems.at[3 * i + j],
        device_id=(cx, cy, c), device_id_type=MESH) for j, (cx, cy) in enumerate(_other_chips(x, y)) for i in range(n)]


def _pad_heads(w):
    lead = w.shape[:-1]
    w4 = w.reshape(*lead, HEADS, DK)
    w4 = jnp.pad(w4, [(0, 0)] * len(lead) + [(0, 0), (0, DKP - DK)])
    return w4.reshape(*lead, HEADS * DKP)


def _unpad_heads(w):
    lead = w.shape[:-1]
    return w.reshape(*lead, HEADS, DKP)[..., :DK].reshape(*lead, HEADS * DK)


def _mix_weight(win4, n_cols):
    D = win4.shape[1]
    w = jnp.transpose(win4[:, :, :n_cols], (1, 0, 2)).reshape(D, N_CHIPS * n_cols)
    o = 2 * D_CONV
    hk = HEADS * DK
    ab = w[:, :o]
    q = _pad_heads(w[:, o:o + hk])
    k = _pad_heads(w[:, o + hk:o + 2 * hk])
    vr = w[:, o + 2 * hk:o + 2 * hk + 2 * D_GLA]
    glr = jnp.pad(w[:, o + 2 * hk + 2 * D_GLA:], ((0, 0), (0, LANE - GATE_RANK)))
    return jnp.concatenate([q, k, vr, ab, glr], axis=1)


def _mix_weight_grad(dgla, dab, dglr, n_cols, n_pad):
    D = dab.shape[0]
    hkp = HEADS * DKP
    w = jnp.concatenate([dab, _unpad_heads(dgla[:, :hkp]), _unpad_heads(dgla[:, hkp:2 * hkp]), dgla[:, 2 * hkp:], dglr[:, :GATE_RANK]], axis=1)
    w = jnp.pad(w.reshape(D, N_CHIPS, n_cols), ((0, 0), (0, 0), (0, n_pad - n_cols)))
    return jnp.transpose(w, (1, 0, 2))


_ARG_NAMES = ['x', 'c', 'w_ada', 'b_ada', 'g_norm_ffn1', 'w_ffn1_in', 'w_ffn1_out', 'g_norm_mix', 'w_in', 'w_dw', 'b_dw', 'g_conv_ln', 'b_conv_ln', 'w_gate_up', 'b_gate', 'g_gla_norm', 'w_out', 'g_norm_ffn2', 'w_ffn2_in', 'w_ffn2_out', 'g_norm_final', 'w_ada_final', 'b_ada_final']
_WEIGHTS = _ARG_NAMES[2:]
_BIG = ('w_ffn1_in', 'w_ffn1_out', 'w_in', 'w_out', 'w_ffn2_in', 'w_ffn2_out')
_SMALL = ('g_norm_ffn1', 'g_norm_mix', 'w_dw', 'b_dw', 'g_conv_ln', 'b_conv_ln', 'w_gate_up', 'b_gate', 'g_gla_norm', 'g_norm_ffn2', 'g_norm_final')


def _ffn_fwd(x, h, gv, w4, wo, nxt, tag):
    if isinstance(h, tuple):
        z, act, h = _ffn_up(x, w4, f"ffn_up_{tag}", norm=h)
    else:
        z, act = _ffn_up(h, w4, f"ffn_up_{tag}")
    y, xn, *hn = _ffn_down(act, wo, x, gv, nxt, f"ffn_down_{tag}")
    return xn, (hn[0] if hn else None), y, (x, h, z, act)


def _ffn_bwd(dxn, dy, saved, g, scale, prev, w4, wo, tag):
    x, h, z, act = saved
    ns = w4.shape[0]
    dwo = _mm_tn(act, dy, f"dw_out_{tag}")
    dz, dx, dsh, dsc, dg, *pv = _ffn_bwd_core(dy, wo, z, w4, x, dxn, g, scale, prev, f"ffn_bwd_{tag}")
    dwi = _dw_ffn_in(h, dz, ns, f"dw_in_{tag}")
    return dx, pv, dict(dshift=dsh, dscale=dsc, dg=dg, dw_in=dwi, dw_out=dwo.reshape(N_CHIPS, -1, dwo.shape[1]))


def _mix_fwd(x, h, gv, wmix, w_dw, b_dw, g_ln, b_ln, wgp, bgp, gn, wout, nxt, tag):
    z = _mm([(h, wmix, 0)], F32, f"mix_in_{tag}")
    u, yc, yconv = _conv_fwd(z, w_dw, b_dw, g_ln, b_ln, f"conv_fwd_{tag}")
    la = _loggate(z, wgp, bgp, f"loggate_{tag}")
    o_raw, sprev, ygla = _gla_fwd(z, la, gn, f"gla_fwd_{tag}")
    y, xn, *hn = _mix_out(yconv, ygla, wout, x, gv, nxt, f"mix_out_{tag}")
    return xn, (hn[0] if hn else None), y, (x, h, z, u, yc, la, o_raw, sprev, yconv, ygla)


def _mix_bwd(dxn, dy, saved, g, scale, prev, wmix, w_dw, g_ln, b_ln, wgp, bgp, gn, wout, n_cols, n_pad, tag):
    x, h, z, u, yc, la, o_raw, sprev, yconv, ygla = saved
    dycat = _mm([(dy, wout, 0)], F32, f"mix_dycat_{tag}", nt=True)
    dwout = _mm_tn_two(yconv, ygla, dy, f"dw_mixout_{tag}")
    dab, dwdw, dbdw, dgln, dbln = _conv_bwd(dycat, z, u, yc, w_dw, g_ln, b_ln, f"conv_bwd_{tag}")
    dgla, dla, dgn = _gla_bwd(dycat, z, la, o_raw, sprev, gn, f"gla_bwd_{tag}")
    dglr, dwgp, dbgp = _loggate_bwd(dla, z, wgp, wgp.T, bgp, f"loggate_bwd_{tag}")
    dx, dsh, dsc, dg, *pv = _dh_normmod_bwd(
        [(dgla, wmix, 0, 0, Z_GLA), (dab, wmix, 0, ZC_A // (2 * D_CONV), 2 * D_CONV), (dglr, wmix, 0, ZC_G // LANE, LANE)],
        x, dxn, g, scale, prev, f"mix_dh_{tag}")
    dwin = _mix_weight_grad(_mm_tn(h, dgla, f"dw_mixin_gla_{tag}"), _mm_tn(h, dab, f"dw_mixin_conv_{tag}"), _mm_tn(h, dglr, f"dw_mixin_gate_{tag}"),
                            n_cols, n_pad)
    grads = dict(dshift=dsh, dscale=dsc, dg=dg, dw_in=dwin, dw_out=dwout.reshape(N_CHIPS, -1, dwout.shape[1]), dw_dw=dwdw, db_dw=dbdw,
                 dg_ln=dgln, db_ln=dbln, dw_gate=_unpad_heads(dwgp[:GATE_RANK]), db_gate=_unpad_heads(dbgp)[0], dgn=dgn)
    return dx, pv, grads


def kernel(x, c, w_ada, b_ada, g_norm_ffn1, w_ffn1_in, w_ffn1_out, g_norm_mix, w_in, w_dw, b_dw, g_conv_ln, b_conv_ln, w_gate_up, b_gate, g_gla_norm, w_out, g_norm_ffn2, w_ffn2_in, w_ffn2_out, g_norm_final, w_ada_final, b_ada_final, loss_target, m_w_ada, m_b_ada, m_g_norm_ffn1, m_w_ffn1_in, m_w_ffn1_out, m_g_norm_mix, m_w_in, m_w_dw, m_b_dw, m_g_conv_ln, m_b_conv_ln, m_w_gate_up, m_b_gate, m_g_gla_norm, m_w_out, m_g_norm_ffn2, m_w_ffn2_in, m_w_ffn2_out, m_g_norm_final, m_w_ada_final, m_b_ada_final, v_w_ada, v_b_ada, v_g_norm_ffn1, v_w_ffn1_in, v_w_ffn1_out, v_g_norm_mix, v_w_in, v_w_dw, v_b_dw, v_g_conv_ln, v_b_conv_ln, v_w_gate_up, v_b_gate, v_g_gla_norm, v_w_out, v_g_norm_ffn2, v_w_ffn2_in, v_w_ffn2_out, v_g_norm_final, v_w_ada_final, v_b_ada_final):
    given = dict(locals())
    W = {n: given[n] for n in _WEIGHTS}
    M1 = {n: given["m_" + n] for n in _WEIGHTS}
    M2 = {n: given["v_" + n] for n in _WEIGHTS}
    xs = x[0]
    tgt = loss_target[0]
    S, D = xs.shape
    L = w_ada.shape[0]
    xi, yi, ci = _place()
    s_me = 2 * xi + yi
    b_me = 4 * xi + 2 * yi + ci
    nsh = w_ada.shape[2]
    nfin = w_ada_final.shape[1]
    n_cols = w_in.shape[2]
    n_pad = -(-n_cols // LANE) * LANE

    def lands_of(l):
        shards = [W[n][l].astype(BF16) for n in _BIG]
        shards[2] = jnp.pad(shards[2], ((0, 0), (0, n_pad - n_cols)))
        return [lax.dynamic_update_index_in_dim(lax.empty((N_CHIPS,) + s.shape, BF16), s, s_me, 0) for s in shards]

    lands = {l: lands_of(l) for l in range(L)}
    ag_groups = [dict(l=0, items=[0, 1], need=0), dict(l=0, items=[2, 3, 4, 5], need=1)]
    ag_groups += [dict(l=l, items=list(range(len(_BIG))), need=3 * l) for l in range(1, L)]

    def ag_start(grp):
        bufs = [lands[grp["l"]][i] for i in grp["items"]]
        return _split_start(bufs, 3 * len(bufs), _ag_starts, f"ag_start_l{grp['l']}_{grp['items'][0]}")

    pend = ag_start(ag_groups[0])
    tok = pend[3][0, 0]

    c_all = _ag_small(c.reshape(8, D // 8) + tok, "ag_c").reshape(8, D)
    tok = None
    parts = [_mod_proj(c_all, w_ada, l, lax.dynamic_slice(b_ada, (l, s_me * nsh), (1, nsh)), f"mod_proj_{l}") for l in range(L)]
    parts.append(_mod_proj(c_all, w_ada_final[None], 0, lax.dynamic_slice(b_ada_final, (s_me * nfin,), (nfin,))[None], "mod_proj_final"))
    mod_all = _ag_small(jnp.concatenate(parts, axis=1), "ag_mod")
    mine = [lax.dynamic_index_in_dim(lax.dynamic_index_in_dim(mod_all, 2 * s + ci, 0, False), b_me, 0, False) for s in range(N_CHIPS)]
    mods = [jnp.concatenate([mine[s][l * nsh:(l + 1) * nsh] for s in range(N_CHIPS)]).reshape(N_MOD, 1, D) for l in range(L)]
    fmod = jnp.concatenate([mine[s][L * nsh:] for s in range(N_CHIPS)]).reshape(2, 1, D)

    tiny = jnp.concatenate([w_dw.reshape(-1), w_gate_up.reshape(-1)])
    tiny_all = _ag_small(jnp.pad(tiny, (0, (-tiny.shape[0]) % (8 * LANE))).reshape(8, -1), "ag_tiny").reshape(8, -1)
    n_dw = w_dw.size
    dw_parts = [lax.dynamic_index_in_dim(tiny_all, 2 * s + ci, 0, False) for s in range(N_CHIPS)]
    w_dw_full = jnp.concatenate([p[:n_dw].reshape(w_dw.shape) for p in dw_parts], axis=2)
    w_gu_full = jnp.concatenate([p[n_dw:n_dw + w_gate_up.size].reshape(w_gate_up.shape) for p in dw_parts], axis=2)

    def layer_weights(l, lands):
        wi1, wo1, win4, wout4, wi2, wo2 = lands
        return dict(
            wi1=wi1, wo1=wo1.reshape(-1, D), wi2=wi2, wo2=wo2.reshape(-1, D), wout=wout4.reshape(-1, D), wmix=_mix_weight(win4, n_cols),
            wgp=jnp.pad(_pad_heads(w_gu_full[l]), ((0, LANE - GATE_RANK), (0, 0))).astype(BF16), bgp=_pad_heads(b_gate[l])[None])

    gnorm = (g_norm_ffn1, g_norm_mix, g_norm_ffn2)
    subs = [dict(l=l, j=j, tag=f"{('ffn1', 'mix', 'ffn2')[j]}_l{l}", g=gnorm[j][l][None], shift=mods[l][3 * j], scale=mods[l][3 * j + 1],
                 gv=mods[l][3 * j + 2] * (1.0 if j == 1 else 0.5)) for l in range(L) for j in range(3)]
    gi = 0
    xcur = xs
    h = None
    for k, sb in enumerate(subs):
        l, j = sb["l"], sb["j"]
        if pend is not None and ag_groups[gi]["need"] == k:
            grp = ag_groups[gi]
            nm = f"l{grp['l']}_{grp['items'][0]}"
            after = xcur if k > 0 else sb["shift"]
            done = _ag_finish(_split_wait(pend[0], pend[1], pend[2], after, _ag_waits, f"ag_wait_{nm}"), f"ag_finish_{nm}")
            for i, a in zip(grp["items"], done):
                lands[grp["l"]][i] = a
            gi += 1
            pend = ag_start(ag_groups[gi]) if gi < len(ag_groups) else None
            tok = pend[3][0, 0] if pend is not None else None
        if h is None:
            h = (sb["g"] if tok is None else sb["g"] + tok, sb["shift"], sb["scale"])
            tok = None
        d = layer_weights(l, lands[l])
        nxt = (subs[k + 1]["g"], subs[k + 1]["shift"], subs[k + 1]["scale"]) if k + 1 < len(subs) else None
        gv = sb["gv"] if tok is None else sb["gv"] + tok
        tok = None
        if j == 1:
            xcur, h, sb["y"], sb["saved"] = _mix_fwd(xcur, h, gv, d["wmix"], w_dw_full[l], b_dw[l][None], g_conv_ln[l][None],
                                                     b_conv_ln[l][None], d["wgp"], d["bgp"], g_gla_norm[l], d["wout"], nxt, sb["tag"])
        else:
            w4, wo = (d["wi1"], d["wo1"]) if j == 0 else (d["wi2"], d["wo2"])
            xcur, h, sb["y"], sb["saved"] = _ffn_fwd(xcur, h, gv, w4, wo, nxt, sb["tag"])
    lw = [layer_weights(l, lands[l]) for l in range(L)]

    c_arr = jnp.stack([ci]).astype(jnp.int32)
    sc_arr = jnp.stack([s_me, ci]).astype(jnp.int32)
    fulls = [lax.empty((L,) + ((W[n].shape[1], n_pad) if n == 'w_in' else W[n].shape[1:]), F32) for n in _BIG]

    def rs_begin(gs, items, l, after=None):
        nm = f"l{l}_{items[0]}"
        sibs = _rs_sibling(gs, f"rs_sibling_{nm}", after)
        return sibs, [_rs_presum(g, sb_, c_arr, f"rs_presum_{i}_l{l}") for i, g, sb_ in zip(items, gs, sibs)]

    def rs_end(gs, sibs, recvs, items, l):
        summed = [_rs_sum(g, sb_, rv, fulls[i], l, sc_arr, f"rs_sum_{i}_l{l}") for i, g, sb_, rv in zip(items, gs, sibs, recvs)]
        for i, f in zip(items, _rs_share(summed, l, f"rs_share_l{l}_{items[0]}")):
            fulls[i] = f

    def rs_start(gs, items, l, sibs=None, after=None):
        if sibs is None:
            sibs, ps = rs_begin(gs, items, l, after)
        else:
            ps = [_rs_presum(g, sb_, c_arr, f"rs_presum_{i}_l{l}") for i, g, sb_ in zip(items, gs, sibs)]
        pend = _split_start(ps + [lax.empty((3,) + p.shape[1:], BF16) for p in ps], 3 * len(ps), _rs_chip_copies, f"rs_start_l{l}_{items[0]}")
        return dict(gs=gs, sibs=sibs, pend=pend, items=items, l=l)

    def sib_start(gs, items, l):
        lands_ = [lax.empty((N_CHIPS, g.shape[1] // 2, g.shape[2]), F32) for g in gs]
        return dict(pend=_split_start(gs + lands_, len(gs), _rs_sibling_copies, f"rs_sibling_start_l{l}"), items=items, l=l, n=len(gs))

    def sib_finish(sp, after):
        bufs = _split_wait(sp["pend"][0], sp["pend"][1], sp["pend"][2], after, _rs_sibling_copies, f"rs_sibling_wait_l{sp['l']}")
        return rs_start(bufs[:sp["n"]], sp["items"], sp["l"], sibs=bufs[sp["n"]:])

    def rs_finish(fl, after):
        pend, n = fl["pend"], len(fl["gs"])
        bufs = _split_wait(pend[0], pend[1], pend[2], after, _rs_chip_copies, f"rs_wait_l{fl['l']}_{fl['items'][0]}")
        rs_end(fl["gs"], fl["sibs"], bufs[n:], fl["items"], fl["l"])

    sq, dx, dfsh, dfsc, dgfin, dy, dgv = _loss_head(xcur, g_norm_final[None], fmod[0], fmod[1], tgt, (subs[-1]["y"], subs[-1]["gv"]))
    loss_part = 0.5 / D * jnp.sum(sq)
    G = {n: [None] * L for n in _SMALL}
    dmods = [None] * L
    in_flight = None
    sib_flight = None
    tok = None
    for l in reversed(range(L)):
        gr = [None] * 3
        for j in reversed(range(3)):
            k = 3 * l + j
            sb, d = subs[k], lw[l]
            prev = (subs[k - 1]["y"], subs[k - 1]["gv"]) if k > 0 else None
            g_vec = sb["g"] if tok is None else sb["g"] + tok
            tok = None
            if j == 1:
                dx, pv, gr[j] = _mix_bwd(dx, dy, sb["saved"], g_vec, sb["scale"], prev, d["wmix"], w_dw_full[l], g_conv_ln[l][None], b_conv_ln[l][None],
                                         d["wgp"], d["bgp"], g_gla_norm[l], d["wout"], n_cols, n_pad, sb["tag"])
            else:
                w4, wo = (d["wi1"], d["wo1"]) if j == 0 else (d["wi2"], d["wo2"])
                dx, pv, gr[j] = _ffn_bwd(dx, dy, sb["saved"], g_vec, sb["scale"], prev, w4, wo, sb["tag"])
            gr[j]["dgv"] = dgv
            dy, dgv = pv if pv else (None, None)
            if j == 2 and sib_flight is not None:
                in_flight = sib_finish(sib_flight, dx)
                sib_flight = None
                tok = in_flight["pend"][3][0, 0]
            if j == 1 and in_flight is not None:
                rs_finish(in_flight, dx)
                in_flight = None
            if j == 1 and l == 0:
                in_flight = rs_start([gr[1]["dw_in"], gr[1]["dw_out"], gr[2]["dw_in"], gr[2]["dw_out"]], [2, 3, 4, 5], l)
                tok = in_flight["pend"][3][0, 0]
        g1, g2, g3 = gr
        if l > 0:
            sib_flight = sib_start([g1["dw_in"], g1["dw_out"], g2["dw_in"], g2["dw_out"], g3["dw_in"], g3["dw_out"]], list(range(len(_BIG))), l)
            tok = sib_flight["pend"][3][0, 0]
        else:
            rs_finish(in_flight, dx)
            in_flight = None
            last_grads = [g1["dw_in"], g1["dw_out"]]
        dmods[l] = jnp.concatenate([g1["dshift"], g1["dscale"], 0.5 * g1["dgv"], g2["dshift"], g2["dscale"], g2["dgv"],
                                    g3["dshift"], g3["dscale"], 0.5 * g3["dgv"]], axis=1)[0]
        G["g_norm_ffn1"][l], G["g_norm_ffn2"][l], G["g_norm_mix"][l] = g1["dg"][0], g3["dg"][0], g2["dg"][0]
        G["w_dw"][l], G["b_dw"][l], G["g_conv_ln"][l], G["b_conv_ln"][l] = g2["dw_dw"], g2["db_dw"][0], g2["dg_ln"][0], g2["db_ln"][0]
        G["w_gate_up"][l], G["b_gate"][l], G["g_gla_norm"][l] = g2["dw_gate"], g2["db_gate"], g2["dgn"]
    grad_x = dx[None]
    gsm = {}

    small = [jnp.stack(G[n]).reshape(-1) for n in _SMALL if n != 'g_norm_final'] + [dgfin[0]]
    dmod_vec = jnp.concatenate(dmods + [dfsh[0], dfsc[0]])
    n_mod_vec = dmod_vec.shape[0]
    vec = jnp.concatenate([dmod_vec] + small + [loss_part[None]])
    n_vec = vec.shape[0]
    vec = jnp.pad(vec, (0, (-n_vec) % (8 * LANE)))
    vec_all = _ag_small(vec.reshape(8, -1), "ag_small_grads").reshape(8, -1)
    last = rs_start(last_grads, [0, 1], 0, after=vec_all)
    vec_sum = _rowsum(vec_all + last["pend"][3][0, 0], "sum_small_grads")[0]
    loss = vec_sum[n_vec - 1]
    off = n_mod_vec
    for n in _SMALL:
        shp = {'w_dw': w_dw_full.shape, 'w_gate_up': w_gu_full.shape}.get(n, W[n].shape)
        cnt = 1
        for dd in shp:
            cnt *= dd
        gsm[n] = vec_sum[off:off + cnt].reshape(shp)
        off += cnt
    gsm['w_dw'] = lax.dynamic_slice_in_dim(gsm['w_dw'], s_me * w_dw.shape[2], w_dw.shape[2], 2)
    gsm['w_gate_up'] = lax.dynamic_slice_in_dim(gsm['w_gate_up'], s_me * w_gate_up.shape[2], w_gate_up.shape[2], 2)
    dmod_sum = vec_sum[:n_mod_vec]
    gsm['b_ada'] = dmod_sum[:L * N_MOD * D].reshape(L, N_MOD * D)
    gsm['b_ada_final'] = dmod_sum[L * N_MOD * D:]
    c_t = c_all.T
    dmod_rows = vec_all[:, :n_mod_vec]
    gsm['w_ada'] = jnp.stack([
        _mod_wgrad(c_t, lax.dynamic_slice_in_dim(dmod_rows, l * N_MOD * D + s_me * nsh, nsh, 1), f"dw_ada_{l}") for l in range(L)])
    gsm['w_ada_final'] = _mod_wgrad(c_t, lax.dynamic_slice_in_dim(dmod_rows, L * N_MOD * D + s_me * nfin, nfin, 1), "dw_ada_final")
    gsm.update({n: (f[:, :, :n_cols] if n == 'w_in' else f) for n, f in zip(_BIG[2:], fulls[2:])})

    outs = {}
    small_names = [n for n in _WEIGHTS if W[n].size < 65536]
    for n in [m for m in _WEIGHTS if m not in _BIG[:2]] + list(_BIG[:2]):
        if n in small_names:
            continue
        if n == _BIG[0]:
            rs_finish(last, outs['w_ada_final'][0])
            gsm.update(dict(zip(_BIG[:2], fulls[:2])))
        shp = W[n].shape
        v2 = lambda a: a.reshape(-1, shp[-1])
        from_rs = n in _BIG and n != 'w_in'
        d_, m_, v_, *g_ = _adamw(v2(W[n]), v2(gsm[n]), v2(M1[n]), v2(M2[n]), f"adamw_{n}", copy_grad=from_rs)
        outs[n] = (d_.reshape(shp), m_.reshape(shp), v_.reshape(shp))
        if from_rs:
            gsm[n] = g_[0].reshape(shp)
    flat = lambda dct: jnp.concatenate([dct[n].reshape(-1) for n in small_names])
    n_small = sum(W[n].size for n in small_names)
    v2 = lambda a: jnp.pad(a, (0, (-n_small) % (8 * LANE))).reshape(-1, LANE)
    d_, m_, v_ = _adamw(v2(flat(W)), v2(flat(gsm)), v2(flat(M1)), v2(flat(M2)), "adamw_small")

    def unflat(a):
        res, o = {}, 0
        a = a.reshape(-1)
        for n in small_names:
            res[n] = a[o:o + W[n].size].reshape(W[n].shape)
            o += W[n].size
        return res

    for n, dd, mm, vv in zip(small_names, unflat(d_).values(), unflat(m_).values(), unflat(v_).values()):
        outs[n] = (dd, mm, vv)

    return (loss, grad_x, *[gsm[n] for n in _WEIGHTS], *[outs[n][0] for n in _WEIGHTS], *[outs[n][1] for n in _WEIGHTS], *[outs[n][2] for n in _WEIGHTS])
```

```python
import jax
import jax.numpy as jnp
from jax import lax
from jax.experimental import pallas as pl
from jax.experimental.pallas import tpu as pltpu

F32 = jnp.float32
BF16 = jnp.bfloat16

CHUNK = 64
HEADS = 4
DK = 64
DV = 128
DKP = 128
GATE_RANK = 16
GATE_TAU = 16.0
N_MOD = 9
EPS = 1e-6
ADAM_LR = 0.001
ADAM_B1 = 0.9
ADAM_B2 = 0.999
ADAM_EPS = 1e-08
ADAM_WD = 0.01
ADAM_STEP = 10

LANE = 128
HALO = 32
VMEM_LIMIT = 52 * 1024 * 1024
MESH = pl.DeviceIdType.MESH
N_CHIPS = 4

D_CONV = 512
D_GLA = HEADS * DV
ZC_Q = 0
ZC_K = ZC_Q + HEADS * DKP
ZC_V = ZC_K + HEADS * DKP
ZC_R = ZC_V + D_GLA
ZC_A = ZC_R + D_GLA
ZC_B = ZC_A + D_CONV
ZC_G = ZC_B + D_CONV
Z_COLS = ZC_G + LANE
Z_GLA = ZC_A


def _div(n, target, mult):
    best = None
    d = mult
    while d <= min(n, target):
        if n % d == 0:
            best = d
        d += mult
    return n if best is None else best


def _cp(sem=None, **kw):
    return pltpu.CompilerParams(dimension_semantics=sem, vmem_limit_bytes=VMEM_LIMIT, **kw)


def _resident(shape, index_map):
    return pl.BlockSpec(shape, index_map, pipeline_mode=pl.Buffered(1))


def _sigmoid(x):
    return 0.5 * jnp.tanh(0.5 * x) + 0.5


def _dot(a, b):
    return jnp.dot(a.astype(BF16), b.astype(BF16), preferred_element_type=F32)


def _dot_nt(a, b):
    return lax.dot_general(a.astype(BF16), b.astype(BF16), (((1,), (1,)), ((), ())), preferred_element_type=F32)


def _dot_tn(a, b):
    return lax.dot_general(a.astype(BF16), b.astype(BF16), (((0,), (0,)), ((), ())), preferred_element_type=F32)


def _dot_exact(a, b):
    return jnp.dot(a, b, preferred_element_type=F32, precision=lax.Precision.HIGHEST)


def _norm_rows(xv, g, shift, scale):
    r = lax.rsqrt(jnp.mean(xv * xv, axis=-1, keepdims=True) + EPS)
    return (xv * r) * g * (1.0 + scale) + shift


def _loss_head(x, g, shift, scale, tgt, prev):
    S, D = x.shape
    tm = _div(S, 512, 8)

    def body(x_ref, g_ref, sh_ref, sc_ref, t_ref, y_ref, gvp_ref, sq_ref, dx_ref, dsh_ref, dsc_ref, dg_ref, dy_ref, dgv_ref):
        sums = (sq_ref, dsh_ref, dsc_ref, dg_ref, dgv_ref)

        @pl.when(pl.program_id(0) == 0)
        def _():
            for o in sums:
                o[...] = jnp.zeros_like(o)

        xv = x_ref[...]
        e = _norm_rows(xv, g_ref[...], sh_ref[...], sc_ref[...]) - t_ref[...]
        dx, dsh, dsc, dg, dy, dgv = _normmod_bwd_rows(xv, e * (1.0 / D), None, g_ref[...], sc_ref[...], (y_ref[...], gvp_ref[...]))
        dx_ref[...] = dx
        dy_ref[...] = dy
        for o, v in zip(sums, (jnp.sum(e * e, axis=0, keepdims=True), dsh, dsc, dg, dgv)):
            o[...] += v

    row = pl.BlockSpec((tm, D), lambda i: (i, 0))
    vec = pl.BlockSpec((1, D), lambda i: (0, 0))
    vs = jax.ShapeDtypeStruct((1, D), F32)
    return pl.pallas_call(
        body, grid=(S // tm,), in_specs=[row, vec, vec, vec, row, row, vec], out_specs=[vec, row, vec, vec, vec, row, vec],
        out_shape=[vs, jax.ShapeDtypeStruct((S, D), F32), vs, vs, vs, jax.ShapeDtypeStruct((S, D), BF16), vs],
        compiler_params=_cp(("arbitrary",)), name="loss_head",
    )(x, g, shift, scale, tgt, *prev)


def _normmod_bwd_rows(xv, dh, dres, gv, sc, prev):
    r = lax.rsqrt(jnp.mean(xv * xv, axis=-1, keepdims=True) + EPS)
    xh = xv * r
    dsh = jnp.sum(dh, axis=0, keepdims=True)
    dsc = jnp.sum(dh * (xh * gv), axis=0, keepdims=True)
    dn = dh * (1.0 + sc)
    dg = jnp.sum(dn * xh, axis=0, keepdims=True)
    dxh = dn * gv
    dx = r * (dxh - xh * jnp.mean(dxh * xh, axis=-1, keepdims=True))
    if dres is not None:
        dx = dx + dres
    if prev is None:
        return dx, dsh, dsc, dg
    y, gvp = prev
    return dx, dsh, dsc, dg, (gvp * dx).astype(BF16), jnp.sum(dx * y, axis=0, keepdims=True)


def _mm(pairs, out_dtype, name, nt=False):
    M = pairs[0][0].shape[0]
    N = pairs[0][1].shape[0] if nt else pairs[0][1].shape[1]
    ktot = sum(a.shape[1] for a, _, _ in pairs)
    tm = _div(M, 512 if ktot <= 4096 else 256, 8)
    n = len(pairs)

    def body(*refs):
        o_ref = refs[2 * n]
        dot = _dot_nt if nt else _dot
        acc = dot(refs[0][...], refs[1][...])
        for p in range(1, n):
            acc = acc + dot(refs[2 * p][...], refs[2 * p + 1][...])
        o_ref[...] = acc.astype(o_ref.dtype)

    ins, args = [], []
    for a, b, blk in pairs:
        k = a.shape[1]
        ins.append(pl.BlockSpec((tm, k), lambda i: (i, 0)))
        ins.append(_resident((N, k), lambda i, blk=blk: (0, blk)) if nt else _resident((k, N), lambda i: (0, 0)))
        args += [a, b]
    return pl.pallas_call(
        body, grid=(M // tm,), in_specs=ins, out_specs=pl.BlockSpec((tm, N), lambda i: (i, 0)),
        out_shape=jax.ShapeDtypeStruct((M, N), out_dtype), compiler_params=_cp(("parallel",)), name=name,
    )(*args)


TN_ROWS = 2048


def _mm_tn(a, g, name):
    S, Ka = a.shape
    N = g.shape[1]
    tk = _div(Ka, 1408, LANE)
    tn = _div(N, 1408, LANE)
    ts = _div(S, TN_ROWS, 8)

    def body(a_ref, g_ref, o_ref):
        @pl.when(pl.program_id(2) == 0)
        def _():
            o_ref[...] = jnp.zeros_like(o_ref)

        o_ref[...] += _dot_tn(a_ref[...], g_ref[...])

    return pl.pallas_call(
        body, grid=(Ka // tk, N // tn, S // ts),
        in_specs=[pl.BlockSpec((ts, tk), lambda i, j, s: (s, i)), pl.BlockSpec((ts, tn), lambda i, j, s: (s, j))],
        out_specs=pl.BlockSpec((tk, tn), lambda i, j, s: (i, j)),
        out_shape=jax.ShapeDtypeStruct((Ka, N), F32),
        compiler_params=_cp(("parallel", "parallel", "arbitrary")), name=name,
    )(a, g)


def _mm_tn_two(a0, a1, g, name):
    S, K = a0.shape
    N = g.shape[1]
    ts = _div(S, TN_ROWS, 8)

    def body(a0_ref, a1_ref, g_ref, o_ref):
        i = pl.program_id(0)

        @pl.when(pl.program_id(1) == 0)
        def _():
            o_ref[...] = jnp.zeros_like(o_ref)

        @pl.when(i == 0)
        def _():
            o_ref[...] += _dot_tn(a0_ref[...], g_ref[...])

        @pl.when(i == 1)
        def _():
            o_ref[...] += _dot_tn(a1_ref[...], g_ref[...])

    return pl.pallas_call(
        body, grid=(2, S // ts),
        in_specs=[pl.BlockSpec((ts, K), lambda i, s: (jnp.where(i == 0, s, 0), 0)),
                  pl.BlockSpec((ts, K), lambda i, s: (jnp.where(i == 1, s, 0), 0)),
                  pl.BlockSpec((ts, N), lambda i, s: (s, 0))],
        out_specs=pl.BlockSpec((K, N), lambda i, s: (i, 0)),
        out_shape=jax.ShapeDtypeStruct((2 * K, N), F32),
        compiler_params=_cp(("parallel", "arbitrary")), name=name,
    )(a0, a1, g)


def _swiglu(gt, up):
    return gt * _sigmoid(gt) * up


def _ffn_up(h, w4, name, norm=None):
    S, D = h.shape
    ns, _, C = w4.shape
    hs = ns // 2
    tm = _div(S, 256, 8)
    nn = 3 if norm else 0

    def body(h_ref, w_ref, *rest):
        z_ref, a_ref = rest[nn:nn + 2]
        if norm:
            hv = _norm_rows(h_ref[...], rest[0][...], rest[1][...], rest[2][...]).astype(BF16)
            rest[nn + 2][...] = hv
        else:
            hv = h_ref[...]
        for s in range(hs):
            gt = _dot(hv, w_ref[s])
            up = _dot(hv, w_ref[hs + s])
            sg = _sigmoid(gt)
            silu = gt * sg
            z_ref[:, s * C:(s + 1) * C] = (up * (sg * (1.0 + gt * (1.0 - sg)))).astype(BF16)
            z_ref[:, (hs + s) * C:(hs + s + 1) * C] = silu.astype(BF16)
            a_ref[:, s * C:(s + 1) * C] = (silu * up).astype(BF16)

    row = pl.BlockSpec((tm, D), lambda i: (i, 0))
    vec = pl.BlockSpec((1, D), lambda i: (0, 0))
    return pl.pallas_call(
        body, grid=(S // tm,), in_specs=[row, _resident((ns, D, C), lambda i: (0, 0, 0))] + [vec] * nn,
        out_specs=[pl.BlockSpec((tm, ns * C), lambda i: (i, 0)), pl.BlockSpec((tm, hs * C), lambda i: (i, 0))] + [row] * (nn // 3),
        out_shape=[jax.ShapeDtypeStruct((S, ns * C), BF16), jax.ShapeDtypeStruct((S, hs * C), BF16)] + [jax.ShapeDtypeStruct((S, D), BF16)] * (nn // 3),
        compiler_params=_cp(("parallel",)), name=name,
    )(h, w4, *(norm or ()))


def _resid_outputs(y, x_ref, gv_ref, nxt_refs, out_refs):
    out_refs[0][...] = y
    xn = x_ref[...] + gv_ref[...] * y
    out_refs[1][...] = xn
    if nxt_refs:
        out_refs[2][...] = _norm_rows(xn, nxt_refs[0][...], nxt_refs[1][...], nxt_refs[2][...]).astype(BF16)


def _ffn_down(act, wo, x, gv, nxt, name):
    S = act.shape[0]
    Fd, D = wo.shape
    tm = _div(S, 512, 8)
    nn = 3 if nxt else 0

    def body(a_ref, w_ref, x_ref, gv_ref, *rest):
        _resid_outputs(_dot(a_ref[...], w_ref[...]), x_ref, gv_ref, rest[:nn], rest[nn:])

    row = pl.BlockSpec((tm, D), lambda i: (i, 0))
    vec = pl.BlockSpec((1, D), lambda i: (0, 0))
    os_ = jax.ShapeDtypeStruct((S, D), F32)
    return pl.pallas_call(
        body, grid=(S // tm,),
        in_specs=[pl.BlockSpec((tm, Fd), lambda i: (i, 0)), _resident((Fd, D), lambda i: (0, 0)), row, vec] + [vec] * nn,
        out_specs=[row, row] + [row] * (nn // 3), out_shape=[os_, os_] + [jax.ShapeDtypeStruct((S, D), BF16)] * (nn // 3),
        compiler_params=_cp(("parallel",)), name=name,
    )(act, wo, x, gv, *(nxt or ()))


def _dw_ffn_in(h, dz, ns, name):
    S, D = h.shape
    C = dz.shape[1] // ns
    ts = _div(S, TN_ROWS, 8)

    def body(h_ref, g_ref, o_ref):
        @pl.when(pl.program_id(1) == 0)
        def _():
            o_ref[...] = jnp.zeros_like(o_ref)

        o_ref[...] += _dot_tn(h_ref[...], g_ref[...])

    return pl.pallas_call(
        body, grid=(ns, S // ts),
        in_specs=[pl.BlockSpec((ts, D), lambda j, s: (s, 0)), pl.BlockSpec((ts, C), lambda j, s: (s, j))],
        out_specs=pl.BlockSpec((None, D, C), lambda j, s: (j, 0, 0)), out_shape=jax.ShapeDtypeStruct((ns, D, C), F32),
        compiler_params=_cp(("parallel", "arbitrary")), name=name,
    )(h, dz)


def _dh_normmod_bwd(pairs, x, dres, g, scale, prev, name):
    S, D = x.shape
    tm = _div(S, 256, 8)
    n = len(pairs)
    with_prev = prev is not None

    def body(*refs):
        refs = list(refs)
        mm = refs[:2 * n]
        x_ref, dr_ref, g_ref, sc_ref = refs[2 * n:2 * n + 4]
        outs = refs[2 * n + 4 + 2 * with_prev:]

        @pl.when(pl.program_id(0) == 0)
        def _():
            for o in outs[1:4] + outs[5:]:
                o[...] = jnp.zeros_like(o)

        dh = _dot_nt(mm[0][...], mm[1][...])
        for p in range(1, n):
            dh = dh + _dot_nt(mm[2 * p][...], mm[2 * p + 1][...])
        pv = (refs[2 * n + 4][...], refs[2 * n + 5][...]) if with_prev else None
        res = _normmod_bwd_rows(x_ref[...], dh, dr_ref[...], g_ref[...], sc_ref[...], pv)
        outs[0][...] = res[0]
        for o, v in zip(outs[1:4], res[1:4]):
            o[...] += v
        if with_prev:
            outs[4][...] = res[4]
            outs[5][...] += res[5]

    row = pl.BlockSpec((tm, D), lambda i: (i, 0))
    vec = pl.BlockSpec((1, D), lambda i: (0, 0))
    ins, args = [], []
    for a, b, a_blk, b_blk, k in pairs:
        ins.append(pl.BlockSpec((tm, k), lambda i, a_blk=a_blk: (i, a_blk)))
        ins.append(_resident((None, D, k), lambda i, b_blk=b_blk: (b_blk, 0, 0)) if b.ndim == 3 else _resident((D, k), lambda i, b_blk=b_blk: (0, b_blk)))
        args += [a, b]
    ins += [row, row, vec, vec] + [row, vec] * with_prev
    args += [x, dres, g, scale] + (list(prev) if with_prev else [])
    vs = jax.ShapeDtypeStruct((1, D), F32)
    return pl.pallas_call(
        body, grid=(S // tm,), in_specs=ins, out_specs=[row, vec, vec, vec] + [row, vec] * with_prev,
        out_shape=[jax.ShapeDtypeStruct((S, D), F32), vs, vs, vs] + [jax.ShapeDtypeStruct((S, D), BF16), vs] * with_prev,
        compiler_params=_cp(("arbitrary",)), name=name,
    )(*args)


def _ffn_bwd_core(dy, wo, z, w4, x, dres, g, scale, prev, name):
    S, D = x.shape
    Fd = wo.shape[0]
    ns, _, C = w4.shape
    tm = _div(S, 256, 8)
    with_prev = prev is not None

    def body(dy_ref, wo_ref, q_ref, p_ref, w4_ref, x_ref, dr_ref, g_ref, sc_ref, *rest):
        outs = rest[2 * with_prev:]
        dz_ref, outs = outs[0], outs[1:]

        @pl.when(pl.program_id(0) == 0)
        def _():
            for o in outs[1:4] + outs[5:]:
                o[...] = jnp.zeros_like(o)

        da = _dot_nt(dy_ref[...], wo_ref[...])
        dz_ref[:, :Fd] = (da * q_ref[...].astype(F32)).astype(BF16)
        dz_ref[:, Fd:] = (da * p_ref[...].astype(F32)).astype(BF16)
        dh = _dot_nt(dz_ref[:, 0:C], w4_ref[0])
        for s in range(1, ns):
            dh = dh + _dot_nt(dz_ref[:, s * C:(s + 1) * C], w4_ref[s])
        pv = (rest[0][...], rest[1][...]) if with_prev else None
        res = _normmod_bwd_rows(x_ref[...], dh, dr_ref[...], g_ref[...], sc_ref[...], pv)
        outs[0][...] = res[0]
        for o, v in zip(outs[1:4], res[1:4]):
            o[...] += v
        if with_prev:
            outs[4][...] = res[4]
            outs[5][...] += res[5]

    row = pl.BlockSpec((tm, D), lambda i: (i, 0))
    vec = pl.BlockSpec((1, D), lambda i: (0, 0))
    wide = pl.BlockSpec((tm, 2 * Fd), lambda i: (i, 0))
    vs = jax.ShapeDtypeStruct((1, D), F32)
    return pl.pallas_call(
        body, grid=(S // tm,),
        in_specs=[row, _resident((Fd, D), lambda i: (0, 0)), pl.BlockSpec((tm, Fd), lambda i: (i, 0)), pl.BlockSpec((tm, Fd), lambda i: (i, 1)),
                  _resident((ns, D, C), lambda i: (0, 0, 0)), row, row, vec, vec] + [row, vec] * with_prev,
        out_specs=[wide, row, vec, vec, vec] + [row, vec] * with_prev,
        out_shape=[jax.ShapeDtypeStruct((S, 2 * Fd), BF16), jax.ShapeDtypeStruct((S, D), F32), vs, vs, vs] + [jax.ShapeDtypeStruct((S, D), BF16), vs] * with_prev,
        compiler_params=_cp(("arbitrary",)), name=name,
    )(dy, wo, z, z, w4, x, dres, g, scale, *(prev or ()))


def _mix_out(yconv, ygla, wout, x, gv, nxt, name):
    S, Kc = yconv.shape
    Kg = ygla.shape[1]
    D = wout.shape[1]
    tm = _div(S, 512, 8)
    nn = 3 if nxt else 0

    def body(a_ref, b_ref, w_ref, x_ref, gv_ref, *rest):
        y = _dot(a_ref[...], w_ref[0:Kc, :]) + _dot(b_ref[...], w_ref[Kc:Kc + Kg, :])
        _resid_outputs(y, x_ref, gv_ref, rest[:nn], rest[nn:])

    row = pl.BlockSpec((tm, D), lambda i: (i, 0))
    vec = pl.BlockSpec((1, D), lambda i: (0, 0))
    os_ = jax.ShapeDtypeStruct((S, D), F32)
    return pl.pallas_call(
        body, grid=(S // tm,),
        in_specs=[pl.BlockSpec((tm, Kc), lambda i: (i, 0)), pl.BlockSpec((tm, Kg), lambda i: (i, 0)), _resident((Kc + Kg, D), lambda i: (0, 0)), row,
                  vec] + [vec] * nn,
        out_specs=[row, row] + [row] * (nn // 3), out_shape=[os_, os_] + [jax.ShapeDtypeStruct((S, D), BF16)] * (nn // 3),
        compiler_params=_cp(("parallel",)), name=name,
    )(yconv, ygla, wout, x, gv, *(nxt or ()))


def _ln_parts(yc, g, b):
    mu = jnp.mean(yc, axis=-1, keepdims=True)
    xc = yc - mu
    rs = lax.rsqrt(jnp.mean(xc * xc, axis=-1, keepdims=True) + EPS)
    xh = xc * rs
    return xh, rs, xh * g + b


SUB = 8
CONV_ROWS = 32


def _shifted_copies(ext8, rows):
    for b in range(1, SUB):
        ext8[b, pl.ds(0, rows - SUB), :] = ext8[0, pl.ds(b, rows - SUB), :]


def _tap(o):
    return o % SUB, o - o % SUB


def _conv_fwd(z, w_dw, b_dw, g_ln, b_ln, name):
    S = z.shape[0]
    W, C = w_dw.shape
    ts = _div(S, 512, HALO)
    hb = ts // HALO
    off = HALO - (W - 1)
    ca, cb = ZC_A // C, ZC_B // C
    rb = 2 * CONV_ROWS

    def body(a_ref, b_ref, pa_ref, pb_ref, w_ref, bd_ref, g_ref, bl_ref, u_ref, yc_ref, o_ref, ext8):
        keep = (pl.program_id(0) > 0).astype(F32)
        u = a_ref[...] * _sigmoid(b_ref[...])
        ext8[0, pl.ds(0, HALO), :] = pa_ref[...] * _sigmoid(pb_ref[...]) * keep
        ext8[0, pl.ds(HALO, ts), :] = u
        u_ref[...] = u
        _shifted_copies(ext8, ts + HALO)

        for lg in range(C // LANE):
            lanes = pl.ds(lg * LANE, LANE)
            taps = [jnp.broadcast_to(w_ref[pl.ds(j, 1), lanes], (SUB, LANE)) for j in range(W)]
            bias = jnp.broadcast_to(bd_ref[:, lanes], (SUB, LANE))

            def sub(i, carry, lanes=lanes, taps=taps, bias=bias):
                r0 = pl.multiple_of(i * rb, rb)
                accs = [bias] * (rb // SUB)
                for j in range(W):
                    b, a = _tap(off + j)
                    for r in range(rb // SUB):
                        accs[r] = accs[r] + taps[j] * ext8[b, pl.ds(r0 + a + r * SUB, SUB), lanes]
                for r in range(rb // SUB):
                    yc_ref[pl.ds(r0 + r * SUB, SUB), lanes] = accs[r]
                return carry

            lax.fori_loop(0, ts // rb, sub, 0)
        _, _, ln = _ln_parts(yc_ref[...], g_ref[...], bl_ref[...])
        o_ref[...] = (ln * _sigmoid(ln)).astype(BF16)

    cur = lambda col: pl.BlockSpec((ts, C), lambda i: (i, col))
    prev = lambda col: pl.BlockSpec((HALO, C), lambda i: (jnp.maximum(i * hb - 1, 0), col))
    vec = pl.BlockSpec((1, C), lambda i: (0, 0))
    row = pl.BlockSpec((ts, C), lambda i: (i, 0))
    fs = jax.ShapeDtypeStruct((S, C), F32)
    return pl.pallas_call(
        body, grid=(S // ts,),
        in_specs=[cur(ca), cur(cb), prev(ca), prev(cb), pl.BlockSpec((W, C), lambda i: (0, 0)), vec, vec, vec],
        out_specs=[row, row, row], out_shape=[fs, fs, jax.ShapeDtypeStruct((S, C), BF16)],
        scratch_shapes=[pltpu.VMEM((SUB, ts + HALO, C), F32)],
        compiler_params=_cp(("parallel",)), name=name,
    )(z, z, z, z, w_dw, b_dw, g_ln, b_ln)


def _conv_bwd(dycat, z, u, yc, w_dw, g_ln, b_ln, name):
    S = z.shape[0]
    W, C = w_dw.shape
    ts = _div(S, 512, HALO)
    hb = ts // HALO
    nblk = S // ts
    off = HALO - (W - 1)
    ca, cb = ZC_A // C, ZC_B // C
    rb = CONV_ROWS

    def ln_silu_bwd(dy, ycv, g, b):
        xh, rs, ln = _ln_parts(ycv, g, b)
        sl = _sigmoid(ln)
        dln = dy * (sl * (1.0 + ln * (1.0 - sl)))
        dxh = dln * g
        dyc = rs * (dxh - jnp.mean(dxh, axis=-1, keepdims=True) - xh * jnp.mean(dxh * xh, axis=-1, keepdims=True))
        return dyc, dln, xh

    def body(dy_ref, ndy_ref, yc_ref, nyc_ref, u_ref, pu_ref, a_ref, b_ref, w_ref, g_ref, bl_ref,
             dab_ref, dw_ref, dbd_ref, dg_ref, dbl_ref, uext8, dext8, dwacc):
        i = pl.program_id(0)

        @pl.when(i == 0)
        def _():
            dwacc[...] = jnp.zeros_like(dwacc)
            dbd_ref[...] = jnp.zeros_like(dbd_ref)
            dg_ref[...] = jnp.zeros_like(dg_ref)
            dbl_ref[...] = jnp.zeros_like(dbl_ref)

        g = g_ref[...]
        bl = bl_ref[...]
        dyc, dln, xh = ln_silu_bwd(dy_ref[...], yc_ref[...], g, bl)
        ndyc, _, _ = ln_silu_bwd(ndy_ref[...], nyc_ref[...], g, bl)
        dg_ref[...] += jnp.sum(dln * xh, axis=0, keepdims=True)
        dbl_ref[...] += jnp.sum(dln, axis=0, keepdims=True)
        dbd_ref[...] += jnp.sum(dyc, axis=0, keepdims=True)
        dext8[0, pl.ds(0, ts), :] = dyc
        dext8[0, pl.ds(ts, HALO), :] = ndyc * (i < nblk - 1).astype(F32)
        uext8[0, pl.ds(0, HALO), :] = pu_ref[...] * (i > 0).astype(F32)
        uext8[0, pl.ds(HALO, ts), :] = u_ref[...]
        _shifted_copies(dext8, ts + HALO)
        _shifted_copies(uext8, ts + HALO)

        def sub(k, carry):
            r0 = pl.multiple_of(k * rb, rb)
            rows = pl.ds(r0, rb)
            dyt = dext8[0, rows, :]
            du = jnp.zeros((rb, C), F32)
            for j in range(W):
                b, a = _tap(W - 1 - j)
                du = du + w_ref[pl.ds(j, 1), :] * dext8[b, pl.ds(r0 + a, rb), :]
                b, a = _tap(off + j)
                p = dyt * uext8[b, pl.ds(r0 + a, rb), :]
                part = p[0:SUB]
                for q in range(1, rb // SUB):
                    part = part + p[q * SUB:(q + 1) * SUB]
                dwacc[j] += part
            sb = _sigmoid(b_ref[rows, :])
            dab_ref[rows, 0:C] = (du * sb).astype(BF16)
            dab_ref[rows, C:2 * C] = (du * a_ref[rows, :] * sb * (1.0 - sb)).astype(BF16)
            return carry

        lax.fori_loop(0, ts // rb, sub, 0)

        @pl.when(i == nblk - 1)
        def _():
            for j in range(W):
                dw_ref[pl.ds(j, 1), :] = jnp.sum(dwacc[j], axis=0, keepdims=True)

    row = pl.BlockSpec((ts, C), lambda i: (i, 0))
    nxt = pl.BlockSpec((HALO, C), lambda i: (jnp.minimum((i + 1) * hb, S // HALO - 1), 0))
    prv = pl.BlockSpec((HALO, C), lambda i: (jnp.maximum(i * hb - 1, 0), 0))
    vec = pl.BlockSpec((1, C), lambda i: (0, 0))
    wsp = pl.BlockSpec((W, C), lambda i: (0, 0))
    vs = jax.ShapeDtypeStruct((1, C), F32)
    return pl.pallas_call(
        body, grid=(nblk,),
        in_specs=[row, nxt, row, nxt, row, prv, pl.BlockSpec((ts, C), lambda i: (i, ca)), pl.BlockSpec((ts, C), lambda i: (i, cb)), wsp, vec, vec],
        out_specs=[pl.BlockSpec((ts, 2 * C), lambda i: (i, 0)), wsp, vec, vec, vec],
        out_shape=[jax.ShapeDtypeStruct((S, 2 * C), BF16), jax.ShapeDtypeStruct((W, C), F32), vs, vs, vs],
        scratch_shapes=[pltpu.VMEM((SUB, ts + HALO, C), F32), pltpu.VMEM((SUB, ts + HALO, C), F32), pltpu.VMEM((W, SUB, C), F32)],
        compiler_params=_cp(("arbitrary",)), name=name,
    )(dycat, dycat, yc, yc, u, u, z, z, w_dw, g_ln, b_ln)


def _log_gate(zg):
    return (jnp.minimum(zg, 0.0) - jnp.log(1.0 + jnp.exp(-jnp.abs(zg)))) * (1.0 / GATE_TAU)


def _loggate(z, wgp, bgp, name):
    S = z.shape[0]
    N = wgp.shape[1]
    ts = _div(S, 512, 8)

    def body(g_ref, w_ref, b_ref, o_ref):
        o_ref[...] = _log_gate(_dot(g_ref[...], w_ref[...]) + b_ref[...])

    return pl.pallas_call(
        body, grid=(S // ts,),
        in_specs=[pl.BlockSpec((ts, LANE), lambda i: (i, ZC_G // LANE)), pl.BlockSpec((LANE, N), lambda i: (0, 0)), pl.BlockSpec((1, N), lambda i: (0, 0))],
        out_specs=pl.BlockSpec((ts, N), lambda i: (i, 0)), out_shape=jax.ShapeDtypeStruct((S, N), F32),
        compiler_params=_cp(("parallel",)), name=name,
    )(z, wgp, bgp)


def _loggate_bwd(dla, z, wgp, wgp_t, bgp, name):
    S = z.shape[0]
    N = wgp.shape[1]
    ts = _div(S, 512, 8)

    def body(dla_ref, g_ref, w_ref, wt_ref, b_ref, dg_ref, dw_ref, db_ref):
        @pl.when(pl.program_id(0) == 0)
        def _():
            dw_ref[...] = jnp.zeros_like(dw_ref)
            db_ref[...] = jnp.zeros_like(db_ref)

        glr = g_ref[...]
        zg = _dot(glr, w_ref[...]) + b_ref[...]
        dzg = dla_ref[...] * (1.0 / GATE_TAU) * (1.0 - _sigmoid(zg))
        dg_ref[...] = _dot(dzg, wt_ref[...]).astype(BF16)
        dw_ref[...] += _dot_tn(glr, dzg)
        db_ref[...] += jnp.sum(dzg, axis=0, keepdims=True)

    return pl.pallas_call(
        body, grid=(S // ts,),
        in_specs=[pl.BlockSpec((ts, N), lambda i: (i, 0)), pl.BlockSpec((ts, LANE), lambda i: (i, ZC_G // LANE)),
                  pl.BlockSpec((LANE, N), lambda i: (0, 0)), pl.BlockSpec((N, LANE), lambda i: (0, 0)), pl.BlockSpec((1, N), lambda i: (0, 0))],
        out_specs=[pl.BlockSpec((ts, LANE), lambda i: (i, 0)), pl.BlockSpec((LANE, N), lambda i: (0, 0)), pl.BlockSpec((1, N), lambda i: (0, 0))],
        out_shape=[jax.ShapeDtypeStruct((S, LANE), BF16), jax.ShapeDtypeStruct((LANE, N), F32), jax.ShapeDtypeStruct((1, N), F32)],
        compiler_params=_cp(("arbitrary",)), name=name,
    )(dla, z, wgp, wgp_t, bgp)


def _bdot(a, b, ca, cb):
    return lax.dot_general(a.astype(BF16), b.astype(BF16), (((ca,), (cb,)), ((0,), (0,))), preferred_element_type=F32)


def _bdot_exact(a, b):
    return lax.dot_general(a, b, (((2,), (1,)), ((0,), (0,))), preferred_element_type=F32, precision=lax.Precision.HIGHEST)


def _tiles(ref, cpb):
    return jnp.stack([ref[pl.ds(c * CHUNK, CHUNK), pl.ds(h * LANE, LANE)] for c in range(cpb) for h in range(HEADS)])


def _tri_masks(n):
    ri = lax.broadcasted_iota(jnp.int32, (n, CHUNK, CHUNK), 1)
    ci = lax.broadcasted_iota(jnp.int32, (n, CHUNK, CHUNK), 2)
    return ri >= ci, (ri >= ci).astype(F32), (ri <= ci).astype(F32)


def _chunk_fwd_terms(q, k, la, tril):
    bc = _bdot_exact(tril, la)
    bend = jnp.sum(la, axis=1, keepdims=True)
    eb = jnp.exp(bc)
    enb = jnp.exp(-bc)
    ee = jnp.exp(bend - bc)
    qs = q * (DK ** -0.5)
    return bend, eb, enb, ee, qs * eb, qs * enb, k * enb, k * eb, k * ee


def _gla_fwd(z, la, gn, name):
    S = z.shape[0]
    W = HEADS * LANE
    tb = _div(S, 512, CHUNK)
    cpb = tb // CHUNK

    def body(q_ref, k_ref, v_ref, r_ref, la_ref, gn_ref, o_ref, sp_ref, y_ref, st):
        @pl.when(pl.program_id(0) == 0)
        def _():
            st[...] = jnp.zeros_like(st)

        tri, tril, _ = _tri_masks(cpb * HEADS)
        q, k, v, rv, lav = (_tiles(r, cpb) for r in (q_ref, k_ref, v_ref, r_ref, la_ref))
        bend, _, _, _, qf, qb, kb, kf, ke = _chunk_fwd_terms(q, k, lav, tril)
        att = jnp.where(tri, _bdot(qf, kb, 2, 2), _bdot(qb, kf, 2, 2))
        o_intra = _bdot(att, v, 2, 1)
        u = _bdot(v, ke, 1, 1)
        gdec = jnp.exp(bend)
        s_prev = [None] * (cpb * HEADS)
        for h in range(HEADS):
            s = st[h]
            for c in range(cpb):
                b = c * HEADS + h
                s_prev[b] = s
                s = s * gdec[b] + u[b]
            st[h] = s
        s_prev = jnp.stack(s_prev)
        o = o_intra + _bdot(qf, s_prev, 2, 2)
        rms = lax.rsqrt(jnp.mean(o * o, axis=-1, keepdims=True) + EPS)
        gn = jnp.stack([gn_ref[pl.ds(h, 1), :] for _ in range(cpb) for h in range(HEADS)])
        y = (o * rms * gn * (rv * _sigmoid(rv))).astype(BF16)
        for c in range(cpb):
            for h in range(HEADS):
                b = c * HEADS + h
                rows, ln = pl.ds(c * CHUNK, CHUNK), pl.ds(h * LANE, LANE)
                o_ref[rows, ln] = o[b]
                y_ref[rows, ln] = y[b]
                sp_ref[h, c] = s_prev[b]

    zb = lambda base: pl.BlockSpec((tb, W), lambda i: (i, base // W))
    hb_ = pl.BlockSpec((tb, W), lambda i: (i, 0))
    return pl.pallas_call(
        body, grid=(S // tb,),
        in_specs=[zb(ZC_Q), zb(ZC_K), zb(ZC_V), zb(ZC_R), hb_, pl.BlockSpec((HEADS, DV), lambda i: (0, 0))],
        out_specs=[hb_, pl.BlockSpec((HEADS, cpb, DV, DKP), lambda i: (0, i, 0, 0)), hb_],
        out_shape=[jax.ShapeDtypeStruct((S, D_GLA), F32), jax.ShapeDtypeStruct((HEADS, S // CHUNK, DV, DKP), F32),
                   jax.ShapeDtypeStruct((S, D_GLA), BF16)],
        scratch_shapes=[pltpu.VMEM((HEADS, DV, DKP), F32)],
        compiler_params=_cp(("arbitrary",)), name=name,
    )(z, z, z, z, la, gn)


def _gla_bwd(dycat, z, la, o_raw, sprev, gn, name):
    S = z.shape[0]
    W = HEADS * LANE
    tb = _div(S, 512, CHUNK)
    cpb = tb // CHUNK
    nb = S // tb

    def body(q_ref, k_ref, v_ref, r_ref, la_ref, o_ref, sp_ref, dy_ref, gn_ref, dz_ref, dla_ref, dgn_ref, dst):
        @pl.when(pl.program_id(0) == 0)
        def _():
            dst[...] = jnp.zeros_like(dst)
            dgn_ref[...] = jnp.zeros_like(dgn_ref)

        nt = cpb * HEADS
        tri, tril, triu = _tri_masks(nt)
        q, k, v, rv, lav, o, dy = (_tiles(r, cpb) for r in (q_ref, k_ref, v_ref, r_ref, la_ref, o_ref, dy_ref))
        bend, eb, enb, ee, qf, qb, kb, kf, ke = _chunk_fwd_terms(q, k, lav, tril)
        att = jnp.where(tri, _bdot(qf, kb, 2, 2), _bdot(qb, kf, 2, 2))
        s_prev = jnp.stack([sp_ref[h, c] for c in range(cpb) for h in range(HEADS)])
        gdec = jnp.exp(bend)
        gn = jnp.stack([gn_ref[pl.ds(h, 1), :] for _ in range(cpb) for h in range(HEADS)])
        rms = lax.rsqrt(jnp.mean(o * o, axis=-1, keepdims=True) + EPS)
        oh = o * rms
        sg = _sigmoid(rv)
        sr = rv * sg
        d_r = (dy * oh * gn * (sg * (1.0 + rv * (1.0 - sg)))).astype(BF16)
        dgn = jnp.sum(dy * sr * oh, axis=1, keepdims=True)
        w = dy * sr * gn
        do = rms * (w - oh * jnp.mean(w * oh, axis=-1, keepdims=True))
        p = _bdot(do, qf, 1, 1)
        ds = [None] * nt
        for h in range(HEADS):
            s = dst[h]
            for c in reversed(range(cpb)):
                b = c * HEADS + h
                ds[b] = s
                s = s * gdec[b] + p[b]
            dst[h] = s
            dgn_ref[pl.ds(h, 1), :] += sum(dgn[c * HEADS + h] for c in range(cpb))
        ds = jnp.stack(ds)
        datt = _bdot(do, v, 2, 2)
        daf = jnp.where(tri, datt, 0.0)
        dab = jnp.where(tri, 0.0, datt)
        d_v = (_bdot(att, do, 1, 1) + _bdot(ke, ds, 2, 2)).astype(BF16)
        dke = _bdot(v, ds, 2, 1)
        dqf = _bdot(daf, kb, 2, 1) + _bdot(do, s_prev, 2, 1)
        dkb = _bdot(daf, qf, 1, 1)
        dqb = _bdot(dab, kf, 2, 1)
        dkf = _bdot(dab, qb, 1, 1)
        dg = jnp.sum(ds * s_prev, axis=1, keepdims=True)
        d_q = ((dqf * eb + dqb * enb) * (DK ** -0.5)).astype(BF16)
        d_k = (dkb * enb + dkf * eb + dke * ee).astype(BF16)
        dbc = dqf * qf - dkb * kb - dqb * qb + dkf * kf - dke * ke
        dbend = jnp.sum(dke * ke, axis=1, keepdims=True) + dg * gdec
        dla = _bdot_exact(triu, dbc) + dbend
        for c in range(cpb):
            for h in range(HEADS):
                b = c * HEADS + h
                rows = pl.ds(c * CHUNK, CHUNK)
                for base, val in ((ZC_Q, d_q), (ZC_K, d_k), (ZC_V, d_v), (ZC_R, d_r)):
                    dz_ref[rows, pl.ds(base + h * LANE, LANE)] = val[b]
                dla_ref[rows, pl.ds(h * LANE, LANE)] = dla[b]

    zb = lambda base: pl.BlockSpec((tb, W), lambda i: (nb - 1 - i, base // W))
    hb_ = pl.BlockSpec((tb, W), lambda i: (nb - 1 - i, 0))
    return pl.pallas_call(
        body, grid=(nb,),
        in_specs=[zb(ZC_Q), zb(ZC_K), zb(ZC_V), zb(ZC_R), hb_, hb_,
                  pl.BlockSpec((HEADS, cpb, DV, DKP), lambda i: (0, nb - 1 - i, 0, 0)),
                  pl.BlockSpec((tb, W), lambda i: (nb - 1 - i, 1)),
                  pl.BlockSpec((HEADS, DV), lambda i: (0, 0))],
        out_specs=[pl.BlockSpec((tb, Z_GLA), lambda i: (nb - 1 - i, 0)), hb_, pl.BlockSpec((HEADS, DV), lambda i: (0, 0))],
        out_shape=[jax.ShapeDtypeStruct((S, Z_GLA), BF16), jax.ShapeDtypeStruct((S, HEADS * DKP), F32), jax.ShapeDtypeStruct((HEADS, DV), F32)],
        scratch_shapes=[pltpu.VMEM((HEADS, DV, DKP), F32)],
        compiler_params=_cp(("arbitrary",)), name=name,
    )(z, z, z, z, la, o_raw, sprev, dycat, gn)


def _mod_proj(c_all, w3, layer, b, name):
    B, D = c_all.shape
    N = w3.shape[2]
    tn = _div(N, 1024, LANE)

    def body(c_ref, w_ref, b_ref, o_ref):
        cv = c_ref[...]
        o_ref[...] = _dot(cv * _sigmoid(cv), w_ref[...]) + b_ref[...]

    return pl.pallas_call(
        body, grid=(N // tn,),
        in_specs=[pl.BlockSpec((B, D), lambda j: (0, 0)), pl.BlockSpec((None, D, tn), lambda j: (layer, 0, j)), pl.BlockSpec((1, tn), lambda j: (0, j))],
        out_specs=pl.BlockSpec((B, tn), lambda j: (0, j)), out_shape=jax.ShapeDtypeStruct((B, N), F32),
        compiler_params=_cp(("parallel",)), name=name,
    )(c_all, w3, b)


def _mod_wgrad(c_t, dm, name):
    D, B = c_t.shape
    N = dm.shape[1]
    tn = _div(N, 1024, LANE)

    def body(c_ref, d_ref, o_ref):
        cv = c_ref[...]
        ca = cv * _sigmoid(cv)
        acc = ca[:, 0:1] * d_ref[pl.ds(0, 1), :]
        for b in range(1, B):
            acc = acc + ca[:, b:b + 1] * d_ref[pl.ds(b, 1), :]
        o_ref[...] = acc

    return pl.pallas_call(
        body, grid=(N // tn,),
        in_specs=[pl.BlockSpec((D, B), lambda j: (0, 0)), pl.BlockSpec((B, tn), lambda j: (0, j))],
        out_specs=pl.BlockSpec((D, tn), lambda j: (0, j)), out_shape=jax.ShapeDtypeStruct((D, N), F32),
        compiler_params=_cp(("parallel",)), name=name,
    )(c_t, dm)


def _rowsum(xs, name):
    n, N = xs.shape
    tn = _div(N, 8192, LANE)

    def body(x_ref, o_ref):
        acc = x_ref[pl.ds(0, 1), :]
        for r in range(1, n):
            acc = acc + x_ref[pl.ds(r, 1), :]
        o_ref[...] = acc

    return pl.pallas_call(
        body, grid=(N // tn,), in_specs=[pl.BlockSpec((n, tn), lambda j: (0, j))],
        out_specs=pl.BlockSpec((1, tn), lambda j: (0, j)), out_shape=jax.ShapeDtypeStruct((1, N), F32),
        compiler_params=_cp(("parallel",)), name=name,
    )(xs)


def _adamw(w, g, m, v, name, copy_grad=False):
    R, C = w.shape
    tr = _div(R, max(8, (1 << 18) // C), 8)

    def body(w_ref, g_ref, m_ref, v_ref, d_ref, nm_ref, nv_ref, *g_out):
        gv = g_ref[...]
        if copy_grad:
            g_out[0][...] = gv
        mn = ADAM_B1 * m_ref[...] + (1.0 - ADAM_B1) * gv
        vn = ADAM_B2 * v_ref[...] + (1.0 - ADAM_B2) * (gv * gv)
        m_hat = mn / (1.0 - ADAM_B1 ** ADAM_STEP)
        v_hat = vn / (1.0 - ADAM_B2 ** ADAM_STEP)
        d_ref[...] = -ADAM_LR * (m_hat / (jnp.sqrt(v_hat) + ADAM_EPS) + ADAM_WD * w_ref[...])
        nm_ref[...] = mn
        nv_ref[...] = vn

    blk = pl.BlockSpec((tr, C), lambda i: (i, 0))
    os_ = jax.ShapeDtypeStruct((R, C), F32)
    n_out = 4 if copy_grad else 3
    return pl.pallas_call(
        body, grid=(R // tr,), in_specs=[blk] * 4, out_specs=[blk] * n_out, out_shape=[os_] * n_out,
        compiler_params=_cp(("parallel",)), name=name,
    )(w, g, m, v)


def _place():
    return lax.axis_index("x"), lax.axis_index("y"), lax.axis_index("c")


def _other_chips(x, y):
    return [(1 - x, y), (x, 1 - y), (1 - x, 1 - y)]


def _half(c, rows):
    return pl.ds(c * (rows // 2), rows // 2)


_ANY = pl.BlockSpec(memory_space=pl.ANY)


def _ag_small(v, name):
    r, n = v.shape

    def body(v_ref, o_ref, send_sems, recv_sems):
        x, y, c = _place()
        me = 4 * x + 2 * y + c
        o_ref[pl.ds(me, 1)] = v_ref[...][None]
        peers = [(x ^ (k >> 2), y ^ ((k >> 1) & 1), c ^ (k & 1)) for k in range(1, 8)]
        copies = []
        for k, peer in enumerate(peers):
            cp = pltpu.make_async_remote_copy(
                src_ref=v_ref, dst_ref=o_ref.at[me], send_sem=send_sems.at[k], recv_sem=recv_sems.at[k],
                device_id=peer, device_id_type=MESH)
            cp.start()
            copies.append(cp)
        for cp in copies:
            cp.wait()

    return pl.pallas_call(
        body, out_shape=jax.ShapeDtypeStruct((8, r, n), v.dtype),
        in_specs=[pl.BlockSpec(memory_space=pltpu.VMEM)], out_specs=pl.BlockSpec(memory_space=pltpu.VMEM),
        scratch_shapes=[pltpu.SemaphoreType.DMA((7,)), pltpu.SemaphoreType.DMA((7,))],
        compiler_params=pltpu.CompilerParams(vmem_limit_bytes=VMEM_LIMIT), name=name,
    )(v)


def _rs_sibling_copies(bufs, send_sems, recv_sems):
    n = len(bufs) // 2
    x, y, c = _place()
    return [pltpu.make_async_remote_copy(
        src_ref=bufs[i].at[:, _half(1 - c, bufs[i].shape[1])], dst_ref=bufs[n + i], send_sem=send_sems.at[i], recv_sem=recv_sems.at[i],
        device_id=(x, y, 1 - c), device_id_type=MESH) for i in range(n)]


def _rs_sibling(gs, name, after=None):
    n = len(gs)

    def body(*refs):
        copies = _rs_sibling_copies(refs[:n] + refs[n + 1:2 * n + 1], refs[2 * n + 1], refs[2 * n + 2])
        for cp in copies:
            cp.start()
        for cp in copies:
            cp.wait()

    return pl.pallas_call(
        body, out_shape=[jax.ShapeDtypeStruct((N_CHIPS, g.shape[1] // 2, g.shape[2]), g.dtype) for g in gs],
        in_specs=[_ANY] * (n + 1), out_specs=[_ANY] * n,
        scratch_shapes=[pltpu.SemaphoreType.DMA((n,)), pltpu.SemaphoreType.DMA((n,))],
        compiler_params=pltpu.CompilerParams(has_side_effects=True), name=name,
    )(*gs, gs[0] if after is None else after)


def _rs_presum(g, sib, c_arr, name):
    ns, R, C = g.shape
    rh = R // 2
    tr = _div(rh, max(16, (1 << 19) // C), 16)
    nrb = rh // tr

    def body(c_ref, g_ref, s_ref, o_ref):
        o_ref[...] = (g_ref[...] + s_ref[...]).astype(BF16)

    return pl.pallas_call(
        body, out_shape=jax.ShapeDtypeStruct((ns, rh, C), BF16),
        grid_spec=pltpu.PrefetchScalarGridSpec(
            num_scalar_prefetch=1, grid=(ns, nrb),
            in_specs=[pl.BlockSpec((None, tr, C), lambda s, r, c_ref: (s, c_ref[0] * nrb + r, 0)),
                      pl.BlockSpec((None, tr, C), lambda s, r, c_ref: (s, r, 0))],
            out_specs=pl.BlockSpec((None, tr, C), lambda s, r, c_ref: (s, r, 0))),
        compiler_params=_cp(("parallel", "parallel")), name=name,
    )(c_arr, g, sib)


def _rs_sum(g, sib, recv, full, layer, sc_arr, name):
    ns, R, C = g.shape
    rh = R // 2
    tr = _div(rh, max(16, (1 << 18) // C), 16)
    nrb = rh // tr

    def body(sc_ref, g_ref, s_ref, r_ref, f_ref, o_ref):
        acc = g_ref[...] + s_ref[...]
        for j in range(3):
            acc = acc + r_ref[j].astype(F32)
        o_ref[...] = acc

    return pl.pallas_call(
        body, out_shape=jax.ShapeDtypeStruct(full.shape, F32),
        grid_spec=pltpu.PrefetchScalarGridSpec(
            num_scalar_prefetch=1, grid=(nrb,),
            in_specs=[pl.BlockSpec((None, tr, C), lambda r, sc: (sc[0], sc[1] * nrb + r, 0)),
                      pl.BlockSpec((None, tr, C), lambda r, sc: (sc[0], r, 0)),
                      pl.BlockSpec((3, tr, C), lambda r, sc: (0, r, 0)),
                      _ANY],
            out_specs=pl.BlockSpec((None, tr, C), lambda r, sc: (layer, sc[1] * nrb + r, 0))),
        input_output_aliases={4: 0},
        compiler_params=_cp(("parallel",)), name=name,
    )(sc_arr, g, sib, recv, full)


def _rs_share(fulls, layer, name):
    n = len(fulls)

    def body(*refs):
        src, out = refs[:n], refs[n:2 * n]
        send_sems, recv_sems = refs[2 * n:]
        x, y, c = _place()
        copies = []
        for i in range(n):
            rows = out[i].shape[1]
            cp = pltpu.make_async_remote_copy(
                src_ref=out[i].at[layer, _half(c, rows)], dst_ref=out[i].at[layer, _half(c, rows)],
                send_sem=send_sems.at[i], recv_sem=recv_sems.at[i], device_id=(x, y, 1 - c), device_id_type=MESH)
            cp.start()
            copies.append(cp)
        for cp in copies:
            cp.wait()

    return pl.pallas_call(
        body, out_shape=[jax.ShapeDtypeStruct(f.shape, f.dtype) for f in fulls],
        in_specs=[_ANY] * n, out_specs=[_ANY] * n, input_output_aliases={i: i for i in range(n)},
        scratch_shapes=[pltpu.SemaphoreType.DMA((n,)), pltpu.SemaphoreType.DMA((n,))],
        compiler_params=pltpu.CompilerParams(has_side_effects=True), name=name,
    )(*fulls)


_HBM = pl.BlockSpec(memory_space=pltpu.HBM)
_SEM = pl.BlockSpec(memory_space=pltpu.SEMAPHORE)
_EFFECT = pltpu.SideEffectType.DATAFLOW_SIDE_EFFECTING


def _in_hbm(a):
    return pltpu.with_memory_space_constraint(a, pltpu.HBM)


def _split_start(bufs, n_sem, copies_of, name):
    nb = len(bufs)

    def body(*refs):
        for cp in copies_of(refs[:nb], refs[nb], refs[nb + 1]):
            cp.start()
        refs[-1][...] = jnp.zeros_like(refs[-1])

    out = pl.pallas_call(
        body, name=name,
        out_shape=(pltpu.SemaphoreType.DMA((n_sem,)), pltpu.SemaphoreType.DMA((n_sem,)), *[pltpu.HBM(a.shape, a.dtype) for a in bufs],
                   jax.ShapeDtypeStruct((SUB, LANE), F32)),
        in_specs=[_HBM] * nb, out_specs=(_SEM, _SEM, *([_HBM] * nb), pl.BlockSpec(memory_space=pltpu.VMEM)),
        input_output_aliases={i: 2 + i for i in range(nb)},
        compiler_params=pltpu.CompilerParams(has_side_effects=_EFFECT),
    )(*[_in_hbm(a) for a in bufs])
    return out[0], out[1], list(out[2:2 + nb]), out[-1]


def _split_wait(send_sems, recv_sems, bufs, after, copies_of, name):
    nb = len(bufs)

    def body(*refs):
        for cp in copies_of(refs[:nb], refs[nb], refs[nb + 1]):
            cp.wait_send()
            cp.wait_recv()

    return list(pl.pallas_call(
        body, name=name, out_shape=[pltpu.HBM(a.shape, a.dtype) for a in bufs],
        in_specs=[_HBM] * nb + [_SEM, _SEM, _ANY], out_specs=[_HBM] * nb,
        input_output_aliases={i: i for i in range(nb)},
        compiler_params=pltpu.CompilerParams(has_side_effects=_EFFECT),
    )(*bufs, send_sems, recv_sems, after))


def _ag_half_copies(land, send_sems, recv_sems, landing_of_mine):
    x, y, c = _place()
    cps = []
    for j, (cx, cy) in enumerate(_other_chips(x, y)):
        for i in range(len(land)):
            rows = _half(c, land[i].shape[1])
            s = 2 * x + y if landing_of_mine else 2 * cx + cy
            cps.append(pltpu.make_async_remote_copy(
                src_ref=land[i].at[2 * x + y, rows], dst_ref=land[i].at[s, rows], send_sem=send_sems.at[3 * i + j], recv_sem=recv_sems.at[3 * i + j],
                device_id=(cx, cy, c), device_id_type=MESH))
    return cps


def _ag_starts(land, send_sems, recv_sems):
    return _ag_half_copies(land, send_sems, recv_sems, True)


def _ag_waits(land, send_sems, recv_sems):
    return _ag_half_copies(land, send_sems, recv_sems, False)


def _ag_finish(lands, name):
    n = len(lands)

    def body(*refs):
        land = refs[n:2 * n]
        send_sems, recv_sems = refs[2 * n:]
        x, y, c = _place()
        sibling = (x, y, 1 - c)

        def copy(k, i, s, h):
            blk = land[i].at[s, _half(h, land[i].shape[1])]
            return pltpu.make_async_remote_copy(
                src_ref=blk, dst_ref=blk, send_sem=send_sems.at[k], recv_sem=recv_sems.at[k], device_id=sibling, device_id_type=MESH)

        chips = _other_chips(x, y)
        passed = [copy(3 * i + j, i, 2 * cx + cy, c) for j, (cx, cy) in enumerate(chips) for i in range(n)]
        for cp in passed:
            cp.start()
        for j, (cx, cy) in enumerate(chips):
            for i in range(n):
                copy(3 * i + j, i, 2 * cx + cy, 1 - c).wait_recv()
        for cp in passed:
            cp.wait_send()

    return pl.pallas_call(
        body, out_shape=[jax.ShapeDtypeStruct(a.shape, a.dtype) for a in lands],
        in_specs=[_ANY] * n, out_specs=[_ANY] * n, input_output_aliases={i: i for i in range(n)},
        scratch_shapes=[pltpu.SemaphoreType.DMA((3 * n,)), pltpu.SemaphoreType.DMA((3 * n,))],
        compiler_params=pltpu.CompilerParams(has_side_effects=True), name=name,
    )(*lands)


def _rs_chip_copies(bufs, send_sems, recv_sems):
    n = len(bufs) // 2
    x, y, c = _place()
    return [pltpu.make_async_remote_copy(
        src_ref=bufs[i].at[2 * cx + cy], dst_ref=bufs[n + i].at[j], send_sem=send_sems.at[3 * i + j], recv_sem=recv_sems.at[3 * i + j],
        device_id=(cx, cy, c), device_id_type=MESH) for j, (cx, cy) in enumerate(_other_chips(x, y)) for i in range(n)]


def _pad_heads(w):
    lead = w.shape[:-1]
    w4 = w.reshape(*lead, HEADS, DK)
    w4 = jnp.pad(w4, [(0, 0)] * len(lead) + [(0, 0), (0, DKP - DK)])
    return w4.reshape(*lead, HEADS * DKP)


def _unpad_heads(w):
    lead = w.shape[:-1]
    return w.reshape(*lead, HEADS, DKP)[..., :DK].reshape(*lead, HEADS * DK)


def _mix_column_runs():
    o = 2 * D_CONV
    runs = [(0, D_CONV, ZC_A), (D_CONV, D_CONV, ZC_B)]
    runs += [(o + h * DK, DK, ZC_Q + h * DKP) for h in range(HEADS)]
    runs += [(o + HEADS * DK + h * DK, DK, ZC_K + h * DKP) for h in range(HEADS)]
    o += 2 * HEADS * DK
    return runs + [(o, D_GLA, ZC_V), (o + D_GLA, D_GLA, ZC_R), (o + 2 * D_GLA, GATE_RANK, ZC_G)]


def _mix_weight(win4, n_cols):
    D = win4.shape[1]
    pieces, z = [], 0
    for c0, width, z0 in sorted(_mix_column_runs(), key=lambda r: r[2]):
        if z0 > z:
            pieces.append(jnp.zeros((D, z0 - z), win4.dtype))
        c = c0
        while c < c0 + width:
            s, lo = divmod(c, n_cols)
            hi = min(n_cols, lo + c0 + width - c)
            pieces.append(win4[s, :, lo:hi])
            c += hi - lo
        z = z0 + width
    pieces.append(jnp.zeros((D, Z_COLS - z), win4.dtype))
    return jnp.concatenate(pieces, axis=1)


def _mix_weight_grad(dgla, dab, dglr, n_cols, n_pad):
    D = dab.shape[0]

    def zcols(z0, z1):
        for arr, base in ((dgla, 0), (dab, ZC_A), (dglr, ZC_G)):
            if base <= z0 and z1 <= base + arr.shape[1]:
                return arr[:, z0 - base:z1 - base]

    shards = []
    for s in range(N_CHIPS):
        pieces = []
        for c0, width, z0 in _mix_column_runs():
            lo, hi = max(c0, s * n_cols), min(c0 + width, (s + 1) * n_cols)
            if lo < hi:
                pieces.append(zcols(z0 + lo - c0, z0 + hi - c0))
        pieces.append(jnp.zeros((D, n_pad - n_cols), dab.dtype))
        shards.append(jnp.concatenate(pieces, axis=1))
    return jnp.stack(shards)


_ARG_NAMES = ['x', 'c', 'w_ada', 'b_ada', 'g_norm_ffn1', 'w_ffn1_in', 'w_ffn1_out', 'g_norm_mix', 'w_in', 'w_dw', 'b_dw', 'g_conv_ln', 'b_conv_ln', 'w_gate_up', 'b_gate', 'g_gla_norm', 'w_out', 'g_norm_ffn2', 'w_ffn2_in', 'w_ffn2_out', 'g_norm_final', 'w_ada_final', 'b_ada_final']
_WEIGHTS = _ARG_NAMES[2:]
_BIG = ('w_ffn1_in', 'w_ffn1_out', 'w_in', 'w_out', 'w_ffn2_in', 'w_ffn2_out')
_SMALL = ('g_norm_ffn1', 'g_norm_mix', 'w_dw', 'b_dw', 'g_conv_ln', 'b_conv_ln', 'w_gate_up', 'b_gate', 'g_gla_norm', 'g_norm_ffn2', 'g_norm_final')


def _ffn_fwd(x, h, gv, w4, wo, nxt, tag):
    if isinstance(h, tuple):
        z, act, h = _ffn_up(x, w4, f"ffn_up_{tag}", norm=h)
    else:
        z, act = _ffn_up(h, w4, f"ffn_up_{tag}")
    y, xn, *hn = _ffn_down(act, wo, x, gv, nxt, f"ffn_down_{tag}")
    return xn, (hn[0] if hn else None), y, (x, h, z, act)


def _ffn_bwd(dxn, dy, saved, g, scale, prev, w4, wo, tag):
    x, h, z, act = saved
    ns = w4.shape[0]
    dwo = _mm_tn(act, dy, f"dw_out_{tag}")
    dz, dx, dsh, dsc, dg, *pv = _ffn_bwd_core(dy, wo, z, w4, x, dxn, g, scale, prev, f"ffn_bwd_{tag}")
    dwi = _dw_ffn_in(h, dz, ns, f"dw_in_{tag}")
    return dx, pv, dict(dshift=dsh, dscale=dsc, dg=dg, dw_in=dwi, dw_out=dwo.reshape(N_CHIPS, -1, dwo.shape[1]))


def _mix_fwd(x, h, gv, wmix, w_dw, b_dw, g_ln, b_ln, wgp, bgp, gn, wout, nxt, tag):
    z = _mm([(h, wmix, 0)], F32, f"mix_in_{tag}")
    u, yc, yconv = _conv_fwd(z, w_dw, b_dw, g_ln, b_ln, f"conv_fwd_{tag}")
    la = _loggate(z, wgp, bgp, f"loggate_{tag}")
    o_raw, sprev, ygla = _gla_fwd(z, la, gn, f"gla_fwd_{tag}")
    y, xn, *hn = _mix_out(yconv, ygla, wout, x, gv, nxt, f"mix_out_{tag}")
    return xn, (hn[0] if hn else None), y, (x, h, z, u, yc, la, o_raw, sprev, yconv, ygla)


def _mix_bwd(dxn, dy, saved, g, scale, prev, wmix, w_dw, g_ln, b_ln, wgp, bgp, gn, wout, n_cols, n_pad, tag):
    x, h, z, u, yc, la, o_raw, sprev, yconv, ygla = saved
    dycat = _mm([(dy, wout, 0)], F32, f"mix_dycat_{tag}", nt=True)
    dwout = _mm_tn_two(yconv, ygla, dy, f"dw_mixout_{tag}")
    dab, dwdw, dbdw, dgln, dbln = _conv_bwd(dycat, z, u, yc, w_dw, g_ln, b_ln, f"conv_bwd_{tag}")
    dgla, dla, dgn = _gla_bwd(dycat, z, la, o_raw, sprev, gn, f"gla_bwd_{tag}")
    dglr, dwgp, dbgp = _loggate_bwd(dla, z, wgp, wgp.T, bgp, f"loggate_bwd_{tag}")
    dx, dsh, dsc, dg, *pv = _dh_normmod_bwd(
        [(dgla, wmix, 0, 0, Z_GLA), (dab, wmix, 0, ZC_A // (2 * D_CONV), 2 * D_CONV), (dglr, wmix, 0, ZC_G // LANE, LANE)],
        x, dxn, g, scale, prev, f"mix_dh_{tag}")
    dwin = _mix_weight_grad(_mm_tn(h, dgla, f"dw_mixin_gla_{tag}"), _mm_tn(h, dab, f"dw_mixin_conv_{tag}"), _mm_tn(h, dglr, f"dw_mixin_gate_{tag}"),
                            n_cols, n_pad)
    grads = dict(dshift=dsh, dscale=dsc, dg=dg, dw_in=dwin, dw_out=dwout.reshape(N_CHIPS, -1, dwout.shape[1]), dw_dw=dwdw, db_dw=dbdw,
                 dg_ln=dgln, db_ln=dbln, dw_gate=_unpad_heads(dwgp[:GATE_RANK]), db_gate=_unpad_heads(dbgp)[0], dgn=dgn)
    return dx, pv, grads


def kernel(x, c, w_ada, b_ada, g_norm_ffn1, w_ffn1_in, w_ffn1_out, g_norm_mix, w_in, w_dw, b_dw, g_conv_ln, b_conv_ln, w_gate_up, b_gate, g_gla_norm, w_out, g_norm_ffn2, w_ffn2_in, w_ffn2_out, g_norm_final, w_ada_final, b_ada_final, loss_target, m_w_ada, m_b_ada, m_g_norm_ffn1, m_w_ffn1_in, m_w_ffn1_out, m_g_norm_mix, m_w_in, m_w_dw, m_b_dw, m_g_conv_ln, m_b_conv_ln, m_w_gate_up, m_b_gate, m_g_gla_norm, m_w_out, m_g_norm_ffn2, m_w_ffn2_in, m_w_ffn2_out, m_g_norm_final, m_w_ada_final, m_b_ada_final, v_w_ada, v_b_ada, v_g_norm_ffn1, v_w_ffn1_in, v_w_ffn1_out, v_g_norm_mix, v_w_in, v_w_dw, v_b_dw, v_g_conv_ln, v_b_conv_ln, v_w_gate_up, v_b_gate, v_g_gla_norm, v_w_out, v_g_norm_ffn2, v_w_ffn2_in, v_w_ffn2_out, v_g_norm_final, v_w_ada_final, v_b_ada_final):
    given = dict(locals())
    W = {n: given[n] for n in _WEIGHTS}
    M1 = {n: given["m_" + n] for n in _WEIGHTS}
    M2 = {n: given["v_" + n] for n in _WEIGHTS}
    xs = x[0]
    tgt = loss_target[0]
    S, D = xs.shape
    L = w_ada.shape[0]
    xi, yi, ci = _place()
    s_me = 2 * xi + yi
    b_me = 4 * xi + 2 * yi + ci
    nsh = w_ada.shape[2]
    nfin = w_ada_final.shape[1]
    n_cols = w_in.shape[2]
    n_pad = -(-n_cols // LANE) * LANE

    def lands_of(l):
        shards = [W[n][l].astype(BF16) for n in _BIG]
        shards[2] = jnp.pad(shards[2], ((0, 0), (0, n_pad - n_cols)))
        return [lax.dynamic_update_index_in_dim(lax.empty((N_CHIPS,) + s.shape, BF16), s, s_me, 0) for s in shards]

    lands = {l: lands_of(l) for l in range(L)}
    ag_groups = [dict(l=0, items=[0, 1], need=0), dict(l=0, items=[2, 3, 4, 5], need=1)]
    ag_groups += [dict(l=l, items=list(range(len(_BIG))), need=3 * l) for l in range(1, L)]

    def ag_start(grp):
        bufs = [lands[grp["l"]][i] for i in grp["items"]]
        return _split_start(bufs, 3 * len(bufs), _ag_starts, f"ag_start_l{grp['l']}_{grp['items'][0]}")

    pend = ag_start(ag_groups[0])
    tok = pend[3][0, 0]

    c_all = _ag_small(c.reshape(8, D // 8) + tok, "ag_c").reshape(8, D)
    tok = None
    parts = [_mod_proj(c_all, w_ada, l, lax.dynamic_slice(b_ada, (l, s_me * nsh), (1, nsh)), f"mod_proj_{l}") for l in range(L)]
    parts.append(_mod_proj(c_all, w_ada_final[None], 0, lax.dynamic_slice(b_ada_final, (s_me * nfin,), (nfin,))[None], "mod_proj_final"))
    mod_all = _ag_small(jnp.concatenate(parts, axis=1), "ag_mod")
    mine = [lax.dynamic_index_in_dim(lax.dynamic_index_in_dim(mod_all, 2 * s + ci, 0, False), b_me, 0, False) for s in range(N_CHIPS)]
    mods = [jnp.concatenate([mine[s][l * nsh:(l + 1) * nsh] for s in range(N_CHIPS)]).reshape(N_MOD, 1, D) for l in range(L)]
    fmod = jnp.concatenate([mine[s][L * nsh:] for s in range(N_CHIPS)]).reshape(2, 1, D)

    tiny = jnp.concatenate([w_dw.reshape(-1), w_gate_up.reshape(-1)])
    tiny_all = _ag_small(jnp.pad(tiny, (0, (-tiny.shape[0]) % (8 * LANE))).reshape(8, -1), "ag_tiny").reshape(8, -1)
    n_dw = w_dw.size
    dw_parts = [lax.dynamic_index_in_dim(tiny_all, 2 * s + ci, 0, False) for s in range(N_CHIPS)]
    w_dw_full = jnp.concatenate([p[:n_dw].reshape(w_dw.shape) for p in dw_parts], axis=2)
    w_gu_full = jnp.concatenate([p[n_dw:n_dw + w_gate_up.size].reshape(w_gate_up.shape) for p in dw_parts], axis=2)

    def layer_weights(l, lands):
        wi1, wo1, win4, wout4, wi2, wo2 = lands
        return dict(
            wi1=wi1, wo1=wo1.reshape(-1, D), wi2=wi2, wo2=wo2.reshape(-1, D), wout=wout4.reshape(-1, D), wmix=_mix_weight(win4, n_cols),
            wgp=jnp.pad(_pad_heads(w_gu_full[l]), ((0, LANE - GATE_RANK), (0, 0))).astype(BF16), bgp=_pad_heads(b_gate[l])[None])

    gnorm = (g_norm_ffn1, g_norm_mix, g_norm_ffn2)
    subs = [dict(l=l, j=j, tag=f"{('ffn1', 'mix', 'ffn2')[j]}_l{l}", g=gnorm[j][l][None], shift=mods[l][3 * j], scale=mods[l][3 * j + 1],
                 gv=mods[l][3 * j + 2] * (1.0 if j == 1 else 0.5)) for l in range(L) for j in range(3)]
    gi = 0
    xcur = xs
    h = None
    for k, sb in enumerate(subs):
        l, j = sb["l"], sb["j"]
        if pend is not None and ag_groups[gi]["need"] == k:
            grp = ag_groups[gi]
            nm = f"l{grp['l']}_{grp['items'][0]}"
            after = xcur if k > 0 else sb["shift"]
            done = _ag_finish(_split_wait(pend[0], pend[1], pend[2], after, _ag_waits, f"ag_wait_{nm}"), f"ag_finish_{nm}")
            for i, a in zip(grp["items"], done):
                lands[grp["l"]][i] = a
            gi += 1
            pend = ag_start(ag_groups[gi]) if gi < len(ag_groups) else None
            tok = pend[3][0, 0] if pend is not None else None
        if h is None:
            h = (sb["g"] if tok is None else sb["g"] + tok, sb["shift"], sb["scale"])
            tok = None
        d = layer_weights(l, lands[l])
        nxt = (subs[k + 1]["g"], subs[k + 1]["shift"], subs[k + 1]["scale"]) if k + 1 < len(subs) else None
        gv = sb["gv"] if tok is None else sb["gv"] + tok
        tok = None
        if j == 1:
            xcur, h, sb["y"], sb["saved"] = _mix_fwd(xcur, h, gv, d["wmix"], w_dw_full[l], b_dw[l][None], g_conv_ln[l][None],
                                                     b_conv_ln[l][None], d["wgp"], d["bgp"], g_gla_norm[l], d["wout"], nxt, sb["tag"])
        else:
            w4, wo = (d["wi1"], d["wo1"]) if j == 0 else (d["wi2"], d["wo2"])
            xcur, h, sb["y"], sb["saved"] = _ffn_fwd(xcur, h, gv, w4, wo, nxt, sb["tag"])
    lw = [layer_weights(l, lands[l]) for l in range(L)]

    c_arr = jnp.stack([ci]).astype(jnp.int32)
    sc_arr = jnp.stack([s_me, ci]).astype(jnp.int32)
    fulls = [lax.empty((L,) + ((W[n].shape[1], n_pad) if n == 'w_in' else W[n].shape[1:]), F32) for n in _BIG]

    def rs_begin(gs, items, l, after=None):
        nm = f"l{l}_{items[0]}"
        sibs = _rs_sibling(gs, f"rs_sibling_{nm}", after)
        return sibs, [_rs_presum(g, sb_, c_arr, f"rs_presum_{i}_l{l}") for i, g, sb_ in zip(items, gs, sibs)]

    def rs_end(gs, sibs, recvs, items, l):
        summed = [_rs_sum(g, sb_, rv, fulls[i], l, sc_arr, f"rs_sum_{i}_l{l}") for i, g, sb_, rv in zip(items, gs, sibs, recvs)]
        for i, f in zip(items, _rs_share(summed, l, f"rs_share_l{l}_{items[0]}")):
            fulls[i] = f

    def rs_start(gs, items, l, sibs=None, after=None):
        if sibs is None:
            sibs, ps = rs_begin(gs, items, l, after)
        else:
            ps = [_rs_presum(g, sb_, c_arr, f"rs_presum_{i}_l{l}") for i, g, sb_ in zip(items, gs, sibs)]
        pend = _split_start(ps + [lax.empty((3,) + p.shape[1:], BF16) for p in ps], 3 * len(ps), _rs_chip_copies, f"rs_start_l{l}_{items[0]}")
        return dict(gs=gs, sibs=sibs, pend=pend, items=items, l=l)

    def sib_start(gs, items, l):
        lands_ = [lax.empty((N_CHIPS, g.shape[1] // 2, g.shape[2]), F32) for g in gs]
        return dict(pend=_split_start(gs + lands_, len(gs), _rs_sibling_copies, f"rs_sibling_start_l{l}"), items=items, l=l, n=len(gs))

    def sib_finish(sp, after):
        bufs = _split_wait(sp["pend"][0], sp["pend"][1], sp["pend"][2], after, _rs_sibling_copies, f"rs_sibling_wait_l{sp['l']}")
        return rs_start(bufs[:sp["n"]], sp["items"], sp["l"], sibs=bufs[sp["n"]:])

    def rs_finish(fl, after):
        pend, n = fl["pend"], len(fl["gs"])
        bufs = _split_wait(pend[0], pend[1], pend[2], after, _rs_chip_copies, f"rs_wait_l{fl['l']}_{fl['items'][0]}")
        rs_end(fl["gs"], fl["sibs"], bufs[n:], fl["items"], fl["l"])

    sq, dx, dfsh, dfsc, dgfin, dy, dgv = _loss_head(xcur, g_norm_final[None], fmod[0], fmod[1], tgt, (subs[-1]["y"], subs[-1]["gv"]))
    loss_part = 0.5 / D * jnp.sum(sq)
    G = {n: [None] * L for n in _SMALL}
    dmods = [None] * L
    in_flight = None
    sib_flight = None
    tok = None
    for l in reversed(range(L)):
        gr = [None] * 3
        for j in reversed(range(3)):
            k = 3 * l + j
            sb, d = subs[k], lw[l]
            prev = (subs[k - 1]["y"], subs[k - 1]["gv"]) if k > 0 else None
            g_vec = sb["g"] if tok is None else sb["g"] + tok
            tok = None
            if j == 1:
                dx, pv, gr[j] = _mix_bwd(dx, dy, sb["saved"], g_vec, sb["scale"], prev, d["wmix"], w_dw_full[l], g_conv_ln[l][None], b_conv_ln[l][None],
                                         d["wgp"], d["bgp"], g_gla_norm[l], d["wout"], n_cols, n_pad, sb["tag"])
            else:
                w4, wo = (d["wi1"], d["wo1"]) if j == 0 else (d["wi2"], d["wo2"])
                dx, pv, gr[j] = _ffn_bwd(dx, dy, sb["saved"], g_vec, sb["scale"], prev, w4, wo, sb["tag"])
            gr[j]["dgv"] = dgv
            dy, dgv = pv if pv else (None, None)
            if j == 2 and sib_flight is not None:
                in_flight = sib_finish(sib_flight, dx)
                sib_flight = None
                tok = in_flight["pend"][3][0, 0]
            if j == 1 and in_flight is not None:
                rs_finish(in_flight, dx)
                in_flight = None
            if j == 1 and l == 0:
                in_flight = rs_start([gr[1]["dw_in"], gr[1]["dw_out"], gr[2]["dw_in"], gr[2]["dw_out"]], [2, 3, 4, 5], l)
                tok = in_flight["pend"][3][0, 0]
        g1, g2, g3 = gr
        if l > 0:
            sib_flight = sib_start([g1["dw_in"], g1["dw_out"], g2["dw_in"], g2["dw_out"], g3["dw_in"], g3["dw_out"]], list(range(len(_BIG))), l)
            tok = sib_flight["pend"][3][0, 0]
        else:
            rs_finish(in_flight, dx)
            in_flight = None
            last_grads = [g1["dw_in"], g1["dw_out"]]
        dmods[l] = jnp.concatenate([g1["dshift"], g1["dscale"], 0.5 * g1["dgv"], g2["dshift"], g2["dscale"], g2["dgv"],
                                    g3["dshift"], g3["dscale"], 0.5 * g3["dgv"]], axis=1)[0]
        G["g_norm_ffn1"][l], G["g_norm_ffn2"][l], G["g_norm_mix"][l] = g1["dg"][0], g3["dg"][0], g2["dg"][0]
        G["w_dw"][l], G["b_dw"][l], G["g_conv_ln"][l], G["b_conv_ln"][l] = g2["dw_dw"], g2["db_dw"][0], g2["dg_ln"][0], g2["db_ln"][0]
        G["w_gate_up"][l], G["b_gate"][l], G["g_gla_norm"][l] = g2["dw_gate"], g2["db_gate"], g2["dgn"]
    grad_x = dx[None]
    gsm = {}

    small = [jnp.stack(G[n]).reshape(-1) for n in _SMALL if n != 'g_norm_final'] + [dgfin[0]]
    dmod_vec = jnp.concatenate(dmods + [dfsh[0], dfsc[0]])
    n_mod_vec = dmod_vec.shape[0]
    vec = jnp.concatenate([dmod_vec] + small + [loss_part[None]])
    n_vec = vec.shape[0]
    vec = jnp.pad(vec, (0, (-n_vec) % (8 * LANE)))
    vec_all = _ag_small(vec.reshape(8, -1), "ag_small_grads").reshape(8, -1)
    last = rs_start(last_grads, [0, 1], 0, after=vec_all)
    vec_sum = _rowsum(vec_all + last["pend"][3][0, 0], "sum_small_grads")[0]
    loss = vec_sum[n_vec - 1]
    off = n_mod_vec
    for n in _SMALL:
        shp = {'w_dw': w_dw_full.shape, 'w_gate_up': w_gu_full.shape}.get(n, W[n].shape)
        cnt = 1
        for dd in shp:
            cnt *= dd
        gsm[n] = vec_sum[off:off + cnt].reshape(shp)
        off += cnt
    gsm['w_dw'] = lax.dynamic_slice_in_dim(gsm['w_dw'], s_me * w_dw.shape[2], w_dw.shape[2], 2)
    gsm['w_gate_up'] = lax.dynamic_slice_in_dim(gsm['w_gate_up'], s_me * w_gate_up.shape[2], w_gate_up.shape[2], 2)
    dmod_sum = vec_sum[:n_mod_vec]
    gsm['b_ada'] = dmod_sum[:L * N_MOD * D].reshape(L, N_MOD * D)
    gsm['b_ada_final'] = dmod_sum[L * N_MOD * D:]
    c_t = c_all.T
    dmod_rows = vec_all[:, :n_mod_vec]
    gsm['w_ada'] = jnp.stack([
        _mod_wgrad(c_t, lax.dynamic_slice_in_dim(dmod_rows, l * N_MOD * D + s_me * nsh, nsh, 1), f"dw_ada_{l}") for l in range(L)])
    gsm['w_ada_final'] = _mod_wgrad(c_t, lax.dynamic_slice_in_dim(dmod_rows, L * N_MOD * D + s_me * nfin, nfin, 1), "dw_ada_final")
    gsm.update({n: (f[:, :, :n_cols] if n == 'w_in' else f) for n, f in zip(_BIG[2:], fulls[2:])})

    outs = {}
    small_names = [n for n in _WEIGHTS if W[n].size < 65536]
    for n in [m for m in _WEIGHTS if m not in _BIG[:2]] + list(_BIG[:2]):
        if n in small_names:
            continue
        if n == _BIG[0]:
            rs_finish(last, outs['w_ada_final'][0])
            gsm.update(dict(zip(_BIG[:2], fulls[:2])))
        shp = W[n].shape
        v2 = lambda a: a.reshape(-1, shp[-1])
        from_rs = n in _BIG and n != 'w_in'
        d_, m_, v_, *g_ = _adamw(v2(W[n]), v2(gsm[n]), v2(M1[n]), v2(M2[n]), f"adamw_{n}", copy_grad=from_rs)
        outs[n] = (d_.reshape(shp), m_.reshape(shp), v_.reshape(shp))
        if from_rs:
            gsm[n] = g_[0].reshape(shp)
    flat = lambda dct: jnp.concatenate([dct[n].reshape(-1) for n in small_names])
    n_small = sum(W[n].size for n in small_names)
    v2 = lambda a: jnp.pad(a, (0, (-n_small) % (8 * LANE))).reshape(-1, LANE)
    d_, m_, v_ = _adamw(v2(flat(W)), v2(flat(gsm)), v2(flat(M1)), v2(flat(M2)), "adamw_small")

    def unflat(a):
        res, o = {}, 0
        a = a.reshape(-1)
        for n in small_names:
            res[n] = a[o:o + W[n].size].reshape(W[n].shape)
            o += W[n].size
        return res

    for n, dd, mm, vv in zip(small_names, unflat(d_).values(), unflat(m_).values(), unflat(v_).values()):
        outs[n] = (dd, mm, vv)

    return (loss, grad_x, *[gsm[n] for n in _WEIGHTS], *[outs[n][0] for n in _WEIGHTS], *[outs[n][1] for n in _WEIGHTS], *[outs[n][2] for n in _WEIGHTS])
```

```python
import jax
import jax.numpy as jnp
from jax import lax
from jax.experimental import pallas as pl
from jax.experimental.pallas import tpu as pltpu

F32 = jnp.float32
BF16 = jnp.bfloat16

CHUNK = 64
HEADS = 4
DK = 64
DV = 128
DKP = 128
GATE_RANK = 16
GATE_TAU = 16.0
N_MOD = 9
EPS = 1e-6
ADAM_LR = 0.001
ADAM_B1 = 0.9
ADAM_B2 = 0.999
ADAM_EPS = 1e-08
ADAM_WD = 0.01
ADAM_STEP = 10

LANE = 128
HALO = 32
VMEM_LIMIT = 52 * 1024 * 1024
MESH = pl.DeviceIdType.MESH
N_CHIPS = 4

D_CONV = 512
D_GLA = HEADS * DV
ZC_Q = 0
ZC_K = ZC_Q + HEADS * DKP
ZC_V = ZC_K + HEADS * DKP
ZC_R = ZC_V + D_GLA
ZC_A = ZC_R + D_GLA
ZC_B = ZC_A + D_CONV
ZC_G = ZC_B + D_CONV
Z_COLS = ZC_G + LANE
Z_GLA = ZC_A


def _div(n, target, mult):
    best = None
    d = mult
    while d <= min(n, target):
        if n % d == 0:
            best = d
        d += mult
    return n if best is None else best


def _cp(sem=None, **kw):
    return pltpu.CompilerParams(dimension_semantics=sem, vmem_limit_bytes=VMEM_LIMIT, **kw)


def _resident(shape, index_map):
    return pl.BlockSpec(shape, index_map, pipeline_mode=pl.Buffered(1))


def _sigmoid(x):
    return 0.5 * jnp.tanh(0.5 * x) + 0.5


def _dot(a, b):
    return jnp.dot(a.astype(BF16), b.astype(BF16), preferred_element_type=F32)


def _dot_nt(a, b):
    return lax.dot_general(a.astype(BF16), b.astype(BF16), (((1,), (1,)), ((), ())), preferred_element_type=F32)


def _dot_tn(a, b):
    return lax.dot_general(a.astype(BF16), b.astype(BF16), (((0,), (0,)), ((), ())), preferred_element_type=F32)


def _dot_exact(a, b):
    return jnp.dot(a, b, preferred_element_type=F32, precision=lax.Precision.HIGHEST)


def _norm_rows(xv, g, shift, scale):
    r = lax.rsqrt(jnp.mean(xv * xv, axis=-1, keepdims=True) + EPS)
    return (xv * r) * g * (1.0 + scale) + shift


def _loss_head(x, g, shift, scale, tgt, prev):
    S, D = x.shape
    tm = _div(S, 512, 8)

    def body(x_ref, g_ref, sh_ref, sc_ref, t_ref, y_ref, gvp_ref, sq_ref, dx_ref, dsh_ref, dsc_ref, dg_ref, dy_ref, dgv_ref):
        sums = (sq_ref, dsh_ref, dsc_ref, dg_ref, dgv_ref)

        @pl.when(pl.program_id(0) == 0)
        def _():
            for o in sums:
                o[...] = jnp.zeros_like(o)

        xv = x_ref[...]
        e = _norm_rows(xv, g_ref[...], sh_ref[...], sc_ref[...]) - t_ref[...]
        dx, dsh, dsc, dg, dy, dgv = _normmod_bwd_rows(xv, e * (1.0 / D), None, g_ref[...], sc_ref[...], (y_ref[...], gvp_ref[...]))
        dx_ref[...] = dx
        dy_ref[...] = dy
        for o, v in zip(sums, (jnp.sum(e * e, axis=0, keepdims=True), dsh, dsc, dg, dgv)):
            o[...] += v

    row = pl.BlockSpec((tm, D), lambda i: (i, 0))
    vec = pl.BlockSpec((1, D), lambda i: (0, 0))
    vs = jax.ShapeDtypeStruct((1, D), F32)
    return pl.pallas_call(
        body, grid=(S // tm,), in_specs=[row, vec, vec, vec, row, row, vec], out_specs=[vec, row, vec, vec, vec, row, vec],
        out_shape=[vs, jax.ShapeDtypeStruct((S, D), F32), vs, vs, vs, jax.ShapeDtypeStruct((S, D), BF16), vs],
        compiler_params=_cp(("arbitrary",)), name="loss_head",
    )(x, g, shift, scale, tgt, *prev)


def _normmod_bwd_rows(xv, dh, dres, gv, sc, prev):
    r = lax.rsqrt(jnp.mean(xv * xv, axis=-1, keepdims=True) + EPS)
    xh = xv * r
    dsh = jnp.sum(dh, axis=0, keepdims=True)
    dsc = jnp.sum(dh * (xh * gv), axis=0, keepdims=True)
    dn = dh * (1.0 + sc)
    dg = jnp.sum(dn * xh, axis=0, keepdims=True)
    dxh = dn * gv
    dx = r * (dxh - xh * jnp.mean(dxh * xh, axis=-1, keepdims=True))
    if dres is not None:
        dx = dx + dres
    if prev is None:
        return dx, dsh, dsc, dg
    y, gvp = prev
    return dx, dsh, dsc, dg, (gvp * dx).astype(BF16), jnp.sum(dx * y, axis=0, keepdims=True)


def _mm(pairs, out_dtype, name, nt=False):
    M = pairs[0][0].shape[0]
    N = pairs[0][1].shape[0] if nt else pairs[0][1].shape[1]
    ktot = sum(a.shape[1] for a, _, _ in pairs)
    tm = _div(M, 512 if ktot <= 4096 else 256, 8)
    n = len(pairs)

    def body(*refs):
        o_ref = refs[2 * n]
        dot = _dot_nt if nt else _dot
        acc = dot(refs[0][...], refs[1][...])
        for p in range(1, n):
            acc = acc + dot(refs[2 * p][...], refs[2 * p + 1][...])
        o_ref[...] = acc.astype(o_ref.dtype)

    ins, args = [], []
    for a, b, blk in pairs:
        k = a.shape[1]
        ins.append(pl.BlockSpec((tm, k), lambda i: (i, 0)))
        ins.append(_resident((N, k), lambda i, blk=blk: (0, blk)) if nt else _resident((k, N), lambda i: (0, 0)))
        args += [a, b]
    return pl.pallas_call(
        body, grid=(M // tm,), in_specs=ins, out_specs=pl.BlockSpec((tm, N), lambda i: (i, 0)),
        out_shape=jax.ShapeDtypeStruct((M, N), out_dtype), compiler_params=_cp(("parallel",)), name=name,
    )(*args)


TN_ROWS = 2048


def _mm_tn(a, g, name):
    S, Ka = a.shape
    N = g.shape[1]
    tk = _div(Ka, 1408, LANE)
    tn = _div(N, 1408, LANE)
    ts = _div(S, TN_ROWS, 8)

    def body(a_ref, g_ref, o_ref):
        @pl.when(pl.program_id(2) == 0)
        def _():
            o_ref[...] = jnp.zeros_like(o_ref)

        o_ref[...] += _dot_tn(a_ref[...], g_ref[...])

    return pl.pallas_call(
        body, grid=(Ka // tk, N // tn, S // ts),
        in_specs=[pl.BlockSpec((ts, tk), lambda i, j, s: (s, i)), pl.BlockSpec((ts, tn), lambda i, j, s: (s, j))],
        out_specs=pl.BlockSpec((tk, tn), lambda i, j, s: (i, j)),
        out_shape=jax.ShapeDtypeStruct((Ka, N), F32),
        compiler_params=_cp(("parallel", "parallel", "arbitrary")), name=name,
    )(a, g)


def _mm_tn_two(a0, a1, g, name):
    S, K = a0.shape
    N = g.shape[1]
    ts = _div(S, TN_ROWS, 8)

    def body(a0_ref, a1_ref, g_ref, o_ref):
        i = pl.program_id(0)

        @pl.when(pl.program_id(1) == 0)
        def _():
            o_ref[...] = jnp.zeros_like(o_ref)

        @pl.when(i == 0)
        def _():
            o_ref[...] += _dot_tn(a0_ref[...], g_ref[...])

        @pl.when(i == 1)
        def _():
            o_ref[...] += _dot_tn(a1_ref[...], g_ref[...])

    return pl.pallas_call(
        body, grid=(2, S // ts),
        in_specs=[pl.BlockSpec((ts, K), lambda i, s: (jnp.where(i == 0, s, 0), 0)),
                  pl.BlockSpec((ts, K), lambda i, s: (jnp.where(i == 1, s, 0), 0)),
                  pl.BlockSpec((ts, N), lambda i, s: (s, 0))],
        out_specs=pl.BlockSpec((K, N), lambda i, s: (i, 0)),
        out_shape=jax.ShapeDtypeStruct((2 * K, N), F32),
        compiler_params=_cp(("parallel", "arbitrary")), name=name,
    )(a0, a1, g)


def _swiglu(gt, up):
    return gt * _sigmoid(gt) * up


def _ffn_up(h, w4, name, norm=None):
    S, D = h.shape
    ns, _, C = w4.shape
    hs = ns // 2
    tm = _div(S, 256, 8)
    nn = 3 if norm else 0

    def body(h_ref, w_ref, *rest):
        z_ref, a_ref = rest[nn:nn + 2]
        if norm:
            hv = _norm_rows(h_ref[...], rest[0][...], rest[1][...], rest[2][...]).astype(BF16)
            rest[nn + 2][...] = hv
        else:
            hv = h_ref[...]
        for s in range(hs):
            gt = _dot(hv, w_ref[s])
            up = _dot(hv, w_ref[hs + s])
            sg = _sigmoid(gt)
            silu = gt * sg
            z_ref[:, s * C:(s + 1) * C] = (up * (sg * (1.0 + gt * (1.0 - sg)))).astype(BF16)
            z_ref[:, (hs + s) * C:(hs + s + 1) * C] = silu.astype(BF16)
            a_ref[:, s * C:(s + 1) * C] = (silu * up).astype(BF16)

    row = pl.BlockSpec((tm, D), lambda i: (i, 0))
    vec = pl.BlockSpec((1, D), lambda i: (0, 0))
    return pl.pallas_call(
        body, grid=(S // tm,), in_specs=[row, _resident((ns, D, C), lambda i: (0, 0, 0))] + [vec] * nn,
        out_specs=[pl.BlockSpec((tm, ns * C), lambda i: (i, 0)), pl.BlockSpec((tm, hs * C), lambda i: (i, 0))] + [row] * (nn // 3),
        out_shape=[jax.ShapeDtypeStruct((S, ns * C), BF16), jax.ShapeDtypeStruct((S, hs * C), BF16)] + [jax.ShapeDtypeStruct((S, D), BF16)] * (nn // 3),
        compiler_params=_cp(("parallel",)), name=name,
    )(h, w4, *(norm or ()))


def _resid_outputs(y, x_ref, gv_ref, nxt_refs, out_refs):
    out_refs[0][...] = y.astype(BF16)
    xn = x_ref[...] + gv_ref[...] * y
    out_refs[1][...] = xn
    if nxt_refs:
        out_refs[2][...] = _norm_rows(xn, nxt_refs[0][...], nxt_refs[1][...], nxt_refs[2][...]).astype(BF16)


def _ffn_down(act, wo, x, gv, nxt, name):
    S = act.shape[0]
    Fd, D = wo.shape
    tm = _div(S, 512, 8)
    nn = 3 if nxt else 0

    def body(a_ref, w_ref, x_ref, gv_ref, *rest):
        _resid_outputs(_dot(a_ref[...], w_ref[...]), x_ref, gv_ref, rest[:nn], rest[nn:])

    row = pl.BlockSpec((tm, D), lambda i: (i, 0))
    vec = pl.BlockSpec((1, D), lambda i: (0, 0))
    os_ = jax.ShapeDtypeStruct((S, D), F32)
    return pl.pallas_call(
        body, grid=(S // tm,),
        in_specs=[pl.BlockSpec((tm, Fd), lambda i: (i, 0)), _resident((Fd, D), lambda i: (0, 0)), row, vec] + [vec] * nn,
        out_specs=[row, row] + [row] * (nn // 3), out_shape=[jax.ShapeDtypeStruct((S, D), BF16), os_] + [jax.ShapeDtypeStruct((S, D), BF16)] * (nn // 3),
        compiler_params=_cp(("parallel",)), name=name,
    )(act, wo, x, gv, *(nxt or ()))


def _dw_ffn_in(h, dz, ns, name):
    S, D = h.shape
    C = dz.shape[1] // ns
    ts = _div(S, TN_ROWS, 8)

    def body(h_ref, g_ref, o_ref):
        @pl.when(pl.program_id(1) == 0)
        def _():
            o_ref[...] = jnp.zeros_like(o_ref)

        o_ref[...] += _dot_tn(h_ref[...], g_ref[...])

    return pl.pallas_call(
        body, grid=(ns, S // ts),
        in_specs=[pl.BlockSpec((ts, D), lambda j, s: (s, 0)), pl.BlockSpec((ts, C), lambda j, s: (s, j))],
        out_specs=pl.BlockSpec((None, D, C), lambda j, s: (j, 0, 0)), out_shape=jax.ShapeDtypeStruct((ns, D, C), F32),
        compiler_params=_cp(("parallel", "arbitrary")), name=name,
    )(h, dz)


def _dh_normmod_bwd(pairs, x, dres, g, scale, prev, name):
    S, D = x.shape
    tm = _div(S, 256, 8)
    n = len(pairs)
    with_prev = prev is not None

    def body(*refs):
        refs = list(refs)
        mm = refs[:2 * n]
        x_ref, dr_ref, g_ref, sc_ref = refs[2 * n:2 * n + 4]
        outs = refs[2 * n + 4 + 2 * with_prev:]

        @pl.when(pl.program_id(0) == 0)
        def _():
            for o in outs[1:4] + outs[5:]:
                o[...] = jnp.zeros_like(o)

        dh = _dot_nt(mm[0][...], mm[1][...])
        for p in range(1, n):
            dh = dh + _dot_nt(mm[2 * p][...], mm[2 * p + 1][...])
        pv = (refs[2 * n + 4][...], refs[2 * n + 5][...]) if with_prev else None
        res = _normmod_bwd_rows(x_ref[...], dh, dr_ref[...], g_ref[...], sc_ref[...], pv)
        outs[0][...] = res[0]
        for o, v in zip(outs[1:4], res[1:4]):
            o[...] += v
        if with_prev:
            outs[4][...] = res[4]
            outs[5][...] += res[5]

    row = pl.BlockSpec((tm, D), lambda i: (i, 0))
    vec = pl.BlockSpec((1, D), lambda i: (0, 0))
    ins, args = [], []
    for a, b, a_blk, b_blk, k in pairs:
        ins.append(pl.BlockSpec((tm, k), lambda i, a_blk=a_blk: (i, a_blk)))
        ins.append(_resident((None, D, k), lambda i, b_blk=b_blk: (b_blk, 0, 0)) if b.ndim == 3 else _resident((D, k), lambda i, b_blk=b_blk: (0, b_blk)))
        args += [a, b]
    ins += [row, row, vec, vec] + [row, vec] * with_prev
    args += [x, dres, g, scale] + (list(prev) if with_prev else [])
    vs = jax.ShapeDtypeStruct((1, D), F32)
    return pl.pallas_call(
        body, grid=(S // tm,), in_specs=ins, out_specs=[row, vec, vec, vec] + [row, vec] * with_prev,
        out_shape=[jax.ShapeDtypeStruct((S, D), F32), vs, vs, vs] + [jax.ShapeDtypeStruct((S, D), BF16), vs] * with_prev,
        compiler_params=_cp(("arbitrary",)), name=name,
    )(*args)


def _ffn_bwd_core(dy, wo, z, w4, x, dres, g, scale, prev, name):
    S, D = x.shape
    Fd = wo.shape[0]
    ns, _, C = w4.shape
    tm = _div(S, 256, 8)
    with_prev = prev is not None

    def body(dy_ref, wo_ref, q_ref, p_ref, w4_ref, x_ref, dr_ref, g_ref, sc_ref, *rest):
        outs = rest[2 * with_prev:]
        dz_ref, outs = outs[0], outs[1:]

        @pl.when(pl.program_id(0) == 0)
        def _():
            for o in outs[1:4] + outs[5:]:
                o[...] = jnp.zeros_like(o)

        da = _dot_nt(dy_ref[...], wo_ref[...])
        dz_ref[:, :Fd] = (da * q_ref[...].astype(F32)).astype(BF16)
        dz_ref[:, Fd:] = (da * p_ref[...].astype(F32)).astype(BF16)
        dh = _dot_nt(dz_ref[:, 0:C], w4_ref[0])
        for s in range(1, ns):
            dh = dh + _dot_nt(dz_ref[:, s * C:(s + 1) * C], w4_ref[s])
        pv = (rest[0][...], rest[1][...]) if with_prev else None
        res = _normmod_bwd_rows(x_ref[...], dh, dr_ref[...], g_ref[...], sc_ref[...], pv)
        outs[0][...] = res[0]
        for o, v in zip(outs[1:4], res[1:4]):
            o[...] += v
        if with_prev:
            outs[4][...] = res[4]
            outs[5][...] += res[5]

    row = pl.BlockSpec((tm, D), lambda i: (i, 0))
    vec = pl.BlockSpec((1, D), lambda i: (0, 0))
    wide = pl.BlockSpec((tm, 2 * Fd), lambda i: (i, 0))
    vs = jax.ShapeDtypeStruct((1, D), F32)
    return pl.pallas_call(
        body, grid=(S // tm,),
        in_specs=[row, _resident((Fd, D), lambda i: (0, 0)), pl.BlockSpec((tm, Fd), lambda i: (i, 0)), pl.BlockSpec((tm, Fd), lambda i: (i, 1)),
                  _resident((ns, D, C), lambda i: (0, 0, 0)), row, row, vec, vec] + [row, vec] * with_prev,
        out_specs=[wide, row, vec, vec, vec] + [row, vec] * with_prev,
        out_shape=[jax.ShapeDtypeStruct((S, 2 * Fd), BF16), jax.ShapeDtypeStruct((S, D), F32), vs, vs, vs] + [jax.ShapeDtypeStruct((S, D), BF16), vs] * with_prev,
        compiler_params=_cp(("arbitrary",)), name=name,
    )(dy, wo, z, z, w4, x, dres, g, scale, *(prev or ()))


def _mix_out(yconv, ygla, wout, x, gv, nxt, name):
    S, Kc = yconv.shape
    Kg = ygla.shape[1]
    D = wout.shape[1]
    tm = _div(S, 512, 8)
    nn = 3 if nxt else 0

    def body(a_ref, b_ref, w_ref, x_ref, gv_ref, *rest):
        y = _dot(a_ref[...], w_ref[0:Kc, :]) + _dot(b_ref[...], w_ref[Kc:Kc + Kg, :])
        _resid_outputs(y, x_ref, gv_ref, rest[:nn], rest[nn:])

    row = pl.BlockSpec((tm, D), lambda i: (i, 0))
    vec = pl.BlockSpec((1, D), lambda i: (0, 0))
    os_ = jax.ShapeDtypeStruct((S, D), F32)
    return pl.pallas_call(
        body, grid=(S // tm,),
        in_specs=[pl.BlockSpec((tm, Kc), lambda i: (i, 0)), pl.BlockSpec((tm, Kg), lambda i: (i, 0)), _resident((Kc + Kg, D), lambda i: (0, 0)), row,
                  vec] + [vec] * nn,
        out_specs=[row, row] + [row] * (nn // 3), out_shape=[jax.ShapeDtypeStruct((S, D), BF16), os_] + [jax.ShapeDtypeStruct((S, D), BF16)] * (nn // 3),
        compiler_params=_cp(("parallel",)), name=name,
    )(yconv, ygla, wout, x, gv, *(nxt or ()))


def _ln_parts(yc, g, b):
    mu = jnp.mean(yc, axis=-1, keepdims=True)
    xc = yc - mu
    rs = lax.rsqrt(jnp.mean(xc * xc, axis=-1, keepdims=True) + EPS)
    xh = xc * rs
    return xh, rs, xh * g + b


SUB = 8
CONV_ROWS = 32


def _shifted_copies(ext8, rows):
    for b in range(1, SUB):
        ext8[b, pl.ds(0, rows - SUB), :] = ext8[0, pl.ds(b, rows - SUB), :]


def _tap(o):
    return o % SUB, o - o % SUB


def _conv_fwd(z, w_dw, b_dw, g_ln, b_ln, name):
    S = z.shape[0]
    W, C = w_dw.shape
    ts = _div(S, 512, HALO)
    hb = ts // HALO
    off = HALO - (W - 1)
    ca, cb = ZC_A // C, ZC_B // C
    rb = 2 * CONV_ROWS

    def body(a_ref, b_ref, pa_ref, pb_ref, w_ref, bd_ref, g_ref, bl_ref, u_ref, yc_ref, o_ref, ext8):
        keep = (pl.program_id(0) > 0).astype(F32)
        u = a_ref[...] * _sigmoid(b_ref[...])
        ext8[0, pl.ds(0, HALO), :] = pa_ref[...] * _sigmoid(pb_ref[...]) * keep
        ext8[0, pl.ds(HALO, ts), :] = u
        u_ref[...] = u
        _shifted_copies(ext8, ts + HALO)

        for lg in range(C // LANE):
            lanes = pl.ds(lg * LANE, LANE)
            taps = [jnp.broadcast_to(w_ref[pl.ds(j, 1), lanes], (SUB, LANE)) for j in range(W)]
            bias = jnp.broadcast_to(bd_ref[:, lanes], (SUB, LANE))

            def sub(i, carry, lanes=lanes, taps=taps, bias=bias):
                r0 = pl.multiple_of(i * rb, rb)
                accs = [bias] * (rb // SUB)
                for j in range(W):
                    b, a = _tap(off + j)
                    for r in range(rb // SUB):
                        accs[r] = accs[r] + taps[j] * ext8[b, pl.ds(r0 + a + r * SUB, SUB), lanes]
                for r in range(rb // SUB):
                    yc_ref[pl.ds(r0 + r * SUB, SUB), lanes] = accs[r]
                return carry

            lax.fori_loop(0, ts // rb, sub, 0)
        _, _, ln = _ln_parts(yc_ref[...], g_ref[...], bl_ref[...])
        o_ref[...] = (ln * _sigmoid(ln)).astype(BF16)

    cur = lambda col: pl.BlockSpec((ts, C), lambda i: (i, col))
    prev = lambda col: pl.BlockSpec((HALO, C), lambda i: (jnp.maximum(i * hb - 1, 0), col))
    vec = pl.BlockSpec((1, C), lambda i: (0, 0))
    row = pl.BlockSpec((ts, C), lambda i: (i, 0))
    fs = jax.ShapeDtypeStruct((S, C), F32)
    return pl.pallas_call(
        body, grid=(S // ts,),
        in_specs=[cur(ca), cur(cb), prev(ca), prev(cb), pl.BlockSpec((W, C), lambda i: (0, 0)), vec, vec, vec],
        out_specs=[row, row, row], out_shape=[fs, fs, jax.ShapeDtypeStruct((S, C), BF16)],
        scratch_shapes=[pltpu.VMEM((SUB, ts + HALO, C), F32)],
        compiler_params=_cp(("parallel",)), name=name,
    )(z, z, z, z, w_dw, b_dw, g_ln, b_ln)


def _conv_bwd(dycat, z, u, yc, w_dw, g_ln, b_ln, name):
    S = z.shape[0]
    W, C = w_dw.shape
    ts = _div(S, 512, HALO)
    hb = ts // HALO
    nblk = S // ts
    off = HALO - (W - 1)
    ca, cb = ZC_A // C, ZC_B // C
    rb = CONV_ROWS

    def ln_silu_bwd(dy, ycv, g, b):
        xh, rs, ln = _ln_parts(ycv, g, b)
        sl = _sigmoid(ln)
        dln = dy * (sl * (1.0 + ln * (1.0 - sl)))
        dxh = dln * g
        dyc = rs * (dxh - jnp.mean(dxh, axis=-1, keepdims=True) - xh * jnp.mean(dxh * xh, axis=-1, keepdims=True))
        return dyc, dln, xh

    def body(dy_ref, ndy_ref, yc_ref, nyc_ref, u_ref, pu_ref, a_ref, b_ref, w_ref, g_ref, bl_ref,
             dab_ref, dw_ref, dbd_ref, dg_ref, dbl_ref, uext8, dext8, dwacc):
        i = pl.program_id(0)

        @pl.when(i == 0)
        def _():
            dwacc[...] = jnp.zeros_like(dwacc)
            dbd_ref[...] = jnp.zeros_like(dbd_ref)
            dg_ref[...] = jnp.zeros_like(dg_ref)
            dbl_ref[...] = jnp.zeros_like(dbl_ref)

        g = g_ref[...]
        bl = bl_ref[...]
        dyc, dln, xh = ln_silu_bwd(dy_ref[...], yc_ref[...], g, bl)
        ndyc, _, _ = ln_silu_bwd(ndy_ref[...], nyc_ref[...], g, bl)
        dg_ref[...] += jnp.sum(dln * xh, axis=0, keepdims=True)
        dbl_ref[...] += jnp.sum(dln, axis=0, keepdims=True)
        dbd_ref[...] += jnp.sum(dyc, axis=0, keepdims=True)
        dext8[0, pl.ds(0, ts), :] = dyc
        dext8[0, pl.ds(ts, HALO), :] = ndyc * (i < nblk - 1).astype(F32)
        uext8[0, pl.ds(0, HALO), :] = pu_ref[...] * (i > 0).astype(F32)
        uext8[0, pl.ds(HALO, ts), :] = u_ref[...]
        _shifted_copies(dext8, ts + HALO)
        _shifted_copies(uext8, ts + HALO)

        def sub(k, carry):
            r0 = pl.multiple_of(k * rb, rb)
            rows = pl.ds(r0, rb)
            dyt = dext8[0, rows, :]
            du = jnp.zeros((rb, C), F32)
            for j in range(W):
                b, a = _tap(W - 1 - j)
                du = du + w_ref[pl.ds(j, 1), :] * dext8[b, pl.ds(r0 + a, rb), :]
                b, a = _tap(off + j)
                p = dyt * uext8[b, pl.ds(r0 + a, rb), :]
                part = p[0:SUB]
                for q in range(1, rb // SUB):
                    part = part + p[q * SUB:(q + 1) * SUB]
                dwacc[j] += part
            sb = _sigmoid(b_ref[rows, :])
            dab_ref[rows, 0:C] = (du * sb).astype(BF16)
            dab_ref[rows, C:2 * C] = (du * a_ref[rows, :] * sb * (1.0 - sb)).astype(BF16)
            return carry

        lax.fori_loop(0, ts // rb, sub, 0)

        @pl.when(i == nblk - 1)
        def _():
            for j in range(W):
                dw_ref[pl.ds(j, 1), :] = jnp.sum(dwacc[j], axis=0, keepdims=True)

    row = pl.BlockSpec((ts, C), lambda i: (i, 0))
    nxt = pl.BlockSpec((HALO, C), lambda i: (jnp.minimum((i + 1) * hb, S // HALO - 1), 0))
    prv = pl.BlockSpec((HALO, C), lambda i: (jnp.maximum(i * hb - 1, 0), 0))
    vec = pl.BlockSpec((1, C), lambda i: (0, 0))
    wsp = pl.BlockSpec((W, C), lambda i: (0, 0))
    vs = jax.ShapeDtypeStruct((1, C), F32)
    return pl.pallas_call(
        body, grid=(nblk,),
        in_specs=[row, nxt, row, nxt, row, prv, pl.BlockSpec((ts, C), lambda i: (i, ca)), pl.BlockSpec((ts, C), lambda i: (i, cb)), wsp, vec, vec],
        out_specs=[pl.BlockSpec((ts, 2 * C), lambda i: (i, 0)), wsp, vec, vec, vec],
        out_shape=[jax.ShapeDtypeStruct((S, 2 * C), BF16), jax.ShapeDtypeStruct((W, C), F32), vs, vs, vs],
        scratch_shapes=[pltpu.VMEM((SUB, ts + HALO, C), F32), pltpu.VMEM((SUB, ts + HALO, C), F32), pltpu.VMEM((W, SUB, C), F32)],
        compiler_params=_cp(("arbitrary",)), name=name,
    )(dycat, dycat, yc, yc, u, u, z, z, w_dw, g_ln, b_ln)


def _log_gate(zg):
    return (jnp.minimum(zg, 0.0) - jnp.log(1.0 + jnp.exp(-jnp.abs(zg)))) * (1.0 / GATE_TAU)


def _loggate(z, wgp, bgp, name):
    S = z.shape[0]
    N = wgp.shape[1]
    ts = _div(S, 512, 8)

    def body(g_ref, w_ref, b_ref, o_ref):
        o_ref[...] = _log_gate(_dot(g_ref[...], w_ref[...]) + b_ref[...])

    return pl.pallas_call(
        body, grid=(S // ts,),
        in_specs=[pl.BlockSpec((ts, LANE), lambda i: (i, ZC_G // LANE)), pl.BlockSpec((LANE, N), lambda i: (0, 0)), pl.BlockSpec((1, N), lambda i: (0, 0))],
        out_specs=pl.BlockSpec((ts, N), lambda i: (i, 0)), out_shape=jax.ShapeDtypeStruct((S, N), F32),
        compiler_params=_cp(("parallel",)), name=name,
    )(z, wgp, bgp)


def _loggate_bwd(dla, z, wgp, wgp_t, bgp, name):
    S = z.shape[0]
    N = wgp.shape[1]
    ts = _div(S, 512, 8)

    def body(dla_ref, g_ref, w_ref, wt_ref, b_ref, dg_ref, dw_ref, db_ref):
        @pl.when(pl.program_id(0) == 0)
        def _():
            dw_ref[...] = jnp.zeros_like(dw_ref)
            db_ref[...] = jnp.zeros_like(db_ref)

        glr = g_ref[...]
        zg = _dot(glr, w_ref[...]) + b_ref[...]
        dzg = dla_ref[...] * (1.0 / GATE_TAU) * (1.0 - _sigmoid(zg))
        dg_ref[...] = _dot(dzg, wt_ref[...]).astype(BF16)
        dw_ref[...] += _dot_tn(glr, dzg)
        db_ref[...] += jnp.sum(dzg, axis=0, keepdims=True)

    return pl.pallas_call(
        body, grid=(S // ts,),
        in_specs=[pl.BlockSpec((ts, N), lambda i: (i, 0)), pl.BlockSpec((ts, LANE), lambda i: (i, ZC_G // LANE)),
                  pl.BlockSpec((LANE, N), lambda i: (0, 0)), pl.BlockSpec((N, LANE), lambda i: (0, 0)), pl.BlockSpec((1, N), lambda i: (0, 0))],
        out_specs=[pl.BlockSpec((ts, LANE), lambda i: (i, 0)), pl.BlockSpec((LANE, N), lambda i: (0, 0)), pl.BlockSpec((1, N), lambda i: (0, 0))],
        out_shape=[jax.ShapeDtypeStruct((S, LANE), BF16), jax.ShapeDtypeStruct((LANE, N), F32), jax.ShapeDtypeStruct((1, N), F32)],
        compiler_params=_cp(("arbitrary",)), name=name,
    )(dla, z, wgp, wgp_t, bgp)


def _bdot(a, b, ca, cb):
    return lax.dot_general(a.astype(BF16), b.astype(BF16), (((ca,), (cb,)), ((0,), (0,))), preferred_element_type=F32)


def _bdot_exact(a, b):
    return lax.dot_general(a, b, (((2,), (1,)), ((0,), (0,))), preferred_element_type=F32, precision=lax.Precision.HIGHEST)


def _tiles(ref, cpb):
    return jnp.stack([ref[pl.ds(c * CHUNK, CHUNK), pl.ds(h * LANE, LANE)] for c in range(cpb) for h in range(HEADS)])


def _tri_masks(n):
    ri = lax.broadcasted_iota(jnp.int32, (n, CHUNK, CHUNK), 1)
    ci = lax.broadcasted_iota(jnp.int32, (n, CHUNK, CHUNK), 2)
    return ri >= ci, (ri >= ci).astype(F32), (ri <= ci).astype(F32)


def _chunk_fwd_terms(q, k, la, tril):
    bc = _bdot_exact(tril, la)
    bend = jnp.sum(la, axis=1, keepdims=True)
    eb = jnp.exp(bc)
    enb = jnp.exp(-bc)
    ee = jnp.exp(bend - bc)
    qs = q * (DK ** -0.5)
    return bend, eb, enb, ee, qs * eb, qs * enb, k * enb, k * eb, k * ee


def _gla_fwd(z, la, gn, name):
    S = z.shape[0]
    W = HEADS * LANE
    tb = _div(S, 512, CHUNK)
    cpb = tb // CHUNK

    def body(q_ref, k_ref, v_ref, r_ref, la_ref, gn_ref, o_ref, sp_ref, y_ref, st):
        @pl.when(pl.program_id(0) == 0)
        def _():
            st[...] = jnp.zeros_like(st)

        tri, tril, _ = _tri_masks(cpb * HEADS)
        q, k, v, rv, lav = (_tiles(r, cpb) for r in (q_ref, k_ref, v_ref, r_ref, la_ref))
        bend, _, _, _, qf, qb, kb, kf, ke = _chunk_fwd_terms(q, k, lav, tril)
        att = jnp.where(tri, _bdot(qf, kb, 2, 2), _bdot(qb, kf, 2, 2))
        o_intra = _bdot(att, v, 2, 1)
        u = _bdot(v, ke, 1, 1)
        gdec = jnp.exp(bend)
        s_prev = [None] * (cpb * HEADS)
        for h in range(HEADS):
            s = st[h]
            for c in range(cpb):
                b = c * HEADS + h
                s_prev[b] = s
                s = s * gdec[b] + u[b]
            st[h] = s
        s_prev = jnp.stack(s_prev)
        o = o_intra + _bdot(qf, s_prev, 2, 2)
        rms = lax.rsqrt(jnp.mean(o * o, axis=-1, keepdims=True) + EPS)
        gn = jnp.stack([gn_ref[pl.ds(h, 1), :] for _ in range(cpb) for h in range(HEADS)])
        y = (o * rms * gn * (rv * _sigmoid(rv))).astype(BF16)
        for c in range(cpb):
            for h in range(HEADS):
                b = c * HEADS + h
                rows, ln = pl.ds(c * CHUNK, CHUNK), pl.ds(h * LANE, LANE)
                o_ref[rows, ln] = o[b]
                y_ref[rows, ln] = y[b]
                sp_ref[h, c] = s_prev[b]

    zb = lambda base: pl.BlockSpec((tb, W), lambda i: (i, base // W))
    hb_ = pl.BlockSpec((tb, W), lambda i: (i, 0))
    return pl.pallas_call(
        body, grid=(S // tb,),
        in_specs=[zb(ZC_Q), zb(ZC_K), zb(ZC_V), zb(ZC_R), hb_, pl.BlockSpec((HEADS, DV), lambda i: (0, 0))],
        out_specs=[hb_, pl.BlockSpec((HEADS, cpb, DV, DKP), lambda i: (0, i, 0, 0)), hb_],
        out_shape=[jax.ShapeDtypeStruct((S, D_GLA), F32), jax.ShapeDtypeStruct((HEADS, S // CHUNK, DV, DKP), F32),
                   jax.ShapeDtypeStruct((S, D_GLA), BF16)],
        scratch_shapes=[pltpu.VMEM((HEADS, DV, DKP), F32)],
        compiler_params=_cp(("arbitrary",)), name=name,
    )(z, z, z, z, la, gn)


def _gla_bwd(dycat, z, la, o_raw, sprev, gn, name):
    S = z.shape[0]
    W = HEADS * LANE
    tb = _div(S, 512, CHUNK)
    cpb = tb // CHUNK
    nb = S // tb

    def body(q_ref, k_ref, v_ref, r_ref, la_ref, o_ref, sp_ref, dy_ref, gn_ref, dz_ref, dla_ref, dgn_ref, dst):
        @pl.when(pl.program_id(0) == 0)
        def _():
            dst[...] = jnp.zeros_like(dst)
            dgn_ref[...] = jnp.zeros_like(dgn_ref)

        nt = cpb * HEADS
        tri, tril, triu = _tri_masks(nt)
        q, k, v, rv, lav, o, dy = (_tiles(r, cpb) for r in (q_ref, k_ref, v_ref, r_ref, la_ref, o_ref, dy_ref))
        bend, eb, enb, ee, qf, qb, kb, kf, ke = _chunk_fwd_terms(q, k, lav, tril)
        att = jnp.where(tri, _bdot(qf, kb, 2, 2), _bdot(qb, kf, 2, 2))
        s_prev = jnp.stack([sp_ref[h, c] for c in range(cpb) for h in range(HEADS)])
        gdec = jnp.exp(bend)
        gn = jnp.stack([gn_ref[pl.ds(h, 1), :] for _ in range(cpb) for h in range(HEADS)])
        rms = lax.rsqrt(jnp.mean(o * o, axis=-1, keepdims=True) + EPS)
        oh = o * rms
        sg = _sigmoid(rv)
        sr = rv * sg
        d_r = (dy * oh * gn * (sg * (1.0 + rv * (1.0 - sg)))).astype(BF16)
        dgn = jnp.sum(dy * sr * oh, axis=1, keepdims=True)
        w = dy * sr * gn
        do = rms * (w - oh * jnp.mean(w * oh, axis=-1, keepdims=True))
        p = _bdot(do, qf, 1, 1)
        ds = [None] * nt
        for h in range(HEADS):
            s = dst[h]
            for c in reversed(range(cpb)):
                b = c * HEADS + h
                ds[b] = s
                s = s * gdec[b] + p[b]
            dst[h] = s
            dgn_ref[pl.ds(h, 1), :] += sum(dgn[c * HEADS + h] for c in range(cpb))
        ds = jnp.stack(ds)
        datt = _bdot(do, v, 2, 2)
        daf = jnp.where(tri, datt, 0.0)
        dab = jnp.where(tri, 0.0, datt)
        d_v = (_bdot(att, do, 1, 1) + _bdot(ke, ds, 2, 2)).astype(BF16)
        dke = _bdot(v, ds, 2, 1)
        dqf = _bdot(daf, kb, 2, 1) + _bdot(do, s_prev, 2, 1)
        dkb = _bdot(daf, qf, 1, 1)
        dqb = _bdot(dab, kf, 2, 1)
        dkf = _bdot(dab, qb, 1, 1)
        dg = jnp.sum(ds * s_prev, axis=1, keepdims=True)
        d_q = ((dqf * eb + dqb * enb) * (DK ** -0.5)).astype(BF16)
        d_k = (dkb * enb + dkf * eb + dke * ee).astype(BF16)
        dbc = dqf * qf - dkb * kb - dqb * qb + dkf * kf - dke * ke
        dbend = jnp.sum(dke * ke, axis=1, keepdims=True) + dg * gdec
        dla = _bdot_exact(triu, dbc) + dbend
        for c in range(cpb):
            for h in range(HEADS):
                b = c * HEADS + h
                rows = pl.ds(c * CHUNK, CHUNK)
                for base, val in ((ZC_Q, d_q), (ZC_K, d_k), (ZC_V, d_v), (ZC_R, d_r)):
                    dz_ref[rows, pl.ds(base + h * LANE, LANE)] = val[b]
                dla_ref[rows, pl.ds(h * LANE, LANE)] = dla[b]

    zb = lambda base: pl.BlockSpec((tb, W), lambda i: (nb - 1 - i, base // W))
    hb_ = pl.BlockSpec((tb, W), lambda i: (nb - 1 - i, 0))
    return pl.pallas_call(
        body, grid=(nb,),
        in_specs=[zb(ZC_Q), zb(ZC_K), zb(ZC_V), zb(ZC_R), hb_, hb_,
                  pl.BlockSpec((HEADS, cpb, DV, DKP), lambda i: (0, nb - 1 - i, 0, 0)),
                  pl.BlockSpec((tb, W), lambda i: (nb - 1 - i, 1)),
                  pl.BlockSpec((HEADS, DV), lambda i: (0, 0))],
        out_specs=[pl.BlockSpec((tb, Z_GLA), lambda i: (nb - 1 - i, 0)), hb_, pl.BlockSpec((HEADS, DV), lambda i: (0, 0))],
        out_shape=[jax.ShapeDtypeStruct((S, Z_GLA), BF16), jax.ShapeDtypeStruct((S, HEADS * DKP), F32), jax.ShapeDtypeStruct((HEADS, DV), F32)],
        scratch_shapes=[pltpu.VMEM((HEADS, DV, DKP), F32)],
        compiler_params=_cp(("arbitrary",)), name=name,
    )(z, z, z, z, la, o_raw, sprev, dycat, gn)


def _mod_proj(c_all, w3, layer, b, name):
    B, D = c_all.shape
    N = w3.shape[2]
    tn = _div(N, 1024, LANE)

    def body(c_ref, w_ref, b_ref, o_ref):
        cv = c_ref[...]
        o_ref[...] = _dot(cv * _sigmoid(cv), w_ref[...]) + b_ref[...]

    return pl.pallas_call(
        body, grid=(N // tn,),
        in_specs=[pl.BlockSpec((B, D), lambda j: (0, 0)), pl.BlockSpec((None, D, tn), lambda j: (layer, 0, j)), pl.BlockSpec((1, tn), lambda j: (0, j))],
        out_specs=pl.BlockSpec((B, tn), lambda j: (0, j)), out_shape=jax.ShapeDtypeStruct((B, N), F32),
        compiler_params=_cp(("parallel",)), name=name,
    )(c_all, w3, b)


def _mod_wgrad(c_t, dm, name):
    D, B = c_t.shape
    N = dm.shape[1]
    tn = _div(N, 1024, LANE)

    def body(c_ref, d_ref, o_ref):
        cv = c_ref[...]
        ca = cv * _sigmoid(cv)
        acc = ca[:, 0:1] * d_ref[pl.ds(0, 1), :]
        for b in range(1, B):
            acc = acc + ca[:, b:b + 1] * d_ref[pl.ds(b, 1), :]
        o_ref[...] = acc

    return pl.pallas_call(
        body, grid=(N // tn,),
        in_specs=[pl.BlockSpec((D, B), lambda j: (0, 0)), pl.BlockSpec((B, tn), lambda j: (0, j))],
        out_specs=pl.BlockSpec((D, tn), lambda j: (0, j)), out_shape=jax.ShapeDtypeStruct((D, N), F32),
        compiler_params=_cp(("parallel",)), name=name,
    )(c_t, dm)


def _rowsum(xs, name):
    n, N = xs.shape
    tn = _div(N, 8192, LANE)

    def body(x_ref, o_ref):
        acc = x_ref[pl.ds(0, 1), :]
        for r in range(1, n):
            acc = acc + x_ref[pl.ds(r, 1), :]
        o_ref[...] = acc

    return pl.pallas_call(
        body, grid=(N // tn,), in_specs=[pl.BlockSpec((n, tn), lambda j: (0, j))],
        out_specs=pl.BlockSpec((1, tn), lambda j: (0, j)), out_shape=jax.ShapeDtypeStruct((1, N), F32),
        compiler_params=_cp(("parallel",)), name=name,
    )(xs)


def _adamw(w, g, m, v, name, copy_grad=False):
    R, C = w.shape
    tr = _div(R, max(8, (1 << 18) // C), 8)

    def body(w_ref, g_ref, m_ref, v_ref, d_ref, nm_ref, nv_ref, *g_out):
        gv = g_ref[...]
        if copy_grad:
            g_out[0][...] = gv
        mn = ADAM_B1 * m_ref[...] + (1.0 - ADAM_B1) * gv
        vn = ADAM_B2 * v_ref[...] + (1.0 - ADAM_B2) * (gv * gv)
        m_hat = mn / (1.0 - ADAM_B1 ** ADAM_STEP)
        v_hat = vn / (1.0 - ADAM_B2 ** ADAM_STEP)
        d_ref[...] = -ADAM_LR * (m_hat / (jnp.sqrt(v_hat) + ADAM_EPS) + ADAM_WD * w_ref[...])
        nm_ref[...] = mn
        nv_ref[...] = vn

    blk = pl.BlockSpec((tr, C), lambda i: (i, 0))
    os_ = jax.ShapeDtypeStruct((R, C), F32)
    n_out = 4 if copy_grad else 3
    return pl.pallas_call(
        body, grid=(R // tr,), in_specs=[blk] * 4, out_specs=[blk] * n_out, out_shape=[os_] * n_out,
        compiler_params=_cp(("parallel",)), name=name,
    )(w, g, m, v)


def _place():
    return lax.axis_index("x"), lax.axis_index("y"), lax.axis_index("c")


def _other_chips(x, y):
    return [(1 - x, y), (x, 1 - y), (1 - x, 1 - y)]


def _half(c, rows):
    return pl.ds(c * (rows // 2), rows // 2)


_ANY = pl.BlockSpec(memory_space=pl.ANY)


def _ag_small(v, name):
    r, n = v.shape

    def body(v_ref, o_ref, send_sems, recv_sems):
        x, y, c = _place()
        me = 4 * x + 2 * y + c
        o_ref[pl.ds(me, 1)] = v_ref[...][None]
        peers = [(x ^ (k >> 2), y ^ ((k >> 1) & 1), c ^ (k & 1)) for k in range(1, 8)]
        copies = []
        for k, peer in enumerate(peers):
            cp = pltpu.make_async_remote_copy(
                src_ref=v_ref, dst_ref=o_ref.at[me], send_sem=send_sems.at[k], recv_sem=recv_sems.at[k],
                device_id=peer, device_id_type=MESH)
            cp.start()
            copies.append(cp)
        for cp in copies:
            cp.wait()

    return pl.pallas_call(
        body, out_shape=jax.ShapeDtypeStruct((8, r, n), v.dtype),
        in_specs=[pl.BlockSpec(memory_space=pltpu.VMEM)], out_specs=pl.BlockSpec(memory_space=pltpu.VMEM),
        scratch_shapes=[pltpu.SemaphoreType.DMA((7,)), pltpu.SemaphoreType.DMA((7,))],
        compiler_params=pltpu.CompilerParams(vmem_limit_bytes=VMEM_LIMIT), name=name,
    )(v)


def _rs_sibling_copies(bufs, send_sems, recv_sems):
    n = len(bufs) // 2
    x, y, c = _place()
    return [pltpu.make_async_remote_copy(
        src_ref=bufs[i].at[:, _half(1 - c, bufs[i].shape[1])], dst_ref=bufs[n + i], send_sem=send_sems.at[i], recv_sem=recv_sems.at[i],
        device_id=(x, y, 1 - c), device_id_type=MESH) for i in range(n)]


def _rs_sibling(gs, name, after=None):
    n = len(gs)

    def body(*refs):
        copies = _rs_sibling_copies(refs[:n] + refs[n + 1:2 * n + 1], refs[2 * n + 1], refs[2 * n + 2])
        for cp in copies:
            cp.start()
        for cp in copies:
            cp.wait()

    return pl.pallas_call(
        body, out_shape=[jax.ShapeDtypeStruct((N_CHIPS, g.shape[1] // 2, g.shape[2]), g.dtype) for g in gs],
        in_specs=[_ANY] * (n + 1), out_specs=[_ANY] * n,
        scratch_shapes=[pltpu.SemaphoreType.DMA((n,)), pltpu.SemaphoreType.DMA((n,))],
        compiler_params=pltpu.CompilerParams(has_side_effects=True), name=name,
    )(*gs, gs[0] if after is None else after)


def _rs_presum(g, sib, c_arr, name):
    ns, R, C = g.shape
    rh = R // 2
    tr = _div(rh, max(16, (1 << 19) // C), 16)
    nrb = rh // tr

    def body(c_ref, g_ref, s_ref, o_ref):
        o_ref[...] = (g_ref[...] + s_ref[...]).astype(BF16)

    return pl.pallas_call(
        body, out_shape=jax.ShapeDtypeStruct((ns, rh, C), BF16),
        grid_spec=pltpu.PrefetchScalarGridSpec(
            num_scalar_prefetch=1, grid=(ns, nrb),
            in_specs=[pl.BlockSpec((None, tr, C), lambda s, r, c_ref: (s, c_ref[0] * nrb + r, 0)),
                      pl.BlockSpec((None, tr, C), lambda s, r, c_ref: (s, r, 0))],
            out_specs=pl.BlockSpec((None, tr, C), lambda s, r, c_ref: (s, r, 0))),
        compiler_params=_cp(("parallel", "parallel")), name=name,
    )(c_arr, g, sib)


def _rs_sum(g, sib, recv, full, layer, sc_arr, name):
    ns, R, C = g.shape
    rh = R // 2
    tr = _div(rh, max(16, (1 << 18) // C), 16)
    nrb = rh // tr

    def body(sc_ref, g_ref, s_ref, r_ref, f_ref, o_ref):
        acc = g_ref[...] + s_ref[...]
        for j in range(3):
            acc = acc + r_ref[j].astype(F32)
        o_ref[...] = acc

    return pl.pallas_call(
        body, out_shape=jax.ShapeDtypeStruct(full.shape, F32),
        grid_spec=pltpu.PrefetchScalarGridSpec(
            num_scalar_prefetch=1, grid=(nrb,),
            in_specs=[pl.BlockSpec((None, tr, C), lambda r, sc: (sc[0], sc[1] * nrb + r, 0)),
                      pl.BlockSpec((None, tr, C), lambda r, sc: (sc[0], r, 0)),
                      pl.BlockSpec((3, tr, C), lambda r, sc: (0, r, 0)),
                      _ANY],
            out_specs=pl.BlockSpec((None, tr, C), lambda r, sc: (layer, sc[1] * nrb + r, 0))),
        input_output_aliases={4: 0},
        compiler_params=_cp(("parallel",)), name=name,
    )(sc_arr, g, sib, recv, full)


def _rs_share(fulls, layer, name):
    n = len(fulls)

    def body(*refs):
        src, out = refs[:n], refs[n:2 * n]
        send_sems, recv_sems = refs[2 * n:]
        x, y, c = _place()
        copies = []
        for i in range(n):
            rows = out[i].shape[1]
            cp = pltpu.make_async_remote_copy(
                src_ref=out[i].at[layer, _half(c, rows)], dst_ref=out[i].at[layer, _half(c, rows)],
                send_sem=send_sems.at[i], recv_sem=recv_sems.at[i], device_id=(x, y, 1 - c), device_id_type=MESH)
            cp.start()
            copies.append(cp)
        for cp in copies:
            cp.wait()

    return pl.pallas_call(
        body, out_shape=[jax.ShapeDtypeStruct(f.shape, f.dtype) for f in fulls],
        in_specs=[_ANY] * n, out_specs=[_ANY] * n, input_output_aliases={i: i for i in range(n)},
        scratch_shapes=[pltpu.SemaphoreType.DMA((n,)), pltpu.SemaphoreType.DMA((n,))],
        compiler_params=pltpu.CompilerParams(has_side_effects=True), name=name,
    )(*fulls)


_HBM = pl.BlockSpec(memory_space=pltpu.HBM)
_SEM = pl.BlockSpec(memory_space=pltpu.SEMAPHORE)
_EFFECT = pltpu.SideEffectType.DATAFLOW_SIDE_EFFECTING


def _in_hbm(a):
    return pltpu.with_memory_space_constraint(a, pltpu.HBM)


def _split_start(bufs, n_sem, copies_of, name):
    nb = len(bufs)

    def body(*refs):
        for cp in copies_of(refs[:nb], refs[nb], refs[nb + 1]):
            cp.start()
        refs[-1][...] = jnp.zeros_like(refs[-1])

    out = pl.pallas_call(
        body, name=name,
        out_shape=(pltpu.SemaphoreType.DMA((n_sem,)), pltpu.SemaphoreType.DMA((n_sem,)), *[pltpu.HBM(a.shape, a.dtype) for a in bufs],
                   jax.ShapeDtypeStruct((SUB, LANE), F32)),
        in_specs=[_HBM] * nb, out_specs=(_SEM, _SEM, *([_HBM] * nb), pl.BlockSpec(memory_space=pltpu.VMEM)),
        input_output_aliases={i: 2 + i for i in range(nb)},
        compiler_params=pltpu.CompilerParams(has_side_effects=_EFFECT),
    )(*[_in_hbm(a) for a in bufs])
    return out[0], out[1], list(out[2:2 + nb]), out[-1]


def _split_wait(send_sems, recv_sems, bufs, after, copies_of, name):
    nb = len(bufs)

    def body(*refs):
        for cp in copies_of(refs[:nb], refs[nb], refs[nb + 1]):
            cp.wait_send()
            cp.wait_recv()

    return list(pl.pallas_call(
        body, name=name, out_shape=[pltpu.HBM(a.shape, a.dtype) for a in bufs],
        in_specs=[_HBM] * nb + [_SEM, _SEM, _ANY], out_specs=[_HBM] * nb,
        input_output_aliases={i: i for i in range(nb)},
        compiler_params=pltpu.CompilerParams(has_side_effects=_EFFECT),
    )(*bufs, send_sems, recv_sems, after))


def _ag_half_copies(land, send_sems, recv_sems, landing_of_mine):
    x, y, c = _place()
    cps = []
    for j, (cx, cy) in enumerate(_other_chips(x, y)):
        for i in range(len(land)):
            rows = _half(c, land[i].shape[1])
            s = 2 * x + y if landing_of_mine else 2 * cx + cy
            cps.append(pltpu.make_async_remote_copy(
                src_ref=land[i].at[2 * x + y, rows], dst_ref=land[i].at[s, rows], send_sem=send_sems.at[3 * i + j], recv_sem=recv_sems.at[3 * i + j],
                device_id=(cx, cy, c), device_id_type=MESH))
    return cps


def _ag_starts(land, send_sems, recv_sems):
    return _ag_half_copies(land, send_sems, recv_sems, True)


def _ag_waits(land, send_sems, recv_sems):
    return _ag_half_copies(land, send_sems, recv_sems, False)


def _ag_finish(lands, name):
    n = len(lands)

    def body(*refs):
        land = refs[n:2 * n]
        send_sems, recv_sems = refs[2 * n:]
        x, y, c = _place()
        sibling = (x, y, 1 - c)

        def copy(k, i, s, h):
            blk = land[i].at[s, _half(h, land[i].shape[1])]
            return pltpu.make_async_remote_copy(
                src_ref=blk, dst_ref=blk, send_sem=send_sems.at[k], recv_sem=recv_sems.at[k], device_id=sibling, device_id_type=MESH)

        chips = _other_chips(x, y)
        passed = [copy(3 * i + j, i, 2 * cx + cy, c) for j, (cx, cy) in enumerate(chips) for i in range(n)]
        for cp in passed:
            cp.start()
        for j, (cx, cy) in enumerate(chips):
            for i in range(n):
                copy(3 * i + j, i, 2 * cx + cy, 1 - c).wait_recv()
        for cp in passed:
            cp.wait_send()

    return pl.pallas_call(
        body, out_shape=[jax.ShapeDtypeStruct(a.shape, a.dtype) for a in lands],
        in_specs=[_ANY] * n, out_specs=[_ANY] * n, input_output_aliases={i: i for i in range(n)},
        scratch_shapes=[pltpu.SemaphoreType.DMA((3 * n,)), pltpu.SemaphoreType.DMA((3 * n,))],
        compiler_params=pltpu.CompilerParams(has_side_effects=True), name=name,
    )(*lands)


def _rs_chip_copies(bufs, send_sems, recv_sems):
    n = len(bufs) // 2
    x, y, c = _place()
    return [pltpu.make_async_remote_copy(
        src_ref=bufs[i].at[2 * cx + cy], dst_ref=bufs[n + i].at[j], send_sem=send_sems.at[3 * i + j], recv_sem=recv_sems.at[3 * i + j],
        device_id=(cx, cy, c), device_id_type=MESH) for j, (cx, cy) in enumerate(_other_chips(x, y)) for i in range(n)]


def _pad_heads(w):
    lead = w.shape[:-1]
    w4 = w.reshape(*lead, HEADS, DK)
    w4 = jnp.pad(w4, [(0, 0)] * len(lead) + [(0, 0), (0, DKP - DK)])
    return w4.reshape(*lead, HEADS * DKP)


def _unpad_heads(w):
    lead = w.shape[:-1]
    return w.reshape(*lead, HEADS, DKP)[..., :DK].reshape(*lead, HEADS * DK)


def _mix_column_runs():
    o = 2 * D_CONV
    runs = [(0, D_CONV, ZC_A), (D_CONV, D_CONV, ZC_B)]
    runs += [(o + h * DK, DK, ZC_Q + h * DKP) for h in range(HEADS)]
    runs += [(o + HEADS * DK + h * DK, DK, ZC_K + h * DKP) for h in range(HEADS)]
    o += 2 * HEADS * DK
    return runs + [(o, D_GLA, ZC_V), (o + D_GLA, D_GLA, ZC_R), (o + 2 * D_GLA, GATE_RANK, ZC_G)]


def _mix_weight(win4, n_cols):
    D = win4.shape[1]
    pieces, z = [], 0
    for c0, width, z0 in sorted(_mix_column_runs(), key=lambda r: r[2]):
        if z0 > z:
            pieces.append(jnp.zeros((D, z0 - z), win4.dtype))
        c = c0
        while c < c0 + width:
            s, lo = divmod(c, n_cols)
            hi = min(n_cols, lo + c0 + width - c)
            pieces.append(win4[s, :, lo:hi])
            c += hi - lo
        z = z0 + width
    pieces.append(jnp.zeros((D, Z_COLS - z), win4.dtype))
    return jnp.concatenate(pieces, axis=1)


def _mix_weight_grad(dgla, dab, dglr, n_cols, n_pad):
    D = dab.shape[0]

    def zcols(z0, z1):
        for arr, base in ((dgla, 0), (dab, ZC_A), (dglr, ZC_G)):
            if base <= z0 and z1 <= base + arr.shape[1]:
                return arr[:, z0 - base:z1 - base]

    shards = []
    for s in range(N_CHIPS):
        pieces = []
        for c0, width, z0 in _mix_column_runs():
            lo, hi = max(c0, s * n_cols), min(c0 + width, (s + 1) * n_cols)
            if lo < hi:
                pieces.append(zcols(z0 + lo - c0, z0 + hi - c0))
        pieces.append(jnp.zeros((D, n_pad - n_cols), dab.dtype))
        shards.append(jnp.concatenate(pieces, axis=1))
    return jnp.stack(shards)


_ARG_NAMES = ['x', 'c', 'w_ada', 'b_ada', 'g_norm_ffn1', 'w_ffn1_in', 'w_ffn1_out', 'g_norm_mix', 'w_in', 'w_dw', 'b_dw', 'g_conv_ln', 'b_conv_ln', 'w_gate_up', 'b_gate', 'g_gla_norm', 'w_out', 'g_norm_ffn2', 'w_ffn2_in', 'w_ffn2_out', 'g_norm_final', 'w_ada_final', 'b_ada_final']
_WEIGHTS = _ARG_NAMES[2:]
_BIG = ('w_ffn1_in', 'w_ffn1_out', 'w_in', 'w_out', 'w_ffn2_in', 'w_ffn2_out')
_SMALL = ('g_norm_ffn1', 'g_norm_mix', 'w_dw', 'b_dw', 'g_conv_ln', 'b_conv_ln', 'w_gate_up', 'b_gate', 'g_gla_norm', 'g_norm_ffn2', 'g_norm_final')


def _ffn_fwd(x, h, gv, w4, wo, nxt, tag):
    if isinstance(h, tuple):
        z, act, h = _ffn_up(x, w4, f"ffn_up_{tag}", norm=h)
    else:
        z, act = _ffn_up(h, w4, f"ffn_up_{tag}")
    y, xn, *hn = _ffn_down(act, wo, x, gv, nxt, f"ffn_down_{tag}")
    return xn, (hn[0] if hn else None), y, (x, h, z, act)


def _ffn_bwd(dxn, dy, saved, g, scale, prev, w4, wo, tag):
    x, h, z, act = saved
    ns = w4.shape[0]
    dwo = _mm_tn(act, dy, f"dw_out_{tag}")
    dz, dx, dsh, dsc, dg, *pv = _ffn_bwd_core(dy, wo, z, w4, x, dxn, g, scale, prev, f"ffn_bwd_{tag}")
    dwi = _dw_ffn_in(h, dz, ns, f"dw_in_{tag}")
    return dx, pv, dict(dshift=dsh, dscale=dsc, dg=dg, dw_in=dwi, dw_out=dwo.reshape(N_CHIPS, -1, dwo.shape[1]))


def _mix_fwd(x, h, gv, wmix, w_dw, b_dw, g_ln, b_ln, wgp, bgp, gn, wout, nxt, tag):
    z = _mm([(h, wmix, 0)], F32, f"mix_in_{tag}")
    u, yc, yconv = _conv_fwd(z, w_dw, b_dw, g_ln, b_ln, f"conv_fwd_{tag}")
    la = _loggate(z, wgp, bgp, f"loggate_{tag}")
    o_raw, sprev, ygla = _gla_fwd(z, la, gn, f"gla_fwd_{tag}")
    y, xn, *hn = _mix_out(yconv, ygla, wout, x, gv, nxt, f"mix_out_{tag}")
    return xn, (hn[0] if hn else None), y, (x, h, z, u, yc, la, o_raw, sprev, yconv, ygla)


def _mix_bwd(dxn, dy, saved, g, scale, prev, wmix, w_dw, g_ln, b_ln, wgp, bgp, gn, wout, n_cols, n_pad, tag):
    x, h, z, u, yc, la, o_raw, sprev, yconv, ygla = saved
    dycat = _mm([(dy, wout, 0)], F32, f"mix_dycat_{tag}", nt=True)
    dwout = _mm_tn_two(yconv, ygla, dy, f"dw_mixout_{tag}")
    dab, dwdw, dbdw, dgln, dbln = _conv_bwd(dycat, z, u, yc, w_dw, g_ln, b_ln, f"conv_bwd_{tag}")
    dgla, dla, dgn = _gla_bwd(dycat, z, la, o_raw, sprev, gn, f"gla_bwd_{tag}")
    dglr, dwgp, dbgp = _loggate_bwd(dla, z, wgp, wgp.T, bgp, f"loggate_bwd_{tag}")
    dx, dsh, dsc, dg, *pv = _dh_normmod_bwd(
        [(dgla, wmix, 0, 0, Z_GLA), (dab, wmix, 0, ZC_A // (2 * D_CONV), 2 * D_CONV), (dglr, wmix, 0, ZC_G // LANE, LANE)],
        x, dxn, g, scale, prev, f"mix_dh_{tag}")
    dwin = _mix_weight_grad(_mm_tn(h, dgla, f"dw_mixin_gla_{tag}"), _mm_tn(h, dab, f"dw_mixin_conv_{tag}"), _mm_tn(h, dglr, f"dw_mixin_gate_{tag}"),
                            n_cols, n_pad)
    grads = dict(dshift=dsh, dscale=dsc, dg=dg, dw_in=dwin, dw_out=dwout.reshape(N_CHIPS, -1, dwout.shape[1]), dw_dw=dwdw, db_dw=dbdw,
                 dg_ln=dgln, db_ln=dbln, dw_gate=_unpad_heads(dwgp[:GATE_RANK]), db_gate=_unpad_heads(dbgp)[0], dgn=dgn)
    return dx, pv, grads


def kernel(x, c, w_ada, b_ada, g_norm_ffn1, w_ffn1_in, w_ffn1_out, g_norm_mix, w_in, w_dw, b_dw, g_conv_ln, b_conv_ln, w_gate_up, b_gate, g_gla_norm, w_out, g_norm_ffn2, w_ffn2_in, w_ffn2_out, g_norm_final, w_ada_final, b_ada_final, loss_target, m_w_ada, m_b_ada, m_g_norm_ffn1, m_w_ffn1_in, m_w_ffn1_out, m_g_norm_mix, m_w_in, m_w_dw, m_b_dw, m_g_conv_ln, m_b_conv_ln, m_w_gate_up, m_b_gate, m_g_gla_norm, m_w_out, m_g_norm_ffn2, m_w_ffn2_in, m_w_ffn2_out, m_g_norm_final, m_w_ada_final, m_b_ada_final, v_w_ada, v_b_ada, v_g_norm_ffn1, v_w_ffn1_in, v_w_ffn1_out, v_g_norm_mix, v_w_in, v_w_dw, v_b_dw, v_g_conv_ln, v_b_conv_ln, v_w_gate_up, v_b_gate, v_g_gla_norm, v_w_out, v_g_norm_ffn2, v_w_ffn2_in, v_w_ffn2_out, v_g_norm_final, v_w_ada_final, v_b_ada_final):
    given = dict(locals())
    W = {n: given[n] for n in _WEIGHTS}
    M1 = {n: given["m_" + n] for n in _WEIGHTS}
    M2 = {n: given["v_" + n] for n in _WEIGHTS}
    xs = x[0]
    tgt = loss_target[0]
    S, D = xs.shape
    L = w_ada.shape[0]
    xi, yi, ci = _place()
    s_me = 2 * xi + yi
    b_me = 4 * xi + 2 * yi + ci
    nsh = w_ada.shape[2]
    nfin = w_ada_final.shape[1]
    n_cols = w_in.shape[2]
    n_pad = -(-n_cols // LANE) * LANE

    def lands_of(l):
        shards = [W[n][l].astype(BF16) for n in _BIG]
        shards[2] = jnp.pad(shards[2], ((0, 0), (0, n_pad - n_cols)))
        return [lax.dynamic_update_index_in_dim(lax.empty((N_CHIPS,) + s.shape, BF16), s, s_me, 0) for s in shards]

    lands = {l: lands_of(l) for l in range(L)}
    ag_groups = [dict(l=0, items=[0, 1], need=0), dict(l=0, items=[2, 3, 4, 5], need=1)]
    ag_groups += [dict(l=l, items=list(range(len(_BIG))), need=3 * l) for l in range(1, L)]

    def ag_start(grp):
        bufs = [lands[grp["l"]][i] for i in grp["items"]]
        return _split_start(bufs, 3 * len(bufs), _ag_starts, f"ag_start_l{grp['l']}_{grp['items'][0]}")

    pend = ag_start(ag_groups[0])
    tok = pend[3][0, 0]

    c_all = _ag_small(c.reshape(8, D // 8) + tok, "ag_c").reshape(8, D)
    tok = None
    parts = [_mod_proj(c_all, w_ada, l, lax.dynamic_slice(b_ada, (l, s_me * nsh), (1, nsh)), f"mod_proj_{l}") for l in range(L)]
    parts.append(_mod_proj(c_all, w_ada_final[None], 0, lax.dynamic_slice(b_ada_final, (s_me * nfin,), (nfin,))[None], "mod_proj_final"))
    mod_all = _ag_small(jnp.concatenate(parts, axis=1), "ag_mod")
    mine = [lax.dynamic_index_in_dim(lax.dynamic_index_in_dim(mod_all, 2 * s + ci, 0, False), b_me, 0, False) for s in range(N_CHIPS)]
    mods = [jnp.concatenate([mine[s][l * nsh:(l + 1) * nsh] for s in range(N_CHIPS)]).reshape(N_MOD, 1, D) for l in range(L)]
    fmod = jnp.concatenate([mine[s][L * nsh:] for s in range(N_CHIPS)]).reshape(2, 1, D)

    tiny = jnp.concatenate([w_dw.reshape(-1), w_gate_up.reshape(-1)])
    tiny_all = _ag_small(jnp.pad(tiny, (0, (-tiny.shape[0]) % (8 * LANE))).reshape(8, -1), "ag_tiny").reshape(8, -1)
    n_dw = w_dw.size
    dw_parts = [lax.dynamic_index_in_dim(tiny_all, 2 * s + ci, 0, False) for s in range(N_CHIPS)]
    w_dw_full = jnp.concatenate([p[:n_dw].reshape(w_dw.shape) for p in dw_parts], axis=2)
    w_gu_full = jnp.concatenate([p[n_dw:n_dw + w_gate_up.size].reshape(w_gate_up.shape) for p in dw_parts], axis=2)

    def layer_weights(l, lands):
        wi1, wo1, win4, wout4, wi2, wo2 = lands
        return dict(
            wi1=wi1, wo1=wo1.reshape(-1, D), wi2=wi2, wo2=wo2.reshape(-1, D), wout=wout4.reshape(-1, D), wmix=_mix_weight(win4, n_cols),
            wgp=jnp.pad(_pad_heads(w_gu_full[l]), ((0, LANE - GATE_RANK), (0, 0))).astype(BF16), bgp=_pad_heads(b_gate[l])[None])

    gnorm = (g_norm_ffn1, g_norm_mix, g_norm_ffn2)
    subs = [dict(l=l, j=j, tag=f"{('ffn1', 'mix', 'ffn2')[j]}_l{l}", g=gnorm[j][l][None], shift=mods[l][3 * j], scale=mods[l][3 * j + 1],
                 gv=mods[l][3 * j + 2] * (1.0 if j == 1 else 0.5)) for l in range(L) for j in range(3)]
    gi = 0
    xcur = xs
    h = None
    for k, sb in enumerate(subs):
        l, j = sb["l"], sb["j"]
        if pend is not None and ag_groups[gi]["need"] == k:
            grp = ag_groups[gi]
            nm = f"l{grp['l']}_{grp['items'][0]}"
            after = xcur if k > 0 else sb["shift"]
            done = _ag_finish(_split_wait(pend[0], pend[1], pend[2], after, _ag_waits, f"ag_wait_{nm}"), f"ag_finish_{nm}")
            for i, a in zip(grp["items"], done):
                lands[grp["l"]][i] = a
            gi += 1
            pend = ag_start(ag_groups[gi]) if gi < len(ag_groups) else None
            tok = pend[3][0, 0] if pend is not None else None
        if h is None:
            h = (sb["g"] if tok is None else sb["g"] + tok, sb["shift"], sb["scale"])
            tok = None
        d = layer_weights(l, lands[l])
        nxt = (subs[k + 1]["g"], subs[k + 1]["shift"], subs[k + 1]["scale"]) if k + 1 < len(subs) else None
        gv = sb["gv"] if tok is None else sb["gv"] + tok
        tok = None
        if j == 1:
            xcur, h, sb["y"], sb["saved"] = _mix_fwd(xcur, h, gv, d["wmix"], w_dw_full[l], b_dw[l][None], g_conv_ln[l][None],
                                                     b_conv_ln[l][None], d["wgp"], d["bgp"], g_gla_norm[l], d["wout"], nxt, sb["tag"])
        else:
            w4, wo = (d["wi1"], d["wo1"]) if j == 0 else (d["wi2"], d["wo2"])
            xcur, h, sb["y"], sb["saved"] = _ffn_fwd(xcur, h, gv, w4, wo, nxt, sb["tag"])
    lw = [layer_weights(l, lands[l]) for l in range(L)]

    c_arr = jnp.stack([ci]).astype(jnp.int32)
    sc_arr = jnp.stack([s_me, ci]).astype(jnp.int32)
    fulls = [lax.empty((L,) + ((W[n].shape[1], n_pad) if n == 'w_in' else W[n].shape[1:]), F32) for n in _BIG]

    def rs_begin(gs, items, l, after=None):
        nm = f"l{l}_{items[0]}"
        sibs = _rs_sibling(gs, f"rs_sibling_{nm}", after)
        return sibs, [_rs_presum(g, sb_, c_arr, f"rs_presum_{i}_l{l}") for i, g, sb_ in zip(items, gs, sibs)]

    def rs_end(gs, sibs, recvs, items, l):
        summed = [_rs_sum(g, sb_, rv, fulls[i], l, sc_arr, f"rs_sum_{i}_l{l}") for i, g, sb_, rv in zip(items, gs, sibs, recvs)]
        for i, f in zip(items, _rs_share(summed, l, f"rs_share_l{l}_{items[0]}")):
            fulls[i] = f

    def rs_start(gs, items, l, sibs=None, after=None):
        if sibs is None:
            sibs, ps = rs_begin(gs, items, l, after)
        else:
            ps = [_rs_presum(g, sb_, c_arr, f"rs_presum_{i}_l{l}") for i, g, sb_ in zip(items, gs, sibs)]
        pend = _split_start(ps + [lax.empty((3,) + p.shape[1:], BF16) for p in ps], 3 * len(ps), _rs_chip_copies, f"rs_start_l{l}_{items[0]}")
        return dict(gs=gs, sibs=sibs, pend=pend, items=items, l=l)

    def sib_start(gs, items, l):
        lands_ = [lax.empty((N_CHIPS, g.shape[1] // 2, g.shape[2]), F32) for g in gs]
        return dict(pend=_split_start(gs + lands_, len(gs), _rs_sibling_copies, f"rs_sibling_start_l{l}"), items=items, l=l, n=len(gs))

    def sib_finish(sp, after):
        bufs = _split_wait(sp["pend"][0], sp["pend"][1], sp["pend"][2], after, _rs_sibling_copies, f"rs_sibling_wait_l{sp['l']}")
        return rs_start(bufs[:sp["n"]], sp["items"], sp["l"], sibs=bufs[sp["n"]:])

    def rs_finish(fl, after):
        pend, n = fl["pend"], len(fl["gs"])
        bufs = _split_wait(pend[0], pend[1], pend[2], after, _rs_chip_copies, f"rs_wait_l{fl['l']}_{fl['items'][0]}")
        rs_end(fl["gs"], fl["sibs"], bufs[n:], fl["items"], fl["l"])

    sq, dx, dfsh, dfsc, dgfin, dy, dgv = _loss_head(xcur, g_norm_final[None], fmod[0], fmod[1], tgt, (subs[-1]["y"], subs[-1]["gv"]))
    loss_part = 0.5 / D * jnp.sum(sq)
    G = {n: [None] * L for n in _SMALL}
    dmods = [None] * L
    in_flight = None
    sib_flight = None
    tok = None
    for l in reversed(range(L)):
        gr = [None] * 3
        for j in reversed(range(3)):
            k = 3 * l + j
            sb, d = subs[k], lw[l]
            prev = (subs[k - 1]["y"], subs[k - 1]["gv"]) if k > 0 else None
            g_vec = sb["g"] if tok is None else sb["g"] + tok
            tok = None
            if j == 1:
                dx, pv, gr[j] = _mix_bwd(dx, dy, sb["saved"], g_vec, sb["scale"], prev, d["wmix"], w_dw_full[l], g_conv_ln[l][None], b_conv_ln[l][None],
                                         d["wgp"], d["bgp"], g_gla_norm[l], d["wout"], n_cols, n_pad, sb["tag"])
            else:
                w4, wo = (d["wi1"], d["wo1"]) if j == 0 else (d["wi2"], d["wo2"])
                dx, pv, gr[j] = _ffn_bwd(dx, dy, sb["saved"], g_vec, sb["scale"], prev, w4, wo, sb["tag"])
            gr[j]["dgv"] = dgv
            dy, dgv = pv if pv else (None, None)
            if j == 2 and sib_flight is not None:
                in_flight = sib_finish(sib_flight, dx)
                sib_flight = None
                tok = in_flight["pend"][3][0, 0]
            if j == 1 and in_flight is not None:
                rs_finish(in_flight, dx)
                in_flight = None
            if j == 1 and l == 0:
                in_flight = rs_start([gr[1]["dw_in"], gr[1]["dw_out"], gr[2]["dw_in"], gr[2]["dw_out"]], [2, 3, 4, 5], l)
                tok = in_flight["pend"][3][0, 0]
        g1, g2, g3 = gr
        if l > 0:
            sib_flight = sib_start([g1["dw_in"], g1["dw_out"], g2["dw_in"], g2["dw_out"], g3["dw_in"], g3["dw_out"]], list(range(len(_BIG))), l)
            tok = sib_flight["pend"][3][0, 0]
        else:
            rs_finish(in_flight, dx)
            in_flight = None
            last_grads = [g1["dw_in"], g1["dw_out"]]
        dmods[l] = jnp.concatenate([g1["dshift"], g1["dscale"], 0.5 * g1["dgv"], g2["dshift"], g2["dscale"], g2["dgv"],
                                    g3["dshift"], g3["dscale"], 0.5 * g3["dgv"]], axis=1)[0]
        G["g_norm_ffn1"][l], G["g_norm_ffn2"][l], G["g_norm_mix"][l] = g1["dg"][0], g3["dg"][0], g2["dg"][0]
        G["w_dw"][l], G["b_dw"][l], G["g_conv_ln"][l], G["b_conv_ln"][l] = g2["dw_dw"], g2["db_dw"][0], g2["dg_ln"][0], g2["db_ln"][0]
        G["w_gate_up"][l], G["b_gate"][l], G["g_gla_norm"][l] = g2["dw_gate"], g2["db_gate"], g2["dgn"]
    grad_x = dx[None]
    gsm = {}

    small = [jnp.stack(G[n]).reshape(-1) for n in _SMALL if n != 'g_norm_final'] + [dgfin[0]]
    dmod_vec = jnp.concatenate(dmods + [dfsh[0], dfsc[0]])
    n_mod_vec = dmod_vec.shape[0]
    vec = jnp.concatenate([dmod_vec] + small + [loss_part[None]])
    n_vec = vec.shape[0]
    vec = jnp.pad(vec, (0, (-n_vec) % (8 * LANE)))
    vec_all = _ag_small(vec.reshape(8, -1), "ag_small_grads").reshape(8, -1)
    last = rs_start(last_grads, [0, 1], 0, after=vec_all)
    vec_sum = _rowsum(vec_all + last["pend"][3][0, 0], "sum_small_grads")[0]
    loss = vec_sum[n_vec - 1]
    off = n_mod_vec
    for n in _SMALL:
        shp = {'w_dw': w_dw_full.shape, 'w_gate_up': w_gu_full.shape}.get(n, W[n].shape)
        cnt = 1
        for dd in shp:
            cnt *= dd
        gsm[n] = vec_sum[off:off + cnt].reshape(shp)
        off += cnt
    gsm['w_dw'] = lax.dynamic_slice_in_dim(gsm['w_dw'], s_me * w_dw.shape[2], w_dw.shape[2], 2)
    gsm['w_gate_up'] = lax.dynamic_slice_in_dim(gsm['w_gate_up'], s_me * w_gate_up.shape[2], w_gate_up.shape[2], 2)
    dmod_sum = vec_sum[:n_mod_vec]
    gsm['b_ada'] = dmod_sum[:L * N_MOD * D].reshape(L, N_MOD * D)
    gsm['b_ada_final'] = dmod_sum[L * N_MOD * D:]
    c_t = c_all.T
    dmod_rows = vec_all[:, :n_mod_vec]
    gsm['w_ada'] = jnp.stack([
        _mod_wgrad(c_t, lax.dynamic_slice_in_dim(dmod_rows, l * N_MOD * D + s_me * nsh, nsh, 1), f"dw_ada_{l}") for l in range(L)])
    gsm['w_ada_final'] = _mod_wgrad(c_t, lax.dynamic_slice_in_dim(dmod_rows, L * N_MOD * D + s_me * nfin, nfin, 1), "dw_ada_final")
    gsm.update({n: (f[:, :, :n_cols] if n == 'w_in' else f) for n, f in zip(_BIG[2:], fulls[2:])})

    outs = {}
    small_names = [n for n in _WEIGHTS if W[n].size < 65536]
    for n in [m for m in _WEIGHTS if m not in _BIG[:2]] + list(_BIG[:2]):
        if n in small_names:
            continue
        if n == _BIG[0]:
            rs_finish(last, outs['w_ada_final'][0])
            gsm.update(dict(zip(_BIG[:2], fulls[:2])))
        shp = W[n].shape
        v2 = lambda a: a.reshape(-1, shp[-1])
        from_rs = n in _BIG and n != 'w_in'
        d_, m_, v_, *g_ = _adamw(v2(W[n]), v2(gsm[n]), v2(M1[n]), v2(M2[n]), f"adamw_{n}", copy_grad=from_rs)
        outs[n] = (d_.reshape(shp), m_.reshape(shp), v_.reshape(shp))
        if from_rs:
            gsm[n] = g_[0].reshape(shp)
    flat = lambda dct: jnp.concatenate([dct[n].reshape(-1) for n in small_names])
    n_small = sum(W[n].size for n in small_names)
    v2 = lambda a: jnp.pad(a, (0, (-n_small) % (8 * LANE))).reshape(-1, LANE)
    d_, m_, v_ = _adamw(v2(flat(W)), v2(flat(gsm)), v2(flat(M1)), v2(flat(M2)), "adamw_small")

    def unflat(a):
        res, o = {}, 0
        a = a.reshape(-1)
        for n in small_names:
            res[n] = a[o:o + W[n].size].reshape(W[n].shape)
            o += W[n].size
        return res

    for n, dd, mm, vv in zip(small_names, unflat(d_).values(), unflat(m_).values(), unflat(v_).values()):
        outs[n] = (dd, mm, vv)

    return (loss, grad_x, *[gsm[n] for n in _WEIGHTS], *[outs[n][0] for n in _WEIGHTS], *[outs[n][1] for n in _WEIGHTS], *[outs[n][2] for n in _WEIGHTS])
```

```python
import jax
import jax.numpy as jnp
from jax import lax
from jax.experimental import pallas as pl
from jax.experimental.pallas import tpu as pltpu

F32 = jnp.float32
BF16 = jnp.bfloat16

CHUNK = 64
HEADS = 4
DK = 64
DV = 128
DKP = 128
GATE_RANK = 16
GATE_TAU = 16.0
N_MOD = 9
EPS = 1e-6
ADAM_LR = 0.001
ADAM_B1 = 0.9
ADAM_B2 = 0.999
ADAM_EPS = 1e-08
ADAM_WD = 0.01
ADAM_STEP = 10

LANE = 128
HALO = 32
VMEM_LIMIT = 52 * 1024 * 1024
MESH = pl.DeviceIdType.MESH
N_CHIPS = 4

D_CONV = 512
D_GLA = HEADS * DV
ZC_Q = 0
ZC_K = ZC_Q + HEADS * DKP
ZC_V = ZC_K + HEADS * DKP
ZC_R = ZC_V + D_GLA
ZC_A = ZC_R + D_GLA
ZC_B = ZC_A + D_CONV
ZC_G = ZC_B + D_CONV
Z_COLS = ZC_G + LANE
Z_GLA = ZC_A


def _div(n, target, mult):
    best = None
    d = mult
    while d <= min(n, target):
        if n % d == 0:
            best = d
        d += mult
    return n if best is None else best


def _cp(sem=None, **kw):
    return pltpu.CompilerParams(dimension_semantics=sem, vmem_limit_bytes=VMEM_LIMIT, **kw)


def _resident(shape, index_map):
    return pl.BlockSpec(shape, index_map, pipeline_mode=pl.Buffered(1))


def _sigmoid(x):
    return 0.5 * jnp.tanh(0.5 * x) + 0.5


def _dot(a, b):
    return jnp.dot(a.astype(BF16), b.astype(BF16), preferred_element_type=F32)


def _dot_nt(a, b):
    return lax.dot_general(a.astype(BF16), b.astype(BF16), (((1,), (1,)), ((), ())), preferred_element_type=F32)


def _dot_tn(a, b):
    return lax.dot_general(a.astype(BF16), b.astype(BF16), (((0,), (0,)), ((), ())), preferred_element_type=F32)


def _dot_exact(a, b):
    return jnp.dot(a, b, preferred_element_type=F32, precision=lax.Precision.HIGHEST)


def _norm_rows(xv, g, shift, scale):
    r = lax.rsqrt(jnp.mean(xv * xv, axis=-1, keepdims=True) + EPS)
    return (xv * r) * g * (1.0 + scale) + shift


def _loss_head(x, g, shift, scale, tgt, prev):
    S, D = x.shape
    tm = _div(S, 512, 8)

    def body(x_ref, g_ref, sh_ref, sc_ref, t_ref, y_ref, gvp_ref, sq_ref, dx_ref, dsh_ref, dsc_ref, dg_ref, dy_ref, dgv_ref):
        sums = (sq_ref, dsh_ref, dsc_ref, dg_ref, dgv_ref)

        @pl.when(pl.program_id(0) == 0)
        def _():
            for o in sums:
                o[...] = jnp.zeros_like(o)

        xv = x_ref[...]
        e = _norm_rows(xv, g_ref[...], sh_ref[...], sc_ref[...]) - t_ref[...]
        dx, dsh, dsc, dg, dy, dgv = _normmod_bwd_rows(xv, e * (1.0 / D), None, g_ref[...], sc_ref[...], (y_ref[...], gvp_ref[...]))
        dx_ref[...] = dx
        dy_ref[...] = dy
        for o, v in zip(sums, (jnp.sum(e * e, axis=0, keepdims=True), dsh, dsc, dg, dgv)):
            o[...] += v

    row = pl.BlockSpec((tm, D), lambda i: (i, 0))
    vec = pl.BlockSpec((1, D), lambda i: (0, 0))
    vs = jax.ShapeDtypeStruct((1, D), F32)
    return pl.pallas_call(
        body, grid=(S // tm,), in_specs=[row, vec, vec, vec, row, row, vec], out_specs=[vec, row, vec, vec, vec, row, vec],
        out_shape=[vs, jax.ShapeDtypeStruct((S, D), F32), vs, vs, vs, jax.ShapeDtypeStruct((S, D), BF16), vs],
        compiler_params=_cp(("arbitrary",)), name="loss_head",
    )(x, g, shift, scale, tgt, *prev)


def _normmod_bwd_rows(xv, dh, dres, gv, sc, prev):
    r = lax.rsqrt(jnp.mean(xv * xv, axis=-1, keepdims=True) + EPS)
    xh = xv * r
    dsh = jnp.sum(dh, axis=0, keepdims=True)
    dsc = jnp.sum(dh * (xh * gv), axis=0, keepdims=True)
    dn = dh * (1.0 + sc)
    dg = jnp.sum(dn * xh, axis=0, keepdims=True)
    dxh = dn * gv
    dx = r * (dxh - xh * jnp.mean(dxh * xh, axis=-1, keepdims=True))
    if dres is not None:
        dx = dx + dres
    if prev is None:
        return dx, dsh, dsc, dg
    y, gvp = prev
    return dx, dsh, dsc, dg, (gvp * dx).astype(BF16), jnp.sum(dx * y, axis=0, keepdims=True)


def _mm(pairs, out_dtype, name, nt=False):
    M = pairs[0][0].shape[0]
    N = pairs[0][1].shape[0] if nt else pairs[0][1].shape[1]
    ktot = sum(a.shape[1] for a, _, _ in pairs)
    tm = _div(M, 512 if ktot <= 4096 else 256, 8)
    n = len(pairs)

    def body(*refs):
        o_ref = refs[2 * n]
        dot = _dot_nt if nt else _dot
        acc = dot(refs[0][...], refs[1][...])
        for p in range(1, n):
            acc = acc + dot(refs[2 * p][...], refs[2 * p + 1][...])
        o_ref[...] = acc.astype(o_ref.dtype)

    ins, args = [], []
    for a, b, blk in pairs:
        k = a.shape[1]
        ins.append(pl.BlockSpec((tm, k), lambda i: (i, 0)))
        ins.append(_resident((N, k), lambda i, blk=blk: (0, blk)) if nt else _resident((k, N), lambda i: (0, 0)))
        args += [a, b]
    return pl.pallas_call(
        body, grid=(M // tm,), in_specs=ins, out_specs=pl.BlockSpec((tm, N), lambda i: (i, 0)),
        out_shape=jax.ShapeDtypeStruct((M, N), out_dtype), compiler_params=_cp(("parallel",)), name=name,
    )(*args)


TN_ROWS = 2048


def _mm_tn(a, g, name):
    S, Ka = a.shape
    N = g.shape[1]
    tk = _div(Ka, 1408, LANE)
    tn = _div(N, 1408, LANE)
    ts = _div(S, TN_ROWS, 8)

    def body(a_ref, g_ref, o_ref):
        @pl.when(pl.program_id(2) == 0)
        def _():
            o_ref[...] = jnp.zeros_like(o_ref)

        o_ref[...] += _dot_tn(a_ref[...], g_ref[...])

    return pl.pallas_call(
        body, grid=(Ka // tk, N // tn, S // ts),
        in_specs=[pl.BlockSpec((ts, tk), lambda i, j, s: (s, i)), pl.BlockSpec((ts, tn), lambda i, j, s: (s, j))],
        out_specs=pl.BlockSpec((tk, tn), lambda i, j, s: (i, j)),
        out_shape=jax.ShapeDtypeStruct((Ka, N), F32),
        compiler_params=_cp(("parallel", "parallel", "arbitrary")), name=name,
    )(a, g)


def _mm_tn_two(a0, a1, g, name):
    S, K = a0.shape
    N = g.shape[1]
    ts = _div(S, TN_ROWS, 8)

    def body(a0_ref, a1_ref, g_ref, o_ref):
        i = pl.program_id(0)

        @pl.when(pl.program_id(1) == 0)
        def _():
            o_ref[...] = jnp.zeros_like(o_ref)

        @pl.when(i == 0)
        def _():
            o_ref[...] += _dot_tn(a0_ref[...], g_ref[...])

        @pl.when(i == 1)
        def _():
            o_ref[...] += _dot_tn(a1_ref[...], g_ref[...])

    return pl.pallas_call(
        body, grid=(2, S // ts),
        in_specs=[pl.BlockSpec((ts, K), lambda i, s: (jnp.where(i == 0, s, 0), 0)),
                  pl.BlockSpec((ts, K), lambda i, s: (jnp.where(i == 1, s, 0), 0)),
                  pl.BlockSpec((ts, N), lambda i, s: (s, 0))],
        out_specs=pl.BlockSpec((K, N), lambda i, s: (i, 0)),
        out_shape=jax.ShapeDtypeStruct((2 * K, N), F32),
        compiler_params=_cp(("parallel", "arbitrary")), name=name,
    )(a0, a1, g)


def _swiglu(gt, up):
    return gt * _sigmoid(gt) * up


def _ffn_up(h, w4, name, norm=None):
    S, D = h.shape
    ns, _, C = w4.shape
    hs = ns // 2
    tm = _div(S, 256, 8)
    nn = 3 if norm else 0

    def body(h_ref, w_ref, *rest):
        z_ref, a_ref = rest[nn:nn + 2]
        if norm:
            hv = _norm_rows(h_ref[...], rest[0][...], rest[1][...], rest[2][...]).astype(BF16)
            rest[nn + 2][...] = hv
        else:
            hv = h_ref[...]
        for s in range(hs):
            gt = _dot(hv, w_ref[s])
            up = _dot(hv, w_ref[hs + s])
            sg = _sigmoid(gt)
            silu = gt * sg
            z_ref[:, s * C:(s + 1) * C] = (up * (sg * (1.0 + gt * (1.0 - sg)))).astype(BF16)
            z_ref[:, (hs + s) * C:(hs + s + 1) * C] = silu.astype(BF16)
            a_ref[:, s * C:(s + 1) * C] = (silu * up).astype(BF16)

    row = pl.BlockSpec((tm, D), lambda i: (i, 0))
    vec = pl.BlockSpec((1, D), lambda i: (0, 0))
    return pl.pallas_call(
        body, grid=(S // tm,), in_specs=[row, _resident((ns, D, C), lambda i: (0, 0, 0))] + [vec] * nn,
        out_specs=[pl.BlockSpec((tm, ns * C), lambda i: (i, 0)), pl.BlockSpec((tm, hs * C), lambda i: (i, 0))] + [row] * (nn // 3),
        out_shape=[jax.ShapeDtypeStruct((S, ns * C), BF16), jax.ShapeDtypeStruct((S, hs * C), BF16)] + [jax.ShapeDtypeStruct((S, D), BF16)] * (nn // 3),
        compiler_params=_cp(("parallel",)), name=name,
    )(h, w4, *(norm or ()))


def _resid_outputs(y, x_ref, gv_ref, nxt_refs, out_refs):
    out_refs[0][...] = y
    xn = x_ref[...] + gv_ref[...] * y
    out_refs[1][...] = xn
    if nxt_refs:
        out_refs[2][...] = _norm_rows(xn, nxt_refs[0][...], nxt_refs[1][...], nxt_refs[2][...]).astype(BF16)


def _ffn_down(act, wo, x, gv, nxt, name):
    S = act.shape[0]
    Fd, D = wo.shape
    tm = _div(S, 512, 8)
    nn = 3 if nxt else 0

    def body(a_ref, w_ref, x_ref, gv_ref, *rest):
        _resid_outputs(_dot(a_ref[...], w_ref[...]), x_ref, gv_ref, rest[:nn], rest[nn:])

    row = pl.BlockSpec((tm, D), lambda i: (i, 0))
    vec = pl.BlockSpec((1, D), lambda i: (0, 0))
    os_ = jax.ShapeDtypeStruct((S, D), F32)
    return pl.pallas_call(
        body, grid=(S // tm,),
        in_specs=[pl.BlockSpec((tm, Fd), lambda i: (i, 0)), _resident((Fd, D), lambda i: (0, 0)), row, vec] + [vec] * nn,
        out_specs=[row, row] + [row] * (nn // 3), out_shape=[os_, os_] + [jax.ShapeDtypeStruct((S, D), BF16)] * (nn // 3),
        compiler_params=_cp(("parallel",)), name=name,
    )(act, wo, x, gv, *(nxt or ()))


def _dw_ffn_in(h, dz, ns, name):
    S, D = h.shape
    C = dz.shape[1] // ns
    ts = _div(S, TN_ROWS, 8)

    def body(h_ref, g_ref, o_ref):
        @pl.when(pl.program_id(1) == 0)
        def _():
            o_ref[...] = jnp.zeros_like(o_ref)

        o_ref[...] += _dot_tn(h_ref[...], g_ref[...])

    return pl.pallas_call(
        body, grid=(ns, S // ts),
        in_specs=[pl.BlockSpec((ts, D), lambda j, s: (s, 0)), pl.BlockSpec((ts, C), lambda j, s: (s, j))],
        out_specs=pl.BlockSpec((None, D, C), lambda j, s: (j, 0, 0)), out_shape=jax.ShapeDtypeStruct((ns, D, C), F32),
        compiler_params=_cp(("parallel", "arbitrary")), name=name,
    )(h, dz)


def _dh_normmod_bwd(pairs, x, dres, g, scale, prev, name):
    S, D = x.shape
    tm = _div(S, 256, 8)
    n = len(pairs)
    with_prev = prev is not None

    def body(*refs):
        refs = list(refs)
        mm = refs[:2 * n]
        x_ref, dr_ref, g_ref, sc_ref = refs[2 * n:2 * n + 4]
        outs = refs[2 * n + 4 + 2 * with_prev:]

        @pl.when(pl.program_id(0) == 0)
        def _():
            for o in outs[1:4] + outs[5:]:
                o[...] = jnp.zeros_like(o)

        dh = _dot_nt(mm[0][...], mm[1][...])
        for p in range(1, n):
            dh = dh + _dot_nt(mm[2 * p][...], mm[2 * p + 1][...])
        pv = (refs[2 * n + 4][...], refs[2 * n + 5][...]) if with_prev else None
        res = _normmod_bwd_rows(x_ref[...], dh, dr_ref[...], g_ref[...], sc_ref[...], pv)
        outs[0][...] = res[0]
        for o, v in zip(outs[1:4], res[1:4]):
            o[...] += v
        if with_prev:
            outs[4][...] = res[4]
            outs[5][...] += res[5]

    row = pl.BlockSpec((tm, D), lambda i: (i, 0))
    vec = pl.BlockSpec((1, D), lambda i: (0, 0))
    ins, args = [], []
    for a, b, a_blk, b_blk, k in pairs:
        ins.append(pl.BlockSpec((tm, k), lambda i, a_blk=a_blk: (i, a_blk)))
        ins.append(_resident((None, D, k), lambda i, b_blk=b_blk: (b_blk, 0, 0)) if b.ndim == 3 else _resident((D, k), lambda i, b_blk=b_blk: (0, b_blk)))
        args += [a, b]
    ins += [row, row, vec, vec] + [row, vec] * with_prev
    args += [x, dres, g, scale] + (list(prev) if with_prev else [])
    vs = jax.ShapeDtypeStruct((1, D), F32)
    return pl.pallas_call(
        body, grid=(S // tm,), in_specs=ins, out_specs=[row, vec, vec, vec] + [row, vec] * with_prev,
        out_shape=[jax.ShapeDtypeStruct((S, D), F32), vs, vs, vs] + [jax.ShapeDtypeStruct((S, D), BF16), vs] * with_prev,
        compiler_params=_cp(("arbitrary",)), name=name,
    )(*args)


def _ffn_bwd_core(dy, wo, z, w4, x, dres, g, scale, prev, name):
    S, D = x.shape
    Fd = wo.shape[0]
    ns, _, C = w4.shape
    tm = _div(S, 256, 8)
    with_prev = prev is not None

    def body(dy_ref, wo_ref, q_ref, p_ref, w4_ref, x_ref, dr_ref, g_ref, sc_ref, *rest):
        outs = rest[2 * with_prev:]
        dz_ref, outs = outs[0], outs[1:]

        @pl.when(pl.program_id(0) == 0)
        def _():
            for o in outs[1:4] + outs[5:]:
                o[...] = jnp.zeros_like(o)

        da = _dot_nt(dy_ref[...], wo_ref[...])
        dz_ref[:, :Fd] = (da * q_ref[...].astype(F32)).astype(BF16)
        dz_ref[:, Fd:] = (da * p_ref[...].astype(F32)).astype(BF16)
        dh = _dot_nt(dz_ref[:, 0:C], w4_ref[0])
        for s in range(1, ns):
            dh = dh + _dot_nt(dz_ref[:, s * C:(s + 1) * C], w4_ref[s])
        pv = (rest[0][...], rest[1][...]) if with_prev else None
        res = _normmod_bwd_rows(x_ref[...], dh, dr_ref[...], g_ref[...], sc_ref[...], pv)
        outs[0][...] = res[0]
        for o, v in zip(outs[1:4], res[1:4]):
            o[...] += v
        if with_prev:
            outs[4][...] = res[4]
            outs[5][...] += res[5]

    row = pl.BlockSpec((tm, D), lambda i: (i, 0))
    vec = pl.BlockSpec((1, D), lambda i: (0, 0))
    wide = pl.BlockSpec((tm, 2 * Fd), lambda i: (i, 0))
    vs = jax.ShapeDtypeStruct((1, D), F32)
    return pl.pallas_call(
        body, grid=(S // tm,),
        in_specs=[row, _resident((Fd, D), lambda i: (0, 0)), pl.BlockSpec((tm, Fd), lambda i: (i, 0)), pl.BlockSpec((tm, Fd), lambda i: (i, 1)),
                  _resident((ns, D, C), lambda i: (0, 0, 0)), row, row, vec, vec] + [row, vec] * with_prev,
        out_specs=[wide, row, vec, vec, vec] + [row, vec] * with_prev,
        out_shape=[jax.ShapeDtypeStruct((S, 2 * Fd), BF16), jax.ShapeDtypeStruct((S, D), F32), vs, vs, vs] + [jax.ShapeDtypeStruct((S, D), BF16), vs] * with_prev,
        compiler_params=_cp(("arbitrary",)), name=name,
    )(dy, wo, z, z, w4, x, dres, g, scale, *(prev or ()))


def _mix_out(yconv, ygla, wout, x, gv, nxt, name):
    S, Kc = yconv.shape
    Kg = ygla.shape[1]
    D = wout.shape[1]
    tm = _div(S, 512, 8)
    nn = 3 if nxt else 0

    def body(a_ref, b_ref, w_ref, x_ref, gv_ref, *rest):
        y = _dot(a_ref[...], w_ref[0:Kc, :]) + _dot(b_ref[...], w_ref[Kc:Kc + Kg, :])
        _resid_outputs(y, x_ref, gv_ref, rest[:nn], rest[nn:])

    row = pl.BlockSpec((tm, D), lambda i: (i, 0))
    vec = pl.BlockSpec((1, D), lambda i: (0, 0))
    os_ = jax.ShapeDtypeStruct((S, D), F32)
    return pl.pallas_call(
        body, grid=(S // tm,),
        in_specs=[pl.BlockSpec((tm, Kc), lambda i: (i, 0)), pl.BlockSpec((tm, Kg), lambda i: (i, 0)), _resident((Kc + Kg, D), lambda i: (0, 0)), row,
                  vec] + [vec] * nn,
        out_specs=[row, row] + [row] * (nn // 3), out_shape=[os_, os_] + [jax.ShapeDtypeStruct((S, D), BF16)] * (nn // 3),
        compiler_params=_cp(("parallel",)), name=name,
    )(yconv, ygla, wout, x, gv, *(nxt or ()))


def _ln_parts(yc, g, b):
    mu = jnp.mean(yc, axis=-1, keepdims=True)
    xc = yc - mu
    rs = lax.rsqrt(jnp.mean(xc * xc, axis=-1, keepdims=True) + EPS)
    xh = xc * rs
    return xh, rs, xh * g + b


SUB = 8
CONV_ROWS = 32


def _shifted_copies(ext8, rows):
    for b in range(1, SUB):
        ext8[b, pl.ds(0, rows - SUB), :] = ext8[0, pl.ds(b, rows - SUB), :]


def _tap(o):
    return o % SUB, o - o % SUB


def _conv_fwd(z, w_dw, b_dw, g_ln, b_ln, name):
    S = z.shape[0]
    W, C = w_dw.shape
    ts = _div(S, 512, HALO)
    hb = ts // HALO
    off = HALO - (W - 1)
    ca, cb = ZC_A // C, ZC_B // C
    rb = 2 * CONV_ROWS

    def body(a_ref, b_ref, pa_ref, pb_ref, w_ref, bd_ref, g_ref, bl_ref, u_ref, yc_ref, o_ref, ext8):
        keep = (pl.program_id(0) > 0).astype(F32)
        u = a_ref[...] * _sigmoid(b_ref[...])
        ext8[0, pl.ds(0, HALO), :] = pa_ref[...] * _sigmoid(pb_ref[...]) * keep
        ext8[0, pl.ds(HALO, ts), :] = u
        u_ref[...] = u
        _shifted_copies(ext8, ts + HALO)

        for lg in range(C // LANE):
            lanes = pl.ds(lg * LANE, LANE)
            taps = [jnp.broadcast_to(w_ref[pl.ds(j, 1), lanes], (SUB, LANE)) for j in range(W)]
            bias = jnp.broadcast_to(bd_ref[:, lanes], (SUB, LANE))

            def sub(i, carry, lanes=lanes, taps=taps, bias=bias):
                r0 = pl.multiple_of(i * rb, rb)
                accs = [bias] * (rb // SUB)
                for j in range(W):
                    b, a = _tap(off + j)
                    for r in range(rb // SUB):
                        accs[r] = accs[r] + taps[j] * ext8[b, pl.ds(r0 + a + r * SUB, SUB), lanes]
                for r in range(rb // SUB):
                    yc_ref[pl.ds(r0 + r * SUB, SUB), lanes] = accs[r]
                return carry

            lax.fori_loop(0, ts // rb, sub, 0)
        _, _, ln = _ln_parts(yc_ref[...], g_ref[...], bl_ref[...])
        o_ref[...] = (ln * _sigmoid(ln)).astype(BF16)

    cur = lambda col: pl.BlockSpec((ts, C), lambda i: (i, col))
    prev = lambda col: pl.BlockSpec((HALO, C), lambda i: (jnp.maximum(i * hb - 1, 0), col))
    vec = pl.BlockSpec((1, C), lambda i: (0, 0))
    row = pl.BlockSpec((ts, C), lambda i: (i, 0))
    fs = jax.ShapeDtypeStruct((S, C), F32)
    return pl.pallas_call(
        body, grid=(S // ts,),
        in_specs=[cur(ca), cur(cb), prev(ca), prev(cb), pl.BlockSpec((W, C), lambda i: (0, 0)), vec, vec, vec],
        out_specs=[row, row, row], out_shape=[fs, fs, jax.ShapeDtypeStruct((S, C), BF16)],
        scratch_shapes=[pltpu.VMEM((SUB, ts + HALO, C), F32)],
        compiler_params=_cp(("parallel",)), name=name,
    )(z, z, z, z, w_dw, b_dw, g_ln, b_ln)


def _conv_bwd(dycat, z, u, yc, w_dw, g_ln, b_ln, name):
    S = z.shape[0]
    W, C = w_dw.shape
    ts = _div(S, 512, HALO)
    hb = ts // HALO
    nblk = S // ts
    off = HALO - (W - 1)
    ca, cb = ZC_A // C, ZC_B // C
    rb = CONV_ROWS

    def ln_silu_bwd(dy, ycv, g, b):
        xh, rs, ln = _ln_parts(ycv, g, b)
        sl = _sigmoid(ln)
        dln = dy * (sl * (1.0 + ln * (1.0 - sl)))
        dxh = dln * g
        dyc = rs * (dxh - jnp.mean(dxh, axis=-1, keepdims=True) - xh * jnp.mean(dxh * xh, axis=-1, keepdims=True))
        return dyc, dln, xh

    def body(dy_ref, ndy_ref, yc_ref, nyc_ref, u_ref, pu_ref, a_ref, b_ref, w_ref, g_ref, bl_ref,
             dab_ref, dw_ref, dbd_ref, dg_ref, dbl_ref, uext8, dext8, dwacc):
        i = pl.program_id(0)

        @pl.when(i == 0)
        def _():
            dwacc[...] = jnp.zeros_like(dwacc)
            dbd_ref[...] = jnp.zeros_like(dbd_ref)
            dg_ref[...] = jnp.zeros_like(dg_ref)
            dbl_ref[...] = jnp.zeros_like(dbl_ref)

        g = g_ref[...]
        bl = bl_ref[...]
        dyc, dln, xh = ln_silu_bwd(dy_ref[...], yc_ref[...], g, bl)
        ndyc, _, _ = ln_silu_bwd(ndy_ref[...], nyc_ref[...], g, bl)
        dg_ref[...] += jnp.sum(dln * xh, axis=0, keepdims=True)
        dbl_ref[...] += jnp.sum(dln, axis=0, keepdims=True)
        dbd_ref[...] += jnp.sum(dyc, axis=0, keepdims=True)
        dext8[0, pl.ds(0, ts), :] = dyc
        dext8[0, pl.ds(ts, HALO), :] = ndyc * (i < nblk - 1).astype(F32)
        uext8[0, pl.ds(0, HALO), :] = pu_ref[...] * (i > 0).astype(F32)
        uext8[0, pl.ds(HALO, ts), :] = u_ref[...]
        _shifted_copies(dext8, ts + HALO)
        _shifted_copies(uext8, ts + HALO)

        def sub(k, carry):
            r0 = pl.multiple_of(k * rb, rb)
            rows = pl.ds(r0, rb)
            dyt = dext8[0, rows, :]
            du = jnp.zeros((rb, C), F32)
            for j in range(W):
                b, a = _tap(W - 1 - j)
                du = du + w_ref[pl.ds(j, 1), :] * dext8[b, pl.ds(r0 + a, rb), :]
                b, a = _tap(off + j)
                p = dyt * uext8[b, pl.ds(r0 + a, rb), :]
                part = p[0:SUB]
                for q in range(1, rb // SUB):
                    part = part + p[q * SUB:(q + 1) * SUB]
                dwacc[j] += part
            sb = _sigmoid(b_ref[rows, :])
            dab_ref[rows, 0:C] = (du * sb).astype(BF16)
            dab_ref[rows, C:2 * C] = (du * a_ref[rows, :] * sb * (1.0 - sb)).astype(BF16)
            return carry

        lax.fori_loop(0, ts // rb, sub, 0)

        @pl.when(i == nblk - 1)
        def _():
            for j in range(W):
                dw_ref[pl.ds(j, 1), :] = jnp.sum(dwacc[j], axis=0, keepdims=True)

    row = pl.BlockSpec((ts, C), lambda i: (i, 0))
    nxt = pl.BlockSpec((HALO, C), lambda i: (jnp.minimum((i + 1) * hb, S // HALO - 1), 0))
    prv = pl.BlockSpec((HALO, C), lambda i: (jnp.maximum(i * hb - 1, 0), 0))
    vec = pl.BlockSpec((1, C), lambda i: (0, 0))
    wsp = pl.BlockSpec((W, C), lambda i: (0, 0))
    vs = jax.ShapeDtypeStruct((1, C), F32)
    return pl.pallas_call(
        body, grid=(nblk,),
        in_specs=[row, nxt, row, nxt, row, prv, pl.BlockSpec((ts, C), lambda i: (i, ca)), pl.BlockSpec((ts, C), lambda i: (i, cb)), wsp, vec, vec],
        out_specs=[pl.BlockSpec((ts, 2 * C), lambda i: (i, 0)), wsp, vec, vec, vec],
        out_shape=[jax.ShapeDtypeStruct((S, 2 * C), BF16), jax.ShapeDtypeStruct((W, C), F32), vs, vs, vs],
        scratch_shapes=[pltpu.VMEM((SUB, ts + HALO, C), F32), pltpu.VMEM((SUB, ts + HALO, C), F32), pltpu.VMEM((W, SUB, C), F32)],
        compiler_params=_cp(("arbitrary",)), name=name,
    )(dycat, dycat, yc, yc, u, u, z, z, w_dw, g_ln, b_ln)


def _log_gate(zg):
    return (jnp.minimum(zg, 0.0) - jnp.log(1.0 + jnp.exp(-jnp.abs(zg)))) * (1.0 / GATE_TAU)


def _loggate(z, wgp, bgp, name):
    S = z.shape[0]
    N = wgp.shape[1]
    ts = _div(S, 512, 8)

    def body(g_ref, w_ref, b_ref, o_ref):
        o_ref[...] = _log_gate(_dot(g_ref[...], w_ref[...]) + b_ref[...])

    return pl.pallas_call(
        body, grid=(S // ts,),
        in_specs=[pl.BlockSpec((ts, LANE), lambda i: (i, ZC_G // LANE)), pl.BlockSpec((LANE, N), lambda i: (0, 0)), pl.BlockSpec((1, N), lambda i: (0, 0))],
        out_specs=pl.BlockSpec((ts, N), lambda i: (i, 0)), out_shape=jax.ShapeDtypeStruct((S, N), F32),
        compiler_params=_cp(("parallel",)), name=name,
    )(z, wgp, bgp)


def _loggate_bwd(dla, z, wgp, wgp_t, bgp, name):
    S = z.shape[0]
    N = wgp.shape[1]
    ts = _div(S, 512, 8)

    def body(dla_ref, g_ref, w_ref, wt_ref, b_ref, dg_ref, dw_ref, db_ref):
        @pl.when(pl.program_id(0) == 0)
        def _():
            dw_ref[...] = jnp.zeros_like(dw_ref)
            db_ref[...] = jnp.zeros_like(db_ref)

        glr = g_ref[...]
        zg = _dot(glr, w_ref[...]) + b_ref[...]
        dzg = dla_ref[...] * (1.0 / GATE_TAU) * (1.0 - _sigmoid(zg))
        dg_ref[...] = _dot(dzg, wt_ref[...]).astype(BF16)
        dw_ref[...] += _dot_tn(glr, dzg)
        db_ref[...] += jnp.sum(dzg, axis=0, keepdims=True)

    return pl.pallas_call(
        body, grid=(S // ts,),
        in_specs=[pl.BlockSpec((ts, N), lambda i: (i, 0)), pl.BlockSpec((ts, LANE), lambda i: (i, ZC_G // LANE)),
                  pl.BlockSpec((LANE, N), lambda i: (0, 0)), pl.BlockSpec((N, LANE), lambda i: (0, 0)), pl.BlockSpec((1, N), lambda i: (0, 0))],
        out_specs=[pl.BlockSpec((ts, LANE), lambda i: (i, 0)), pl.BlockSpec((LANE, N), lambda i: (0, 0)), pl.BlockSpec((1, N), lambda i: (0, 0))],
        out_shape=[jax.ShapeDtypeStruct((S, LANE), BF16), jax.ShapeDtypeStruct((LANE, N), F32), jax.ShapeDtypeStruct((1, N), F32)],
        compiler_params=_cp(("arbitrary",)), name=name,
    )(dla, z, wgp, wgp_t, bgp)


def _bdot(a, b, ca, cb):
    return lax.dot_general(a.astype(BF16), b.astype(BF16), (((ca,), (cb,)), ((0,), (0,))), preferred_element_type=F32)


def _bdot_exact(a, b):
    return lax.dot_general(a, b, (((2,), (1,)), ((0,), (0,))), preferred_element_type=F32, precision=lax.Precision.HIGHEST)


def _tiles(ref, cpb):
    return jnp.stack([ref[pl.ds(c * CHUNK, CHUNK), pl.ds(h * LANE, LANE)] for c in range(cpb) for h in range(HEADS)])


def _tri_masks(n):
    ri = lax.broadcasted_iota(jnp.int32, (n, CHUNK, CHUNK), 1)
    ci = lax.broadcasted_iota(jnp.int32, (n, CHUNK, CHUNK), 2)
    return ri >= ci, (ri >= ci).astype(F32), (ri <= ci).astype(F32)


def _chunk_fwd_terms(q, k, la, tril):
    bc = _bdot_exact(tril, la)
    bend = jnp.sum(la, axis=1, keepdims=True)
    eb = jnp.exp(bc)
    enb = jnp.exp(-bc)
    ee = jnp.exp(bend - bc)
    qs = q * (DK ** -0.5)
    return bend, eb, enb, ee, qs * eb, qs * enb, k * enb, k * eb, k * ee


def _gla_fwd(z, la, gn, name):
    S = z.shape[0]
    W = HEADS * LANE
    tb = _div(S, 512, CHUNK)
    cpb = tb // CHUNK

    def body(q_ref, k_ref, v_ref, r_ref, la_ref, gn_ref, o_ref, sp_ref, y_ref, st):
        @pl.when(pl.program_id(0) == 0)
        def _():
            st[...] = jnp.zeros_like(st)

        tri, tril, _ = _tri_masks(cpb * HEADS)
        q, k, v, rv, lav = (_tiles(r, cpb) for r in (q_ref, k_ref, v_ref, r_ref, la_ref))
        bend, _, _, _, qf, qb, kb, kf, ke = _chunk_fwd_terms(q, k, lav, tril)
        att = jnp.where(tri, _bdot(qf, kb, 2, 2), _bdot(qb, kf, 2, 2))
        o_intra = _bdot(att, v, 2, 1)
        u = _bdot(v, ke, 1, 1)
        gdec = jnp.exp(bend)
        s_prev = [None] * (cpb * HEADS)
        for h in range(HEADS):
            s = st[h]
            for c in range(cpb):
                b = c * HEADS + h
                s_prev[b] = s
                s = s * gdec[b] + u[b]
            st[h] = s
        s_prev = jnp.stack(s_prev)
        o = o_intra + _bdot(qf, s_prev, 2, 2)
        rms = lax.rsqrt(jnp.mean(o * o, axis=-1, keepdims=True) + EPS)
        gn = jnp.stack([gn_ref[pl.ds(h, 1), :] for _ in range(cpb) for h in range(HEADS)])
        y = (o * rms * gn * (rv * _sigmoid(rv))).astype(BF16)
        for c in range(cpb):
            for h in range(HEADS):
                b = c * HEADS + h
                rows, ln = pl.ds(c * CHUNK, CHUNK), pl.ds(h * LANE, LANE)
                o_ref[rows, ln] = o[b]
                y_ref[rows, ln] = y[b]
                sp_ref[h, c] = s_prev[b]

    zb = lambda base: pl.BlockSpec((tb, W), lambda i: (i, base // W))
    hb_ = pl.BlockSpec((tb, W), lambda i: (i, 0))
    return pl.pallas_call(
        body, grid=(S // tb,),
        in_specs=[zb(ZC_Q), zb(ZC_K), zb(ZC_V), zb(ZC_R), hb_, pl.BlockSpec((HEADS, DV), lambda i: (0, 0))],
        out_specs=[hb_, pl.BlockSpec((HEADS, cpb, DV, DKP), lambda i: (0, i, 0, 0)), hb_],
        out_shape=[jax.ShapeDtypeStruct((S, D_GLA), F32), jax.ShapeDtypeStruct((HEADS, S // CHUNK, DV, DKP), F32),
                   jax.ShapeDtypeStruct((S, D_GLA), BF16)],
        scratch_shapes=[pltpu.VMEM((HEADS, DV, DKP), F32)],
        compiler_params=_cp(("arbitrary",)), name=name,
    )(z, z, z, z, la, gn)


def _gla_bwd(dycat, z, la, o_raw, sprev, gn, name):
    S = z.shape[0]
    W = HEADS * LANE
    tb = _div(S, 512, CHUNK)
    cpb = tb // CHUNK
    nb = S // tb

    def body(q_ref, k_ref, v_ref, r_ref, la_ref, o_ref, sp_ref, dy_ref, gn_ref, dz_ref, dla_ref, dgn_ref, dst):
        @pl.when(pl.program_id(0) == 0)
        def _():
            dst[...] = jnp.zeros_like(dst)
            dgn_ref[...] = jnp.zeros_like(dgn_ref)

        nt = cpb * HEADS
        tri, tril, triu = _tri_masks(nt)
        q, k, v, rv, lav, o, dy = (_tiles(r, cpb) for r in (q_ref, k_ref, v_ref, r_ref, la_ref, o_ref, dy_ref))
        bend, eb, enb, ee, qf, qb, kb, kf, ke = _chunk_fwd_terms(q, k, lav, tril)
        att = jnp.where(tri, _bdot(qf, kb, 2, 2), _bdot(qb, kf, 2, 2))
        s_prev = jnp.stack([sp_ref[h, c] for c in range(cpb) for h in range(HEADS)])
        gdec = jnp.exp(bend)
        gn = jnp.stack([gn_ref[pl.ds(h, 1), :] for _ in range(cpb) for h in range(HEADS)])
        rms = lax.rsqrt(jnp.mean(o * o, axis=-1, keepdims=True) + EPS)
        oh = o * rms
        sg = _sigmoid(rv)
        sr = rv * sg
        d_r = (dy * oh * gn * (sg * (1.0 + rv * (1.0 - sg)))).astype(BF16)
        dgn = jnp.sum(dy * sr * oh, axis=1, keepdims=True)
        w = dy * sr * gn
        do = rms * (w - oh * jnp.mean(w * oh, axis=-1, keepdims=True))
        p = _bdot(do, qf, 1, 1)
        ds = [None] * nt
        for h in range(HEADS):
            s = dst[h]
            for c in reversed(range(cpb)):
                b = c * HEADS + h
                ds[b] = s
                s = s * gdec[b] + p[b]
            dst[h] = s
            dgn_ref[pl.ds(h, 1), :] += sum(dgn[c * HEADS + h] for c in range(cpb))
        ds = jnp.stack(ds)
        datt = _bdot(do, v, 2, 2)
        daf = jnp.where(tri, datt, 0.0)
        dab = jnp.where(tri, 0.0, datt)
        d_v = (_bdot(att, do, 1, 1) + _bdot(ke, ds, 2, 2)).astype(BF16)
        dke = _bdot(v, ds, 2, 1)
        dqf = _bdot(daf, kb, 2, 1) + _bdot(do, s_prev, 2, 1)
        dkb = _bdot(daf, qf, 1, 1)
        dqb = _bdot(dab, kf, 2, 1)
        dkf = _bdot(dab, qb, 1, 1)
        dg = jnp.sum(ds * s_prev, axis=1, keepdims=True)
        d_q = ((dqf * eb + dqb * enb) * (DK ** -0.5)).astype(BF16)
        d_k = (dkb * enb + dkf * eb + dke * ee).astype(BF16)
        dbc = dqf * qf - dkb * kb - dqb * qb + dkf * kf - dke * ke
        dbend = jnp.sum(dke * ke, axis=1, keepdims=True) + dg * gdec
        dla = _bdot_exact(triu, dbc) + dbend
        for c in range(cpb):
            for h in range(HEADS):
                b = c * HEADS + h
                rows = pl.ds(c * CHUNK, CHUNK)
                for base, val in ((ZC_Q, d_q), (ZC_K, d_k), (ZC_V, d_v), (ZC_R, d_r)):
                    dz_ref[rows, pl.ds(base + h * LANE, LANE)] = val[b]
                dla_ref[rows, pl.ds(h * LANE, LANE)] = dla[b]

    zb = lambda base: pl.BlockSpec((tb, W), lambda i: (nb - 1 - i, base // W))
    hb_ = pl.BlockSpec((tb, W), lambda i: (nb - 1 - i, 0))
    return pl.pallas_call(
        body, grid=(nb,),
        in_specs=[zb(ZC_Q), zb(ZC_K), zb(ZC_V), zb(ZC_R), hb_, hb_,
                  pl.BlockSpec((HEADS, cpb, DV, DKP), lambda i: (0, nb - 1 - i, 0, 0)),
                  pl.BlockSpec((tb, W), lambda i: (nb - 1 - i, 1)),
                  pl.BlockSpec((HEADS, DV), lambda i: (0, 0))],
        out_specs=[pl.BlockSpec((tb, Z_GLA), lambda i: (nb - 1 - i, 0)), hb_, pl.BlockSpec((HEADS, DV), lambda i: (0, 0))],
        out_shape=[jax.ShapeDtypeStruct((S, Z_GLA), BF16), jax.ShapeDtypeStruct((S, HEADS * DKP), F32), jax.ShapeDtypeStruct((HEADS, DV), F32)],
        scratch_shapes=[pltpu.VMEM((HEADS, DV, DKP), F32)],
        compiler_params=_cp(("arbitrary",)), name=name,
    )(z, z, z, z, la, o_raw, sprev, dycat, gn)


def _mod_proj(c_all, w3, layer, b, name):
    B, D = c_all.shape
    N = w3.shape[2]
    tn = _div(N, 1024, LANE)

    def body(c_ref, w_ref, b_ref, o_ref):
        cv = c_ref[...]
        o_ref[...] = _dot(cv * _sigmoid(cv), w_ref[...]) + b_ref[...]

    return pl.pallas_call(
        body, grid=(N // tn,),
        in_specs=[pl.BlockSpec((B, D), lambda j: (0, 0)), pl.BlockSpec((None, D, tn), lambda j: (layer, 0, j)), pl.BlockSpec((1, tn), lambda j: (0, j))],
        out_specs=pl.BlockSpec((B, tn), lambda j: (0, j)), out_shape=jax.ShapeDtypeStruct((B, N), F32),
        compiler_params=_cp(("parallel",)), name=name,
    )(c_all, w3, b)


def _mod_wgrad(c_t, dm, name):
    D, B = c_t.shape
    N = dm.shape[1]
    tn = _div(N, 1024, LANE)

    def body(c_ref, d_ref, o_ref):
        cv = c_ref[...]
        ca = cv * _sigmoid(cv)
        acc = ca[:, 0:1] * d_ref[pl.ds(0, 1), :]
        for b in range(1, B):
            acc = acc + ca[:, b:b + 1] * d_ref[pl.ds(b, 1), :]
        o_ref[...] = acc

    return pl.pallas_call(
        body, grid=(N // tn,),
        in_specs=[pl.BlockSpec((D, B), lambda j: (0, 0)), pl.BlockSpec((B, tn), lambda j: (0, j))],
        out_specs=pl.BlockSpec((D, tn), lambda j: (0, j)), out_shape=jax.ShapeDtypeStruct((D, N), F32),
        compiler_params=_cp(("parallel",)), name=name,
    )(c_t, dm)


def _rowsum(xs, name):
    n, N = xs.shape
    tn = _div(N, 8192, LANE)

    def body(x_ref, o_ref):
        acc = x_ref[pl.ds(0, 1), :]
        for r in range(1, n):
            acc = acc + x_ref[pl.ds(r, 1), :]
        o_ref[...] = acc

    return pl.pallas_call(
        body, grid=(N // tn,), in_specs=[pl.BlockSpec((n, tn), lambda j: (0, j))],
        out_specs=pl.BlockSpec((1, tn), lambda j: (0, j)), out_shape=jax.ShapeDtypeStruct((1, N), F32),
        compiler_params=_cp(("parallel",)), name=name,
    )(xs)


def _adamw(w, g, m, v, name, copy_grad=False):
    R, C = w.shape
    tr = _div(R, max(8, (1 << 18) // C), 8)

    def body(w_ref, g_ref, m_ref, v_ref, d_ref, nm_ref, nv_ref, *g_out):
        gv = g_ref[...]
        if copy_grad:
            g_out[0][...] = gv
        mn = ADAM_B1 * m_ref[...] + (1.0 - ADAM_B1) * gv
        vn = ADAM_B2 * v_ref[...] + (1.0 - ADAM_B2) * (gv * gv)
        m_hat = mn / (1.0 - ADAM_B1 ** ADAM_STEP)
        v_hat = vn / (1.0 - ADAM_B2 ** ADAM_STEP)
        d_ref[...] = -ADAM_LR * (m_hat / (jnp.sqrt(v_hat) + ADAM_EPS) + ADAM_WD * w_ref[...])
        nm_ref[...] = mn
        nv_ref[...] = vn

    blk = pl.BlockSpec((tr, C), lambda i: (i, 0))
    os_ = jax.ShapeDtypeStruct((R, C), F32)
    n_out = 4 if copy_grad else 3
    return pl.pallas_call(
        body, grid=(R // tr,), in_specs=[blk] * 4, out_specs=[blk] * n_out, out_shape=[os_] * n_out,
        compiler_params=_cp(("parallel",)), name=name,
    )(w, g, m, v)


def _place():
    return lax.axis_index("x"), lax.axis_index("y"), lax.axis_index("c")


def _other_chips(x, y):
    return [(1 - x, y), (x, 1 - y), (1 - x, 1 - y)]


def _half(c, rows):
    return pl.ds(c * (rows // 2), rows // 2)


_ANY = pl.BlockSpec(memory_space=pl.ANY)


def _ag_small(v, name):
    r, n = v.shape

    def body(v_ref, o_ref, send_sems, recv_sems):
        x, y, c = _place()
        me = 4 * x + 2 * y + c
        o_ref[pl.ds(me, 1)] = v_ref[...][None]
        peers = [(x ^ (k >> 2), y ^ ((k >> 1) & 1), c ^ (k & 1)) for k in range(1, 8)]
        copies = []
        for k, peer in enumerate(peers):
            cp = pltpu.make_async_remote_copy(
                src_ref=v_ref, dst_ref=o_ref.at[me], send_sem=send_sems.at[k], recv_sem=recv_sems.at[k],
                device_id=peer, device_id_type=MESH)
            cp.start()
            copies.append(cp)
        for cp in copies:
            cp.wait()

    return pl.pallas_call(
        body, out_shape=jax.ShapeDtypeStruct((8, r, n), v.dtype),
        in_specs=[pl.BlockSpec(memory_space=pltpu.VMEM)], out_specs=pl.BlockSpec(memory_space=pltpu.VMEM),
        scratch_shapes=[pltpu.SemaphoreType.DMA((7,)), pltpu.SemaphoreType.DMA((7,))],
        compiler_params=pltpu.CompilerParams(vmem_limit_bytes=VMEM_LIMIT), name=name,
    )(v)


def _rs_sibling_copies(bufs, send_sems, recv_sems):
    n = len(bufs) // 2
    x, y, c = _place()
    return [pltpu.make_async_remote_copy(
        src_ref=bufs[i].at[:, _half(1 - c, bufs[i].shape[1])], dst_ref=bufs[n + i], send_sem=send_sems.at[i], recv_sem=recv_sems.at[i],
        device_id=(x, y, 1 - c), device_id_type=MESH) for i in range(n)]


def _rs_sibling(gs, name, after=None):
    n = len(gs)

    def body(*refs):
        copies = _rs_sibling_copies(refs[:n] + refs[n + 1:2 * n + 1], refs[2 * n + 1], refs[2 * n + 2])
        for cp in copies:
            cp.start()
        for cp in copies:
            cp.wait()

    return pl.pallas_call(
        body, out_shape=[jax.ShapeDtypeStruct((N_CHIPS, g.shape[1] // 2, g.shape[2]), g.dtype) for g in gs],
        in_specs=[_ANY] * (n + 1), out_specs=[_ANY] * n,
        scratch_shapes=[pltpu.SemaphoreType.DMA((n,)), pltpu.SemaphoreType.DMA((n,))],
        compiler_params=pltpu.CompilerParams(has_side_effects=True), name=name,
    )(*gs, gs[0] if after is None else after)


def _rs_presum(g, sib, c_arr, name):
    ns, R, C = g.shape
    rh = R // 2
    tr = _div(rh, max(16, (1 << 19) // C), 16)
    nrb = rh // tr

    def body(c_ref, g_ref, s_ref, o_ref):
        o_ref[...] = (g_ref[...] + s_ref[...]).astype(BF16)

    return pl.pallas_call(
        body, out_shape=jax.ShapeDtypeStruct((ns, rh, C), BF16),
        grid_spec=pltpu.PrefetchScalarGridSpec(
            num_scalar_prefetch=1, grid=(ns, nrb),
            in_specs=[pl.BlockSpec((None, tr, C), lambda s, r, c_ref: (s, c_ref[0] * nrb + r, 0)),
                      pl.BlockSpec((None, tr, C), lambda s, r, c_ref: (s, r, 0))],
            out_specs=pl.BlockSpec((None, tr, C), lambda s, r, c_ref: (s, r, 0))),
        compiler_params=_cp(("parallel", "parallel")), name=name,
    )(c_arr, g, sib)


def _rs_sum(g, sib, recv, full, layer, sc_arr, name):
    ns, R, C = g.shape
    rh = R // 2
    tr = _div(rh, max(16, (1 << 18) // C), 16)
    nrb = rh // tr

    def body(sc_ref, g_ref, s_ref, r_ref, f_ref, o_ref):
        acc = g_ref[...] + s_ref[...]
        for j in range(3):
            acc = acc + r_ref[j].astype(F32)
        o_ref[...] = acc

    return pl.pallas_call(
        body, out_shape=jax.ShapeDtypeStruct(full.shape, F32),
        grid_spec=pltpu.PrefetchScalarGridSpec(
            num_scalar_prefetch=1, grid=(nrb,),
            in_specs=[pl.BlockSpec((None, tr, C), lambda r, sc: (sc[0], sc[1] * nrb + r, 0)),
                      pl.BlockSpec((None, tr, C), lambda r, sc: (sc[0], r, 0)),
                      pl.BlockSpec((3, tr, C), lambda r, sc: (0, r, 0)),
                      _ANY],
            out_specs=pl.BlockSpec((None, tr, C), lambda r, sc: (layer, sc[1] * nrb + r, 0))),
        input_output_aliases={4: 0},
        compiler_params=_cp(("parallel",)), name=name,
    )(sc_arr, g, sib, recv, full)


def _rs_share(fulls, layer, name):
    n = len(fulls)

    def body(*refs):
        src, out = refs[:n], refs[n:2 * n]
        send_sems, recv_sems = refs[2 * n:]
        x, y, c = _place()
        copies = []
        for i in range(n):
            rows = out[i].shape[1]
            cp = pltpu.make_async_remote_copy(
                src_ref=out[i].at[layer, _half(c, rows)], dst_ref=out[i].at[layer, _half(c, rows)],
                send_sem=send_sems.at[i], recv_sem=recv_sems.at[i], device_id=(x, y, 1 - c), device_id_type=MESH)
            cp.start()
            copies.append(cp)
        for cp in copies:
            cp.wait()

    return pl.pallas_call(
        body, out_shape=[jax.ShapeDtypeStruct(f.shape, f.dtype) for f in fulls],
        in_specs=[_ANY] * n, out_specs=[_ANY] * n, input_output_aliases={i: i for i in range(n)},
        scratch_shapes=[pltpu.SemaphoreType.DMA((n,)), pltpu.SemaphoreType.DMA((n,))],
        compiler_params=pltpu.CompilerParams(has_side_effects=True), name=name,
    )(*fulls)


_HBM = pl.BlockSpec(memory_space=pltpu.HBM)
_SEM = pl.BlockSpec(memory_space=pltpu.SEMAPHORE)
_EFFECT = pltpu.SideEffectType.DATAFLOW_SIDE_EFFECTING


def _in_hbm(a):
    return pltpu.with_memory_space_constraint(a, pltpu.HBM)


def _split_start(bufs, n_sem, copies_of, name):
    nb = len(bufs)

    def body(*refs):
        for cp in copies_of(refs[:nb], refs[nb], refs[nb + 1]):
            cp.start()
        refs[-1][...] = jnp.zeros_like(refs[-1])

    out = pl.pallas_call(
        body, name=name,
        out_shape=(pltpu.SemaphoreType.DMA((n_sem,)), pltpu.SemaphoreType.DMA((n_sem,)), *[pltpu.HBM(a.shape, a.dtype) for a in bufs],
                   jax.ShapeDtypeStruct((SUB, LANE), F32)),
        in_specs=[_HBM] * nb, out_specs=(_SEM, _SEM, *([_HBM] * nb), pl.BlockSpec(memory_space=pltpu.VMEM)),
        input_output_aliases={i: 2 + i for i in range(nb)},
        compiler_params=pltpu.CompilerParams(has_side_effects=_EFFECT),
    )(*[_in_hbm(a) for a in bufs])
    return out[0], out[1], list(out[2:2 + nb]), out[-1]


def _split_wait(send_sems, recv_sems, bufs, after, copies_of, name):
    nb = len(bufs)

    def body(*refs):
        for cp in copies_of(refs[:nb], refs[nb], refs[nb + 1]):
            cp.wait_send()
            cp.wait_recv()

    return list(pl.pallas_call(
        body, name=name, out_shape=[pltpu.HBM(a.shape, a.dtype) for a in bufs],
        in_specs=[_HBM] * nb + [_SEM, _SEM, _ANY], out_specs=[_HBM] * nb,
        input_output_aliases={i: i for i in range(nb)},
        compiler_params=pltpu.CompilerParams(has_side_effects=_EFFECT),
    )(*bufs, send_sems, recv_sems, after))


def _ag_half_copies(land, send_sems, recv_sems, landing_of_mine):
    x, y, c = _place()
    cps = []
    for j, (cx, cy) in enumerate(_other_chips(x, y)):
        for i in range(len(land)):
            rows = _half(c, land[i].shape[1])
            s = 2 * x + y if landing_of_mine else 2 * cx + cy
            cps.append(pltpu.make_async_remote_copy(
                src_ref=land[i].at[2 * x + y, rows], dst_ref=land[i].at[s, rows], send_sem=send_sems.at[3 * i + j], recv_sem=recv_sems.at[3 * i + j],
                device_id=(cx, cy, c), device_id_type=MESH))
    return cps


def _ag_starts(land, send_sems, recv_sems):
    return _ag_half_copies(land, send_sems, recv_sems, True)


def _ag_waits(land, send_sems, recv_sems):
    return _ag_half_copies(land, send_sems, recv_sems, False)


def _ag_finish(lands, name):
    n = len(lands)

    def body(*refs):
        land = refs[n:2 * n]
        send_sems, recv_sems = refs[2 * n:]
        x, y, c = _place()
        sibling = (x, y, 1 - c)

        def copy(k, i, s, h):
            blk = land[i].at[s, _half(h, land[i].shape[1])]
            return pltpu.make_async_remote_copy(
                src_ref=blk, dst_ref=blk, send_sem=send_sems.at[k], recv_sem=recv_sems.at[k], device_id=sibling, device_id_type=MESH)

        chips = _other_chips(x, y)
        passed = [copy(3 * i + j, i, 2 * cx + cy, c) for j, (cx, cy) in enumerate(chips) for i in range(n)]
        for cp in passed:
            cp.start()
        for j, (cx, cy) in enumerate(chips):
            for i in range(n):
                copy(3 * i + j, i, 2 * cx + cy, 1 - c).wait_recv()
        for cp in passed:
            cp.wait_send()

    return pl.pallas_call(
        body, out_shape=[jax.ShapeDtypeStruct(a.shape, a.dtype) for a in lands],
        in_specs=[_ANY] * n, out_specs=[_ANY] * n, input_output_aliases={i: i for i in range(n)},
        scratch_shapes=[pltpu.SemaphoreType.DMA((3 * n,)), pltpu.SemaphoreType.DMA((3 * n,))],
        compiler_params=pltpu.CompilerParams(has_side_effects=True), name=name,
    )(*lands)


def _rs_chip_copies(bufs, send_sems, recv_sems):
    n = len(bufs) // 2
    x, y, c = _place()
    return [pltpu.make_async_remote_copy(
        src_ref=bufs[i].at[2 * cx + cy], dst_ref=bufs[n + i].at[j], send_sem=send_sems.at[3 * i + j], recv_sem=recv_sems.at[3 * i + j],
        device_id=(cx, cy, c), device_id_type=MESH) for j, (cx, cy) in enumerate(_other_chips(x, y)) for i in range(n)]


def _pad_heads(w):
    lead = w.shape[:-1]
    w4 = w.reshape(*lead, HEADS, DK)
    w4 = jnp.pad(w4, [(0, 0)] * len(lead) + [(0, 0), (0, DKP - DK)])
    return w4.reshape(*lead, HEADS * DKP)


def _unpad_heads(w):
    lead = w.shape[:-1]
    return w.reshape(*lead, HEADS, DKP)[..., :DK].reshape(*lead, HEADS * DK)


def _mix_column_runs():
    o = 2 * D_CONV
    runs = [(0, D_CONV, ZC_A), (D_CONV, D_CONV, ZC_B)]
    runs += [(o + h * DK, DK, ZC_Q + h * DKP) for h in range(HEADS)]
    runs += [(o + HEADS * DK + h * DK, DK, ZC_K + h * DKP) for h in range(HEADS)]
    o += 2 * HEADS * DK
    return runs + [(o, D_GLA, ZC_V), (o + D_GLA, D_GLA, ZC_R), (o + 2 * D_GLA, GATE_RANK, ZC_G)]


def _mix_weight(win4, n_cols):
    D = win4.shape[1]
    pieces, z = [], 0
    for c0, width, z0 in sorted(_mix_column_runs(), key=lambda r: r[2]):
        if z0 > z:
            pieces.append(jnp.zeros((D, z0 - z), win4.dtype))
        c = c0
        while c < c0 + width:
            s, lo = divmod(c, n_cols)
            hi = min(n_cols, lo + c0 + width - c)
            pieces.append(win4[s, :, lo:hi])
            c += hi - lo
        z = z0 + width
    pieces.append(jnp.zeros((D, Z_COLS - z), win4.dtype))
    return jnp.concatenate(pieces, axis=1)


def _mix_weight_grad(dgla, dab, dglr, n_cols, n_pad):
    D = dab.shape[0]

    def zcols(z0, z1):
        for arr, base in ((dgla, 0), (dab, ZC_A), (dglr, ZC_G)):
            if base <= z0 and z1 <= base + arr.shape[1]:
                return arr[:, z0 - base:z1 - base]

    shards = []
    for s in range(N_CHIPS):
        pieces = []
        for c0, width, z0 in _mix_column_runs():
            lo, hi = max(c0, s * n_cols), min(c0 + width, (s + 1) * n_cols)
            if lo < hi:
                pieces.append(zcols(z0 + lo - c0, z0 + hi - c0))
        pieces.append(jnp.zeros((D, n_pad - n_cols), dab.dtype))
        shards.append(jnp.concatenate(pieces, axis=1))
    return jnp.stack(shards)


_ARG_NAMES = ['x', 'c', 'w_ada', 'b_ada', 'g_norm_ffn1', 'w_ffn1_in', 'w_ffn1_out', 'g_norm_mix', 'w_in', 'w_dw', 'b_dw', 'g_conv_ln', 'b_conv_ln', 'w_gate_up', 'b_gate', 'g_gla_norm', 'w_out', 'g_norm_ffn2', 'w_ffn2_in', 'w_ffn2_out', 'g_norm_final', 'w_ada_final', 'b_ada_final']
_WEIGHTS = _ARG_NAMES[2:]
_BIG = ('w_ffn1_in', 'w_ffn1_out', 'w_in', 'w_out', 'w_ffn2_in', 'w_ffn2_out')
_SMALL = ('g_norm_ffn1', 'g_norm_mix', 'w_dw', 'b_dw', 'g_conv_ln', 'b_conv_ln', 'w_gate_up', 'b_gate', 'g_gla_norm', 'g_norm_ffn2', 'g_norm_final')


def _ffn_fwd(x, h, gv, w4, wo, nxt, tag):
    if isinstance(h, tuple):
        z, act, h = _ffn_up(x, w4, f"ffn_up_{tag}", norm=h)
    else:
        z, act = _ffn_up(h, w4, f"ffn_up_{tag}")
    y, xn, *hn = _ffn_down(act, wo, x, gv, nxt, f"ffn_down_{tag}")
    return xn, (hn[0] if hn else None), y, (x, h, z, act)


def _ffn_bwd(dxn, dy, saved, g, scale, prev, w4, wo, tag):
    x, h, z, act = saved
    ns = w4.shape[0]
    dwo = _mm_tn(act, dy, f"dw_out_{tag}")
    dz, dx, dsh, dsc, dg, *pv = _ffn_bwd_core(dy, wo, z, w4, x, dxn, g, scale, prev, f"ffn_bwd_{tag}")
    dwi = _dw_ffn_in(h, dz, ns, f"dw_in_{tag}")
    return dx, pv, dict(dshift=dsh, dscale=dsc, dg=dg, dw_in=dwi, dw_out=dwo.reshape(N_CHIPS, -1, dwo.shape[1]))


def _mix_fwd(x, h, gv, wmix, w_dw, b_dw, g_ln, b_ln, wgp, bgp, gn, wout, nxt, tag):
    z = _mm([(h, wmix, 0)], F32, f"mix_in_{tag}")
    u, yc, yconv = _conv_fwd(z, w_dw, b_dw, g_ln, b_ln, f"conv_fwd_{tag}")
    la = _loggate(z, wgp, bgp, f"loggate_{tag}")
    o_raw, sprev, ygla = _gla_fwd(z, la, gn, f"gla_fwd_{tag}")
    y, xn, *hn = _mix_out(yconv, ygla, wout, x, gv, nxt, f"mix_out_{tag}")
    return xn, (hn[0] if hn else None), y, (x, h, z, u, yc, la, o_raw, sprev, yconv, ygla)


def _mix_bwd(dxn, dy, saved, g, scale, prev, wmix, w_dw, g_ln, b_ln, wgp, bgp, gn, wout, n_cols, n_pad, tag):
    x, h, z, u, yc, la, o_raw, sprev, yconv, ygla = saved
    dycat = _mm([(dy, wout, 0)], F32, f"mix_dycat_{tag}", nt=True)
    dwout = _mm_tn_two(yconv, ygla, dy, f"dw_mixout_{tag}")
    dab, dwdw, dbdw, dgln, dbln = _conv_bwd(dycat, z, u, yc, w_dw, g_ln, b_ln, f"conv_bwd_{tag}")
    dgla, dla, dgn = _gla_bwd(dycat, z, la, o_raw, sprev, gn, f"gla_bwd_{tag}")
    dglr, dwgp, dbgp = _loggate_bwd(dla, z, wgp, wgp.T, bgp, f"loggate_bwd_{tag}")
    dx, dsh, dsc, dg, *pv = _dh_normmod_bwd(
        [(dgla, wmix, 0, 0, Z_GLA), (dab, wmix, 0, ZC_A // (2 * D_CONV), 2 * D_CONV), (dglr, wmix, 0, ZC_G // LANE, LANE)],
        x, dxn, g, scale, prev, f"mix_dh_{tag}")
    dwin = _mix_weight_grad(_mm_tn(h, dgla, f"dw_mixin_gla_{tag}"), _mm_tn(h, dab, f"dw_mixin_conv_{tag}"), _mm_tn(h, dglr, f"dw_mixin_gate_{tag}"),
                            n_cols, n_pad)
    grads = dict(dshift=dsh, dscale=dsc, dg=dg, dw_in=dwin, dw_out=dwout.reshape(N_CHIPS, -1, dwout.shape[1]), dw_dw=dwdw, db_dw=dbdw,
                 dg_ln=dgln, db_ln=dbln, dw_gate=_unpad_heads(dwgp[:GATE_RANK]), db_gate=_unpad_heads(dbgp)[0], dgn=dgn)
    return dx, pv, grads


def kernel(x, c, w_ada, b_ada, g_norm_ffn1, w_ffn1_in, w_ffn1_out, g_norm_mix, w_in, w_dw, b_dw, g_conv_ln, b_conv_ln, w_gate_up, b_gate, g_gla_norm, w_out, g_norm_ffn2, w_ffn2_in, w_ffn2_out, g_norm_final, w_ada_final, b_ada_final, loss_target, m_w_ada, m_b_ada, m_g_norm_ffn1, m_w_ffn1_in, m_w_ffn1_out, m_g_norm_mix, m_w_in, m_w_dw, m_b_dw, m_g_conv_ln, m_b_conv_ln, m_w_gate_up, m_b_gate, m_g_gla_norm, m_w_out, m_g_norm_ffn2, m_w_ffn2_in, m_w_ffn2_out, m_g_norm_final, m_w_ada_final, m_b_ada_final, v_w_ada, v_b_ada, v_g_norm_ffn1, v_w_ffn1_in, v_w_ffn1_out, v_g_norm_mix, v_w_in, v_w_dw, v_b_dw, v_g_conv_ln, v_b_conv_ln, v_w_gate_up, v_b_gate, v_g_gla_norm, v_w_out, v_g_norm_ffn2, v_w_ffn2_in, v_w_ffn2_out, v_g_norm_final, v_w_ada_final, v_b_ada_final):
    given = dict(locals())
    W = {n: given[n] for n in _WEIGHTS}
    M1 = {n: given["m_" + n] for n in _WEIGHTS}
    M2 = {n: given["v_" + n] for n in _WEIGHTS}
    xs = x[0]
    tgt = loss_target[0]
    S, D = xs.shape
    L = w_ada.shape[0]
    xi, yi, ci = _place()
    s_me = 2 * xi + yi
    b_me = 4 * xi + 2 * yi + ci
    nsh = w_ada.shape[2]
    nfin = w_ada_final.shape[1]
    n_cols = w_in.shape[2]
    n_pad = -(-n_cols // LANE) * LANE

    def lands_of(l):
        shards = [W[n][l].astype(BF16) for n in _BIG]
        shards[2] = jnp.pad(shards[2], ((0, 0), (0, n_pad - n_cols)))
        return [lax.dynamic_update_index_in_dim(lax.empty((N_CHIPS,) + s.shape, BF16), s, s_me, 0) for s in shards]

    lands = {l: lands_of(l) for l in range(L)}
    ag_groups = [dict(l=0, items=[0, 1], need=0), dict(l=0, items=[2, 3, 4, 5], need=1)]
    ag_groups += [dict(l=l, items=list(range(len(_BIG))), need=3 * l) for l in range(1, L)]

    def ag_start(grp):
        bufs = [lands[grp["l"]][i] for i in grp["items"]]
        return _split_start(bufs, 3 * len(bufs), _ag_starts, f"ag_start_l{grp['l']}_{grp['items'][0]}")

    pend = ag_start(ag_groups[0])
    tok = pend[3][0, 0]

    c_all = _ag_small(c.reshape(8, D // 8) + tok, "ag_c").reshape(8, D)
    tok = None
    parts = [_mod_proj(c_all, w_ada, l, lax.dynamic_slice(b_ada, (l, s_me * nsh), (1, nsh)), f"mod_proj_{l}") for l in range(L)]
    parts.append(_mod_proj(c_all, w_ada_final[None], 0, lax.dynamic_slice(b_ada_final, (s_me * nfin,), (nfin,))[None], "mod_proj_final"))
    mod_all = _ag_small(jnp.concatenate(parts, axis=1), "ag_mod")
    mine = [lax.dynamic_index_in_dim(lax.dynamic_index_in_dim(mod_all, 2 * s + ci, 0, False), b_me, 0, False) for s in range(N_CHIPS)]
    mods = [jnp.concatenate([mine[s][l * nsh:(l + 1) * nsh] for s in range(N_CHIPS)]).reshape(N_MOD, 1, D) for l in range(L)]
    fmod = jnp.concatenate([mine[s][L * nsh:] for s in range(N_CHIPS)]).reshape(2, 1, D)

    tiny = jnp.concatenate([w_dw.reshape(-1), w_gate_up.reshape(-1)])
    tiny_all = _ag_small(jnp.pad(tiny, (0, (-tiny.shape[0]) % (8 * LANE))).reshape(8, -1), "ag_tiny").reshape(8, -1)
    n_dw = w_dw.size
    dw_parts = [lax.dynamic_index_in_dim(tiny_all, 2 * s + ci, 0, False) for s in range(N_CHIPS)]
    w_dw_full = jnp.concatenate([p[:n_dw].reshape(w_dw.shape) for p in dw_parts], axis=2)
    w_gu_full = jnp.concatenate([p[n_dw:n_dw + w_gate_up.size].reshape(w_gate_up.shape) for p in dw_parts], axis=2)

    def layer_weights(l, lands):
        wi1, wo1, win4, wout4, wi2, wo2 = lands
        return dict(
            wi1=wi1, wo1=wo1.reshape(-1, D), wi2=wi2, wo2=wo2.reshape(-1, D), wout=wout4.reshape(-1, D), wmix=_mix_weight(win4, n_cols),
            wgp=jnp.pad(_pad_heads(w_gu_full[l]), ((0, LANE - GATE_RANK), (0, 0))).astype(BF16), bgp=_pad_heads(b_gate[l])[None])

    gnorm = (g_norm_ffn1, g_norm_mix, g_norm_ffn2)
    subs = [dict(l=l, j=j, tag=f"{('ffn1', 'mix', 'ffn2')[j]}_l{l}", g=gnorm[j][l][None], shift=mods[l][3 * j], scale=mods[l][3 * j + 1],
                 gv=mods[l][3 * j + 2] * (1.0 if j == 1 else 0.5)) for l in range(L) for j in range(3)]
    gi = 0
    xcur = xs
    h = None
    for k, sb in enumerate(subs):
        l, j = sb["l"], sb["j"]
        if pend is not None and ag_groups[gi]["need"] == k:
            grp = ag_groups[gi]
            nm = f"l{grp['l']}_{grp['items'][0]}"
            after = xcur if k > 0 else sb["shift"]
            done = _ag_finish(_split_wait(pend[0], pend[1], pend[2], after, _ag_waits, f"ag_wait_{nm}"), f"ag_finish_{nm}")
            for i, a in zip(grp["items"], done):
                lands[grp["l"]][i] = a
            gi += 1
            pend = ag_start(ag_groups[gi]) if gi < len(ag_groups) else None
            tok = pend[3][0, 0] if pend is not None else None
        if h is None:
            h = (sb["g"] if tok is None else sb["g"] + tok, sb["shift"], sb["scale"])
            tok = None
        d = layer_weights(l, lands[l])
        nxt = (subs[k + 1]["g"], subs[k + 1]["shift"], subs[k + 1]["scale"]) if k + 1 < len(subs) else None
        gv = sb["gv"] if tok is None else sb["gv"] + tok
        tok = None
        if j == 1:
            xcur, h, sb["y"], sb["saved"] = _mix_fwd(xcur, h, gv, d["wmix"], w_dw_full[l], b_dw[l][None], g_conv_ln[l][None],
                                                     b_conv_ln[l][None], d["wgp"], d["bgp"], g_gla_norm[l], d["wout"], nxt, sb["tag"])
        else:
            w4, wo = (d["wi1"], d["wo1"]) if j == 0 else (d["wi2"], d["wo2"])
            xcur, h, sb["y"], sb["saved"] = _ffn_fwd(xcur, h, gv, w4, wo, nxt, sb["tag"])
    lw = [layer_weights(l, lands[l]) for l in range(L)]

    c_arr = jnp.stack([ci]).astype(jnp.int32)
    sc_arr = jnp.stack([s_me, ci]).astype(jnp.int32)
    fulls = [lax.empty((L,) + ((W[n].shape[1], n_pad) if n == 'w_in' else W[n].shape[1:]), F32) for n in _BIG]

    def rs_begin(gs, items, l, after=None):
        nm = f"l{l}_{items[0]}"
        sibs = _rs_sibling(gs, f"rs_sibling_{nm}", after)
        return sibs, [_rs_presum(g, sb_, c_arr, f"rs_presum_{i}_l{l}") for i, g, sb_ in zip(items, gs, sibs)]

    def rs_end(gs, sibs, recvs, items, l):
        summed = [_rs_sum(g, sb_, rv, fulls[i], l, sc_arr, f"rs_sum_{i}_l{l}") for i, g, sb_, rv in zip(items, gs, sibs, recvs)]
        for i, f in zip(items, _rs_share(summed, l, f"rs_share_l{l}_{items[0]}")):
            fulls[i] = f

    def rs_start(gs, items, l, sibs=None, after=None):
        if sibs is None:
            sibs, ps = rs_begin(gs, items, l, after)
        else:
            ps = [_rs_presum(g, sb_, c_arr, f"rs_presum_{i}_l{l}") for i, g, sb_ in zip(items, gs, sibs)]
        pend = _split_start(ps + [lax.empty((3,) + p.shape[1:], BF16) for p in ps], 3 * len(ps), _rs_chip_copies, f"rs_start_l{l}_{items[0]}")
        return dict(gs=gs, sibs=sibs, pend=pend, items=items, l=l)

    def sib_start(gs, items, l):
        lands_ = [lax.empty((N_CHIPS, g.shape[1] // 2, g.shape[2]), F32) for g in gs]
        pend = _split_start(gs + lands_, len(gs), _rs_sibling_copies, f"rs_sibling_start_l{l}_{items[0]}")
        return dict(pend=pend, items=items, l=l, n=len(gs))

    def sib_wait(sp, after):
        bufs = _split_wait(sp["pend"][0], sp["pend"][1], sp["pend"][2], after, _rs_sibling_copies, f"rs_sibling_wait_l{sp['l']}_{sp['items'][0]}")
        return bufs[:sp["n"]], bufs[sp["n"]:]

    def sib_finish(sp, after):
        gs, sibs = sib_wait(sp, after)
        return rs_start(gs, sp["items"], sp["l"], sibs=sibs)

    def rs_finish(fl, after):
        pend, n = fl["pend"], len(fl["gs"])
        bufs = _split_wait(pend[0], pend[1], pend[2], after, _rs_chip_copies, f"rs_wait_l{fl['l']}_{fl['items'][0]}")
        rs_end(fl["gs"], fl["sibs"], bufs[n:], fl["items"], fl["l"])

    sq, dx, dfsh, dfsc, dgfin, dy, dgv = _loss_head(xcur, g_norm_final[None], fmod[0], fmod[1], tgt, (subs[-1]["y"], subs[-1]["gv"]))
    loss_part = 0.5 / D * jnp.sum(sq)
    G = {n: [None] * L for n in _SMALL}
    dmods = [None] * L
    in_flight = None
    sib_flight = None
    tok = None
    for l in reversed(range(L)):
        gr = [None] * 3
        for j in reversed(range(3)):
            k = 3 * l + j
            sb, d = subs[k], lw[l]
            prev = (subs[k - 1]["y"], subs[k - 1]["gv"]) if k > 0 else None
            g_vec = sb["g"] if tok is None else sb["g"] + tok
            tok = None
            if j == 1:
                dx, pv, gr[j] = _mix_bwd(dx, dy, sb["saved"], g_vec, sb["scale"], prev, d["wmix"], w_dw_full[l], g_conv_ln[l][None], b_conv_ln[l][None],
                                         d["wgp"], d["bgp"], g_gla_norm[l], d["wout"], n_cols, n_pad, sb["tag"])
            else:
                w4, wo = (d["wi1"], d["wo1"]) if j == 0 else (d["wi2"], d["wo2"])
                dx, pv, gr[j] = _ffn_bwd(dx, dy, sb["saved"], g_vec, sb["scale"], prev, w4, wo, sb["tag"])
            gr[j]["dgv"] = dgv
            dy, dgv = pv if pv else (None, None)
            if j == 2 and sib_flight is not None:
                in_flight = sib_finish(sib_flight, dx)
                sib_flight = None
                tok = in_flight["pend"][3][0, 0]
            if j == 2 and l == 0:
                sib_flight = sib_start([gr[2]["dw_in"], gr[2]["dw_out"]], [4, 5], l)
                tok = sib_flight["pend"][3][0, 0] + (0.0 if tok is None else tok)
            if j == 1 and in_flight is not None:
                rs_finish(in_flight, dx)
                in_flight = None
            if j == 1 and l == 0:
                gs_ffn2, sibs_ffn2 = sib_wait(sib_flight, dx)
                sib_flight = None
                gs_mix = [gr[1]["dw_in"], gr[1]["dw_out"]]
                in_flight = rs_start(gs_mix + gs_ffn2, [2, 3, 4, 5], l, sibs=list(_rs_sibling(gs_mix, "rs_sibling_l0_2")) + list(sibs_ffn2))
                tok = in_flight["pend"][3][0, 0]
        g1, g2, g3 = gr
        if l > 0:
            sib_flight = sib_start([g1["dw_in"], g1["dw_out"], g2["dw_in"], g2["dw_out"], g3["dw_in"], g3["dw_out"]], list(range(len(_BIG))), l)
            tok = sib_flight["pend"][3][0, 0]
        else:
            rs_finish(in_flight, dx)
            in_flight = None
            last_grads = [g1["dw_in"], g1["dw_out"]]
        dmods[l] = jnp.concatenate([g1["dshift"], g1["dscale"], 0.5 * g1["dgv"], g2["dshift"], g2["dscale"], g2["dgv"],
                                    g3["dshift"], g3["dscale"], 0.5 * g3["dgv"]], axis=1)[0]
        G["g_norm_ffn1"][l], G["g_norm_ffn2"][l], G["g_norm_mix"][l] = g1["dg"][0], g3["dg"][0], g2["dg"][0]
        G["w_dw"][l], G["b_dw"][l], G["g_conv_ln"][l], G["b_conv_ln"][l] = g2["dw_dw"], g2["db_dw"][0], g2["dg_ln"][0], g2["db_ln"][0]
        G["w_gate_up"][l], G["b_gate"][l], G["g_gla_norm"][l] = g2["dw_gate"], g2["db_gate"], g2["dgn"]
    grad_x = dx[None]
    gsm = {}

    small = [jnp.stack(G[n]).reshape(-1) for n in _SMALL if n != 'g_norm_final'] + [dgfin[0]]
    dmod_vec = jnp.concatenate(dmods + [dfsh[0], dfsc[0]])
    n_mod_vec = dmod_vec.shape[0]
    vec = jnp.concatenate([dmod_vec] + small + [loss_part[None]])
    n_vec = vec.shape[0]
    vec = jnp.pad(vec, (0, (-n_vec) % (8 * LANE)))
    vec_all = _ag_small(vec.reshape(8, -1), "ag_small_grads").reshape(8, -1)
    last = rs_start(last_grads, [0, 1], 0, after=vec_all)
    vec_sum = _rowsum(vec_all + last["pend"][3][0, 0], "sum_small_grads")[0]
    loss = vec_sum[n_vec - 1]
    off = n_mod_vec
    for n in _SMALL:
        shp = {'w_dw': w_dw_full.shape, 'w_gate_up': w_gu_full.shape}.get(n, W[n].shape)
        cnt = 1
        for dd in shp:
            cnt *= dd
        gsm[n] = vec_sum[off:off + cnt].reshape(shp)
        off += cnt
    gsm['w_dw'] = lax.dynamic_slice_in_dim(gsm['w_dw'], s_me * w_dw.shape[2], w_dw.shape[2], 2)
    gsm['w_gate_up'] = lax.dynamic_slice_in_dim(gsm['w_gate_up'], s_me * w_gate_up.shape[2], w_gate_up.shape[2], 2)
    dmod_sum = vec_sum[:n_mod_vec]
    gsm['b_ada'] = dmod_sum[:L * N_MOD * D].reshape(L, N_MOD * D)
    gsm['b_ada_final'] = dmod_sum[L * N_MOD * D:]
    c_t = c_all.T
    dmod_rows = vec_all[:, :n_mod_vec]
    gsm['w_ada'] = jnp.stack([
        _mod_wgrad(c_t, lax.dynamic_slice_in_dim(dmod_rows, l * N_MOD * D + s_me * nsh, nsh, 1), f"dw_ada_{l}") for l in range(L)])
    gsm['w_ada_final'] = _mod_wgrad(c_t, lax.dynamic_slice_in_dim(dmod_rows, L * N_MOD * D + s_me * nfin, nfin, 1), "dw_ada_final")
    gsm.update({n: (f[:, :, :n_cols] if n == 'w_in' else f) for n, f in zip(_BIG[2:], fulls[2:])})

    outs = {}
    small_names = [n for n in _WEIGHTS if W[n].size < 65536]
    for n in [m for m in _WEIGHTS if m not in _BIG[:2]] + list(_BIG[:2]):
        if n in small_names:
            continue
        if n == _BIG[0]:
            rs_finish(last, outs['w_ada_final'][0])
            gsm.update(dict(zip(_BIG[:2], fulls[:2])))
        shp = W[n].shape
        v2 = lambda a: a.reshape(-1, shp[-1])
        from_rs = n in _BIG and n != 'w_in'
        d_, m_, v_, *g_ = _adamw(v2(W[n]), v2(gsm[n]), v2(M1[n]), v2(M2[n]), f"adamw_{n}", copy_grad=from_rs)
        outs[n] = (d_.reshape(shp), m_.reshape(shp), v_.reshape(shp))
        if from_rs:
            gsm[n] = g_[0].reshape(shp)
    flat = lambda dct: jnp.concatenate([dct[n].reshape(-1) for n in small_names])
    n_small = sum(W[n].size for n in small_names)
    v2 = lambda a: jnp.pad(a, (0, (-n_small) % (8 * LANE))).reshape(-1, LANE)
    d_, m_, v_ = _adamw(v2(flat(W)), v2(flat(gsm)), v2(flat(M1)), v2(flat(M2)), "adamw_small")

    def unflat(a):
        res, o = {}, 0
        a = a.reshape(-1)
        for n in small_names:
            res[n] = a[o:o + W[n].size].reshape(W[n].shape)
            o += W[n].size
        return res

    for n, dd, mm, vv in zip(small_names, unflat(d_).values(), unflat(m_).values(), unflat(v_).values()):
        outs[n] = (dd, mm, vv)

    return (loss, grad_x, *[gsm[n] for n in _WEIGHTS], *[outs[n][0] for n in _WEIGHTS], *[outs[n][1] for n in _WEIGHTS], *[outs[n][2] for n in _WEIGHTS])
```

```python
import jax
import jax.numpy as jnp
from jax import lax
from jax.experimental import pallas as pl
from jax.experimental.pallas import tpu as pltpu

F32 = jnp.float32
BF16 = jnp.bfloat16

CHUNK = 64
HEADS = 4
DK = 64
DV = 128
DKP = 128
GATE_RANK = 16
GATE_TAU = 16.0
N_MOD = 9
EPS = 1e-6
ADAM_LR = 0.001
ADAM_B1 = 0.9
ADAM_B2 = 0.999
ADAM_EPS = 1e-08
ADAM_WD = 0.01
ADAM_STEP = 10

LANE = 128
HALO = 32
VMEM_LIMIT = 52 * 1024 * 1024
MESH = pl.DeviceIdType.MESH
N_CHIPS = 4

D_CONV = 512
D_GLA = HEADS * DV
ZC_Q = 0
ZC_K = ZC_Q + HEADS * DKP
ZC_V = ZC_K + HEADS * DKP
ZC_R = ZC_V + D_GLA
ZC_A = ZC_R + D_GLA
ZC_B = ZC_A + D_CONV
ZC_G = ZC_B + D_CONV
Z_COLS = ZC_G + LANE
Z_GLA = ZC_A


def _div(n, target, mult):
    best = None
    d = mult
    while d <= min(n, target):
        if n % d == 0:
            best = d
        d += mult
    return n if best is None else best


def _cp(sem=None, **kw):
    return pltpu.CompilerParams(dimension_semantics=sem, vmem_limit_bytes=VMEM_LIMIT, **kw)


def _resident(shape, index_map):
    return pl.BlockSpec(shape, index_map, pipeline_mode=pl.Buffered(1))


def _sigmoid(x):
    return 0.5 * jnp.tanh(0.5 * x) + 0.5


def _dot(a, b):
    return jnp.dot(a.astype(BF16), b.astype(BF16), preferred_element_type=F32)


def _dot_nt(a, b):
    return lax.dot_general(a.astype(BF16), b.astype(BF16), (((1,), (1,)), ((), ())), preferred_element_type=F32)


def _dot_tn(a, b):
    return lax.dot_general(a.astype(BF16), b.astype(BF16), (((0,), (0,)), ((), ())), preferred_element_type=F32)


def _dot_exact(a, b):
    return jnp.dot(a, b, preferred_element_type=F32, precision=lax.Precision.HIGHEST)


def _norm_rows(xv, g, shift, scale):
    r = lax.rsqrt(jnp.mean(xv * xv, axis=-1, keepdims=True) + EPS)
    return (xv * r) * g * (1.0 + scale) + shift


def _loss_head(x, g, shift, scale, tgt, prev):
    S, D = x.shape
    tm = _div(S, 512, 8)

    def body(x_ref, g_ref, sh_ref, sc_ref, t_ref, y_ref, gvp_ref, sq_ref, dx_ref, dsh_ref, dsc_ref, dg_ref, dy_ref, dgv_ref):
        sums = (sq_ref, dsh_ref, dsc_ref, dg_ref, dgv_ref)

        @pl.when(pl.program_id(0) == 0)
        def _():
            for o in sums:
                o[...] = jnp.zeros_like(o)

        xv = x_ref[...]
        e = _norm_rows(xv, g_ref[...], sh_ref[...], sc_ref[...]) - t_ref[...]
        dx, dsh, dsc, dg, dy, dgv = _normmod_bwd_rows(xv, e * (1.0 / D), None, g_ref[...], sc_ref[...], (y_ref[...], gvp_ref[...]))
        dx_ref[...] = dx
        dy_ref[...] = dy
        for o, v in zip(sums, (jnp.sum(e * e, axis=0, keepdims=True), dsh, dsc, dg, dgv)):
            o[...] += v

    row = pl.BlockSpec((tm, D), lambda i: (i, 0))
    vec = pl.BlockSpec((1, D), lambda i: (0, 0))
    vs = jax.ShapeDtypeStruct((1, D), F32)
    return pl.pallas_call(
        body, grid=(S // tm,), in_specs=[row, vec, vec, vec, row, row, vec], out_specs=[vec, row, vec, vec, vec, row, vec],
        out_shape=[vs, jax.ShapeDtypeStruct((S, D), F32), vs, vs, vs, jax.ShapeDtypeStruct((S, D), BF16), vs],
        compiler_params=_cp(("arbitrary",)), name="loss_head",
    )(x, g, shift, scale, tgt, *prev)


def _normmod_bwd_rows(xv, dh, dres, gv, sc, prev):
    r = lax.rsqrt(jnp.mean(xv * xv, axis=-1, keepdims=True) + EPS)
    xh = xv * r
    dsh = jnp.sum(dh, axis=0, keepdims=True)
    dsc = jnp.sum(dh * (xh * gv), axis=0, keepdims=True)
    dn = dh * (1.0 + sc)
    dg = jnp.sum(dn * xh, axis=0, keepdims=True)
    dxh = dn * gv
    dx = r * (dxh - xh * jnp.mean(dxh * xh, axis=-1, keepdims=True))
    if dres is not None:
        dx = dx + dres
    if prev is None:
        return dx, dsh, dsc, dg
    y, gvp = prev
    return dx, dsh, dsc, dg, (gvp * dx).astype(BF16), jnp.sum(dx * y, axis=0, keepdims=True)


def _mm(pairs, out_dtype, name, nt=False):
    M = pairs[0][0].shape[0]
    N = pairs[0][1].shape[0] if nt else pairs[0][1].shape[1]
    ktot = sum(a.shape[1] for a, _, _ in pairs)
    tm = _div(M, 512 if ktot <= 4096 else 256, 8)
    n = len(pairs)

    def body(*refs):
        o_ref = refs[2 * n]
        dot = _dot_nt if nt else _dot
        acc = dot(refs[0][...], refs[1][...])
        for p in range(1, n):
            acc = acc + dot(refs[2 * p][...], refs[2 * p + 1][...])
        o_ref[...] = acc.astype(o_ref.dtype)

    ins, args = [], []
    for a, b, blk in pairs:
        k = a.shape[1]
        ins.append(pl.BlockSpec((tm, k), lambda i: (i, 0)))
        ins.append(_resident((N, k), lambda i, blk=blk: (0, blk)) if nt else _resident((k, N), lambda i: (0, 0)))
        args += [a, b]
    return pl.pallas_call(
        body, grid=(M // tm,), in_specs=ins, out_specs=pl.BlockSpec((tm, N), lambda i: (i, 0)),
        out_shape=jax.ShapeDtypeStruct((M, N), out_dtype), compiler_params=_cp(("parallel",)), name=name,
    )(*args)


TN_ROWS = 2048


def _mm_tn(a, g, name):
    S, Ka = a.shape
    N = g.shape[1]
    tk = _div(Ka, 1408, LANE)
    tn = _div(N, 1408, LANE)
    ts = _div(S, TN_ROWS, 8)

    def body(a_ref, g_ref, o_ref):
        @pl.when(pl.program_id(2) == 0)
        def _():
            o_ref[...] = jnp.zeros_like(o_ref)

        o_ref[...] += _dot_tn(a_ref[...], g_ref[...])

    return pl.pallas_call(
        body, grid=(Ka // tk, N // tn, S // ts),
        in_specs=[pl.BlockSpec((ts, tk), lambda i, j, s: (s, i)), pl.BlockSpec((ts, tn), lambda i, j, s: (s, j))],
        out_specs=pl.BlockSpec((tk, tn), lambda i, j, s: (i, j)),
        out_shape=jax.ShapeDtypeStruct((Ka, N), F32),
        compiler_params=_cp(("parallel", "parallel", "arbitrary")), name=name,
    )(a, g)


def _mm_tn_two(a0, a1, g, name):
    S, K = a0.shape
    N = g.shape[1]
    ts = _div(S, TN_ROWS, 8)

    def body(a0_ref, a1_ref, g_ref, o_ref):
        i = pl.program_id(0)

        @pl.when(pl.program_id(1) == 0)
        def _():
            o_ref[...] = jnp.zeros_like(o_ref)

        @pl.when(i == 0)
        def _():
            o_ref[...] += _dot_tn(a0_ref[...], g_ref[...])

        @pl.when(i == 1)
        def _():
            o_ref[...] += _dot_tn(a1_ref[...], g_ref[...])

    return pl.pallas_call(
        body, grid=(2, S // ts),
        in_specs=[pl.BlockSpec((ts, K), lambda i, s: (jnp.where(i == 0, s, 0), 0)),
                  pl.BlockSpec((ts, K), lambda i, s: (jnp.where(i == 1, s, 0), 0)),
                  pl.BlockSpec((ts, N), lambda i, s: (s, 0))],
        out_specs=pl.BlockSpec((K, N), lambda i, s: (i, 0)),
        out_shape=jax.ShapeDtypeStruct((2 * K, N), F32),
        compiler_params=_cp(("parallel", "arbitrary")), name=name,
    )(a0, a1, g)


def _swiglu(gt, up):
    return gt * _sigmoid(gt) * up


def _ffn_up(h, w4, name, norm=None):
    S, D = h.shape
    ns, _, C = w4.shape
    hs = ns // 2
    tm = _div(S, 256, 8)
    nn = 3 if norm else 0

    def body(h_ref, w_ref, *rest):
        z_ref, a_ref = rest[nn:nn + 2]
        if norm:
            hv = _norm_rows(h_ref[...], rest[0][...], rest[1][...], rest[2][...]).astype(BF16)
            rest[nn + 2][...] = hv
        else:
            hv = h_ref[...]
        for s in range(hs):
            gt = _dot(hv, w_ref[s])
            up = _dot(hv, w_ref[hs + s])
            sg = _sigmoid(gt)
            silu = gt * sg
            z_ref[:, s * C:(s + 1) * C] = (up * (sg * (1.0 + gt * (1.0 - sg)))).astype(BF16)
            z_ref[:, (hs + s) * C:(hs + s + 1) * C] = silu.astype(BF16)
            a_ref[:, s * C:(s + 1) * C] = (silu * up).astype(BF16)

    row = pl.BlockSpec((tm, D), lambda i: (i, 0))
    vec = pl.BlockSpec((1, D), lambda i: (0, 0))
    return pl.pallas_call(
        body, grid=(S // tm,), in_specs=[row, _resident((ns, D, C), lambda i: (0, 0, 0))] + [vec] * nn,
        out_specs=[pl.BlockSpec((tm, ns * C), lambda i: (i, 0)), pl.BlockSpec((tm, hs * C), lambda i: (i, 0))] + [row] * (nn // 3),
        out_shape=[jax.ShapeDtypeStruct((S, ns * C), BF16), jax.ShapeDtypeStruct((S, hs * C), BF16)] + [jax.ShapeDtypeStruct((S, D), BF16)] * (nn // 3),
        compiler_params=_cp(("parallel",)), name=name,
    )(h, w4, *(norm or ()))


def _resid_outputs(y, x_ref, gv_ref, nxt_refs, out_refs):
    out_refs[0][...] = y
    xn = x_ref[...] + gv_ref[...] * y
    out_refs[1][...] = xn
    if nxt_refs:
        out_refs[2][...] = _norm_rows(xn, nxt_refs[0][...], nxt_refs[1][...], nxt_refs[2][...]).astype(BF16)


def _ffn_down(act, wo, x, gv, nxt, name):
    S = act.shape[0]
    Fd, D = wo.shape
    tm = _div(S, 512, 8)
    nn = 3 if nxt else 0

    def body(a_ref, w_ref, x_ref, gv_ref, *rest):
        _resid_outputs(_dot(a_ref[...], w_ref[...]), x_ref, gv_ref, rest[:nn], rest[nn:])

    row = pl.BlockSpec((tm, D), lambda i: (i, 0))
    vec = pl.BlockSpec((1, D), lambda i: (0, 0))
    os_ = jax.ShapeDtypeStruct((S, D), F32)
    return pl.pallas_call(
        body, grid=(S // tm,),
        in_specs=[pl.BlockSpec((tm, Fd), lambda i: (i, 0)), _resident((Fd, D), lambda i: (0, 0)), row, vec] + [vec] * nn,
        out_specs=[row, row] + [row] * (nn // 3), out_shape=[os_, os_] + [jax.ShapeDtypeStruct((S, D), BF16)] * (nn // 3),
        compiler_params=_cp(("parallel",)), name=name,
    )(act, wo, x, gv, *(nxt or ()))


def _dw_ffn_in(h, dz, ns, name):
    S, D = h.shape
    C = dz.shape[1] // ns
    ts = _div(S, TN_ROWS, 8)

    def body(h_ref, g_ref, o_ref):
        @pl.when(pl.program_id(1) == 0)
        def _():
            o_ref[...] = jnp.zeros_like(o_ref)

        o_ref[...] += _dot_tn(h_ref[...], g_ref[...])

    return pl.pallas_call(
        body, grid=(ns, S // ts),
        in_specs=[pl.BlockSpec((ts, D), lambda j, s: (s, 0)), pl.BlockSpec((ts, C), lambda j, s: (s, j))],
        out_specs=pl.BlockSpec((None, D, C), lambda j, s: (j, 0, 0)), out_shape=jax.ShapeDtypeStruct((ns, D, C), F32),
        compiler_params=_cp(("parallel", "arbitrary")), name=name,
    )(h, dz)


def _dh_normmod_bwd(pairs, x, dres, g, scale, prev, name):
    S, D = x.shape
    tm = _div(S, 256, 8)
    n = len(pairs)
    with_prev = prev is not None

    def body(*refs):
        refs = list(refs)
        mm = refs[:2 * n]
        x_ref, dr_ref, g_ref, sc_ref = refs[2 * n:2 * n + 4]
        outs = refs[2 * n + 4 + 2 * with_prev:]

        @pl.when(pl.program_id(0) == 0)
        def _():
            for o in outs[1:4] + outs[5:]:
                o[...] = jnp.zeros_like(o)

        dh = _dot_nt(mm[0][...], mm[1][...])
        for p in range(1, n):
            dh = dh + _dot_nt(mm[2 * p][...], mm[2 * p + 1][...])
        pv = (refs[2 * n + 4][...], refs[2 * n + 5][...]) if with_prev else None
        res = _normmod_bwd_rows(x_ref[...], dh, dr_ref[...], g_ref[...], sc_ref[...], pv)
        outs[0][...] = res[0]
        for o, v in zip(outs[1:4], res[1:4]):
            o[...] += v
        if with_prev:
            outs[4][...] = res[4]
            outs[5][...] += res[5]

    row = pl.BlockSpec((tm, D), lambda i: (i, 0))
    vec = pl.BlockSpec((1, D), lambda i: (0, 0))
    ins, args = [], []
    for a, b, a_blk, b_blk, k in pairs:
        ins.append(pl.BlockSpec((tm, k), lambda i, a_blk=a_blk: (i, a_blk)))
        ins.append(_resident((None, D, k), lambda i, b_blk=b_blk: (b_blk, 0, 0)) if b.ndim == 3 else _resident((D, k), lambda i, b_blk=b_blk: (0, b_blk)))
        args += [a, b]
    ins += [row, row, vec, vec] + [row, vec] * with_prev
    args += [x, dres, g, scale] + (list(prev) if with_prev else [])
    vs = jax.ShapeDtypeStruct((1, D), F32)
    return pl.pallas_call(
        body, grid=(S // tm,), in_specs=ins, out_specs=[row, vec, vec, vec] + [row, vec] * with_prev,
        out_shape=[jax.ShapeDtypeStruct((S, D), F32), vs, vs, vs] + [jax.ShapeDtypeStruct((S, D), BF16), vs] * with_prev,
        compiler_params=_cp(("arbitrary",)), name=name,
    )(*args)


def _ffn_bwd_core(dy, wo, z, w4, x, dres, g, scale, prev, name):
    S, D = x.shape
    Fd = wo.shape[0]
    ns, _, C = w4.shape
    tm = _div(S, 256, 8)
    with_prev = prev is not None

    def body(dy_ref, wo_ref, q_ref, p_ref, w4_ref, x_ref, dr_ref, g_ref, sc_ref, *rest):
        outs = rest[2 * with_prev:]
        dz_ref, outs = outs[0], outs[1:]

        @pl.when(pl.program_id(0) == 0)
        def _():
            for o in outs[1:4] + outs[5:]:
                o[...] = jnp.zeros_like(o)

        da = _dot_nt(dy_ref[...], wo_ref[...])
        dz_ref[:, :Fd] = (da * q_ref[...].astype(F32)).astype(BF16)
        dz_ref[:, Fd:] = (da * p_ref[...].astype(F32)).astype(BF16)
        dh = _dot_nt(dz_ref[:, 0:C], w4_ref[0])
        for s in range(1, ns):
            dh = dh + _dot_nt(dz_ref[:, s * C:(s + 1) * C], w4_ref[s])
        pv = (rest[0][...], rest[1][...]) if with_prev else None
        res = _normmod_bwd_rows(x_ref[...], dh, dr_ref[...], g_ref[...], sc_ref[...], pv)
        outs[0][...] = res[0]
        for o, v in zip(outs[1:4], res[1:4]):
            o[...] += v
        if with_prev:
            outs[4][...] = res[4]
            outs[5][...] += res[5]

    row = pl.BlockSpec((tm, D), lambda i: (i, 0))
    vec = pl.BlockSpec((1, D), lambda i: (0, 0))
    wide = pl.BlockSpec((tm, 2 * Fd), lambda i: (i, 0))
    vs = jax.ShapeDtypeStruct((1, D), F32)
    return pl.pallas_call(
        body, grid=(S // tm,),
        in_specs=[row, _resident((Fd, D), lambda i: (0, 0)), pl.BlockSpec((tm, Fd), lambda i: (i, 0)), pl.BlockSpec((tm, Fd), lambda i: (i, 1)),
                  _resident((ns, D, C), lambda i: (0, 0, 0)), row, row, vec, vec] + [row, vec] * with_prev,
        out_specs=[wide, row, vec, vec, vec] + [row, vec] * with_prev,
        out_shape=[jax.ShapeDtypeStruct((S, 2 * Fd), BF16), jax.ShapeDtypeStruct((S, D), F32), vs, vs, vs] + [jax.ShapeDtypeStruct((S, D), BF16), vs] * with_prev,
        compiler_params=_cp(("arbitrary",)), name=name,
    )(dy, wo, z, z, w4, x, dres, g, scale, *(prev or ()))


def _mix_out(yconv, ygla, wout, x, gv, nxt, name):
    S, Kc = yconv.shape
    Kg = ygla.shape[1]
    D = wout.shape[1]
    tm = _div(S, 512, 8)
    nn = 3 if nxt else 0

    def body(a_ref, b_ref, w_ref, x_ref, gv_ref, *rest):
        y = _dot(a_ref[...], w_ref[0:Kc, :]) + _dot(b_ref[...], w_ref[Kc:Kc + Kg, :])
        _resid_outputs(y, x_ref, gv_ref, rest[:nn], rest[nn:])

    row = pl.BlockSpec((tm, D), lambda i: (i, 0))
    vec = pl.BlockSpec((1, D), lambda i: (0, 0))
    os_ = jax.ShapeDtypeStruct((S, D), F32)
    return pl.pallas_call(
        body, grid=(S // tm,),
        in_specs=[pl.BlockSpec((tm, Kc), lambda i: (i, 0)), pl.BlockSpec((tm, Kg), lambda i: (i, 0)), _resident((Kc + Kg, D), lambda i: (0, 0)), row,
                  vec] + [vec] * nn,
        out_specs=[row, row] + [row] * (nn // 3), out_shape=[os_, os_] + [jax.ShapeDtypeStruct((S, D), BF16)] * (nn // 3),
        compiler_params=_cp(("parallel",)), name=name,
    )(yconv, ygla, wout, x, gv, *(nxt or ()))


def _ln_parts(yc, g, b):
    mu = jnp.mean(yc, axis=-1, keepdims=True)
    xc = yc - mu
    rs = lax.rsqrt(jnp.mean(xc * xc, axis=-1, keepdims=True) + EPS)
    xh = xc * rs
    return xh, rs, xh * g + b


SUB = 8
CONV_ROWS = 32


def _shifted_copies(ext8, rows):
    for b in range(1, SUB):
        ext8[b, pl.ds(0, rows - SUB), :] = ext8[0, pl.ds(b, rows - SUB), :]


def _tap(o):
    return o % SUB, o - o % SUB


def _conv_fwd(z, w_dw, b_dw, g_ln, b_ln, name):
    S = z.shape[0]
    W, C = w_dw.shape
    ts = _div(S, 512, HALO)
    hb = ts // HALO
    off = HALO - (W - 1)
    ca, cb = ZC_A // C, ZC_B // C
    rb = 2 * CONV_ROWS

    def body(a_ref, b_ref, pa_ref, pb_ref, w_ref, bd_ref, g_ref, bl_ref, u_ref, yc_ref, o_ref, ext8):
        keep = (pl.program_id(0) > 0).astype(F32)
        u = a_ref[...] * _sigmoid(b_ref[...])
        ext8[0, pl.ds(0, HALO), :] = pa_ref[...] * _sigmoid(pb_ref[...]) * keep
        ext8[0, pl.ds(HALO, ts), :] = u
        u_ref[...] = u
        _shifted_copies(ext8, ts + HALO)

        for lg in range(C // LANE):
            lanes = pl.ds(lg * LANE, LANE)
            taps = [jnp.broadcast_to(w_ref[pl.ds(j, 1), lanes], (SUB, LANE)) for j in range(W)]
            bias = jnp.broadcast_to(bd_ref[:, lanes], (SUB, LANE))

            def sub(i, carry, lanes=lanes, taps=taps, bias=bias):
                r0 = pl.multiple_of(i * rb, rb)
                accs = [bias] * (rb // SUB)
                for j in range(W):
                    b, a = _tap(off + j)
                    for r in range(rb // SUB):
                        accs[r] = accs[r] + taps[j] * ext8[b, pl.ds(r0 + a + r * SUB, SUB), lanes]
                for r in range(rb // SUB):
                    yc_ref[pl.ds(r0 + r * SUB, SUB), lanes] = accs[r]
                return carry

            lax.fori_loop(0, ts // rb, sub, 0)
        _, _, ln = _ln_parts(yc_ref[...], g_ref[...], bl_ref[...])
        o_ref[...] = (ln * _sigmoid(ln)).astype(BF16)

    cur = lambda col: pl.BlockSpec((ts, C), lambda i: (i, col))
    prev = lambda col: pl.BlockSpec((HALO, C), lambda i: (jnp.maximum(i * hb - 1, 0), col))
    vec = pl.BlockSpec((1, C), lambda i: (0, 0))
    row = pl.BlockSpec((ts, C), lambda i: (i, 0))
    fs = jax.ShapeDtypeStruct((S, C), F32)
    return pl.pallas_call(
        body, grid=(S // ts,),
        in_specs=[cur(ca), cur(cb), prev(ca), prev(cb), pl.BlockSpec((W, C), lambda i: (0, 0)), vec, vec, vec],
        out_specs=[row, row, row], out_shape=[fs, fs, jax.ShapeDtypeStruct((S, C), BF16)],
        scratch_shapes=[pltpu.VMEM((SUB, ts + HALO, C), F32)],
        compiler_params=_cp(("parallel",)), name=name,
    )(z, z, z, z, w_dw, b_dw, g_ln, b_ln)


def _conv_bwd(dycat, z, u, yc, w_dw, g_ln, b_ln, name):
    S = z.shape[0]
    W, C = w_dw.shape
    ts = _div(S, 512, HALO)
    hb = ts // HALO
    nblk = S // ts
    off = HALO - (W - 1)
    ca, cb = ZC_A // C, ZC_B // C
    rb = CONV_ROWS

    def ln_silu_bwd(dy, ycv, g, b):
        xh, rs, ln = _ln_parts(ycv, g, b)
        sl = _sigmoid(ln)
        dln = dy * (sl * (1.0 + ln * (1.0 - sl)))
        dxh = dln * g
        dyc = rs * (dxh - jnp.mean(dxh, axis=-1, keepdims=True) - xh * jnp.mean(dxh * xh, axis=-1, keepdims=True))
        return dyc, dln, xh

    def body(dy_ref, ndy_ref, yc_ref, nyc_ref, u_ref, pu_ref, a_ref, b_ref, w_ref, g_ref, bl_ref,
             dab_ref, dw_ref, dbd_ref, dg_ref, dbl_ref, uext8, dext8, dwacc):
        i = pl.program_id(0)

        @pl.when(i == 0)
        def _():
            dwacc[...] = jnp.zeros_like(dwacc)
            dbd_ref[...] = jnp.zeros_like(dbd_ref)
            dg_ref[...] = jnp.zeros_like(dg_ref)
            dbl_ref[...] = jnp.zeros_like(dbl_ref)

        g = g_ref[...]
        bl = bl_ref[...]
        dyc, dln, xh = ln_silu_bwd(dy_ref[...], yc_ref[...], g, bl)
        ndyc, _, _ = ln_silu_bwd(ndy_ref[...], nyc_ref[...], g, bl)
        dg_ref[...] += jnp.sum(dln * xh, axis=0, keepdims=True)
        dbl_ref[...] += jnp.sum(dln, axis=0, keepdims=True)
        dbd_ref[...] += jnp.sum(dyc, axis=0, keepdims=True)
        dext8[0, pl.ds(0, ts), :] = dyc
        dext8[0, pl.ds(ts, HALO), :] = ndyc * (i < nblk - 1).astype(F32)
        uext8[0, pl.ds(0, HALO), :] = pu_ref[...] * (i > 0).astype(F32)
        uext8[0, pl.ds(HALO, ts), :] = u_ref[...]
        _shifted_copies(dext8, ts + HALO)
        _shifted_copies(uext8, ts + HALO)

        def sub(k, carry):
            r0 = pl.multiple_of(k * rb, rb)
            rows = pl.ds(r0, rb)
            dyt = dext8[0, rows, :]
            du = jnp.zeros((rb, C), F32)
            for j in range(W):
                b, a = _tap(W - 1 - j)
                du = du + w_ref[pl.ds(j, 1), :] * dext8[b, pl.ds(r0 + a, rb), :]
                b, a = _tap(off + j)
                p = dyt * uext8[b, pl.ds(r0 + a, rb), :]
                part = p[0:SUB]
                for q in range(1, rb // SUB):
                    part = part + p[q * SUB:(q + 1) * SUB]
                dwacc[j] += part
            sb = _sigmoid(b_ref[rows, :])
            dab_ref[rows, 0:C] = (du * sb).astype(BF16)
            dab_ref[rows, C:2 * C] = (du * a_ref[rows, :] * sb * (1.0 - sb)).astype(BF16)
            return carry

        lax.fori_loop(0, ts // rb, sub, 0)

        @pl.when(i == nblk - 1)
        def _():
            for j in range(W):
                dw_ref[pl.ds(j, 1), :] = jnp.sum(dwacc[j], axis=0, keepdims=True)

    row = pl.BlockSpec((ts, C), lambda i: (i, 0))
    nxt = pl.BlockSpec((HALO, C), lambda i: (jnp.minimum((i + 1) * hb, S // HALO - 1), 0))
    prv = pl.BlockSpec((HALO, C), lambda i: (jnp.maximum(i * hb - 1, 0), 0))
    vec = pl.BlockSpec((1, C), lambda i: (0, 0))
    wsp = pl.BlockSpec((W, C), lambda i: (0, 0))
    vs = jax.ShapeDtypeStruct((1, C), F32)
    return pl.pallas_call(
        body, grid=(nblk,),
        in_specs=[row, nxt, row, nxt, row, prv, pl.BlockSpec((ts, C), lambda i: (i, ca)), pl.BlockSpec((ts, C), lambda i: (i, cb)), wsp, vec, vec],
        out_specs=[pl.BlockSpec((ts, 2 * C), lambda i: (i, 0)), wsp, vec, vec, vec],
        out_shape=[jax.ShapeDtypeStruct((S, 2 * C), BF16), jax.ShapeDtypeStruct((W, C), F32), vs, vs, vs],
        scratch_shapes=[pltpu.VMEM((SUB, ts + HALO, C), F32), pltpu.VMEM((SUB, ts + HALO, C), F32), pltpu.VMEM((W, SUB, C), F32)],
        compiler_params=_cp(("arbitrary",)), name=name,
    )(dycat, dycat, yc, yc, u, u, z, z, w_dw, g_ln, b_ln)


def _log_gate(zg):
    return (jnp.minimum(zg, 0.0) - jnp.log(1.0 + jnp.exp(-jnp.abs(zg)))) * (1.0 / GATE_TAU)


def _loggate(z, wgp, bgp, name):
    S = z.shape[0]
    N = wgp.shape[1]
    ts = _div(S, 512, 8)

    def body(g_ref, w_ref, b_ref, o_ref):
        o_ref[...] = _log_gate(_dot(g_ref[...], w_ref[...]) + b_ref[...])

    return pl.pallas_call(
        body, grid=(S // ts,),
        in_specs=[pl.BlockSpec((ts, LANE), lambda i: (i, ZC_G // LANE)), pl.BlockSpec((LANE, N), lambda i: (0, 0)), pl.BlockSpec((1, N), lambda i: (0, 0))],
        out_specs=pl.BlockSpec((ts, N), lambda i: (i, 0)), out_shape=jax.ShapeDtypeStruct((S, N), F32),
        compiler_params=_cp(("parallel",)), name=name,
    )(z, wgp, bgp)


def _loggate_bwd(dla, z, wgp, wgp_t, bgp, name):
    S = z.shape[0]
    N = wgp.shape[1]
    ts = _div(S, 512, 8)

    def body(dla_ref, g_ref, w_ref, wt_ref, b_ref, dg_ref, dw_ref, db_ref):
        @pl.when(pl.program_id(0) == 0)
        def _():
            dw_ref[...] = jnp.zeros_like(dw_ref)
            db_ref[...] = jnp.zeros_like(db_ref)

        glr = g_ref[...]
        zg = _dot(glr, w_ref[...]) + b_ref[...]
        dzg = dla_ref[...] * (1.0 / GATE_TAU) * (1.0 - _sigmoid(zg))
        dg_ref[...] = _dot(dzg, wt_ref[...]).astype(BF16)
        dw_ref[...] += _dot_tn(glr, dzg)
        db_ref[...] += jnp.sum(dzg, axis=0, keepdims=True)

    return pl.pallas_call(
        body, grid=(S // ts,),
        in_specs=[pl.BlockSpec((ts, N), lambda i: (i, 0)), pl.BlockSpec((ts, LANE), lambda i: (i, ZC_G // LANE)),
                  pl.BlockSpec((LANE, N), lambda i: (0, 0)), pl.BlockSpec((N, LANE), lambda i: (0, 0)), pl.BlockSpec((1, N), lambda i: (0, 0))],
        out_specs=[pl.BlockSpec((ts, LANE), lambda i: (i, 0)), pl.BlockSpec((LANE, N), lambda i: (0, 0)), pl.BlockSpec((1, N), lambda i: (0, 0))],
        out_shape=[jax.ShapeDtypeStruct((S, LANE), BF16), jax.ShapeDtypeStruct((LANE, N), F32), jax.ShapeDtypeStruct((1, N), F32)],
        compiler_params=_cp(("arbitrary",)), name=name,
    )(dla, z, wgp, wgp_t, bgp)


def _bdot(a, b, ca, cb):
    return lax.dot_general(a.astype(BF16), b.astype(BF16), (((ca,), (cb,)), ((0,), (0,))), preferred_element_type=F32)


def _bdot_exact(a, b):
    return lax.dot_general(a, b, (((2,), (1,)), ((0,), (0,))), preferred_element_type=F32, precision=lax.Precision.HIGHEST)


def _tiles(ref, cpb):
    return jnp.stack([ref[pl.ds(c * CHUNK, CHUNK), pl.ds(h * LANE, LANE)] for c in range(cpb) for h in range(HEADS)])


def _tri_masks(n):
    ri = lax.broadcasted_iota(jnp.int32, (n, CHUNK, CHUNK), 1)
    ci = lax.broadcasted_iota(jnp.int32, (n, CHUNK, CHUNK), 2)
    return ri >= ci, (ri >= ci).astype(F32), (ri <= ci).astype(F32)


def _chunk_fwd_terms(q, k, la, tril):
    bc = _bdot_exact(tril, la)
    bend = jnp.sum(la, axis=1, keepdims=True)
    eb = jnp.exp(bc)
    enb = jnp.exp(-bc)
    ee = jnp.exp(bend - bc)
    qs = q * (DK ** -0.5)
    return bend, eb, enb, ee, qs * eb, qs * enb, k * enb, k * eb, k * ee


def _gla_fwd(z, la, gn, name):
    S = z.shape[0]
    W = HEADS * LANE
    tb = _div(S, 512, CHUNK)
    cpb = tb // CHUNK

    def body(q_ref, k_ref, v_ref, r_ref, la_ref, gn_ref, o_ref, sp_ref, y_ref, st):
        @pl.when(pl.program_id(0) == 0)
        def _():
            st[...] = jnp.zeros_like(st)

        tri, tril, _ = _tri_masks(cpb * HEADS)
        q, k, v, rv, lav = (_tiles(r, cpb) for r in (q_ref, k_ref, v_ref, r_ref, la_ref))
        bend, _, _, _, qf, qb, kb, kf, ke = _chunk_fwd_terms(q, k, lav, tril)
        att = jnp.where(tri, _bdot(qf, kb, 2, 2), _bdot(qb, kf, 2, 2))
        o_intra = _bdot(att, v, 2, 1)
        u = _bdot(v, ke, 1, 1)
        gdec = jnp.exp(bend)
        s_prev = [None] * (cpb * HEADS)
        for h in range(HEADS):
            s = st[h]
            for c in range(cpb):
                b = c * HEADS + h
                s_prev[b] = s
                s = s * gdec[b] + u[b]
            st[h] = s
        s_prev = jnp.stack(s_prev)
        o = o_intra + _bdot(qf, s_prev, 2, 2)
        rms = lax.rsqrt(jnp.mean(o * o, axis=-1, keepdims=True) + EPS)
        gn = jnp.stack([gn_ref[pl.ds(h, 1), :] for _ in range(cpb) for h in range(HEADS)])
        y = (o * rms * gn * (rv * _sigmoid(rv))).astype(BF16)
        for c in range(cpb):
            for h in range(HEADS):
                b = c * HEADS + h
                rows, ln = pl.ds(c * CHUNK, CHUNK), pl.ds(h * LANE, LANE)
                o_ref[rows, ln] = o[b]
                y_ref[rows, ln] = y[b]
                sp_ref[h, c] = s_prev[b]

    zb = lambda base: pl.BlockSpec((tb, W), lambda i: (i, base // W))
    hb_ = pl.BlockSpec((tb, W), lambda i: (i, 0))
    return pl.pallas_call(
        body, grid=(S // tb,),
        in_specs=[zb(ZC_Q), zb(ZC_K), zb(ZC_V), zb(ZC_R), hb_, pl.BlockSpec((HEADS, DV), lambda i: (0, 0))],
        out_specs=[hb_, pl.BlockSpec((HEADS, cpb, DV, DKP), lambda i: (0, i, 0, 0)), hb_],
        out_shape=[jax.ShapeDtypeStruct((S, D_GLA), F32), jax.ShapeDtypeStruct((HEADS, S // CHUNK, DV, DKP), F32),
                   jax.ShapeDtypeStruct((S, D_GLA), BF16)],
        scratch_shapes=[pltpu.VMEM((HEADS, DV, DKP), F32)],
        compiler_params=_cp(("arbitrary",)), name=name,
    )(z, z, z, z, la, gn)


def _gla_bwd(dycat, z, la, o_raw, sprev, gn, name):
    S = z.shape[0]
    W = HEADS * LANE
    tb = _div(S, 512, CHUNK)
    cpb = tb // CHUNK
    nb = S // tb

    def body(q_ref, k_ref, v_ref, r_ref, la_ref, o_ref, sp_ref, dy_ref, gn_ref, dz_ref, dla_ref, dgn_ref, dst):
        @pl.when(pl.program_id(0) == 0)
        def _():
            dst[...] = jnp.zeros_like(dst)
            dgn_ref[...] = jnp.zeros_like(dgn_ref)

        nt = cpb * HEADS
        tri, tril, triu = _tri_masks(nt)
        q, k, v, rv, lav, o, dy = (_tiles(r, cpb) for r in (q_ref, k_ref, v_ref, r_ref, la_ref, o_ref, dy_ref))
        bend, eb, enb, ee, qf, qb, kb, kf, ke = _chunk_fwd_terms(q, k, lav, tril)
        att = jnp.where(tri, _bdot(qf, kb, 2, 2), _bdot(qb, kf, 2, 2))
        s_prev = jnp.stack([sp_ref[h, c] for c in range(cpb) for h in range(HEADS)])
        gdec = jnp.exp(bend)
        gn = jnp.stack([gn_ref[pl.ds(h, 1), :] for _ in range(cpb) for h in range(HEADS)])
        rms = lax.rsqrt(jnp.mean(o * o, axis=-1, keepdims=True) + EPS)
        oh = o * rms
        sg = _sigmoid(rv)
        sr = rv * sg
        d_r = (dy * oh * gn * (sg * (1.0 + rv * (1.0 - sg)))).astype(BF16)
        dgn = jnp.sum(dy * sr * oh, axis=1, keepdims=True)
        w = dy * sr * gn
        do = rms * (w - oh * jnp.mean(w * oh, axis=-1, keepdims=True))
        p = _bdot(do, qf, 1, 1)
        ds = [None] * nt
        for h in range(HEADS):
            s = dst[h]
            for c in reversed(range(cpb)):
                b = c * HEADS + h
                ds[b] = s
                s = s * gdec[b] + p[b]
            dst[h] = s
            dgn_ref[pl.ds(h, 1), :] += sum(dgn[c * HEADS + h] for c in range(cpb))
        ds = jnp.stack(ds)
        datt = _bdot(do, v, 2, 2)
        daf = jnp.where(tri, datt, 0.0)
        dab = jnp.where(tri, 0.0, datt)
        d_v = (_bdot(att, do, 1, 1) + _bdot(ke, ds, 2, 2)).astype(BF16)
        dke = _bdot(v, ds, 2, 1)
        dqf = _bdot(daf, kb, 2, 1) + _bdot(do, s_prev, 2, 1)
        dkb = _bdot(daf, qf, 1, 1)
        dqb = _bdot(dab, kf, 2, 1)
        dkf = _bdot(dab, qb, 1, 1)
        dg = jnp.sum(ds * s_prev, axis=1, keepdims=True)
        d_q = ((dqf * eb + dqb * enb) * (DK ** -0.5)).astype(BF16)
        d_k = (dkb * enb + dkf * eb + dke * ee).astype(BF16)
        dbc = dqf * qf - dkb * kb - dqb * qb + dkf * kf - dke * ke
        dbend = jnp.sum(dke * ke, axis=1, keepdims=True) + dg * gdec
        dla = _bdot_exact(triu, dbc) + dbend
        for c in range(cpb):
            for h in range(HEADS):
                b = c * HEADS + h
                rows = pl.ds(c * CHUNK, CHUNK)
                for base, val in ((ZC_Q, d_q), (ZC_K, d_k), (ZC_V, d_v), (ZC_R, d_r)):
                    dz_ref[rows, pl.ds(base + h * LANE, LANE)] = val[b]
                dla_ref[rows, pl.ds(h * LANE, LANE)] = dla[b]

    zb = lambda base: pl.BlockSpec((tb, W), lambda i: (nb - 1 - i, base // W))
    hb_ = pl.BlockSpec((tb, W), lambda i: (nb - 1 - i, 0))
    return pl.pallas_call(
        body, grid=(nb,),
        in_specs=[zb(ZC_Q), zb(ZC_K), zb(ZC_V), zb(ZC_R), hb_, hb_,
                  pl.BlockSpec((HEADS, cpb, DV, DKP), lambda i: (0, nb - 1 - i, 0, 0)),
                  pl.BlockSpec((tb, W), lambda i: (nb - 1 - i, 1)),
                  pl.BlockSpec((HEADS, DV), lambda i: (0, 0))],
        out_specs=[pl.BlockSpec((tb, Z_GLA), lambda i: (nb - 1 - i, 0)), hb_, pl.BlockSpec((HEADS, DV), lambda i: (0, 0))],
        out_shape=[jax.ShapeDtypeStruct((S, Z_GLA), BF16), jax.ShapeDtypeStruct((S, HEADS * DKP), F32), jax.ShapeDtypeStruct((HEADS, DV), F32)],
        scratch_shapes=[pltpu.VMEM((HEADS, DV, DKP), F32)],
        compiler_params=_cp(("arbitrary",)), name=name,
    )(z, z, z, z, la, o_raw, sprev, dycat, gn)


def _mod_proj(c_all, w3, layer, b, name):
    B, D = c_all.shape
    N = w3.shape[2]
    tn = _div(N, 1024, LANE)

    def body(c_ref, w_ref, b_ref, o_ref):
        cv = c_ref[...]
        o_ref[...] = _dot(cv * _sigmoid(cv), w_ref[...]) + b_ref[...]

    return pl.pallas_call(
        body, grid=(N // tn,),
        in_specs=[pl.BlockSpec((B, D), lambda j: (0, 0)), pl.BlockSpec((None, D, tn), lambda j: (layer, 0, j)), pl.BlockSpec((1, tn), lambda j: (0, j))],
        out_specs=pl.BlockSpec((B, tn), lambda j: (0, j)), out_shape=jax.ShapeDtypeStruct((B, N), F32),
        compiler_params=_cp(("parallel",)), name=name,
    )(c_all, w3, b)


def _mod_wgrad(c_t, dm, name):
    D, B = c_t.shape
    N = dm.shape[1]
    tn = _div(N, 1024, LANE)

    def body(c_ref, d_ref, o_ref):
        cv = c_ref[...]
        ca = cv * _sigmoid(cv)
        acc = ca[:, 0:1] * d_ref[pl.ds(0, 1), :]
        for b in range(1, B):
            acc = acc + ca[:, b:b + 1] * d_ref[pl.ds(b, 1), :]
        o_ref[...] = acc

    return pl.pallas_call(
        body, grid=(N // tn,),
        in_specs=[pl.BlockSpec((D, B), lambda j: (0, 0)), pl.BlockSpec((B, tn), lambda j: (0, j))],
        out_specs=pl.BlockSpec((D, tn), lambda j: (0, j)), out_shape=jax.ShapeDtypeStruct((D, N), F32),
        compiler_params=_cp(("parallel",)), name=name,
    )(c_t, dm)


def _rowsum(xs, name):
    n, N = xs.shape
    tn = _div(N, 8192, LANE)

    def body(x_ref, o_ref):
        acc = x_ref[pl.ds(0, 1), :]
        for r in range(1, n):
            acc = acc + x_ref[pl.ds(r, 1), :]
        o_ref[...] = acc

    return pl.pallas_call(
        body, grid=(N // tn,), in_specs=[pl.BlockSpec((n, tn), lambda j: (0, j))],
        out_specs=pl.BlockSpec((1, tn), lambda j: (0, j)), out_shape=jax.ShapeDtypeStruct((1, N), F32),
        compiler_params=_cp(("parallel",)), name=name,
    )(xs)


def _adamw(w, g, m, v, name, copy_grad=False):
    R, C = w.shape
    tr = _div(R, max(8, (1 << 18) // C), 8)

    def body(w_ref, g_ref, m_ref, v_ref, d_ref, nm_ref, nv_ref, *g_out):
        gv = g_ref[...]
        if copy_grad:
            g_out[0][...] = gv
        mn = ADAM_B1 * m_ref[...] + (1.0 - ADAM_B1) * gv
        vn = ADAM_B2 * v_ref[...] + (1.0 - ADAM_B2) * (gv * gv)
        m_hat = mn / (1.0 - ADAM_B1 ** ADAM_STEP)
        v_hat = vn / (1.0 - ADAM_B2 ** ADAM_STEP)
        d_ref[...] = -ADAM_LR * (m_hat / (jnp.sqrt(v_hat) + ADAM_EPS) + ADAM_WD * w_ref[...])
        nm_ref[...] = mn
        nv_ref[...] = vn

    blk = pl.BlockSpec((tr, C), lambda i: (i, 0))
    os_ = jax.ShapeDtypeStruct((R, C), F32)
    n_out = 4 if copy_grad else 3
    return pl.pallas_call(
        body, grid=(R // tr,), in_specs=[blk] * 4, out_specs=[blk] * n_out, out_shape=[os_] * n_out,
        compiler_params=_cp(("parallel",)), name=name,
    )(w, g, m, v)


def _place():
    return lax.axis_index("x"), lax.axis_index("y"), lax.axis_index("c")


def _other_chips(x, y):
    return [(1 - x, y), (x, 1 - y), (1 - x, 1 - y)]


def _half(c, rows):
    return pl.ds(c * (rows // 2), rows // 2)


_ANY = pl.BlockSpec(memory_space=pl.ANY)


def _ag_small(v, name):
    r, n = v.shape

    def body(v_ref, o_ref, send_sems, recv_sems):
        x, y, c = _place()
        me = 4 * x + 2 * y + c
        o_ref[pl.ds(me, 1)] = v_ref[...][None]
        peers = [(x ^ (k >> 2), y ^ ((k >> 1) & 1), c ^ (k & 1)) for k in range(1, 8)]
        copies = []
        for k, peer in enumerate(peers):
            cp = pltpu.make_async_remote_copy(
                src_ref=v_ref, dst_ref=o_ref.at[me], send_sem=send_sems.at[k], recv_sem=recv_sems.at[k],
                device_id=peer, device_id_type=MESH)
            cp.start()
            copies.append(cp)
        for cp in copies:
            cp.wait()

    return pl.pallas_call(
        body, out_shape=jax.ShapeDtypeStruct((8, r, n), v.dtype),
        in_specs=[pl.BlockSpec(memory_space=pltpu.VMEM)], out_specs=pl.BlockSpec(memory_space=pltpu.VMEM),
        scratch_shapes=[pltpu.SemaphoreType.DMA((7,)), pltpu.SemaphoreType.DMA((7,))],
        compiler_params=pltpu.CompilerParams(vmem_limit_bytes=VMEM_LIMIT), name=name,
    )(v)


def _rs_sibling_copies(bufs, send_sems, recv_sems):
    n = len(bufs) // 2
    x, y, c = _place()
    return [pltpu.make_async_remote_copy(
        src_ref=bufs[i].at[:, _half(1 - c, bufs[i].shape[1])], dst_ref=bufs[n + i], send_sem=send_sems.at[i], recv_sem=recv_sems.at[i],
        device_id=(x, y, 1 - c), device_id_type=MESH) for i in range(n)]


def _rs_sibling(gs, name, after=None):
    n = len(gs)

    def body(*refs):
        copies = _rs_sibling_copies(refs[:n] + refs[n + 1:2 * n + 1], refs[2 * n + 1], refs[2 * n + 2])
        for cp in copies:
            cp.start()
        for cp in copies:
            cp.wait()

    return pl.pallas_call(
        body, out_shape=[jax.ShapeDtypeStruct((N_CHIPS, g.shape[1] // 2, g.shape[2]), g.dtype) for g in gs],
        in_specs=[_ANY] * (n + 1), out_specs=[_ANY] * n,
        scratch_shapes=[pltpu.SemaphoreType.DMA((n,)), pltpu.SemaphoreType.DMA((n,))],
        compiler_params=pltpu.CompilerParams(has_side_effects=True), name=name,
    )(*gs, gs[0] if after is None else after)


def _rs_presum(g, sib, c_arr, name):
    ns, R, C = g.shape
    rh = R // 2
    tr = _div(rh, max(16, (1 << 19) // C), 16)
    nrb = rh // tr

    def body(c_ref, g_ref, s_ref, o_ref):
        o_ref[...] = (g_ref[...] + s_ref[...]).astype(BF16)

    return pl.pallas_call(
        body, out_shape=jax.ShapeDtypeStruct((ns, rh, C), BF16),
        grid_spec=pltpu.PrefetchScalarGridSpec(
            num_scalar_prefetch=1, grid=(ns, nrb),
            in_specs=[pl.BlockSpec((None, tr, C), lambda s, r, c_ref: (s, c_ref[0] * nrb + r, 0)),
                      pl.BlockSpec((None, tr, C), lambda s, r, c_ref: (s, r, 0))],
            out_specs=pl.BlockSpec((None, tr, C), lambda s, r, c_ref: (s, r, 0))),
        compiler_params=_cp(("parallel", "parallel")), name=name,
    )(c_arr, g, sib)


def _rs_sum(g, sib, recv, full, layer, sc_arr, name):
    ns, R, C = g.shape
    rh = R // 2
    tr = _div(rh, max(16, (1 << 18) // C), 16)
    nrb = rh // tr

    def body(sc_ref, g_ref, s_ref, r_ref, f_ref, o_ref):
        acc = g_ref[...] + s_ref[...]
        for j in range(3):
            acc = acc + r_ref[j].astype(F32)
        o_ref[...] = acc

    return pl.pallas_call(
        body, out_shape=jax.ShapeDtypeStruct(full.shape, F32),
        grid_spec=pltpu.PrefetchScalarGridSpec(
            num_scalar_prefetch=1, grid=(nrb,),
            in_specs=[pl.BlockSpec((None, tr, C), lambda r, sc: (sc[0], sc[1] * nrb + r, 0)),
                      pl.BlockSpec((None, tr, C), lambda r, sc: (sc[0], r, 0)),
                      pl.BlockSpec((3, tr, C), lambda r, sc: (0, r, 0)),
                      _ANY],
            out_specs=pl.BlockSpec((None, tr, C), lambda r, sc: (layer, sc[1] * nrb + r, 0))),
        input_output_aliases={4: 0},
        compiler_params=_cp(("parallel",)), name=name,
    )(sc_arr, g, sib, recv, full)


def _rs_share(fulls, layer, name):
    n = len(fulls)

    def body(*refs):
        src, out = refs[:n], refs[n:2 * n]
        send_sems, recv_sems = refs[2 * n:]
        x, y, c = _place()
        copies = []
        for i in range(n):
            rows = out[i].shape[1]
            cp = pltpu.make_async_remote_copy(
                src_ref=out[i].at[layer, _half(c, rows)], dst_ref=out[i].at[layer, _half(c, rows)],
                send_sem=send_sems.at[i], recv_sem=recv_sems.at[i], device_id=(x, y, 1 - c), device_id_type=MESH)
            cp.start()
            copies.append(cp)
        for cp in copies:
            cp.wait()

    return pl.pallas_call(
        body, out_shape=[jax.ShapeDtypeStruct(f.shape, f.dtype) for f in fulls],
        in_specs=[_ANY] * n, out_specs=[_ANY] * n, input_output_aliases={i: i for i in range(n)},
        scratch_shapes=[pltpu.SemaphoreType.DMA((n,)), pltpu.SemaphoreType.DMA((n,))],
        compiler_params=pltpu.CompilerParams(has_side_effects=True), name=name,
    )(*fulls)


_HBM = pl.BlockSpec(memory_space=pltpu.HBM)
_SEM = pl.BlockSpec(memory_space=pltpu.SEMAPHORE)
_EFFECT = pltpu.SideEffectType.DATAFLOW_SIDE_EFFECTING


def _in_hbm(a):
    return pltpu.with_memory_space_constraint(a, pltpu.HBM)


def _split_start(bufs, n_sem, copies_of, name):
    nb = len(bufs)

    def body(*refs):
        for cp in copies_of(refs[:nb], refs[nb], refs[nb + 1]):
            cp.start()
        refs[-1][...] = jnp.zeros_like(refs[-1])

    out = pl.pallas_call(
        body, name=name,
        out_shape=(pltpu.SemaphoreType.DMA((n_sem,)), pltpu.SemaphoreType.DMA((n_sem,)), *[pltpu.HBM(a.shape, a.dtype) for a in bufs],
                   jax.ShapeDtypeStruct((SUB, LANE), F32)),
        in_specs=[_HBM] * nb, out_specs=(_SEM, _SEM, *([_HBM] * nb), pl.BlockSpec(memory_space=pltpu.VMEM)),
        input_output_aliases={i: 2 + i for i in range(nb)},
        compiler_params=pltpu.CompilerParams(has_side_effects=_EFFECT),
    )(*[_in_hbm(a) for a in bufs])
    return out[0], out[1], list(out[2:2 + nb]), out[-1]


def _split_wait(send_sems, recv_sems, bufs, after, copies_of, name):
    nb = len(bufs)

    def body(*refs):
        for cp in copies_of(refs[:nb], refs[nb], refs[nb + 1]):
            cp.wait_send()
            cp.wait_recv()

    return list(pl.pallas_call(
        body, name=name, out_shape=[pltpu.HBM(a.shape, a.dtype) for a in bufs],
        in_specs=[_HBM] * nb + [_SEM, _SEM, _ANY], out_specs=[_HBM] * nb,
        input_output_aliases={i: i for i in range(nb)},
        compiler_params=pltpu.CompilerParams(has_side_effects=_EFFECT),
    )(*bufs, send_sems, recv_sems, after))


def _ag_half_copies(land, send_sems, recv_sems, landing_of_mine, whole=False):
    x, y, c = _place()
    cps = []
    for j, (cx, cy) in enumerate(_other_chips(x, y)):
        for i in range(len(land)):
            rows = pl.ds(0, land[i].shape[1]) if whole else _half(c, land[i].shape[1])
            s = 2 * x + y if landing_of_mine else 2 * cx + cy
            cps.append(pltpu.make_async_remote_copy(
                src_ref=land[i].at[2 * x + y, rows], dst_ref=land[i].at[s, rows], send_sem=send_sems.at[3 * i + j], recv_sem=recv_sems.at[3 * i + j],
                device_id=(cx, cy, c), device_id_type=MESH))
    return cps


def _ag_starts(land, send_sems, recv_sems):
    return _ag_half_copies(land, send_sems, recv_sems, True)


def _ag_waits(land, send_sems, recv_sems):
    return _ag_half_copies(land, send_sems, recv_sems, False)


def _ag_starts_whole(land, send_sems, recv_sems):
    return _ag_half_copies(land, send_sems, recv_sems, True, whole=True)


def _ag_waits_whole(land, send_sems, recv_sems):
    return _ag_half_copies(land, send_sems, recv_sems, False, whole=True)


def _ag_finish(lands, name):
    n = len(lands)

    def body(*refs):
        land = refs[n:2 * n]
        send_sems, recv_sems = refs[2 * n:]
        x, y, c = _place()
        sibling = (x, y, 1 - c)

        def copy(k, i, s, h):
            blk = land[i].at[s, _half(h, land[i].shape[1])]
            return pltpu.make_async_remote_copy(
                src_ref=blk, dst_ref=blk, send_sem=send_sems.at[k], recv_sem=recv_sems.at[k], device_id=sibling, device_id_type=MESH)

        chips = _other_chips(x, y)
        passed = [copy(3 * i + j, i, 2 * cx + cy, c) for j, (cx, cy) in enumerate(chips) for i in range(n)]
        for cp in passed:
            cp.start()
        for j, (cx, cy) in enumerate(chips):
            for i in range(n):
                copy(3 * i + j, i, 2 * cx + cy, 1 - c).wait_recv()
        for cp in passed:
            cp.wait_send()

    return pl.pallas_call(
        body, out_shape=[jax.ShapeDtypeStruct(a.shape, a.dtype) for a in lands],
        in_specs=[_ANY] * n, out_specs=[_ANY] * n, input_output_aliases={i: i for i in range(n)},
        scratch_shapes=[pltpu.SemaphoreType.DMA((3 * n,)), pltpu.SemaphoreType.DMA((3 * n,))],
        compiler_params=pltpu.CompilerParams(has_side_effects=True), name=name,
    )(*lands)


def _rs_chip_copies(bufs, send_sems, recv_sems):
    n = len(bufs) // 2
    x, y, c = _place()
    return [pltpu.make_async_remote_copy(
        src_ref=bufs[i].at[2 * cx + cy], dst_ref=bufs[n + i].at[j], send_sem=send_sems.at[3 * i + j], recv_sem=recv_sems.at[3 * i + j],
        device_id=(cx, cy, c), device_id_type=MESH) for j, (cx, cy) in enumerate(_other_chips(x, y)) for i in range(n)]


def _pad_heads(w):
    lead = w.shape[:-1]
    w4 = w.reshape(*lead, HEADS, DK)
    w4 = jnp.pad(w4, [(0, 0)] * len(lead) + [(0, 0), (0, DKP - DK)])
    return w4.reshape(*lead, HEADS * DKP)


def _unpad_heads(w):
    lead = w.shape[:-1]
    return w.reshape(*lead, HEADS, DKP)[..., :DK].reshape(*lead, HEADS * DK)


def _mix_column_runs():
    o = 2 * D_CONV
    runs = [(0, D_CONV, ZC_A), (D_CONV, D_CONV, ZC_B)]
    runs += [(o + h * DK, DK, ZC_Q + h * DKP) for h in range(HEADS)]
    runs += [(o + HEADS * DK + h * DK, DK, ZC_K + h * DKP) for h in range(HEADS)]
    o += 2 * HEADS * DK
    return runs + [(o, D_GLA, ZC_V), (o + D_GLA, D_GLA, ZC_R), (o + 2 * D_GLA, GATE_RANK, ZC_G)]


def _mix_weight(win4, n_cols):
    D = win4.shape[1]
    pieces, z = [], 0
    for c0, width, z0 in sorted(_mix_column_runs(), key=lambda r: r[2]):
        if z0 > z:
            pieces.append(jnp.zeros((D, z0 - z), win4.dtype))
        c = c0
        while c < c0 + width:
            s, lo = divmod(c, n_cols)
            hi = min(n_cols, lo + c0 + width - c)
            pieces.append(win4[s, :, lo:hi])
            c += hi - lo
        z = z0 + width
    pieces.append(jnp.zeros((D, Z_COLS - z), win4.dtype))
    return jnp.concatenate(pieces, axis=1)


def _mix_weight_grad(dgla, dab, dglr, n_cols, n_pad):
    D = dab.shape[0]

    def zcols(z0, z1):
        for arr, base in ((dgla, 0), (dab, ZC_A), (dglr, ZC_G)):
            if base <= z0 and z1 <= base + arr.shape[1]:
                return arr[:, z0 - base:z1 - base]

    shards = []
    for s in range(N_CHIPS):
        pieces = []
        for c0, width, z0 in _mix_column_runs():
            lo, hi = max(c0, s * n_cols), min(c0 + width, (s + 1) * n_cols)
            if lo < hi:
                pieces.append(zcols(z0 + lo - c0, z0 + hi - c0))
        pieces.append(jnp.zeros((D, n_pad - n_cols), dab.dtype))
        shards.append(jnp.concatenate(pieces, axis=1))
    return jnp.stack(shards)


_ARG_NAMES = ['x', 'c', 'w_ada', 'b_ada', 'g_norm_ffn1', 'w_ffn1_in', 'w_ffn1_out', 'g_norm_mix', 'w_in', 'w_dw', 'b_dw', 'g_conv_ln', 'b_conv_ln', 'w_gate_up', 'b_gate', 'g_gla_norm', 'w_out', 'g_norm_ffn2', 'w_ffn2_in', 'w_ffn2_out', 'g_norm_final', 'w_ada_final', 'b_ada_final']
_WEIGHTS = _ARG_NAMES[2:]
_BIG = ('w_ffn1_in', 'w_ffn1_out', 'w_in', 'w_out', 'w_ffn2_in', 'w_ffn2_out')
_SMALL = ('g_norm_ffn1', 'g_norm_mix', 'w_dw', 'b_dw', 'g_conv_ln', 'b_conv_ln', 'w_gate_up', 'b_gate', 'g_gla_norm', 'g_norm_ffn2', 'g_norm_final')


def _ffn_fwd(x, h, gv, w4, wo, nxt, tag):
    if isinstance(h, tuple):
        z, act, h = _ffn_up(x, w4, f"ffn_up_{tag}", norm=h)
    else:
        z, act = _ffn_up(h, w4, f"ffn_up_{tag}")
    y, xn, *hn = _ffn_down(act, wo, x, gv, nxt, f"ffn_down_{tag}")
    return xn, (hn[0] if hn else None), y, (x, h, z, act)


def _ffn_bwd(dxn, dy, saved, g, scale, prev, w4, wo, tag):
    x, h, z, act = saved
    ns = w4.shape[0]
    dwo = _mm_tn(act, dy, f"dw_out_{tag}")
    dz, dx, dsh, dsc, dg, *pv = _ffn_bwd_core(dy, wo, z, w4, x, dxn, g, scale, prev, f"ffn_bwd_{tag}")
    dwi = _dw_ffn_in(h, dz, ns, f"dw_in_{tag}")
    return dx, pv, dict(dshift=dsh, dscale=dsc, dg=dg, dw_in=dwi, dw_out=dwo.reshape(N_CHIPS, -1, dwo.shape[1]))


def _mix_fwd(x, h, gv, wmix, w_dw, b_dw, g_ln, b_ln, wgp, bgp, gn, wout, nxt, tag):
    z = _mm([(h, wmix, 0)], F32, f"mix_in_{tag}")
    u, yc, yconv = _conv_fwd(z, w_dw, b_dw, g_ln, b_ln, f"conv_fwd_{tag}")
    la = _loggate(z, wgp, bgp, f"loggate_{tag}")
    o_raw, sprev, ygla = _gla_fwd(z, la, gn, f"gla_fwd_{tag}")
    y, xn, *hn = _mix_out(yconv, ygla, wout, x, gv, nxt, f"mix_out_{tag}")
    return xn, (hn[0] if hn else None), y, (x, h, z, u, yc, la, o_raw, sprev, yconv, ygla)


def _mix_bwd(dxn, dy, saved, g, scale, prev, wmix, w_dw, g_ln, b_ln, wgp, bgp, gn, wout, n_cols, n_pad, tag):
    x, h, z, u, yc, la, o_raw, sprev, yconv, ygla = saved
    dycat = _mm([(dy, wout, 0)], F32, f"mix_dycat_{tag}", nt=True)
    dwout = _mm_tn_two(yconv, ygla, dy, f"dw_mixout_{tag}")
    dab, dwdw, dbdw, dgln, dbln = _conv_bwd(dycat, z, u, yc, w_dw, g_ln, b_ln, f"conv_bwd_{tag}")
    dgla, dla, dgn = _gla_bwd(dycat, z, la, o_raw, sprev, gn, f"gla_bwd_{tag}")
    dglr, dwgp, dbgp = _loggate_bwd(dla, z, wgp, wgp.T, bgp, f"loggate_bwd_{tag}")
    dx, dsh, dsc, dg, *pv = _dh_normmod_bwd(
        [(dgla, wmix, 0, 0, Z_GLA), (dab, wmix, 0, ZC_A // (2 * D_CONV), 2 * D_CONV), (dglr, wmix, 0, ZC_G // LANE, LANE)],
        x, dxn, g, scale, prev, f"mix_dh_{tag}")
    dwin = _mix_weight_grad(_mm_tn(h, dgla, f"dw_mixin_gla_{tag}"), _mm_tn(h, dab, f"dw_mixin_conv_{tag}"), _mm_tn(h, dglr, f"dw_mixin_gate_{tag}"),
                            n_cols, n_pad)
    grads = dict(dshift=dsh, dscale=dsc, dg=dg, dw_in=dwin, dw_out=dwout.reshape(N_CHIPS, -1, dwout.shape[1]), dw_dw=dwdw, db_dw=dbdw,
                 dg_ln=dgln, db_ln=dbln, dw_gate=_unpad_heads(dwgp[:GATE_RANK]), db_gate=_unpad_heads(dbgp)[0], dgn=dgn)
    return dx, pv, grads


def kernel(x, c, w_ada, b_ada, g_norm_ffn1, w_ffn1_in, w_ffn1_out, g_norm_mix, w_in, w_dw, b_dw, g_conv_ln, b_conv_ln, w_gate_up, b_gate, g_gla_norm, w_out, g_norm_ffn2, w_ffn2_in, w_ffn2_out, g_norm_final, w_ada_final, b_ada_final, loss_target, m_w_ada, m_b_ada, m_g_norm_ffn1, m_w_ffn1_in, m_w_ffn1_out, m_g_norm_mix, m_w_in, m_w_dw, m_b_dw, m_g_conv_ln, m_b_conv_ln, m_w_gate_up, m_b_gate, m_g_gla_norm, m_w_out, m_g_norm_ffn2, m_w_ffn2_in, m_w_ffn2_out, m_g_norm_final, m_w_ada_final, m_b_ada_final, v_w_ada, v_b_ada, v_g_norm_ffn1, v_w_ffn1_in, v_w_ffn1_out, v_g_norm_mix, v_w_in, v_w_dw, v_b_dw, v_g_conv_ln, v_b_conv_ln, v_w_gate_up, v_b_gate, v_g_gla_norm, v_w_out, v_g_norm_ffn2, v_w_ffn2_in, v_w_ffn2_out, v_g_norm_final, v_w_ada_final, v_b_ada_final):
    given = dict(locals())
    W = {n: given[n] for n in _WEIGHTS}
    M1 = {n: given["m_" + n] for n in _WEIGHTS}
    M2 = {n: given["v_" + n] for n in _WEIGHTS}
    xs = x[0]
    tgt = loss_target[0]
    S, D = xs.shape
    L = w_ada.shape[0]
    xi, yi, ci = _place()
    s_me = 2 * xi + yi
    b_me = 4 * xi + 2 * yi + ci
    nsh = w_ada.shape[2]
    nfin = w_ada_final.shape[1]
    n_cols = w_in.shape[2]
    n_pad = -(-n_cols // LANE) * LANE

    def lands_of(l):
        shards = [W[n][l].astype(BF16) for n in _BIG]
        shards[2] = jnp.pad(shards[2], ((0, 0), (0, n_pad - n_cols)))
        return [lax.dynamic_update_index_in_dim(lax.empty((N_CHIPS,) + s.shape, BF16), s, s_me, 0) for s in shards]

    lands = {l: lands_of(l) for l in range(L)}
    ag_groups = [dict(l=0, items=[0, 1], need=0), dict(l=0, items=[2, 3, 4, 5], need=1)]
    ag_groups += [dict(l=l, items=list(range(len(_BIG))), need=3 * l, whole=True) for l in range(1, L)]

    def ag_start(grp):
        bufs = [lands[grp["l"]][i] for i in grp["items"]]
        starts = _ag_starts_whole if grp.get("whole") else _ag_starts
        return _split_start(bufs, 3 * len(bufs), starts, f"ag_start_l{grp['l']}_{grp['items'][0]}")

    pend = ag_start(ag_groups[0])
    tok = pend[3][0, 0]

    c_all = _ag_small(c.reshape(8, D // 8) + tok, "ag_c").reshape(8, D)
    tok = None
    parts = [_mod_proj(c_all, w_ada, l, lax.dynamic_slice(b_ada, (l, s_me * nsh), (1, nsh)), f"mod_proj_{l}") for l in range(L)]
    parts.append(_mod_proj(c_all, w_ada_final[None], 0, lax.dynamic_slice(b_ada_final, (s_me * nfin,), (nfin,))[None], "mod_proj_final"))
    mod_all = _ag_small(jnp.concatenate(parts, axis=1), "ag_mod")
    mine = [lax.dynamic_index_in_dim(lax.dynamic_index_in_dim(mod_all, 2 * s + ci, 0, False), b_me, 0, False) for s in range(N_CHIPS)]
    mods = [jnp.concatenate([mine[s][l * nsh:(l + 1) * nsh] for s in range(N_CHIPS)]).reshape(N_MOD, 1, D) for l in range(L)]
    fmod = jnp.concatenate([mine[s][L * nsh:] for s in range(N_CHIPS)]).reshape(2, 1, D)

    tiny = jnp.concatenate([w_dw.reshape(-1), w_gate_up.reshape(-1)])
    tiny_all = _ag_small(jnp.pad(tiny, (0, (-tiny.shape[0]) % (8 * LANE))).reshape(8, -1), "ag_tiny").reshape(8, -1)
    n_dw = w_dw.size
    dw_parts = [lax.dynamic_index_in_dim(tiny_all, 2 * s + ci, 0, False) for s in range(N_CHIPS)]
    w_dw_full = jnp.concatenate([p[:n_dw].reshape(w_dw.shape) for p in dw_parts], axis=2)
    w_gu_full = jnp.concatenate([p[n_dw:n_dw + w_gate_up.size].reshape(w_gate_up.shape) for p in dw_parts], axis=2)

    def layer_weights(l, lands):
        wi1, wo1, win4, wout4, wi2, wo2 = lands
        return dict(
            wi1=wi1, wo1=wo1.reshape(-1, D), wi2=wi2, wo2=wo2.reshape(-1, D), wout=wout4.reshape(-1, D), wmix=_mix_weight(win4, n_cols),
            wgp=jnp.pad(_pad_heads(w_gu_full[l]), ((0, LANE - GATE_RANK), (0, 0))).astype(BF16), bgp=_pad_heads(b_gate[l])[None])

    gnorm = (g_norm_ffn1, g_norm_mix, g_norm_ffn2)
    subs = [dict(l=l, j=j, tag=f"{('ffn1', 'mix', 'ffn2')[j]}_l{l}", g=gnorm[j][l][None], shift=mods[l][3 * j], scale=mods[l][3 * j + 1],
                 gv=mods[l][3 * j + 2] * (1.0 if j == 1 else 0.5)) for l in range(L) for j in range(3)]
    gi = 0
    xcur = xs
    h = None
    for k, sb in enumerate(subs):
        l, j = sb["l"], sb["j"]
        if pend is not None and ag_groups[gi]["need"] == k:
            grp = ag_groups[gi]
            nm = f"l{grp['l']}_{grp['items'][0]}"
            after = xcur if k > 0 else sb["shift"]
            if grp.get("whole"):
                done = _split_wait(pend[0], pend[1], pend[2], after, _ag_waits_whole, f"ag_wait_{nm}")
            else:
                done = _ag_finish(_split_wait(pend[0], pend[1], pend[2], after, _ag_waits, f"ag_wait_{nm}"), f"ag_finish_{nm}")
            for i, a in zip(grp["items"], done):
                lands[grp["l"]][i] = a
            gi += 1
            pend = ag_start(ag_groups[gi]) if gi < len(ag_groups) else None
            tok = pend[3][0, 0] if pend is not None else None
        if h is None:
            h = (sb["g"] if tok is None else sb["g"] + tok, sb["shift"], sb["scale"])
            tok = None
        d = layer_weights(l, lands[l])
        nxt = (subs[k + 1]["g"], subs[k + 1]["shift"], subs[k + 1]["scale"]) if k + 1 < len(subs) else None
        gv = sb["gv"] if tok is None else sb["gv"] + tok
        tok = None
        if j == 1:
            xcur, h, sb["y"], sb["saved"] = _mix_fwd(xcur, h, gv, d["wmix"], w_dw_full[l], b_dw[l][None], g_conv_ln[l][None],
                                                     b_conv_ln[l][None], d["wgp"], d["bgp"], g_gla_norm[l], d["wout"], nxt, sb["tag"])
        else:
            w4, wo = (d["wi1"], d["wo1"]) if j == 0 else (d["wi2"], d["wo2"])
            xcur, h, sb["y"], sb["saved"] = _ffn_fwd(xcur, h, gv, w4, wo, nxt, sb["tag"])
    lw = [layer_weights(l, lands[l]) for l in range(L)]

    c_arr = jnp.stack([ci]).astype(jnp.int32)
    sc_arr = jnp.stack([s_me, ci]).astype(jnp.int32)
    fulls = [lax.empty((L,) + ((W[n].shape[1], n_pad) if n == 'w_in' else W[n].shape[1:]), F32) for n in _BIG]

    def rs_begin(gs, items, l, after=None):
        nm = f"l{l}_{items[0]}"
        sibs = _rs_sibling(gs, f"rs_sibling_{nm}", after)
        return sibs, [_rs_presum(g, sb_, c_arr, f"rs_presum_{i}_l{l}") for i, g, sb_ in zip(items, gs, sibs)]

    def rs_end(gs, sibs, recvs, items, l):
        summed = [_rs_sum(g, sb_, rv, fulls[i], l, sc_arr, f"rs_sum_{i}_l{l}") for i, g, sb_, rv in zip(items, gs, sibs, recvs)]
        for i, f in zip(items, _rs_share(summed, l, f"rs_share_l{l}_{items[0]}")):
            fulls[i] = f

    def rs_start(gs, items, l, sibs=None, after=None):
        if sibs is None:
            sibs, ps = rs_begin(gs, items, l, after)
        else:
            ps = [_rs_presum(g, sb_, c_arr, f"rs_presum_{i}_l{l}") for i, g, sb_ in zip(items, gs, sibs)]
        pend = _split_start(ps + [lax.empty((3,) + p.shape[1:], BF16) for p in ps], 3 * len(ps), _rs_chip_copies, f"rs_start_l{l}_{items[0]}")
        return dict(gs=gs, sibs=sibs, pend=pend, items=items, l=l)

    def sib_start(gs, items, l):
        lands_ = [lax.empty((N_CHIPS, g.shape[1] // 2, g.shape[2]), F32) for g in gs]
        return dict(pend=_split_start(gs + lands_, len(gs), _rs_sibling_copies, f"rs_sibling_start_l{l}"), items=items, l=l, n=len(gs))

    def sib_finish(sp, after):
        bufs = _split_wait(sp["pend"][0], sp["pend"][1], sp["pend"][2], after, _rs_sibling_copies, f"rs_sibling_wait_l{sp['l']}")
        return rs_start(bufs[:sp["n"]], sp["items"], sp["l"], sibs=bufs[sp["n"]:])

    def rs_finish(fl, after):
        pend, n = fl["pend"], len(fl["gs"])
        bufs = _split_wait(pend[0], pend[1], pend[2], after, _rs_chip_copies, f"rs_wait_l{fl['l']}_{fl['items'][0]}")
        rs_end(fl["gs"], fl["sibs"], bufs[n:], fl["items"], fl["l"])

    sq, dx, dfsh, dfsc, dgfin, dy, dgv = _loss_head(xcur, g_norm_final[None], fmod[0], fmod[1], tgt, (subs[-1]["y"], subs[-1]["gv"]))
    loss_part = 0.5 / D * jnp.sum(sq)
    G = {n: [None] * L for n in _SMALL}
    dmods = [None] * L
    in_flight = None
    sib_flight = None
    tok = None
    for l in reversed(range(L)):
        gr = [None] * 3
        for j in reversed(range(3)):
            k = 3 * l + j
            sb, d = subs[k], lw[l]
            prev = (subs[k - 1]["y"], subs[k - 1]["gv"]) if k > 0 else None
            g_vec = sb["g"] if tok is None else sb["g"] + tok
            tok = None
            if j == 1:
                dx, pv, gr[j] = _mix_bwd(dx, dy, sb["saved"], g_vec, sb["scale"], prev, d["wmix"], w_dw_full[l], g_conv_ln[l][None], b_conv_ln[l][None],
                                         d["wgp"], d["bgp"], g_gla_norm[l], d["wout"], n_cols, n_pad, sb["tag"])
            else:
                w4, wo = (d["wi1"], d["wo1"]) if j == 0 else (d["wi2"], d["wo2"])
                dx, pv, gr[j] = _ffn_bwd(dx, dy, sb["saved"], g_vec, sb["scale"], prev, w4, wo, sb["tag"])
            gr[j]["dgv"] = dgv
            dy, dgv = pv if pv else (None, None)
            if j == 2 and sib_flight is not None:
                in_flight = sib_finish(sib_flight, dx)
                sib_flight = None
                tok = in_flight["pend"][3][0, 0]
            if j == 1 and in_flight is not None:
                rs_finish(in_flight, dx)
                in_flight = None
            if j == 1 and l == 0:
                in_flight = rs_start([gr[1]["dw_in"], gr[1]["dw_out"], gr[2]["dw_in"], gr[2]["dw_out"]], [2, 3, 4, 5], l)
                tok = in_flight["pend"][3][0, 0]
        g1, g2, g3 = gr
        if l > 0:
            sib_flight = sib_start([g1["dw_in"], g1["dw_out"], g2["dw_in"], g2["dw_out"], g3["dw_in"], g3["dw_out"]], list(range(len(_BIG))), l)
            tok = sib_flight["pend"][3][0, 0]
        else:
            rs_finish(in_flight, dx)
            in_flight = None
            last_grads = [g1["dw_in"], g1["dw_out"]]
        dmods[l] = jnp.concatenate([g1["dshift"], g1["dscale"], 0.5 * g1["dgv"], g2["dshift"], g2["dscale"], g2["dgv"],
                                    g3["dshift"], g3["dscale"], 0.5 * g3["dgv"]], axis=1)[0]
        G["g_norm_ffn1"][l], G["g_norm_ffn2"][l], G["g_norm_mix"][l] = g1["dg"][0], g3["dg"][0], g2["dg"][0]
        G["w_dw"][l], G["b_dw"][l], G["g_conv_ln"][l], G["b_conv_ln"][l] = g2["dw_dw"], g2["db_dw"][0], g2["dg_ln"][0], g2["db_ln"][0]
        G["w_gate_up"][l], G["b_gate"][l], G["g_gla_norm"][l] = g2["dw_gate"], g2["db_gate"], g2["dgn"]
    grad_x = dx[None]
    gsm = {}

    small = [jnp.stack(G[n]).reshape(-1) for n in _SMALL if n != 'g_norm_final'] + [dgfin[0]]
    dmod_vec = jnp.concatenate(dmods + [dfsh[0], dfsc[0]])
    n_mod_vec = dmod_vec.shape[0]
    vec = jnp.concatenate([dmod_vec] + small + [loss_part[None]])
    n_vec = vec.shape[0]
    vec = jnp.pad(vec, (0, (-n_vec) % (8 * LANE)))
    vec_all = _ag_small(vec.reshape(8, -1), "ag_small_grads").reshape(8, -1)
    last = rs_start(last_grads, [0, 1], 0, after=vec_all)
    vec_sum = _rowsum(vec_all + last["pend"][3][0, 0], "sum_small_grads")[0]
    loss = vec_sum[n_vec - 1]
    off = n_mod_vec
    for n in _SMALL:
        shp = {'w_dw': w_dw_full.shape, 'w_gate_up': w_gu_full.shape}.get(n, W[n].shape)
        cnt = 1
        for dd in shp:
            cnt *= dd
        gsm[n] = vec_sum[off:off + cnt].reshape(shp)
        off += cnt
    gsm['w_dw'] = lax.dynamic_slice_in_dim(gsm['w_dw'], s_me * w_dw.shape[2], w_dw.shape[2], 2)
    gsm['w_gate_up'] = lax.dynamic_slice_in_dim(gsm['w_gate_up'], s_me * w_gate_up.shape[2], w_gate_up.shape[2], 2)
    dmod_sum = vec_sum[:n_mod_vec]
    gsm['b_ada'] = dmod_sum[:L * N_MOD * D].reshape(L, N_MOD * D)
    gsm['b_ada_final'] = dmod_sum[L * N_MOD * D:]
    c_t = c_all.T
    dmod_rows = vec_all[:, :n_mod_vec]
    gsm['w_ada'] = jnp.stack([
        _mod_wgrad(c_t, lax.dynamic_slice_in_dim(dmod_rows, l * N_MOD * D + s_me * nsh, nsh, 1), f"dw_ada_{l}") for l in range(L)])
    gsm['w_ada_final'] = _mod_wgrad(c_t, lax.dynamic_slice_in_dim(dmod_rows, L * N_MOD * D + s_me * nfin, nfin, 1), "dw_ada_final")
    gsm.update({n: (f[:, :, :n_cols] if n == 'w_in' else f) for n, f in zip(_BIG[2:], fulls[2:])})

    outs = {}
    small_names = [n for n in _WEIGHTS if W[n].size < 65536]
    for n in [m for m in _WEIGHTS if m not in _BIG[:2]] + list(_BIG[:2]):
        if n in small_names:
            continue
        if n == _BIG[0]:
            rs_finish(last, outs['w_ada_final'][0])
            gsm.update(dict(zip(_BIG[:2], fulls[:2])))
        shp = W[n].shape
        v2 = lambda a: a.reshape(-1, shp[-1])
        from_rs = n in _BIG and n != 'w_in'
        d_, m_, v_, *g_ = _adamw(v2(W[n]), v2(gsm[n]), v2(M1[n]), v2(M2[n]), f"adamw_{n}", copy_grad=from_rs)
        outs[n] = (d_.reshape(shp), m_.reshape(shp), v_.reshape(shp))
        if from_rs:
            gsm[n] = g_[0].reshape(shp)
    flat = lambda dct: jnp.concatenate([dct[n].reshape(-1) for n in small_names])
    n_small = sum(W[n].size for n in small_names)
    v2 = lambda a: jnp.pad(a, (0, (-n_small) % (8 * LANE))).reshape(-1, LANE)
    d_, m_, v_ = _adamw(v2(flat(W)), v2(flat(gsm)), v2(flat(M1)), v2(flat(M2)), "adamw_small")

    def unflat(a):
        res, o = {}, 0
        a = a.reshape(-1)
        for n in small_names:
            res[n] = a[o:o + W[n].size].reshape(W[n].shape)
            o += W[n].size
        return res

    for n, dd, mm, vv in zip(small_names, unflat(d_).values(), unflat(m_).values(), unflat(v_).values()):
        outs[n] = (dd, mm, vv)

    return (loss, grad_x, *[gsm[n] for n in _WEIGHTS], *[outs[n][0] for n in _WEIGHTS], *[outs[n][1] for n in _WEIGHTS], *[outs[n][2] for n in _WEIGHTS])
```
